```python
import jax
import jax.numpy as jnp
from jax import lax
import numpy as np

D_MODEL = 1024
BATCH = 8
SEQ = 16384
DEPTH = 1

GRID_W = 64
CTX_LEN = 256
HEAD_DIM = 64
A_HEADS = 8
A_KV_HEADS = 2
A_GROUPS = A_HEADS // A_KV_HEADS
B_HEADS = 8
B_KV_HEADS = 2
B_GROUPS = B_HEADS // B_KV_HEADS
WINDOW = 128
BLOCK = 128
ROPE_THETA = 10000.0
D_FF = 2816
CONV_WIDTH = 3
LN_EPS = 1e-5
QK_EPS = 1e-6
N_MOD = 6
DEEPNORM_ALPHA = (2.0 * DEPTH) ** 0.25
DEEPNORM_BETA = (8.0 * DEPTH) ** -0.25

OFF_QA = 0
OFF_KA = OFF_QA + A_HEADS * HEAD_DIM
OFF_VA = OFF_KA + A_KV_HEADS * HEAD_DIM
OFF_QB = OFF_VA + A_KV_HEADS * HEAD_DIM
OFF_KB = OFF_QB + B_HEADS * HEAD_DIM
OFF_VB = OFF_KB + B_KV_HEADS * HEAD_DIM
OFF_GA = OFF_VB + B_KV_HEADS * HEAD_DIM
OFF_GB = OFF_GA + D_MODEL
IN_COLS = OFF_GB + D_MODEL

kernel_name = "hybrid_window_axial_gqa_convffn_dit_layer"


def _layer_norm(x, g, b):
    xf = x.astype(jnp.float32)
    mu = jnp.mean(xf, axis=-1, keepdims=True)
    var = jnp.mean(jnp.square(xf - mu), axis=-1, keepdims=True)
    return ((xf - mu) * lax.rsqrt(var + LN_EPS) * g + b).astype(x.dtype)


def _qk_rms(t, g):
    tf = t.astype(jnp.float32)
    return (tf * lax.rsqrt(jnp.mean(tf * tf, axis=-1, keepdims=True) + QK_EPS) * g).astype(t.dtype)


def _axial_rope_tables(n_tok, dtype):
    pos = jnp.arange(n_tok, dtype=jnp.int32)
    rows = (pos // GRID_W).astype(jnp.float32)
    cols = (pos % GRID_W).astype(jnp.float32)
    n_freq = HEAD_DIM // 4
    inv_freq = ROPE_THETA ** (-jnp.arange(n_freq, dtype=jnp.float32) / n_freq)
    ang_r = rows[:, None, None] * inv_freq
    ang_c = cols[:, None, None] * inv_freq
    return (jnp.cos(ang_r).astype(dtype), jnp.sin(ang_r).astype(dtype),
            jnp.cos(ang_c).astype(dtype), jnp.sin(ang_c).astype(dtype))


def _rotate_half(t, cos, sin):
    t1, t2 = jnp.split(t, 2, axis=-1)
    return jnp.concatenate([t1 * cos - t2 * sin, t2 * cos + t1 * sin], axis=-1)


def _rope_2d(t, tables):
    cos_r, sin_r, cos_c, sin_c = tables
    t_row, t_col = jnp.split(t, 2, axis=-1)
    return jnp.concatenate([_rotate_half(t_row, cos_r, sin_r), _rotate_half(t_col, cos_c, sin_c)], axis=-1)


def _gqa_scores(q, k):
    return jnp.einsum("bqhgd,bkhd->bhgqk", q, k).astype(jnp.float32) * (HEAD_DIM ** -0.5)


def _gqa_values(p, v):
    return jnp.einsum("bhgqk,bkhd->bqhgd", p.astype(v.dtype), v)


def _sink_softmax(scores, sink):
    sink = sink.astype(jnp.float32)[:, :, None, None]
    m = jnp.maximum(jnp.max(scores, axis=-1, keepdims=True), sink)
    e = jnp.exp(scores - m)
    return e / (jnp.sum(e, axis=-1, keepdims=True) + jnp.exp(sink - m))


def _windowed_attention(q, k, v, k_ctx, v_ctx, sink):
    bsz, n_tok = q.shape[:2]
    pad = ((0, 0), (BLOCK, BLOCK), (0, 0), (0, 0))
    k_pad = jnp.pad(k, pad)
    v_pad = jnp.pad(v, pad)
    ctx_mask = jnp.ones((BLOCK, k_ctx.shape[1]), dtype=bool)

    def one_block(i):
        start = i * BLOCK
        q_blk = lax.dynamic_slice_in_dim(q, start, BLOCK, axis=1)
        k_blk = jnp.concatenate([lax.dynamic_slice_in_dim(k_pad, start, 3 * BLOCK, axis=1), k_ctx], axis=1)
        v_blk = jnp.concatenate([lax.dynamic_slice_in_dim(v_pad, start, 3 * BLOCK, axis=1), v_ctx], axis=1)
        q_pos = start + jnp.arange(BLOCK)
        k_pos = start - BLOCK + jnp.arange(3 * BLOCK)
        band = ((jnp.abs(q_pos[:, None] - k_pos[None, :]) <= WINDOW)
                & (k_pos[None, :] >= 0) & (k_pos[None, :] < n_tok))
        mask = jnp.concatenate([band, ctx_mask], axis=1)
        scores = jnp.where(mask, _gqa_scores(q_blk, k_blk), -jnp.inf)
        return _gqa_values(_sink_softmax(scores, sink), v_blk)

    out = lax.map(one_block, jnp.arange(n_tok // BLOCK))
    return jnp.moveaxis(out, 0, 1).reshape(bsz, n_tok, -1)


def _global_attention(q, k_all, v_all):
    bsz, n_tok = q.shape[:2]

    def one_block(i):
        q_blk = lax.dynamic_slice_in_dim(q, i * BLOCK, BLOCK, axis=1)
        p = jax.nn.softmax(_gqa_scores(q_blk, k_all), axis=-1)
        return _gqa_values(p, v_all)

    out = lax.map(one_block, jnp.arange(n_tok // BLOCK))
    return jnp.moveaxis(out, 0, 1).reshape(bsz, n_tok, -1)


def _merge_branches(o_a, o_b, gate_logits, w_branch_a, w_branch_b, w_out):
    g_a, g_b = jnp.split(jax.nn.sigmoid(gate_logits), 2, axis=-1)
    return (g_a * (o_a @ w_branch_a) + g_b * (o_b @ w_branch_b)) @ w_out


def _context_kv(h_c, w_in, b_in, k_norm_g):
    bsz, n_ctx, _ = h_c.shape
    kv_a = (h_c @ w_in[:, OFF_KA:OFF_QB] + b_in[OFF_KA:OFF_QB]).reshape(bsz, n_ctx, 2 * A_KV_HEADS, HEAD_DIM)
    kv_b = (h_c @ w_in[:, OFF_KB:OFF_GA] + b_in[OFF_KB:OFF_GA]).reshape(bsz, n_ctx, 2 * B_KV_HEADS, HEAD_DIM)
    k_a, v_a = jnp.split(kv_a, 2, axis=2)
    k_b, v_b = jnp.split(kv_b, 2, axis=2)
    return (k_a, v_a, _qk_rms(k_b, k_norm_g), v_b)


def _latent_token_mixer(h, kv_ctx, w_in, b_in, sink, q_norm_g, k_norm_g,
                        w_branch_a, w_branch_b, w_out, rope):
    bsz, n_tok, _ = h.shape
    k_a_c, v_a_c, k_b_c, v_b_c = kv_ctx
    proj = h @ w_in + b_in

    def heads(lo, hi, n):
        return proj[..., lo:hi].reshape(bsz, n_tok, n, HEAD_DIM)

    q_a = _rope_2d(heads(OFF_QA, OFF_KA, A_HEADS), rope).reshape(bsz, n_tok, A_KV_HEADS, A_GROUPS, HEAD_DIM)
    k_a = _rope_2d(heads(OFF_KA, OFF_VA, A_KV_HEADS), rope)
    v_a = heads(OFF_VA, OFF_QB, A_KV_HEADS)
    q_b = _rope_2d(_qk_rms(heads(OFF_QB, OFF_KB, B_HEADS), q_norm_g), rope).reshape(
        bsz, n_tok, B_KV_HEADS, B_GROUPS, HEAD_DIM)
    k_b = _rope_2d(_qk_rms(heads(OFF_KB, OFF_VB, B_KV_HEADS), k_norm_g), rope)
    v_b = heads(OFF_VB, OFF_GA, B_KV_HEADS)
    o_a = _windowed_attention(q_a, k_a, v_a, k_a_c, v_a_c, sink.reshape(A_KV_HEADS, A_GROUPS))
    o_b = _global_attention(q_b, jnp.concatenate([k_b, k_b_c], axis=1), jnp.concatenate([v_b, v_b_c], axis=1))
    return _merge_branches(o_a, o_b, proj[..., OFF_GA:], w_branch_a, w_branch_b, w_out)


def _context_token_mixer(h_c, kv_ctx, w_in, b_in, sink, q_norm_g, w_branch_a, w_branch_b, w_out):
    bsz, n_ctx, _ = h_c.shape
    k_a_c, v_a_c, k_b_c, v_b_c = kv_ctx
    q_a = (h_c @ w_in[:, OFF_QA:OFF_KA] + b_in[OFF_QA:OFF_KA]).reshape(bsz, n_ctx, A_KV_HEADS, A_GROUPS, HEAD_DIM)
    q_b = _qk_rms((h_c @ w_in[:, OFF_QB:OFF_KB] + b_in[OFF_QB:OFF_KB]).reshape(bsz, n_ctx, B_HEADS, HEAD_DIM),
                  q_norm_g).reshape(bsz, n_ctx, B_KV_HEADS, B_GROUPS, HEAD_DIM)
    gate_logits = h_c @ w_in[:, OFF_GA:] + b_in[OFF_GA:]
    o_a = _gqa_values(_sink_softmax(_gqa_scores(q_a, k_a_c), sink.reshape(A_KV_HEADS, A_GROUPS)), v_a_c)
    o_b = _gqa_values(jax.nn.softmax(_gqa_scores(q_b, k_b_c), axis=-1), v_b_c)
    return _merge_branches(o_a.reshape(bsz, n_ctx, -1), o_b.reshape(bsz, n_ctx, -1), gate_logits,
                           w_branch_a, w_branch_b, w_out)


def _conv_ffn(h, w_up, conv_w, conv_b, w_down):
    u = h @ w_up
    half = CONV_WIDTH // 2
    u = lax.conv_general_dilated(u, conv_w[:, None, :], window_strides=(1,), padding=((half, half),),
                                 dimension_numbers=("NWC", "WIO", "NWC"),
                                 feature_group_count=u.shape[-1]) + conv_b
    gate, val = jnp.split(u, 2, axis=-1)
    return (jax.nn.silu(gate) * val) @ w_down


def _fwd_setup_inputs(seed: int = 0) -> dict:
    key = jax.random.key(seed)
    ks = jax.random.split(key, 24)
    f32 = jnp.float32

    def normal(k, shape, scale):
        return jax.random.normal(k, shape, f32) * scale

    L = DEPTH
    return {
        "x": normal(ks[0], (BATCH, SEQ, D_MODEL), 1.0),
        "c": normal(ks[1], (BATCH, D_MODEL), 1.0),
        "ctx": normal(ks[2], (BATCH, CTX_LEN, D_MODEL), 1.0),
        "c_ctx": normal(ks[3], (D_MODEL,), 1.0),
        "w_mod": normal(ks[4], (L, D_MODEL, N_MOD * D_MODEL), 0.5 * D_MODEL ** -0.5),
        "b_mod": normal(ks[5], (L, N_MOD * D_MODEL), 0.02),
        "w_in": normal(ks[6], (L, D_MODEL, IN_COLS), D_MODEL ** -0.5),
        "b_in": normal(ks[7], (L, IN_COLS), 0.02),
        "attn_sink": normal(ks[8], (L, A_HEADS), 0.5),
        "q_norm_g": 1.0 + normal(ks[9], (L, HEAD_DIM), 0.05),
        "k_norm_g": 1.0 + normal(ks[10], (L, HEAD_DIM), 0.05),
        "w_branch_a": normal(ks[11], (L, A_HEADS * HEAD_DIM, D_MODEL), (A_HEADS * HEAD_DIM) ** -0.5),
        "w_branch_b": normal(ks[12], (L, B_HEADS * HEAD_DIM, D_MODEL), (B_HEADS * HEAD_DIM) ** -0.5),
        "w_out": normal(ks[13], (L, D_MODEL, D_MODEL), DEEPNORM_BETA * D_MODEL ** -0.5),
        "ln1_g": 1.0 + normal(ks[14], (L, D_MODEL), 0.05),
        "ln1_b": normal(ks[15], (L, D_MODEL), 0.02),
        "w_up": normal(ks[16], (L, D_MODEL, 2 * D_FF), D_MODEL ** -0.5),
        "conv_w": normal(ks[17], (L, CONV_WIDTH, 2 * D_FF), CONV_WIDTH ** -0.5),
        "conv_b": normal(ks[18], (L, 2 * D_FF), 0.02),
        "w_down": normal(ks[19], (L, D_FF, D_MODEL), DEEPNORM_BETA * D_FF ** -0.5),
        "ln2_g": 1.0 + normal(ks[20], (L, D_MODEL), 0.05),
        "ln2_b": normal(ks[21], (L, D_MODEL), 0.02),
    }


def _fwd_reference(x, c, ctx, c_ctx, w_mod, b_mod, w_in, b_in, attn_sink, q_norm_g, k_norm_g,
              w_branch_a, w_branch_b, w_out, ln1_g, ln1_b, w_up, conv_w, conv_b, w_down,
              ln2_g, ln2_b):
    n_tok = x.shape[1]
    rope = _axial_rope_tables(n_tok, x.dtype)
    for l in range(DEPTH):
        last = l == DEPTH - 1
        mod = jax.nn.silu(c) @ w_mod[l] + b_mod[l]
        shift1, scale1, gate1, shift2, scale2, gate2 = jnp.split(mod[:, None, :], N_MOD, axis=-1)
        n_mod_c = 2 if last else N_MOD
        mod_c = jax.nn.silu(c_ctx) @ w_mod[l, :, :n_mod_c * D_MODEL] + b_mod[l, :n_mod_c * D_MODEL]
        mods_c = jnp.split(mod_c, n_mod_c)
        h_c = ctx * (1.0 + mods_c[1]) + mods_c[0]
        kv_ctx = _context_kv(h_c, w_in[l], b_in[l], k_norm_g[l])

        h = x * (1.0 + scale1) + shift1
        y = _latent_token_mixer(h, kv_ctx, w_in[l], b_in[l], attn_sink[l], q_norm_g[l], k_norm_g[l],
                                w_branch_a[l], w_branch_b[l], w_out[l], rope)
        x = _layer_norm(DEEPNORM_ALPHA * x + gate1 * y, ln1_g[l], ln1_b[l])

        h = x * (1.0 + scale2) + shift2
        y = _conv_ffn(h, w_up[l], conv_w[l], conv_b[l], w_down[l])
        x = _layer_norm(DEEPNORM_ALPHA * x + gate2 * y, ln2_g[l], ln2_b[l])

        if not last:
            y_c = _context_token_mixer(h_c, kv_ctx, w_in[l], b_in[l], attn_sink[l], q_norm_g[l],
                                       w_branch_a[l], w_branch_b[l], w_out[l])
            ctx = _layer_norm(DEEPNORM_ALPHA * ctx + mods_c[2] * y_c, ln1_g[l], ln1_b[l])
            h_c = ctx * (1.0 + mods_c[4]) + mods_c[3]
            y_c = _conv_ffn(h_c, w_up[l], conv_w[l], conv_b[l], w_down[l])
            ctx = _layer_norm(DEEPNORM_ALPHA * ctx + mods_c[5] * y_c, ln2_g[l], ln2_b[l])
    return x


import jax as _jax
import jax.numpy as _jnp

TWIN_FORMAT = 'train_step'
FWD_PARAMS = ['x', 'c', 'ctx', 'c_ctx', 'w_mod', 'b_mod', 'w_in', 'b_in', 'attn_sink', 'q_norm_g', 'k_norm_g', 'w_branch_a', 'w_branch_b', 'w_out', 'ln1_g', 'ln1_b', 'w_up', 'conv_w', 'conv_b', 'w_down', 'ln2_g', 'ln2_b']
TWIN_WEIGHTS = ['c_ctx', 'w_mod', 'b_mod', 'w_in', 'b_in', 'attn_sink', 'q_norm_g', 'k_norm_g', 'w_branch_a', 'w_branch_b', 'w_out', 'ln1_g', 'ln1_b', 'w_up', 'conv_w', 'conv_b', 'w_down', 'ln2_g', 'ln2_b']
TWIN_DIFF_INPUT = 'x'
TWIN_INPUTS = ['x', 'c', 'ctx', 'c_ctx', 'w_mod', 'b_mod', 'w_in', 'b_in', 'attn_sink', 'q_norm_g', 'k_norm_g', 'w_branch_a', 'w_branch_b', 'w_out', 'ln1_g', 'ln1_b', 'w_up', 'conv_w', 'conv_b', 'w_down', 'ln2_g', 'ln2_b', 'loss_target', 'm_c_ctx', 'm_w_mod', 'm_b_mod', 'm_w_in', 'm_b_in', 'm_attn_sink', 'm_q_norm_g', 'm_k_norm_g', 'm_w_branch_a', 'm_w_branch_b', 'm_w_out', 'm_ln1_g', 'm_ln1_b', 'm_w_up', 'm_conv_w', 'm_conv_b', 'm_w_down', 'm_ln2_g', 'm_ln2_b', 'v_c_ctx', 'v_w_mod', 'v_b_mod', 'v_w_in', 'v_b_in', 'v_attn_sink', 'v_q_norm_g', 'v_k_norm_g', 'v_w_branch_a', 'v_w_branch_b', 'v_w_out', 'v_ln1_g', 'v_ln1_b', 'v_w_up', 'v_conv_w', 'v_conv_b', 'v_w_down', 'v_ln2_g', 'v_ln2_b']
TWIN_OUTPUTS = ['loss', 'grad_x', 'grad_c_ctx', 'grad_w_mod', 'grad_b_mod', 'grad_w_in', 'grad_b_in', 'grad_attn_sink', 'grad_q_norm_g', 'grad_k_norm_g', 'grad_w_branch_a', 'grad_w_branch_b', 'grad_w_out', 'grad_ln1_g', 'grad_ln1_b', 'grad_w_up', 'grad_conv_w', 'grad_conv_b', 'grad_w_down', 'grad_ln2_g', 'grad_ln2_b', 'delta_c_ctx', 'delta_w_mod', 'delta_b_mod', 'delta_w_in', 'delta_b_in', 'delta_attn_sink', 'delta_q_norm_g', 'delta_k_norm_g', 'delta_w_branch_a', 'delta_w_branch_b', 'delta_w_out', 'delta_ln1_g', 'delta_ln1_b', 'delta_w_up', 'delta_conv_w', 'delta_conv_b', 'delta_w_down', 'delta_ln2_g', 'delta_ln2_b', 'new_m_c_ctx', 'new_m_w_mod', 'new_m_b_mod', 'new_m_w_in', 'new_m_b_in', 'new_m_attn_sink', 'new_m_q_norm_g', 'new_m_k_norm_g', 'new_m_w_branch_a', 'new_m_w_branch_b', 'new_m_w_out', 'new_m_ln1_g', 'new_m_ln1_b', 'new_m_w_up', 'new_m_conv_w', 'new_m_conv_b', 'new_m_w_down', 'new_m_ln2_g', 'new_m_ln2_b', 'new_v_c_ctx', 'new_v_w_mod', 'new_v_b_mod', 'new_v_w_in', 'new_v_b_in', 'new_v_attn_sink', 'new_v_q_norm_g', 'new_v_k_norm_g', 'new_v_w_branch_a', 'new_v_w_branch_b', 'new_v_w_out', 'new_v_ln1_g', 'new_v_ln1_b', 'new_v_w_up', 'new_v_conv_w', 'new_v_conv_b', 'new_v_w_down', 'new_v_ln2_g', 'new_v_ln2_b']
TWIN_LEAF_KINDS = {'loss': 'loss', 'grad_x': 'grad_x', 'grad_c_ctx': 'grad_w', 'grad_w_mod': 'grad_w', 'grad_b_mod': 'grad_w', 'grad_w_in': 'grad_w', 'grad_b_in': 'grad_w', 'grad_attn_sink': 'grad_w', 'grad_q_norm_g': 'grad_w', 'grad_k_norm_g': 'grad_w', 'grad_w_branch_a': 'grad_w', 'grad_w_branch_b': 'grad_w', 'grad_w_out': 'grad_w', 'grad_ln1_g': 'grad_w', 'grad_ln1_b': 'grad_w', 'grad_w_up': 'grad_w', 'grad_conv_w': 'grad_w', 'grad_conv_b': 'grad_w', 'grad_w_down': 'grad_w', 'grad_ln2_g': 'grad_w', 'grad_ln2_b': 'grad_w', 'delta_c_ctx': 'delta_w', 'delta_w_mod': 'delta_w', 'delta_b_mod': 'delta_w', 'delta_w_in': 'delta_w', 'delta_b_in': 'delta_w', 'delta_attn_sink': 'delta_w', 'delta_q_norm_g': 'delta_w', 'delta_k_norm_g': 'delta_w', 'delta_w_branch_a': 'delta_w', 'delta_w_branch_b': 'delta_w', 'delta_w_out': 'delta_w', 'delta_ln1_g': 'delta_w', 'delta_ln1_b': 'delta_w', 'delta_w_up': 'delta_w', 'delta_conv_w': 'delta_w', 'delta_conv_b': 'delta_w', 'delta_w_down': 'delta_w', 'delta_ln2_g': 'delta_w', 'delta_ln2_b': 'delta_w', 'new_m_c_ctx': 'new_m', 'new_m_w_mod': 'new_m', 'new_m_b_mod': 'new_m', 'new_m_w_in': 'new_m', 'new_m_b_in': 'new_m', 'new_m_attn_sink': 'new_m', 'new_m_q_norm_g': 'new_m', 'new_m_k_norm_g': 'new_m', 'new_m_w_branch_a': 'new_m', 'new_m_w_branch_b': 'new_m', 'new_m_w_out': 'new_m', 'new_m_ln1_g': 'new_m', 'new_m_ln1_b': 'new_m', 'new_m_w_up': 'new_m', 'new_m_conv_w': 'new_m', 'new_m_conv_b': 'new_m', 'new_m_w_down': 'new_m', 'new_m_ln2_g': 'new_m', 'new_m_ln2_b': 'new_m', 'new_v_c_ctx': 'new_v', 'new_v_w_mod': 'new_v', 'new_v_b_mod': 'new_v', 'new_v_w_in': 'new_v', 'new_v_b_in': 'new_v', 'new_v_attn_sink': 'new_v', 'new_v_q_norm_g': 'new_v', 'new_v_k_norm_g': 'new_v', 'new_v_w_branch_a': 'new_v', 'new_v_w_branch_b': 'new_v', 'new_v_w_out': 'new_v', 'new_v_ln1_g': 'new_v', 'new_v_ln1_b': 'new_v', 'new_v_w_up': 'new_v', 'new_v_conv_w': 'new_v', 'new_v_conv_b': 'new_v', 'new_v_w_down': 'new_v', 'new_v_ln2_g': 'new_v', 'new_v_ln2_b': 'new_v'}


def _forward(args):
    return _fwd_reference(*[args[k] for k in FWD_PARAMS])


def _output_shape():
    def fwd():
        inp = _fwd_setup_inputs(0)
        return _fwd_reference(*[inp[k] for k in FWD_PARAMS])
    out = _jax.eval_shape(fwd)
    return out.shape, out.dtype

N_MICROBATCH = 1
ADAM_LR = 0.001
ADAM_B1 = 0.9
ADAM_B2 = 0.999
ADAM_EPS = 1e-08
ADAM_WD = 0.01
ADAM_STEP = 10
PER_EXAMPLE_BATCH_AXIS = {'x': 0, 'c': 0, 'ctx': 0, 'loss_target': 0}
SHARED_INPUTS = []
_WEIGHT_DTYPES = {'c_ctx': _jnp.float32, 'w_mod': _jnp.float32, 'b_mod': _jnp.float32, 'w_in': _jnp.float32, 'b_in': _jnp.float32, 'attn_sink': _jnp.float32, 'q_norm_g': _jnp.float32, 'k_norm_g': _jnp.float32, 'w_branch_a': _jnp.float32, 'w_branch_b': _jnp.float32, 'w_out': _jnp.float32, 'ln1_g': _jnp.float32, 'ln1_b': _jnp.float32, 'w_up': _jnp.float32, 'conv_w': _jnp.float32, 'conv_b': _jnp.float32, 'w_down': _jnp.float32, 'ln2_g': _jnp.float32, 'ln2_b': _jnp.float32}
MOMENT_SCALE = {'c_ctx': 7.212045e-03, 'w_mod': 4.864107e-02, 'b_mod': 9.798368e-02, 'w_in': 9.536086e-03, 'b_in': 4.561500e-02, 'attn_sink': 8.082521e-05, 'q_norm_g': 9.969689e-03, 'k_norm_g': 1.017068e-02, 'w_branch_a': 1.079923e-02, 'w_branch_b': 1.362672e-02, 'w_out': 2.951194e-02, 'ln1_g': 1.170431e+01, 'ln1_b': 2.003594e+00, 'w_up': 2.452091e-02, 'conv_w': 2.446929e-02, 'conv_b': 2.465919e-02, 'w_down': 6.767492e-02, 'ln2_g': 1.289520e+02, 'ln2_b': 2.880447e+00}


def _to_microbatches(a, axis):
    t = _jnp.moveaxis(a, axis, 0)
    t = t.reshape((N_MICROBATCH, t.shape[0] // N_MICROBATCH) + t.shape[1:])
    return _jnp.moveaxis(t, 1, axis + 1)


def setup_inputs(seed: int = 0) -> dict:
    inp = _fwd_setup_inputs(seed)
    key = _jax.random.fold_in(_jax.random.key(seed), 7919)
    shape, _ = _output_shape()
    out = dict(inp)
    out["loss_target"] = _jax.random.normal(_jax.random.fold_in(key, 0), shape, _jnp.float32)
    for i, name in enumerate(TWIN_WEIGHTS):
        w = inp[name].astype(_jnp.float32)
        if MOMENT_SCALE is None:
            s = _jnp.sqrt(_jnp.mean(_jnp.square(w)) + 1e-30)
        else:
            s = MOMENT_SCALE[name]
        km, kv = _jax.random.split(_jax.random.fold_in(key, i + 1))
        out[name] = w
        out["m_" + name] = s * _jax.random.normal(km, w.shape, _jnp.float32)
        out["v_" + name] = (s * s) * _jax.random.uniform(kv, w.shape, _jnp.float32, 0.5, 1.5)
    if N_MICROBATCH > 1:
        for name, axis in PER_EXAMPLE_BATCH_AXIS.items():
            out[name] = _to_microbatches(out[name], axis)
    return {'x': out['x'], 'c': out['c'], 'ctx': out['ctx'], 'c_ctx': out['c_ctx'], 'w_mod': out['w_mod'], 'b_mod': out['b_mod'], 'w_in': out['w_in'], 'b_in': out['b_in'], 'attn_sink': out['attn_sink'], 'q_norm_g': out['q_norm_g'], 'k_norm_g': out['k_norm_g'], 'w_branch_a': out['w_branch_a'], 'w_branch_b': out['w_branch_b'], 'w_out': out['w_out'], 'ln1_g': out['ln1_g'], 'ln1_b': out['ln1_b'], 'w_up': out['w_up'], 'conv_w': out['conv_w'], 'conv_b': out['conv_b'], 'w_down': out['w_down'], 'ln2_g': out['ln2_g'], 'ln2_b': out['ln2_b'], 'loss_target': out['loss_target'], 'm_c_ctx': out['m_c_ctx'], 'm_w_mod': out['m_w_mod'], 'm_b_mod': out['m_b_mod'], 'm_w_in': out['m_w_in'], 'm_b_in': out['m_b_in'], 'm_attn_sink': out['m_attn_sink'], 'm_q_norm_g': out['m_q_norm_g'], 'm_k_norm_g': out['m_k_norm_g'], 'm_w_branch_a': out['m_w_branch_a'], 'm_w_branch_b': out['m_w_branch_b'], 'm_w_out': out['m_w_out'], 'm_ln1_g': out['m_ln1_g'], 'm_ln1_b': out['m_ln1_b'], 'm_w_up': out['m_w_up'], 'm_conv_w': out['m_conv_w'], 'm_conv_b': out['m_conv_b'], 'm_w_down': out['m_w_down'], 'm_ln2_g': out['m_ln2_g'], 'm_ln2_b': out['m_ln2_b'], 'v_c_ctx': out['v_c_ctx'], 'v_w_mod': out['v_w_mod'], 'v_b_mod': out['v_b_mod'], 'v_w_in': out['v_w_in'], 'v_b_in': out['v_b_in'], 'v_attn_sink': out['v_attn_sink'], 'v_q_norm_g': out['v_q_norm_g'], 'v_k_norm_g': out['v_k_norm_g'], 'v_w_branch_a': out['v_w_branch_a'], 'v_w_branch_b': out['v_w_branch_b'], 'v_w_out': out['v_w_out'], 'v_ln1_g': out['v_ln1_g'], 'v_ln1_b': out['v_ln1_b'], 'v_w_up': out['v_w_up'], 'v_conv_w': out['v_conv_w'], 'v_conv_b': out['v_conv_b'], 'v_w_down': out['v_w_down'], 'v_ln2_g': out['v_ln2_g'], 'v_ln2_b': out['v_ln2_b']}


def _loss(weights, diff, rest, loss_target):
    with _jax.named_scope("forward"):
        args = {**rest, TWIN_DIFF_INPUT: diff, **{k: w.astype(_WEIGHT_DTYPES[k]) for k, w in weights.items()}}
        y = _forward(args)
    with _jax.named_scope("loss_head"):
        err = _jnp.square(y.astype(_jnp.float32) - loss_target)
        return 0.5 * _jnp.sum(_jnp.mean(err, axis=-1)) if err.ndim else 0.5 * err


def _adamw(w, g, m, v):
    m = ADAM_B1 * m + (1.0 - ADAM_B1) * g
    v = ADAM_B2 * v + (1.0 - ADAM_B2) * _jnp.square(g)
    m_hat = m / (1.0 - ADAM_B1 ** ADAM_STEP)
    v_hat = v / (1.0 - ADAM_B2 ** ADAM_STEP)
    delta = -ADAM_LR * (m_hat / (_jnp.sqrt(v_hat) + ADAM_EPS) + ADAM_WD * w)
    return delta, m, v


def reference(x, c, ctx, c_ctx, w_mod, b_mod, w_in, b_in, attn_sink, q_norm_g, k_norm_g, w_branch_a, w_branch_b, w_out, ln1_g, ln1_b, w_up, conv_w, conv_b, w_down, ln2_g, ln2_b, loss_target, m_c_ctx, m_w_mod, m_b_mod, m_w_in, m_b_in, m_attn_sink, m_q_norm_g, m_k_norm_g, m_w_branch_a, m_w_branch_b, m_w_out, m_ln1_g, m_ln1_b, m_w_up, m_conv_w, m_conv_b, m_w_down, m_ln2_g, m_ln2_b, v_c_ctx, v_w_mod, v_b_mod, v_w_in, v_b_in, v_attn_sink, v_q_norm_g, v_k_norm_g, v_w_branch_a, v_w_branch_b, v_w_out, v_ln1_g, v_ln1_b, v_w_up, v_conv_w, v_conv_b, v_w_down, v_ln2_g, v_ln2_b):
    given = dict(x=x, c=c, ctx=ctx, c_ctx=c_ctx, w_mod=w_mod, b_mod=b_mod, w_in=w_in, b_in=b_in, attn_sink=attn_sink, q_norm_g=q_norm_g, k_norm_g=k_norm_g, w_branch_a=w_branch_a, w_branch_b=w_branch_b, w_out=w_out, ln1_g=ln1_g, ln1_b=ln1_b, w_up=w_up, conv_w=conv_w, conv_b=conv_b, w_down=w_down, ln2_g=ln2_g, ln2_b=ln2_b, loss_target=loss_target, m_c_ctx=m_c_ctx, m_w_mod=m_w_mod, m_b_mod=m_b_mod, m_w_in=m_w_in, m_b_in=m_b_in, m_attn_sink=m_attn_sink, m_q_norm_g=m_q_norm_g, m_k_norm_g=m_k_norm_g, m_w_branch_a=m_w_branch_a, m_w_branch_b=m_w_branch_b, m_w_out=m_w_out, m_ln1_g=m_ln1_g, m_ln1_b=m_ln1_b, m_w_up=m_w_up, m_conv_w=m_conv_w, m_conv_b=m_conv_b, m_w_down=m_w_down, m_ln2_g=m_ln2_g, m_ln2_b=m_ln2_b, v_c_ctx=v_c_ctx, v_w_mod=v_w_mod, v_b_mod=v_b_mod, v_w_in=v_w_in, v_b_in=v_b_in, v_attn_sink=v_attn_sink, v_q_norm_g=v_q_norm_g, v_k_norm_g=v_k_norm_g, v_w_branch_a=v_w_branch_a, v_w_branch_b=v_w_branch_b, v_w_out=v_w_out, v_ln1_g=v_ln1_g, v_ln1_b=v_ln1_b, v_w_up=v_w_up, v_conv_w=v_conv_w, v_conv_b=v_conv_b, v_w_down=v_w_down, v_ln2_g=v_ln2_g, v_ln2_b=v_ln2_b)
    weights = {n: given[n] for n in TWIN_WEIGHTS}
    shared = {n: given[n] for n in SHARED_INPUTS}
    per_example = {n: given[n] for n in ['x', 'c', 'ctx']}
    grad_fn = _jax.value_and_grad(_loss, argnums=(0, 1))

    def one_microbatch(ex, loss_target):
        ex = dict(ex)
        diff = ex.pop(TWIN_DIFF_INPUT)
        return grad_fn(weights, diff, {**shared, **ex}, loss_target)

    if N_MICROBATCH == 1:
        loss, (grad_w, grad_x) = one_microbatch(per_example, given["loss_target"])
    else:
        def body(carry, xs):
            loss_sum, grad_sum = carry
            l_k, (gw_k, gx_k) = one_microbatch(xs[0], xs[1])
            with _jax.named_scope("update"):
                return (loss_sum + l_k, _jax.tree.map(_jnp.add, grad_sum, gw_k)), gx_k

        init = (_jnp.zeros((), _jnp.float32), _jax.tree.map(_jnp.zeros_like, weights))
        (loss, grad_w), grad_x = _jax.lax.scan(body, init, (per_example, given["loss_target"]))
    with _jax.named_scope("update"):
        delta_w, new_m, new_v = {}, {}, {}
        for n in TWIN_WEIGHTS:
            delta_w[n], new_m[n], new_v[n] = _adamw(weights[n], grad_w[n], given["m_" + n], given["v_" + n])
    return (loss, grad_x, *[grad_w[n] for n in TWIN_WEIGHTS], *[delta_w[n] for n in TWIN_WEIGHTS],
            *[new_m[n] for n in TWIN_WEIGHTS], *[new_v[n] for n in TWIN_WEIGHTS])
```

```python
import functools
import math

import jax
import jax.numpy as jnp
from jax import lax
from jax.experimental import pallas as pl
from jax.experimental.pallas import tpu as pltpu

F32 = jnp.float32
BF16 = jnp.bfloat16

D_MODEL = 1024
HEAD_DIM = 64
N_HEADS = 8
N_KV = 2
WINDOW = 128
GRID_W = 64
ROPE_THETA = 10000.0
D_FF = 2816
LN_EPS = 1e-5
QK_EPS = 1e-6
ALPHA = 2.0 ** 0.25
Q_SCALE = HEAD_DIM ** -0.5
OFF_GA = 1536
IN_COLS = 3584
ADAM_LR, ADAM_B1, ADAM_B2, ADAM_EPS, ADAM_WD, ADAM_STEP = 0.001, 0.9, 0.999, 1e-8, 0.01, 10

LANES = 128
VMEM_BUDGET = 52 * 1024 * 1024
N_CHIPS = 4
N_DEV = 8
NEG = -1e30
MESH = pl.DeviceIdType.MESH


def _sigmoid(x):
    return 1.0 / (1.0 + jnp.exp(-x))


def _dot(a, b):
    return jnp.dot(a, b, preferred_element_type=F32)


def _dot_nt(a, b):
    return lax.dot_general(a, b, (((1,), (1,)), ((), ())), preferred_element_type=F32)


def _dot_tn(a, b):
    return lax.dot_general(a, b, (((0,), (0,)), ((), ())), preferred_element_type=F32)


def _call(body, *, name, grid, in_specs, out_specs, out_shape, scratch=(), sem=None, **kw):
    params = dict(vmem_limit_bytes=VMEM_BUDGET)
    if sem is not None:
        params["dimension_semantics"] = sem
    return pl.pallas_call(body, name=name, grid=grid, in_specs=in_specs, out_specs=out_specs,
                          out_shape=out_shape, scratch_shapes=list(scratch),
                          compiler_params=pltpu.CompilerParams(**params), **kw)


def _full(shape):
    n = len(shape)
    return pl.BlockSpec(shape, lambda *_: (0,) * n)


def _sds(shape, dtype=F32):
    return jax.ShapeDtypeStruct(shape, dtype)


def _mm_nn4(a, shift, scale, w4, bias, *, mode, split_out, out_dtype, tm, name):
    m, kdim = a.shape
    nb, _, ns = w4.shape

    def body(a_ref, sh_ref, sc_ref, w_ref, b_ref, o_ref):
        av = a_ref[...]
        if mode == "modulate":
            av = av * (1.0 + sc_ref[...]) + sh_ref[...]
        else:
            av = av * _sigmoid(av)
        o_ref[...] = (_dot(av.astype(BF16), w_ref[...]) + b_ref[...]).astype(out_dtype)

    if split_out:
        out_shape = _sds((2, m, 2 * ns), out_dtype)
        out_spec = pl.BlockSpec((None, tm, ns), lambda i, k: (k // 2, i, k % 2))
    else:
        out_shape = _sds((m, nb * ns), out_dtype)
        out_spec = pl.BlockSpec((tm, ns), lambda i, k: (i, k))
    return _call(
        body, name=name, grid=(m // tm, nb),
        in_specs=[pl.BlockSpec((tm, kdim), lambda i, k: (i, 0)),
                  pl.BlockSpec((1, kdim), lambda i, k: (0, 0)),
                  pl.BlockSpec((1, kdim), lambda i, k: (0, 0)),
                  pl.BlockSpec((None, kdim, ns), lambda i, k: (k, 0, 0)),
                  pl.BlockSpec((1, ns), lambda i, k: (0, k))],
        out_specs=out_spec, out_shape=out_shape, sem=("parallel", "arbitrary"),
    )(a, shift, scale, w4, bias)


def _mm_tn(a, b, *, a_spec, b_spec, grid, out_shape, out_spec, name, mod=None, init=None, colsum_spec=None,
           colsum_shape=None):
    red = len(grid) - 1
    has_mod, has_init, has_cs = mod is not None, init is not None, colsum_spec is not None

    def body(*refs):
        refs = list(refs)
        a_ref, b_ref = refs[0], refs[1]
        pos = 2
        if has_mod:
            sh_ref, sc_ref = refs[2], refs[3]
            pos = 4
        if has_init:
            init_ref = refs[pos]
            pos += 1
        o_ref = refs[pos]
        cs_ref = refs[pos + 1] if has_cs else None
        s = pl.program_id(red)

        @pl.when(s == 0)
        def _():
            o_ref[...] = init_ref[...] if has_init else jnp.zeros(o_ref.shape, F32)
            if has_cs:
                cs_ref[...] = jnp.zeros(cs_ref.shape, F32)

        av = a_ref[...]
        if has_mod:
            av = av * (1.0 + sc_ref[...]) + sh_ref[...]
        bv = b_ref[...]
        o_ref[...] += _dot_tn(av.astype(BF16), bv)
        if has_cs:
            cs_ref[...] += jnp.broadcast_to(jnp.sum(bv.astype(F32), axis=0, keepdims=True), cs_ref.shape)

    ins, in_specs = [a, b], [a_spec, b_spec]
    if has_mod:
        kdim = mod[0].shape[-1]
        ins += list(mod)
        in_specs += [_full((1, kdim)), _full((1, kdim))]
    if has_init:
        ins.append(init)
        in_specs.append(out_spec)
    out_specs, out_shapes = out_spec, out_shape
    if has_cs:
        out_specs, out_shapes = [out_spec, colsum_spec], [out_shape, colsum_shape]
    sem = ("parallel",) * red + ("arbitrary",)
    return _call(body, name=name, grid=grid, in_specs=in_specs, out_specs=out_specs, out_shape=out_shapes,
                 sem=sem)(*ins)


def _rope_tables(n_tok):
    pos = jnp.arange(n_tok, dtype=jnp.int32)
    rows = (pos // GRID_W).astype(F32)
    cols = (pos % GRID_W).astype(F32)
    n_freq = HEAD_DIM // 4
    inv_freq = ROPE_THETA ** (-jnp.arange(n_freq, dtype=F32) / n_freq)
    ang_r = rows[:, None] * inv_freq
    ang_c = cols[:, None] * inv_freq
    cos = jnp.concatenate([jnp.cos(ang_r)] * 2 + [jnp.cos(ang_c)] * 2, axis=-1)
    sin = jnp.concatenate([-jnp.sin(ang_r), jnp.sin(ang_r), -jnp.sin(ang_c), jnp.sin(ang_c)], axis=-1)
    return jnp.tile(cos, (1, 2)), jnp.tile(sin, (1, 2))


def _lane(shape):
    return lax.broadcasted_iota(jnp.int32, shape, 1)


def _rope_partner(t, lane):
    return jnp.where((lane % 32) < 16, pltpu.roll(t, LANES - 16, 1), pltpu.roll(t, 16, 1))


def _half_mean(s, lane):
    lo = jnp.sum(jnp.where(lane < HEAD_DIM, s, 0.0), axis=-1, keepdims=True)
    hi = jnp.sum(jnp.where(lane < HEAD_DIM, 0.0, s), axis=-1, keepdims=True)
    return jnp.where(lane < HEAD_DIM, lo, hi) * (1.0 / HEAD_DIM)


def _prep(proj, cos, sin, qg, kg, *, tm, name):
    m = proj.shape[0]

    def body(p_ref, cos_ref, sin_ref, qg_ref, kg_ref, qa_ref, ka_ref, va_ref, qb_ref, kb_ref, vb_ref):
        lane = _lane((tm, LANES))
        cosv, sinv = cos_ref[...], sin_ref[...]
        low = lane < HEAD_DIM

        def rope(t):
            return t * cosv + _rope_partner(t, lane) * sinv

        def rms(t, g):
            return t * lax.rsqrt(_half_mean(t * t, lane) + QK_EPS) * g

        def place(q_ref, j, chunk):
            sw = pltpu.roll(chunk, HEAD_DIM, 1)
            if j < 2:
                h0, h1 = jnp.where(low, chunk, 0.0), jnp.where(low, sw, 0.0)
            else:
                h0, h1 = jnp.where(low, 0.0, sw), jnp.where(low, 0.0, chunk)
            q_ref[2 * j] = h0.astype(BF16)
            q_ref[2 * j + 1] = h1.astype(BF16)

        for j in range(4):
            place(qa_ref, j, rope(p_ref[:, j * LANES:(j + 1) * LANES]) * Q_SCALE)
            place(qb_ref, j, rope(rms(p_ref[:, 768 + j * LANES:768 + (j + 1) * LANES], qg_ref[...])) * Q_SCALE)
        ka_ref[...] = rope(p_ref[:, 512:640]).astype(BF16)
        va_ref[...] = p_ref[:, 640:768].astype(BF16)
        kb_ref[...] = rope(rms(p_ref[:, 1280:1408], kg_ref[...])).astype(BF16)
        vb_ref[...] = p_ref[:, 1408:1536].astype(BF16)

    row = pl.BlockSpec((tm, LANES), lambda i: (i, 0))
    qspec = pl.BlockSpec((N_HEADS, tm, LANES), lambda i: (0, i, 0))
    return _call(
        body, name=name, grid=(m // tm,),
        in_specs=[pl.BlockSpec((tm, OFF_GA), lambda i: (i, 0)), row, row, _full((1, LANES)), _full((1, LANES))],
        out_specs=[qspec, row, row, qspec, row, row],
        out_shape=[_sds((N_HEADS, m, LANES), BF16), _sds((m, LANES), BF16), _sds((m, LANES), BF16),
                   _sds((N_HEADS, m, LANES), BF16), _sds((m, LANES), BF16), _sds((m, LANES), BF16)],
        sem=("parallel",),
    )(proj, cos, sin, qg, kg)


def _prep_bwd(dqa, dka, dva, dqb, dkb, dvb, proj, cos, sin, qg, kg, dgl, *, tm, name):
    m = proj.shape[0]

    def body(dqa_ref, dka_ref, dva_ref, dqb_ref, dkb_ref, dvb_ref, p_ref, cos_ref, sin_ref, qg_ref, kg_ref,
             dgl_ref, dp_ref, dqg_ref, dkg_ref):
        i = pl.program_id(0)
        lane = _lane((tm, LANES))
        cosv, sinv = cos_ref[...], sin_ref[...]
        low = lane < HEAD_DIM

        @pl.when(i == 0)
        def _():
            dqg_ref[...] = jnp.zeros(dqg_ref.shape, F32)
            dkg_ref[...] = jnp.zeros(dkg_ref.shape, F32)

        def unrope(d):
            return d * cosv - _rope_partner(d, lane) * sinv

        def unplace(dq_ref, j):
            d0, d1 = dq_ref[2 * j], dq_ref[2 * j + 1]
            if j < 2:
                return jnp.where(low, d0, pltpu.roll(d1, HEAD_DIM, 1))
            return jnp.where(low, pltpu.roll(d0, HEAD_DIM, 1), d1)

        def unrms(dtn, t, g):
            r = lax.rsqrt(_half_mean(t * t, lane) + QK_EPS)
            u = dtn * g
            dt = r * u - t * (r * r * r) * _half_mean(u * t, lane)
            return dt, jnp.sum(dtn * t * r, axis=0, keepdims=True)

        for j in range(4):
            dp_ref[:, j * LANES:(j + 1) * LANES] = (unrope(unplace(dqa_ref, j)) * Q_SCALE).astype(BF16)
            c0 = 768 + j * LANES
            dt, dg = unrms(unrope(unplace(dqb_ref, j)) * Q_SCALE, p_ref[:, c0:c0 + LANES], qg_ref[...])
            dp_ref[:, c0:c0 + LANES] = dt.astype(BF16)
            dqg_ref[:, j * LANES:(j + 1) * LANES] += dg
        dp_ref[:, 512:640] = unrope(dka_ref[...]).astype(BF16)
        dp_ref[:, 640:768] = dva_ref[...].astype(BF16)
        dt, dg = unrms(unrope(dkb_ref[...]), p_ref[:, 1280:1408], kg_ref[...])
        dp_ref[:, 1280:1408] = dt.astype(BF16)
        dkg_ref[...] += dg
        dp_ref[:, 1408:1536] = dvb_ref[...].astype(BF16)
        dp_ref[:, OFF_GA:] = dgl_ref[...]

    row = pl.BlockSpec((tm, LANES), lambda i: (i, 0))
    qspec = pl.BlockSpec((N_HEADS, tm, LANES), lambda i: (0, i, 0))
    return _call(
        body, name=name, grid=(m // tm,),
        in_specs=[qspec, row, row, qspec, row, row, pl.BlockSpec((tm, OFF_GA), lambda i: (i, 0)), row, row,
                  _full((1, LANES)), _full((1, LANES)), pl.BlockSpec((tm, IN_COLS - OFF_GA), lambda i: (i, 0))],
        out_specs=[pl.BlockSpec((tm, IN_COLS), lambda i: (i, 0)), _full((1, 512)), _full((1, LANES))],
        out_shape=[_sds((m, IN_COLS), BF16), _sds((1, 512)), _sds((1, LANES))],
        sem=("arbitrary",),
    )(dqa, dka, dva, dqb, dkb, dvb, proj, cos, sin, qg, kg, dgl)


def _attn_glob_fwd(q, k, v, kc, vc, *, tq, tk):
    nh, s, _ = q.shape
    nc = kc.shape[0]

    def body(q_ref, k_ref, v_ref, kc_ref, vc_ref, o_ref, lse_ref, m_sc, l_sc, acc_sc):
        qv = q_ref[...]
        m_sc[...] = jnp.full(m_sc.shape, NEG, F32)
        l_sc[...] = jnp.zeros(l_sc.shape, F32)
        acc_sc[...] = jnp.zeros(acc_sc.shape, F32)

        def update(kv, vv):
            sc = _dot_nt(qv, kv)
            m_old = m_sc[...]
            m_new = jnp.maximum(m_old, jnp.max(sc, axis=-1, keepdims=True))
            p = jnp.exp(sc - m_new)
            al = jnp.exp(m_old - m_new)
            l_sc[...] = l_sc[...] * al + jnp.sum(p, axis=-1, keepdims=True)
            acc_sc[...] = acc_sc[...] * al + _dot(p.astype(BF16), vv)
            m_sc[...] = m_new

        def loop(c, carry):
            off = pl.multiple_of(c * tk, tk)
            update(k_ref[pl.ds(off, tk), :], v_ref[pl.ds(off, tk), :])
            return carry

        lax.fori_loop(0, s // tk, loop, 0)
        update(kc_ref[...], vc_ref[...])
        o_ref[...] = (acc_sc[...] / l_sc[...]).astype(BF16)
        lse_ref[...] = m_sc[...] + jnp.log(l_sc[...])

    return _call(
        body, name="attn_glob_fwd", grid=(nh, s // tq),
        in_specs=[pl.BlockSpec((None, tq, LANES), lambda h, i: (h, i, 0)),
                  _full((s, LANES)), _full((s, LANES)), _full((nc, LANES)), _full((nc, LANES))],
        out_specs=[pl.BlockSpec((None, tq, LANES), lambda h, i: (h, i, 0)),
                   pl.BlockSpec((None, tq, 1), lambda h, i: (h, i, 0))],
        out_shape=[_sds((nh, s, LANES), BF16), _sds((nh, s, 1))],
        scratch=[pltpu.VMEM((tq, 1), F32), pltpu.VMEM((tq, 1), F32), pltpu.VMEM((tq, LANES), F32)],
        sem=("parallel", "parallel"),
    )(q, k, v, kc, vc)


def _attn_glob_dq(q, do, o, lse, k, v, kc, vc, *, tq, tk):
    nh, s, _ = q.shape
    nc = kc.shape[0]

    def body(q_ref, do_ref, o_ref, lse_ref, k_ref, v_ref, kc_ref, vc_ref, dq_ref, dl_ref, dkc_ref, dvc_ref, dq_sc):
        first = jnp.logical_and(pl.program_id(0) == 0, pl.program_id(1) == 0)

        @pl.when(first)
        def _():
            dkc_ref[...] = jnp.zeros(dkc_ref.shape, F32)
            dvc_ref[...] = jnp.zeros(dvc_ref.shape, F32)

        qv, dov, lse = q_ref[...], do_ref[...], lse_ref[...]
        delta = jnp.sum(dov.astype(F32) * o_ref[...].astype(F32), axis=-1, keepdims=True)
        dl_ref[...] = delta
        dq_sc[...] = jnp.zeros(dq_sc.shape, F32)

        def grads(kv, vv):
            p = jnp.exp(_dot_nt(qv, kv) - lse)
            ds = p * (_dot_nt(dov, vv) - delta)
            return p.astype(BF16), ds.astype(BF16)

        def loop(c, carry):
            off = pl.multiple_of(c * tk, tk)
            kv = k_ref[pl.ds(off, tk), :]
            _, dsb = grads(kv, v_ref[pl.ds(off, tk), :])
            dq_sc[...] += _dot(dsb, kv)
            return carry

        lax.fori_loop(0, s // tk, loop, 0)
        kcv = kc_ref[...]
        pb, dsb = grads(kcv, vc_ref[...])
        dq_ref[...] = dq_sc[...] + _dot(dsb, kcv)
        dkc_ref[...] += _dot_tn(dsb, qv)
        dvc_ref[...] += _dot_tn(pb, dov)

    qs = pl.BlockSpec((None, tq, LANES), lambda h, i: (h, i, 0))
    cs = pl.BlockSpec((None, tq, 1), lambda h, i: (h, i, 0))
    return _call(
        body, name="attn_glob_dq", grid=(nh, s // tq),
        in_specs=[qs, qs, qs, cs, _full((s, LANES)), _full((s, LANES)), _full((nc, LANES)), _full((nc, LANES))],
        out_specs=[qs, cs, _full((nc, LANES)), _full((nc, LANES))],
        out_shape=[_sds((nh, s, LANES)), _sds((nh, s, 1)), _sds((nc, LANES)), _sds((nc, LANES))],
        scratch=[pltpu.VMEM((tq, LANES), F32)],
        sem=("arbitrary", "arbitrary"),
    )(q, do, o, lse, k, v, kc, vc)


def _attn_glob_dkv(q, do, lse_row, dl_row, k, v, *, tq, tk):
    nh, s, _ = q.shape

    def body(k_ref, v_ref, q_ref, do_ref, lse_ref, dl_ref, dk_ref, dv_ref):
        @pl.when(pl.program_id(1) == 0)
        def _():
            dk_ref[...] = jnp.zeros(dk_ref.shape, F32)
            dv_ref[...] = jnp.zeros(dv_ref.shape, F32)

        kv, vv = k_ref[...], v_ref[...]

        def loop(c, carry):
            off = pl.multiple_of(c * tq, tq)
            qv = q_ref[pl.ds(off, tq), :]
            dov = do_ref[pl.ds(off, tq), :]
            pt = jnp.exp(_dot_nt(kv, qv) - lse_ref[:, pl.ds(off, tq)])
            dst = pt * (_dot_nt(vv, dov) - dl_ref[:, pl.ds(off, tq)])
            dk_ref[...] += _dot(dst.astype(BF16), qv)
            dv_ref[...] += _dot(pt.astype(BF16), dov)
            return carry

        lax.fori_loop(0, s // tq, loop, 0)

    ks = pl.BlockSpec((tk, LANES), lambda j, h: (j, 0))
    qs = pl.BlockSpec((None, s, LANES), lambda j, h: (h, 0, 0))
    rs = pl.BlockSpec((None, 1, s), lambda j, h: (h, 0, 0))
    return _call(
        body, name="attn_glob_dkv", grid=(s // tk, nh),
        in_specs=[ks, ks, qs, qs, rs, rs], out_specs=[ks, ks],
        out_shape=[_sds((s, LANES)), _sds((s, LANES))],
        sem=("parallel", "arbitrary"),
    )(k, v, q, do, lse_row, dl_row)


WIN_SPAN = 2 * WINDOW


def _band(rows0, cols0, shape):
    r = rows0 + lax.broadcasted_iota(jnp.int32, shape, 0)
    c = cols0 + lax.broadcasted_iota(jnp.int32, shape, 1)
    return jnp.abs(r - c) <= WINDOW


def _win_start(blk, t, s):
    return pl.multiple_of(jnp.clip(blk * t - WINDOW, 0, s - t - WIN_SPAN), WINDOW)


def _attn_win_fwd(q, k, v, kc, vc, sink, *, tq):
    nh, s, _ = q.shape
    nc = kc.shape[0]
    tw = tq + WIN_SPAN

    def body(sink_ref, q_ref, k_ref, v_ref, kc_ref, vc_ref, o_ref, lse_ref):
        h, i = pl.program_id(0), pl.program_id(1)
        k0 = _win_start(i, tq, s)
        qv = q_ref[...]
        kv, vv = k_ref[pl.ds(k0, tw), :], v_ref[pl.ds(k0, tw), :]
        sc = jnp.where(_band(i * tq, k0, (tq, tw)), _dot_nt(qv, kv), NEG)
        scc = _dot_nt(qv, kc_ref[...])
        snk = sink_ref[h]
        m = jnp.maximum(jnp.maximum(jnp.max(sc, axis=-1, keepdims=True), jnp.max(scc, axis=-1, keepdims=True)), snk)
        p, pc = jnp.exp(sc - m), jnp.exp(scc - m)
        l = jnp.sum(p, axis=-1, keepdims=True) + jnp.sum(pc, axis=-1, keepdims=True) + jnp.exp(snk - m)
        acc = _dot(p.astype(BF16), vv) + _dot(pc.astype(BF16), vc_ref[...])
        o_ref[...] = (acc / l).astype(BF16)
        lse_ref[...] = m + jnp.log(l)

    return _call(
        body, name="attn_win_fwd", grid=(nh, s // tq),
        in_specs=[pl.BlockSpec(memory_space=pltpu.SMEM),
                  pl.BlockSpec((None, tq, LANES), lambda h, i: (h, i, 0)),
                  _full((s, LANES)), _full((s, LANES)), _full((nc, LANES)), _full((nc, LANES))],
        out_specs=[pl.BlockSpec((None, tq, LANES), lambda h, i: (h, i, 0)),
                   pl.BlockSpec((None, tq, 1), lambda h, i: (h, i, 0))],
        out_shape=[_sds((nh, s, LANES), BF16), _sds((nh, s, 1))],
        sem=("parallel", "parallel"),
    )(sink, q, k, v, kc, vc)


def _attn_win_dq(q, do, o, lse, k, v, kc, vc, sink, *, tq):
    nh, s, _ = q.shape
    nc = kc.shape[0]
    tw = tq + WIN_SPAN
    nq = s // tq

    def body(sink_ref, q_ref, do_ref, o_ref, lse_ref, k_ref, v_ref, kc_ref, vc_ref,
             dq_ref, dl_ref, dkc_ref, dvc_ref, dsk_ref):
        h, i = pl.program_id(0), pl.program_id(1)

        @pl.when(jnp.logical_and(h == 0, i == 0))
        def _():
            dkc_ref[...] = jnp.zeros(dkc_ref.shape, F32)
            dvc_ref[...] = jnp.zeros(dvc_ref.shape, F32)

        k0 = _win_start(i, tq, s)
        qv, dov, lse = q_ref[...], do_ref[...], lse_ref[...]
        kv, vv = k_ref[pl.ds(k0, tw), :], v_ref[pl.ds(k0, tw), :]
        kcv, vcv = kc_ref[...], vc_ref[...]
        delta = jnp.sum(dov.astype(F32) * o_ref[...].astype(F32), axis=-1, keepdims=True)
        dl_ref[...] = delta
        p = jnp.where(_band(i * tq, k0, (tq, tw)), jnp.exp(_dot_nt(qv, kv) - lse), 0.0)
        ds = (p * (_dot_nt(dov, vv) - delta)).astype(BF16)
        pc = jnp.exp(_dot_nt(qv, kcv) - lse)
        dsc = (pc * (_dot_nt(dov, vcv) - delta)).astype(BF16)
        dq_ref[...] = _dot(ds, kv) + _dot(dsc, kcv)
        dkc_ref[...] += _dot_tn(dsc, qv)
        dvc_ref[...] += _dot_tn(pc.astype(BF16), dov)
        dsk = -jnp.sum(jnp.exp(sink_ref[h] - lse) * delta)
        dsk_ref[...] = jnp.full(dsk_ref.shape, dsk, F32)

    qs = pl.BlockSpec((None, tq, LANES), lambda h, i: (h, i, 0))
    cs = pl.BlockSpec((None, tq, 1), lambda h, i: (h, i, 0))
    return _call(
        body, name="attn_win_dq", grid=(nh, nq),
        in_specs=[pl.BlockSpec(memory_space=pltpu.SMEM), qs, qs, qs, cs,
                  _full((s, LANES)), _full((s, LANES)), _full((nc, LANES)), _full((nc, LANES))],
        out_specs=[qs, cs, _full((nc, LANES)), _full((nc, LANES)),
                   pl.BlockSpec((None, None, 8, LANES), lambda h, i: (h, i, 0, 0))],
        out_shape=[_sds((nh, s, LANES)), _sds((nh, s, 1)), _sds((nc, LANES)), _sds((nc, LANES)),
                   _sds((nh, nq, 8, LANES))],
        sem=("arbitrary", "arbitrary"),
    )(sink, q, do, o, lse, k, v, kc, vc)


def _attn_win_dkv(q, do, lse_row, dl_row, k, v, *, tk):
    nh, s, _ = q.shape
    tw = tk + WIN_SPAN

    def body(k_ref, v_ref, q_ref, do_ref, lse_ref, dl_ref, dk_ref, dv_ref):
        j = pl.program_id(0)

        @pl.when(pl.program_id(1) == 0)
        def _():
            dk_ref[...] = jnp.zeros(dk_ref.shape, F32)
            dv_ref[...] = jnp.zeros(dv_ref.shape, F32)

        q0 = _win_start(j, tk, s)
        kv, vv = k_ref[...], v_ref[...]
        qv, dov = q_ref[pl.ds(q0, tw), :], do_ref[pl.ds(q0, tw), :]
        pt = jnp.where(_band(j * tk, q0, (tk, tw)), jnp.exp(_dot_nt(kv, qv) - lse_ref[:, pl.ds(q0, tw)]), 0.0)
        dst = pt * (_dot_nt(vv, dov) - dl_ref[:, pl.ds(q0, tw)])
        dk_ref[...] += _dot(dst.astype(BF16), qv)
        dv_ref[...] += _dot(pt.astype(BF16), dov)

    ks = pl.BlockSpec((tk, LANES), lambda j, h: (j, 0))
    qs = pl.BlockSpec((None, s, LANES), lambda j, h: (h, 0, 0))
    rs = pl.BlockSpec((None, 1, s), lambda j, h: (h, 0, 0))
    return _call(
        body, name="attn_win_dkv", grid=(s // tk, nh),
        in_specs=[ks, ks, qs, qs, rs, rs], out_specs=[ks, ks],
        out_shape=[_sds((s, LANES)), _sds((s, LANES))],
        sem=("parallel", "arbitrary"),
    )(k, v, q, do, lse_row, dl_row)


def _ln_fwd(z, g, b):
    mu = jnp.mean(z, axis=-1, keepdims=True)
    zc = z - mu
    r = lax.rsqrt(jnp.mean(zc * zc, axis=-1, keepdims=True) + LN_EPS)
    return zc * r * g + b, mu, r


def _ln_bwd(dy, xhat, r, g):
    dxh = dy * g
    return r * (dxh - jnp.mean(dxh, axis=-1, keepdims=True) - xhat * jnp.mean(dxh * xhat, axis=-1, keepdims=True))


def _heads_matmul(o_ref, w_ref):
    acc = _dot(o_ref[0], w_ref[0])
    for h in range(1, N_HEADS):
        acc += _dot(o_ref[h], w_ref[h])
    return acc


def _gate_specs(tm):
    return [pl.BlockSpec((tm, 512), functools.partial(lambda i, b: (i, b), b=OFF_GA // 512 + b)) for b in range(4)]


def _merge_fwd(oa, ob, proj, x, gate1, wba, wbb, w_out, ln_g, ln_b, *, tm):
    s = x.shape[0]

    def body(oa_ref, ob_ref, g0, g1, g2, g3, x_ref, gt_ref, wba_ref, wbb_ref, wo_ref, lg_ref, lb_ref,
             x1_ref, y_ref, mu_ref, r_ref):
        ga = _sigmoid(jnp.concatenate([g0[...], g1[...]], axis=1))
        gb = _sigmoid(jnp.concatenate([g2[...], g3[...]], axis=1))
        merged = ga * _heads_matmul(oa_ref, wba_ref) + gb * _heads_matmul(ob_ref, wbb_ref)
        y = _dot(merged.astype(BF16), wo_ref[...])
        x1, mu, r = _ln_fwd(ALPHA * x_ref[...] + gt_ref[...] * y, lg_ref[...], lb_ref[...])
        x1_ref[...] = x1
        y_ref[...] = y
        mu_ref[...] = mu
        r_ref[...] = r

    hs = pl.BlockSpec((N_HEADS, tm, LANES), lambda i: (0, i, 0))
    row = pl.BlockSpec((tm, D_MODEL), lambda i: (i, 0))
    col = pl.BlockSpec((tm, 1), lambda i: (i, 0))
    vec = _full((1, D_MODEL))
    wh = _full((N_HEADS, LANES, D_MODEL))
    return _call(
        body, name="merge_fwd", grid=(s // tm,),
        in_specs=[hs, hs, *_gate_specs(tm), row, vec, wh, wh, _full((D_MODEL, D_MODEL)), vec, vec],
        out_specs=[row, row, col, col],
        out_shape=[_sds((s, D_MODEL)), _sds((s, D_MODEL)), _sds((s, 1)), _sds((s, 1))],
        sem=("parallel",),
    )(oa, ob, proj, proj, proj, proj, x, gate1, wba, wbb, w_out, ln_g, ln_b)


def _merge_bwd(dy, oa, ob, proj, wba, wbb, w_out, *, tm):
    s = dy.shape[0]

    def body(dy_ref, oa_ref, ob_ref, g0, g1, g2, g3, wba_ref, wbb_ref, wo_ref,
             dgl_ref, doa_ref, dob_ref, dpa_ref, dpb_ref, mg_ref):
        dm = _dot_nt(dy_ref[...], wo_ref[...])
        ga = _sigmoid(jnp.concatenate([g0[...], g1[...]], axis=1))
        gb = _sigmoid(jnp.concatenate([g2[...], g3[...]], axis=1))
        pa, pb = _heads_matmul(oa_ref, wba_ref), _heads_matmul(ob_ref, wbb_ref)
        mg_ref[...] = (ga * pa + gb * pb).astype(BF16)
        dgl_ref[:, :D_MODEL] = (dm * pa * ga * (1.0 - ga)).astype(BF16)
        dgl_ref[:, D_MODEL:] = (dm * pb * gb * (1.0 - gb)).astype(BF16)
        dpa, dpb = (dm * ga).astype(BF16), (dm * gb).astype(BF16)
        dpa_ref[...] = dpa
        dpb_ref[...] = dpb
        for h in range(N_HEADS):
            doa_ref[h] = _dot_nt(dpa, wba_ref[h]).astype(BF16)
            dob_ref[h] = _dot_nt(dpb, wbb_ref[h]).astype(BF16)

    hs = pl.BlockSpec((N_HEADS, tm, LANES), lambda i: (0, i, 0))
    row = pl.BlockSpec((tm, D_MODEL), lambda i: (i, 0))
    wh = _full((N_HEADS, LANES, D_MODEL))
    return _call(
        body, name="merge_bwd", grid=(s // tm,),
        in_specs=[row, hs, hs, *_gate_specs(tm), wh, wh, _full((D_MODEL, D_MODEL))],
        out_specs=[pl.BlockSpec((tm, 2 * D_MODEL), lambda i: (i, 0)), hs, hs, row, row, row],
        out_shape=[_sds((s, 2 * D_MODEL), BF16), _sds((N_HEADS, s, LANES), BF16), _sds((N_HEADS, s, LANES), BF16),
                   _sds((s, D_MODEL), BF16), _sds((s, D_MODEL), BF16), _sds((s, D_MODEL), BF16)],
        sem=("parallel",),
    )(dy, oa, ob, proj, proj, proj, proj, wba, wbb, w_out)


FF_TC = 256


def _shift_rows(t, prev_row, next_row):
    n = t.shape[0]
    r = lax.broadcasted_iota(jnp.int32, t.shape, 0)
    up = jnp.where(r == 0, prev_row, pltpu.roll(t, 1, 0))
    dn = jnp.where(r == n - 1, next_row, pltpu.roll(t, n - 1, 0))
    return up, dn


def _halo_specs(tm, s, tc):
    nb8 = s // 8
    main = pl.BlockSpec((2, tm, tc), lambda j, i: (0, i, j))
    prev = pl.BlockSpec((2, 8, tc), lambda j, i: (0, jnp.maximum(i * (tm // 8) - 1, 0), j))
    nxt = pl.BlockSpec((2, 8, tc), lambda j, i: (0, jnp.minimum((i + 1) * (tm // 8), nb8 - 1), j))
    return main, prev, nxt


def _halo_rows(prev_ref, next_ref, half, i, n_i):
    prev_row = jnp.where(i == 0, 0.0, prev_ref[half, 7:8, :].astype(F32))
    next_row = jnp.where(i == n_i - 1, 0.0, next_ref[half, 0:1, :].astype(F32))
    return prev_row, next_row


def _conv(t, prev_row, next_row, w, b):
    up, dn = _shift_rows(t, prev_row, next_row)
    return w[0:1, :] * up + w[1:2, :] * t + w[2:3, :] * dn + b


def _ffn_act_fwd(u, cw, cb, *, tm):
    _, s, ff = u.shape
    n_i = s // tm

    def body(u_ref, up_ref, un_ref, cw_ref, cb_ref, a_ref):
        i = pl.program_id(1)
        gc = _conv(u_ref[0], *_halo_rows(up_ref, un_ref, 0, i, n_i), cw_ref[0], cb_ref[0])
        vc = _conv(u_ref[1], *_halo_rows(up_ref, un_ref, 1, i, n_i), cw_ref[1], cb_ref[1])
        a_ref[...] = (gc * _sigmoid(gc) * vc).astype(BF16)

    main, prev, nxt = _halo_specs(tm, s, FF_TC)
    return _call(
        body, name="ffn_act_fwd", grid=(ff // FF_TC, n_i),
        in_specs=[main, prev, nxt, pl.BlockSpec((2, 3, FF_TC), lambda j, i: (0, 0, j)),
                  pl.BlockSpec((2, 1, FF_TC), lambda j, i: (0, 0, j))],
        out_specs=pl.BlockSpec((tm, FF_TC), lambda j, i: (i, j)),
        out_shape=_sds((s, ff), BF16), sem=("parallel", "parallel"),
    )(u, u, u, cw, cb)


def _ffn_act_bwd(dy2, w_down, u, cw, cb, *, tm):
    _, s, ff = u.shape
    n_i = s // tm

    def body(dy_ref, wd_ref, u_ref, up_ref, un_ref, cw_ref, cb_ref, dc_ref, dcw_ref, dcb_ref):
        i = pl.program_id(1)

        @pl.when(i == 0)
        def _():
            dcw_ref[...] = jnp.zeros(dcw_ref.shape, F32)
            dcb_ref[...] = jnp.zeros(dcb_ref.shape, F32)

        da = _dot_nt(dy_ref[...], wd_ref[...])
        ug, uv = u_ref[0], u_ref[1]
        ugp, ugn = _shift_rows(ug, *_halo_rows(up_ref, un_ref, 0, i, n_i))
        uvp, uvn = _shift_rows(uv, *_halo_rows(up_ref, un_ref, 1, i, n_i))
        wg, wv = cw_ref[0], cw_ref[1]
        gc = wg[0:1, :] * ugp + wg[1:2, :] * ug + wg[2:3, :] * ugn + cb_ref[0]
        vc = wv[0:1, :] * uvp + wv[1:2, :] * uv + wv[2:3, :] * uvn + cb_ref[1]
        sg = _sigmoid(gc)
        dg = da * vc * sg * (1.0 + gc * (1.0 - sg))
        dv = da * gc * sg
        dc_ref[0] = dg
        dc_ref[1] = dv
        for half, (d, taps) in enumerate(((dg, (ugp, ug, ugn)), (dv, (uvp, uv, uvn)))):
            for tap in range(3):
                dcw_ref[half, tap:tap + 1, :] += jnp.sum(d * taps[tap], axis=0, keepdims=True)
            dcb_ref[half] += jnp.sum(d, axis=0, keepdims=True)

    main, prev, nxt = _halo_specs(tm, s, FF_TC)
    return _call(
        body, name="ffn_act_bwd", grid=(ff // FF_TC, n_i),
        in_specs=[pl.BlockSpec((tm, D_MODEL), lambda j, i: (i, 0)), pl.BlockSpec((FF_TC, D_MODEL), lambda j, i: (j, 0)),
                  main, prev, nxt, pl.BlockSpec((2, 3, FF_TC), lambda j, i: (0, 0, j)),
                  pl.BlockSpec((2, 1, FF_TC), lambda j, i: (0, 0, j))],
        out_specs=[main, pl.BlockSpec((2, 3, FF_TC), lambda j, i: (0, 0, j)),
                   pl.BlockSpec((2, 1, FF_TC), lambda j, i: (0, 0, j))],
        out_shape=[_sds((2, s, ff)), _sds((2, 3, ff)), _sds((2, 1, ff))],
        sem=("parallel", "arbitrary"),
    )(dy2, w_down, u, u, u, cw, cb)


def _conv_bwd_input(dc, cw, *, tm):
    _, s, ff = dc.shape
    n_i = s // tm

    def body(d_ref, dp_ref, dn_ref, cw_ref, du_ref):
        i = pl.program_id(1)
        for half in range(2):
            up, dn = _shift_rows(d_ref[half], *_halo_rows(dp_ref, dn_ref, half, i, n_i))
            w = cw_ref[half]
            du_ref[half] = (w[0:1, :] * dn + w[1:2, :] * d_ref[half] + w[2:3, :] * up).astype(BF16)

    main, prev, nxt = _halo_specs(tm, s, FF_TC)
    return _call(
        body, name="conv_bwd_input", grid=(ff // FF_TC, n_i),
        in_specs=[main, prev, nxt, pl.BlockSpec((2, 3, FF_TC), lambda j, i: (0, 0, j))],
        out_specs=main, out_shape=_sds((2, s, ff), BF16), sem=("parallel", "parallel"),
    )(dc, dc, dc, cw)


def _ffn_down_loss(a, w_down, x1, target, gate2, ln_g, ln_b, *, tm):
    s, ff = a.shape
    n_i = s // tm

    def body(a_ref, wd_ref, x1_ref, tg_ref, gt_ref, lg_ref, lb_ref, ls_ref, dy_ref, dx_ref, dg_ref, db_ref, dgt_ref):
        @pl.when(pl.program_id(0) == 0)
        def _():
            dg_ref[...] = jnp.zeros(dg_ref.shape, F32)
            db_ref[...] = jnp.zeros(db_ref.shape, F32)
            dgt_ref[...] = jnp.zeros(dgt_ref.shape, F32)

        y2 = _dot(a_ref[...], wd_ref[...])
        z = ALPHA * x1_ref[...] + gt_ref[...] * y2
        mu = jnp.mean(z, axis=-1, keepdims=True)
        zc = z - mu
        r = lax.rsqrt(jnp.mean(zc * zc, axis=-1, keepdims=True) + LN_EPS)
        xhat = zc * r
        diff = xhat * lg_ref[...] + lb_ref[...] - tg_ref[...]
        ls_ref[...] = jnp.full(ls_ref.shape, 0.5 / D_MODEL * jnp.sum(diff * diff), F32)
        dx2 = diff * (1.0 / D_MODEL)
        dg_ref[...] += jnp.sum(dx2 * xhat, axis=0, keepdims=True)
        db_ref[...] += jnp.sum(dx2, axis=0, keepdims=True)
        dz = _ln_bwd(dx2, xhat, r, lg_ref[...])
        dgt_ref[...] += jnp.sum(dz * y2, axis=0, keepdims=True)
        dy_ref[...] = (gt_ref[...] * dz).astype(BF16)
        dx_ref[...] = ALPHA * dz

    row = pl.BlockSpec((tm, D_MODEL), lambda i: (i, 0))
    vec = _full((1, D_MODEL))
    return _call(
        body, name="ffn_down_loss", grid=(n_i,),
        in_specs=[pl.BlockSpec((tm, ff), lambda i: (i, 0)), _full((ff, D_MODEL)), row, row, vec, vec, vec],
        out_specs=[pl.BlockSpec((None, 8, LANES), lambda i: (i, 0, 0)), row, row, vec, vec, vec],
        out_shape=[_sds((n_i, 8, LANES)), _sds((s, D_MODEL), BF16), _sds((s, D_MODEL)),
                   _sds((1, D_MODEL)), _sds((1, D_MODEL)), _sds((1, D_MODEL))],
        sem=("arbitrary",),
    )(a, w_down, x1, target, gate2, ln_g, ln_b)


def _ffn_up_bwd(du, wup4, dx1a, x1, scale2, x, y, mu1, r1, gate1, ln_g, *, tm):
    s = x.shape[0]
    nb, _, ns = wup4.shape

    def body(du_ref, w_ref, dxa_ref, x1_ref, sc_ref, x_ref, y_ref, mu_ref, r_ref, gt_ref, lg_ref,
             dxo_ref, dy_ref, dsc_ref, dsh_ref, dg_ref, db_ref, dgt_ref, acc):
        i, k = pl.program_id(0), pl.program_id(1)

        @pl.when(jnp.logical_and(i == 0, k == 0))
        def _():
            for ref in (dsc_ref, dsh_ref, dg_ref, db_ref, dgt_ref):
                ref[...] = jnp.zeros(ref.shape, F32)

        @pl.when(k == 0)
        def _():
            acc[...] = jnp.zeros(acc.shape, F32)

        acc[...] += _dot_nt(du_ref[...], w_ref[...])

        @pl.when(k == nb - 1)
        def _():
            dh = acc[...]
            x1 = x1_ref[...]
            dsc_ref[...] += jnp.sum(dh * x1, axis=0, keepdims=True)
            dsh_ref[...] += jnp.sum(dh, axis=0, keepdims=True)
            dx1 = dxa_ref[...] + dh * (1.0 + sc_ref[...])
            yv = y_ref[...]
            xhat = (ALPHA * x_ref[...] + gt_ref[...] * yv - mu_ref[...]) * r_ref[...]
            dg_ref[...] += jnp.sum(dx1 * xhat, axis=0, keepdims=True)
            db_ref[...] += jnp.sum(dx1, axis=0, keepdims=True)
            dz = _ln_bwd(dx1, xhat, r_ref[...], lg_ref[...])
            dgt_ref[...] += jnp.sum(dz * yv, axis=0, keepdims=True)
            dy_ref[...] = (gt_ref[...] * dz).astype(BF16)
            dxo_ref[...] = ALPHA * dz

    row = pl.BlockSpec((tm, D_MODEL), lambda i, k: (i, 0))
    col = pl.BlockSpec((tm, 1), lambda i, k: (i, 0))
    vec = _full((1, D_MODEL))
    return _call(
        body, name="ffn_up_bwd", grid=(s // tm, nb),
        in_specs=[pl.BlockSpec((None, tm, ns), lambda i, k: (k // 2, i, k % 2)),
                  pl.BlockSpec((None, D_MODEL, ns), lambda i, k: (k, 0, 0)),
                  row, row, vec, row, row, col, col, vec, vec],
        out_specs=[row, row, vec, vec, vec, vec, vec],
        out_shape=[_sds((s, D_MODEL)), _sds((s, D_MODEL), BF16)] + [_sds((1, D_MODEL))] * 5,
        scratch=[pltpu.VMEM((tm, D_MODEL), F32)],
        sem=("arbitrary", "arbitrary"),
    )(du, wup4, dx1a, x1, scale2, x, y, mu1, r1, gate1, ln_g)


def _mm_nt4_mod_bwd(dp, w4, dxa, x, scale, *, tm, name):
    m = x.shape[0]
    nb, kdim, ns = w4.shape

    def body(dp_ref, w_ref, dxa_ref, x_ref, sc_ref, dx_ref, dsc_ref, dsh_ref, acc):
        i, k = pl.program_id(0), pl.program_id(1)

        @pl.when(jnp.logical_and(i == 0, k == 0))
        def _():
            dsc_ref[...] = jnp.zeros(dsc_ref.shape, F32)
            dsh_ref[...] = jnp.zeros(dsh_ref.shape, F32)

        @pl.when(k == 0)
        def _():
            acc[...] = jnp.zeros(acc.shape, F32)

        acc[...] += _dot_nt(dp_ref[...], w_ref[...])

        @pl.when(k == nb - 1)
        def _():
            dh = acc[...]
            dsc_ref[...] += jnp.sum(dh * x_ref[...], axis=0, keepdims=True)
            dsh_ref[...] += jnp.sum(dh, axis=0, keepdims=True)
            dx_ref[...] = dxa_ref[...] + dh * (1.0 + sc_ref[...])

    row = pl.BlockSpec((tm, kdim), lambda i, k: (i, 0))
    vec = _full((1, kdim))
    return _call(
        body, name=name, grid=(m // tm, nb),
        in_specs=[pl.BlockSpec((tm, ns), lambda i, k: (i, k)), pl.BlockSpec((None, kdim, ns), lambda i, k: (k, 0, 0)),
                  row, row, vec],
        out_specs=[row, vec, vec],
        out_shape=[_sds((m, kdim)), _sds((1, kdim)), _sds((1, kdim))],
        scratch=[pltpu.VMEM((tm, kdim), F32)],
        sem=("arbitrary", "arbitrary"),
    )(dp, w4, dxa, x, scale)


def _pad_heads_w(w):
    w8 = w.reshape(N_HEADS, HEAD_DIM, w.shape[-1])
    z = jnp.zeros_like(w8)
    first = (jnp.arange(N_HEADS) < N_HEADS // N_KV)[:, None, None]
    return jnp.where(first, jnp.concatenate([w8, z], axis=1), jnp.concatenate([z, w8], axis=1))


def _unpad_heads_w(g):
    first = (jnp.arange(N_HEADS) < N_HEADS // N_KV)[:, None, None]
    return jnp.where(first, g[:, :HEAD_DIM], g[:, HEAD_DIM:]).reshape(N_HEADS * HEAD_DIM, g.shape[-1])


def _rep8(a):
    return jnp.broadcast_to(a.reshape(1, -1), (8, a.size))


def _first_row(a):
    r8 = _rep8(a)
    return jnp.where(lax.broadcasted_iota(jnp.int32, r8.shape, 0) == 0, r8, 0.0)


def _to_blocks4(w):
    k, n = w.shape
    return w.reshape(k, N_CHIPS, n // N_CHIPS).transpose(1, 0, 2)


def _local_step(x, c, ctx, c_ctx, wmod4, b_mod, win4, b_in, sink, qn, kn, wba, wbb, w_out, ln1_g, ln1_b,
                wup4, cw, cb, w_down, ln2_g, ln2_b, target):
    s, nc = x.shape[0], ctx.shape[0]
    tm = min(512, s)
    tm2 = min(256, s)
    zvec = jnp.zeros((1, D_MODEL), F32)

    cc = jnp.concatenate([_rep8(c), _rep8(c_ctx)], axis=0)
    mods = _mm_nn4(cc, zvec, zvec, wmod4, b_mod, mode="silu", split_out=False, out_dtype=F32, tm=16, name="mod_vectors")
    shift1, scale1, gate1, shift2, scale2, gate2 = [mods[0:1, i * D_MODEL:(i + 1) * D_MODEL] for i in range(6)]
    shift_c, scale_c = mods[8:9, :D_MODEL], mods[8:9, D_MODEL:2 * D_MODEL]

    cos, sin = _rope_tables(s)
    cos_c, sin_c = jnp.ones((nc, LANES), F32), jnp.zeros((nc, LANES), F32)
    qg, kg = jnp.tile(qn, (1, 2)), jnp.tile(kn, (1, 2))

    proj_c = _mm_nn4(ctx, shift_c, scale_c, win4, b_in, mode="modulate", split_out=False, out_dtype=F32, tm=nc,
                     name="in_proj_ctx")
    _, kac, vac, _, kbc, vbc = _prep(proj_c, cos_c, sin_c, qg, kg, tm=nc, name="prep_ctx")
    proj = _mm_nn4(x, shift1, scale1, win4, b_in, mode="modulate", split_out=False, out_dtype=F32, tm=tm, name="in_proj")
    qa, ka, va, qb, kb, vb = _prep(proj, cos, sin, qg, kg, tm=tm, name="prep")
    oa, lse_a = _attn_win_fwd(qa, ka, va, kac, vac, sink, tq=tm2)
    ob, lse_b = _attn_glob_fwd(qb, kb, vb, kbc, vbc, tq=tm, tk=tm)
    wba_p, wbb_p = _pad_heads_w(wba), _pad_heads_w(wbb)
    x1, y, mu1, r1 = _merge_fwd(oa, ob, proj, x, gate1, wba_p, wbb_p, w_out, ln1_g, ln1_b, tm=tm2)
    u = _mm_nn4(x1, shift2, scale2, wup4, jnp.zeros((1, 2 * D_FF), F32), mode="modulate", split_out=True,
                out_dtype=F32, tm=tm, name="ffn_up")
    cw2 = cw.reshape(3, 2, D_FF).transpose(1, 0, 2)
    cb2 = cb.reshape(2, 1, D_FF)
    a = _ffn_act_fwd(u, cw2, cb2, tm=tm)
    ls, dy2, dx1a, dln2_g, dln2_b, dgate2 = _ffn_down_loss(a, w_down, x1, target, gate2, ln2_g, ln2_b, tm=tm2)
    loss = jnp.sum(ls[:, 0, 0])

    n_s = s // tm
    dw_down = _mm_tn(a, dy2, a_spec=pl.BlockSpec((tm, D_FF), lambda t: (t, 0)),
                     b_spec=pl.BlockSpec((tm, D_MODEL), lambda t: (t, 0)), grid=(n_s,),
                     out_shape=_sds((D_FF, D_MODEL)), out_spec=_full((D_FF, D_MODEL)), name="dw_down")
    dc, dcw2, dcb2 = _ffn_act_bwd(dy2, w_down, u, cw2, cb2, tm=tm)
    du = _conv_bwd_input(dc, cw2, tm=tm)
    dxz1, dy, dscale2, dshift2, dln1_g, dln1_b, dgate1 = _ffn_up_bwd(
        du, wup4, dx1a, x1, scale2, x, y, mu1, r1, gate1, ln1_g, tm=tm2)
    ns_up = wup4.shape[-1]
    dw_up4 = _mm_tn(x1, du, a_spec=pl.BlockSpec((tm, D_MODEL), lambda k, t: (t, 0)),
                    b_spec=pl.BlockSpec((None, tm, ns_up), lambda k, t: (k // 2, t, k % 2)), grid=(N_CHIPS, n_s),
                    out_shape=_sds((N_CHIPS, D_MODEL, ns_up)),
                    out_spec=pl.BlockSpec((None, D_MODEL, ns_up), lambda k, t: (k, 0, 0)),
                    mod=(shift2, scale2), name="dw_up")

    dgl, doa, dob, dpa, dpb, merged = _merge_bwd(dy, oa, ob, proj, wba_p, wbb_p, w_out, tm=tm2)
    rowspec = pl.BlockSpec((tm, D_MODEL), lambda t: (t, 0))
    dw_out = _mm_tn(merged, dy, a_spec=rowspec, b_spec=rowspec, grid=(n_s,), out_shape=_sds((D_MODEL, D_MODEL)),
                    out_spec=_full((D_MODEL, D_MODEL)), name="dw_out")
    hspec = dict(a_spec=pl.BlockSpec((None, tm, LANES), lambda h, t: (h, t, 0)),
                 b_spec=pl.BlockSpec((tm, D_MODEL), lambda h, t: (t, 0)), grid=(N_HEADS, n_s),
                 out_shape=_sds((N_HEADS, LANES, D_MODEL)),
                 out_spec=pl.BlockSpec((None, LANES, D_MODEL), lambda h, t: (h, 0, 0)))
    dwba = _unpad_heads_w(_mm_tn(oa, dpa, name="dw_branch_a", **hspec))
    dwbb = _unpad_heads_w(_mm_tn(ob, dpb, name="dw_branch_b", **hspec))

    dqa, dla, dkac, dvac, dsk = _attn_win_dq(qa, doa, oa, lse_a, ka, va, kac, vac, sink, tq=tm2)
    dka, dva = _attn_win_dkv(qa, doa, lse_a.reshape(N_HEADS, 1, s), dla.reshape(N_HEADS, 1, s), ka, va, tk=tm2)
    dqb, dlb, dkbc, dvbc = _attn_glob_dq(qb, dob, ob, lse_b, kb, vb, kbc, vbc, tq=tm, tk=tm)
    dkb, dvb = _attn_glob_dkv(qb, dob, lse_b.reshape(N_HEADS, 1, s), dlb.reshape(N_HEADS, 1, s), kb, vb,
                              tq=tm, tk=min(1024, s))
    dsink = jnp.sum(dsk[:, :, 0, 0], axis=1)

    dproj, dqg, dkg = _prep_bwd(dqa, dka, dva, dqb, dkb, dvb, proj, cos, sin, qg, kg, dgl, tm=tm, name="prep_bwd")
    grad_x, dscale1, dshift1 = _mm_nt4_mod_bwd(dproj, win4, dxz1, x, scale1, tm=tm, name="in_proj_bwd")
    ns_in = win4.shape[-1]
    win_spec = dict(b_spec=pl.BlockSpec((None, None, ns_in), lambda k, t: (0, 0, k)),
                    out_shape=_sds((N_CHIPS, D_MODEL, ns_in)),
                    out_spec=pl.BlockSpec((None, D_MODEL, ns_in), lambda k, t: (k, 0, 0)),
                    colsum_spec=pl.BlockSpec((8, ns_in), lambda k, t: (0, k)), colsum_shape=_sds((8, IN_COLS)))
    win_spec["b_spec"] = pl.BlockSpec((tm, ns_in), lambda k, t: (t, k))
    dw_in4, db_in = _mm_tn(x, dproj, a_spec=pl.BlockSpec((tm, D_MODEL), lambda k, t: (t, 0)), grid=(N_CHIPS, n_s),
                           mod=(shift1, scale1), name="dw_in", **win_spec)

    zq = jnp.zeros((N_HEADS, nc, LANES), F32)
    dproj_c, _, dkg_c = _prep_bwd(zq, dkac, dvac, zq, dkbc, dvbc, proj_c, cos_c, sin_c, qg, kg,
                                  jnp.zeros((nc, IN_COLS - OFF_GA), BF16), tm=nc, name="prep_bwd_ctx")
    _, dscale_c, dshift_c = _mm_nt4_mod_bwd(dproj_c, win4, jnp.zeros((nc, D_MODEL), F32), ctx, scale_c, tm=nc,
                                            name="in_proj_bwd_ctx")
    win_spec["b_spec"] = pl.BlockSpec((nc, ns_in), lambda k, t: (t, k))
    dw_in4, db_in_c = _mm_tn(ctx, dproj_c, a_spec=pl.BlockSpec((nc, D_MODEL), lambda k, t: (t, 0)), grid=(N_CHIPS, 1),
                             mod=(shift_c, scale_c), init=dw_in4, name="dw_in_ctx", **win_spec)

    dmod = jnp.concatenate([dshift1, dscale1, dgate1, dshift2, dscale2, dgate2], axis=1)
    dmodc = jnp.concatenate([dshift_c, dscale_c], axis=1)
    dmodc_pad = jnp.concatenate([dmodc, jnp.zeros((1, 4 * D_MODEL), F32)], axis=1)
    dmodc8 = _first_row(dmodc_pad).astype(BF16)
    z8 = jnp.zeros((8, D_MODEL), F32)
    dsilu_c, _, _ = _mm_nt4_mod_bwd(dmodc8, wmod4, z8, z8, zvec, tm=8, name="c_ctx_bwd")
    sg = _sigmoid(c_ctx)
    dc_ctx = dsilu_c[0:1] * sg * (1.0 + c_ctx * (1.0 - sg))

    dqn = jnp.sum(dqg.reshape(N_HEADS, HEAD_DIM), axis=0, keepdims=True)
    dkn = jnp.sum((dkg + dkg_c).reshape(N_KV, HEAD_DIM), axis=0, keepdims=True)
    grads = dict(
        w_in4=dw_in4, b_in=db_in[0:1] + db_in_c[0:1], sink=dsink, qn=dqn, kn=dkn, wba=dwba, wbb=dwbb, w_out=dw_out,
        ln1_g=dln1_g, ln1_b=dln1_b, w_up4=dw_up4, conv_w=dcw2.transpose(1, 0, 2).reshape(3, 2 * D_FF),
        conv_b=dcb2.reshape(1, 2 * D_FF), w_down=dw_down, ln2_g=dln2_g, ln2_b=dln2_b,
        c_ctx=dc_ctx, dmod=dmod, dmodc=dmodc)
    return loss, grad_x, grads


ANY = pl.BlockSpec(memory_space=pl.ANY)


def _mesh_pos():
    return lax.axis_index("x"), lax.axis_index("y"), lax.axis_index("c")


def _other_chips(x, y):
    return [(1 - x, y), (x, 1 - y), (1 - x, 1 - y)]


def _remote(src, dst, send, recv, dev):
    return pltpu.make_async_remote_copy(src_ref=src, dst_ref=dst, send_sem=send, recv_sem=recv, device_id=dev,
                                        device_id_type=MESH)


def _gather_shards(arrs, small):
    na = len(arrs)
    halves = [a.shape[0] // 2 for a in arrs]

    def body(*refs):
        ins, small_ref = refs[:na], refs[na]
        outs, small_out = refs[na + 1:2 * na + 1], refs[2 * na + 1]
        send, recv, loc = refs[2 * na + 2:]
        x, y, c = _mesh_pos()
        me = 2 * x + y
        chips = _other_chips(x, y)

        def half(a, cc):
            return pl.ds(cc * halves[a], halves[a])

        local = [pltpu.make_async_copy(ins[a], outs[a].at[me], loc.at[a]) for a in range(na)]
        local.append(pltpu.make_async_copy(small_ref, small_out.at[me], loc.at[na]))
        for cp in local:
            cp.start()
        sends = []
        for j, chip in enumerate(chips):
            for a in range(na):
                sends.append(_remote(ins[a].at[half(a, c)], outs[a].at[me, half(a, c)], send.at[a, j], recv.at[a, j],
                                     (*chip, c)))
            sends.append(_remote(small_ref, small_out.at[me], send.at[na, j], recv.at[na, j], (*chip, c)))
        for cp in sends:
            cp.start()
        for j, chip in enumerate(chips):
            kj = 2 * chip[0] + chip[1]
            for a in range(na):
                landed = outs[a].at[kj, half(a, c)]
                _remote(landed, landed, send.at[a, j], recv.at[a, j], (*chip, c)).wait_recv()
                fwd = _remote(landed, landed, send.at[a, 3 + j], recv.at[a, 3 + j], (x, y, 1 - c))
                fwd.start()
                sends.append(fwd)
            _remote(small_ref, small_out.at[kj], send.at[na, j], recv.at[na, j], (*chip, c)).wait_recv()
        for j, chip in enumerate(chips):
            kj = 2 * chip[0] + chip[1]
            for a in range(na):
                other = outs[a].at[kj, half(a, 1 - c)]
                _remote(other, other, send.at[a, 3 + j], recv.at[a, 3 + j], (x, y, 1 - c)).wait_recv()
        for cp in sends:
            cp.wait_send()
        for cp in local:
            cp.wait()

    out_shape = [_sds((N_CHIPS,) + a.shape, a.dtype) for a in arrs] + [_sds((N_CHIPS,) + small.shape, small.dtype)]
    return pl.pallas_call(
        body, name="gather_shards", in_specs=[ANY] * (na + 1), out_specs=[ANY] * (na + 1), out_shape=out_shape,
        scratch_shapes=[pltpu.SemaphoreType.DMA((na + 1, 6)), pltpu.SemaphoreType.DMA((na + 1, 6)),
                        pltpu.SemaphoreType.DMA((na + 1,))],
    )(*arrs, small)


def _allgather_rows(v):
    r, n = v.shape

    def body(v_ref, out_ref, send, recv, loc):
        x, y, c = _mesh_pos()
        me, sibling = (x, y, c), (x, y, 1 - c)
        chips = _other_chips(x, y)

        def rows(px, py, pc):
            return out_ref.at[4 * px + 2 * py + pc]

        def copy(k, block, to, src=None):
            return _remote(rows(*block) if src is None else src, rows(*block), send.at[k], recv.at[k], to)

        mine = pltpu.make_async_copy(v_ref, rows(*me), loc)
        mine.start()
        first = [copy(0, me, sibling, src=v_ref)] + [copy(1 + j, me, (*chip, c), src=v_ref) for j, chip in enumerate(chips)]
        for cp in first:
            cp.start()
        passed = [copy(4 + j, (*chip, c), sibling) for j, chip in enumerate(chips)]
        for j, chip in enumerate(chips):
            copy(1 + j, (*chip, c), me).wait_recv()
            passed[j].start()
        copy(0, sibling, me).wait_recv()
        for j, chip in enumerate(chips):
            copy(4 + j, (*chip, 1 - c), me).wait_recv()
        for cp in first + passed:
            cp.wait_send()
        mine.wait()

    return pl.pallas_call(
        body, name="allgather_rows", in_specs=[pl.BlockSpec(memory_space=pltpu.VMEM)],
        out_specs=pl.BlockSpec(memory_space=pltpu.VMEM), out_shape=_sds((N_DEV, r, n), v.dtype),
        scratch_shapes=[pltpu.SemaphoreType.DMA((7,)), pltpu.SemaphoreType.DMA((7,)), pltpu.SemaphoreType.DMA],
    )(v)


def _swap_other_half(g):
    nb, r, n = g.shape
    rh = r // 2

    def body(g_ref, out_ref, send, recv):
        x, y, c = _mesh_pos()
        cp = _remote(g_ref.at[:, pl.ds((1 - c) * rh, rh), :], out_ref, send, recv, (x, y, 1 - c))
        cp.start()
        cp.wait()

    return pl.pallas_call(
        body, name="swap_other_half", in_specs=[ANY], out_specs=ANY, out_shape=_sds((nb, rh, n), g.dtype),
        scratch_shapes=[pltpu.SemaphoreType.DMA, pltpu.SemaphoreType.DMA],
    )(g)


def _scatter_to_chips(p):
    def body(p_ref, out_ref, send, recv, loc):
        x, y, c = _mesh_pos()
        me = 2 * x + y
        chips = _other_chips(x, y)
        mine = pltpu.make_async_copy(p_ref.at[me], out_ref.at[me], loc)
        mine.start()
        sends = [_remote(p_ref.at[2 * chip[0] + chip[1]], out_ref.at[me], send.at[j], recv.at[j], (*chip, c))
                 for j, chip in enumerate(chips)]
        for cp in sends:
            cp.start()
        for j, chip in enumerate(chips):
            kj = 2 * chip[0] + chip[1]
            _remote(p_ref.at[kj], out_ref.at[kj], send.at[j], recv.at[j], (*chip, c)).wait_recv()
        for cp in sends:
            cp.wait_send()
        mine.wait()

    return pl.pallas_call(
        body, name="scatter_to_chips", in_specs=[ANY], out_specs=ANY, out_shape=_sds(p.shape, p.dtype),
        scratch_shapes=[pltpu.SemaphoreType.DMA((3,)), pltpu.SemaphoreType.DMA((3,)), pltpu.SemaphoreType.DMA],
    )(p)


def _join_halves(f):
    def body(f_ref, out_ref, send, recv, loc):
        x, y, c = _mesh_pos()
        mine = pltpu.make_async_copy(f_ref, out_ref.at[c], loc)
        mine.start()
        cp = _remote(f_ref, out_ref.at[c], send, recv, (x, y, 1 - c))
        cp.start()
        _remote(f_ref, out_ref.at[1 - c], send, recv, (x, y, 1 - c)).wait_recv()
        cp.wait_send()
        mine.wait()

    return pl.pallas_call(
        body, name="join_halves", in_specs=[ANY], out_specs=ANY, out_shape=_sds((2,) + f.shape, f.dtype),
        scratch_shapes=[pltpu.SemaphoreType.DMA, pltpu.SemaphoreType.DMA, pltpu.SemaphoreType.DMA],
    )(f)


def _row_tile(rows, cap=512):
    t = cap - cap % 8
    while rows % t:
        t -= 8
    return t


def _add_blocks(a, b):
    nb, r, n = a.shape
    tr = _row_tile(r)

    def body(a_ref, b_ref, o_ref):
        o_ref[...] = a_ref[...] + b_ref[...]

    spec = pl.BlockSpec((None, tr, n), lambda k, i: (k, i, 0))
    return _call(body, name="add_blocks", grid=(nb, r // tr), in_specs=[spec, spec], out_specs=spec,
                 out_shape=_sds(a.shape), sem=("parallel", "parallel"))(a, b)


def _sum_leading(a, *, name):
    nk, r, n = a.shape
    tr = _row_tile(r)

    def body(a_ref, o_ref):
        acc = a_ref[0]
        for k in range(1, nk):
            acc = acc + a_ref[k]
        o_ref[...] = acc

    return _call(body, name=name, grid=(r // tr,), in_specs=[pl.BlockSpec((nk, tr, n), lambda i: (0, i, 0))],
                 out_specs=pl.BlockSpec((tr, n), lambda i: (i, 0)), out_shape=_sds((r, n)), sem=("parallel",))(a)


def _silu_outer(a, b):
    kdim, n = a.shape[1], b.shape[1]

    def body(a_ref, b_ref, o_ref):
        av = a_ref[...]
        av = av * _sigmoid(av)
        bv = b_ref[...]
        ah, bh = av.astype(BF16), bv.astype(BF16)
        al, bl = (av - ah.astype(F32)).astype(BF16), (bv - bh.astype(F32)).astype(BF16)
        o_ref[...] = _dot_tn(ah, bh) + (_dot_tn(ah, bl) + _dot_tn(al, bh))

    return _call(body, name="dw_mod", grid=(1,), in_specs=[_full(a.shape), _full(b.shape)], out_specs=_full((kdim, n)),
                 out_shape=_sds((kdim, n)))(a, b)


def _adamw(w, g, m, v):
    r, n = w.shape
    tr = _row_tile(r)

    def body(w_ref, g_ref, m_ref, v_ref, d_ref, nm_ref, nv_ref):
        gv = g_ref[...]
        nm = ADAM_B1 * m_ref[...] + (1.0 - ADAM_B1) * gv
        nv = ADAM_B2 * v_ref[...] + (1.0 - ADAM_B2) * (gv * gv)
        m_hat = nm / (1.0 - ADAM_B1 ** ADAM_STEP)
        v_hat = nv / (1.0 - ADAM_B2 ** ADAM_STEP)
        d_ref[...] = -ADAM_LR * (m_hat / (jnp.sqrt(v_hat) + ADAM_EPS) + ADAM_WD * w_ref[...])
        nm_ref[...] = nm
        nv_ref[...] = nv

    spec = pl.BlockSpec((tr, n), lambda i: (i, 0))
    return _call(body, name="adamw", grid=(r // tr,), in_specs=[spec] * 4, out_specs=[spec] * 3,
                 out_shape=[_sds((r, n))] * 3, sem=("parallel",))(w, g, m, v)


BIG = ("w_in", "w_branch_a", "w_branch_b", "w_out", "w_up", "w_down", "conv_w")
BIG_ROWS = 3584
SMALL = ("b_mod", "b_in", "conv_b", "ln1_g", "ln1_b", "ln2_g", "ln2_b", "c_ctx", "attn_sink", "q_norm_g", "k_norm_g")
SMALL_ROWS = 8 * len(SMALL)


def _rows(a, n_rows):
    flat = a.reshape(-1)
    return jnp.pad(flat, (0, n_rows * D_MODEL - flat.shape[0])).reshape(n_rows, D_MODEL)


def _group8(a):
    return _rep8(_rows(a, 1)) if a.size <= D_MODEL else _rows(a, 8)


def _ungroup8(p, shape):
    size = math.prod(shape)
    return (p[0, :size] if size <= D_MODEL else p.reshape(-1)[:size]).reshape(shape)


def _pack_big(t):
    parts = [t[n].reshape(-1, D_MODEL) for n in BIG[:-1]] + [_rows(t["conv_w"], 8)]
    used = sum(p.shape[0] for p in parts)
    return jnp.concatenate(parts + [jnp.zeros((BIG_ROWS - used, D_MODEL), F32)], axis=0)


def _unpack_big(p, like):
    out, r = {}, 0
    for n in BIG:
        size = math.prod(like[n].shape)
        nr = size // D_MODEL if n != "conv_w" else 8
        out[n] = p[r:r + nr].reshape(-1)[:size].reshape(like[n].shape)
        r += nr
    return out


def _pack_small(t):
    return jnp.concatenate([_group8(t[n]) for n in SMALL], axis=0)


def _unpack_small(p, like):
    return {n: _ungroup8(p[8 * i:8 * i + 8], like[n].shape) for i, n in enumerate(SMALL)}


WEIGHTS = ("c_ctx", "w_mod", "b_mod", "w_in", "b_in", "attn_sink", "q_norm_g", "k_norm_g", "w_branch_a", "w_branch_b",
           "w_out", "ln1_g", "ln1_b", "w_up", "conv_w", "conv_b", "w_down", "ln2_g", "ln2_b")


def kernel(x, c, ctx, c_ctx, w_mod, b_mod, w_in, b_in, attn_sink, q_norm_g, k_norm_g, w_branch_a, w_branch_b, w_out, ln1_g, ln1_b, w_up, conv_w, conv_b, w_down, ln2_g, ln2_b, loss_target, m_c_ctx, m_w_mod, m_b_mod, m_w_in, m_b_in, m_attn_sink, m_q_norm_g, m_k_norm_g, m_w_branch_a, m_w_branch_b, m_w_out, m_ln1_g, m_ln1_b, m_w_up, m_conv_w, m_conv_b, m_w_down, m_ln2_g, m_ln2_b, v_c_ctx, v_w_mod, v_b_mod, v_w_in, v_b_in, v_attn_sink, v_q_norm_g, v_k_norm_g, v_w_branch_a, v_w_branch_b, v_w_out, v_ln1_g, v_ln1_b, v_w_up, v_conv_w, v_conv_b, v_w_down, v_ln2_g, v_ln2_b):
    w = dict(c_ctx=c_ctx, w_mod=w_mod, b_mod=b_mod, w_in=w_in, b_in=b_in, attn_sink=attn_sink, q_norm_g=q_norm_g,
             k_norm_g=k_norm_g, w_branch_a=w_branch_a, w_branch_b=w_branch_b, w_out=w_out, ln1_g=ln1_g, ln1_b=ln1_b,
             w_up=w_up, conv_w=conv_w, conv_b=conv_b, w_down=w_down, ln2_g=ln2_g, ln2_b=ln2_b)
    m = dict(c_ctx=m_c_ctx, w_mod=m_w_mod, b_mod=m_b_mod, w_in=m_w_in, b_in=m_b_in, attn_sink=m_attn_sink,
             q_norm_g=m_q_norm_g, k_norm_g=m_k_norm_g, w_branch_a=m_w_branch_a, w_branch_b=m_w_branch_b, w_out=m_w_out,
             ln1_g=m_ln1_g, ln1_b=m_ln1_b, w_up=m_w_up, conv_w=m_conv_w, conv_b=m_conv_b, w_down=m_w_down,
             ln2_g=m_ln2_g, ln2_b=m_ln2_b)
    v = dict(c_ctx=v_c_ctx, w_mod=v_w_mod, b_mod=v_b_mod, w_in=v_w_in, b_in=v_b_in, attn_sink=v_attn_sink,
             q_norm_g=v_q_norm_g, k_norm_g=v_k_norm_g, w_branch_a=v_w_branch_a, w_branch_b=v_w_branch_b, w_out=v_w_out,
             ln1_g=v_ln1_g, ln1_b=v_ln1_b, w_up=v_w_up, conv_w=v_conv_w, conv_b=v_conv_b, w_down=v_w_down,
             ln2_g=v_ln2_g, ln2_b=v_ln2_b)
    xp, yp, _ = _mesh_pos()
    me = 2 * xp + yp

    branches = jnp.concatenate([w_branch_a[0], w_branch_b[0]], axis=0)
    wide = jnp.concatenate([w_mod[0], w_in[0], w_up[0], branches], axis=1).astype(BF16)
    tall = jnp.concatenate([w_out[0], w_down[0]], axis=0).astype(BF16)
    wide4, tall4, cw4 = _gather_shards([wide, tall], conv_w[0])
    n_mod, n_in, n_up = w_mod.shape[-1], w_in.shape[-1], w_up.shape[-1]
    wmod4 = wide4[:, :, :n_mod]
    win4 = wide4[:, :, n_mod:n_mod + n_in]
    wup4 = wide4[:, :, n_mod + n_in:n_mod + n_in + n_up]
    br4 = wide4[:, :, n_mod + n_in + n_up:]
    n_br = w_branch_a.shape[1]
    wba = br4[:, :n_br].transpose(1, 0, 2).reshape(n_br, D_MODEL)
    wbb = br4[:, n_br:].transpose(1, 0, 2).reshape(n_br, D_MODEL)
    n_out = w_out.shape[1]
    w_out_full = tall4[:, :n_out].reshape(D_MODEL, D_MODEL)
    w_down_full = tall4[:, n_out:].reshape(D_FF, D_MODEL)
    cw_full = cw4.transpose(1, 0, 2).reshape(3, 2 * D_FF)

    loss, grad_x, g = _local_step(
        x[0], c, ctx[0], c_ctx[None], wmod4, b_mod, win4, b_in, attn_sink[0], q_norm_g, k_norm_g, wba, wbb, w_out_full,
        ln1_g, ln1_b, wup4, cw_full, conv_b, w_down_full, ln2_g, ln2_b, loss_target[0])
    loss = lax.psum(loss, ("x", "y", "c"))

    sent = dict(c=c, dmod=g["dmod"], dmodc=g["dmodc"], b_in=g["b_in"], conv_b=g["conv_b"], ln1_g=g["ln1_g"],
                ln1_b=g["ln1_b"], ln2_g=g["ln2_g"], ln2_b=g["ln2_b"], c_ctx=g["c_ctx"], attn_sink=g["sink"],
                q_norm_g=g["qn"], k_norm_g=g["kn"])
    every = _allgather_rows(jnp.concatenate([_group8(a) for a in sent.values()], axis=0))
    total = _sum_leading(every, name="sum_devices")
    slot = {n: slice(8 * i, 8 * i + 8) for i, n in enumerate(sent)}
    gs = {n: _ungroup8(total[slot[n]], sent[n].shape) for n in SMALL if n in sent}
    dmodc_sum = jnp.concatenate([_ungroup8(total[slot["dmodc"]], (1, 2 * D_MODEL)), jnp.zeros((1, 4 * D_MODEL), F32)],
                                axis=1)
    gs["b_mod"] = _ungroup8(total[slot["dmod"]], b_mod.shape) + dmodc_sum
    acts = jnp.concatenate([every[:, slot["c"].start], _rep8(c_ctx)], axis=0)
    dmods = jnp.concatenate([every[:, slot["dmod"]].reshape(N_DEV, -1)[:, :6 * D_MODEL], _first_row(dmodc_sum)], axis=0)
    g_w_mod = _silu_outer(acts, lax.dynamic_slice_in_dim(dmods, me * n_mod, n_mod, axis=1))

    cw_g4 = _to_blocks4(g["conv_w"])
    packed = jnp.concatenate([
        g["w_in4"].reshape(N_CHIPS, -1, D_MODEL), _to_blocks4(g["wba"]).reshape(N_CHIPS, -1, D_MODEL),
        _to_blocks4(g["wbb"]).reshape(N_CHIPS, -1, D_MODEL), g["w_out"].reshape(N_CHIPS, -1, D_MODEL),
        g["w_up4"].reshape(N_CHIPS, -1, D_MODEL), g["w_down"].reshape(N_CHIPS, -1, D_MODEL),
        jnp.pad(cw_g4.reshape(N_CHIPS, -1), ((0, 0), (0, 8 * D_MODEL - cw_g4.shape[1] * cw_g4.shape[2]))).reshape(
            N_CHIPS, 8, D_MODEL),
        jnp.zeros((N_CHIPS, BIG_ROWS - 3528, D_MODEL), F32)], axis=1)
    rh = BIG_ROWS // 2
    cpos = lax.axis_index("c")
    my_half = lax.dynamic_slice_in_dim(packed, cpos * rh, rh, axis=1)
    chip_sum = _add_blocks(my_half, _swap_other_half(packed))
    half_sum = _sum_leading(_scatter_to_chips(chip_sum), name="sum_chips")
    g_big = _unpack_big(_join_halves(half_sum).reshape(BIG_ROWS, D_MODEL), w)

    grads = dict(gs, w_mod=g_w_mod, **g_big)

    def pack_all(t):
        rows = jnp.concatenate([_pack_big(t), t["w_mod"].reshape(-1, D_MODEL), _pack_small(t)], axis=0)
        return jnp.pad(rows, ((0, -rows.shape[0] % 256), (0, 0)))

    delta_p, new_m_p, new_v_p = _adamw(pack_all(w), pack_all(grads), pack_all(m), pack_all(v))

    def unpack_all(p):
        r_mod = BIG_ROWS + w_mod.size // D_MODEL
        out = _unpack_big(p[:BIG_ROWS], w)
        out["w_mod"] = p[BIG_ROWS:r_mod].reshape(w_mod.shape)
        out.update(_unpack_small(p[r_mod:r_mod + SMALL_ROWS], w))
        return out

    grads = {n: grads[n].reshape(w[n].shape) for n in WEIGHTS}
    delta, new_m, new_v = unpack_all(delta_p), unpack_all(new_m_p), unpack_all(new_v_p)
    return (loss, grad_x[None], *[grads[n] for n in WEIGHTS], *[delta[n] for n in WEIGHTS],
            *[new_m[n] for n in WEIGHTS], *[new_v[n] for n in WEIGHTS])
```

```python
import functools
import math

import jax
import jax.numpy as jnp
from jax import lax
from jax.experimental import pallas as pl
from jax.experimental.pallas import tpu as pltpu

F32 = jnp.float32
BF16 = jnp.bfloat16

D_MODEL = 1024
HEAD_DIM = 64
N_HEADS = 8
N_KV = 2
WINDOW = 128
GRID_W = 64
ROPE_THETA = 10000.0
D_FF = 2816
LN_EPS = 1e-5
QK_EPS = 1e-6
ALPHA = 2.0 ** 0.25
Q_SCALE = HEAD_DIM ** -0.5
OFF_GA = 1536
IN_COLS = 3584
ADAM_LR, ADAM_B1, ADAM_B2, ADAM_EPS, ADAM_WD, ADAM_STEP = 0.001, 0.9, 0.999, 1e-8, 0.01, 10

LANES = 128
VMEM_BUDGET = 52 * 1024 * 1024
N_CHIPS = 4
N_DEV = 8
NEG = -1e30
MESH = pl.DeviceIdType.MESH


def _sigmoid(x):
    return 1.0 / (1.0 + jnp.exp(-x))


def _dot(a, b):
    return jnp.dot(a, b, preferred_element_type=F32)


def _dot_nt(a, b):
    return lax.dot_general(a, b, (((1,), (1,)), ((), ())), preferred_element_type=F32)


def _dot_tn(a, b):
    return lax.dot_general(a, b, (((0,), (0,)), ((), ())), preferred_element_type=F32)


def _call(body, *, name, grid, in_specs, out_specs, out_shape, scratch=(), sem=None, **kw):
    params = dict(vmem_limit_bytes=VMEM_BUDGET)
    if sem is not None:
        params["dimension_semantics"] = sem
    return pl.pallas_call(body, name=name, grid=grid, in_specs=in_specs, out_specs=out_specs,
                          out_shape=out_shape, scratch_shapes=list(scratch),
                          compiler_params=pltpu.CompilerParams(**params), **kw)


def _full(shape):
    n = len(shape)
    return pl.BlockSpec(shape, lambda *_: (0,) * n)


def _sds(shape, dtype=F32):
    return jax.ShapeDtypeStruct(shape, dtype)


def _mm_nn4(a, shift, scale, w4, bias, *, mode, split_out, out_dtype, tm, name):
    m, kdim = a.shape
    nb, _, ns = w4.shape

    def body(a_ref, sh_ref, sc_ref, w_ref, b_ref, o_ref):
        av = a_ref[...]
        if mode == "modulate":
            av = av * (1.0 + sc_ref[...]) + sh_ref[...]
        else:
            av = av * _sigmoid(av)
        o_ref[...] = (_dot(av.astype(BF16), w_ref[...]) + b_ref[...]).astype(out_dtype)

    if split_out:
        out_shape = _sds((2, m, 2 * ns), out_dtype)
        out_spec = pl.BlockSpec((None, tm, ns), lambda i, k: (k // 2, i, k % 2))
    else:
        out_shape = _sds((m, nb * ns), out_dtype)
        out_spec = pl.BlockSpec((tm, ns), lambda i, k: (i, k))
    return _call(
        body, name=name, grid=(m // tm, nb),
        in_specs=[pl.BlockSpec((tm, kdim), lambda i, k: (i, 0)),
                  pl.BlockSpec((1, kdim), lambda i, k: (0, 0)),
                  pl.BlockSpec((1, kdim), lambda i, k: (0, 0)),
                  pl.BlockSpec((None, kdim, ns), lambda i, k: (k, 0, 0)),
                  pl.BlockSpec((1, ns), lambda i, k: (0, k))],
        out_specs=out_spec, out_shape=out_shape, sem=("parallel", "arbitrary"),
    )(a, shift, scale, w4, bias)


def _mm_tn(a, b, *, a_spec, b_spec, grid, out_shape, out_spec, name, mod=None, init=None, colsum_spec=None,
           colsum_shape=None):
    red = len(grid) - 1
    has_mod, has_init, has_cs = mod is not None, init is not None, colsum_spec is not None

    def body(*refs):
        refs = list(refs)
        a_ref, b_ref = refs[0], refs[1]
        pos = 2
        if has_mod:
            sh_ref, sc_ref = refs[2], refs[3]
            pos = 4
        if has_init:
            init_ref = refs[pos]
            pos += 1
        o_ref = refs[pos]
        cs_ref = refs[pos + 1] if has_cs else None
        s = pl.program_id(red)

        @pl.when(s == 0)
        def _():
            o_ref[...] = init_ref[...] if has_init else jnp.zeros(o_ref.shape, F32)
            if has_cs:
                cs_ref[...] = jnp.zeros(cs_ref.shape, F32)

        av = a_ref[...]
        if has_mod:
            av = av * (1.0 + sc_ref[...]) + sh_ref[...]
        bv = b_ref[...]
        o_ref[...] += _dot_tn(av.astype(BF16), bv)
        if has_cs:
            cs_ref[...] += jnp.broadcast_to(jnp.sum(bv.astype(F32), axis=0, keepdims=True), cs_ref.shape)

    ins, in_specs = [a, b], [a_spec, b_spec]
    if has_mod:
        kdim = mod[0].shape[-1]
        ins += list(mod)
        in_specs += [_full((1, kdim)), _full((1, kdim))]
    if has_init:
        ins.append(init)
        in_specs.append(out_spec)
    out_specs, out_shapes = out_spec, out_shape
    if has_cs:
        out_specs, out_shapes = [out_spec, colsum_spec], [out_shape, colsum_shape]
    sem = ("parallel",) * red + ("arbitrary",)
    return _call(body, name=name, grid=grid, in_specs=in_specs, out_specs=out_specs, out_shape=out_shapes,
                 sem=sem)(*ins)


def _rope_tables(n_tok):
    pos = jnp.arange(n_tok, dtype=jnp.int32)
    rows = (pos // GRID_W).astype(F32)
    cols = (pos % GRID_W).astype(F32)
    n_freq = HEAD_DIM // 4
    inv_freq = ROPE_THETA ** (-jnp.arange(n_freq, dtype=F32) / n_freq)
    ang_r = rows[:, None] * inv_freq
    ang_c = cols[:, None] * inv_freq
    cos = jnp.concatenate([jnp.cos(ang_r)] * 2 + [jnp.cos(ang_c)] * 2, axis=-1)
    sin = jnp.concatenate([-jnp.sin(ang_r), jnp.sin(ang_r), -jnp.sin(ang_c), jnp.sin(ang_c)], axis=-1)
    return jnp.tile(cos, (1, 2)), jnp.tile(sin, (1, 2))


def _lane(shape):
    return lax.broadcasted_iota(jnp.int32, shape, 1)


def _rope_partner(t, lane):
    return jnp.where((lane % 32) < 16, pltpu.roll(t, LANES - 16, 1), pltpu.roll(t, 16, 1))


def _half_mean(s, lane):
    lo = jnp.sum(jnp.where(lane < HEAD_DIM, s, 0.0), axis=-1, keepdims=True)
    hi = jnp.sum(jnp.where(lane < HEAD_DIM, 0.0, s), axis=-1, keepdims=True)
    return jnp.where(lane < HEAD_DIM, lo, hi) * (1.0 / HEAD_DIM)


def _prep(proj, cos, sin, qg, kg, *, tm, name):
    m = proj.shape[0]

    def body(p_ref, cos_ref, sin_ref, qg_ref, kg_ref, qa_ref, ka_ref, va_ref, qb_ref, kb_ref, vb_ref):
        lane = _lane((tm, LANES))
        cosv, sinv = cos_ref[...], sin_ref[...]
        low = lane < HEAD_DIM

        def rope(t):
            return t * cosv + _rope_partner(t, lane) * sinv

        def rms(t, g):
            return t * lax.rsqrt(_half_mean(t * t, lane) + QK_EPS) * g

        def place(q_ref, j, chunk):
            sw = pltpu.roll(chunk, HEAD_DIM, 1)
            if j < 2:
                h0, h1 = jnp.where(low, chunk, 0.0), jnp.where(low, sw, 0.0)
            else:
                h0, h1 = jnp.where(low, 0.0, sw), jnp.where(low, 0.0, chunk)
            q_ref[2 * j] = h0.astype(BF16)
            q_ref[2 * j + 1] = h1.astype(BF16)

        for j in range(4):
            place(qa_ref, j, rope(p_ref[:, j * LANES:(j + 1) * LANES]) * Q_SCALE)
            place(qb_ref, j, rope(rms(p_ref[:, 768 + j * LANES:768 + (j + 1) * LANES], qg_ref[...])) * Q_SCALE)
        ka_ref[...] = rope(p_ref[:, 512:640]).astype(BF16)
        va_ref[...] = p_ref[:, 640:768].astype(BF16)
        kb_ref[...] = rope(rms(p_ref[:, 1280:1408], kg_ref[...])).astype(BF16)
        vb_ref[...] = p_ref[:, 1408:1536].astype(BF16)

    row = pl.BlockSpec((tm, LANES), lambda i: (i, 0))
    qspec = pl.BlockSpec((N_HEADS, tm, LANES), lambda i: (0, i, 0))
    return _call(
        body, name=name, grid=(m // tm,),
        in_specs=[pl.BlockSpec((tm, OFF_GA), lambda i: (i, 0)), row, row, _full((1, LANES)), _full((1, LANES))],
        out_specs=[qspec, row, row, qspec, row, row],
        out_shape=[_sds((N_HEADS, m, LANES), BF16), _sds((m, LANES), BF16), _sds((m, LANES), BF16),
                   _sds((N_HEADS, m, LANES), BF16), _sds((m, LANES), BF16), _sds((m, LANES), BF16)],
        sem=("parallel",),
    )(proj, cos, sin, qg, kg)


def _prep_bwd(dqa, dka, dva, dqb, dkb, dvb, proj, cos, sin, qg, kg, dgl, *, tm, name):
    m = proj.shape[0]

    def body(dqa_ref, dka_ref, dva_ref, dqb_ref, dkb_ref, dvb_ref, p_ref, cos_ref, sin_ref, qg_ref, kg_ref,
             dgl_ref, dp_ref, dqg_ref, dkg_ref):
        i = pl.program_id(0)
        lane = _lane((tm, LANES))
        cosv, sinv = cos_ref[...], sin_ref[...]
        low = lane < HEAD_DIM

        @pl.when(i == 0)
        def _():
            dqg_ref[...] = jnp.zeros(dqg_ref.shape, F32)
            dkg_ref[...] = jnp.zeros(dkg_ref.shape, F32)

        def unrope(d):
            return d * cosv - _rope_partner(d, lane) * sinv

        def unplace(dq_ref, j):
            d0, d1 = dq_ref[2 * j], dq_ref[2 * j + 1]
            if j < 2:
                return jnp.where(low, d0, pltpu.roll(d1, HEAD_DIM, 1))
            return jnp.where(low, pltpu.roll(d0, HEAD_DIM, 1), d1)

        def unrms(dtn, t, g):
            r = lax.rsqrt(_half_mean(t * t, lane) + QK_EPS)
            u = dtn * g
            dt = r * u - t * (r * r * r) * _half_mean(u * t, lane)
            return dt, jnp.sum(dtn * t * r, axis=0, keepdims=True)

        for j in range(4):
            dp_ref[:, j * LANES:(j + 1) * LANES] = (unrope(unplace(dqa_ref, j)) * Q_SCALE).astype(BF16)
            c0 = 768 + j * LANES
            dt, dg = unrms(unrope(unplace(dqb_ref, j)) * Q_SCALE, p_ref[:, c0:c0 + LANES], qg_ref[...])
            dp_ref[:, c0:c0 + LANES] = dt.astype(BF16)
            dqg_ref[:, j * LANES:(j + 1) * LANES] += dg
        dp_ref[:, 512:640] = unrope(dka_ref[...]).astype(BF16)
        dp_ref[:, 640:768] = dva_ref[...].astype(BF16)
        dt, dg = unrms(unrope(dkb_ref[...]), p_ref[:, 1280:1408], kg_ref[...])
        dp_ref[:, 1280:1408] = dt.astype(BF16)
        dkg_ref[...] += dg
        dp_ref[:, 1408:1536] = dvb_ref[...].astype(BF16)
        dp_ref[:, OFF_GA:] = dgl_ref[...]

    row = pl.BlockSpec((tm, LANES), lambda i: (i, 0))
    qspec = pl.BlockSpec((N_HEADS, tm, LANES), lambda i: (0, i, 0))
    return _call(
        body, name=name, grid=(m // tm,),
        in_specs=[qspec, row, row, qspec, row, row, pl.BlockSpec((tm, OFF_GA), lambda i: (i, 0)), row, row,
                  _full((1, LANES)), _full((1, LANES)), pl.BlockSpec((tm, IN_COLS - OFF_GA), lambda i: (i, 0))],
        out_specs=[pl.BlockSpec((tm, IN_COLS), lambda i: (i, 0)), _full((1, 512)), _full((1, LANES))],
        out_shape=[_sds((m, IN_COLS), BF16), _sds((1, 512)), _sds((1, LANES))],
        sem=("arbitrary",),
    )(dqa, dka, dva, dqb, dkb, dvb, proj, cos, sin, qg, kg, dgl)


def _attn_glob_fwd(qt, k, vt, kc, vct, *, tq, tk):
    nh, _, s = qt.shape
    nc = kc.shape[0]

    def body(qt_ref, k_ref, vt_ref, kc_ref, vct_ref, ot_ref, lse_ref, acc_sc, st_sc):
        qtv = qt_ref[...]
        acc_sc[...] = jnp.zeros(acc_sc.shape, F32)
        n_chunks = s // tk

        def update(st, vtv, m_old, l_old):
            m_new = jnp.maximum(m_old, jnp.max(st, axis=0, keepdims=True))
            pt = jnp.exp(st - m_new)
            al = jnp.exp(m_old - m_new)
            acc_sc[...] = acc_sc[...] * al + _dot(vtv, pt.astype(BF16))
            return m_new, l_old * al + jnp.sum(pt, axis=0, keepdims=True)

        def loop(c, carry):
            off = pl.multiple_of(c * tk, tk)
            nxt = pl.multiple_of(jnp.minimum(c + 1, n_chunks - 1) * tk, tk)
            st = st_sc[...]
            st_next = _dot(k_ref[pl.ds(nxt, tk), :], qtv)
            carry = update(st, vt_ref[:, pl.ds(off, tk)], *carry)
            st_sc[...] = st_next
            return carry

        init = (jnp.full((1, tq), NEG, F32), jnp.zeros((1, tq), F32))
        st_sc[...] = _dot(k_ref[pl.ds(0, tk), :], qtv)
        m, l = lax.fori_loop(0, n_chunks, loop, update(_dot(kc_ref[...], qtv), vct_ref[...], *init))
        ot_ref[...] = (acc_sc[...] / l).astype(BF16)
        lse_ref[...] = m + jnp.log(l)

    return _call(
        body, name="attn_glob_fwd", grid=(nh, s // tq),
        in_specs=[pl.BlockSpec((None, LANES, tq), lambda h, i: (h, 0, i)),
                  _full((s, LANES)), _full((LANES, s)), _full((nc, LANES)), _full((LANES, nc))],
        out_specs=[pl.BlockSpec((None, LANES, tq), lambda h, i: (h, 0, i)),
                   pl.BlockSpec((None, 1, tq), lambda h, i: (h, 0, i))],
        out_shape=[_sds((nh, LANES, s), BF16), _sds((nh, 1, s))],
        scratch=[pltpu.VMEM((LANES, tq), F32), pltpu.VMEM((tk, tq), F32)],
        sem=("parallel", "parallel"),
    )(qt, k, vt, kc, vct)


def _attn_glob_dq(qt, dot, ot, lse, k, kt, v, kc, kct, vc, *, tq, tk):
    nh, _, s = qt.shape
    nc = kc.shape[0]

    def body(qt_ref, dot_ref, ot_ref, lse_ref, k_ref, kt_ref, v_ref, kc_ref, kct_ref, vc_ref,
             dqt_ref, dl_ref, dkct_ref, dvct_ref, acc_sc, st_sc, dp_sc):
        first = jnp.logical_and(pl.program_id(0) == 0, pl.program_id(1) == 0)

        @pl.when(first)
        def _():
            dkct_ref[...] = jnp.zeros(dkct_ref.shape, F32)
            dvct_ref[...] = jnp.zeros(dvct_ref.shape, F32)

        qtv, dotv, lse = qt_ref[...], dot_ref[...], lse_ref[...]
        delta = jnp.sum(dotv.astype(F32) * ot_ref[...].astype(F32), axis=0, keepdims=True)
        dl_ref[...] = delta

        n_chunks = s // tk

        def grads(st, dpt):
            pt = jnp.exp(st - lse)
            return pt.astype(BF16), (pt * (dpt - delta)).astype(BF16)

        def loop(c, carry):
            off = pl.multiple_of(c * tk, tk)
            nxt = pl.multiple_of(jnp.minimum(c + 1, n_chunks - 1) * tk, tk)
            st, dpt = st_sc[...], dp_sc[...]
            st_next = _dot(k_ref[pl.ds(nxt, tk), :], qtv)
            dp_next = _dot(v_ref[pl.ds(nxt, tk), :], dotv)
            _, dsb = grads(st, dpt)
            acc_sc[...] += _dot(kt_ref[:, pl.ds(off, tk)], dsb)
            st_sc[...] = st_next
            dp_sc[...] = dp_next
            return carry

        st_sc[...] = _dot(k_ref[pl.ds(0, tk), :], qtv)
        dp_sc[...] = _dot(v_ref[pl.ds(0, tk), :], dotv)
        pb, dsb = grads(_dot(kc_ref[...], qtv), _dot(vc_ref[...], dotv))
        acc_sc[...] = _dot(kct_ref[...], dsb)
        dkct_ref[...] += _dot_nt(qtv, dsb)
        dvct_ref[...] += _dot_nt(dotv, pb)
        lax.fori_loop(0, n_chunks, loop, 0)
        dqt_ref[...] = acc_sc[...]

    qs = pl.BlockSpec((None, LANES, tq), lambda h, i: (h, 0, i))
    rs = pl.BlockSpec((None, 1, tq), lambda h, i: (h, 0, i))
    return _call(
        body, name="attn_glob_dq", grid=(nh, s // tq),
        in_specs=[qs, qs, qs, rs, _full((s, LANES)), _full((LANES, s)), _full((s, LANES)), _full((nc, LANES)),
                  _full((LANES, nc)), _full((nc, LANES))],
        out_specs=[qs, rs, _full((LANES, nc)), _full((LANES, nc))],
        out_shape=[_sds((nh, LANES, s)), _sds((nh, 1, s)), _sds((LANES, nc)), _sds((LANES, nc))],
        scratch=[pltpu.VMEM((LANES, tq), F32), pltpu.VMEM((tk, tq), F32), pltpu.VMEM((tk, tq), F32)],
        sem=("arbitrary", "arbitrary"),
    )(qt, dot, ot, lse, k, kt, v, kc, kct, vc)


def _attn_glob_dkv(qt, dot, lse, dl, k, v, *, tq, tk):
    nh, _, s = qt.shape

    def body(k_ref, v_ref, qt_ref, dot_ref, lse_ref, dl_ref, dkt_ref, dvt_ref, st_sc, dp_sc):
        @pl.when(pl.program_id(1) == 0)
        def _():
            dkt_ref[...] = jnp.zeros(dkt_ref.shape, F32)
            dvt_ref[...] = jnp.zeros(dvt_ref.shape, F32)

        kv, vv = k_ref[...], v_ref[...]
        n_chunks = s // tq

        def loop(c, carry):
            off = pl.multiple_of(c * tq, tq)
            nxt = pl.multiple_of(jnp.minimum(c + 1, n_chunks - 1) * tq, tq)
            st, dpt = st_sc[...], dp_sc[...]
            st_next = _dot(kv, qt_ref[:, pl.ds(nxt, tq)])
            dp_next = _dot(vv, dot_ref[:, pl.ds(nxt, tq)])
            pt = jnp.exp(st - lse_ref[:, pl.ds(off, tq)])
            dst = pt * (dpt - dl_ref[:, pl.ds(off, tq)])
            dkt_ref[...] += _dot_nt(qt_ref[:, pl.ds(off, tq)], dst.astype(BF16))
            dvt_ref[...] += _dot_nt(dot_ref[:, pl.ds(off, tq)], pt.astype(BF16))
            st_sc[...] = st_next
            dp_sc[...] = dp_next
            return carry

        st_sc[...] = _dot(kv, qt_ref[:, pl.ds(0, tq)])
        dp_sc[...] = _dot(vv, dot_ref[:, pl.ds(0, tq)])
        lax.fori_loop(0, n_chunks, loop, 0)

    ks = pl.BlockSpec((tk, LANES), lambda j, h: (j, 0))
    ts = pl.BlockSpec((LANES, tk), lambda j, h: (0, j))
    qs = pl.BlockSpec((None, LANES, s), lambda j, h: (h, 0, 0))
    rs = pl.BlockSpec((None, 1, s), lambda j, h: (h, 0, 0))
    return _call(
        body, name="attn_glob_dkv", grid=(s // tk, nh),
        in_specs=[ks, ks, qs, qs, rs, rs], out_specs=[ts, ts],
        out_shape=[_sds((LANES, s)), _sds((LANES, s))],
        scratch=[pltpu.VMEM((tk, tq), F32), pltpu.VMEM((tk, tq), F32)],
        sem=("parallel", "arbitrary"),
    )(k, v, qt, dot, lse, dl)


WIN_SPAN = 2 * WINDOW


def _band(rows0, cols0, shape):
    r = rows0 + lax.broadcasted_iota(jnp.int32, shape, 0)
    c = cols0 + lax.broadcasted_iota(jnp.int32, shape, 1)
    return jnp.abs(r - c) <= WINDOW


def _win_start(blk, t, s):
    return pl.multiple_of(jnp.clip(blk * t - WINDOW, 0, s - t - WIN_SPAN), WINDOW)


def _attn_win_fwd(q, k, v, kc, vc, sink, *, tq):
    nh, s, _ = q.shape
    nc = kc.shape[0]
    tw = tq + WIN_SPAN

    def body(sink_ref, q_ref, k_ref, v_ref, kc_ref, vc_ref, o_ref, lse_ref):
        h, i = pl.program_id(0), pl.program_id(1)
        k0 = _win_start(i, tq, s)
        qv = q_ref[...]
        kv, vv = k_ref[pl.ds(k0, tw), :], v_ref[pl.ds(k0, tw), :]
        sc = jnp.where(_band(i * tq, k0, (tq, tw)), _dot_nt(qv, kv), NEG)
        scc = _dot_nt(qv, kc_ref[...])
        snk = sink_ref[h]
        m = jnp.maximum(jnp.maximum(jnp.max(sc, axis=-1, keepdims=True), jnp.max(scc, axis=-1, keepdims=True)), snk)
        p, pc = jnp.exp(sc - m), jnp.exp(scc - m)
        l = jnp.sum(p, axis=-1, keepdims=True) + jnp.sum(pc, axis=-1, keepdims=True) + jnp.exp(snk - m)
        acc = _dot(p.astype(BF16), vv) + _dot(pc.astype(BF16), vc_ref[...])
        o_ref[...] = (acc / l).astype(BF16)
        lse_ref[...] = m + jnp.log(l)

    return _call(
        body, name="attn_win_fwd", grid=(nh, s // tq),
        in_specs=[pl.BlockSpec(memory_space=pltpu.SMEM),
                  pl.BlockSpec((None, tq, LANES), lambda h, i: (h, i, 0)),
                  _full((s, LANES)), _full((s, LANES)), _full((nc, LANES)), _full((nc, LANES))],
        out_specs=[pl.BlockSpec((None, tq, LANES), lambda h, i: (h, i, 0)),
                   pl.BlockSpec((None, tq, 1), lambda h, i: (h, i, 0))],
        out_shape=[_sds((nh, s, LANES), BF16), _sds((nh, s, 1))],
        sem=("parallel", "parallel"),
    )(sink, q, k, v, kc, vc)


def _attn_win_dq(q, do, o, lse, k, v, kc, vc, sink, *, tq):
    nh, s, _ = q.shape
    nc = kc.shape[0]
    tw = tq + WIN_SPAN
    nq = s // tq

    def body(sink_ref, q_ref, do_ref, o_ref, lse_ref, k_ref, v_ref, kc_ref, vc_ref,
             dq_ref, dl_ref, dkc_ref, dvc_ref, dsk_ref):
        h, i = pl.program_id(0), pl.program_id(1)

        @pl.when(jnp.logical_and(h == 0, i == 0))
        def _():
            dkc_ref[...] = jnp.zeros(dkc_ref.shape, F32)
            dvc_ref[...] = jnp.zeros(dvc_ref.shape, F32)

        k0 = _win_start(i, tq, s)
        qv, dov, lse = q_ref[...], do_ref[...], lse_ref[...]
        kv, vv = k_ref[pl.ds(k0, tw), :], v_ref[pl.ds(k0, tw), :]
        kcv, vcv = kc_ref[...], vc_ref[...]
        delta = jnp.sum(dov.astype(F32) * o_ref[...].astype(F32), axis=-1, keepdims=True)
        dl_ref[...] = delta
        p = jnp.where(_band(i * tq, k0, (tq, tw)), jnp.exp(_dot_nt(qv, kv) - lse), 0.0)
        ds = (p * (_dot_nt(dov, vv) - delta)).astype(BF16)
        pc = jnp.exp(_dot_nt(qv, kcv) - lse)
        dsc = (pc * (_dot_nt(dov, vcv) - delta)).astype(BF16)
        dq_ref[...] = _dot(ds, kv) + _dot(dsc, kcv)
        dkc_ref[...] += _dot_tn(dsc, qv)
        dvc_ref[...] += _dot_tn(pc.astype(BF16), dov)
        dsk = -jnp.sum(jnp.exp(sink_ref[h] - lse) * delta)
        dsk_ref[...] = jnp.full(dsk_ref.shape, dsk, F32)

    qs = pl.BlockSpec((None, tq, LANES), lambda h, i: (h, i, 0))
    cs = pl.BlockSpec((None, tq, 1), lambda h, i: (h, i, 0))
    return _call(
        body, name="attn_win_dq", grid=(nh, nq),
        in_specs=[pl.BlockSpec(memory_space=pltpu.SMEM), qs, qs, qs, cs,
                  _full((s, LANES)), _full((s, LANES)), _full((nc, LANES)), _full((nc, LANES))],
        out_specs=[qs, cs, _full((nc, LANES)), _full((nc, LANES)),
                   pl.BlockSpec((None, None, 8, LANES), lambda h, i: (h, i, 0, 0))],
        out_shape=[_sds((nh, s, LANES)), _sds((nh, s, 1)), _sds((nc, LANES)), _sds((nc, LANES)),
                   _sds((nh, nq, 8, LANES))],
        sem=("arbitrary", "arbitrary"),
    )(sink, q, do, o, lse, k, v, kc, vc)


def _attn_win_dkv(q, do, lse_row, dl_row, k, v, *, tk):
    nh, s, _ = q.shape
    tw = tk + WIN_SPAN

    def body(k_ref, v_ref, q_ref, do_ref, lse_ref, dl_ref, dk_ref, dv_ref):
        j = pl.program_id(0)

        @pl.when(pl.program_id(1) == 0)
        def _():
            dk_ref[...] = jnp.zeros(dk_ref.shape, F32)
            dv_ref[...] = jnp.zeros(dv_ref.shape, F32)

        q0 = _win_start(j, tk, s)
        kv, vv = k_ref[...], v_ref[...]
        qv, dov = q_ref[pl.ds(q0, tw), :], do_ref[pl.ds(q0, tw), :]
        pt = jnp.where(_band(j * tk, q0, (tk, tw)), jnp.exp(_dot_nt(kv, qv) - lse_ref[:, pl.ds(q0, tw)]), 0.0)
        dst = pt * (_dot_nt(vv, dov) - dl_ref[:, pl.ds(q0, tw)])
        dk_ref[...] += _dot(dst.astype(BF16), qv)
        dv_ref[...] += _dot(pt.astype(BF16), dov)

    ks = pl.BlockSpec((tk, LANES), lambda j, h: (j, 0))
    qs = pl.BlockSpec((None, s, LANES), lambda j, h: (h, 0, 0))
    rs = pl.BlockSpec((None, 1, s), lambda j, h: (h, 0, 0))
    return _call(
        body, name="attn_win_dkv", grid=(s // tk, nh),
        in_specs=[ks, ks, qs, qs, rs, rs], out_specs=[ks, ks],
        out_shape=[_sds((s, LANES)), _sds((s, LANES))],
        sem=("parallel", "arbitrary"),
    )(k, v, q, do, lse_row, dl_row)


def _ln_fwd(z, g, b):
    mu = jnp.mean(z, axis=-1, keepdims=True)
    zc = z - mu
    r = lax.rsqrt(jnp.mean(zc * zc, axis=-1, keepdims=True) + LN_EPS)
    return zc * r * g + b, mu, r


def _ln_bwd(dy, xhat, r, g):
    dxh = dy * g
    return r * (dxh - jnp.mean(dxh, axis=-1, keepdims=True) - xhat * jnp.mean(dxh * xhat, axis=-1, keepdims=True))


def _heads_matmul(o_ref, w_ref):
    acc = _dot(o_ref[0], w_ref[0])
    for h in range(1, N_HEADS):
        acc += _dot(o_ref[h], w_ref[h])
    return acc


def _gate_specs(tm):
    return [pl.BlockSpec((tm, 512), functools.partial(lambda i, b: (i, b), b=OFF_GA // 512 + b)) for b in range(4)]


def _merge_fwd(oa, ob, proj, x, gate1, wba, wbb, w_out, ln_g, ln_b, *, tm):
    s = x.shape[0]

    def body(oa_ref, ob_ref, g0, g1, g2, g3, x_ref, gt_ref, wba_ref, wbb_ref, wo_ref, lg_ref, lb_ref,
             x1_ref, y_ref, mu_ref, r_ref):
        ga = _sigmoid(jnp.concatenate([g0[...], g1[...]], axis=1))
        gb = _sigmoid(jnp.concatenate([g2[...], g3[...]], axis=1))
        merged = ga * _heads_matmul(oa_ref, wba_ref) + gb * _heads_matmul(ob_ref, wbb_ref)
        y = _dot(merged.astype(BF16), wo_ref[...])
        x1, mu, r = _ln_fwd(ALPHA * x_ref[...] + gt_ref[...] * y, lg_ref[...], lb_ref[...])
        x1_ref[...] = x1
        y_ref[...] = y
        mu_ref[...] = mu
        r_ref[...] = r

    hs = pl.BlockSpec((N_HEADS, tm, LANES), lambda i: (0, i, 0))
    row = pl.BlockSpec((tm, D_MODEL), lambda i: (i, 0))
    col = pl.BlockSpec((tm, 1), lambda i: (i, 0))
    vec = _full((1, D_MODEL))
    wh = _full((N_HEADS, LANES, D_MODEL))
    return _call(
        body, name="merge_fwd", grid=(s // tm,),
        in_specs=[hs, hs, *_gate_specs(tm), row, vec, wh, wh, _full((D_MODEL, D_MODEL)), vec, vec],
        out_specs=[row, row, col, col],
        out_shape=[_sds((s, D_MODEL)), _sds((s, D_MODEL)), _sds((s, 1)), _sds((s, 1))],
        sem=("parallel",),
    )(oa, ob, proj, proj, proj, proj, x, gate1, wba, wbb, w_out, ln_g, ln_b)


def _merge_bwd(dy, oa, ob, proj, wba, wbb, w_out, *, tm):
    s = dy.shape[0]

    def body(dy_ref, oa_ref, ob_ref, g0, g1, g2, g3, wba_ref, wbb_ref, wo_ref,
             dgl_ref, doa_ref, dob_ref, dpa_ref, dpb_ref, mg_ref):
        dm = _dot_nt(dy_ref[...], wo_ref[...])
        ga = _sigmoid(jnp.concatenate([g0[...], g1[...]], axis=1))
        gb = _sigmoid(jnp.concatenate([g2[...], g3[...]], axis=1))
        pa, pb = _heads_matmul(oa_ref, wba_ref), _heads_matmul(ob_ref, wbb_ref)
        mg_ref[...] = (ga * pa + gb * pb).astype(BF16)
        dgl_ref[:, :D_MODEL] = (dm * pa * ga * (1.0 - ga)).astype(BF16)
        dgl_ref[:, D_MODEL:] = (dm * pb * gb * (1.0 - gb)).astype(BF16)
        dpa, dpb = (dm * ga).astype(BF16), (dm * gb).astype(BF16)
        dpa_ref[...] = dpa
        dpb_ref[...] = dpb
        for h in range(N_HEADS):
            doa_ref[h] = _dot_nt(dpa, wba_ref[h]).astype(BF16)
            dob_ref[h] = _dot_nt(dpb, wbb_ref[h]).astype(BF16)

    hs = pl.BlockSpec((N_HEADS, tm, LANES), lambda i: (0, i, 0))
    row = pl.BlockSpec((tm, D_MODEL), lambda i: (i, 0))
    wh = _full((N_HEADS, LANES, D_MODEL))
    return _call(
        body, name="merge_bwd", grid=(s // tm,),
        in_specs=[row, hs, hs, *_gate_specs(tm), wh, wh, _full((D_MODEL, D_MODEL))],
        out_specs=[pl.BlockSpec((tm, 2 * D_MODEL), lambda i: (i, 0)), hs, hs, row, row, row],
        out_shape=[_sds((s, 2 * D_MODEL), BF16), _sds((N_HEADS, s, LANES), BF16), _sds((N_HEADS, s, LANES), BF16),
                   _sds((s, D_MODEL), BF16), _sds((s, D_MODEL), BF16), _sds((s, D_MODEL), BF16)],
        sem=("parallel",),
    )(dy, oa, ob, proj, proj, proj, proj, wba, wbb, w_out)


FF_TC = 256


def _shift_rows(t, prev_row, next_row):
    n = t.shape[0]
    r = lax.broadcasted_iota(jnp.int32, t.shape, 0)
    up = jnp.where(r == 0, prev_row, pltpu.roll(t, 1, 0))
    dn = jnp.where(r == n - 1, next_row, pltpu.roll(t, n - 1, 0))
    return up, dn


def _halo_specs(tm, s, tc):
    nb8 = s // 8
    main = pl.BlockSpec((2, tm, tc), lambda j, i: (0, i, j))
    prev = pl.BlockSpec((2, 8, tc), lambda j, i: (0, jnp.maximum(i * (tm // 8) - 1, 0), j))
    nxt = pl.BlockSpec((2, 8, tc), lambda j, i: (0, jnp.minimum((i + 1) * (tm // 8), nb8 - 1), j))
    return main, prev, nxt


def _halo_rows(prev_ref, next_ref, half, i, n_i):
    prev_row = jnp.where(i == 0, 0.0, prev_ref[half, 7:8, :].astype(F32))
    next_row = jnp.where(i == n_i - 1, 0.0, next_ref[half, 0:1, :].astype(F32))
    return prev_row, next_row


def _conv(t, prev_row, next_row, w, b):
    up, dn = _shift_rows(t, prev_row, next_row)
    return w[0:1, :] * up + w[1:2, :] * t + w[2:3, :] * dn + b


def _ffn_act_fwd(u, cw, cb, *, tm):
    _, s, ff = u.shape
    n_i = s // tm

    def body(u_ref, up_ref, un_ref, cw_ref, cb_ref, a_ref):
        i = pl.program_id(1)
        gc = _conv(u_ref[0], *_halo_rows(up_ref, un_ref, 0, i, n_i), cw_ref[0], cb_ref[0])
        vc = _conv(u_ref[1], *_halo_rows(up_ref, un_ref, 1, i, n_i), cw_ref[1], cb_ref[1])
        a_ref[...] = (gc * _sigmoid(gc) * vc).astype(BF16)

    main, prev, nxt = _halo_specs(tm, s, FF_TC)
    return _call(
        body, name="ffn_act_fwd", grid=(ff // FF_TC, n_i),
        in_specs=[main, prev, nxt, pl.BlockSpec((2, 3, FF_TC), lambda j, i: (0, 0, j)),
                  pl.BlockSpec((2, 1, FF_TC), lambda j, i: (0, 0, j))],
        out_specs=pl.BlockSpec((tm, FF_TC), lambda j, i: (i, j)),
        out_shape=_sds((s, ff), BF16), sem=("parallel", "parallel"),
    )(u, u, u, cw, cb)


def _ffn_act_bwd(dy2, w_down, u, cw, cb, *, tm):
    _, s, ff = u.shape
    n_i = s // tm

    def body(dy_ref, wd_ref, u_ref, up_ref, un_ref, cw_ref, cb_ref, dc_ref, dcw_ref, dcb_ref):
        i = pl.program_id(1)

        @pl.when(i == 0)
        def _():
            dcw_ref[...] = jnp.zeros(dcw_ref.shape, F32)
            dcb_ref[...] = jnp.zeros(dcb_ref.shape, F32)

        da = _dot_nt(dy_ref[...], wd_ref[...])
        ug, uv = u_ref[0], u_ref[1]
        ugp, ugn = _shift_rows(ug, *_halo_rows(up_ref, un_ref, 0, i, n_i))
        uvp, uvn = _shift_rows(uv, *_halo_rows(up_ref, un_ref, 1, i, n_i))
        wg, wv = cw_ref[0], cw_ref[1]
        gc = wg[0:1, :] * ugp + wg[1:2, :] * ug + wg[2:3, :] * ugn + cb_ref[0]
        vc = wv[0:1, :] * uvp + wv[1:2, :] * uv + wv[2:3, :] * uvn + cb_ref[1]
        sg = _sigmoid(gc)
        dg = da * vc * sg * (1.0 + gc * (1.0 - sg))
        dv = da * gc * sg
        dc_ref[0] = dg
        dc_ref[1] = dv
        for half, (d, taps) in enumerate(((dg, (ugp, ug, ugn)), (dv, (uvp, uv, uvn)))):
            for tap in range(3):
                dcw_ref[half, tap:tap + 1, :] += jnp.sum(d * taps[tap], axis=0, keepdims=True)
            dcb_ref[half] += jnp.sum(d, axis=0, keepdims=True)

    main, prev, nxt = _halo_specs(tm, s, FF_TC)
    return _call(
        body, name="ffn_act_bwd", grid=(ff // FF_TC, n_i),
        in_specs=[pl.BlockSpec((tm, D_MODEL), lambda j, i: (i, 0)), pl.BlockSpec((FF_TC, D_MODEL), lambda j, i: (j, 0)),
                  main, prev, nxt, pl.BlockSpec((2, 3, FF_TC), lambda j, i: (0, 0, j)),
                  pl.BlockSpec((2, 1, FF_TC), lambda j, i: (0, 0, j))],
        out_specs=[main, pl.BlockSpec((2, 3, FF_TC), lambda j, i: (0, 0, j)),
                   pl.BlockSpec((2, 1, FF_TC), lambda j, i: (0, 0, j))],
        out_shape=[_sds((2, s, ff)), _sds((2, 3, ff)), _sds((2, 1, ff))],
        sem=("parallel", "arbitrary"),
    )(dy2, w_down, u, u, u, cw, cb)


def _conv_bwd_input(dc, cw, *, tm):
    _, s, ff = dc.shape
    n_i = s // tm

    def body(d_ref, dp_ref, dn_ref, cw_ref, du_ref):
        i = pl.program_id(1)
        for half in range(2):
            up, dn = _shift_rows(d_ref[half], *_halo_rows(dp_ref, dn_ref, half, i, n_i))
            w = cw_ref[half]
            du_ref[half] = (w[0:1, :] * dn + w[1:2, :] * d_ref[half] + w[2:3, :] * up).astype(BF16)

    main, prev, nxt = _halo_specs(tm, s, FF_TC)
    return _call(
        body, name="conv_bwd_input", grid=(ff // FF_TC, n_i),
        in_specs=[main, prev, nxt, pl.BlockSpec((2, 3, FF_TC), lambda j, i: (0, 0, j))],
        out_specs=main, out_shape=_sds((2, s, ff), BF16), sem=("parallel", "parallel"),
    )(dc, dc, dc, cw)


def _ffn_down_loss(a, w_down, x1, target, gate2, ln_g, ln_b, *, tm):
    s, ff = a.shape
    n_i = s // tm

    def body(a_ref, wd_ref, x1_ref, tg_ref, gt_ref, lg_ref, lb_ref, ls_ref, dy_ref, dx_ref, dg_ref, db_ref, dgt_ref):
        @pl.when(pl.program_id(0) == 0)
        def _():
            dg_ref[...] = jnp.zeros(dg_ref.shape, F32)
            db_ref[...] = jnp.zeros(db_ref.shape, F32)
            dgt_ref[...] = jnp.zeros(dgt_ref.shape, F32)

        y2 = _dot(a_ref[...], wd_ref[...])
        z = ALPHA * x1_ref[...] + gt_ref[...] * y2
        mu = jnp.mean(z, axis=-1, keepdims=True)
        zc = z - mu
        r = lax.rsqrt(jnp.mean(zc * zc, axis=-1, keepdims=True) + LN_EPS)
        xhat = zc * r
        diff = xhat * lg_ref[...] + lb_ref[...] - tg_ref[...]
        ls_ref[...] = jnp.full(ls_ref.shape, 0.5 / D_MODEL * jnp.sum(diff * diff), F32)
        dx2 = diff * (1.0 / D_MODEL)
        dg_ref[...] += jnp.sum(dx2 * xhat, axis=0, keepdims=True)
        db_ref[...] += jnp.sum(dx2, axis=0, keepdims=True)
        dz = _ln_bwd(dx2, xhat, r, lg_ref[...])
        dgt_ref[...] += jnp.sum(dz * y2, axis=0, keepdims=True)
        dy_ref[...] = (gt_ref[...] * dz).astype(BF16)
        dx_ref[...] = ALPHA * dz

    row = pl.BlockSpec((tm, D_MODEL), lambda i: (i, 0))
    vec = _full((1, D_MODEL))
    return _call(
        body, name="ffn_down_loss", grid=(n_i,),
        in_specs=[pl.BlockSpec((tm, ff), lambda i: (i, 0)), _full((ff, D_MODEL)), row, row, vec, vec, vec],
        out_specs=[pl.BlockSpec((None, 8, LANES), lambda i: (i, 0, 0)), row, row, vec, vec, vec],
        out_shape=[_sds((n_i, 8, LANES)), _sds((s, D_MODEL), BF16), _sds((s, D_MODEL)),
                   _sds((1, D_MODEL)), _sds((1, D_MODEL)), _sds((1, D_MODEL))],
        sem=("arbitrary",),
    )(a, w_down, x1, target, gate2, ln_g, ln_b)


def _ffn_up_bwd(du, wup4, dx1a, x1, scale2, x, y, mu1, r1, gate1, ln_g, *, tm):
    s = x.shape[0]
    nb, _, ns = wup4.shape

    def body(du_ref, w_ref, dxa_ref, x1_ref, sc_ref, x_ref, y_ref, mu_ref, r_ref, gt_ref, lg_ref,
             dxo_ref, dy_ref, dsc_ref, dsh_ref, dg_ref, db_ref, dgt_ref, acc):
        i, k = pl.program_id(0), pl.program_id(1)

        @pl.when(jnp.logical_and(i == 0, k == 0))
        def _():
            for ref in (dsc_ref, dsh_ref, dg_ref, db_ref, dgt_ref):
                ref[...] = jnp.zeros(ref.shape, F32)

        @pl.when(k == 0)
        def _():
            acc[...] = jnp.zeros(acc.shape, F32)

        acc[...] += _dot_nt(du_ref[...], w_ref[...])

        @pl.when(k == nb - 1)
        def _():
            dh = acc[...]
            x1 = x1_ref[...]
            dsc_ref[...] += jnp.sum(dh * x1, axis=0, keepdims=True)
            dsh_ref[...] += jnp.sum(dh, axis=0, keepdims=True)
            dx1 = dxa_ref[...] + dh * (1.0 + sc_ref[...])
            yv = y_ref[...]
            xhat = (ALPHA * x_ref[...] + gt_ref[...] * yv - mu_ref[...]) * r_ref[...]
            dg_ref[...] += jnp.sum(dx1 * xhat, axis=0, keepdims=True)
            db_ref[...] += jnp.sum(dx1, axis=0, keepdims=True)
            dz = _ln_bwd(dx1, xhat, r_ref[...], lg_ref[...])
            dgt_ref[...] += jnp.sum(dz * yv, axis=0, keepdims=True)
            dy_ref[...] = (gt_ref[...] * dz).astype(BF16)
            dxo_ref[...] = ALPHA * dz

    row = pl.BlockSpec((tm, D_MODEL), lambda i, k: (i, 0))
    col = pl.BlockSpec((tm, 1), lambda i, k: (i, 0))
    vec = _full((1, D_MODEL))
    return _call(
        body, name="ffn_up_bwd", grid=(s // tm, nb),
        in_specs=[pl.BlockSpec((None, tm, ns), lambda i, k: (k // 2, i, k % 2)),
                  pl.BlockSpec((None, D_MODEL, ns), lambda i, k: (k, 0, 0)),
                  row, row, vec, row, row, col, col, vec, vec],
        out_specs=[row, row, vec, vec, vec, vec, vec],
        out_shape=[_sds((s, D_MODEL)), _sds((s, D_MODEL), BF16)] + [_sds((1, D_MODEL))] * 5,
        scratch=[pltpu.VMEM((tm, D_MODEL), F32)],
        sem=("arbitrary", "arbitrary"),
    )(du, wup4, dx1a, x1, scale2, x, y, mu1, r1, gate1, ln_g)


def _mm_nt4_mod_bwd(dp, w4, dxa, x, scale, *, tm, name):
    m = x.shape[0]
    nb, kdim, ns = w4.shape

    def body(dp_ref, w_ref, dxa_ref, x_ref, sc_ref, dx_ref, dsc_ref, dsh_ref, acc):
        i, k = pl.program_id(0), pl.program_id(1)

        @pl.when(jnp.logical_and(i == 0, k == 0))
        def _():
            dsc_ref[...] = jnp.zeros(dsc_ref.shape, F32)
            dsh_ref[...] = jnp.zeros(dsh_ref.shape, F32)

        @pl.when(k == 0)
        def _():
            acc[...] = jnp.zeros(acc.shape, F32)

        acc[...] += _dot_nt(dp_ref[...], w_ref[...])

        @pl.when(k == nb - 1)
        def _():
            dh = acc[...]
            dsc_ref[...] += jnp.sum(dh * x_ref[...], axis=0, keepdims=True)
            dsh_ref[...] += jnp.sum(dh, axis=0, keepdims=True)
            dx_ref[...] = dxa_ref[...] + dh * (1.0 + sc_ref[...])

    row = pl.BlockSpec((tm, kdim), lambda i, k: (i, 0))
    vec = _full((1, kdim))
    return _call(
        body, name=name, grid=(m // tm, nb),
        in_specs=[pl.BlockSpec((tm, ns), lambda i, k: (i, k)), pl.BlockSpec((None, kdim, ns), lambda i, k: (k, 0, 0)),
                  row, row, vec],
        out_specs=[row, vec, vec],
        out_shape=[_sds((m, kdim)), _sds((1, kdim)), _sds((1, kdim))],
        scratch=[pltpu.VMEM((tm, kdim), F32)],
        sem=("arbitrary", "arbitrary"),
    )(dp, w4, dxa, x, scale)


def _pad_heads_w(w):
    w8 = w.reshape(N_HEADS, HEAD_DIM, w.shape[-1])
    z = jnp.zeros_like(w8)
    first = (jnp.arange(N_HEADS) < N_HEADS // N_KV)[:, None, None]
    return jnp.where(first, jnp.concatenate([w8, z], axis=1), jnp.concatenate([z, w8], axis=1))


def _unpad_heads_w(g):
    first = (jnp.arange(N_HEADS) < N_HEADS // N_KV)[:, None, None]
    return jnp.where(first, g[:, :HEAD_DIM], g[:, HEAD_DIM:]).reshape(N_HEADS * HEAD_DIM, g.shape[-1])


def _rep8(a):
    return jnp.broadcast_to(a.reshape(1, -1), (8, a.size))


def _first_row(a):
    r8 = _rep8(a)
    return jnp.where(lax.broadcasted_iota(jnp.int32, r8.shape, 0) == 0, r8, 0.0)


def _to_blocks4(w):
    k, n = w.shape
    return w.reshape(k, N_CHIPS, n // N_CHIPS).transpose(1, 0, 2)


def _local_step(x, c, ctx, c_ctx, wmod4, b_mod, win4, b_in, sink, qn, kn, wba, wbb, w_out, ln1_g, ln1_b,
                wup4, cw, cb, w_down, ln2_g, ln2_b, target):
    s, nc = x.shape[0], ctx.shape[0]
    tm = min(512, s)
    tm2 = min(256, s)
    zvec = jnp.zeros((1, D_MODEL), F32)

    cc = jnp.concatenate([_rep8(c), _rep8(c_ctx)], axis=0)
    mods = _mm_nn4(cc, zvec, zvec, wmod4, b_mod, mode="silu", split_out=False, out_dtype=F32, tm=16, name="mod_vectors")
    shift1, scale1, gate1, shift2, scale2, gate2 = [mods[0:1, i * D_MODEL:(i + 1) * D_MODEL] for i in range(6)]
    shift_c, scale_c = mods[8:9, :D_MODEL], mods[8:9, D_MODEL:2 * D_MODEL]

    cos, sin = _rope_tables(s)
    cos_c, sin_c = jnp.ones((nc, LANES), F32), jnp.zeros((nc, LANES), F32)
    qg, kg = jnp.tile(qn, (1, 2)), jnp.tile(kn, (1, 2))

    proj_c = _mm_nn4(ctx, shift_c, scale_c, win4, b_in, mode="modulate", split_out=False, out_dtype=F32, tm=nc,
                     name="in_proj_ctx")
    _, kac, vac, _, kbc, vbc = _prep(proj_c, cos_c, sin_c, qg, kg, tm=nc, name="prep_ctx")
    proj = _mm_nn4(x, shift1, scale1, win4, b_in, mode="modulate", split_out=False, out_dtype=F32, tm=tm, name="in_proj")
    qa, ka, va, qb, kb, vb = _prep(proj, cos, sin, qg, kg, tm=tm, name="prep")
    oa, lse_a = _attn_win_fwd(qa, ka, va, kac, vac, sink, tq=tm2)
    qbt = jnp.swapaxes(qb, 1, 2)
    obt, lse_b = _attn_glob_fwd(qbt, kb, vb.T, kbc, vbc.T, tq=tm, tk=tm)
    ob = jnp.swapaxes(obt, 1, 2)
    wba_p, wbb_p = _pad_heads_w(wba), _pad_heads_w(wbb)
    x1, y, mu1, r1 = _merge_fwd(oa, ob, proj, x, gate1, wba_p, wbb_p, w_out, ln1_g, ln1_b, tm=tm2)
    u = _mm_nn4(x1, shift2, scale2, wup4, jnp.zeros((1, 2 * D_FF), F32), mode="modulate", split_out=True,
                out_dtype=F32, tm=tm, name="ffn_up")
    cw2 = cw.reshape(3, 2, D_FF).transpose(1, 0, 2)
    cb2 = cb.reshape(2, 1, D_FF)
    a = _ffn_act_fwd(u, cw2, cb2, tm=tm)
    ls, dy2, dx1a, dln2_g, dln2_b, dgate2 = _ffn_down_loss(a, w_down, x1, target, gate2, ln2_g, ln2_b, tm=tm2)
    loss = jnp.sum(ls[:, 0, 0])

    n_s = s // tm
    dw_down = _mm_tn(a, dy2, a_spec=pl.BlockSpec((tm, D_FF), lambda t: (t, 0)),
                     b_spec=pl.BlockSpec((tm, D_MODEL), lambda t: (t, 0)), grid=(n_s,),
                     out_shape=_sds((D_FF, D_MODEL)), out_spec=_full((D_FF, D_MODEL)), name="dw_down")
    dc, dcw2, dcb2 = _ffn_act_bwd(dy2, w_down, u, cw2, cb2, tm=tm)
    du = _conv_bwd_input(dc, cw2, tm=tm)
    dxz1, dy, dscale2, dshift2, dln1_g, dln1_b, dgate1 = _ffn_up_bwd(
        du, wup4, dx1a, x1, scale2, x, y, mu1, r1, gate1, ln1_g, tm=tm2)
    ns_up = wup4.shape[-1]
    dw_up4 = _mm_tn(x1, du, a_spec=pl.BlockSpec((tm, D_MODEL), lambda k, t: (t, 0)),
                    b_spec=pl.BlockSpec((None, tm, ns_up), lambda k, t: (k // 2, t, k % 2)), grid=(N_CHIPS, n_s),
                    out_shape=_sds((N_CHIPS, D_MODEL, ns_up)),
                    out_spec=pl.BlockSpec((None, D_MODEL, ns_up), lambda k, t: (k, 0, 0)),
                    mod=(shift2, scale2), name="dw_up")

    dgl, doa, dob, dpa, dpb, merged = _merge_bwd(dy, oa, ob, proj, wba_p, wbb_p, w_out, tm=tm2)
    rowspec = pl.BlockSpec((tm, D_MODEL), lambda t: (t, 0))
    dw_out = _mm_tn(merged, dy, a_spec=rowspec, b_spec=rowspec, grid=(n_s,), out_shape=_sds((D_MODEL, D_MODEL)),
                    out_spec=_full((D_MODEL, D_MODEL)), name="dw_out")
    hspec = dict(a_spec=pl.BlockSpec((None, tm, LANES), lambda h, t: (h, t, 0)),
                 b_spec=pl.BlockSpec((tm, D_MODEL), lambda h, t: (t, 0)), grid=(N_HEADS, n_s),
                 out_shape=_sds((N_HEADS, LANES, D_MODEL)),
                 out_spec=pl.BlockSpec((None, LANES, D_MODEL), lambda h, t: (h, 0, 0)))
    dwba = _unpad_heads_w(_mm_tn(oa, dpa, name="dw_branch_a", **hspec))
    dwbb = _unpad_heads_w(_mm_tn(ob, dpb, name="dw_branch_b", **hspec))

    dqa, dla, dkac, dvac, dsk = _attn_win_dq(qa, doa, oa, lse_a, ka, va, kac, vac, sink, tq=tm2)
    dka, dva = _attn_win_dkv(qa, doa, lse_a.reshape(N_HEADS, 1, s), dla.reshape(N_HEADS, 1, s), ka, va, tk=tm2)
    dobt = jnp.swapaxes(dob, 1, 2)
    dqbt, dlb, dkbct, dvbct = _attn_glob_dq(qbt, dobt, obt, lse_b, kb, kb.T, vb, kbc, kbc.T, vbc, tq=tm, tk=tm)
    dkbt, dvbt = _attn_glob_dkv(qbt, dobt, lse_b, dlb, kb, vb, tq=tm, tk=tm)
    dqb, dkb, dvb, dkbc, dvbc = jnp.swapaxes(dqbt, 1, 2), dkbt.T, dvbt.T, dkbct.T, dvbct.T
    dsink = jnp.sum(dsk[:, :, 0, 0], axis=1)

    dproj, dqg, dkg = _prep_bwd(dqa, dka, dva, dqb, dkb, dvb, proj, cos, sin, qg, kg, dgl, tm=tm, name="prep_bwd")
    grad_x, dscale1, dshift1 = _mm_nt4_mod_bwd(dproj, win4, dxz1, x, scale1, tm=tm, name="in_proj_bwd")
    ns_in = win4.shape[-1]
    win_spec = dict(b_spec=pl.BlockSpec((None, None, ns_in), lambda k, t: (0, 0, k)),
                    out_shape=_sds((N_CHIPS, D_MODEL, ns_in)),
                    out_spec=pl.BlockSpec((None, D_MODEL, ns_in), lambda k, t: (k, 0, 0)),
                    colsum_spec=pl.BlockSpec((8, ns_in), lambda k, t: (0, k)), colsum_shape=_sds((8, IN_COLS)))
    win_spec["b_spec"] = pl.BlockSpec((tm, ns_in), lambda k, t: (t, k))
    dw_in4, db_in = _mm_tn(x, dproj, a_spec=pl.BlockSpec((tm, D_MODEL), lambda k, t: (t, 0)), grid=(N_CHIPS, n_s),
                           mod=(shift1, scale1), name="dw_in", **win_spec)

    zq = jnp.zeros((N_HEADS, nc, LANES), F32)
    dproj_c, _, dkg_c = _prep_bwd(zq, dkac, dvac, zq, dkbc, dvbc, proj_c, cos_c, sin_c, qg, kg,
                                  jnp.zeros((nc, IN_COLS - OFF_GA), BF16), tm=nc, name="prep_bwd_ctx")
    _, dscale_c, dshift_c = _mm_nt4_mod_bwd(dproj_c, win4, jnp.zeros((nc, D_MODEL), F32), ctx, scale_c, tm=nc,
                                            name="in_proj_bwd_ctx")
    win_spec["b_spec"] = pl.BlockSpec((nc, ns_in), lambda k, t: (t, k))
    dw_in4, db_in_c = _mm_tn(ctx, dproj_c, a_spec=pl.BlockSpec((nc, D_MODEL), lambda k, t: (t, 0)), grid=(N_CHIPS, 1),
                             mod=(shift_c, scale_c), init=dw_in4, name="dw_in_ctx", **win_spec)

    dmod = jnp.concatenate([dshift1, dscale1, dgate1, dshift2, dscale2, dgate2], axis=1)
    dmodc = jnp.concatenate([dshift_c, dscale_c], axis=1)
    dmodc_pad = jnp.concatenate([dmodc, jnp.zeros((1, 4 * D_MODEL), F32)], axis=1)
    dmodc8 = _first_row(dmodc_pad).astype(BF16)
    z8 = jnp.zeros((8, D_MODEL), F32)
    dsilu_c, _, _ = _mm_nt4_mod_bwd(dmodc8, wmod4, z8, z8, zvec, tm=8, name="c_ctx_bwd")
    sg = _sigmoid(c_ctx)
    dc_ctx = dsilu_c[0:1] * sg * (1.0 + c_ctx * (1.0 - sg))

    dqn = jnp.sum(dqg.reshape(N_HEADS, HEAD_DIM), axis=0, keepdims=True)
    dkn = jnp.sum((dkg + dkg_c).reshape(N_KV, HEAD_DIM), axis=0, keepdims=True)
    grads = dict(
        w_in4=dw_in4, b_in=db_in[0:1] + db_in_c[0:1], sink=dsink, qn=dqn, kn=dkn, wba=dwba, wbb=dwbb, w_out=dw_out,
        ln1_g=dln1_g, ln1_b=dln1_b, w_up4=dw_up4, conv_w=dcw2.transpose(1, 0, 2).reshape(3, 2 * D_FF),
        conv_b=dcb2.reshape(1, 2 * D_FF), w_down=dw_down, ln2_g=dln2_g, ln2_b=dln2_b,
        c_ctx=dc_ctx, dmod=dmod, dmodc=dmodc)
    return loss, grad_x, grads


ANY = pl.BlockSpec(memory_space=pl.ANY)


def _mesh_pos():
    return lax.axis_index("x"), lax.axis_index("y"), lax.axis_index("c")


def _other_chips(x, y):
    return [(1 - x, y), (x, 1 - y), (1 - x, 1 - y)]


def _remote(src, dst, send, recv, dev):
    return pltpu.make_async_remote_copy(src_ref=src, dst_ref=dst, send_sem=send, recv_sem=recv, device_id=dev,
                                        device_id_type=MESH)


def _gather_shards(arrs, small):
    na = len(arrs)
    halves = [a.shape[0] // 2 for a in arrs]

    def body(*refs):
        ins, small_ref = refs[:na], refs[na]
        outs, small_out = refs[na + 1:2 * na + 1], refs[2 * na + 1]
        send, recv, loc = refs[2 * na + 2:]
        x, y, c = _mesh_pos()
        me = 2 * x + y
        chips = _other_chips(x, y)

        def half(a, cc):
            return pl.ds(cc * halves[a], halves[a])

        local = [pltpu.make_async_copy(ins[a], outs[a].at[me], loc.at[a]) for a in range(na)]
        local.append(pltpu.make_async_copy(small_ref, small_out.at[me], loc.at[na]))
        for cp in local:
            cp.start()
        sends = []
        for j, chip in enumerate(chips):
            for a in range(na):
                sends.append(_remote(ins[a].at[half(a, c)], outs[a].at[me, half(a, c)], send.at[a, j], recv.at[a, j],
                                     (*chip, c)))
            sends.append(_remote(small_ref, small_out.at[me], send.at[na, j], recv.at[na, j], (*chip, c)))
        for cp in sends:
            cp.start()
        for j, chip in enumerate(chips):
            kj = 2 * chip[0] + chip[1]
            for a in range(na):
                landed = outs[a].at[kj, half(a, c)]
                _remote(landed, landed, send.at[a, j], recv.at[a, j], (*chip, c)).wait_recv()
                fwd = _remote(landed, landed, send.at[a, 3 + j], recv.at[a, 3 + j], (x, y, 1 - c))
                fwd.start()
                sends.append(fwd)
            _remote(small_ref, small_out.at[kj], send.at[na, j], recv.at[na, j], (*chip, c)).wait_recv()
        for j, chip in enumerate(chips):
            kj = 2 * chip[0] + chip[1]
            for a in range(na):
                other = outs[a].at[kj, half(a, 1 - c)]
                _remote(other, other, send.at[a, 3 + j], recv.at[a, 3 + j], (x, y, 1 - c)).wait_recv()
        for cp in sends:
            cp.wait_send()
        for cp in local:
            cp.wait()

    out_shape = [_sds((N_CHIPS,) + a.shape, a.dtype) for a in arrs] + [_sds((N_CHIPS,) + small.shape, small.dtype)]
    return pl.pallas_call(
        body, name="gather_shards", in_specs=[ANY] * (na + 1), out_specs=[ANY] * (na + 1), out_shape=out_shape,
        scratch_shapes=[pltpu.SemaphoreType.DMA((na + 1, 6)), pltpu.SemaphoreType.DMA((na + 1, 6)),
                        pltpu.SemaphoreType.DMA((na + 1,))],
    )(*arrs, small)


def _allgather_rows(v):
    r, n = v.shape

    def body(v_ref, out_ref, send, recv, loc):
        x, y, c = _mesh_pos()
        me, sibling = (x, y, c), (x, y, 1 - c)
        chips = _other_chips(x, y)

        def rows(px, py, pc):
            return out_ref.at[4 * px + 2 * py + pc]

        def copy(k, block, to, src=None):
            return _remote(rows(*block) if src is None else src, rows(*block), send.at[k], recv.at[k], to)

        mine = pltpu.make_async_copy(v_ref, rows(*me), loc)
        mine.start()
        first = [copy(0, me, sibling, src=v_ref)] + [copy(1 + j, me, (*chip, c), src=v_ref) for j, chip in enumerate(chips)]
        for cp in first:
            cp.start()
        passed = [copy(4 + j, (*chip, c), sibling) for j, chip in enumerate(chips)]
        for j, chip in enumerate(chips):
            copy(1 + j, (*chip, c), me).wait_recv()
            passed[j].start()
        copy(0, sibling, me).wait_recv()
        for j, chip in enumerate(chips):
            copy(4 + j, (*chip, 1 - c), me).wait_recv()
        for cp in first + passed:
            cp.wait_send()
        mine.wait()

    return pl.pallas_call(
        body, name="allgather_rows", in_specs=[pl.BlockSpec(memory_space=pltpu.VMEM)],
        out_specs=pl.BlockSpec(memory_space=pltpu.VMEM), out_shape=_sds((N_DEV, r, n), v.dtype),
        scratch_shapes=[pltpu.SemaphoreType.DMA((7,)), pltpu.SemaphoreType.DMA((7,)), pltpu.SemaphoreType.DMA],
    )(v)


def _swap_other_half(g):
    nb, r, n = g.shape
    rh = r // 2

    def body(g_ref, out_ref, send, recv):
        x, y, c = _mesh_pos()
        cp = _remote(g_ref.at[:, pl.ds((1 - c) * rh, rh), :], out_ref, send, recv, (x, y, 1 - c))
        cp.start()
        cp.wait()

    return pl.pallas_call(
        body, name="swap_other_half", in_specs=[ANY], out_specs=ANY, out_shape=_sds((nb, rh, n), g.dtype),
        scratch_shapes=[pltpu.SemaphoreType.DMA, pltpu.SemaphoreType.DMA],
    )(g)


def _scatter_to_chips(p):
    def body(p_ref, out_ref, send, recv, loc):
        x, y, c = _mesh_pos()
        me = 2 * x + y
        chips = _other_chips(x, y)
        mine = pltpu.make_async_copy(p_ref.at[me], out_ref.at[me], loc)
        mine.start()
        sends = [_remote(p_ref.at[2 * chip[0] + chip[1]], out_ref.at[me], send.at[j], recv.at[j], (*chip, c))
                 for j, chip in enumerate(chips)]
        for cp in sends:
            cp.start()
        for j, chip in enumerate(chips):
            kj = 2 * chip[0] + chip[1]
            _remote(p_ref.at[kj], out_ref.at[kj], send.at[j], recv.at[j], (*chip, c)).wait_recv()
        for cp in sends:
            cp.wait_send()
        mine.wait()

    return pl.pallas_call(
        body, name="scatter_to_chips", in_specs=[ANY], out_specs=ANY, out_shape=_sds(p.shape, p.dtype),
        scratch_shapes=[pltpu.SemaphoreType.DMA((3,)), pltpu.SemaphoreType.DMA((3,)), pltpu.SemaphoreType.DMA],
    )(p)


def _join_halves(f):
    def body(f_ref, out_ref, send, recv, loc):
        x, y, c = _mesh_pos()
        mine = pltpu.make_async_copy(f_ref, out_ref.at[c], loc)
        mine.start()
        cp = _remote(f_ref, out_ref.at[c], send, recv, (x, y, 1 - c))
        cp.start()
        _remote(f_ref, out_ref.at[1 - c], send, recv, (x, y, 1 - c)).wait_recv()
        cp.wait_send()
        mine.wait()

    return pl.pallas_call(
        body, name="join_halves", in_specs=[ANY], out_specs=ANY, out_shape=_sds((2,) + f.shape, f.dtype),
        scratch_shapes=[pltpu.SemaphoreType.DMA, pltpu.SemaphoreType.DMA, pltpu.SemaphoreType.DMA],
    )(f)


def _row_tile(rows, cap=512):
    t = cap - cap % 8
    while rows % t:
        t -= 8
    return t


def _add_blocks(a, b):
    nb, r, n = a.shape
    tr = _row_tile(r)

    def body(a_ref, b_ref, o_ref):
        o_ref[...] = a_ref[...] + b_ref[...]

    spec = pl.BlockSpec((None, tr, n), lambda k, i: (k, i, 0))
    return _call(body, name="add_blocks", grid=(nb, r // tr), in_specs=[spec, spec], out_specs=spec,
                 out_shape=_sds(a.shape), sem=("parallel", "parallel"))(a, b)


def _sum_leading(a, *, name):
    nk, r, n = a.shape
    tr = _row_tile(r)

    def body(a_ref, o_ref):
        acc = a_ref[0]
        for k in range(1, nk):
            acc = acc + a_ref[k]
        o_ref[...] = acc

    return _call(body, name=name, grid=(r // tr,), in_specs=[pl.BlockSpec((nk, tr, n), lambda i: (0, i, 0))],
                 out_specs=pl.BlockSpec((tr, n), lambda i: (i, 0)), out_shape=_sds((r, n)), sem=("parallel",))(a)


def _silu_outer(a, b):
    kdim, n = a.shape[1], b.shape[1]

    def body(a_ref, b_ref, o_ref):
        av = a_ref[...]
        av = av * _sigmoid(av)
        bv = b_ref[...]
        ah, bh = av.astype(BF16), bv.astype(BF16)
        al, bl = (av - ah.astype(F32)).astype(BF16), (bv - bh.astype(F32)).astype(BF16)
        o_ref[...] = _dot_tn(ah, bh) + (_dot_tn(ah, bl) + _dot_tn(al, bh))

    return _call(body, name="dw_mod", grid=(1,), in_specs=[_full(a.shape), _full(b.shape)], out_specs=_full((kdim, n)),
                 out_shape=_sds((kdim, n)))(a, b)


def _adamw(w, g, m, v):
    r, n = w.shape
    tr = _row_tile(r)

    def body(w_ref, g_ref, m_ref, v_ref, d_ref, nm_ref, nv_ref):
        gv = g_ref[...]
        nm = ADAM_B1 * m_ref[...] + (1.0 - ADAM_B1) * gv
        nv = ADAM_B2 * v_ref[...] + (1.0 - ADAM_B2) * (gv * gv)
        m_hat = nm / (1.0 - ADAM_B1 ** ADAM_STEP)
        v_hat = nv / (1.0 - ADAM_B2 ** ADAM_STEP)
        d_ref[...] = -ADAM_LR * (m_hat / (jnp.sqrt(v_hat) + ADAM_EPS) + ADAM_WD * w_ref[...])
        nm_ref[...] = nm
        nv_ref[...] = nv

    spec = pl.BlockSpec((tr, n), lambda i: (i, 0))
    return _call(body, name="adamw", grid=(r // tr,), in_specs=[spec] * 4, out_specs=[spec] * 3,
                 out_shape=[_sds((r, n))] * 3, sem=("parallel",))(w, g, m, v)


BIG = ("w_in", "w_branch_a", "w_branch_b", "w_out", "w_up", "w_down", "conv_w")
BIG_ROWS = 3584
SMALL = ("b_mod", "b_in", "conv_b", "ln1_g", "ln1_b", "ln2_g", "ln2_b", "c_ctx", "attn_sink", "q_norm_g", "k_norm_g")
SMALL_ROWS = 8 * len(SMALL)


def _rows(a, n_rows):
    flat = a.reshape(-1)
    return jnp.pad(flat, (0, n_rows * D_MODEL - flat.shape[0])).reshape(n_rows, D_MODEL)


def _group8(a):
    return _rep8(_rows(a, 1)) if a.size <= D_MODEL else _rows(a, 8)


def _ungroup8(p, shape):
    size = math.prod(shape)
    return (p[0, :size] if size <= D_MODEL else p.reshape(-1)[:size]).reshape(shape)


def _pack_big(t):
    parts = [t[n].reshape(-1, D_MODEL) for n in BIG[:-1]] + [_rows(t["conv_w"], 8)]
    used = sum(p.shape[0] for p in parts)
    return jnp.concatenate(parts + [jnp.zeros((BIG_ROWS - used, D_MODEL), F32)], axis=0)


def _unpack_big(p, like):
    out, r = {}, 0
    for n in BIG:
        size = math.prod(like[n].shape)
        nr = size // D_MODEL if n != "conv_w" else 8
        out[n] = p[r:r + nr].reshape(-1)[:size].reshape(like[n].shape)
        r += nr
    return out


def _pack_small(t):
    return jnp.concatenate([_group8(t[n]) for n in SMALL], axis=0)


def _unpack_small(p, like):
    return {n: _ungroup8(p[8 * i:8 * i + 8], like[n].shape) for i, n in enumerate(SMALL)}


WEIGHTS = ("c_ctx", "w_mod", "b_mod", "w_in", "b_in", "attn_sink", "q_norm_g", "k_norm_g", "w_branch_a", "w_branch_b",
           "w_out", "ln1_g", "ln1_b", "w_up", "conv_w", "conv_b", "w_down", "ln2_g", "ln2_b")


def kernel(x, c, ctx, c_ctx, w_mod, b_mod, w_in, b_in, attn_sink, q_norm_g, k_norm_g, w_branch_a, w_branch_b, w_out, ln1_g, ln1_b, w_up, conv_w, conv_b, w_down, ln2_g, ln2_b, loss_target, m_c_ctx, m_w_mod, m_b_mod, m_w_in, m_b_in, m_attn_sink, m_q_norm_g, m_k_norm_g, m_w_branch_a, m_w_branch_b, m_w_out, m_ln1_g, m_ln1_b, m_w_up, m_conv_w, m_conv_b, m_w_down, m_ln2_g, m_ln2_b, v_c_ctx, v_w_mod, v_b_mod, v_w_in, v_b_in, v_attn_sink, v_q_norm_g, v_k_norm_g, v_w_branch_a, v_w_branch_b, v_w_out, v_ln1_g, v_ln1_b, v_w_up, v_conv_w, v_conv_b, v_w_down, v_ln2_g, v_ln2_b):
    w = dict(c_ctx=c_ctx, w_mod=w_mod, b_mod=b_mod, w_in=w_in, b_in=b_in, attn_sink=attn_sink, q_norm_g=q_norm_g,
             k_norm_g=k_norm_g, w_branch_a=w_branch_a, w_branch_b=w_branch_b, w_out=w_out, ln1_g=ln1_g, ln1_b=ln1_b,
             w_up=w_up, conv_w=conv_w, conv_b=conv_b, w_down=w_down, ln2_g=ln2_g, ln2_b=ln2_b)
    m = dict(c_ctx=m_c_ctx, w_mod=m_w_mod, b_mod=m_b_mod, w_in=m_w_in, b_in=m_b_in, attn_sink=m_attn_sink,
             q_norm_g=m_q_norm_g, k_norm_g=m_k_norm_g, w_branch_a=m_w_branch_a, w_branch_b=m_w_branch_b, w_out=m_w_out,
             ln1_g=m_ln1_g, ln1_b=m_ln1_b, w_up=m_w_up, conv_w=m_conv_w, conv_b=m_conv_b, w_down=m_w_down,
             ln2_g=m_ln2_g, ln2_b=m_ln2_b)
    v = dict(c_ctx=v_c_ctx, w_mod=v_w_mod, b_mod=v_b_mod, w_in=v_w_in, b_in=v_b_in, attn_sink=v_attn_sink,
             q_norm_g=v_q_norm_g, k_norm_g=v_k_norm_g, w_branch_a=v_w_branch_a, w_branch_b=v_w_branch_b, w_out=v_w_out,
             ln1_g=v_ln1_g, ln1_b=v_ln1_b, w_up=v_w_up, conv_w=v_conv_w, conv_b=v_conv_b, w_down=v_w_down,
             ln2_g=v_ln2_g, ln2_b=v_ln2_b)
    xp, yp, _ = _mesh_pos()
    me = 2 * xp + yp

    branches = jnp.concatenate([w_branch_a[0], w_branch_b[0]], axis=0)
    wide = jnp.concatenate([w_mod[0], w_in[0], w_up[0], branches], axis=1).astype(BF16)
    tall = jnp.concatenate([w_out[0], w_down[0]], axis=0).astype(BF16)
    wide4, tall4, cw4 = _gather_shards([wide, tall], conv_w[0])
    n_mod, n_in, n_up = w_mod.shape[-1], w_in.shape[-1], w_up.shape[-1]
    wmod4 = wide4[:, :, :n_mod]
    win4 = wide4[:, :, n_mod:n_mod + n_in]
    wup4 = wide4[:, :, n_mod + n_in:n_mod + n_in + n_up]
    br4 = wide4[:, :, n_mod + n_in + n_up:]
    n_br = w_branch_a.shape[1]
    wba = br4[:, :n_br].transpose(1, 0, 2).reshape(n_br, D_MODEL)
    wbb = br4[:, n_br:].transpose(1, 0, 2).reshape(n_br, D_MODEL)
    n_out = w_out.shape[1]
    w_out_full = tall4[:, :n_out].reshape(D_MODEL, D_MODEL)
    w_down_full = tall4[:, n_out:].reshape(D_FF, D_MODEL)
    cw_full = cw4.transpose(1, 0, 2).reshape(3, 2 * D_FF)

    loss, grad_x, g = _local_step(
        x[0], c, ctx[0], c_ctx[None], wmod4, b_mod, win4, b_in, attn_sink[0], q_norm_g, k_norm_g, wba, wbb, w_out_full,
        ln1_g, ln1_b, wup4, cw_full, conv_b, w_down_full, ln2_g, ln2_b, loss_target[0])
    loss = lax.psum(loss, ("x", "y", "c"))

    sent = dict(c=c, dmod=g["dmod"], dmodc=g["dmodc"], b_in=g["b_in"], conv_b=g["conv_b"], ln1_g=g["ln1_g"],
                ln1_b=g["ln1_b"], ln2_g=g["ln2_g"], ln2_b=g["ln2_b"], c_ctx=g["c_ctx"], attn_sink=g["sink"],
                q_norm_g=g["qn"], k_norm_g=g["kn"])
    every = _allgather_rows(jnp.concatenate([_group8(a) for a in sent.values()], axis=0))
    total = _sum_leading(every, name="sum_devices")
    slot = {n: slice(8 * i, 8 * i + 8) for i, n in enumerate(sent)}
    gs = {n: _ungroup8(total[slot[n]], sent[n].shape) for n in SMALL if n in sent}
    dmodc_sum = jnp.concatenate([_ungroup8(total[slot["dmodc"]], (1, 2 * D_MODEL)), jnp.zeros((1, 4 * D_MODEL), F32)],
                                axis=1)
    gs["b_mod"] = _ungroup8(total[slot["dmod"]], b_mod.shape) + dmodc_sum
    acts = jnp.concatenate([every[:, slot["c"].start], _rep8(c_ctx)], axis=0)
    dmods = jnp.concatenate([every[:, slot["dmod"]].reshape(N_DEV, -1)[:, :6 * D_MODEL], _first_row(dmodc_sum)], axis=0)
    g_w_mod = _silu_outer(acts, lax.dynamic_slice_in_dim(dmods, me * n_mod, n_mod, axis=1))

    cw_g4 = _to_blocks4(g["conv_w"])
    packed = jnp.concatenate([
        g["w_in4"].reshape(N_CHIPS, -1, D_MODEL), _to_blocks4(g["wba"]).reshape(N_CHIPS, -1, D_MODEL),
        _to_blocks4(g["wbb"]).reshape(N_CHIPS, -1, D_MODEL), g["w_out"].reshape(N_CHIPS, -1, D_MODEL),
        g["w_up4"].reshape(N_CHIPS, -1, D_MODEL), g["w_down"].reshape(N_CHIPS, -1, D_MODEL),
        jnp.pad(cw_g4.reshape(N_CHIPS, -1), ((0, 0), (0, 8 * D_MODEL - cw_g4.shape[1] * cw_g4.shape[2]))).reshape(
            N_CHIPS, 8, D_MODEL),
        jnp.zeros((N_CHIPS, BIG_ROWS - 3528, D_MODEL), F32)], axis=1)
    rh = BIG_ROWS // 2
    cpos = lax.axis_index("c")
    my_half = lax.dynamic_slice_in_dim(packed, cpos * rh, rh, axis=1)
    chip_sum = _add_blocks(my_half, _swap_other_half(packed))
    half_sum = _sum_leading(_scatter_to_chips(chip_sum), name="sum_chips")
    g_big = _unpack_big(_join_halves(half_sum).reshape(BIG_ROWS, D_MODEL), w)

    grads = dict(gs, w_mod=g_w_mod, **g_big)

    def pack_all(t):
        rows = jnp.concatenate([_pack_big(t), t["w_mod"].reshape(-1, D_MODEL), _pack_small(t)], axis=0)
        return jnp.pad(rows, ((0, -rows.shape[0] % 256), (0, 0)))

    delta_p, new_m_p, new_v_p = _adamw(pack_all(w), pack_all(grads), pack_all(m), pack_all(v))

    def unpack_all(p):
        r_mod = BIG_ROWS + w_mod.size // D_MODEL
        out = _unpack_big(p[:BIG_ROWS], w)
        out["w_mod"] = p[BIG_ROWS:r_mod].reshape(w_mod.shape)
        out.update(_unpack_small(p[r_mod:r_mod + SMALL_ROWS], w))
        return out

    grads = {n: grads[n].reshape(w[n].shape) for n in WEIGHTS}
    delta, new_m, new_v = unpack_all(delta_p), unpack_all(new_m_p), unpack_all(new_v_p)
    return (loss, grad_x[None], *[grads[n] for n in WEIGHTS], *[delta[n] for n in WEIGHTS],
            *[new_m[n] for n in WEIGHTS], *[new_v[n] for n in WEIGHTS])
```

```python
import functools
import math

import jax
import jax.numpy as jnp
from jax import lax
from jax.experimental import pallas as pl
from jax.experimental.pallas import tpu as pltpu

F32 = jnp.float32
BF16 = jnp.bfloat16

D_MODEL = 1024
HEAD_DIM = 64
N_HEADS = 8
N_KV = 2
WINDOW = 128
GRID_W = 64
ROPE_THETA = 10000.0
D_FF = 2816
LN_EPS = 1e-5
QK_EPS = 1e-6
ALPHA = 2.0 ** 0.25
Q_SCALE = HEAD_DIM ** -0.5
OFF_GA = 1536
IN_COLS = 3584
ADAM_LR, ADAM_B1, ADAM_B2, ADAM_EPS, ADAM_WD, ADAM_STEP = 0.001, 0.9, 0.999, 1e-8, 0.01, 10

LANES = 128
VMEM_BUDGET = 52 * 1024 * 1024
N_CHIPS = 4
N_DEV = 8
NEG = -1e30
MESH = pl.DeviceIdType.MESH


def _sigmoid(x):
    return 1.0 / (1.0 + jnp.exp(-x))


def _dot(a, b):
    return jnp.dot(a, b, preferred_element_type=F32)


def _dot_nt(a, b):
    return lax.dot_general(a, b, (((1,), (1,)), ((), ())), preferred_element_type=F32)


def _dot_tn(a, b):
    return lax.dot_general(a, b, (((0,), (0,)), ((), ())), preferred_element_type=F32)


def _call(body, *, name, grid, in_specs, out_specs, out_shape, scratch=(), sem=None, **kw):
    params = dict(vmem_limit_bytes=VMEM_BUDGET)
    if sem is not None:
        params["dimension_semantics"] = sem
    return pl.pallas_call(body, name=name, grid=grid, in_specs=in_specs, out_specs=out_specs,
                          out_shape=out_shape, scratch_shapes=list(scratch),
                          compiler_params=pltpu.CompilerParams(**params), **kw)


def _full(shape):
    n = len(shape)
    return pl.BlockSpec(shape, lambda *_: (0,) * n)


def _sds(shape, dtype=F32):
    return jax.ShapeDtypeStruct(shape, dtype)


def _mm_nn4(a, shift, scale, w4, bias, *, mode, split_out, out_dtype, tm, name):
    m, kdim = a.shape
    nb, _, ns = w4.shape

    def body(a_ref, sh_ref, sc_ref, w_ref, b_ref, o_ref):
        av = a_ref[...]
        if mode == "modulate":
            av = av * (1.0 + sc_ref[...]) + sh_ref[...]
        else:
            av = av * _sigmoid(av)
        o_ref[...] = (_dot(av.astype(BF16), w_ref[...]) + b_ref[...]).astype(out_dtype)

    if split_out:
        out_shape = _sds((2, m, 2 * ns), out_dtype)
        out_spec = pl.BlockSpec((None, tm, ns), lambda i, k: (k // 2, i, k % 2))
    else:
        out_shape = _sds((m, nb * ns), out_dtype)
        out_spec = pl.BlockSpec((tm, ns), lambda i, k: (i, k))
    return _call(
        body, name=name, grid=(m // tm, nb),
        in_specs=[pl.BlockSpec((tm, kdim), lambda i, k: (i, 0)),
                  pl.BlockSpec((1, kdim), lambda i, k: (0, 0)),
                  pl.BlockSpec((1, kdim), lambda i, k: (0, 0)),
                  pl.BlockSpec((None, kdim, ns), lambda i, k: (k, 0, 0)),
                  pl.BlockSpec((1, ns), lambda i, k: (0, k))],
        out_specs=out_spec, out_shape=out_shape, sem=("parallel", "arbitrary"),
    )(a, shift, scale, w4, bias)


def _mm_tn(a, b, *, a_spec, b_spec, grid, out_shape, out_spec, name, mod=None, init=None, colsum_spec=None,
           colsum_shape=None):
    red = len(grid) - 1
    has_mod, has_init, has_cs = mod is not None, init is not None, colsum_spec is not None

    def body(*refs):
        refs = list(refs)
        a_ref, b_ref = refs[0], refs[1]
        pos = 2
        if has_mod:
            sh_ref, sc_ref = refs[2], refs[3]
            pos = 4
        if has_init:
            init_ref = refs[pos]
            pos += 1
        o_ref = refs[pos]
        cs_ref = refs[pos + 1] if has_cs else None
        s = pl.program_id(red)

        @pl.when(s == 0)
        def _():
            o_ref[...] = init_ref[...] if has_init else jnp.zeros(o_ref.shape, F32)
            if has_cs:
                cs_ref[...] = jnp.zeros(cs_ref.shape, F32)

        av = a_ref[...]
        if has_mod:
            av = av * (1.0 + sc_ref[...]) + sh_ref[...]
        bv = b_ref[...]
        o_ref[...] += _dot_tn(av.astype(BF16), bv)
        if has_cs:
            cs_ref[...] += jnp.broadcast_to(jnp.sum(bv.astype(F32), axis=0, keepdims=True), cs_ref.shape)

    ins, in_specs = [a, b], [a_spec, b_spec]
    if has_mod:
        kdim = mod[0].shape[-1]
        ins += list(mod)
        in_specs += [_full((1, kdim)), _full((1, kdim))]
    if has_init:
        ins.append(init)
        in_specs.append(out_spec)
    out_specs, out_shapes = out_spec, out_shape
    if has_cs:
        out_specs, out_shapes = [out_spec, colsum_spec], [out_shape, colsum_shape]
    sem = ("parallel",) * red + ("arbitrary",)
    return _call(body, name=name, grid=grid, in_specs=in_specs, out_specs=out_specs, out_shape=out_shapes,
                 sem=sem)(*ins)


def _rope_tables(n_tok):
    pos = jnp.arange(n_tok, dtype=jnp.int32)
    rows = (pos // GRID_W).astype(F32)
    cols = (pos % GRID_W).astype(F32)
    n_freq = HEAD_DIM // 4
    inv_freq = ROPE_THETA ** (-jnp.arange(n_freq, dtype=F32) / n_freq)
    ang_r = rows[:, None] * inv_freq
    ang_c = cols[:, None] * inv_freq
    cos = jnp.concatenate([jnp.cos(ang_r)] * 2 + [jnp.cos(ang_c)] * 2, axis=-1)
    sin = jnp.concatenate([-jnp.sin(ang_r), jnp.sin(ang_r), -jnp.sin(ang_c), jnp.sin(ang_c)], axis=-1)
    return jnp.tile(cos, (1, 2)), jnp.tile(sin, (1, 2))


def _lane(shape):
    return lax.broadcasted_iota(jnp.int32, shape, 1)


def _rope_partner(t, lane):
    return jnp.where((lane % 32) < 16, pltpu.roll(t, LANES - 16, 1), pltpu.roll(t, 16, 1))


def _half_mean(s, lane):
    lo = jnp.sum(jnp.where(lane < HEAD_DIM, s, 0.0), axis=-1, keepdims=True)
    hi = jnp.sum(jnp.where(lane < HEAD_DIM, 0.0, s), axis=-1, keepdims=True)
    return jnp.where(lane < HEAD_DIM, lo, hi) * (1.0 / HEAD_DIM)


def _prep(proj, cos, sin, qg, kg, *, tm, name):
    m = proj.shape[0]

    def body(p_ref, cos_ref, sin_ref, qg_ref, kg_ref, qa_ref, ka_ref, va_ref, qb_ref, kb_ref, vb_ref):
        lane = _lane((tm, LANES))
        cosv, sinv = cos_ref[...], sin_ref[...]
        low = lane < HEAD_DIM

        def rope(t):
            return t * cosv + _rope_partner(t, lane) * sinv

        def rms(t, g):
            return t * lax.rsqrt(_half_mean(t * t, lane) + QK_EPS) * g

        def place(q_ref, j, chunk):
            sw = pltpu.roll(chunk, HEAD_DIM, 1)
            if j < 2:
                h0, h1 = jnp.where(low, chunk, 0.0), jnp.where(low, sw, 0.0)
            else:
                h0, h1 = jnp.where(low, 0.0, sw), jnp.where(low, 0.0, chunk)
            q_ref[2 * j] = h0.astype(BF16)
            q_ref[2 * j + 1] = h1.astype(BF16)

        for j in range(4):
            place(qa_ref, j, rope(p_ref[:, j * LANES:(j + 1) * LANES]) * Q_SCALE)
            place(qb_ref, j, rope(rms(p_ref[:, 768 + j * LANES:768 + (j + 1) * LANES], qg_ref[...])) * Q_SCALE)
        ka_ref[...] = rope(p_ref[:, 512:640]).astype(BF16)
        va_ref[...] = p_ref[:, 640:768].astype(BF16)
        kb_ref[...] = rope(rms(p_ref[:, 1280:1408], kg_ref[...])).astype(BF16)
        vb_ref[...] = p_ref[:, 1408:1536].astype(BF16)

    row = pl.BlockSpec((tm, LANES), lambda i: (i, 0))
    qspec = pl.BlockSpec((N_HEADS, tm, LANES), lambda i: (0, i, 0))
    return _call(
        body, name=name, grid=(m // tm,),
        in_specs=[pl.BlockSpec((tm, OFF_GA), lambda i: (i, 0)), row, row, _full((1, LANES)), _full((1, LANES))],
        out_specs=[qspec, row, row, qspec, row, row],
        out_shape=[_sds((N_HEADS, m, LANES), BF16), _sds((m, LANES), BF16), _sds((m, LANES), BF16),
                   _sds((N_HEADS, m, LANES), BF16), _sds((m, LANES), BF16), _sds((m, LANES), BF16)],
        sem=("parallel",),
    )(proj, cos, sin, qg, kg)


def _prep_bwd(dqa, dka, dva, dqb, dkb, dvb, proj, cos, sin, qg, kg, dgl, *, tm, name):
    m = proj.shape[0]

    def body(dqa_ref, dka_ref, dva_ref, dqb_ref, dkb_ref, dvb_ref, p_ref, cos_ref, sin_ref, qg_ref, kg_ref,
             dgl_ref, dp_ref, dqg_ref, dkg_ref):
        i = pl.program_id(0)
        lane = _lane((tm, LANES))
        cosv, sinv = cos_ref[...], sin_ref[...]
        low = lane < HEAD_DIM

        @pl.when(i == 0)
        def _():
            dqg_ref[...] = jnp.zeros(dqg_ref.shape, F32)
            dkg_ref[...] = jnp.zeros(dkg_ref.shape, F32)

        def unrope(d):
            return d * cosv - _rope_partner(d, lane) * sinv

        def unplace(dq_ref, j):
            d0, d1 = dq_ref[2 * j], dq_ref[2 * j + 1]
            if j < 2:
                return jnp.where(low, d0, pltpu.roll(d1, HEAD_DIM, 1))
            return jnp.where(low, pltpu.roll(d0, HEAD_DIM, 1), d1)

        def unrms(dtn, t, g):
            r = lax.rsqrt(_half_mean(t * t, lane) + QK_EPS)
            u = dtn * g
            dt = r * u - t * (r * r * r) * _half_mean(u * t, lane)
            return dt, jnp.sum(dtn * t * r, axis=0, keepdims=True)

        for j in range(4):
            dp_ref[:, j * LANES:(j + 1) * LANES] = (unrope(unplace(dqa_ref, j)) * Q_SCALE).astype(BF16)
            c0 = 768 + j * LANES
            dt, dg = unrms(unrope(unplace(dqb_ref, j)) * Q_SCALE, p_ref[:, c0:c0 + LANES], qg_ref[...])
            dp_ref[:, c0:c0 + LANES] = dt.astype(BF16)
            dqg_ref[:, j * LANES:(j + 1) * LANES] += dg
        dp_ref[:, 512:640] = unrope(dka_ref[...]).astype(BF16)
        dp_ref[:, 640:768] = dva_ref[...].astype(BF16)
        dt, dg = unrms(unrope(dkb_ref[...]), p_ref[:, 1280:1408], kg_ref[...])
        dp_ref[:, 1280:1408] = dt.astype(BF16)
        dkg_ref[...] += dg
        dp_ref[:, 1408:1536] = dvb_ref[...].astype(BF16)
        dp_ref[:, OFF_GA:] = dgl_ref[...]

    row = pl.BlockSpec((tm, LANES), lambda i: (i, 0))
    qspec = pl.BlockSpec((N_HEADS, tm, LANES), lambda i: (0, i, 0))
    return _call(
        body, name=name, grid=(m // tm,),
        in_specs=[qspec, row, row, qspec, row, row, pl.BlockSpec((tm, OFF_GA), lambda i: (i, 0)), row, row,
                  _full((1, LANES)), _full((1, LANES)), pl.BlockSpec((tm, IN_COLS - OFF_GA), lambda i: (i, 0))],
        out_specs=[pl.BlockSpec((tm, IN_COLS), lambda i: (i, 0)), _full((1, 512)), _full((1, LANES))],
        out_shape=[_sds((m, IN_COLS), BF16), _sds((1, 512)), _sds((1, LANES))],
        sem=("arbitrary",),
    )(dqa, dka, dva, dqb, dkb, dvb, proj, cos, sin, qg, kg, dgl)


def _attn_glob_fwd(qt, k, vt, kc, vct, *, tq, tk):
    nh, _, s = qt.shape
    nc = kc.shape[0]

    def body(qt_ref, k_ref, vt_ref, kc_ref, vct_ref, ot_ref, lse_ref, acc_sc, st_sc):
        qtv = qt_ref[...]
        acc_sc[...] = jnp.zeros(acc_sc.shape, F32)
        n_chunks = s // tk

        def update(st, vtv, m_old, l_old):
            m_new = jnp.maximum(m_old, jnp.max(st, axis=0, keepdims=True))
            pt = jnp.exp(st - m_new)
            al = jnp.exp(m_old - m_new)
            acc_sc[...] = acc_sc[...] * al + _dot(vtv, pt.astype(BF16))
            return m_new, l_old * al + jnp.sum(pt, axis=0, keepdims=True)

        def loop(c, carry):
            off = pl.multiple_of(c * tk, tk)
            nxt = pl.multiple_of(jnp.minimum(c + 1, n_chunks - 1) * tk, tk)
            st = st_sc[...]
            st_next = _dot(k_ref[pl.ds(nxt, tk), :], qtv)
            carry = update(st, vt_ref[:, pl.ds(off, tk)], *carry)
            st_sc[...] = st_next
            return carry

        init = (jnp.full((1, tq), NEG, F32), jnp.zeros((1, tq), F32))
        st_sc[...] = _dot(k_ref[pl.ds(0, tk), :], qtv)
        m, l = lax.fori_loop(0, n_chunks, loop, update(_dot(kc_ref[...], qtv), vct_ref[...], *init))
        ot_ref[...] = (acc_sc[...] / l).astype(BF16)
        lse_ref[...] = m + jnp.log(l)

    return _call(
        body, name="attn_glob_fwd", grid=(nh, s // tq),
        in_specs=[pl.BlockSpec((None, LANES, tq), lambda h, i: (h, 0, i)),
                  _full((s, LANES)), _full((LANES, s)), _full((nc, LANES)), _full((LANES, nc))],
        out_specs=[pl.BlockSpec((None, LANES, tq), lambda h, i: (h, 0, i)),
                   pl.BlockSpec((None, 1, tq), lambda h, i: (h, 0, i))],
        out_shape=[_sds((nh, LANES, s), BF16), _sds((nh, 1, s))],
        scratch=[pltpu.VMEM((LANES, tq), F32), pltpu.VMEM((tk, tq), F32)],
        sem=("parallel", "parallel"),
    )(qt, k, vt, kc, vct)


def _attn_glob_bwd(qt, dot, ot, lse, k, kt, v, kc, kct, vc, *, tq, tk):
    nh, _, s = qt.shape
    nc = kc.shape[0]
    n_q = s // tq

    def body(qt_ref, dot_ref, ot_ref, lse_ref, k_ref, kt_ref, v_ref, kc_ref, kct_ref, vc_ref,
             dqt_ref, dkt_ref, dvt_ref, dkct_ref, dvct_ref, acc_sc, st_sc, dp_sc, dkt_sc, dvt_sc):
        h, i = pl.program_id(0), pl.program_id(1)

        @pl.when(jnp.logical_and(h == 0, i == 0))
        def _():
            dkct_ref[...] = jnp.zeros(dkct_ref.shape, F32)
            dvct_ref[...] = jnp.zeros(dvct_ref.shape, F32)
            dkt_sc[...] = jnp.zeros(dkt_sc.shape, F32)
            dvt_sc[...] = jnp.zeros(dvt_sc.shape, F32)

        qtv, dotv, lse = qt_ref[...], dot_ref[...], lse_ref[...]
        delta = jnp.sum(dotv.astype(F32) * ot_ref[...].astype(F32), axis=0, keepdims=True)
        n_chunks = s // tk

        def grads(st, dpt):
            pt = jnp.exp(st - lse)
            return pt.astype(BF16), (pt * (dpt - delta)).astype(BF16)

        def loop(c, carry):
            off = pl.multiple_of(c * tk, tk)
            nxt = pl.multiple_of(jnp.minimum(c + 1, n_chunks - 1) * tk, tk)
            st, dpt = st_sc[...], dp_sc[...]
            st_next = _dot(k_ref[pl.ds(nxt, tk), :], qtv)
            dp_next = _dot(v_ref[pl.ds(nxt, tk), :], dotv)
            pb, dsb = grads(st, dpt)
            acc_sc[...] += _dot(kt_ref[:, pl.ds(off, tk)], dsb)
            dkt_sc[:, pl.ds(off, tk)] += _dot_nt(qtv, dsb)
            dvt_sc[:, pl.ds(off, tk)] += _dot_nt(dotv, pb)
            st_sc[...] = st_next
            dp_sc[...] = dp_next
            return carry

        st_sc[...] = _dot(k_ref[pl.ds(0, tk), :], qtv)
        dp_sc[...] = _dot(v_ref[pl.ds(0, tk), :], dotv)
        pb, dsb = grads(_dot(kc_ref[...], qtv), _dot(vc_ref[...], dotv))
        acc_sc[...] = _dot(kct_ref[...], dsb)
        dkct_ref[...] += _dot_nt(qtv, dsb)
        dvct_ref[...] += _dot_nt(dotv, pb)
        lax.fori_loop(0, n_chunks, loop, 0)
        dqt_ref[...] = acc_sc[...]

        @pl.when(jnp.logical_and(h == nh - 1, i == n_q - 1))
        def _():
            pltpu.sync_copy(dkt_sc, dkt_ref)
            pltpu.sync_copy(dvt_sc, dvt_ref)

    qs = pl.BlockSpec((None, LANES, tq), lambda h, i: (h, 0, i))
    rs = pl.BlockSpec((None, 1, tq), lambda h, i: (h, 0, i))
    return _call(
        body, name="attn_glob_bwd", grid=(nh, n_q),
        in_specs=[qs, qs, qs, rs, _full((s, LANES)), _full((LANES, s)), _full((s, LANES)), _full((nc, LANES)),
                  _full((LANES, nc)), _full((nc, LANES))],
        out_specs=[qs, ANY, ANY, _full((LANES, nc)), _full((LANES, nc))],
        out_shape=[_sds((nh, LANES, s)), _sds((LANES, s)), _sds((LANES, s)), _sds((LANES, nc)), _sds((LANES, nc))],
        scratch=[pltpu.VMEM((LANES, tq), F32), pltpu.VMEM((tk, tq), F32), pltpu.VMEM((tk, tq), F32),
                 pltpu.VMEM((LANES, s), F32), pltpu.VMEM((LANES, s), F32)],
        sem=("arbitrary", "arbitrary"),
    )(qt, dot, ot, lse, k, kt, v, kc, kct, vc)


def _attn_glob_dq(qt, dot, ot, lse, k, kt, v, kc, kct, vc, *, tq, tk):
    nh, _, s = qt.shape
    nc = kc.shape[0]

    def body(qt_ref, dot_ref, ot_ref, lse_ref, k_ref, kt_ref, v_ref, kc_ref, kct_ref, vc_ref,
             dqt_ref, dl_ref, dkct_ref, dvct_ref, acc_sc, st_sc, dp_sc):
        first = jnp.logical_and(pl.program_id(0) == 0, pl.program_id(1) == 0)

        @pl.when(first)
        def _():
            dkct_ref[...] = jnp.zeros(dkct_ref.shape, F32)
            dvct_ref[...] = jnp.zeros(dvct_ref.shape, F32)

        qtv, dotv, lse = qt_ref[...], dot_ref[...], lse_ref[...]
        delta = jnp.sum(dotv.astype(F32) * ot_ref[...].astype(F32), axis=0, keepdims=True)
        dl_ref[...] = delta

        n_chunks = s // tk

        def grads(st, dpt):
            pt = jnp.exp(st - lse)
            return pt.astype(BF16), (pt * (dpt - delta)).astype(BF16)

        def loop(c, carry):
            off = pl.multiple_of(c * tk, tk)
            nxt = pl.multiple_of(jnp.minimum(c + 1, n_chunks - 1) * tk, tk)
            st, dpt = st_sc[...], dp_sc[...]
            st_next = _dot(k_ref[pl.ds(nxt, tk), :], qtv)
            dp_next = _dot(v_ref[pl.ds(nxt, tk), :], dotv)
            _, dsb = grads(st, dpt)
            acc_sc[...] += _dot(kt_ref[:, pl.ds(off, tk)], dsb)
            st_sc[...] = st_next
            dp_sc[...] = dp_next
            return carry

        st_sc[...] = _dot(k_ref[pl.ds(0, tk), :], qtv)
        dp_sc[...] = _dot(v_ref[pl.ds(0, tk), :], dotv)
        pb, dsb = grads(_dot(kc_ref[...], qtv), _dot(vc_ref[...], dotv))
        acc_sc[...] = _dot(kct_ref[...], dsb)
        dkct_ref[...] += _dot_nt(qtv, dsb)
        dvct_ref[...] += _dot_nt(dotv, pb)
        lax.fori_loop(0, n_chunks, loop, 0)
        dqt_ref[...] = acc_sc[...]

    qs = pl.BlockSpec((None, LANES, tq), lambda h, i: (h, 0, i))
    rs = pl.BlockSpec((None, 1, tq), lambda h, i: (h, 0, i))
    return _call(
        body, name="attn_glob_dq", grid=(nh, s // tq),
        in_specs=[qs, qs, qs, rs, _full((s, LANES)), _full((LANES, s)), _full((s, LANES)), _full((nc, LANES)),
                  _full((LANES, nc)), _full((nc, LANES))],
        out_specs=[qs, rs, _full((LANES, nc)), _full((LANES, nc))],
        out_shape=[_sds((nh, LANES, s)), _sds((nh, 1, s)), _sds((LANES, nc)), _sds((LANES, nc))],
        scratch=[pltpu.VMEM((LANES, tq), F32), pltpu.VMEM((tk, tq), F32), pltpu.VMEM((tk, tq), F32)],
        sem=("arbitrary", "arbitrary"),
    )(qt, dot, ot, lse, k, kt, v, kc, kct, vc)


def _attn_glob_dkv(qt, dot, lse, dl, k, v, *, tq, tk):
    nh, _, s = qt.shape

    def body(k_ref, v_ref, qt_ref, dot_ref, lse_ref, dl_ref, dkt_ref, dvt_ref, st_sc, dp_sc):
        @pl.when(pl.program_id(1) == 0)
        def _():
            dkt_ref[...] = jnp.zeros(dkt_ref.shape, F32)
            dvt_ref[...] = jnp.zeros(dvt_ref.shape, F32)

        kv, vv = k_ref[...], v_ref[...]
        n_chunks = s // tq

        def loop(c, carry):
            off = pl.multiple_of(c * tq, tq)
            nxt = pl.multiple_of(jnp.minimum(c + 1, n_chunks - 1) * tq, tq)
            st, dpt = st_sc[...], dp_sc[...]
            st_next = _dot(kv, qt_ref[:, pl.ds(nxt, tq)])
            dp_next = _dot(vv, dot_ref[:, pl.ds(nxt, tq)])
            pt = jnp.exp(st - lse_ref[:, pl.ds(off, tq)])
            dst = pt * (dpt - dl_ref[:, pl.ds(off, tq)])
            dkt_ref[...] += _dot_nt(qt_ref[:, pl.ds(off, tq)], dst.astype(BF16))
            dvt_ref[...] += _dot_nt(dot_ref[:, pl.ds(off, tq)], pt.astype(BF16))
            st_sc[...] = st_next
            dp_sc[...] = dp_next
            return carry

        st_sc[...] = _dot(kv, qt_ref[:, pl.ds(0, tq)])
        dp_sc[...] = _dot(vv, dot_ref[:, pl.ds(0, tq)])
        lax.fori_loop(0, n_chunks, loop, 0)

    ks = pl.BlockSpec((tk, LANES), lambda j, h: (j, 0))
    ts = pl.BlockSpec((LANES, tk), lambda j, h: (0, j))
    qs = pl.BlockSpec((None, LANES, s), lambda j, h: (h, 0, 0))
    rs = pl.BlockSpec((None, 1, s), lambda j, h: (h, 0, 0))
    return _call(
        body, name="attn_glob_dkv", grid=(s // tk, nh),
        in_specs=[ks, ks, qs, qs, rs, rs], out_specs=[ts, ts],
        out_shape=[_sds((LANES, s)), _sds((LANES, s))],
        scratch=[pltpu.VMEM((tk, tq), F32), pltpu.VMEM((tk, tq), F32)],
        sem=("parallel", "arbitrary"),
    )(k, v, qt, dot, lse, dl)


WIN_SPAN = 2 * WINDOW


def _band(rows0, cols0, shape):
    r = rows0 + lax.broadcasted_iota(jnp.int32, shape, 0)
    c = cols0 + lax.broadcasted_iota(jnp.int32, shape, 1)
    return jnp.abs(r - c) <= WINDOW


def _win_start(blk, t, s):
    return pl.multiple_of(jnp.clip(blk * t - WINDOW, 0, s - t - WIN_SPAN), WINDOW)


def _attn_win_fwd(q, k, v, kc, vc, sink, *, tq):
    nh, s, _ = q.shape
    nc = kc.shape[0]
    tw = tq + WIN_SPAN

    def body(sink_ref, q_ref, k_ref, v_ref, kc_ref, vc_ref, o_ref, lse_ref):
        h, i = pl.program_id(0), pl.program_id(1)
        k0 = _win_start(i, tq, s)
        qv = q_ref[...]
        kv, vv = k_ref[pl.ds(k0, tw), :], v_ref[pl.ds(k0, tw), :]
        sc = jnp.where(_band(i * tq, k0, (tq, tw)), _dot_nt(qv, kv), NEG)
        scc = _dot_nt(qv, kc_ref[...])
        snk = sink_ref[h]
        m = jnp.maximum(jnp.maximum(jnp.max(sc, axis=-1, keepdims=True), jnp.max(scc, axis=-1, keepdims=True)), snk)
        p, pc = jnp.exp(sc - m), jnp.exp(scc - m)
        l = jnp.sum(p, axis=-1, keepdims=True) + jnp.sum(pc, axis=-1, keepdims=True) + jnp.exp(snk - m)
        acc = _dot(p.astype(BF16), vv) + _dot(pc.astype(BF16), vc_ref[...])
        o_ref[...] = (acc / l).astype(BF16)
        lse_ref[...] = m + jnp.log(l)

    return _call(
        body, name="attn_win_fwd", grid=(nh, s // tq),
        in_specs=[pl.BlockSpec(memory_space=pltpu.SMEM),
                  pl.BlockSpec((None, tq, LANES), lambda h, i: (h, i, 0)),
                  _full((s, LANES)), _full((s, LANES)), _full((nc, LANES)), _full((nc, LANES))],
        out_specs=[pl.BlockSpec((None, tq, LANES), lambda h, i: (h, i, 0)),
                   pl.BlockSpec((None, tq, 1), lambda h, i: (h, i, 0))],
        out_shape=[_sds((nh, s, LANES), BF16), _sds((nh, s, 1))],
        sem=("parallel", "parallel"),
    )(sink, q, k, v, kc, vc)


def _attn_win_dq(q, do, o, lse, k, v, kc, vc, sink, *, tq):
    nh, s, _ = q.shape
    nc = kc.shape[0]
    tw = tq + WIN_SPAN
    nq = s // tq

    def body(sink_ref, q_ref, do_ref, o_ref, lse_ref, k_ref, v_ref, kc_ref, vc_ref,
             dq_ref, dl_ref, dkc_ref, dvc_ref, dsk_ref):
        h, i = pl.program_id(0), pl.program_id(1)

        @pl.when(jnp.logical_and(h == 0, i == 0))
        def _():
            dkc_ref[...] = jnp.zeros(dkc_ref.shape, F32)
            dvc_ref[...] = jnp.zeros(dvc_ref.shape, F32)

        k0 = _win_start(i, tq, s)
        qv, dov, lse = q_ref[...], do_ref[...], lse_ref[...]
        kv, vv = k_ref[pl.ds(k0, tw), :], v_ref[pl.ds(k0, tw), :]
        kcv, vcv = kc_ref[...], vc_ref[...]
        delta = jnp.sum(dov.astype(F32) * o_ref[...].astype(F32), axis=-1, keepdims=True)
        dl_ref[...] = delta
        p = jnp.where(_band(i * tq, k0, (tq, tw)), jnp.exp(_dot_nt(qv, kv) - lse), 0.0)
        ds = (p * (_dot_nt(dov, vv) - delta)).astype(BF16)
        pc = jnp.exp(_dot_nt(qv, kcv) - lse)
        dsc = (pc * (_dot_nt(dov, vcv) - delta)).astype(BF16)
        dq_ref[...] = _dot(ds, kv) + _dot(dsc, kcv)
        dkc_ref[...] += _dot_tn(dsc, qv)
        dvc_ref[...] += _dot_tn(pc.astype(BF16), dov)
        dsk = -jnp.sum(jnp.exp(sink_ref[h] - lse) * delta)
        dsk_ref[...] = jnp.full(dsk_ref.shape, dsk, F32)

    qs = pl.BlockSpec((None, tq, LANES), lambda h, i: (h, i, 0))
    cs = pl.BlockSpec((None, tq, 1), lambda h, i: (h, i, 0))
    return _call(
        body, name="attn_win_dq", grid=(nh, nq),
        in_specs=[pl.BlockSpec(memory_space=pltpu.SMEM), qs, qs, qs, cs,
                  _full((s, LANES)), _full((s, LANES)), _full((nc, LANES)), _full((nc, LANES))],
        out_specs=[qs, cs, _full((nc, LANES)), _full((nc, LANES)),
                   pl.BlockSpec((None, None, 8, LANES), lambda h, i: (h, i, 0, 0))],
        out_shape=[_sds((nh, s, LANES)), _sds((nh, s, 1)), _sds((nc, LANES)), _sds((nc, LANES)),
                   _sds((nh, nq, 8, LANES))],
        sem=("arbitrary", "arbitrary"),
    )(sink, q, do, o, lse, k, v, kc, vc)


def _attn_win_dkv(q, do, lse_row, dl_row, k, v, *, tk):
    nh, s, _ = q.shape
    tw = tk + WIN_SPAN

    def body(k_ref, v_ref, q_ref, do_ref, lse_ref, dl_ref, dk_ref, dv_ref, dk_sc, dv_sc):
        h, j = pl.program_id(0), pl.program_id(1)
        q0 = _win_start(j, tk, s)
        kv, vv = k_ref[...], v_ref[...]
        qv, dov = q_ref[pl.ds(q0, tw), :], do_ref[pl.ds(q0, tw), :]
        pt = jnp.where(_band(j * tk, q0, (tk, tw)), jnp.exp(_dot_nt(kv, qv) - lse_ref[:, pl.ds(q0, tw)]), 0.0)
        dst = pt * (_dot_nt(vv, dov) - dl_ref[:, pl.ds(q0, tw)])
        dk, dv = _dot(dst.astype(BF16), qv), _dot(pt.astype(BF16), dov)
        rows = pl.ds(pl.multiple_of(j * tk, tk), tk)

        @pl.when(h == 0)
        def _():
            dk_sc[rows, :] = dk
            dv_sc[rows, :] = dv

        @pl.when(h > 0)
        def _():
            dk_sc[rows, :] += dk
            dv_sc[rows, :] += dv

        @pl.when(jnp.logical_and(h == nh - 1, j == s // tk - 1))
        def _():
            pltpu.sync_copy(dk_sc, dk_ref)
            pltpu.sync_copy(dv_sc, dv_ref)

    ks = pl.BlockSpec((tk, LANES), lambda h, j: (j, 0))
    qs = pl.BlockSpec((None, s, LANES), lambda h, j: (h, 0, 0))
    rs = pl.BlockSpec((None, 1, s), lambda h, j: (h, 0, 0))
    return _call(
        body, name="attn_win_dkv", grid=(nh, s // tk),
        in_specs=[ks, ks, qs, qs, rs, rs], out_specs=[ANY, ANY],
        out_shape=[_sds((s, LANES)), _sds((s, LANES))],
        scratch=[pltpu.VMEM((s, LANES), F32), pltpu.VMEM((s, LANES), F32)],
        sem=("arbitrary", "arbitrary"),
    )(k, v, q, do, lse_row, dl_row)


def _ln_fwd(z, g, b):
    mu = jnp.mean(z, axis=-1, keepdims=True)
    zc = z - mu
    r = lax.rsqrt(jnp.mean(zc * zc, axis=-1, keepdims=True) + LN_EPS)
    return zc * r * g + b, mu, r


def _ln_bwd(dy, xhat, r, g):
    dxh = dy * g
    return r * (dxh - jnp.mean(dxh, axis=-1, keepdims=True) - xhat * jnp.mean(dxh * xhat, axis=-1, keepdims=True))


def _heads_matmul(o_ref, w_ref):
    acc = _dot(o_ref[0], w_ref[0])
    for h in range(1, N_HEADS):
        acc += _dot(o_ref[h], w_ref[h])
    return acc


def _gate_specs(tm):
    return [pl.BlockSpec((tm, 512), functools.partial(lambda i, b: (i, b), b=OFF_GA // 512 + b)) for b in range(4)]


def _merge_fwd(oa, ob, proj, x, gate1, wba, wbb, w_out, ln_g, ln_b, *, tm):
    s = x.shape[0]

    def body(oa_ref, ob_ref, g0, g1, g2, g3, x_ref, gt_ref, wba_ref, wbb_ref, wo_ref, lg_ref, lb_ref,
             x1_ref, y_ref, mu_ref, r_ref):
        ga = _sigmoid(jnp.concatenate([g0[...], g1[...]], axis=1))
        gb = _sigmoid(jnp.concatenate([g2[...], g3[...]], axis=1))
        merged = ga * _heads_matmul(oa_ref, wba_ref) + gb * _heads_matmul(ob_ref, wbb_ref)
        y = _dot(merged.astype(BF16), wo_ref[...])
        x1, mu, r = _ln_fwd(ALPHA * x_ref[...] + gt_ref[...] * y, lg_ref[...], lb_ref[...])
        x1_ref[...] = x1
        y_ref[...] = y
        mu_ref[...] = mu
        r_ref[...] = r

    hs = pl.BlockSpec((N_HEADS, tm, LANES), lambda i: (0, i, 0))
    row = pl.BlockSpec((tm, D_MODEL), lambda i: (i, 0))
    col = pl.BlockSpec((tm, 1), lambda i: (i, 0))
    vec = _full((1, D_MODEL))
    wh = _full((N_HEADS, LANES, D_MODEL))
    return _call(
        body, name="merge_fwd", grid=(s // tm,),
        in_specs=[hs, hs, *_gate_specs(tm), row, vec, wh, wh, _full((D_MODEL, D_MODEL)), vec, vec],
        out_specs=[row, row, col, col],
        out_shape=[_sds((s, D_MODEL)), _sds((s, D_MODEL)), _sds((s, 1)), _sds((s, 1))],
        sem=("parallel",),
    )(oa, ob, proj, proj, proj, proj, x, gate1, wba, wbb, w_out, ln_g, ln_b)


def _merge_bwd(dy, oa, ob, proj, wba, wbb, w_out, *, tm):
    s = dy.shape[0]

    def body(dy_ref, oa_ref, ob_ref, g0, g1, g2, g3, wba_ref, wbb_ref, wo_ref,
             dgl_ref, doa_ref, dob_ref, dpa_ref, dpb_ref, mg_ref):
        dm = _dot_nt(dy_ref[...], wo_ref[...])
        ga = _sigmoid(jnp.concatenate([g0[...], g1[...]], axis=1))
        gb = _sigmoid(jnp.concatenate([g2[...], g3[...]], axis=1))
        pa, pb = _heads_matmul(oa_ref, wba_ref), _heads_matmul(ob_ref, wbb_ref)
        mg_ref[...] = (ga * pa + gb * pb).astype(BF16)
        dgl_ref[:, :D_MODEL] = (dm * pa * ga * (1.0 - ga)).astype(BF16)
        dgl_ref[:, D_MODEL:] = (dm * pb * gb * (1.0 - gb)).astype(BF16)
        dpa, dpb = (dm * ga).astype(BF16), (dm * gb).astype(BF16)
        dpa_ref[...] = dpa
        dpb_ref[...] = dpb
        for h in range(N_HEADS):
            doa_ref[h] = _dot_nt(dpa, wba_ref[h]).astype(BF16)
            dob_ref[h] = _dot_nt(dpb, wbb_ref[h]).astype(BF16)

    hs = pl.BlockSpec((N_HEADS, tm, LANES), lambda i: (0, i, 0))
    row = pl.BlockSpec((tm, D_MODEL), lambda i: (i, 0))
    wh = _full((N_HEADS, LANES, D_MODEL))
    return _call(
        body, name="merge_bwd", grid=(s // tm,),
        in_specs=[row, hs, hs, *_gate_specs(tm), wh, wh, _full((D_MODEL, D_MODEL))],
        out_specs=[pl.BlockSpec((tm, 2 * D_MODEL), lambda i: (i, 0)), hs, hs, row, row, row],
        out_shape=[_sds((s, 2 * D_MODEL), BF16), _sds((N_HEADS, s, LANES), BF16), _sds((N_HEADS, s, LANES), BF16),
                   _sds((s, D_MODEL), BF16), _sds((s, D_MODEL), BF16), _sds((s, D_MODEL), BF16)],
        sem=("parallel",),
    )(dy, oa, ob, proj, proj, proj, proj, wba, wbb, w_out)


FF_TC = 256


def _shift_rows(t, prev_row, next_row):
    n = t.shape[0]
    r = lax.broadcasted_iota(jnp.int32, t.shape, 0)
    up = jnp.where(r == 0, prev_row, pltpu.roll(t, 1, 0))
    dn = jnp.where(r == n - 1, next_row, pltpu.roll(t, n - 1, 0))
    return up, dn


def _halo_specs(tm, s, tc):
    nb8 = s // 8
    main = pl.BlockSpec((2, tm, tc), lambda j, i: (0, i, j))
    prev = pl.BlockSpec((2, 8, tc), lambda j, i: (0, jnp.maximum(i * (tm // 8) - 1, 0), j))
    nxt = pl.BlockSpec((2, 8, tc), lambda j, i: (0, jnp.minimum((i + 1) * (tm // 8), nb8 - 1), j))
    return main, prev, nxt


def _halo_rows(prev_ref, next_ref, half, i, n_i):
    prev_row = jnp.where(i == 0, 0.0, prev_ref[half, 7:8, :].astype(F32))
    next_row = jnp.where(i == n_i - 1, 0.0, next_ref[half, 0:1, :].astype(F32))
    return prev_row, next_row


def _conv(t, prev_row, next_row, w, b):
    up, dn = _shift_rows(t, prev_row, next_row)
    return w[0:1, :] * up + w[1:2, :] * t + w[2:3, :] * dn + b


def _ffn_act_fwd(u, cw, cb, *, tm):
    _, s, ff = u.shape
    n_i = s // tm

    def body(u_ref, up_ref, un_ref, cw_ref, cb_ref, a_ref):
        i = pl.program_id(1)
        gc = _conv(u_ref[0], *_halo_rows(up_ref, un_ref, 0, i, n_i), cw_ref[0], cb_ref[0])
        vc = _conv(u_ref[1], *_halo_rows(up_ref, un_ref, 1, i, n_i), cw_ref[1], cb_ref[1])
        a_ref[...] = (gc * _sigmoid(gc) * vc).astype(BF16)

    main, prev, nxt = _halo_specs(tm, s, FF_TC)
    return _call(
        body, name="ffn_act_fwd", grid=(ff // FF_TC, n_i),
        in_specs=[main, prev, nxt, pl.BlockSpec((2, 3, FF_TC), lambda j, i: (0, 0, j)),
                  pl.BlockSpec((2, 1, FF_TC), lambda j, i: (0, 0, j))],
        out_specs=pl.BlockSpec((tm, FF_TC), lambda j, i: (i, j)),
        out_shape=_sds((s, ff), BF16), sem=("parallel", "parallel"),
    )(u, u, u, cw, cb)


def _ffn_act_bwd(dy2, w_down, u, cw, cb, *, tm):
    _, s, ff = u.shape
    n_i = s // tm

    def body(dy_ref, wd_ref, u_ref, up_ref, un_ref, cw_ref, cb_ref, dc_ref, dcw_ref, dcb_ref):
        i = pl.program_id(1)

        @pl.when(i == 0)
        def _():
            dcw_ref[...] = jnp.zeros(dcw_ref.shape, F32)
            dcb_ref[...] = jnp.zeros(dcb_ref.shape, F32)

        da = _dot_nt(dy_ref[...], wd_ref[...])
        ug, uv = u_ref[0], u_ref[1]
        ugp, ugn = _shift_rows(ug, *_halo_rows(up_ref, un_ref, 0, i, n_i))
        uvp, uvn = _shift_rows(uv, *_halo_rows(up_ref, un_ref, 1, i, n_i))
        wg, wv = cw_ref[0], cw_ref[1]
        gc = wg[0:1, :] * ugp + wg[1:2, :] * ug + wg[2:3, :] * ugn + cb_ref[0]
        vc = wv[0:1, :] * uvp + wv[1:2, :] * uv + wv[2:3, :] * uvn + cb_ref[1]
        sg = _sigmoid(gc)
        dg = da * vc * sg * (1.0 + gc * (1.0 - sg))
        dv = da * gc * sg
        dc_ref[0] = dg
        dc_ref[1] = dv
        for half, (d, taps) in enumerate(((dg, (ugp, ug, ugn)), (dv, (uvp, uv, uvn)))):
            for tap in range(3):
                dcw_ref[half, tap:tap + 1, :] += jnp.sum(d * taps[tap], axis=0, keepdims=True)
            dcb_ref[half] += jnp.sum(d, axis=0, keepdims=True)

    main, prev, nxt = _halo_specs(tm, s, FF_TC)
    return _call(
        body, name="ffn_act_bwd", grid=(ff // FF_TC, n_i),
        in_specs=[pl.BlockSpec((tm, D_MODEL), lambda j, i: (i, 0)), pl.BlockSpec((FF_TC, D_MODEL), lambda j, i: (j, 0)),
                  main, prev, nxt, pl.BlockSpec((2, 3, FF_TC), lambda j, i: (0, 0, j)),
                  pl.BlockSpec((2, 1, FF_TC), lambda j, i: (0, 0, j))],
        out_specs=[main, pl.BlockSpec((2, 3, FF_TC), lambda j, i: (0, 0, j)),
                   pl.BlockSpec((2, 1, FF_TC), lambda j, i: (0, 0, j))],
        out_shape=[_sds((2, s, ff)), _sds((2, 3, ff)), _sds((2, 1, ff))],
        sem=("parallel", "arbitrary"),
    )(dy2, w_down, u, u, u, cw, cb)


def _conv_bwd_input(dc, cw, *, tm):
    _, s, ff = dc.shape
    n_i = s // tm

    def body(d_ref, dp_ref, dn_ref, cw_ref, du_ref):
        i = pl.program_id(1)
        for half in range(2):
            up, dn = _shift_rows(d_ref[half], *_halo_rows(dp_ref, dn_ref, half, i, n_i))
            w = cw_ref[half]
            du_ref[half] = (w[0:1, :] * dn + w[1:2, :] * d_ref[half] + w[2:3, :] * up).astype(BF16)

    main, prev, nxt = _halo_specs(tm, s, FF_TC)
    return _call(
        body, name="conv_bwd_input", grid=(ff // FF_TC, n_i),
        in_specs=[main, prev, nxt, pl.BlockSpec((2, 3, FF_TC), lambda j, i: (0, 0, j))],
        out_specs=main, out_shape=_sds((2, s, ff), BF16), sem=("parallel", "parallel"),
    )(dc, dc, dc, cw)


def _ffn_down_loss(a, w_down, x1, target, gate2, ln_g, ln_b, *, tm):
    s, ff = a.shape
    n_i = s // tm

    def body(a_ref, wd_ref, x1_ref, tg_ref, gt_ref, lg_ref, lb_ref, ls_ref, dy_ref, dx_ref, dg_ref, db_ref, dgt_ref):
        @pl.when(pl.program_id(0) == 0)
        def _():
            dg_ref[...] = jnp.zeros(dg_ref.shape, F32)
            db_ref[...] = jnp.zeros(db_ref.shape, F32)
            dgt_ref[...] = jnp.zeros(dgt_ref.shape, F32)

        y2 = _dot(a_ref[...], wd_ref[...])
        z = ALPHA * x1_ref[...] + gt_ref[...] * y2
        mu = jnp.mean(z, axis=-1, keepdims=True)
        zc = z - mu
        r = lax.rsqrt(jnp.mean(zc * zc, axis=-1, keepdims=True) + LN_EPS)
        xhat = zc * r
        diff = xhat * lg_ref[...] + lb_ref[...] - tg_ref[...]
        ls_ref[...] = jnp.full(ls_ref.shape, 0.5 / D_MODEL * jnp.sum(diff * diff), F32)
        dx2 = diff * (1.0 / D_MODEL)
        dg_ref[...] += jnp.sum(dx2 * xhat, axis=0, keepdims=True)
        db_ref[...] += jnp.sum(dx2, axis=0, keepdims=True)
        dz = _ln_bwd(dx2, xhat, r, lg_ref[...])
        dgt_ref[...] += jnp.sum(dz * y2, axis=0, keepdims=True)
        dy_ref[...] = (gt_ref[...] * dz).astype(BF16)
        dx_ref[...] = ALPHA * dz

    row = pl.BlockSpec((tm, D_MODEL), lambda i: (i, 0))
    vec = _full((1, D_MODEL))
    return _call(
        body, name="ffn_down_loss", grid=(n_i,),
        in_specs=[pl.BlockSpec((tm, ff), lambda i: (i, 0)), _full((ff, D_MODEL)), row, row, vec, vec, vec],
        out_specs=[pl.BlockSpec((None, 8, LANES), lambda i: (i, 0, 0)), row, row, vec, vec, vec],
        out_shape=[_sds((n_i, 8, LANES)), _sds((s, D_MODEL), BF16), _sds((s, D_MODEL)),
                   _sds((1, D_MODEL)), _sds((1, D_MODEL)), _sds((1, D_MODEL))],
        sem=("arbitrary",),
    )(a, w_down, x1, target, gate2, ln_g, ln_b)


def _ffn_up_bwd(du, wup4, dx1a, x1, scale2, x, y, mu1, r1, gate1, ln_g, *, tm):
    s = x.shape[0]
    nb, _, ns = wup4.shape

    def body(du_ref, w_ref, dxa_ref, x1_ref, sc_ref, x_ref, y_ref, mu_ref, r_ref, gt_ref, lg_ref,
             dxo_ref, dy_ref, dsc_ref, dsh_ref, dg_ref, db_ref, dgt_ref, acc):
        i, k = pl.program_id(0), pl.program_id(1)

        @pl.when(jnp.logical_and(i == 0, k == 0))
        def _():
            for ref in (dsc_ref, dsh_ref, dg_ref, db_ref, dgt_ref):
                ref[...] = jnp.zeros(ref.shape, F32)

        @pl.when(k == 0)
        def _():
            acc[...] = jnp.zeros(acc.shape, F32)

        acc[...] += _dot_nt(du_ref[...], w_ref[...])

        @pl.when(k == nb - 1)
        def _():
            dh = acc[...]
            x1 = x1_ref[...]
            dsc_ref[...] += jnp.sum(dh * x1, axis=0, keepdims=True)
            dsh_ref[...] += jnp.sum(dh, axis=0, keepdims=True)
            dx1 = dxa_ref[...] + dh * (1.0 + sc_ref[...])
            yv = y_ref[...]
            xhat = (ALPHA * x_ref[...] + gt_ref[...] * yv - mu_ref[...]) * r_ref[...]
            dg_ref[...] += jnp.sum(dx1 * xhat, axis=0, keepdims=True)
            db_ref[...] += jnp.sum(dx1, axis=0, keepdims=True)
            dz = _ln_bwd(dx1, xhat, r_ref[...], lg_ref[...])
            dgt_ref[...] += jnp.sum(dz * yv, axis=0, keepdims=True)
            dy_ref[...] = (gt_ref[...] * dz).astype(BF16)
            dxo_ref[...] = ALPHA * dz

    row = pl.BlockSpec((tm, D_MODEL), lambda i, k: (i, 0))
    col = pl.BlockSpec((tm, 1), lambda i, k: (i, 0))
    vec = _full((1, D_MODEL))
    return _call(
        body, name="ffn_up_bwd", grid=(s // tm, nb),
        in_specs=[pl.BlockSpec((None, tm, ns), lambda i, k: (k // 2, i, k % 2)),
                  pl.BlockSpec((None, D_MODEL, ns), lambda i, k: (k, 0, 0)),
                  row, row, vec, row, row, col, col, vec, vec],
        out_specs=[row, row, vec, vec, vec, vec, vec],
        out_shape=[_sds((s, D_MODEL)), _sds((s, D_MODEL), BF16)] + [_sds((1, D_MODEL))] * 5,
        scratch=[pltpu.VMEM((tm, D_MODEL), F32)],
        sem=("arbitrary", "arbitrary"),
    )(du, wup4, dx1a, x1, scale2, x, y, mu1, r1, gate1, ln_g)


def _mm_nt4_mod_bwd(dp, w4, dxa, x, scale, *, tm, name):
    m = x.shape[0]
    nb, kdim, ns = w4.shape

    def body(dp_ref, w_ref, dxa_ref, x_ref, sc_ref, dx_ref, dsc_ref, dsh_ref, acc):
        i, k = pl.program_id(0), pl.program_id(1)

        @pl.when(jnp.logical_and(i == 0, k == 0))
        def _():
            dsc_ref[...] = jnp.zeros(dsc_ref.shape, F32)
            dsh_ref[...] = jnp.zeros(dsh_ref.shape, F32)

        @pl.when(k == 0)
        def _():
            acc[...] = jnp.zeros(acc.shape, F32)

        acc[...] += _dot_nt(dp_ref[...], w_ref[...])

        @pl.when(k == nb - 1)
        def _():
            dh = acc[...]
            dsc_ref[...] += jnp.sum(dh * x_ref[...], axis=0, keepdims=True)
            dsh_ref[...] += jnp.sum(dh, axis=0, keepdims=True)
            dx_ref[...] = dxa_ref[...] + dh * (1.0 + sc_ref[...])

    row = pl.BlockSpec((tm, kdim), lambda i, k: (i, 0))
    vec = _full((1, kdim))
    return _call(
        body, name=name, grid=(m // tm, nb),
        in_specs=[pl.BlockSpec((tm, ns), lambda i, k: (i, k)), pl.BlockSpec((None, kdim, ns), lambda i, k: (k, 0, 0)),
                  row, row, vec],
        out_specs=[row, vec, vec],
        out_shape=[_sds((m, kdim)), _sds((1, kdim)), _sds((1, kdim))],
        scratch=[pltpu.VMEM((tm, kdim), F32)],
        sem=("arbitrary", "arbitrary"),
    )(dp, w4, dxa, x, scale)


def _pad_heads_w(w):
    w8 = w.reshape(N_HEADS, HEAD_DIM, w.shape[-1])
    z = jnp.zeros_like(w8)
    first = (jnp.arange(N_HEADS) < N_HEADS // N_KV)[:, None, None]
    return jnp.where(first, jnp.concatenate([w8, z], axis=1), jnp.concatenate([z, w8], axis=1))


def _unpad_heads_w(g):
    first = (jnp.arange(N_HEADS) < N_HEADS // N_KV)[:, None, None]
    return jnp.where(first, g[:, :HEAD_DIM], g[:, HEAD_DIM:]).reshape(N_HEADS * HEAD_DIM, g.shape[-1])


def _rep8(a):
    return jnp.broadcast_to(a.reshape(1, -1), (8, a.size))


def _first_row(a):
    r8 = _rep8(a)
    return jnp.where(lax.broadcasted_iota(jnp.int32, r8.shape, 0) == 0, r8, 0.0)


def _to_blocks4(w):
    k, n = w.shape
    return w.reshape(k, N_CHIPS, n // N_CHIPS).transpose(1, 0, 2)


def _local_step(x, c, ctx, c_ctx, wmod4, b_mod, win4, b_in, sink, qn, kn, wba, wbb, w_out, ln1_g, ln1_b,
                wup4, cw, cb, w_down, ln2_g, ln2_b, target):
    s, nc = x.shape[0], ctx.shape[0]
    tm = min(512, s)
    tm2 = min(256, s)
    zvec = jnp.zeros((1, D_MODEL), F32)

    cc = jnp.concatenate([_rep8(c), _rep8(c_ctx)], axis=0)
    mods = _mm_nn4(cc, zvec, zvec, wmod4, b_mod, mode="silu", split_out=False, out_dtype=F32, tm=16, name="mod_vectors")
    shift1, scale1, gate1, shift2, scale2, gate2 = [mods[0:1, i * D_MODEL:(i + 1) * D_MODEL] for i in range(6)]
    shift_c, scale_c = mods[8:9, :D_MODEL], mods[8:9, D_MODEL:2 * D_MODEL]

    cos, sin = _rope_tables(s)
    cos_c, sin_c = jnp.ones((nc, LANES), F32), jnp.zeros((nc, LANES), F32)
    qg, kg = jnp.tile(qn, (1, 2)), jnp.tile(kn, (1, 2))

    proj_c = _mm_nn4(ctx, shift_c, scale_c, win4, b_in, mode="modulate", split_out=False, out_dtype=F32, tm=nc,
                     name="in_proj_ctx")
    _, kac, vac, _, kbc, vbc = _prep(proj_c, cos_c, sin_c, qg, kg, tm=nc, name="prep_ctx")
    proj = _mm_nn4(x, shift1, scale1, win4, b_in, mode="modulate", split_out=False, out_dtype=F32, tm=tm, name="in_proj")
    qa, ka, va, qb, kb, vb = _prep(proj, cos, sin, qg, kg, tm=tm, name="prep")
    oa, lse_a = _attn_win_fwd(qa, ka, va, kac, vac, sink, tq=tm2)
    qbt = jnp.swapaxes(qb, 1, 2)
    obt, lse_b = _attn_glob_fwd(qbt, kb, vb.T, kbc, vbc.T, tq=tm, tk=tm)
    ob = jnp.swapaxes(obt, 1, 2)
    wba_p, wbb_p = _pad_heads_w(wba), _pad_heads_w(wbb)
    x1, y, mu1, r1 = _merge_fwd(oa, ob, proj, x, gate1, wba_p, wbb_p, w_out, ln1_g, ln1_b, tm=tm2)
    u = _mm_nn4(x1, shift2, scale2, wup4, jnp.zeros((1, 2 * D_FF), F32), mode="modulate", split_out=True,
                out_dtype=F32, tm=tm, name="ffn_up")
    cw2 = cw.reshape(3, 2, D_FF).transpose(1, 0, 2)
    cb2 = cb.reshape(2, 1, D_FF)
    a = _ffn_act_fwd(u, cw2, cb2, tm=tm)
    ls, dy2, dx1a, dln2_g, dln2_b, dgate2 = _ffn_down_loss(a, w_down, x1, target, gate2, ln2_g, ln2_b, tm=tm2)
    loss = jnp.sum(ls[:, 0, 0])

    n_s = s // tm
    dw_down = _mm_tn(a, dy2, a_spec=pl.BlockSpec((tm, D_FF), lambda t: (t, 0)),
                     b_spec=pl.BlockSpec((tm, D_MODEL), lambda t: (t, 0)), grid=(n_s,),
                     out_shape=_sds((D_FF, D_MODEL)), out_spec=_full((D_FF, D_MODEL)), name="dw_down")
    dc, dcw2, dcb2 = _ffn_act_bwd(dy2, w_down, u, cw2, cb2, tm=tm)
    du = _conv_bwd_input(dc, cw2, tm=tm)
    dxz1, dy, dscale2, dshift2, dln1_g, dln1_b, dgate1 = _ffn_up_bwd(
        du, wup4, dx1a, x1, scale2, x, y, mu1, r1, gate1, ln1_g, tm=tm2)
    ns_up = wup4.shape[-1]
    dw_up4 = _mm_tn(x1, du, a_spec=pl.BlockSpec((tm, D_MODEL), lambda k, t: (t, 0)),
                    b_spec=pl.BlockSpec((None, tm, ns_up), lambda k, t: (k // 2, t, k % 2)), grid=(N_CHIPS, n_s),
                    out_shape=_sds((N_CHIPS, D_MODEL, ns_up)),
                    out_spec=pl.BlockSpec((None, D_MODEL, ns_up), lambda k, t: (k, 0, 0)),
                    mod=(shift2, scale2), name="dw_up")

    dgl, doa, dob, dpa, dpb, merged = _merge_bwd(dy, oa, ob, proj, wba_p, wbb_p, w_out, tm=tm2)
    rowspec = pl.BlockSpec((tm, D_MODEL), lambda t: (t, 0))
    dw_out = _mm_tn(merged, dy, a_spec=rowspec, b_spec=rowspec, grid=(n_s,), out_shape=_sds((D_MODEL, D_MODEL)),
                    out_spec=_full((D_MODEL, D_MODEL)), name="dw_out")
    hspec = dict(a_spec=pl.BlockSpec((None, tm, LANES), lambda h, t: (h, t, 0)),
                 b_spec=pl.BlockSpec((tm, D_MODEL), lambda h, t: (t, 0)), grid=(N_HEADS, n_s),
                 out_shape=_sds((N_HEADS, LANES, D_MODEL)),
                 out_spec=pl.BlockSpec((None, LANES, D_MODEL), lambda h, t: (h, 0, 0)))
    dwba = _unpad_heads_w(_mm_tn(oa, dpa, name="dw_branch_a", **hspec))
    dwbb = _unpad_heads_w(_mm_tn(ob, dpb, name="dw_branch_b", **hspec))

    dqa, dla, dkac, dvac, dsk = _attn_win_dq(qa, doa, oa, lse_a, ka, va, kac, vac, sink, tq=tm2)
    dka, dva = _attn_win_dkv(qa, doa, lse_a.reshape(N_HEADS, 1, s), dla.reshape(N_HEADS, 1, s), ka, va, tk=tm2)
    dobt = jnp.swapaxes(dob, 1, 2)
    dqbt, dkbt, dvbt, dkbct, dvbct = _attn_glob_bwd(qbt, dobt, obt, lse_b, kb, kb.T, vb, kbc, kbc.T, vbc, tq=tm, tk=tm)
    dqb, dkb, dvb, dkbc, dvbc = jnp.swapaxes(dqbt, 1, 2), dkbt.T, dvbt.T, dkbct.T, dvbct.T
    dsink = jnp.sum(dsk[:, :, 0, 0], axis=1)

    dproj, dqg, dkg = _prep_bwd(dqa, dka, dva, dqb, dkb, dvb, proj, cos, sin, qg, kg, dgl, tm=tm, name="prep_bwd")
    grad_x, dscale1, dshift1 = _mm_nt4_mod_bwd(dproj, win4, dxz1, x, scale1, tm=tm, name="in_proj_bwd")
    ns_in = win4.shape[-1]
    win_spec = dict(b_spec=pl.BlockSpec((None, None, ns_in), lambda k, t: (0, 0, k)),
                    out_shape=_sds((N_CHIPS, D_MODEL, ns_in)),
                    out_spec=pl.BlockSpec((None, D_MODEL, ns_in), lambda k, t: (k, 0, 0)),
                    colsum_spec=pl.BlockSpec((8, ns_in), lambda k, t: (0, k)), colsum_shape=_sds((8, IN_COLS)))
    win_spec["b_spec"] = pl.BlockSpec((tm, ns_in), lambda k, t: (t, k))
    dw_in4, db_in = _mm_tn(x, dproj, a_spec=pl.BlockSpec((tm, D_MODEL), lambda k, t: (t, 0)), grid=(N_CHIPS, n_s),
                           mod=(shift1, scale1), name="dw_in", **win_spec)

    zq = jnp.zeros((N_HEADS, nc, LANES), F32)
    dproj_c, _, dkg_c = _prep_bwd(zq, dkac, dvac, zq, dkbc, dvbc, proj_c, cos_c, sin_c, qg, kg,
                                  jnp.zeros((nc, IN_COLS - OFF_GA), BF16), tm=nc, name="prep_bwd_ctx")
    _, dscale_c, dshift_c = _mm_nt4_mod_bwd(dproj_c, win4, jnp.zeros((nc, D_MODEL), F32), ctx, scale_c, tm=nc,
                                            name="in_proj_bwd_ctx")
    win_spec["b_spec"] = pl.BlockSpec((nc, ns_in), lambda k, t: (t, k))
    dw_in4, db_in_c = _mm_tn(ctx, dproj_c, a_spec=pl.BlockSpec((nc, D_MODEL), lambda k, t: (t, 0)), grid=(N_CHIPS, 1),
                             mod=(shift_c, scale_c), init=dw_in4, name="dw_in_ctx", **win_spec)

    dmod = jnp.concatenate([dshift1, dscale1, dgate1, dshift2, dscale2, dgate2], axis=1)
    dmodc = jnp.concatenate([dshift_c, dscale_c], axis=1)
    dmodc_pad = jnp.concatenate([dmodc, jnp.zeros((1, 4 * D_MODEL), F32)], axis=1)
    dmodc8 = _first_row(dmodc_pad).astype(BF16)
    z8 = jnp.zeros((8, D_MODEL), F32)
    dsilu_c, _, _ = _mm_nt4_mod_bwd(dmodc8, wmod4, z8, z8, zvec, tm=8, name="c_ctx_bwd")
    sg = _sigmoid(c_ctx)
    dc_ctx = dsilu_c[0:1] * sg * (1.0 + c_ctx * (1.0 - sg))

    dqn = jnp.sum(dqg.reshape(N_HEADS, HEAD_DIM), axis=0, keepdims=True)
    dkn = jnp.sum((dkg + dkg_c).reshape(N_KV, HEAD_DIM), axis=0, keepdims=True)
    grads = dict(
        w_in4=dw_in4, b_in=db_in[0:1] + db_in_c[0:1], sink=dsink, qn=dqn, kn=dkn, wba=dwba, wbb=dwbb, w_out=dw_out,
        ln1_g=dln1_g, ln1_b=dln1_b, w_up4=dw_up4, conv_w=dcw2.transpose(1, 0, 2).reshape(3, 2 * D_FF),
        conv_b=dcb2.reshape(1, 2 * D_FF), w_down=dw_down, ln2_g=dln2_g, ln2_b=dln2_b,
        c_ctx=dc_ctx, dmod=dmod, dmodc=dmodc)
    return loss, grad_x, grads


ANY = pl.BlockSpec(memory_space=pl.ANY)


def _mesh_pos():
    return lax.axis_index("x"), lax.axis_index("y"), lax.axis_index("c")


def _other_chips(x, y):
    return [(1 - x, y), (x, 1 - y), (1 - x, 1 - y)]


def _remote(src, dst, send, recv, dev):
    return pltpu.make_async_remote_copy(src_ref=src, dst_ref=dst, send_sem=send, recv_sem=recv, device_id=dev,
                                        device_id_type=MESH)


def _gather_shards(arrs, small):
    na = len(arrs)
    halves = [a.shape[0] // 2 for a in arrs]

    def body(*refs):
        ins, small_ref = refs[:na], refs[na]
        outs, small_out = refs[na + 1:2 * na + 1], refs[2 * na + 1]
        send, recv, loc = refs[2 * na + 2:]
        x, y, c = _mesh_pos()
        me = 2 * x + y
        chips = _other_chips(x, y)

        def half(a, cc):
            return pl.ds(cc * halves[a], halves[a])

        local = [pltpu.make_async_copy(ins[a], outs[a].at[me], loc.at[a]) for a in range(na)]
        local.append(pltpu.make_async_copy(small_ref, small_out.at[me], loc.at[na]))
        for cp in local:
            cp.start()
        sends = []
        for j, chip in enumerate(chips):
            for a in range(na):
                sends.append(_remote(ins[a].at[half(a, c)], outs[a].at[me, half(a, c)], send.at[a, j], recv.at[a, j],
                                     (*chip, c)))
            sends.append(_remote(small_ref, small_out.at[me], send.at[na, j], recv.at[na, j], (*chip, c)))
        for cp in sends:
            cp.start()
        for j, chip in enumerate(chips):
            kj = 2 * chip[0] + chip[1]
            for a in range(na):
                landed = outs[a].at[kj, half(a, c)]
                _remote(landed, landed, send.at[a, j], recv.at[a, j], (*chip, c)).wait_recv()
                fwd = _remote(landed, landed, send.at[a, 3 + j], recv.at[a, 3 + j], (x, y, 1 - c))
                fwd.start()
                sends.append(fwd)
            _remote(small_ref, small_out.at[kj], send.at[na, j], recv.at[na, j], (*chip, c)).wait_recv()
        for j, chip in enumerate(chips):
            kj = 2 * chip[0] + chip[1]
            for a in range(na):
                other = outs[a].at[kj, half(a, 1 - c)]
                _remote(other, other, send.at[a, 3 + j], recv.at[a, 3 + j], (x, y, 1 - c)).wait_recv()
        for cp in sends:
            cp.wait_send()
        for cp in local:
            cp.wait()

    out_shape = [_sds((N_CHIPS,) + a.shape, a.dtype) for a in arrs] + [_sds((N_CHIPS,) + small.shape, small.dtype)]
    return pl.pallas_call(
        body, name="gather_shards", in_specs=[ANY] * (na + 1), out_specs=[ANY] * (na + 1), out_shape=out_shape,
        scratch_shapes=[pltpu.SemaphoreType.DMA((na + 1, 6)), pltpu.SemaphoreType.DMA((na + 1, 6)),
                        pltpu.SemaphoreType.DMA((na + 1,))],
    )(*arrs, small)


def _allgather_rows(v):
    r, n = v.shape

    def body(v_ref, out_ref, send, recv, loc):
        x, y, c = _mesh_pos()
        me, sibling = (x, y, c), (x, y, 1 - c)
        chips = _other_chips(x, y)

        def rows(px, py, pc):
            return out_ref.at[4 * px + 2 * py + pc]

        def copy(k, block, to, src=None):
            return _remote(rows(*block) if src is None else src, rows(*block), send.at[k], recv.at[k], to)

        mine = pltpu.make_async_copy(v_ref, rows(*me), loc)
        mine.start()
        first = [copy(0, me, sibling, src=v_ref)] + [copy(1 + j, me, (*chip, c), src=v_ref) for j, chip in enumerate(chips)]
        for cp in first:
            cp.start()
        passed = [copy(4 + j, (*chip, c), sibling) for j, chip in enumerate(chips)]
        for j, chip in enumerate(chips):
            copy(1 + j, (*chip, c), me).wait_recv()
            passed[j].start()
        copy(0, sibling, me).wait_recv()
        for j, chip in enumerate(chips):
            copy(4 + j, (*chip, 1 - c), me).wait_recv()
        for cp in first + passed:
            cp.wait_send()
        mine.wait()

    return pl.pallas_call(
        body, name="allgather_rows", in_specs=[pl.BlockSpec(memory_space=pltpu.VMEM)],
        out_specs=pl.BlockSpec(memory_space=pltpu.VMEM), out_shape=_sds((N_DEV, r, n), v.dtype),
        scratch_shapes=[pltpu.SemaphoreType.DMA((7,)), pltpu.SemaphoreType.DMA((7,)), pltpu.SemaphoreType.DMA],
    )(v)


def _swap_other_half(g):
    nb, r, n = g.shape
    rh = r // 2

    def body(g_ref, out_ref, send, recv):
        x, y, c = _mesh_pos()
        cp = _remote(g_ref.at[:, pl.ds((1 - c) * rh, rh), :], out_ref, send, recv, (x, y, 1 - c))
        cp.start()
        cp.wait()

    return pl.pallas_call(
        body, name="swap_other_half", in_specs=[ANY], out_specs=ANY, out_shape=_sds((nb, rh, n), g.dtype),
        scratch_shapes=[pltpu.SemaphoreType.DMA, pltpu.SemaphoreType.DMA],
    )(g)


def _scatter_to_chips(p):
    def body(p_ref, out_ref, send, recv, loc):
        x, y, c = _mesh_pos()
        me = 2 * x + y
        chips = _other_chips(x, y)
        mine = pltpu.make_async_copy(p_ref.at[me], out_ref.at[me], loc)
        mine.start()
        sends = [_remote(p_ref.at[2 * chip[0] + chip[1]], out_ref.at[me], send.at[j], recv.at[j], (*chip, c))
                 for j, chip in enumerate(chips)]
        for cp in sends:
            cp.start()
        for j, chip in enumerate(chips):
            kj = 2 * chip[0] + chip[1]
            _remote(p_ref.at[kj], out_ref.at[kj], send.at[j], recv.at[j], (*chip, c)).wait_recv()
        for cp in sends:
            cp.wait_send()
        mine.wait()

    return pl.pallas_call(
        body, name="scatter_to_chips", in_specs=[ANY], out_specs=ANY, out_shape=_sds(p.shape, p.dtype),
        scratch_shapes=[pltpu.SemaphoreType.DMA((3,)), pltpu.SemaphoreType.DMA((3,)), pltpu.SemaphoreType.DMA],
    )(p)


def _join_halves(f):
    def body(f_ref, out_ref, send, recv, loc):
        x, y, c = _mesh_pos()
        mine = pltpu.make_async_copy(f_ref, out_ref.at[c], loc)
        mine.start()
        cp = _remote(f_ref, out_ref.at[c], send, recv, (x, y, 1 - c))
        cp.start()
        _remote(f_ref, out_ref.at[1 - c], send, recv, (x, y, 1 - c)).wait_recv()
        cp.wait_send()
        mine.wait()

    return pl.pallas_call(
        body, name="join_halves", in_specs=[ANY], out_specs=ANY, out_shape=_sds((2,) + f.shape, f.dtype),
        scratch_shapes=[pltpu.SemaphoreType.DMA, pltpu.SemaphoreType.DMA, pltpu.SemaphoreType.DMA],
    )(f)


def _row_tile(rows, cap=512):
    t = cap - cap % 8
    while rows % t:
        t -= 8
    return t


def _add_blocks(a, b):
    nb, r, n = a.shape
    tr = _row_tile(r)

    def body(a_ref, b_ref, o_ref):
        o_ref[...] = a_ref[...] + b_ref[...]

    spec = pl.BlockSpec((None, tr, n), lambda k, i: (k, i, 0))
    return _call(body, name="add_blocks", grid=(nb, r // tr), in_specs=[spec, spec], out_specs=spec,
                 out_shape=_sds(a.shape), sem=("parallel", "parallel"))(a, b)


def _sum_leading(a, *, name):
    nk, r, n = a.shape
    tr = _row_tile(r)

    def body(a_ref, o_ref):
        acc = a_ref[0]
        for k in range(1, nk):
            acc = acc + a_ref[k]
        o_ref[...] = acc

    return _call(body, name=name, grid=(r // tr,), in_specs=[pl.BlockSpec((nk, tr, n), lambda i: (0, i, 0))],
                 out_specs=pl.BlockSpec((tr, n), lambda i: (i, 0)), out_shape=_sds((r, n)), sem=("parallel",))(a)


def _silu_outer(a, b):
    kdim, n = a.shape[1], b.shape[1]

    def body(a_ref, b_ref, o_ref):
        av = a_ref[...]
        av = av * _sigmoid(av)
        bv = b_ref[...]
        ah, bh = av.astype(BF16), bv.astype(BF16)
        al, bl = (av - ah.astype(F32)).astype(BF16), (bv - bh.astype(F32)).astype(BF16)
        o_ref[...] = _dot_tn(ah, bh) + (_dot_tn(ah, bl) + _dot_tn(al, bh))

    return _call(body, name="dw_mod", grid=(1,), in_specs=[_full(a.shape), _full(b.shape)], out_specs=_full((kdim, n)),
                 out_shape=_sds((kdim, n)))(a, b)


def _adamw(w, g, m, v):
    r, n = w.shape
    tr = _row_tile(r)

    def body(w_ref, g_ref, m_ref, v_ref, d_ref, nm_ref, nv_ref):
        gv = g_ref[...]
        nm = ADAM_B1 * m_ref[...] + (1.0 - ADAM_B1) * gv
        nv = ADAM_B2 * v_ref[...] + (1.0 - ADAM_B2) * (gv * gv)
        m_hat = nm / (1.0 - ADAM_B1 ** ADAM_STEP)
        v_hat = nv / (1.0 - ADAM_B2 ** ADAM_STEP)
        d_ref[...] = -ADAM_LR * (m_hat / (jnp.sqrt(v_hat) + ADAM_EPS) + ADAM_WD * w_ref[...])
        nm_ref[...] = nm
        nv_ref[...] = nv

    spec = pl.BlockSpec((tr, n), lambda i: (i, 0))
    return _call(body, name="adamw", grid=(r // tr,), in_specs=[spec] * 4, out_specs=[spec] * 3,
                 out_shape=[_sds((r, n))] * 3, sem=("parallel",))(w, g, m, v)


BIG = ("w_in", "w_branch_a", "w_branch_b", "w_out", "w_up", "w_down", "conv_w")
BIG_ROWS = 3584
SMALL = ("b_mod", "b_in", "conv_b", "ln1_g", "ln1_b", "ln2_g", "ln2_b", "c_ctx", "attn_sink", "q_norm_g", "k_norm_g")
SMALL_ROWS = 8 * len(SMALL)


def _rows(a, n_rows):
    flat = a.reshape(-1)
    return jnp.pad(flat, (0, n_rows * D_MODEL - flat.shape[0])).reshape(n_rows, D_MODEL)


def _group8(a):
    return _rep8(_rows(a, 1)) if a.size <= D_MODEL else _rows(a, 8)


def _ungroup8(p, shape):
    size = math.prod(shape)
    return (p[0, :size] if size <= D_MODEL else p.reshape(-1)[:size]).reshape(shape)


def _pack_big(t):
    parts = [t[n].reshape(-1, D_MODEL) for n in BIG[:-1]] + [_rows(t["conv_w"], 8)]
    used = sum(p.shape[0] for p in parts)
    return jnp.concatenate(parts + [jnp.zeros((BIG_ROWS - used, D_MODEL), F32)], axis=0)


def _unpack_big(p, like):
    out, r = {}, 0
    for n in BIG:
        size = math.prod(like[n].shape)
        nr = size // D_MODEL if n != "conv_w" else 8
        out[n] = p[r:r + nr].reshape(-1)[:size].reshape(like[n].shape)
        r += nr
    return out


def _pack_small(t):
    return jnp.concatenate([_group8(t[n]) for n in SMALL], axis=0)


def _unpack_small(p, like):
    return {n: _ungroup8(p[8 * i:8 * i + 8], like[n].shape) for i, n in enumerate(SMALL)}


WEIGHTS = ("c_ctx", "w_mod", "b_mod", "w_in", "b_in", "attn_sink", "q_norm_g", "k_norm_g", "w_branch_a", "w_branch_b",
           "w_out", "ln1_g", "ln1_b", "w_up", "conv_w", "conv_b", "w_down", "ln2_g", "ln2_b")


def kernel(x, c, ctx, c_ctx, w_mod, b_mod, w_in, b_in, attn_sink, q_norm_g, k_norm_g, w_branch_a, w_branch_b, w_out, ln1_g, ln1_b, w_up, conv_w, conv_b, w_down, ln2_g, ln2_b, loss_target, m_c_ctx, m_w_mod, m_b_mod, m_w_in, m_b_in, m_attn_sink, m_q_norm_g, m_k_norm_g, m_w_branch_a, m_w_branch_b, m_w_out, m_ln1_g, m_ln1_b, m_w_up, m_conv_w, m_conv_b, m_w_down, m_ln2_g, m_ln2_b, v_c_ctx, v_w_mod, v_b_mod, v_w_in, v_b_in, v_attn_sink, v_q_norm_g, v_k_norm_g, v_w_branch_a, v_w_branch_b, v_w_out, v_ln1_g, v_ln1_b, v_w_up, v_conv_w, v_conv_b, v_w_down, v_ln2_g, v_ln2_b):
    w = dict(c_ctx=c_ctx, w_mod=w_mod, b_mod=b_mod, w_in=w_in, b_in=b_in, attn_sink=attn_sink, q_norm_g=q_norm_g,
             k_norm_g=k_norm_g, w_branch_a=w_branch_a, w_branch_b=w_branch_b, w_out=w_out, ln1_g=ln1_g, ln1_b=ln1_b,
             w_up=w_up, conv_w=conv_w, conv_b=conv_b, w_down=w_down, ln2_g=ln2_g, ln2_b=ln2_b)
    m = dict(c_ctx=m_c_ctx, w_mod=m_w_mod, b_mod=m_b_mod, w_in=m_w_in, b_in=m_b_in, attn_sink=m_attn_sink,
             q_norm_g=m_q_norm_g, k_norm_g=m_k_norm_g, w_branch_a=m_w_branch_a, w_branch_b=m_w_branch_b, w_out=m_w_out,
             ln1_g=m_ln1_g, ln1_b=m_ln1_b, w_up=m_w_up, conv_w=m_conv_w, conv_b=m_conv_b, w_down=m_w_down,
             ln2_g=m_ln2_g, ln2_b=m_ln2_b)
    v = dict(c_ctx=v_c_ctx, w_mod=v_w_mod, b_mod=v_b_mod, w_in=v_w_in, b_in=v_b_in, attn_sink=v_attn_sink,
             q_norm_g=v_q_norm_g, k_norm_g=v_k_norm_g, w_branch_a=v_w_branch_a, w_branch_b=v_w_branch_b, w_out=v_w_out,
             ln1_g=v_ln1_g, ln1_b=v_ln1_b, w_up=v_w_up, conv_w=v_conv_w, conv_b=v_conv_b, w_down=v_w_down,
             ln2_g=v_ln2_g, ln2_b=v_ln2_b)
    xp, yp, _ = _mesh_pos()
    me = 2 * xp + yp

    branches = jnp.concatenate([w_branch_a[0], w_branch_b[0]], axis=0)
    wide = jnp.concatenate([w_mod[0], w_in[0], w_up[0], branches], axis=1).astype(BF16)
    tall = jnp.concatenate([w_out[0], w_down[0]], axis=0).astype(BF16)
    wide4, tall4, cw4 = _gather_shards([wide, tall], conv_w[0])
    n_mod, n_in, n_up = w_mod.shape[-1], w_in.shape[-1], w_up.shape[-1]
    wmod4 = wide4[:, :, :n_mod]
    win4 = wide4[:, :, n_mod:n_mod + n_in]
    wup4 = wide4[:, :, n_mod + n_in:n_mod + n_in + n_up]
    br4 = wide4[:, :, n_mod + n_in + n_up:]
    n_br = w_branch_a.shape[1]
    wba = br4[:, :n_br].transpose(1, 0, 2).reshape(n_br, D_MODEL)
    wbb = br4[:, n_br:].transpose(1, 0, 2).reshape(n_br, D_MODEL)
    n_out = w_out.shape[1]
    w_out_full = tall4[:, :n_out].reshape(D_MODEL, D_MODEL)
    w_down_full = tall4[:, n_out:].reshape(D_FF, D_MODEL)
    cw_full = cw4.transpose(1, 0, 2).reshape(3, 2 * D_FF)

    loss, grad_x, g = _local_step(
        x[0], c, ctx[0], c_ctx[None], wmod4, b_mod, win4, b_in, attn_sink[0], q_norm_g, k_norm_g, wba, wbb, w_out_full,
        ln1_g, ln1_b, wup4, cw_full, conv_b, w_down_full, ln2_g, ln2_b, loss_target[0])
    loss = lax.psum(loss, ("x", "y", "c"))

    sent = dict(c=c, dmod=g["dmod"], dmodc=g["dmodc"], b_in=g["b_in"], conv_b=g["conv_b"], ln1_g=g["ln1_g"],
                ln1_b=g["ln1_b"], ln2_g=g["ln2_g"], ln2_b=g["ln2_b"], c_ctx=g["c_ctx"], attn_sink=g["sink"],
                q_norm_g=g["qn"], k_norm_g=g["kn"])
    every = _allgather_rows(jnp.concatenate([_group8(a) for a in sent.values()], axis=0))
    total = _sum_leading(every, name="sum_devices")
    slot = {n: slice(8 * i, 8 * i + 8) for i, n in enumerate(sent)}
    gs = {n: _ungroup8(total[slot[n]], sent[n].shape) for n in SMALL if n in sent}
    dmodc_sum = jnp.concatenate([_ungroup8(total[slot["dmodc"]], (1, 2 * D_MODEL)), jnp.zeros((1, 4 * D_MODEL), F32)],
                                axis=1)
    gs["b_mod"] = _ungroup8(total[slot["dmod"]], b_mod.shape) + dmodc_sum
    acts = jnp.concatenate([every[:, slot["c"].start], _rep8(c_ctx)], axis=0)
    dmods = jnp.concatenate([every[:, slot["dmod"]].reshape(N_DEV, -1)[:, :6 * D_MODEL], _first_row(dmodc_sum)], axis=0)
    g_w_mod = _silu_outer(acts, lax.dynamic_slice_in_dim(dmods, me * n_mod, n_mod, axis=1))

    cw_g4 = _to_blocks4(g["conv_w"])
    packed = jnp.concatenate([
        g["w_in4"].reshape(N_CHIPS, -1, D_MODEL), _to_blocks4(g["wba"]).reshape(N_CHIPS, -1, D_MODEL),
        _to_blocks4(g["wbb"]).reshape(N_CHIPS, -1, D_MODEL), g["w_out"].reshape(N_CHIPS, -1, D_MODEL),
        g["w_up4"].reshape(N_CHIPS, -1, D_MODEL), g["w_down"].reshape(N_CHIPS, -1, D_MODEL),
        jnp.pad(cw_g4.reshape(N_CHIPS, -1), ((0, 0), (0, 8 * D_MODEL - cw_g4.shape[1] * cw_g4.shape[2]))).reshape(
            N_CHIPS, 8, D_MODEL),
        jnp.zeros((N_CHIPS, BIG_ROWS - 3528, D_MODEL), F32)], axis=1)
    rh = BIG_ROWS // 2
    cpos = lax.axis_index("c")
    my_half = lax.dynamic_slice_in_dim(packed, cpos * rh, rh, axis=1)
    chip_sum = _add_blocks(my_half, _swap_other_half(packed))
    half_sum = _sum_leading(_scatter_to_chips(chip_sum), name="sum_chips")
    g_big = _unpack_big(_join_halves(half_sum).reshape(BIG_ROWS, D_MODEL), w)

    grads = dict(gs, w_mod=g_w_mod, **g_big)

    def pack_all(t):
        rows = jnp.concatenate([_pack_big(t), t["w_mod"].reshape(-1, D_MODEL), _pack_small(t)], axis=0)
        return jnp.pad(rows, ((0, -rows.shape[0] % 256), (0, 0)))

    delta_p, new_m_p, new_v_p = _adamw(pack_all(w), pack_all(grads), pack_all(m), pack_all(v))

    def unpack_all(p):
        r_mod = BIG_ROWS + w_mod.size // D_MODEL
        out = _unpack_big(p[:BIG_ROWS], w)
        out["w_mod"] = p[BIG_ROWS:r_mod].reshape(w_mod.shape)
        out.update(_unpack_small(p[r_mod:r_mod + SMALL_ROWS], w))
        return out

    grads = {n: grads[n].reshape(w[n].shape) for n in WEIGHTS}
    delta, new_m, new_v = unpack_all(delta_p), unpack_all(new_m_p), unpack_all(new_v_p)
    return (loss, grad_x[None], *[grads[n] for n in WEIGHTS], *[delta[n] for n in WEIGHTS],
            *[new_m[n] for n in WEIGHTS], *[new_v[n] for n in WEIGHTS])
```

```python
import functools
import math

import jax
import jax.numpy as jnp
from jax import lax
from jax.experimental import pallas as pl
from jax.experimental.pallas import tpu as pltpu

F32 = jnp.float32
BF16 = jnp.bfloat16

D_MODEL = 1024
HEAD_DIM = 64
N_HEADS = 8
N_KV = 2
WINDOW = 128
GRID_W = 64
ROPE_THETA = 10000.0
D_FF = 2816
LN_EPS = 1e-5
QK_EPS = 1e-6
ALPHA = 2.0 ** 0.25
Q_SCALE = HEAD_DIM ** -0.5
OFF_GA = 1536
IN_COLS = 3584
ADAM_LR, ADAM_B1, ADAM_B2, ADAM_EPS, ADAM_WD, ADAM_STEP = 0.001, 0.9, 0.999, 1e-8, 0.01, 10

LANES = 128
VMEM_BUDGET = 52 * 1024 * 1024
N_CHIPS = 4
N_DEV = 8
NEG = -1e30
MESH = pl.DeviceIdType.MESH


def _sigmoid(x):
    return 1.0 / (1.0 + jnp.exp(-x))


def _dot(a, b):
    return jnp.dot(a, b, preferred_element_type=F32)


def _dot_nt(a, b):
    return lax.dot_general(a, b, (((1,), (1,)), ((), ())), preferred_element_type=F32)


def _dot_tn(a, b):
    return lax.dot_general(a, b, (((0,), (0,)), ((), ())), preferred_element_type=F32)


def _call(body, *, name, grid, in_specs, out_specs, out_shape, scratch=(), sem=None, **kw):
    params = dict(vmem_limit_bytes=VMEM_BUDGET)
    if sem is not None:
        params["dimension_semantics"] = sem
    return pl.pallas_call(body, name=name, grid=grid, in_specs=in_specs, out_specs=out_specs,
                          out_shape=out_shape, scratch_shapes=list(scratch),
                          compiler_params=pltpu.CompilerParams(**params), **kw)


def _full(shape):
    n = len(shape)
    return pl.BlockSpec(shape, lambda *_: (0,) * n)


def _sds(shape, dtype=F32):
    return jax.ShapeDtypeStruct(shape, dtype)


def _mm_nn4(a, shift, scale, w4, bias, *, mode, split_out, out_dtype, tm, name):
    m, kdim = a.shape
    nb, _, ns = w4.shape

    def body(a_ref, sh_ref, sc_ref, w_ref, b_ref, o_ref):
        av = a_ref[...]
        if mode == "modulate":
            av = av * (1.0 + sc_ref[...]) + sh_ref[...]
        else:
            av = av * _sigmoid(av)
        o_ref[...] = (_dot(av.astype(BF16), w_ref[...]) + b_ref[...]).astype(out_dtype)

    if split_out:
        out_shape = _sds((2, m, 2 * ns), out_dtype)
        out_spec = pl.BlockSpec((None, tm, ns), lambda i, k: (k // 2, i, k % 2))
    else:
        out_shape = _sds((m, nb * ns), out_dtype)
        out_spec = pl.BlockSpec((tm, ns), lambda i, k: (i, k))
    return _call(
        body, name=name, grid=(m // tm, nb),
        in_specs=[pl.BlockSpec((tm, kdim), lambda i, k: (i, 0)),
                  pl.BlockSpec((1, kdim), lambda i, k: (0, 0)),
                  pl.BlockSpec((1, kdim), lambda i, k: (0, 0)),
                  pl.BlockSpec((None, kdim, ns), lambda i, k: (k, 0, 0)),
                  pl.BlockSpec((1, ns), lambda i, k: (0, k))],
        out_specs=out_spec, out_shape=out_shape, sem=("parallel", "arbitrary"),
    )(a, shift, scale, w4, bias)


def _mm_tn(a, b, *, a_spec, b_spec, grid, out_shape, out_spec, name, mod=None, init=None, colsum_spec=None,
           colsum_shape=None, a_is_t=False):
    red = len(grid) - 1
    has_mod, has_init, has_cs = mod is not None, init is not None, colsum_spec is not None

    def body(*refs):
        refs = list(refs)
        a_ref, b_ref = refs[0], refs[1]
        pos = 2
        if has_mod:
            sh_ref, sc_ref = refs[2], refs[3]
            pos = 4
        if has_init:
            init_ref = refs[pos]
            pos += 1
        o_ref = refs[pos]
        cs_ref = refs[pos + 1] if has_cs else None
        s = pl.program_id(red)

        @pl.when(s == 0)
        def _():
            o_ref[...] = init_ref[...] if has_init else jnp.zeros(o_ref.shape, F32)
            if has_cs:
                cs_ref[...] = jnp.zeros(cs_ref.shape, F32)

        av = a_ref[...]
        if has_mod:
            av = av * (1.0 + sc_ref[...]) + sh_ref[...]
        bv = b_ref[...]
        o_ref[...] += (_dot if a_is_t else _dot_tn)(av.astype(BF16), bv)
        if has_cs:
            cs_ref[...] += jnp.broadcast_to(jnp.sum(bv.astype(F32), axis=0, keepdims=True), cs_ref.shape)

    ins, in_specs = [a, b], [a_spec, b_spec]
    if has_mod:
        kdim = mod[0].shape[-1]
        ins += list(mod)
        in_specs += [_full((1, kdim)), _full((1, kdim))]
    if has_init:
        ins.append(init)
        in_specs.append(out_spec)
    out_specs, out_shapes = out_spec, out_shape
    if has_cs:
        out_specs, out_shapes = [out_spec, colsum_spec], [out_shape, colsum_shape]
    sem = ("parallel",) * red + ("arbitrary",)
    return _call(body, name=name, grid=grid, in_specs=in_specs, out_specs=out_specs, out_shape=out_shapes,
                 sem=sem)(*ins)


def _rope_tables(n_tok):
    pos = jnp.arange(n_tok, dtype=jnp.int32)
    rows = (pos // GRID_W).astype(F32)
    cols = (pos % GRID_W).astype(F32)
    n_freq = HEAD_DIM // 4
    inv_freq = ROPE_THETA ** (-jnp.arange(n_freq, dtype=F32) / n_freq)
    ang_r = rows[:, None] * inv_freq
    ang_c = cols[:, None] * inv_freq
    cos = jnp.concatenate([jnp.cos(ang_r)] * 2 + [jnp.cos(ang_c)] * 2, axis=-1)
    sin = jnp.concatenate([-jnp.sin(ang_r), jnp.sin(ang_r), -jnp.sin(ang_c), jnp.sin(ang_c)], axis=-1)
    return jnp.tile(cos, (1, 2)), jnp.tile(sin, (1, 2))


def _lane(shape):
    return lax.broadcasted_iota(jnp.int32, shape, 1)


def _rope_partner(t, lane):
    return jnp.where((lane % 32) < 16, pltpu.roll(t, LANES - 16, 1), pltpu.roll(t, 16, 1))


def _half_mean(s, lane):
    lo = jnp.sum(jnp.where(lane < HEAD_DIM, s, 0.0), axis=-1, keepdims=True)
    hi = jnp.sum(jnp.where(lane < HEAD_DIM, 0.0, s), axis=-1, keepdims=True)
    return jnp.where(lane < HEAD_DIM, lo, hi) * (1.0 / HEAD_DIM)


def _prep(proj, cos, sin, qg, kg, *, tm, name):
    m = proj.shape[0]

    def body(p_ref, cos_ref, sin_ref, qg_ref, kg_ref, qa_ref, ka_ref, va_ref, qb_ref, kb_ref, vb_ref):
        lane = _lane((tm, LANES))
        cosv, sinv = cos_ref[...], sin_ref[...]
        low = lane < HEAD_DIM

        def rope(t):
            return t * cosv + _rope_partner(t, lane) * sinv

        def rms(t, g):
            return t * lax.rsqrt(_half_mean(t * t, lane) + QK_EPS) * g

        def place(q_ref, j, chunk):
            sw = pltpu.roll(chunk, HEAD_DIM, 1)
            if j < 2:
                h0, h1 = jnp.where(low, chunk, 0.0), jnp.where(low, sw, 0.0)
            else:
                h0, h1 = jnp.where(low, 0.0, sw), jnp.where(low, 0.0, chunk)
            q_ref[2 * j] = h0.astype(BF16)
            q_ref[2 * j + 1] = h1.astype(BF16)

        for j in range(4):
            place(qa_ref, j, rope(p_ref[:, j * LANES:(j + 1) * LANES]) * Q_SCALE)
            place(qb_ref, j, rope(rms(p_ref[:, 768 + j * LANES:768 + (j + 1) * LANES], qg_ref[...])) * Q_SCALE)
        ka_ref[...] = rope(p_ref[:, 512:640]).astype(BF16)
        va_ref[...] = p_ref[:, 640:768].astype(BF16)
        kb_ref[...] = rope(rms(p_ref[:, 1280:1408], kg_ref[...])).astype(BF16)
        vb_ref[...] = p_ref[:, 1408:1536].astype(BF16)

    row = pl.BlockSpec((tm, LANES), lambda i: (i, 0))
    qspec = pl.BlockSpec((N_HEADS, tm, LANES), lambda i: (0, i, 0))
    return _call(
        body, name=name, grid=(m // tm,),
        in_specs=[pl.BlockSpec((tm, OFF_GA), lambda i: (i, 0)), row, row, _full((1, LANES)), _full((1, LANES))],
        out_specs=[qspec, row, row, qspec, row, row],
        out_shape=[_sds((N_HEADS, m, LANES), BF16), _sds((m, LANES), BF16), _sds((m, LANES), BF16),
                   _sds((N_HEADS, m, LANES), BF16), _sds((m, LANES), BF16), _sds((m, LANES), BF16)],
        sem=("parallel",),
    )(proj, cos, sin, qg, kg)


def _prep_bwd(dqa, dka, dva, dqb, dkb, dvb, proj, cos, sin, qg, kg, dgl, *, tm, name):
    m = proj.shape[0]

    def body(dqa_ref, dka_ref, dva_ref, dqb_ref, dkb_ref, dvb_ref, p_ref, cos_ref, sin_ref, qg_ref, kg_ref,
             dgl_ref, dp_ref, dqg_ref, dkg_ref):
        i = pl.program_id(0)
        lane = _lane((tm, LANES))
        cosv, sinv = cos_ref[...], sin_ref[...]
        low = lane < HEAD_DIM

        @pl.when(i == 0)
        def _():
            dqg_ref[...] = jnp.zeros(dqg_ref.shape, F32)
            dkg_ref[...] = jnp.zeros(dkg_ref.shape, F32)

        def unrope(d):
            return d * cosv - _rope_partner(d, lane) * sinv

        def unplace(dq_ref, j):
            d0, d1 = dq_ref[2 * j], dq_ref[2 * j + 1]
            if j < 2:
                return jnp.where(low, d0, pltpu.roll(d1, HEAD_DIM, 1))
            return jnp.where(low, pltpu.roll(d0, HEAD_DIM, 1), d1)

        def unrms(dtn, t, g):
            r = lax.rsqrt(_half_mean(t * t, lane) + QK_EPS)
            u = dtn * g
            dt = r * u - t * (r * r * r) * _half_mean(u * t, lane)
            return dt, jnp.sum(dtn * t * r, axis=0, keepdims=True)

        for j in range(4):
            dp_ref[:, j * LANES:(j + 1) * LANES] = (unrope(unplace(dqa_ref, j)) * Q_SCALE).astype(BF16)
            c0 = 768 + j * LANES
            dt, dg = unrms(unrope(unplace(dqb_ref, j)) * Q_SCALE, p_ref[:, c0:c0 + LANES], qg_ref[...])
            dp_ref[:, c0:c0 + LANES] = dt.astype(BF16)
            dqg_ref[:, j * LANES:(j + 1) * LANES] += dg
        dp_ref[:, 512:640] = unrope(dka_ref[...]).astype(BF16)
        dp_ref[:, 640:768] = dva_ref[...].astype(BF16)
        dt, dg = unrms(unrope(dkb_ref[...]), p_ref[:, 1280:1408], kg_ref[...])
        dp_ref[:, 1280:1408] = dt.astype(BF16)
        dkg_ref[...] += dg
        dp_ref[:, 1408:1536] = dvb_ref[...].astype(BF16)
        dp_ref[:, OFF_GA:] = dgl_ref[...]

    row = pl.BlockSpec((tm, LANES), lambda i: (i, 0))
    qspec = pl.BlockSpec((N_HEADS, tm, LANES), lambda i: (0, i, 0))
    return _call(
        body, name=name, grid=(m // tm,),
        in_specs=[qspec, row, row, qspec, row, row, pl.BlockSpec((tm, OFF_GA), lambda i: (i, 0)), row, row,
                  _full((1, LANES)), _full((1, LANES)), pl.BlockSpec((tm, IN_COLS - OFF_GA), lambda i: (i, 0))],
        out_specs=[pl.BlockSpec((tm, IN_COLS), lambda i: (i, 0)), _full((1, 512)), _full((1, LANES))],
        out_shape=[_sds((m, IN_COLS), BF16), _sds((1, 512)), _sds((1, LANES))],
        sem=("arbitrary",),
    )(dqa, dka, dva, dqb, dkb, dvb, proj, cos, sin, qg, kg, dgl)


def _attn_glob_fwd(qt, k, vt, kc, vct, *, tq, tk):
    nh, _, s = qt.shape
    nc = kc.shape[0]

    def body(qt_ref, k_ref, vt_ref, kc_ref, vct_ref, ot_ref, lse_ref, acc_sc, st_sc):
        qtv = qt_ref[...]
        acc_sc[...] = jnp.zeros(acc_sc.shape, F32)
        n_chunks = s // tk

        def update(st, vtv, m_old, l_old):
            m_new = jnp.maximum(m_old, jnp.max(st, axis=0, keepdims=True))
            pt = jnp.exp(st - m_new)
            al = jnp.exp(m_old - m_new)
            acc_sc[...] = acc_sc[...] * al + _dot(vtv, pt.astype(BF16))
            return m_new, l_old * al + jnp.sum(pt, axis=0, keepdims=True)

        def loop(c, carry):
            off = pl.multiple_of(c * tk, tk)
            nxt = pl.multiple_of(jnp.minimum(c + 1, n_chunks - 1) * tk, tk)
            st = st_sc[...]
            st_next = _dot(k_ref[pl.ds(nxt, tk), :], qtv)
            carry = update(st, vt_ref[:, pl.ds(off, tk)], *carry)
            st_sc[...] = st_next
            return carry

        init = (jnp.full((1, tq), NEG, F32), jnp.zeros((1, tq), F32))
        st_sc[...] = _dot(k_ref[pl.ds(0, tk), :], qtv)
        m, l = lax.fori_loop(0, n_chunks, loop, update(_dot(kc_ref[...], qtv), vct_ref[...], *init))
        ot_ref[...] = (acc_sc[...] / l).astype(BF16)
        lse_ref[...] = m + jnp.log(l)

    return _call(
        body, name="attn_glob_fwd", grid=(nh, s // tq),
        in_specs=[pl.BlockSpec((None, LANES, tq), lambda h, i: (h, 0, i)),
                  _full((s, LANES)), _full((LANES, s)), _full((nc, LANES)), _full((LANES, nc))],
        out_specs=[pl.BlockSpec((None, LANES, tq), lambda h, i: (h, 0, i)),
                   pl.BlockSpec((None, 1, tq), lambda h, i: (h, 0, i))],
        out_shape=[_sds((nh, LANES, s), BF16), _sds((nh, 1, s))],
        scratch=[pltpu.VMEM((LANES, tq), F32), pltpu.VMEM((tk, tq), F32)],
        sem=("parallel", "parallel"),
    )(qt, k, vt, kc, vct)


def _attn_glob_bwd(qt, dot, ot, lse, k, kt, v, kc, kct, vc, *, tq, tk):
    nh, _, s = qt.shape
    nc = kc.shape[0]
    n_q = s // tq

    def body(qt_ref, dot_ref, ot_ref, lse_ref, k_ref, kt_ref, v_ref, kc_ref, kct_ref, vc_ref,
             dqt_ref, dkt_ref, dvt_ref, dkct_ref, dvct_ref, acc_sc, st_sc, dp_sc, dkt_sc, dvt_sc):
        h, i = pl.program_id(0), pl.program_id(1)

        @pl.when(jnp.logical_and(h == 0, i == 0))
        def _():
            dkct_ref[...] = jnp.zeros(dkct_ref.shape, F32)
            dvct_ref[...] = jnp.zeros(dvct_ref.shape, F32)
            dkt_sc[...] = jnp.zeros(dkt_sc.shape, F32)
            dvt_sc[...] = jnp.zeros(dvt_sc.shape, F32)

        qtv, dotv, lse = qt_ref[...], dot_ref[...], lse_ref[...]
        delta = jnp.sum(dotv.astype(F32) * ot_ref[...].astype(F32), axis=0, keepdims=True)
        n_chunks = s // tk

        def grads(st, dpt):
            pt = jnp.exp(st - lse)
            return pt.astype(BF16), (pt * (dpt - delta)).astype(BF16)

        def loop(c, carry):
            off = pl.multiple_of(c * tk, tk)
            nxt = pl.multiple_of(jnp.minimum(c + 1, n_chunks - 1) * tk, tk)
            st, dpt = st_sc[...], dp_sc[...]
            st_next = _dot(k_ref[pl.ds(nxt, tk), :], qtv)
            dp_next = _dot(v_ref[pl.ds(nxt, tk), :], dotv)
            pb, dsb = grads(st, dpt)
            acc_sc[...] += _dot(kt_ref[:, pl.ds(off, tk)], dsb)
            dkt_sc[:, pl.ds(off, tk)] += _dot_nt(qtv, dsb)
            dvt_sc[:, pl.ds(off, tk)] += _dot_nt(dotv, pb)
            st_sc[...] = st_next
            dp_sc[...] = dp_next
            return carry

        st_sc[...] = _dot(k_ref[pl.ds(0, tk), :], qtv)
        dp_sc[...] = _dot(v_ref[pl.ds(0, tk), :], dotv)
        pb, dsb = grads(_dot(kc_ref[...], qtv), _dot(vc_ref[...], dotv))
        acc_sc[...] = _dot(kct_ref[...], dsb)
        dkct_ref[...] += _dot_nt(qtv, dsb)
        dvct_ref[...] += _dot_nt(dotv, pb)
        lax.fori_loop(0, n_chunks, loop, 0)
        dqt_ref[...] = acc_sc[...]

        @pl.when(jnp.logical_and(h == nh - 1, i == n_q - 1))
        def _():
            pltpu.sync_copy(dkt_sc, dkt_ref)
            pltpu.sync_copy(dvt_sc, dvt_ref)

    qs = pl.BlockSpec((None, LANES, tq), lambda h, i: (h, 0, i))
    rs = pl.BlockSpec((None, 1, tq), lambda h, i: (h, 0, i))
    return _call(
        body, name="attn_glob_bwd", grid=(nh, n_q),
        in_specs=[qs, qs, qs, rs, _full((s, LANES)), _full((LANES, s)), _full((s, LANES)), _full((nc, LANES)),
                  _full((LANES, nc)), _full((nc, LANES))],
        out_specs=[qs, ANY, ANY, _full((LANES, nc)), _full((LANES, nc))],
        out_shape=[_sds((nh, LANES, s)), _sds((LANES, s)), _sds((LANES, s)), _sds((LANES, nc)), _sds((LANES, nc))],
        scratch=[pltpu.VMEM((LANES, tq), F32), pltpu.VMEM((tk, tq), F32), pltpu.VMEM((tk, tq), F32),
                 pltpu.VMEM((LANES, s), F32), pltpu.VMEM((LANES, s), F32)],
        sem=("arbitrary", "arbitrary"),
    )(qt, dot, ot, lse, k, kt, v, kc, kct, vc)


def _attn_glob_dq(qt, dot, ot, lse, k, kt, v, kc, kct, vc, *, tq, tk):
    nh, _, s = qt.shape
    nc = kc.shape[0]

    def body(qt_ref, dot_ref, ot_ref, lse_ref, k_ref, kt_ref, v_ref, kc_ref, kct_ref, vc_ref,
             dqt_ref, dl_ref, dkct_ref, dvct_ref, acc_sc, st_sc, dp_sc):
        first = jnp.logical_and(pl.program_id(0) == 0, pl.program_id(1) == 0)

        @pl.when(first)
        def _():
            dkct_ref[...] = jnp.zeros(dkct_ref.shape, F32)
            dvct_ref[...] = jnp.zeros(dvct_ref.shape, F32)

        qtv, dotv, lse = qt_ref[...], dot_ref[...], lse_ref[...]
        delta = jnp.sum(dotv.astype(F32) * ot_ref[...].astype(F32), axis=0, keepdims=True)
        dl_ref[...] = delta

        n_chunks = s // tk

        def grads(st, dpt):
            pt = jnp.exp(st - lse)
            return pt.astype(BF16), (pt * (dpt - delta)).astype(BF16)

        def loop(c, carry):
            off = pl.multiple_of(c * tk, tk)
            nxt = pl.multiple_of(jnp.minimum(c + 1, n_chunks - 1) * tk, tk)
            st, dpt = st_sc[...], dp_sc[...]
            st_next = _dot(k_ref[pl.ds(nxt, tk), :], qtv)
            dp_next = _dot(v_ref[pl.ds(nxt, tk), :], dotv)
            _, dsb = grads(st, dpt)
            acc_sc[...] += _dot(kt_ref[:, pl.ds(off, tk)], dsb)
            st_sc[...] = st_next
            dp_sc[...] = dp_next
            return carry

        st_sc[...] = _dot(k_ref[pl.ds(0, tk), :], qtv)
        dp_sc[...] = _dot(v_ref[pl.ds(0, tk), :], dotv)
        pb, dsb = grads(_dot(kc_ref[...], qtv), _dot(vc_ref[...], dotv))
        acc_sc[...] = _dot(kct_ref[...], dsb)
        dkct_ref[...] += _dot_nt(qtv, dsb)
        dvct_ref[...] += _dot_nt(dotv, pb)
        lax.fori_loop(0, n_chunks, loop, 0)
        dqt_ref[...] = acc_sc[...]

    qs = pl.BlockSpec((None, LANES, tq), lambda h, i: (h, 0, i))
    rs = pl.BlockSpec((None, 1, tq), lambda h, i: (h, 0, i))
    return _call(
        body, name="attn_glob_dq", grid=(nh, s // tq),
        in_specs=[qs, qs, qs, rs, _full((s, LANES)), _full((LANES, s)), _full((s, LANES)), _full((nc, LANES)),
                  _full((LANES, nc)), _full((nc, LANES))],
        out_specs=[qs, rs, _full((LANES, nc)), _full((LANES, nc))],
        out_shape=[_sds((nh, LANES, s)), _sds((nh, 1, s)), _sds((LANES, nc)), _sds((LANES, nc))],
        scratch=[pltpu.VMEM((LANES, tq), F32), pltpu.VMEM((tk, tq), F32), pltpu.VMEM((tk, tq), F32)],
        sem=("arbitrary", "arbitrary"),
    )(qt, dot, ot, lse, k, kt, v, kc, kct, vc)


def _attn_glob_dkv(qt, dot, lse, dl, k, v, *, tq, tk):
    nh, _, s = qt.shape

    def body(k_ref, v_ref, qt_ref, dot_ref, lse_ref, dl_ref, dkt_ref, dvt_ref, st_sc, dp_sc):
        @pl.when(pl.program_id(1) == 0)
        def _():
            dkt_ref[...] = jnp.zeros(dkt_ref.shape, F32)
            dvt_ref[...] = jnp.zeros(dvt_ref.shape, F32)

        kv, vv = k_ref[...], v_ref[...]
        n_chunks = s // tq

        def loop(c, carry):
            off = pl.multiple_of(c * tq, tq)
            nxt = pl.multiple_of(jnp.minimum(c + 1, n_chunks - 1) * tq, tq)
            st, dpt = st_sc[...], dp_sc[...]
            st_next = _dot(kv, qt_ref[:, pl.ds(nxt, tq)])
            dp_next = _dot(vv, dot_ref[:, pl.ds(nxt, tq)])
            pt = jnp.exp(st - lse_ref[:, pl.ds(off, tq)])
            dst = pt * (dpt - dl_ref[:, pl.ds(off, tq)])
            dkt_ref[...] += _dot_nt(qt_ref[:, pl.ds(off, tq)], dst.astype(BF16))
            dvt_ref[...] += _dot_nt(dot_ref[:, pl.ds(off, tq)], pt.astype(BF16))
            st_sc[...] = st_next
            dp_sc[...] = dp_next
            return carry

        st_sc[...] = _dot(kv, qt_ref[:, pl.ds(0, tq)])
        dp_sc[...] = _dot(vv, dot_ref[:, pl.ds(0, tq)])
        lax.fori_loop(0, n_chunks, loop, 0)

    ks = pl.BlockSpec((tk, LANES), lambda j, h: (j, 0))
    ts = pl.BlockSpec((LANES, tk), lambda j, h: (0, j))
    qs = pl.BlockSpec((None, LANES, s), lambda j, h: (h, 0, 0))
    rs = pl.BlockSpec((None, 1, s), lambda j, h: (h, 0, 0))
    return _call(
        body, name="attn_glob_dkv", grid=(s // tk, nh),
        in_specs=[ks, ks, qs, qs, rs, rs], out_specs=[ts, ts],
        out_shape=[_sds((LANES, s)), _sds((LANES, s))],
        scratch=[pltpu.VMEM((tk, tq), F32), pltpu.VMEM((tk, tq), F32)],
        sem=("parallel", "arbitrary"),
    )(k, v, qt, dot, lse, dl)


WIN_SPAN = 2 * WINDOW


def _band(rows0, cols0, shape):
    r = rows0 + lax.broadcasted_iota(jnp.int32, shape, 0)
    c = cols0 + lax.broadcasted_iota(jnp.int32, shape, 1)
    return jnp.abs(r - c) <= WINDOW


def _win_start(blk, t, s):
    return pl.multiple_of(jnp.clip(blk * t - WINDOW, 0, s - t - WIN_SPAN), WINDOW)


def _attn_win_fwd(q, k, v, kc, vc, sink, *, tq):
    nh, s, _ = q.shape
    nc = kc.shape[0]
    tw = tq + WIN_SPAN

    def body(sink_ref, q_ref, k_ref, v_ref, kc_ref, vc_ref, o_ref, lse_ref):
        h, i = pl.program_id(0), pl.program_id(1)
        k0 = _win_start(i, tq, s)
        qv = q_ref[...]
        kv, vv = k_ref[pl.ds(k0, tw), :], v_ref[pl.ds(k0, tw), :]
        sc = jnp.where(_band(i * tq, k0, (tq, tw)), _dot_nt(qv, kv), NEG)
        scc = _dot_nt(qv, kc_ref[...])
        snk = sink_ref[h]
        m = jnp.maximum(jnp.maximum(jnp.max(sc, axis=-1, keepdims=True), jnp.max(scc, axis=-1, keepdims=True)), snk)
        p, pc = jnp.exp(sc - m), jnp.exp(scc - m)
        l = jnp.sum(p, axis=-1, keepdims=True) + jnp.sum(pc, axis=-1, keepdims=True) + jnp.exp(snk - m)
        acc = _dot(p.astype(BF16), vv) + _dot(pc.astype(BF16), vc_ref[...])
        o_ref[...] = (acc / l).astype(BF16)
        lse_ref[...] = m + jnp.log(l)

    return _call(
        body, name="attn_win_fwd", grid=(nh, s // tq),
        in_specs=[pl.BlockSpec(memory_space=pltpu.SMEM),
                  pl.BlockSpec((None, tq, LANES), lambda h, i: (h, i, 0)),
                  _full((s, LANES)), _full((s, LANES)), _full((nc, LANES)), _full((nc, LANES))],
        out_specs=[pl.BlockSpec((None, tq, LANES), lambda h, i: (h, i, 0)),
                   pl.BlockSpec((None, tq, 1), lambda h, i: (h, i, 0))],
        out_shape=[_sds((nh, s, LANES), BF16), _sds((nh, s, 1))],
        sem=("parallel", "parallel"),
    )(sink, q, k, v, kc, vc)


def _attn_win_dq(q, do, o, lse, k, v, kc, vc, sink, *, tq):
    nh, s, _ = q.shape
    nc = kc.shape[0]
    tw = tq + WIN_SPAN
    nq = s // tq

    def body(sink_ref, q_ref, do_ref, o_ref, lse_ref, k_ref, v_ref, kc_ref, vc_ref,
             dq_ref, dl_ref, dkc_ref, dvc_ref, dsk_ref):
        h, i = pl.program_id(0), pl.program_id(1)

        @pl.when(jnp.logical_and(h == 0, i == 0))
        def _():
            dkc_ref[...] = jnp.zeros(dkc_ref.shape, F32)
            dvc_ref[...] = jnp.zeros(dvc_ref.shape, F32)

        k0 = _win_start(i, tq, s)
        qv, dov, lse = q_ref[...], do_ref[...], lse_ref[...]
        kv, vv = k_ref[pl.ds(k0, tw), :], v_ref[pl.ds(k0, tw), :]
        kcv, vcv = kc_ref[...], vc_ref[...]
        delta = jnp.sum(dov.astype(F32) * o_ref[...].astype(F32), axis=-1, keepdims=True)
        dl_ref[...] = delta
        p = jnp.where(_band(i * tq, k0, (tq, tw)), jnp.exp(_dot_nt(qv, kv) - lse), 0.0)
        ds = (p * (_dot_nt(dov, vv) - delta)).astype(BF16)
        pc = jnp.exp(_dot_nt(qv, kcv) - lse)
        dsc = (pc * (_dot_nt(dov, vcv) - delta)).astype(BF16)
        dq_ref[...] = _dot(ds, kv) + _dot(dsc, kcv)
        dkc_ref[...] += _dot_tn(dsc, qv)
        dvc_ref[...] += _dot_tn(pc.astype(BF16), dov)
        dsk = -jnp.sum(jnp.exp(sink_ref[h] - lse) * delta)
        dsk_ref[...] = jnp.full(dsk_ref.shape, dsk, F32)

    qs = pl.BlockSpec((None, tq, LANES), lambda h, i: (h, i, 0))
    cs = pl.BlockSpec((None, tq, 1), lambda h, i: (h, i, 0))
    return _call(
        body, name="attn_win_dq", grid=(nh, nq),
        in_specs=[pl.BlockSpec(memory_space=pltpu.SMEM), qs, qs, qs, cs,
                  _full((s, LANES)), _full((s, LANES)), _full((nc, LANES)), _full((nc, LANES))],
        out_specs=[qs, cs, _full((nc, LANES)), _full((nc, LANES)),
                   pl.BlockSpec((None, None, 8, LANES), lambda h, i: (h, i, 0, 0))],
        out_shape=[_sds((nh, s, LANES)), _sds((nh, s, 1)), _sds((nc, LANES)), _sds((nc, LANES)),
                   _sds((nh, nq, 8, LANES))],
        sem=("arbitrary", "arbitrary"),
    )(sink, q, do, o, lse, k, v, kc, vc)


def _attn_win_dkv(q, do, lse_row, dl_row, k, v, *, tk):
    nh, s, _ = q.shape
    tw = tk + WIN_SPAN

    def body(k_ref, v_ref, q_ref, do_ref, lse_ref, dl_ref, dk_ref, dv_ref, dk_sc, dv_sc):
        h, j = pl.program_id(0), pl.program_id(1)
        q0 = _win_start(j, tk, s)
        kv, vv = k_ref[...], v_ref[...]
        qv, dov = q_ref[pl.ds(q0, tw), :], do_ref[pl.ds(q0, tw), :]
        pt = jnp.where(_band(j * tk, q0, (tk, tw)), jnp.exp(_dot_nt(kv, qv) - lse_ref[:, pl.ds(q0, tw)]), 0.0)
        dst = pt * (_dot_nt(vv, dov) - dl_ref[:, pl.ds(q0, tw)])
        dk, dv = _dot(dst.astype(BF16), qv), _dot(pt.astype(BF16), dov)
        rows = pl.ds(pl.multiple_of(j * tk, tk), tk)

        @pl.when(h == 0)
        def _():
            dk_sc[rows, :] = dk
            dv_sc[rows, :] = dv

        @pl.when(h > 0)
        def _():
            dk_sc[rows, :] += dk
            dv_sc[rows, :] += dv

        @pl.when(jnp.logical_and(h == nh - 1, j == s // tk - 1))
        def _():
            pltpu.sync_copy(dk_sc, dk_ref)
            pltpu.sync_copy(dv_sc, dv_ref)

    ks = pl.BlockSpec((tk, LANES), lambda h, j: (j, 0))
    qs = pl.BlockSpec((None, s, LANES), lambda h, j: (h, 0, 0))
    rs = pl.BlockSpec((None, 1, s), lambda h, j: (h, 0, 0))
    return _call(
        body, name="attn_win_dkv", grid=(nh, s // tk),
        in_specs=[ks, ks, qs, qs, rs, rs], out_specs=[ANY, ANY],
        out_shape=[_sds((s, LANES)), _sds((s, LANES))],
        scratch=[pltpu.VMEM((s, LANES), F32), pltpu.VMEM((s, LANES), F32)],
        sem=("arbitrary", "arbitrary"),
    )(k, v, q, do, lse_row, dl_row)


def _ln_fwd(z, g, b):
    mu = jnp.mean(z, axis=-1, keepdims=True)
    zc = z - mu
    r = lax.rsqrt(jnp.mean(zc * zc, axis=-1, keepdims=True) + LN_EPS)
    return zc * r * g + b, mu, r


def _ln_bwd(dy, xhat, r, g):
    dxh = dy * g
    return r * (dxh - jnp.mean(dxh, axis=-1, keepdims=True) - xhat * jnp.mean(dxh * xhat, axis=-1, keepdims=True))


def _heads_matmul(o_ref, w_ref):
    acc = _dot(o_ref[0], w_ref[0])
    for h in range(1, N_HEADS):
        acc += _dot(o_ref[h], w_ref[h])
    return acc


def _gate_specs(tm):
    return [pl.BlockSpec((tm, 512), functools.partial(lambda i, b: (i, b), b=OFF_GA // 512 + b)) for b in range(4)]


def _merge_fwd(oa, ob, proj, x, gate1, wba, wbb, w_out, ln_g, ln_b, *, tm):
    s = x.shape[0]

    def body(oa_ref, ob_ref, g0, g1, g2, g3, x_ref, gt_ref, wba_ref, wbb_ref, wo_ref, lg_ref, lb_ref,
             x1_ref, y_ref, mu_ref, r_ref):
        ga = _sigmoid(jnp.concatenate([g0[...], g1[...]], axis=1))
        gb = _sigmoid(jnp.concatenate([g2[...], g3[...]], axis=1))
        merged = ga * _heads_matmul(oa_ref, wba_ref) + gb * _heads_matmul(ob_ref, wbb_ref)
        y = _dot(merged.astype(BF16), wo_ref[...])
        x1, mu, r = _ln_fwd(ALPHA * x_ref[...] + gt_ref[...] * y, lg_ref[...], lb_ref[...])
        x1_ref[...] = x1
        y_ref[...] = y
        mu_ref[...] = mu
        r_ref[...] = r

    hs = pl.BlockSpec((N_HEADS, tm, LANES), lambda i: (0, i, 0))
    row = pl.BlockSpec((tm, D_MODEL), lambda i: (i, 0))
    col = pl.BlockSpec((tm, 1), lambda i: (i, 0))
    vec = _full((1, D_MODEL))
    wh = _full((N_HEADS, LANES, D_MODEL))
    return _call(
        body, name="merge_fwd", grid=(s // tm,),
        in_specs=[hs, hs, *_gate_specs(tm), row, vec, wh, wh, _full((D_MODEL, D_MODEL)), vec, vec],
        out_specs=[row, row, col, col],
        out_shape=[_sds((s, D_MODEL)), _sds((s, D_MODEL)), _sds((s, 1)), _sds((s, 1))],
        sem=("parallel",),
    )(oa, ob, proj, proj, proj, proj, x, gate1, wba, wbb, w_out, ln_g, ln_b)


def _merge_bwd(dy, oa, ob, proj, wba, wbb, w_out, *, tm):
    s = dy.shape[0]

    def body(dy_ref, oa_ref, ob_ref, g0, g1, g2, g3, wba_ref, wbb_ref, wo_ref,
             dgl_ref, doa_ref, dob_ref, dpa_ref, dpb_ref, mg_ref):
        dm = _dot_nt(dy_ref[...], wo_ref[...])
        ga = _sigmoid(jnp.concatenate([g0[...], g1[...]], axis=1))
        gb = _sigmoid(jnp.concatenate([g2[...], g3[...]], axis=1))
        pa, pb = _heads_matmul(oa_ref, wba_ref), _heads_matmul(ob_ref, wbb_ref)
        mg_ref[...] = (ga * pa + gb * pb).astype(BF16)
        dgl_ref[:, :D_MODEL] = (dm * pa * ga * (1.0 - ga)).astype(BF16)
        dgl_ref[:, D_MODEL:] = (dm * pb * gb * (1.0 - gb)).astype(BF16)
        dpa, dpb = (dm * ga).astype(BF16), (dm * gb).astype(BF16)
        dpa_ref[...] = dpa
        dpb_ref[...] = dpb
        for h in range(N_HEADS):
            doa_ref[h] = _dot_nt(dpa, wba_ref[h]).astype(BF16)
            dob_ref[h] = _dot_nt(dpb, wbb_ref[h]).astype(BF16)

    hs = pl.BlockSpec((N_HEADS, tm, LANES), lambda i: (0, i, 0))
    row = pl.BlockSpec((tm, D_MODEL), lambda i: (i, 0))
    wh = _full((N_HEADS, LANES, D_MODEL))
    return _call(
        body, name="merge_bwd", grid=(s // tm,),
        in_specs=[row, hs, hs, *_gate_specs(tm), wh, wh, _full((D_MODEL, D_MODEL))],
        out_specs=[pl.BlockSpec((tm, 2 * D_MODEL), lambda i: (i, 0)), hs, hs, row, row, row],
        out_shape=[_sds((s, 2 * D_MODEL), BF16), _sds((N_HEADS, s, LANES), BF16), _sds((N_HEADS, s, LANES), BF16),
                   _sds((s, D_MODEL), BF16), _sds((s, D_MODEL), BF16), _sds((s, D_MODEL), BF16)],
        sem=("parallel",),
    )(dy, oa, ob, proj, proj, proj, proj, wba, wbb, w_out)


FF_TC = 256


def _shift_rows(t, prev_row, next_row):
    n = t.shape[0]
    r = lax.broadcasted_iota(jnp.int32, t.shape, 0)
    up = jnp.where(r == 0, prev_row, pltpu.roll(t, 1, 0))
    dn = jnp.where(r == n - 1, next_row, pltpu.roll(t, n - 1, 0))
    return up, dn


def _halo_specs(tm, s, tc):
    nb8 = s // 8
    main = pl.BlockSpec((2, tm, tc), lambda j, i: (0, i, j))
    prev = pl.BlockSpec((2, 8, tc), lambda j, i: (0, jnp.maximum(i * (tm // 8) - 1, 0), j))
    nxt = pl.BlockSpec((2, 8, tc), lambda j, i: (0, jnp.minimum((i + 1) * (tm // 8), nb8 - 1), j))
    return main, prev, nxt


def _halo_rows(prev_ref, next_ref, half, i, n_i):
    prev_row = jnp.where(i == 0, 0.0, prev_ref[half, 7:8, :].astype(F32))
    next_row = jnp.where(i == n_i - 1, 0.0, next_ref[half, 0:1, :].astype(F32))
    return prev_row, next_row


def _conv(t, prev_row, next_row, w, b):
    up, dn = _shift_rows(t, prev_row, next_row)
    return w[0:1, :] * up + w[1:2, :] * t + w[2:3, :] * dn + b


def _ffn_act_fwd(u, cw, cb, *, tm):
    _, s, ff = u.shape
    n_i = s // tm

    def body(u_ref, up_ref, un_ref, cw_ref, cb_ref, a_ref):
        i = pl.program_id(1)
        gc = _conv(u_ref[0], *_halo_rows(up_ref, un_ref, 0, i, n_i), cw_ref[0], cb_ref[0])
        vc = _conv(u_ref[1], *_halo_rows(up_ref, un_ref, 1, i, n_i), cw_ref[1], cb_ref[1])
        a_ref[...] = (gc * _sigmoid(gc) * vc).astype(BF16)

    main, prev, nxt = _halo_specs(tm, s, FF_TC)
    return _call(
        body, name="ffn_act_fwd", grid=(ff // FF_TC, n_i),
        in_specs=[main, prev, nxt, pl.BlockSpec((2, 3, FF_TC), lambda j, i: (0, 0, j)),
                  pl.BlockSpec((2, 1, FF_TC), lambda j, i: (0, 0, j))],
        out_specs=pl.BlockSpec((tm, FF_TC), lambda j, i: (i, j)),
        out_shape=_sds((s, ff), BF16), sem=("parallel", "parallel"),
    )(u, u, u, cw, cb)


def _ffn_act_bwd(dy2, w_down, u, cw, cb, *, tm):
    _, s, ff = u.shape
    n_i = s // tm

    def body(dy_ref, wd_ref, u_ref, up_ref, un_ref, cw_ref, cb_ref, dc_ref, dcw_ref, dcb_ref):
        i = pl.program_id(1)

        @pl.when(i == 0)
        def _():
            dcw_ref[...] = jnp.zeros(dcw_ref.shape, F32)
            dcb_ref[...] = jnp.zeros(dcb_ref.shape, F32)

        da = _dot_nt(dy_ref[...], wd_ref[...])
        ug, uv = u_ref[0], u_ref[1]
        ugp, ugn = _shift_rows(ug, *_halo_rows(up_ref, un_ref, 0, i, n_i))
        uvp, uvn = _shift_rows(uv, *_halo_rows(up_ref, un_ref, 1, i, n_i))
        wg, wv = cw_ref[0], cw_ref[1]
        gc = wg[0:1, :] * ugp + wg[1:2, :] * ug + wg[2:3, :] * ugn + cb_ref[0]
        vc = wv[0:1, :] * uvp + wv[1:2, :] * uv + wv[2:3, :] * uvn + cb_ref[1]
        sg = _sigmoid(gc)
        dg = da * vc * sg * (1.0 + gc * (1.0 - sg))
        dv = da * gc * sg
        dc_ref[0] = dg
        dc_ref[1] = dv
        for half, (d, taps) in enumerate(((dg, (ugp, ug, ugn)), (dv, (uvp, uv, uvn)))):
            for tap in range(3):
                dcw_ref[half, tap:tap + 1, :] += jnp.sum(d * taps[tap], axis=0, keepdims=True)
            dcb_ref[half] += jnp.sum(d, axis=0, keepdims=True)

    main, prev, nxt = _halo_specs(tm, s, FF_TC)
    return _call(
        body, name="ffn_act_bwd", grid=(ff // FF_TC, n_i),
        in_specs=[pl.BlockSpec((tm, D_MODEL), lambda j, i: (i, 0)), pl.BlockSpec((FF_TC, D_MODEL), lambda j, i: (j, 0)),
                  main, prev, nxt, pl.BlockSpec((2, 3, FF_TC), lambda j, i: (0, 0, j)),
                  pl.BlockSpec((2, 1, FF_TC), lambda j, i: (0, 0, j))],
        out_specs=[main, pl.BlockSpec((2, 3, FF_TC), lambda j, i: (0, 0, j)),
                   pl.BlockSpec((2, 1, FF_TC), lambda j, i: (0, 0, j))],
        out_shape=[_sds((2, s, ff)), _sds((2, 3, ff)), _sds((2, 1, ff))],
        sem=("parallel", "arbitrary"),
    )(dy2, w_down, u, u, u, cw, cb)


def _conv_bwd_input(dc, cw, *, tm):
    _, s, ff = dc.shape
    n_i = s // tm

    def body(d_ref, dp_ref, dn_ref, cw_ref, du_ref):
        i = pl.program_id(1)
        for half in range(2):
            up, dn = _shift_rows(d_ref[half], *_halo_rows(dp_ref, dn_ref, half, i, n_i))
            w = cw_ref[half]
            du_ref[half] = (w[0:1, :] * dn + w[1:2, :] * d_ref[half] + w[2:3, :] * up).astype(BF16)

    main, prev, nxt = _halo_specs(tm, s, FF_TC)
    return _call(
        body, name="conv_bwd_input", grid=(ff // FF_TC, n_i),
        in_specs=[main, prev, nxt, pl.BlockSpec((2, 3, FF_TC), lambda j, i: (0, 0, j))],
        out_specs=main, out_shape=_sds((2, s, ff), BF16), sem=("parallel", "parallel"),
    )(dc, dc, dc, cw)


def _ffn_down_loss(a, w_down, x1, target, gate2, ln_g, ln_b, *, tm):
    s, ff = a.shape
    n_i = s // tm

    def body(a_ref, wd_ref, x1_ref, tg_ref, gt_ref, lg_ref, lb_ref, ls_ref, dy_ref, dx_ref, dg_ref, db_ref, dgt_ref):
        @pl.when(pl.program_id(0) == 0)
        def _():
            dg_ref[...] = jnp.zeros(dg_ref.shape, F32)
            db_ref[...] = jnp.zeros(db_ref.shape, F32)
            dgt_ref[...] = jnp.zeros(dgt_ref.shape, F32)

        y2 = _dot(a_ref[...], wd_ref[...])
        z = ALPHA * x1_ref[...] + gt_ref[...] * y2
        mu = jnp.mean(z, axis=-1, keepdims=True)
        zc = z - mu
        r = lax.rsqrt(jnp.mean(zc * zc, axis=-1, keepdims=True) + LN_EPS)
        xhat = zc * r
        diff = xhat * lg_ref[...] + lb_ref[...] - tg_ref[...]
        ls_ref[...] = jnp.full(ls_ref.shape, 0.5 / D_MODEL * jnp.sum(diff * diff), F32)
        dx2 = diff * (1.0 / D_MODEL)
        dg_ref[...] += jnp.sum(dx2 * xhat, axis=0, keepdims=True)
        db_ref[...] += jnp.sum(dx2, axis=0, keepdims=True)
        dz = _ln_bwd(dx2, xhat, r, lg_ref[...])
        dgt_ref[...] += jnp.sum(dz * y2, axis=0, keepdims=True)
        dy_ref[...] = (gt_ref[...] * dz).astype(BF16)
        dx_ref[...] = ALPHA * dz

    row = pl.BlockSpec((tm, D_MODEL), lambda i: (i, 0))
    vec = _full((1, D_MODEL))
    return _call(
        body, name="ffn_down_loss", grid=(n_i,),
        in_specs=[pl.BlockSpec((tm, ff), lambda i: (i, 0)), _full((ff, D_MODEL)), row, row, vec, vec, vec],
        out_specs=[pl.BlockSpec((None, 8, LANES), lambda i: (i, 0, 0)), row, row, vec, vec, vec],
        out_shape=[_sds((n_i, 8, LANES)), _sds((s, D_MODEL), BF16), _sds((s, D_MODEL)),
                   _sds((1, D_MODEL)), _sds((1, D_MODEL)), _sds((1, D_MODEL))],
        sem=("arbitrary",),
    )(a, w_down, x1, target, gate2, ln_g, ln_b)


def _ffn_up_bwd(du, wup4, dx1a, x1, scale2, x, y, mu1, r1, gate1, ln_g, *, tm):
    s = x.shape[0]
    nb, _, ns = wup4.shape

    def body(du_ref, w_ref, dxa_ref, x1_ref, sc_ref, x_ref, y_ref, mu_ref, r_ref, gt_ref, lg_ref,
             dxo_ref, dy_ref, dsc_ref, dsh_ref, dg_ref, db_ref, dgt_ref, acc):
        i, k = pl.program_id(0), pl.program_id(1)

        @pl.when(jnp.logical_and(i == 0, k == 0))
        def _():
            for ref in (dsc_ref, dsh_ref, dg_ref, db_ref, dgt_ref):
                ref[...] = jnp.zeros(ref.shape, F32)

        @pl.when(k == 0)
        def _():
            acc[...] = jnp.zeros(acc.shape, F32)

        acc[...] += _dot_nt(du_ref[...], w_ref[...])

        @pl.when(k == nb - 1)
        def _():
            dh = acc[...]
            x1 = x1_ref[...]
            dsc_ref[...] += jnp.sum(dh * x1, axis=0, keepdims=True)
            dsh_ref[...] += jnp.sum(dh, axis=0, keepdims=True)
            dx1 = dxa_ref[...] + dh * (1.0 + sc_ref[...])
            yv = y_ref[...]
            xhat = (ALPHA * x_ref[...] + gt_ref[...] * yv - mu_ref[...]) * r_ref[...]
            dg_ref[...] += jnp.sum(dx1 * xhat, axis=0, keepdims=True)
            db_ref[...] += jnp.sum(dx1, axis=0, keepdims=True)
            dz = _ln_bwd(dx1, xhat, r_ref[...], lg_ref[...])
            dgt_ref[...] += jnp.sum(dz * yv, axis=0, keepdims=True)
            dy_ref[...] = (gt_ref[...] * dz).astype(BF16)
            dxo_ref[...] = ALPHA * dz

    row = pl.BlockSpec((tm, D_MODEL), lambda i, k: (i, 0))
    col = pl.BlockSpec((tm, 1), lambda i, k: (i, 0))
    vec = _full((1, D_MODEL))
    return _call(
        body, name="ffn_up_bwd", grid=(s // tm, nb),
        in_specs=[pl.BlockSpec((None, tm, ns), lambda i, k: (k // 2, i, k % 2)),
                  pl.BlockSpec((None, D_MODEL, ns), lambda i, k: (k, 0, 0)),
                  row, row, vec, row, row, col, col, vec, vec],
        out_specs=[row, row, vec, vec, vec, vec, vec],
        out_shape=[_sds((s, D_MODEL)), _sds((s, D_MODEL), BF16)] + [_sds((1, D_MODEL))] * 5,
        scratch=[pltpu.VMEM((tm, D_MODEL), F32)],
        sem=("arbitrary", "arbitrary"),
    )(du, wup4, dx1a, x1, scale2, x, y, mu1, r1, gate1, ln_g)


def _mm_nt4_mod_bwd(dp, w4, dxa, x, scale, *, tm, name):
    m = x.shape[0]
    nb, kdim, ns = w4.shape

    def body(dp_ref, w_ref, dxa_ref, x_ref, sc_ref, dx_ref, dsc_ref, dsh_ref, acc):
        i, k = pl.program_id(0), pl.program_id(1)

        @pl.when(jnp.logical_and(i == 0, k == 0))
        def _():
            dsc_ref[...] = jnp.zeros(dsc_ref.shape, F32)
            dsh_ref[...] = jnp.zeros(dsh_ref.shape, F32)

        @pl.when(k == 0)
        def _():
            acc[...] = jnp.zeros(acc.shape, F32)

        acc[...] += _dot_nt(dp_ref[...], w_ref[...])

        @pl.when(k == nb - 1)
        def _():
            dh = acc[...]
            dsc_ref[...] += jnp.sum(dh * x_ref[...], axis=0, keepdims=True)
            dsh_ref[...] += jnp.sum(dh, axis=0, keepdims=True)
            dx_ref[...] = dxa_ref[...] + dh * (1.0 + sc_ref[...])

    row = pl.BlockSpec((tm, kdim), lambda i, k: (i, 0))
    vec = _full((1, kdim))
    return _call(
        body, name=name, grid=(m // tm, nb),
        in_specs=[pl.BlockSpec((tm, ns), lambda i, k: (i, k)), pl.BlockSpec((None, kdim, ns), lambda i, k: (k, 0, 0)),
                  row, row, vec],
        out_specs=[row, vec, vec],
        out_shape=[_sds((m, kdim)), _sds((1, kdim)), _sds((1, kdim))],
        scratch=[pltpu.VMEM((tm, kdim), F32)],
        sem=("arbitrary", "arbitrary"),
    )(dp, w4, dxa, x, scale)


def _pad_heads_w(w):
    w8 = w.reshape(N_HEADS, HEAD_DIM, w.shape[-1])
    z = jnp.zeros_like(w8)
    first = (jnp.arange(N_HEADS) < N_HEADS // N_KV)[:, None, None]
    return jnp.where(first, jnp.concatenate([w8, z], axis=1), jnp.concatenate([z, w8], axis=1))


def _unpad_heads_w(g):
    first = (jnp.arange(N_HEADS) < N_HEADS // N_KV)[:, None, None]
    return jnp.where(first, g[:, :HEAD_DIM], g[:, HEAD_DIM:]).reshape(N_HEADS * HEAD_DIM, g.shape[-1])


def _rep8(a):
    return jnp.broadcast_to(a.reshape(1, -1), (8, a.size))


def _first_row(a):
    r8 = _rep8(a)
    return jnp.where(lax.broadcasted_iota(jnp.int32, r8.shape, 0) == 0, r8, 0.0)


def _to_blocks4(w):
    k, n = w.shape
    return w.reshape(k, N_CHIPS, n // N_CHIPS).transpose(1, 0, 2)


def _local_step(x, c, ctx, c_ctx, wmod4, b_mod, win4, b_in, sink, qn, kn, wba, wbb, w_out, ln1_g, ln1_b,
                wup4, cw, cb, w_down, ln2_g, ln2_b, target):
    s, nc = x.shape[0], ctx.shape[0]
    tm = min(512, s)
    tm2 = min(256, s)
    zvec = jnp.zeros((1, D_MODEL), F32)

    cc = jnp.concatenate([_rep8(c), _rep8(c_ctx)], axis=0)
    mods = _mm_nn4(cc, zvec, zvec, wmod4, b_mod, mode="silu", split_out=False, out_dtype=F32, tm=16, name="mod_vectors")
    shift1, scale1, gate1, shift2, scale2, gate2 = [mods[0:1, i * D_MODEL:(i + 1) * D_MODEL] for i in range(6)]
    shift_c, scale_c = mods[8:9, :D_MODEL], mods[8:9, D_MODEL:2 * D_MODEL]

    cos, sin = _rope_tables(s)
    cos_c, sin_c = jnp.ones((nc, LANES), F32), jnp.zeros((nc, LANES), F32)
    qg, kg = jnp.tile(qn, (1, 2)), jnp.tile(kn, (1, 2))

    proj_c = _mm_nn4(ctx, shift_c, scale_c, win4, b_in, mode="modulate", split_out=False, out_dtype=F32, tm=nc,
                     name="in_proj_ctx")
    _, kac, vac, _, kbc, vbc = _prep(proj_c, cos_c, sin_c, qg, kg, tm=nc, name="prep_ctx")
    proj = _mm_nn4(x, shift1, scale1, win4, b_in, mode="modulate", split_out=False, out_dtype=F32, tm=tm, name="in_proj")
    qa, ka, va, qb, kb, vb = _prep(proj, cos, sin, qg, kg, tm=tm, name="prep")
    oa, lse_a = _attn_win_fwd(qa, ka, va, kac, vac, sink, tq=tm)
    qbt = jnp.swapaxes(qb, 1, 2)
    obt, lse_b = _attn_glob_fwd(qbt, kb, vb.T, kbc, vbc.T, tq=tm, tk=min(1024, s))
    ob = jnp.swapaxes(obt, 1, 2)
    wba_p, wbb_p = _pad_heads_w(wba), _pad_heads_w(wbb)
    x1, y, mu1, r1 = _merge_fwd(oa, ob, proj, x, gate1, wba_p, wbb_p, w_out, ln1_g, ln1_b, tm=tm2)
    u = _mm_nn4(x1, shift2, scale2, wup4, jnp.zeros((1, 2 * D_FF), F32), mode="modulate", split_out=True,
                out_dtype=F32, tm=tm, name="ffn_up")
    cw2 = cw.reshape(3, 2, D_FF).transpose(1, 0, 2)
    cb2 = cb.reshape(2, 1, D_FF)
    a = _ffn_act_fwd(u, cw2, cb2, tm=tm)
    ls, dy2, dx1a, dln2_g, dln2_b, dgate2 = _ffn_down_loss(a, w_down, x1, target, gate2, ln2_g, ln2_b, tm=tm2)
    loss = jnp.sum(ls[:, 0, 0])

    n_s = s // tm
    dw_down = _mm_tn(a, dy2, a_spec=pl.BlockSpec((tm, D_FF), lambda t: (t, 0)),
                     b_spec=pl.BlockSpec((tm, D_MODEL), lambda t: (t, 0)), grid=(n_s,),
                     out_shape=_sds((D_FF, D_MODEL)), out_spec=_full((D_FF, D_MODEL)), name="dw_down")
    dc, dcw2, dcb2 = _ffn_act_bwd(dy2, w_down, u, cw2, cb2, tm=tm)
    du = _conv_bwd_input(dc, cw2, tm=tm)
    dxz1, dy, dscale2, dshift2, dln1_g, dln1_b, dgate1 = _ffn_up_bwd(
        du, wup4, dx1a, x1, scale2, x, y, mu1, r1, gate1, ln1_g, tm=tm2)
    ns_up = wup4.shape[-1]
    dw_up4 = _mm_tn(x1, du, a_spec=pl.BlockSpec((tm, D_MODEL), lambda k, t: (t, 0)),
                    b_spec=pl.BlockSpec((None, tm, ns_up), lambda k, t: (k // 2, t, k % 2)), grid=(N_CHIPS, n_s),
                    out_shape=_sds((N_CHIPS, D_MODEL, ns_up)),
                    out_spec=pl.BlockSpec((None, D_MODEL, ns_up), lambda k, t: (k, 0, 0)),
                    mod=(shift2, scale2), name="dw_up")

    dgl, doa, dob, dpa, dpb, merged = _merge_bwd(dy, oa, ob, proj, wba_p, wbb_p, w_out, tm=tm2)
    rowspec = pl.BlockSpec((tm, D_MODEL), lambda t: (t, 0))
    dw_out = _mm_tn(merged, dy, a_spec=rowspec, b_spec=rowspec, grid=(n_s,), out_shape=_sds((D_MODEL, D_MODEL)),
                    out_spec=_full((D_MODEL, D_MODEL)), name="dw_out")
    hspec = dict(a_spec=pl.BlockSpec((None, LANES, tm), lambda h, t: (h, 0, t)),
                 b_spec=pl.BlockSpec((tm, D_MODEL), lambda h, t: (t, 0)), grid=(N_HEADS, n_s),
                 out_shape=_sds((N_HEADS, LANES, D_MODEL)),
                 out_spec=pl.BlockSpec((None, LANES, D_MODEL), lambda h, t: (h, 0, 0)), a_is_t=True)
    dwba = _unpad_heads_w(_mm_tn(jnp.swapaxes(oa, 1, 2), dpa, name="dw_branch_a", **hspec))
    dwbb = _unpad_heads_w(_mm_tn(obt, dpb, name="dw_branch_b", **hspec))

    dqa, dla, dkac, dvac, dsk = _attn_win_dq(qa, doa, oa, lse_a, ka, va, kac, vac, sink, tq=tm)
    dka, dva = _attn_win_dkv(qa, doa, lse_a.reshape(N_HEADS, 1, s), dla.reshape(N_HEADS, 1, s), ka, va, tk=tm)
    dobt = jnp.swapaxes(dob, 1, 2)
    dqbt, dkbt, dvbt, dkbct, dvbct = _attn_glob_bwd(qbt, dobt, obt, lse_b, kb, kb.T, vb, kbc, kbc.T, vbc, tq=tm, tk=tm)
    dqb, dkb, dvb, dkbc, dvbc = jnp.swapaxes(dqbt, 1, 2), dkbt.T, dvbt.T, dkbct.T, dvbct.T
    dsink = jnp.sum(dsk[:, :, 0, 0], axis=1)

    dproj, dqg, dkg = _prep_bwd(dqa, dka, dva, dqb, dkb, dvb, proj, cos, sin, qg, kg, dgl, tm=tm, name="prep_bwd")
    grad_x, dscale1, dshift1 = _mm_nt4_mod_bwd(dproj, win4, dxz1, x, scale1, tm=tm, name="in_proj_bwd")
    ns_in = win4.shape[-1]
    win_spec = dict(b_spec=pl.BlockSpec((None, None, ns_in), lambda k, t: (0, 0, k)),
                    out_shape=_sds((N_CHIPS, D_MODEL, ns_in)),
                    out_spec=pl.BlockSpec((None, D_MODEL, ns_in), lambda k, t: (k, 0, 0)),
                    colsum_spec=pl.BlockSpec((8, ns_in), lambda k, t: (0, k)), colsum_shape=_sds((8, IN_COLS)))
    win_spec["b_spec"] = pl.BlockSpec((tm, ns_in), lambda k, t: (t, k))
    dw_in4, db_in = _mm_tn(x, dproj, a_spec=pl.BlockSpec((tm, D_MODEL), lambda k, t: (t, 0)), grid=(N_CHIPS, n_s),
                           mod=(shift1, scale1), name="dw_in", **win_spec)

    zq = jnp.zeros((N_HEADS, nc, LANES), F32)
    dproj_c, _, dkg_c = _prep_bwd(zq, dkac, dvac, zq, dkbc, dvbc, proj_c, cos_c, sin_c, qg, kg,
                                  jnp.zeros((nc, IN_COLS - OFF_GA), BF16), tm=nc, name="prep_bwd_ctx")
    _, dscale_c, dshift_c = _mm_nt4_mod_bwd(dproj_c, win4, jnp.zeros((nc, D_MODEL), F32), ctx, scale_c, tm=nc,
                                            name="in_proj_bwd_ctx")
    win_spec["b_spec"] = pl.BlockSpec((nc, ns_in), lambda k, t: (t, k))
    dw_in4, db_in_c = _mm_tn(ctx, dproj_c, a_spec=pl.BlockSpec((nc, D_MODEL), lambda k, t: (t, 0)), grid=(N_CHIPS, 1),
                             mod=(shift_c, scale_c), init=dw_in4, name="dw_in_ctx", **win_spec)

    dmod = jnp.concatenate([dshift1, dscale1, dgate1, dshift2, dscale2, dgate2], axis=1)
    dmodc = jnp.concatenate([dshift_c, dscale_c], axis=1)
    dmodc_pad = jnp.concatenate([dmodc, jnp.zeros((1, 4 * D_MODEL), F32)], axis=1)
    dmodc8 = _first_row(dmodc_pad).astype(BF16)
    z8 = jnp.zeros((8, D_MODEL), F32)
    dsilu_c, _, _ = _mm_nt4_mod_bwd(dmodc8, wmod4, z8, z8, zvec, tm=8, name="c_ctx_bwd")
    sg = _sigmoid(c_ctx)
    dc_ctx = dsilu_c[0:1] * sg * (1.0 + c_ctx * (1.0 - sg))

    dqn = jnp.sum(dqg.reshape(N_HEADS, HEAD_DIM), axis=0, keepdims=True)
    dkn = jnp.sum((dkg + dkg_c).reshape(N_KV, HEAD_DIM), axis=0, keepdims=True)
    grads = dict(
        w_in4=dw_in4, b_in=db_in[0:1] + db_in_c[0:1], sink=dsink, qn=dqn, kn=dkn, wba=dwba, wbb=dwbb, w_out=dw_out,
        ln1_g=dln1_g, ln1_b=dln1_b, w_up4=dw_up4, conv_w=dcw2.transpose(1, 0, 2).reshape(3, 2 * D_FF),
        conv_b=dcb2.reshape(1, 2 * D_FF), w_down=dw_down, ln2_g=dln2_g, ln2_b=dln2_b,
        c_ctx=dc_ctx, dmod=dmod, dmodc=dmodc)
    return loss, grad_x, grads


ANY = pl.BlockSpec(memory_space=pl.ANY)


def _mesh_pos():
    return lax.axis_index("x"), lax.axis_index("y"), lax.axis_index("c")


def _other_chips(x, y):
    return [(1 - x, y), (x, 1 - y), (1 - x, 1 - y)]


def _remote(src, dst, send, recv, dev):
    return pltpu.make_async_remote_copy(src_ref=src, dst_ref=dst, send_sem=send, recv_sem=recv, device_id=dev,
                                        device_id_type=MESH)


def _gather_shards(arrs, small):
    na = len(arrs)
    halves = [a.shape[0] // 2 for a in arrs]

    def body(*refs):
        ins, small_ref = refs[:na], refs[na]
        outs, small_out = refs[na + 1:2 * na + 1], refs[2 * na + 1]
        send, recv, loc = refs[2 * na + 2:]
        x, y, c = _mesh_pos()
        me = 2 * x + y
        chips = _other_chips(x, y)

        def half(a, cc):
            return pl.ds(cc * halves[a], halves[a])

        local = [pltpu.make_async_copy(ins[a], outs[a].at[me], loc.at[a]) for a in range(na)]
        local.append(pltpu.make_async_copy(small_ref, small_out.at[me], loc.at[na]))
        for cp in local:
            cp.start()
        sends = []
        for j, chip in enumerate(chips):
            for a in range(na):
                sends.append(_remote(ins[a].at[half(a, c)], outs[a].at[me, half(a, c)], send.at[a, j], recv.at[a, j],
                                     (*chip, c)))
            sends.append(_remote(small_ref, small_out.at[me], send.at[na, j], recv.at[na, j], (*chip, c)))
        for cp in sends:
            cp.start()
        for j, chip in enumerate(chips):
            kj = 2 * chip[0] + chip[1]
            for a in range(na):
                landed = outs[a].at[kj, half(a, c)]
                _remote(landed, landed, send.at[a, j], recv.at[a, j], (*chip, c)).wait_recv()
                fwd = _remote(landed, landed, send.at[a, 3 + j], recv.at[a, 3 + j], (x, y, 1 - c))
                fwd.start()
                sends.append(fwd)
            _remote(small_ref, small_out.at[kj], send.at[na, j], recv.at[na, j], (*chip, c)).wait_recv()
        for j, chip in enumerate(chips):
            kj = 2 * chip[0] + chip[1]
            for a in range(na):
                other = outs[a].at[kj, half(a, 1 - c)]
                _remote(other, other, send.at[a, 3 + j], recv.at[a, 3 + j], (x, y, 1 - c)).wait_recv()
        for cp in sends:
            cp.wait_send()
        for cp in local:
            cp.wait()

    out_shape = [_sds((N_CHIPS,) + a.shape, a.dtype) for a in arrs] + [_sds((N_CHIPS,) + small.shape, small.dtype)]
    return pl.pallas_call(
        body, name="gather_shards", in_specs=[ANY] * (na + 1), out_specs=[ANY] * (na + 1), out_shape=out_shape,
        scratch_shapes=[pltpu.SemaphoreType.DMA((na + 1, 6)), pltpu.SemaphoreType.DMA((na + 1, 6)),
                        pltpu.SemaphoreType.DMA((na + 1,))],
    )(*arrs, small)


def _allgather_rows(v):
    r, n = v.shape

    def body(v_ref, out_ref, send, recv, loc):
        x, y, c = _mesh_pos()
        me, sibling = (x, y, c), (x, y, 1 - c)
        chips = _other_chips(x, y)

        def rows(px, py, pc):
            return out_ref.at[4 * px + 2 * py + pc]

        def copy(k, block, to, src=None):
            return _remote(rows(*block) if src is None else src, rows(*block), send.at[k], recv.at[k], to)

        mine = pltpu.make_async_copy(v_ref, rows(*me), loc)
        mine.start()
        first = [copy(0, me, sibling, src=v_ref)] + [copy(1 + j, me, (*chip, c), src=v_ref) for j, chip in enumerate(chips)]
        for cp in first:
            cp.start()
        passed = [copy(4 + j, (*chip, c), sibling) for j, chip in enumerate(chips)]
        for j, chip in enumerate(chips):
            copy(1 + j, (*chip, c), me).wait_recv()
            passed[j].start()
        copy(0, sibling, me).wait_recv()
        for j, chip in enumerate(chips):
            copy(4 + j, (*chip, 1 - c), me).wait_recv()
        for cp in first + passed:
            cp.wait_send()
        mine.wait()

    return pl.pallas_call(
        body, name="allgather_rows", in_specs=[pl.BlockSpec(memory_space=pltpu.VMEM)],
        out_specs=pl.BlockSpec(memory_space=pltpu.VMEM), out_shape=_sds((N_DEV, r, n), v.dtype),
        scratch_shapes=[pltpu.SemaphoreType.DMA((7,)), pltpu.SemaphoreType.DMA((7,)), pltpu.SemaphoreType.DMA],
    )(v)


def _swap_other_half(g):
    nb, r, n = g.shape
    rh = r // 2

    def body(g_ref, out_ref, send, recv):
        x, y, c = _mesh_pos()
        cp = _remote(g_ref.at[:, pl.ds((1 - c) * rh, rh), :], out_ref, send, recv, (x, y, 1 - c))
        cp.start()
        cp.wait()

    return pl.pallas_call(
        body, name="swap_other_half", in_specs=[ANY], out_specs=ANY, out_shape=_sds((nb, rh, n), g.dtype),
        scratch_shapes=[pltpu.SemaphoreType.DMA, pltpu.SemaphoreType.DMA],
    )(g)


def _scatter_to_chips(p):
    def body(p_ref, out_ref, send, recv, loc):
        x, y, c = _mesh_pos()
        me = 2 * x + y
        chips = _other_chips(x, y)
        mine = pltpu.make_async_copy(p_ref.at[me], out_ref.at[me], loc)
        mine.start()
        sends = [_remote(p_ref.at[2 * chip[0] + chip[1]], out_ref.at[me], send.at[j], recv.at[j], (*chip, c))
                 for j, chip in enumerate(chips)]
        for cp in sends:
            cp.start()
        for j, chip in enumerate(chips):
            kj = 2 * chip[0] + chip[1]
            _remote(p_ref.at[kj], out_ref.at[kj], send.at[j], recv.at[j], (*chip, c)).wait_recv()
        for cp in sends:
            cp.wait_send()
        mine.wait()

    return pl.pallas_call(
        body, name="scatter_to_chips", in_specs=[ANY], out_specs=ANY, out_shape=_sds(p.shape, p.dtype),
        scratch_shapes=[pltpu.SemaphoreType.DMA((3,)), pltpu.SemaphoreType.DMA((3,)), pltpu.SemaphoreType.DMA],
    )(p)


def _join_halves(f):
    def body(f_ref, out_ref, send, recv, loc):
        x, y, c = _mesh_pos()
        mine = pltpu.make_async_copy(f_ref, out_ref.at[c], loc)
        mine.start()
        cp = _remote(f_ref, out_ref.at[c], send, recv, (x, y, 1 - c))
        cp.start()
        _remote(f_ref, out_ref.at[1 - c], send, recv, (x, y, 1 - c)).wait_recv()
        cp.wait_send()
        mine.wait()

    return pl.pallas_call(
        body, name="join_halves", in_specs=[ANY], out_specs=ANY, out_shape=_sds((2,) + f.shape, f.dtype),
        scratch_shapes=[pltpu.SemaphoreType.DMA, pltpu.SemaphoreType.DMA, pltpu.SemaphoreType.DMA],
    )(f)


def _row_tile(rows, cap=512):
    t = cap - cap % 8
    while rows % t:
        t -= 8
    return t


def _add_blocks(a, b, out_dtype):
    nb, r, n = a.shape
    tr = _row_tile(r)

    def body(a_ref, b_ref, o_ref):
        o_ref[...] = (a_ref[...] + b_ref[...]).astype(out_dtype)

    spec = pl.BlockSpec((None, tr, n), lambda k, i: (k, i, 0))
    return _call(body, name="add_blocks", grid=(nb, r // tr), in_specs=[spec, spec], out_specs=spec,
                 out_shape=_sds(a.shape, out_dtype), sem=("parallel", "parallel"))(a, b)


def _sum_leading(a, *, name):
    nk, r, n = a.shape
    tr = _row_tile(r)

    def body(a_ref, o_ref):
        acc = a_ref[0].astype(F32)
        for k in range(1, nk):
            acc = acc + a_ref[k].astype(F32)
        o_ref[...] = acc

    return _call(body, name=name, grid=(r // tr,), in_specs=[pl.BlockSpec((nk, tr, n), lambda i: (0, i, 0))],
                 out_specs=pl.BlockSpec((tr, n), lambda i: (i, 0)), out_shape=_sds((r, n)), sem=("parallel",))(a)


def _silu_outer(a, b):
    kdim, n = a.shape[1], b.shape[1]

    def body(a_ref, b_ref, o_ref):
        av = a_ref[...]
        av = av * _sigmoid(av)
        bv = b_ref[...]
        ah, bh = av.astype(BF16), bv.astype(BF16)
        al, bl = (av - ah.astype(F32)).astype(BF16), (bv - bh.astype(F32)).astype(BF16)
        o_ref[...] = _dot_tn(ah, bh) + (_dot_tn(ah, bl) + _dot_tn(al, bh))

    return _call(body, name="dw_mod", grid=(1,), in_specs=[_full(a.shape), _full(b.shape)], out_specs=_full((kdim, n)),
                 out_shape=_sds((kdim, n)))(a, b)


def _adamw(w, g, m, v):
    r, n = w.shape
    tr = _row_tile(r)

    def body(w_ref, g_ref, m_ref, v_ref, d_ref, nm_ref, nv_ref):
        gv = g_ref[...]
        nm = ADAM_B1 * m_ref[...] + (1.0 - ADAM_B1) * gv
        nv = ADAM_B2 * v_ref[...] + (1.0 - ADAM_B2) * (gv * gv)
        m_hat = nm / (1.0 - ADAM_B1 ** ADAM_STEP)
        v_hat = nv / (1.0 - ADAM_B2 ** ADAM_STEP)
        d_ref[...] = -ADAM_LR * (m_hat / (jnp.sqrt(v_hat) + ADAM_EPS) + ADAM_WD * w_ref[...])
        nm_ref[...] = nm
        nv_ref[...] = nv

    spec = pl.BlockSpec((tr, n), lambda i: (i, 0))
    return _call(body, name="adamw", grid=(r // tr,), in_specs=[spec] * 4, out_specs=[spec] * 3,
                 out_shape=[_sds((r, n))] * 3, sem=("parallel",))(w, g, m, v)


BIG = ("w_in", "w_branch_a", "w_branch_b", "w_out", "w_up", "w_down", "conv_w")
BIG_ROWS = 3584
SMALL = ("b_mod", "b_in", "conv_b", "ln1_g", "ln1_b", "ln2_g", "ln2_b", "c_ctx", "attn_sink", "q_norm_g", "k_norm_g")
SMALL_ROWS = 8 * len(SMALL)


def _rows(a, n_rows):
    flat = a.reshape(-1)
    return jnp.pad(flat, (0, n_rows * D_MODEL - flat.shape[0])).reshape(n_rows, D_MODEL)


def _group8(a):
    return _rep8(_rows(a, 1)) if a.size <= D_MODEL else _rows(a, 8)


def _ungroup8(p, shape):
    size = math.prod(shape)
    return (p[0, :size] if size <= D_MODEL else p.reshape(-1)[:size]).reshape(shape)


def _pack_big(t):
    parts = [t[n].reshape(-1, D_MODEL) for n in BIG[:-1]] + [_rows(t["conv_w"], 8)]
    used = sum(p.shape[0] for p in parts)
    return jnp.concatenate(parts + [jnp.zeros((BIG_ROWS - used, D_MODEL), F32)], axis=0)


def _unpack_big(p, like):
    out, r = {}, 0
    for n in BIG:
        size = math.prod(like[n].shape)
        nr = size // D_MODEL if n != "conv_w" else 8
        out[n] = p[r:r + nr].reshape(-1)[:size].reshape(like[n].shape)
        r += nr
    return out


def _pack_small(t):
    return jnp.concatenate([_group8(t[n]) for n in SMALL], axis=0)


def _unpack_small(p, like):
    return {n: _ungroup8(p[8 * i:8 * i + 8], like[n].shape) for i, n in enumerate(SMALL)}


WEIGHTS = ("c_ctx", "w_mod", "b_mod", "w_in", "b_in", "attn_sink", "q_norm_g", "k_norm_g", "w_branch_a", "w_branch_b",
           "w_out", "ln1_g", "ln1_b", "w_up", "conv_w", "conv_b", "w_down", "ln2_g", "ln2_b")


def kernel(x, c, ctx, c_ctx, w_mod, b_mod, w_in, b_in, attn_sink, q_norm_g, k_norm_g, w_branch_a, w_branch_b, w_out, ln1_g, ln1_b, w_up, conv_w, conv_b, w_down, ln2_g, ln2_b, loss_target, m_c_ctx, m_w_mod, m_b_mod, m_w_in, m_b_in, m_attn_sink, m_q_norm_g, m_k_norm_g, m_w_branch_a, m_w_branch_b, m_w_out, m_ln1_g, m_ln1_b, m_w_up, m_conv_w, m_conv_b, m_w_down, m_ln2_g, m_ln2_b, v_c_ctx, v_w_mod, v_b_mod, v_w_in, v_b_in, v_attn_sink, v_q_norm_g, v_k_norm_g, v_w_branch_a, v_w_branch_b, v_w_out, v_ln1_g, v_ln1_b, v_w_up, v_conv_w, v_conv_b, v_w_down, v_ln2_g, v_ln2_b):
    w = dict(c_ctx=c_ctx, w_mod=w_mod, b_mod=b_mod, w_in=w_in, b_in=b_in, attn_sink=attn_sink, q_norm_g=q_norm_g,
             k_norm_g=k_norm_g, w_branch_a=w_branch_a, w_branch_b=w_branch_b, w_out=w_out, ln1_g=ln1_g, ln1_b=ln1_b,
             w_up=w_up, conv_w=conv_w, conv_b=conv_b, w_down=w_down, ln2_g=ln2_g, ln2_b=ln2_b)
    m = dict(c_ctx=m_c_ctx, w_mod=m_w_mod, b_mod=m_b_mod, w_in=m_w_in, b_in=m_b_in, attn_sink=m_attn_sink,
             q_norm_g=m_q_norm_g, k_norm_g=m_k_norm_g, w_branch_a=m_w_branch_a, w_branch_b=m_w_branch_b, w_out=m_w_out,
             ln1_g=m_ln1_g, ln1_b=m_ln1_b, w_up=m_w_up, conv_w=m_conv_w, conv_b=m_conv_b, w_down=m_w_down,
             ln2_g=m_ln2_g, ln2_b=m_ln2_b)
    v = dict(c_ctx=v_c_ctx, w_mod=v_w_mod, b_mod=v_b_mod, w_in=v_w_in, b_in=v_b_in, attn_sink=v_attn_sink,
             q_norm_g=v_q_norm_g, k_norm_g=v_k_norm_g, w_branch_a=v_w_branch_a, w_branch_b=v_w_branch_b, w_out=v_w_out,
             ln1_g=v_ln1_g, ln1_b=v_ln1_b, w_up=v_w_up, conv_w=v_conv_w, conv_b=v_conv_b, w_down=v_w_down,
             ln2_g=v_ln2_g, ln2_b=v_ln2_b)
    xp, yp, _ = _mesh_pos()
    me = 2 * xp + yp

    branches = jnp.concatenate([w_branch_a[0], w_branch_b[0]], axis=0)
    wide = jnp.concatenate([w_mod[0], w_in[0], w_up[0], branches], axis=1).astype(BF16)
    tall = jnp.concatenate([w_out[0], w_down[0]], axis=0).astype(BF16)
    wide4, tall4, cw4 = _gather_shards([wide, tall], conv_w[0])
    n_mod, n_in, n_up = w_mod.shape[-1], w_in.shape[-1], w_up.shape[-1]
    wmod4 = wide4[:, :, :n_mod]
    win4 = wide4[:, :, n_mod:n_mod + n_in]
    wup4 = wide4[:, :, n_mod + n_in:n_mod + n_in + n_up]
    br4 = wide4[:, :, n_mod + n_in + n_up:]
    n_br = w_branch_a.shape[1]
    wba = br4[:, :n_br].transpose(1, 0, 2).reshape(n_br, D_MODEL)
    wbb = br4[:, n_br:].transpose(1, 0, 2).reshape(n_br, D_MODEL)
    n_out = w_out.shape[1]
    w_out_full = tall4[:, :n_out].reshape(D_MODEL, D_MODEL)
    w_down_full = tall4[:, n_out:].reshape(D_FF, D_MODEL)
    cw_full = cw4.transpose(1, 0, 2).reshape(3, 2 * D_FF)

    loss, grad_x, g = _local_step(
        x[0], c, ctx[0], c_ctx[None], wmod4, b_mod, win4, b_in, attn_sink[0], q_norm_g, k_norm_g, wba, wbb, w_out_full,
        ln1_g, ln1_b, wup4, cw_full, conv_b, w_down_full, ln2_g, ln2_b, loss_target[0])
    loss = lax.psum(loss, ("x", "y", "c"))

    sent = dict(c=c, dmod=g["dmod"], dmodc=g["dmodc"], b_in=g["b_in"], conv_b=g["conv_b"], ln1_g=g["ln1_g"],
                ln1_b=g["ln1_b"], ln2_g=g["ln2_g"], ln2_b=g["ln2_b"], c_ctx=g["c_ctx"], attn_sink=g["sink"],
                q_norm_g=g["qn"], k_norm_g=g["kn"])
    every = _allgather_rows(jnp.concatenate([_group8(a) for a in sent.values()], axis=0))
    total = _sum_leading(every, name="sum_devices")
    slot = {n: slice(8 * i, 8 * i + 8) for i, n in enumerate(sent)}
    gs = {n: _ungroup8(total[slot[n]], sent[n].shape) for n in SMALL if n in sent}
    dmodc_sum = jnp.concatenate([_ungroup8(total[slot["dmodc"]], (1, 2 * D_MODEL)), jnp.zeros((1, 4 * D_MODEL), F32)],
                                axis=1)
    gs["b_mod"] = _ungroup8(total[slot["dmod"]], b_mod.shape) + dmodc_sum
    acts = jnp.concatenate([every[:, slot["c"].start], _rep8(c_ctx)], axis=0)
    dmods = jnp.concatenate([every[:, slot["dmod"]].reshape(N_DEV, -1)[:, :6 * D_MODEL], _first_row(dmodc_sum)], axis=0)
    g_w_mod = _silu_outer(acts, lax.dynamic_slice_in_dim(dmods, me * n_mod, n_mod, axis=1))

    cw_g4 = _to_blocks4(g["conv_w"])
    packed = jnp.concatenate([
        g["w_in4"].reshape(N_CHIPS, -1, D_MODEL), _to_blocks4(g["wba"]).reshape(N_CHIPS, -1, D_MODEL),
        _to_blocks4(g["wbb"]).reshape(N_CHIPS, -1, D_MODEL), g["w_out"].reshape(N_CHIPS, -1, D_MODEL),
        g["w_up4"].reshape(N_CHIPS, -1, D_MODEL), g["w_down"].reshape(N_CHIPS, -1, D_MODEL),
        jnp.pad(cw_g4.reshape(N_CHIPS, -1), ((0, 0), (0, 8 * D_MODEL - cw_g4.shape[1] * cw_g4.shape[2]))).reshape(
            N_CHIPS, 8, D_MODEL),
        jnp.zeros((N_CHIPS, BIG_ROWS - 3528, D_MODEL), F32)], axis=1)
    rh = BIG_ROWS // 2
    cpos = lax.axis_index("c")
    my_half = lax.dynamic_slice_in_dim(packed, cpos * rh, rh, axis=1)
    chip_sum = _add_blocks(my_half, _swap_other_half(packed), BF16)
    half_sum = _sum_leading(_scatter_to_chips(chip_sum), name="sum_chips")
    g_big = _unpack_big(_join_halves(half_sum).reshape(BIG_ROWS, D_MODEL), w)

    grads = dict(gs, w_mod=g_w_mod, **g_big)

    def pack_all(t):
        rows = jnp.concatenate([_pack_big(t), t["w_mod"].reshape(-1, D_MODEL), _pack_small(t)], axis=0)
        return jnp.pad(rows, ((0, -rows.shape[0] % 256), (0, 0)))

    delta_p, new_m_p, new_v_p = _adamw(pack_all(w), pack_all(grads), pack_all(m), pack_all(v))

    def unpack_all(p):
        r_mod = BIG_ROWS + w_mod.size // D_MODEL
        out = _unpack_big(p[:BIG_ROWS], w)
        out["w_mod"] = p[BIG_ROWS:r_mod].reshape(w_mod.shape)
        out.update(_unpack_small(p[r_mod:r_mod + SMALL_ROWS], w))
        return out

    grads = {n: grads[n].reshape(w[n].shape) for n in WEIGHTS}
    delta, new_m, new_v = unpack_all(delta_p), unpack_all(new_m_p), unpack_all(new_v_p)
    return (loss, grad_x[None], *[grads[n] for n in WEIGHTS], *[delta[n] for n in WEIGHTS],
            *[new_m[n] for n in WEIGHTS], *[new_v[n] for n in WEIGHTS])
```

```python
import functools
import math

import jax
import jax.numpy as jnp
from jax import lax
from jax.experimental import pallas as pl
from jax.experimental.pallas import tpu as pltpu

F32 = jnp.float32
BF16 = jnp.bfloat16

D_MODEL = 1024
HEAD_DIM = 64
N_HEADS = 8
N_KV = 2
WINDOW = 128
GRID_W = 64
ROPE_THETA = 10000.0
D_FF = 2816
LN_EPS = 1e-5
QK_EPS = 1e-6
ALPHA = 2.0 ** 0.25
Q_SCALE = HEAD_DIM ** -0.5
OFF_GA = 1536
IN_COLS = 3584
ADAM_LR, ADAM_B1, ADAM_B2, ADAM_EPS, ADAM_WD, ADAM_STEP = 0.001, 0.9, 0.999, 1e-8, 0.01, 10

LANES = 128
VMEM_BUDGET = 52 * 1024 * 1024
N_CHIPS = 4
N_DEV = 8
NEG = -1e30
MESH = pl.DeviceIdType.MESH


def _sigmoid(x):
    return 1.0 / (1.0 + jnp.exp(-x))


def _dot(a, b):
    return jnp.dot(a, b, preferred_element_type=F32)


def _dot_nt(a, b):
    return lax.dot_general(a, b, (((1,), (1,)), ((), ())), preferred_element_type=F32)


def _dot_tn(a, b):
    return lax.dot_general(a, b, (((0,), (0,)), ((), ())), preferred_element_type=F32)


def _call(body, *, name, grid, in_specs, out_specs, out_shape, scratch=(), sem=None, **kw):
    params = dict(vmem_limit_bytes=VMEM_BUDGET)
    if sem is not None:
        params["dimension_semantics"] = sem
    return pl.pallas_call(body, name=name, grid=grid, in_specs=in_specs, out_specs=out_specs,
                          out_shape=out_shape, scratch_shapes=list(scratch),
                          compiler_params=pltpu.CompilerParams(**params), **kw)


def _full(shape):
    n = len(shape)
    return pl.BlockSpec(shape, lambda *_: (0,) * n)


def _sds(shape, dtype=F32):
    return jax.ShapeDtypeStruct(shape, dtype)


def _mm_nn4(a, shift, scale, w4, bias, *, mode, split_out, out_dtype, tm, name):
    m, kdim = a.shape
    nb, _, ns = w4.shape

    def body(a_ref, sh_ref, sc_ref, w_ref, b_ref, o_ref):
        av = a_ref[...]
        if mode == "modulate":
            av = av * (1.0 + sc_ref[...]) + sh_ref[...]
        else:
            av = av * _sigmoid(av)
        o_ref[...] = (_dot(av.astype(BF16), w_ref[...]) + b_ref[...]).astype(out_dtype)

    if split_out:
        out_shape = _sds((2, m, 2 * ns), out_dtype)
        out_spec = pl.BlockSpec((None, tm, ns), lambda i, k: (k // 2, i, k % 2))
    else:
        out_shape = _sds((m, nb * ns), out_dtype)
        out_spec = pl.BlockSpec((tm, ns), lambda i, k: (i, k))
    return _call(
        body, name=name, grid=(m // tm, nb),
        in_specs=[pl.BlockSpec((tm, kdim), lambda i, k: (i, 0)),
                  pl.BlockSpec((1, kdim), lambda i, k: (0, 0)),
                  pl.BlockSpec((1, kdim), lambda i, k: (0, 0)),
                  pl.BlockSpec((None, kdim, ns), lambda i, k: (k, 0, 0)),
                  pl.BlockSpec((1, ns), lambda i, k: (0, k))],
        out_specs=out_spec, out_shape=out_shape, sem=("parallel", "arbitrary"),
    )(a, shift, scale, w4, bias)


def _mm_tn(a, b, *, a_spec, b_spec, grid, out_shape, out_spec, name, mod=None, init=None, colsum_spec=None,
           colsum_shape=None, a_is_t=False):
    red = len(grid) - 1
    has_mod, has_init, has_cs = mod is not None, init is not None, colsum_spec is not None

    def body(*refs):
        refs = list(refs)
        a_ref, b_ref = refs[0], refs[1]
        pos = 2
        if has_mod:
            sh_ref, sc_ref = refs[2], refs[3]
            pos = 4
        if has_init:
            init_ref = refs[pos]
            pos += 1
        o_ref = refs[pos]
        cs_ref = refs[pos + 1] if has_cs else None
        s = pl.program_id(red)

        @pl.when(s == 0)
        def _():
            o_ref[...] = init_ref[...] if has_init else jnp.zeros(o_ref.shape, F32)
            if has_cs:
                cs_ref[...] = jnp.zeros(cs_ref.shape, F32)

        av = a_ref[...]
        if has_mod:
            av = av * (1.0 + sc_ref[...]) + sh_ref[...]
        bv = b_ref[...]
        o_ref[...] += (_dot if a_is_t else _dot_tn)(av.astype(BF16), bv)
        if has_cs:
            cs_ref[...] += jnp.broadcast_to(jnp.sum(bv.astype(F32), axis=0, keepdims=True), cs_ref.shape)

    ins, in_specs = [a, b], [a_spec, b_spec]
    if has_mod:
        kdim = mod[0].shape[-1]
        ins += list(mod)
        in_specs += [_full((1, kdim)), _full((1, kdim))]
    if has_init:
        ins.append(init)
        in_specs.append(out_spec)
    out_specs, out_shapes = out_spec, out_shape
    if has_cs:
        out_specs, out_shapes = [out_spec, colsum_spec], [out_shape, colsum_shape]
    sem = ("parallel",) * red + ("arbitrary",)
    return _call(body, name=name, grid=grid, in_specs=in_specs, out_specs=out_specs, out_shape=out_shapes,
                 sem=sem)(*ins)


def _rope_tables(n_tok):
    pos = jnp.arange(n_tok, dtype=jnp.int32)
    rows = (pos // GRID_W).astype(F32)
    cols = (pos % GRID_W).astype(F32)
    n_freq = HEAD_DIM // 4
    inv_freq = ROPE_THETA ** (-jnp.arange(n_freq, dtype=F32) / n_freq)
    ang_r = rows[:, None] * inv_freq
    ang_c = cols[:, None] * inv_freq
    cos = jnp.concatenate([jnp.cos(ang_r)] * 2 + [jnp.cos(ang_c)] * 2, axis=-1)
    sin = jnp.concatenate([-jnp.sin(ang_r), jnp.sin(ang_r), -jnp.sin(ang_c), jnp.sin(ang_c)], axis=-1)
    return jnp.tile(cos, (1, 2)), jnp.tile(sin, (1, 2))


def _lane(shape):
    return lax.broadcasted_iota(jnp.int32, shape, 1)


def _rope_partner(t, lane):
    return jnp.where((lane % 32) < 16, pltpu.roll(t, LANES - 16, 1), pltpu.roll(t, 16, 1))


def _half_mean(s, lane):
    lo = jnp.sum(jnp.where(lane < HEAD_DIM, s, 0.0), axis=-1, keepdims=True)
    hi = jnp.sum(jnp.where(lane < HEAD_DIM, 0.0, s), axis=-1, keepdims=True)
    return jnp.where(lane < HEAD_DIM, lo, hi) * (1.0 / HEAD_DIM)


def _prep(proj, cos, sin, qg, kg, *, tm, name):
    m = proj.shape[0]

    def body(p_ref, cos_ref, sin_ref, qg_ref, kg_ref, qa_ref, ka_ref, va_ref, qb_ref, kb_ref, vb_ref):
        lane = _lane((tm, LANES))
        cosv, sinv = cos_ref[...], sin_ref[...]
        low = lane < HEAD_DIM

        def rope(t):
            return t * cosv + _rope_partner(t, lane) * sinv

        def rms(t, g):
            return t * lax.rsqrt(_half_mean(t * t, lane) + QK_EPS) * g

        def place(q_ref, j, chunk):
            sw = pltpu.roll(chunk, HEAD_DIM, 1)
            if j < 2:
                h0, h1 = jnp.where(low, chunk, 0.0), jnp.where(low, sw, 0.0)
            else:
                h0, h1 = jnp.where(low, 0.0, sw), jnp.where(low, 0.0, chunk)
            q_ref[2 * j] = h0.astype(BF16)
            q_ref[2 * j + 1] = h1.astype(BF16)

        for j in range(4):
            place(qa_ref, j, rope(p_ref[:, j * LANES:(j + 1) * LANES]) * Q_SCALE)
            place(qb_ref, j, rope(rms(p_ref[:, 768 + j * LANES:768 + (j + 1) * LANES], qg_ref[...])) * Q_SCALE)
        ka_ref[...] = rope(p_ref[:, 512:640]).astype(BF16)
        va_ref[...] = p_ref[:, 640:768].astype(BF16)
        kb_ref[...] = rope(rms(p_ref[:, 1280:1408], kg_ref[...])).astype(BF16)
        vb_ref[...] = p_ref[:, 1408:1536].astype(BF16)

    row = pl.BlockSpec((tm, LANES), lambda i: (i, 0))
    qspec = pl.BlockSpec((N_HEADS, tm, LANES), lambda i: (0, i, 0))
    return _call(
        body, name=name, grid=(m // tm,),
        in_specs=[pl.BlockSpec((tm, OFF_GA), lambda i: (i, 0)), row, row, _full((1, LANES)), _full((1, LANES))],
        out_specs=[qspec, row, row, qspec, row, row],
        out_shape=[_sds((N_HEADS, m, LANES), BF16), _sds((m, LANES), BF16), _sds((m, LANES), BF16),
                   _sds((N_HEADS, m, LANES), BF16), _sds((m, LANES), BF16), _sds((m, LANES), BF16)],
        sem=("parallel",),
    )(proj, cos, sin, qg, kg)


def _prep_bwd(dqa, dka, dva, dqb, dkb, dvb, proj, cos, sin, qg, kg, dgl, *, tm, name):
    m = proj.shape[0]

    def body(dqa_ref, dka_ref, dva_ref, dqb_ref, dkb_ref, dvb_ref, p_ref, cos_ref, sin_ref, qg_ref, kg_ref,
             dgl_ref, dp_ref, dqg_ref, dkg_ref):
        i = pl.program_id(0)
        lane = _lane((tm, LANES))
        cosv, sinv = cos_ref[...], sin_ref[...]
        low = lane < HEAD_DIM

        @pl.when(i == 0)
        def _():
            dqg_ref[...] = jnp.zeros(dqg_ref.shape, F32)
            dkg_ref[...] = jnp.zeros(dkg_ref.shape, F32)

        def unrope(d):
            return d * cosv - _rope_partner(d, lane) * sinv

        def unplace(dq_ref, j):
            d0, d1 = dq_ref[2 * j], dq_ref[2 * j + 1]
            if j < 2:
                return jnp.where(low, d0, pltpu.roll(d1, HEAD_DIM, 1))
            return jnp.where(low, pltpu.roll(d0, HEAD_DIM, 1), d1)

        def unrms(dtn, t, g):
            r = lax.rsqrt(_half_mean(t * t, lane) + QK_EPS)
            u = dtn * g
            dt = r * u - t * (r * r * r) * _half_mean(u * t, lane)
            return dt, jnp.sum(dtn * t * r, axis=0, keepdims=True)

        for j in range(4):
            dp_ref[:, j * LANES:(j + 1) * LANES] = (unrope(unplace(dqa_ref, j)) * Q_SCALE).astype(BF16)
            c0 = 768 + j * LANES
            dt, dg = unrms(unrope(unplace(dqb_ref, j)) * Q_SCALE, p_ref[:, c0:c0 + LANES], qg_ref[...])
            dp_ref[:, c0:c0 + LANES] = dt.astype(BF16)
            dqg_ref[:, j * LANES:(j + 1) * LANES] += dg
        dp_ref[:, 512:640] = unrope(dka_ref[...]).astype(BF16)
        dp_ref[:, 640:768] = dva_ref[...].astype(BF16)
        dt, dg = unrms(unrope(dkb_ref[...]), p_ref[:, 1280:1408], kg_ref[...])
        dp_ref[:, 1280:1408] = dt.astype(BF16)
        dkg_ref[...] += dg
        dp_ref[:, 1408:1536] = dvb_ref[...].astype(BF16)
        dp_ref[:, OFF_GA:] = dgl_ref[...]

    row = pl.BlockSpec((tm, LANES), lambda i: (i, 0))
    qspec = pl.BlockSpec((N_HEADS, tm, LANES), lambda i: (0, i, 0))
    return _call(
        body, name=name, grid=(m // tm,),
        in_specs=[qspec, row, row, qspec, row, row, pl.BlockSpec((tm, OFF_GA), lambda i: (i, 0)), row, row,
                  _full((1, LANES)), _full((1, LANES)), pl.BlockSpec((tm, IN_COLS - OFF_GA), lambda i: (i, 0))],
        out_specs=[pl.BlockSpec((tm, IN_COLS), lambda i: (i, 0)), _full((1, 512)), _full((1, LANES))],
        out_shape=[_sds((m, IN_COLS), BF16), _sds((1, 512)), _sds((1, LANES))],
        sem=("arbitrary",),
    )(dqa, dka, dva, dqb, dkb, dvb, proj, cos, sin, qg, kg, dgl)


def _attn_glob_fwd(qt, k, vt, kc, vct, *, tq, tk):
    nh, _, s = qt.shape
    nc = kc.shape[0]

    def body(qt_ref, k_ref, vt_ref, kc_ref, vct_ref, ot_ref, lse_ref, acc_sc, st_sc):
        qtv = qt_ref[...]
        acc_sc[...] = jnp.zeros(acc_sc.shape, F32)
        n_chunks = s // tk

        def update(st, vtv, m_old, l_old):
            m_new = jnp.maximum(m_old, jnp.max(st, axis=0, keepdims=True))
            pt = jnp.exp(st - m_new)
            al = jnp.exp(m_old - m_new)
            acc_sc[...] = acc_sc[...] * al + _dot(vtv, pt.astype(BF16))
            return m_new, l_old * al + jnp.sum(pt, axis=0, keepdims=True)

        def loop(c, carry):
            off = pl.multiple_of(c * tk, tk)
            nxt = pl.multiple_of(jnp.minimum(c + 1, n_chunks - 1) * tk, tk)
            st = st_sc[...]
            st_next = _dot(k_ref[pl.ds(nxt, tk), :], qtv)
            carry = update(st, vt_ref[:, pl.ds(off, tk)], *carry)
            st_sc[...] = st_next
            return carry

        init = (jnp.full((1, tq), NEG, F32), jnp.zeros((1, tq), F32))
        st_sc[...] = _dot(k_ref[pl.ds(0, tk), :], qtv)
        m, l = lax.fori_loop(0, n_chunks, loop, update(_dot(kc_ref[...], qtv), vct_ref[...], *init))
        ot_ref[...] = (acc_sc[...] / l).astype(BF16)
        lse_ref[...] = m + jnp.log(l)

    return _call(
        body, name="attn_glob_fwd", grid=(nh, s // tq),
        in_specs=[pl.BlockSpec((None, LANES, tq), lambda h, i: (h, 0, i)),
                  _full((s, LANES)), _full((LANES, s)), _full((nc, LANES)), _full((LANES, nc))],
        out_specs=[pl.BlockSpec((None, LANES, tq), lambda h, i: (h, 0, i)),
                   pl.BlockSpec((None, 1, tq), lambda h, i: (h, 0, i))],
        out_shape=[_sds((nh, LANES, s), BF16), _sds((nh, 1, s))],
        scratch=[pltpu.VMEM((LANES, tq), F32), pltpu.VMEM((tk, tq), F32)],
        sem=("parallel", "parallel"),
    )(qt, k, vt, kc, vct)


def _attn_glob_bwd(qt, dot, ot, lse, k, kt, v, kc, kct, vc, *, tq, tk):
    nh, _, s = qt.shape
    nc = kc.shape[0]
    n_q = s // tq

    def body(qt_ref, dot_ref, ot_ref, lse_ref, k_ref, kt_ref, v_ref, kc_ref, kct_ref, vc_ref,
             dqt_ref, dkt_ref, dvt_ref, dkct_ref, dvct_ref, acc_sc, st_sc, dp_sc, dkt_sc, dvt_sc):
        h, i = pl.program_id(0), pl.program_id(1)

        @pl.when(jnp.logical_and(h == 0, i == 0))
        def _():
            dkct_ref[...] = jnp.zeros(dkct_ref.shape, F32)
            dvct_ref[...] = jnp.zeros(dvct_ref.shape, F32)
            dkt_sc[...] = jnp.zeros(dkt_sc.shape, F32)
            dvt_sc[...] = jnp.zeros(dvt_sc.shape, F32)

        qtv, dotv, lse = qt_ref[...], dot_ref[...], lse_ref[...]
        delta = jnp.sum(dotv.astype(F32) * ot_ref[...].astype(F32), axis=0, keepdims=True)
        n_chunks = s // tk

        def grads(st, dpt):
            pt = jnp.exp(st - lse)
            return pt.astype(BF16), (pt * (dpt - delta)).astype(BF16)

        def loop(c, carry):
            off = pl.multiple_of(c * tk, tk)
            nxt = pl.multiple_of(jnp.minimum(c + 1, n_chunks - 1) * tk, tk)
            st, dpt = st_sc[...], dp_sc[...]
            st_next = _dot(k_ref[pl.ds(nxt, tk), :], qtv)
            dp_next = _dot(v_ref[pl.ds(nxt, tk), :], dotv)
            pb, dsb = grads(st, dpt)
            acc_sc[...] += _dot(kt_ref[:, pl.ds(off, tk)], dsb)
            dkt_sc[:, pl.ds(off, tk)] += _dot_nt(qtv, dsb)
            dvt_sc[:, pl.ds(off, tk)] += _dot_nt(dotv, pb)
            st_sc[...] = st_next
            dp_sc[...] = dp_next
            return carry

        st_sc[...] = _dot(k_ref[pl.ds(0, tk), :], qtv)
        dp_sc[...] = _dot(v_ref[pl.ds(0, tk), :], dotv)
        pb, dsb = grads(_dot(kc_ref[...], qtv), _dot(vc_ref[...], dotv))
        acc_sc[...] = _dot(kct_ref[...], dsb)
        dkct_ref[...] += _dot_nt(qtv, dsb)
        dvct_ref[...] += _dot_nt(dotv, pb)
        lax.fori_loop(0, n_chunks, loop, 0)
        dqt_ref[...] = acc_sc[...]

        @pl.when(jnp.logical_and(h == nh - 1, i == n_q - 1))
        def _():
            pltpu.sync_copy(dkt_sc, dkt_ref)
            pltpu.sync_copy(dvt_sc, dvt_ref)

    qs = pl.BlockSpec((None, LANES, tq), lambda h, i: (h, 0, i))
    rs = pl.BlockSpec((None, 1, tq), lambda h, i: (h, 0, i))
    return _call(
        body, name="attn_glob_bwd", grid=(nh, n_q),
        in_specs=[qs, qs, qs, rs, _full((s, LANES)), _full((LANES, s)), _full((s, LANES)), _full((nc, LANES)),
                  _full((LANES, nc)), _full((nc, LANES))],
        out_specs=[qs, ANY, ANY, _full((LANES, nc)), _full((LANES, nc))],
        out_shape=[_sds((nh, LANES, s)), _sds((LANES, s)), _sds((LANES, s)), _sds((LANES, nc)), _sds((LANES, nc))],
        scratch=[pltpu.VMEM((LANES, tq), F32), pltpu.VMEM((tk, tq), F32), pltpu.VMEM((tk, tq), F32),
                 pltpu.VMEM((LANES, s), F32), pltpu.VMEM((LANES, s), F32)],
        sem=("arbitrary", "arbitrary"),
    )(qt, dot, ot, lse, k, kt, v, kc, kct, vc)


def _attn_glob_dq(qt, dot, ot, lse, k, kt, v, kc, kct, vc, *, tq, tk):
    nh, _, s = qt.shape
    nc = kc.shape[0]

    def body(qt_ref, dot_ref, ot_ref, lse_ref, k_ref, kt_ref, v_ref, kc_ref, kct_ref, vc_ref,
             dqt_ref, dl_ref, dkct_ref, dvct_ref, acc_sc, st_sc, dp_sc):
        first = jnp.logical_and(pl.program_id(0) == 0, pl.program_id(1) == 0)

        @pl.when(first)
        def _():
            dkct_ref[...] = jnp.zeros(dkct_ref.shape, F32)
            dvct_ref[...] = jnp.zeros(dvct_ref.shape, F32)

        qtv, dotv, lse = qt_ref[...], dot_ref[...], lse_ref[...]
        delta = jnp.sum(dotv.astype(F32) * ot_ref[...].astype(F32), axis=0, keepdims=True)
        dl_ref[...] = delta

        n_chunks = s // tk

        def grads(st, dpt):
            pt = jnp.exp(st - lse)
            return pt.astype(BF16), (pt * (dpt - delta)).astype(BF16)

        def loop(c, carry):
            off = pl.multiple_of(c * tk, tk)
            nxt = pl.multiple_of(jnp.minimum(c + 1, n_chunks - 1) * tk, tk)
            st, dpt = st_sc[...], dp_sc[...]
            st_next = _dot(k_ref[pl.ds(nxt, tk), :], qtv)
            dp_next = _dot(v_ref[pl.ds(nxt, tk), :], dotv)
            _, dsb = grads(st, dpt)
            acc_sc[...] += _dot(kt_ref[:, pl.ds(off, tk)], dsb)
            st_sc[...] = st_next
            dp_sc[...] = dp_next
            return carry

        st_sc[...] = _dot(k_ref[pl.ds(0, tk), :], qtv)
        dp_sc[...] = _dot(v_ref[pl.ds(0, tk), :], dotv)
        pb, dsb = grads(_dot(kc_ref[...], qtv), _dot(vc_ref[...], dotv))
        acc_sc[...] = _dot(kct_ref[...], dsb)
        dkct_ref[...] += _dot_nt(qtv, dsb)
        dvct_ref[...] += _dot_nt(dotv, pb)
        lax.fori_loop(0, n_chunks, loop, 0)
        dqt_ref[...] = acc_sc[...]

    qs = pl.BlockSpec((None, LANES, tq), lambda h, i: (h, 0, i))
    rs = pl.BlockSpec((None, 1, tq), lambda h, i: (h, 0, i))
    return _call(
        body, name="attn_glob_dq", grid=(nh, s // tq),
        in_specs=[qs, qs, qs, rs, _full((s, LANES)), _full((LANES, s)), _full((s, LANES)), _full((nc, LANES)),
                  _full((LANES, nc)), _full((nc, LANES))],
        out_specs=[qs, rs, _full((LANES, nc)), _full((LANES, nc))],
        out_shape=[_sds((nh, LANES, s)), _sds((nh, 1, s)), _sds((LANES, nc)), _sds((LANES, nc))],
        scratch=[pltpu.VMEM((LANES, tq), F32), pltpu.VMEM((tk, tq), F32), pltpu.VMEM((tk, tq), F32)],
        sem=("arbitrary", "arbitrary"),
    )(qt, dot, ot, lse, k, kt, v, kc, kct, vc)


def _attn_glob_dkv(qt, dot, lse, dl, k, v, *, tq, tk):
    nh, _, s = qt.shape

    def body(k_ref, v_ref, qt_ref, dot_ref, lse_ref, dl_ref, dkt_ref, dvt_ref, st_sc, dp_sc):
        @pl.when(pl.program_id(1) == 0)
        def _():
            dkt_ref[...] = jnp.zeros(dkt_ref.shape, F32)
            dvt_ref[...] = jnp.zeros(dvt_ref.shape, F32)

        kv, vv = k_ref[...], v_ref[...]
        n_chunks = s // tq

        def loop(c, carry):
            off = pl.multiple_of(c * tq, tq)
            nxt = pl.multiple_of(jnp.minimum(c + 1, n_chunks - 1) * tq, tq)
            st, dpt = st_sc[...], dp_sc[...]
            st_next = _dot(kv, qt_ref[:, pl.ds(nxt, tq)])
            dp_next = _dot(vv, dot_ref[:, pl.ds(nxt, tq)])
            pt = jnp.exp(st - lse_ref[:, pl.ds(off, tq)])
            dst = pt * (dpt - dl_ref[:, pl.ds(off, tq)])
            dkt_ref[...] += _dot_nt(qt_ref[:, pl.ds(off, tq)], dst.astype(BF16))
            dvt_ref[...] += _dot_nt(dot_ref[:, pl.ds(off, tq)], pt.astype(BF16))
            st_sc[...] = st_next
            dp_sc[...] = dp_next
            return carry

        st_sc[...] = _dot(kv, qt_ref[:, pl.ds(0, tq)])
        dp_sc[...] = _dot(vv, dot_ref[:, pl.ds(0, tq)])
        lax.fori_loop(0, n_chunks, loop, 0)

    ks = pl.BlockSpec((tk, LANES), lambda j, h: (j, 0))
    ts = pl.BlockSpec((LANES, tk), lambda j, h: (0, j))
    qs = pl.BlockSpec((None, LANES, s), lambda j, h: (h, 0, 0))
    rs = pl.BlockSpec((None, 1, s), lambda j, h: (h, 0, 0))
    return _call(
        body, name="attn_glob_dkv", grid=(s // tk, nh),
        in_specs=[ks, ks, qs, qs, rs, rs], out_specs=[ts, ts],
        out_shape=[_sds((LANES, s)), _sds((LANES, s))],
        scratch=[pltpu.VMEM((tk, tq), F32), pltpu.VMEM((tk, tq), F32)],
        sem=("parallel", "arbitrary"),
    )(k, v, qt, dot, lse, dl)


WIN_SPAN = 2 * WINDOW


def _band(rows0, cols0, shape):
    r = rows0 + lax.broadcasted_iota(jnp.int32, shape, 0)
    c = cols0 + lax.broadcasted_iota(jnp.int32, shape, 1)
    return jnp.abs(r - c) <= WINDOW


def _win_start(blk, t, s):
    return pl.multiple_of(jnp.clip(blk * t - WINDOW, 0, s - t - WIN_SPAN), WINDOW)


def _attn_win_fwd(q, k, v, kc, vc, sink, *, tq):
    nh, s, _ = q.shape
    nc = kc.shape[0]
    tw = tq + WIN_SPAN

    def body(sink_ref, q_ref, k_ref, v_ref, kc_ref, vc_ref, o_ref, lse_ref):
        h, i = pl.program_id(0), pl.program_id(1)
        k0 = _win_start(i, tq, s)
        qv = q_ref[...]
        kv, vv = k_ref[pl.ds(k0, tw), :], v_ref[pl.ds(k0, tw), :]
        sc = jnp.where(_band(i * tq, k0, (tq, tw)), _dot_nt(qv, kv), NEG)
        scc = _dot_nt(qv, kc_ref[...])
        snk = sink_ref[h]
        m = jnp.maximum(jnp.maximum(jnp.max(sc, axis=-1, keepdims=True), jnp.max(scc, axis=-1, keepdims=True)), snk)
        p, pc = jnp.exp(sc - m), jnp.exp(scc - m)
        l = jnp.sum(p, axis=-1, keepdims=True) + jnp.sum(pc, axis=-1, keepdims=True) + jnp.exp(snk - m)
        acc = _dot(p.astype(BF16), vv) + _dot(pc.astype(BF16), vc_ref[...])
        o_ref[...] = (acc / l).astype(BF16)
        lse_ref[...] = m + jnp.log(l)

    return _call(
        body, name="attn_win_fwd", grid=(nh, s // tq),
        in_specs=[pl.BlockSpec(memory_space=pltpu.SMEM),
                  pl.BlockSpec((None, tq, LANES), lambda h, i: (h, i, 0)),
                  _full((s, LANES)), _full((s, LANES)), _full((nc, LANES)), _full((nc, LANES))],
        out_specs=[pl.BlockSpec((None, tq, LANES), lambda h, i: (h, i, 0)),
                   pl.BlockSpec((None, tq, 1), lambda h, i: (h, i, 0))],
        out_shape=[_sds((nh, s, LANES), BF16), _sds((nh, s, 1))],
        sem=("parallel", "parallel"),
    )(sink, q, k, v, kc, vc)


def _attn_win_dq(q, do, o, lse, k, v, kc, vc, sink, *, tq):
    nh, s, _ = q.shape
    nc = kc.shape[0]
    tw = tq + WIN_SPAN
    nq = s // tq

    def body(sink_ref, q_ref, do_ref, o_ref, lse_ref, k_ref, v_ref, kc_ref, vc_ref,
             dq_ref, dl_ref, dkc_ref, dvc_ref, dsk_ref):
        h, i = pl.program_id(0), pl.program_id(1)

        @pl.when(jnp.logical_and(h == 0, i == 0))
        def _():
            dkc_ref[...] = jnp.zeros(dkc_ref.shape, F32)
            dvc_ref[...] = jnp.zeros(dvc_ref.shape, F32)

        k0 = _win_start(i, tq, s)
        qv, dov, lse = q_ref[...], do_ref[...], lse_ref[...]
        kv, vv = k_ref[pl.ds(k0, tw), :], v_ref[pl.ds(k0, tw), :]
        kcv, vcv = kc_ref[...], vc_ref[...]
        delta = jnp.sum(dov.astype(F32) * o_ref[...].astype(F32), axis=-1, keepdims=True)
        dl_ref[...] = delta
        p = jnp.where(_band(i * tq, k0, (tq, tw)), jnp.exp(_dot_nt(qv, kv) - lse), 0.0)
        ds = (p * (_dot_nt(dov, vv) - delta)).astype(BF16)
        pc = jnp.exp(_dot_nt(qv, kcv) - lse)
        dsc = (pc * (_dot_nt(dov, vcv) - delta)).astype(BF16)
        dq_ref[...] = _dot(ds, kv) + _dot(dsc, kcv)
        dkc_ref[...] += _dot_tn(dsc, qv)
        dvc_ref[...] += _dot_tn(pc.astype(BF16), dov)
        dsk = -jnp.sum(jnp.exp(sink_ref[h] - lse) * delta)
        dsk_ref[...] = jnp.full(dsk_ref.shape, dsk, F32)

    qs = pl.BlockSpec((None, tq, LANES), lambda h, i: (h, i, 0))
    cs = pl.BlockSpec((None, tq, 1), lambda h, i: (h, i, 0))
    return _call(
        body, name="attn_win_dq", grid=(nh, nq),
        in_specs=[pl.BlockSpec(memory_space=pltpu.SMEM), qs, qs, qs, cs,
                  _full((s, LANES)), _full((s, LANES)), _full((nc, LANES)), _full((nc, LANES))],
        out_specs=[qs, cs, _full((nc, LANES)), _full((nc, LANES)),
                   pl.BlockSpec((None, None, 8, LANES), lambda h, i: (h, i, 0, 0))],
        out_shape=[_sds((nh, s, LANES)), _sds((nh, s, 1)), _sds((nc, LANES)), _sds((nc, LANES)),
                   _sds((nh, nq, 8, LANES))],
        sem=("arbitrary", "arbitrary"),
    )(sink, q, do, o, lse, k, v, kc, vc)


def _attn_win_dkv(q, do, lse_row, dl_row, k, v, *, tk):
    nh, s, _ = q.shape
    tw = tk + WIN_SPAN

    def body(k_ref, v_ref, q_ref, do_ref, lse_ref, dl_ref, dk_ref, dv_ref, dk_sc, dv_sc):
        h, j = pl.program_id(0), pl.program_id(1)
        q0 = _win_start(j, tk, s)
        kv, vv = k_ref[...], v_ref[...]
        qv, dov = q_ref[pl.ds(q0, tw), :], do_ref[pl.ds(q0, tw), :]
        pt = jnp.where(_band(j * tk, q0, (tk, tw)), jnp.exp(_dot_nt(kv, qv) - lse_ref[:, pl.ds(q0, tw)]), 0.0)
        dst = pt * (_dot_nt(vv, dov) - dl_ref[:, pl.ds(q0, tw)])
        dk, dv = _dot(dst.astype(BF16), qv), _dot(pt.astype(BF16), dov)
        rows = pl.ds(pl.multiple_of(j * tk, tk), tk)

        @pl.when(h == 0)
        def _():
            dk_sc[rows, :] = dk
            dv_sc[rows, :] = dv

        @pl.when(h > 0)
        def _():
            dk_sc[rows, :] += dk
            dv_sc[rows, :] += dv

        @pl.when(jnp.logical_and(h == nh - 1, j == s // tk - 1))
        def _():
            pltpu.sync_copy(dk_sc, dk_ref)
            pltpu.sync_copy(dv_sc, dv_ref)

    ks = pl.BlockSpec((tk, LANES), lambda h, j: (j, 0))
    qs = pl.BlockSpec((None, s, LANES), lambda h, j: (h, 0, 0))
    rs = pl.BlockSpec((None, 1, s), lambda h, j: (h, 0, 0))
    return _call(
        body, name="attn_win_dkv", grid=(nh, s // tk),
        in_specs=[ks, ks, qs, qs, rs, rs], out_specs=[ANY, ANY],
        out_shape=[_sds((s, LANES)), _sds((s, LANES))],
        scratch=[pltpu.VMEM((s, LANES), F32), pltpu.VMEM((s, LANES), F32)],
        sem=("arbitrary", "arbitrary"),
    )(k, v, q, do, lse_row, dl_row)


def _ln_fwd(z, g, b):
    mu = jnp.mean(z, axis=-1, keepdims=True)
    zc = z - mu
    r = lax.rsqrt(jnp.mean(zc * zc, axis=-1, keepdims=True) + LN_EPS)
    return zc * r * g + b, mu, r


def _ln_bwd(dy, xhat, r, g):
    dxh = dy * g
    return r * (dxh - jnp.mean(dxh, axis=-1, keepdims=True) - xhat * jnp.mean(dxh * xhat, axis=-1, keepdims=True))


def _heads_matmul(o_ref, w_ref):
    acc = _dot(o_ref[0], w_ref[0])
    for h in range(1, N_HEADS):
        acc += _dot(o_ref[h], w_ref[h])
    return acc


def _gate_specs(tm):
    return [pl.BlockSpec((tm, 512), functools.partial(lambda i, b: (i, b), b=OFF_GA // 512 + b)) for b in range(4)]


def _merge_fwd(oa, ob, proj, x, gate1, wba, wbb, w_out, ln_g, ln_b, *, tm):
    s = x.shape[0]

    def body(oa_ref, ob_ref, g0, g1, g2, g3, x_ref, gt_ref, wba_ref, wbb_ref, wo_ref, lg_ref, lb_ref,
             x1_ref, y_ref, mu_ref, r_ref):
        ga = _sigmoid(jnp.concatenate([g0[...], g1[...]], axis=1))
        gb = _sigmoid(jnp.concatenate([g2[...], g3[...]], axis=1))
        merged = ga * _heads_matmul(oa_ref, wba_ref) + gb * _heads_matmul(ob_ref, wbb_ref)
        y = _dot(merged.astype(BF16), wo_ref[...])
        x1, mu, r = _ln_fwd(ALPHA * x_ref[...] + gt_ref[...] * y, lg_ref[...], lb_ref[...])
        x1_ref[...] = x1
        y_ref[...] = y
        mu_ref[...] = mu
        r_ref[...] = r

    hs = pl.BlockSpec((N_HEADS, tm, LANES), lambda i: (0, i, 0))
    row = pl.BlockSpec((tm, D_MODEL), lambda i: (i, 0))
    col = pl.BlockSpec((tm, 1), lambda i: (i, 0))
    vec = _full((1, D_MODEL))
    wh = _full((N_HEADS, LANES, D_MODEL))
    return _call(
        body, name="merge_fwd", grid=(s // tm,),
        in_specs=[hs, hs, *_gate_specs(tm), row, vec, wh, wh, _full((D_MODEL, D_MODEL)), vec, vec],
        out_specs=[row, row, col, col],
        out_shape=[_sds((s, D_MODEL)), _sds((s, D_MODEL)), _sds((s, 1)), _sds((s, 1))],
        sem=("parallel",),
    )(oa, ob, proj, proj, proj, proj, x, gate1, wba, wbb, w_out, ln_g, ln_b)


def _merge_bwd(dy, oa, ob, oat, obt, proj, wba, wbb, w_out, *, tm):
    s = dy.shape[0]

    def body(dy_ref, oa_ref, ob_ref, oat_ref, obt_ref, g0, g1, g2, g3, wba_ref, wbb_ref, wo_ref,
             dgl_ref, doa_ref, dobt_ref, mg_ref, dwa_ref, dwb_ref):
        @pl.when(pl.program_id(0) == 0)
        def _():
            dwa_ref[...] = jnp.zeros(dwa_ref.shape, F32)
            dwb_ref[...] = jnp.zeros(dwb_ref.shape, F32)

        dm = _dot_nt(dy_ref[...], wo_ref[...])
        ga = _sigmoid(jnp.concatenate([g0[...], g1[...]], axis=1))
        gb = _sigmoid(jnp.concatenate([g2[...], g3[...]], axis=1))
        pa, pb = _heads_matmul(oa_ref, wba_ref), _heads_matmul(ob_ref, wbb_ref)
        mg_ref[...] = (ga * pa + gb * pb).astype(BF16)
        dgl_ref[:, :D_MODEL] = (dm * pa * ga * (1.0 - ga)).astype(BF16)
        dgl_ref[:, D_MODEL:] = (dm * pb * gb * (1.0 - gb)).astype(BF16)
        dpa, dpb = (dm * ga).astype(BF16), (dm * gb).astype(BF16)
        for h in range(N_HEADS):
            doa_ref[h] = _dot_nt(dpa, wba_ref[h]).astype(BF16)
            dobt_ref[h] = _dot_nt(wbb_ref[h], dpb).astype(BF16)
            dwa_ref[h] += _dot(oat_ref[h], dpa)
            dwb_ref[h] += _dot(obt_ref[h], dpb)

    hs = pl.BlockSpec((N_HEADS, tm, LANES), lambda i: (0, i, 0))
    hts = pl.BlockSpec((N_HEADS, LANES, tm), lambda i: (0, 0, i))
    row = pl.BlockSpec((tm, D_MODEL), lambda i: (i, 0))
    wh = _full((N_HEADS, LANES, D_MODEL))
    return _call(
        body, name="merge_bwd", grid=(s // tm,),
        in_specs=[row, hs, hs, hts, hts, *_gate_specs(tm), wh, wh, _full((D_MODEL, D_MODEL))],
        out_specs=[pl.BlockSpec((tm, 2 * D_MODEL), lambda i: (i, 0)), hs, hts, row, wh, wh],
        out_shape=[_sds((s, 2 * D_MODEL), BF16), _sds((N_HEADS, s, LANES), BF16), _sds((N_HEADS, LANES, s), BF16),
                   _sds((s, D_MODEL), BF16), _sds((N_HEADS, LANES, D_MODEL)), _sds((N_HEADS, LANES, D_MODEL))],
        sem=("arbitrary",),
    )(dy, oa, ob, oat, obt, proj, proj, proj, proj, wba, wbb, w_out)


FF_TC = 256


def _shift_rows(t, prev_row, next_row):
    n = t.shape[0]
    r = lax.broadcasted_iota(jnp.int32, t.shape, 0)
    up = jnp.where(r == 0, prev_row, pltpu.roll(t, 1, 0))
    dn = jnp.where(r == n - 1, next_row, pltpu.roll(t, n - 1, 0))
    return up, dn


def _halo_specs(tm, s, tc):
    nb8 = s // 8
    main = pl.BlockSpec((2, tm, tc), lambda j, i: (0, i, j))
    prev = pl.BlockSpec((2, 8, tc), lambda j, i: (0, jnp.maximum(i * (tm // 8) - 1, 0), j))
    nxt = pl.BlockSpec((2, 8, tc), lambda j, i: (0, jnp.minimum((i + 1) * (tm // 8), nb8 - 1), j))
    return main, prev, nxt


def _halo_rows(prev_ref, next_ref, half, i, n_i):
    prev_row = jnp.where(i == 0, 0.0, prev_ref[half, 7:8, :].astype(F32))
    next_row = jnp.where(i == n_i - 1, 0.0, next_ref[half, 0:1, :].astype(F32))
    return prev_row, next_row


def _conv(t, prev_row, next_row, w, b):
    up, dn = _shift_rows(t, prev_row, next_row)
    return w[0:1, :] * up + w[1:2, :] * t + w[2:3, :] * dn + b


def _ffn_act_fwd(u, cw, cb, *, tm):
    _, s, ff = u.shape
    n_i = s // tm

    def body(u_ref, up_ref, un_ref, cw_ref, cb_ref, a_ref):
        i = pl.program_id(1)
        gc = _conv(u_ref[0], *_halo_rows(up_ref, un_ref, 0, i, n_i), cw_ref[0], cb_ref[0])
        vc = _conv(u_ref[1], *_halo_rows(up_ref, un_ref, 1, i, n_i), cw_ref[1], cb_ref[1])
        a_ref[...] = (gc * _sigmoid(gc) * vc).astype(BF16)

    main, prev, nxt = _halo_specs(tm, s, FF_TC)
    return _call(
        body, name="ffn_act_fwd", grid=(ff // FF_TC, n_i),
        in_specs=[main, prev, nxt, pl.BlockSpec((2, 3, FF_TC), lambda j, i: (0, 0, j)),
                  pl.BlockSpec((2, 1, FF_TC), lambda j, i: (0, 0, j))],
        out_specs=pl.BlockSpec((tm, FF_TC), lambda j, i: (i, j)),
        out_shape=_sds((s, ff), BF16), sem=("parallel", "parallel"),
    )(u, u, u, cw, cb)


def _ffn_act_bwd(dy2, w_down, u, cw, cb, *, tm):
    _, s, ff = u.shape
    n_i = s // tm

    def body(dy_ref, wd_ref, u_ref, up_ref, un_ref, cw_ref, cb_ref, dc_ref, dcw_ref, dcb_ref):
        i = pl.program_id(1)

        @pl.when(i == 0)
        def _():
            dcw_ref[...] = jnp.zeros(dcw_ref.shape, F32)
            dcb_ref[...] = jnp.zeros(dcb_ref.shape, F32)

        da = _dot_nt(dy_ref[...], wd_ref[...])
        ug, uv = u_ref[0], u_ref[1]
        ugp, ugn = _shift_rows(ug, *_halo_rows(up_ref, un_ref, 0, i, n_i))
        uvp, uvn = _shift_rows(uv, *_halo_rows(up_ref, un_ref, 1, i, n_i))
        wg, wv = cw_ref[0], cw_ref[1]
        gc = wg[0:1, :] * ugp + wg[1:2, :] * ug + wg[2:3, :] * ugn + cb_ref[0]
        vc = wv[0:1, :] * uvp + wv[1:2, :] * uv + wv[2:3, :] * uvn + cb_ref[1]
        sg = _sigmoid(gc)
        dg = da * vc * sg * (1.0 + gc * (1.0 - sg))
        dv = da * gc * sg
        dc_ref[0] = dg
        dc_ref[1] = dv
        for half, (d, taps) in enumerate(((dg, (ugp, ug, ugn)), (dv, (uvp, uv, uvn)))):
            for tap in range(3):
                dcw_ref[half, tap:tap + 1, :] += jnp.sum(d * taps[tap], axis=0, keepdims=True)
            dcb_ref[half] += jnp.sum(d, axis=0, keepdims=True)

    main, prev, nxt = _halo_specs(tm, s, FF_TC)
    return _call(
        body, name="ffn_act_bwd", grid=(ff // FF_TC, n_i),
        in_specs=[pl.BlockSpec((tm, D_MODEL), lambda j, i: (i, 0)), pl.BlockSpec((FF_TC, D_MODEL), lambda j, i: (j, 0)),
                  main, prev, nxt, pl.BlockSpec((2, 3, FF_TC), lambda j, i: (0, 0, j)),
                  pl.BlockSpec((2, 1, FF_TC), lambda j, i: (0, 0, j))],
        out_specs=[main, pl.BlockSpec((2, 3, FF_TC), lambda j, i: (0, 0, j)),
                   pl.BlockSpec((2, 1, FF_TC), lambda j, i: (0, 0, j))],
        out_shape=[_sds((2, s, ff)), _sds((2, 3, ff)), _sds((2, 1, ff))],
        sem=("parallel", "arbitrary"),
    )(dy2, w_down, u, u, u, cw, cb)


def _conv_bwd_input(dc, cw, *, tm):
    _, s, ff = dc.shape
    n_i = s // tm

    def body(d_ref, dp_ref, dn_ref, cw_ref, du_ref):
        i = pl.program_id(1)
        for half in range(2):
            up, dn = _shift_rows(d_ref[half], *_halo_rows(dp_ref, dn_ref, half, i, n_i))
            w = cw_ref[half]
            du_ref[half] = (w[0:1, :] * dn + w[1:2, :] * d_ref[half] + w[2:3, :] * up).astype(BF16)

    main, prev, nxt = _halo_specs(tm, s, FF_TC)
    return _call(
        body, name="conv_bwd_input", grid=(ff // FF_TC, n_i),
        in_specs=[main, prev, nxt, pl.BlockSpec((2, 3, FF_TC), lambda j, i: (0, 0, j))],
        out_specs=main, out_shape=_sds((2, s, ff), BF16), sem=("parallel", "parallel"),
    )(dc, dc, dc, cw)


def _ffn_down_loss(a, w_down, x1, target, gate2, ln_g, ln_b, *, tm):
    s, ff = a.shape
    n_i = s // tm

    def body(a_ref, wd_ref, x1_ref, tg_ref, gt_ref, lg_ref, lb_ref, ls_ref, dy_ref, dx_ref, dg_ref, db_ref, dgt_ref):
        @pl.when(pl.program_id(0) == 0)
        def _():
            dg_ref[...] = jnp.zeros(dg_ref.shape, F32)
            db_ref[...] = jnp.zeros(db_ref.shape, F32)
            dgt_ref[...] = jnp.zeros(dgt_ref.shape, F32)

        y2 = _dot(a_ref[...], wd_ref[...])
        z = ALPHA * x1_ref[...] + gt_ref[...] * y2
        mu = jnp.mean(z, axis=-1, keepdims=True)
        zc = z - mu
        r = lax.rsqrt(jnp.mean(zc * zc, axis=-1, keepdims=True) + LN_EPS)
        xhat = zc * r
        diff = xhat * lg_ref[...] + lb_ref[...] - tg_ref[...]
        ls_ref[...] = jnp.full(ls_ref.shape, 0.5 / D_MODEL * jnp.sum(diff * diff), F32)
        dx2 = diff * (1.0 / D_MODEL)
        dg_ref[...] += jnp.sum(dx2 * xhat, axis=0, keepdims=True)
        db_ref[...] += jnp.sum(dx2, axis=0, keepdims=True)
        dz = _ln_bwd(dx2, xhat, r, lg_ref[...])
        dgt_ref[...] += jnp.sum(dz * y2, axis=0, keepdims=True)
        dy_ref[...] = (gt_ref[...] * dz).astype(BF16)
        dx_ref[...] = ALPHA * dz

    row = pl.BlockSpec((tm, D_MODEL), lambda i: (i, 0))
    vec = _full((1, D_MODEL))
    return _call(
        body, name="ffn_down_loss", grid=(n_i,),
        in_specs=[pl.BlockSpec((tm, ff), lambda i: (i, 0)), _full((ff, D_MODEL)), row, row, vec, vec, vec],
        out_specs=[pl.BlockSpec((None, 8, LANES), lambda i: (i, 0, 0)), row, row, vec, vec, vec],
        out_shape=[_sds((n_i, 8, LANES)), _sds((s, D_MODEL), BF16), _sds((s, D_MODEL)),
                   _sds((1, D_MODEL)), _sds((1, D_MODEL)), _sds((1, D_MODEL))],
        sem=("arbitrary",),
    )(a, w_down, x1, target, gate2, ln_g, ln_b)


def _ffn_up_bwd(du, wup4, dx1a, x1, scale2, x, y, mu1, r1, gate1, ln_g, *, tm):
    s = x.shape[0]
    nb, _, ns = wup4.shape

    def body(du_ref, w_ref, dxa_ref, x1_ref, sc_ref, x_ref, y_ref, mu_ref, r_ref, gt_ref, lg_ref,
             dxo_ref, dy_ref, dsc_ref, dsh_ref, dg_ref, db_ref, dgt_ref, acc):
        i, k = pl.program_id(0), pl.program_id(1)

        @pl.when(jnp.logical_and(i == 0, k == 0))
        def _():
            for ref in (dsc_ref, dsh_ref, dg_ref, db_ref, dgt_ref):
                ref[...] = jnp.zeros(ref.shape, F32)

        @pl.when(k == 0)
        def _():
            acc[...] = jnp.zeros(acc.shape, F32)

        acc[...] += _dot_nt(du_ref[...], w_ref[...])

        @pl.when(k == nb - 1)
        def _():
            dh = acc[...]
            x1 = x1_ref[...]
            dsc_ref[...] += jnp.sum(dh * x1, axis=0, keepdims=True)
            dsh_ref[...] += jnp.sum(dh, axis=0, keepdims=True)
            dx1 = dxa_ref[...] + dh * (1.0 + sc_ref[...])
            yv = y_ref[...]
            xhat = (ALPHA * x_ref[...] + gt_ref[...] * yv - mu_ref[...]) * r_ref[...]
            dg_ref[...] += jnp.sum(dx1 * xhat, axis=0, keepdims=True)
            db_ref[...] += jnp.sum(dx1, axis=0, keepdims=True)
            dz = _ln_bwd(dx1, xhat, r_ref[...], lg_ref[...])
            dgt_ref[...] += jnp.sum(dz * yv, axis=0, keepdims=True)
            dy_ref[...] = (gt_ref[...] * dz).astype(BF16)
            dxo_ref[...] = ALPHA * dz

    row = pl.BlockSpec((tm, D_MODEL), lambda i, k: (i, 0))
    col = pl.BlockSpec((tm, 1), lambda i, k: (i, 0))
    vec = _full((1, D_MODEL))
    return _call(
        body, name="ffn_up_bwd", grid=(s // tm, nb),
        in_specs=[pl.BlockSpec((None, tm, ns), lambda i, k: (k // 2, i, k % 2)),
                  pl.BlockSpec((None, D_MODEL, ns), lambda i, k: (k, 0, 0)),
                  row, row, vec, row, row, col, col, vec, vec],
        out_specs=[row, row, vec, vec, vec, vec, vec],
        out_shape=[_sds((s, D_MODEL)), _sds((s, D_MODEL), BF16)] + [_sds((1, D_MODEL))] * 5,
        scratch=[pltpu.VMEM((tm, D_MODEL), F32)],
        sem=("arbitrary", "arbitrary"),
    )(du, wup4, dx1a, x1, scale2, x, y, mu1, r1, gate1, ln_g)


def _mm_nt4_mod_bwd(dp, w4, dxa, x, scale, *, tm, name):
    m = x.shape[0]
    nb, kdim, ns = w4.shape

    def body(dp_ref, w_ref, dxa_ref, x_ref, sc_ref, dx_ref, dsc_ref, dsh_ref, acc):
        i, k = pl.program_id(0), pl.program_id(1)

        @pl.when(jnp.logical_and(i == 0, k == 0))
        def _():
            dsc_ref[...] = jnp.zeros(dsc_ref.shape, F32)
            dsh_ref[...] = jnp.zeros(dsh_ref.shape, F32)

        @pl.when(k == 0)
        def _():
            acc[...] = jnp.zeros(acc.shape, F32)

        acc[...] += _dot_nt(dp_ref[...], w_ref[...])

        @pl.when(k == nb - 1)
        def _():
            dh = acc[...]
            dsc_ref[...] += jnp.sum(dh * x_ref[...], axis=0, keepdims=True)
            dsh_ref[...] += jnp.sum(dh, axis=0, keepdims=True)
            dx_ref[...] = dxa_ref[...] + dh * (1.0 + sc_ref[...])

    row = pl.BlockSpec((tm, kdim), lambda i, k: (i, 0))
    vec = _full((1, kdim))
    return _call(
        body, name=name, grid=(m // tm, nb),
        in_specs=[pl.BlockSpec((tm, ns), lambda i, k: (i, k)), pl.BlockSpec((None, kdim, ns), lambda i, k: (k, 0, 0)),
                  row, row, vec],
        out_specs=[row, vec, vec],
        out_shape=[_sds((m, kdim)), _sds((1, kdim)), _sds((1, kdim))],
        scratch=[pltpu.VMEM((tm, kdim), F32)],
        sem=("arbitrary", "arbitrary"),
    )(dp, w4, dxa, x, scale)


def _pad_heads_w(w):
    w8 = w.reshape(N_HEADS, HEAD_DIM, w.shape[-1])
    z = jnp.zeros_like(w8)
    first = (jnp.arange(N_HEADS) < N_HEADS // N_KV)[:, None, None]
    return jnp.where(first, jnp.concatenate([w8, z], axis=1), jnp.concatenate([z, w8], axis=1))


def _unpad_heads_w(g):
    first = (jnp.arange(N_HEADS) < N_HEADS // N_KV)[:, None, None]
    return jnp.where(first, g[:, :HEAD_DIM], g[:, HEAD_DIM:]).reshape(N_HEADS * HEAD_DIM, g.shape[-1])


def _rep8(a):
    return jnp.broadcast_to(a.reshape(1, -1), (8, a.size))


def _first_row(a):
    r8 = _rep8(a)
    return jnp.where(lax.broadcasted_iota(jnp.int32, r8.shape, 0) == 0, r8, 0.0)


def _to_blocks4(w):
    k, n = w.shape
    return w.reshape(k, N_CHIPS, n // N_CHIPS).transpose(1, 0, 2)


def _local_step(x, c, ctx, c_ctx, wmod4, b_mod, win4, b_in, sink, qn, kn, wba, wbb, w_out, ln1_g, ln1_b,
                wup4, cw, cb, w_down, ln2_g, ln2_b, target):
    s, nc = x.shape[0], ctx.shape[0]
    tm = min(512, s)
    tm2 = min(256, s)
    zvec = jnp.zeros((1, D_MODEL), F32)

    cc = jnp.concatenate([_rep8(c), _rep8(c_ctx)], axis=0)
    mods = _mm_nn4(cc, zvec, zvec, wmod4, b_mod, mode="silu", split_out=False, out_dtype=F32, tm=16, name="mod_vectors")
    shift1, scale1, gate1, shift2, scale2, gate2 = [mods[0:1, i * D_MODEL:(i + 1) * D_MODEL] for i in range(6)]
    shift_c, scale_c = mods[8:9, :D_MODEL], mods[8:9, D_MODEL:2 * D_MODEL]

    cos, sin = _rope_tables(s)
    cos_c, sin_c = jnp.ones((nc, LANES), F32), jnp.zeros((nc, LANES), F32)
    qg, kg = jnp.tile(qn, (1, 2)), jnp.tile(kn, (1, 2))

    proj_c = _mm_nn4(ctx, shift_c, scale_c, win4, b_in, mode="modulate", split_out=False, out_dtype=F32, tm=nc,
                     name="in_proj_ctx")
    _, kac, vac, _, kbc, vbc = _prep(proj_c, cos_c, sin_c, qg, kg, tm=nc, name="prep_ctx")
    proj = _mm_nn4(x, shift1, scale1, win4, b_in, mode="modulate", split_out=False, out_dtype=F32, tm=tm, name="in_proj")
    qa, ka, va, qb, kb, vb = _prep(proj, cos, sin, qg, kg, tm=tm, name="prep")
    oa, lse_a = _attn_win_fwd(qa, ka, va, kac, vac, sink, tq=tm)
    qbt = jnp.swapaxes(qb, 1, 2)
    obt, lse_b = _attn_glob_fwd(qbt, kb, vb.T, kbc, vbc.T, tq=tm, tk=min(1024, s))
    ob = jnp.swapaxes(obt, 1, 2)
    wba_p, wbb_p = _pad_heads_w(wba), _pad_heads_w(wbb)
    x1, y, mu1, r1 = _merge_fwd(oa, ob, proj, x, gate1, wba_p, wbb_p, w_out, ln1_g, ln1_b, tm=tm2)
    u = _mm_nn4(x1, shift2, scale2, wup4, jnp.zeros((1, 2 * D_FF), F32), mode="modulate", split_out=True,
                out_dtype=F32, tm=tm, name="ffn_up")
    cw2 = cw.reshape(3, 2, D_FF).transpose(1, 0, 2)
    cb2 = cb.reshape(2, 1, D_FF)
    a = _ffn_act_fwd(u, cw2, cb2, tm=tm)
    ls, dy2, dx1a, dln2_g, dln2_b, dgate2 = _ffn_down_loss(a, w_down, x1, target, gate2, ln2_g, ln2_b, tm=tm2)
    loss = jnp.sum(ls[:, 0, 0])

    n_s = s // tm
    dw_down = _mm_tn(a, dy2, a_spec=pl.BlockSpec((tm, D_FF), lambda t: (t, 0)),
                     b_spec=pl.BlockSpec((tm, D_MODEL), lambda t: (t, 0)), grid=(n_s,),
                     out_shape=_sds((D_FF, D_MODEL)), out_spec=_full((D_FF, D_MODEL)), name="dw_down")
    dc, dcw2, dcb2 = _ffn_act_bwd(dy2, w_down, u, cw2, cb2, tm=tm)
    du = _conv_bwd_input(dc, cw2, tm=tm)
    dxz1, dy, dscale2, dshift2, dln1_g, dln1_b, dgate1 = _ffn_up_bwd(
        du, wup4, dx1a, x1, scale2, x, y, mu1, r1, gate1, ln1_g, tm=tm2)
    ns_up = wup4.shape[-1]
    dw_up4 = _mm_tn(x1, du, a_spec=pl.BlockSpec((tm, D_MODEL), lambda k, t: (t, 0)),
                    b_spec=pl.BlockSpec((None, tm, ns_up), lambda k, t: (k // 2, t, k % 2)), grid=(N_CHIPS, n_s),
                    out_shape=_sds((N_CHIPS, D_MODEL, ns_up)),
                    out_spec=pl.BlockSpec((None, D_MODEL, ns_up), lambda k, t: (k, 0, 0)),
                    mod=(shift2, scale2), name="dw_up")

    dgl, doa, dobt, merged, dwba_p, dwbb_p = _merge_bwd(dy, oa, ob, jnp.swapaxes(oa, 1, 2), obt, proj, wba_p, wbb_p,
                                                        w_out, tm=tm2)
    dwba, dwbb = _unpad_heads_w(dwba_p), _unpad_heads_w(dwbb_p)
    rowspec = pl.BlockSpec((tm, D_MODEL), lambda t: (t, 0))
    dw_out = _mm_tn(merged, dy, a_spec=rowspec, b_spec=rowspec, grid=(n_s,), out_shape=_sds((D_MODEL, D_MODEL)),
                    out_spec=_full((D_MODEL, D_MODEL)), name="dw_out")

    dqa, dla, dkac, dvac, dsk = _attn_win_dq(qa, doa, oa, lse_a, ka, va, kac, vac, sink, tq=tm)
    dka, dva = _attn_win_dkv(qa, doa, lse_a.reshape(N_HEADS, 1, s), dla.reshape(N_HEADS, 1, s), ka, va, tk=tm)
    dqbt, dkbt, dvbt, dkbct, dvbct = _attn_glob_bwd(qbt, dobt, obt, lse_b, kb, kb.T, vb, kbc, kbc.T, vbc, tq=tm, tk=tm)
    dqb, dkb, dvb, dkbc, dvbc = jnp.swapaxes(dqbt, 1, 2), dkbt.T, dvbt.T, dkbct.T, dvbct.T
    dsink = jnp.sum(dsk[:, :, 0, 0], axis=1)

    dproj, dqg, dkg = _prep_bwd(dqa, dka, dva, dqb, dkb, dvb, proj, cos, sin, qg, kg, dgl, tm=tm, name="prep_bwd")
    grad_x, dscale1, dshift1 = _mm_nt4_mod_bwd(dproj, win4, dxz1, x, scale1, tm=tm, name="in_proj_bwd")
    ns_in = win4.shape[-1]
    win_spec = dict(b_spec=pl.BlockSpec((None, None, ns_in), lambda k, t: (0, 0, k)),
                    out_shape=_sds((N_CHIPS, D_MODEL, ns_in)),
                    out_spec=pl.BlockSpec((None, D_MODEL, ns_in), lambda k, t: (k, 0, 0)),
                    colsum_spec=pl.BlockSpec((8, ns_in), lambda k, t: (0, k)), colsum_shape=_sds((8, IN_COLS)))
    win_spec["b_spec"] = pl.BlockSpec((tm, ns_in), lambda k, t: (t, k))
    dw_in4, db_in = _mm_tn(x, dproj, a_spec=pl.BlockSpec((tm, D_MODEL), lambda k, t: (t, 0)), grid=(N_CHIPS, n_s),
                           mod=(shift1, scale1), name="dw_in", **win_spec)

    zq = jnp.zeros((N_HEADS, nc, LANES), F32)
    dproj_c, _, dkg_c = _prep_bwd(zq, dkac, dvac, zq, dkbc, dvbc, proj_c, cos_c, sin_c, qg, kg,
                                  jnp.zeros((nc, IN_COLS - OFF_GA), BF16), tm=nc, name="prep_bwd_ctx")
    _, dscale_c, dshift_c = _mm_nt4_mod_bwd(dproj_c, win4, jnp.zeros((nc, D_MODEL), F32), ctx, scale_c, tm=nc,
                                            name="in_proj_bwd_ctx")
    win_spec["b_spec"] = pl.BlockSpec((nc, ns_in), lambda k, t: (t, k))
    dw_in4, db_in_c = _mm_tn(ctx, dproj_c, a_spec=pl.BlockSpec((nc, D_MODEL), lambda k, t: (t, 0)), grid=(N_CHIPS, 1),
                             mod=(shift_c, scale_c), init=dw_in4, name="dw_in_ctx", **win_spec)

    dmod = jnp.concatenate([dshift1, dscale1, dgate1, dshift2, dscale2, dgate2], axis=1)
    dmodc = jnp.concatenate([dshift_c, dscale_c], axis=1)
    dmodc_pad = jnp.concatenate([dmodc, jnp.zeros((1, 4 * D_MODEL), F32)], axis=1)
    dmodc8 = _first_row(dmodc_pad).astype(BF16)
    z8 = jnp.zeros((8, D_MODEL), F32)
    dsilu_c, _, _ = _mm_nt4_mod_bwd(dmodc8, wmod4, z8, z8, zvec, tm=8, name="c_ctx_bwd")
    sg = _sigmoid(c_ctx)
    dc_ctx = dsilu_c[0:1] * sg * (1.0 + c_ctx * (1.0 - sg))

    dqn = jnp.sum(dqg.reshape(N_HEADS, HEAD_DIM), axis=0, keepdims=True)
    dkn = jnp.sum((dkg + dkg_c).reshape(N_KV, HEAD_DIM), axis=0, keepdims=True)
    grads = dict(
        w_in4=dw_in4, b_in=db_in[0:1] + db_in_c[0:1], sink=dsink, qn=dqn, kn=dkn, wba=dwba, wbb=dwbb, w_out=dw_out,
        ln1_g=dln1_g, ln1_b=dln1_b, w_up4=dw_up4, conv_w=dcw2.transpose(1, 0, 2).reshape(3, 2 * D_FF),
        conv_b=dcb2.reshape(1, 2 * D_FF), w_down=dw_down, ln2_g=dln2_g, ln2_b=dln2_b,
        c_ctx=dc_ctx, dmod=dmod, dmodc=dmodc)
    return loss, grad_x, grads


ANY = pl.BlockSpec(memory_space=pl.ANY)


def _mesh_pos():
    return lax.axis_index("x"), lax.axis_index("y"), lax.axis_index("c")


def _other_chips(x, y):
    return [(1 - x, y), (x, 1 - y), (1 - x, 1 - y)]


def _remote(src, dst, send, recv, dev):
    return pltpu.make_async_remote_copy(src_ref=src, dst_ref=dst, send_sem=send, recv_sem=recv, device_id=dev,
                                        device_id_type=MESH)


def _set_block(stack, block, k):
    return lax.dynamic_update_slice(stack, block[None], (k,) + (0,) * block.ndim)


def _gather_shards(arrs, small):
    na = len(arrs)
    halves = [a.shape[0] // 2 for a in arrs]

    def body(*refs):
        ins, small_ref = refs[:na], refs[na]
        outs, small_out = refs[na + 1:2 * na + 1], refs[2 * na + 1]
        send, recv = refs[2 * na + 2:]
        x, y, c = _mesh_pos()
        me = 2 * x + y
        chips = _other_chips(x, y)

        def half(a, cc):
            return pl.ds(cc * halves[a], halves[a])

        sends = []
        for j, chip in enumerate(chips):
            for a in range(na):
                sends.append(_remote(ins[a].at[half(a, c)], outs[a].at[me, half(a, c)], send.at[a, j], recv.at[a, j],
                                     (*chip, c)))
            sends.append(_remote(small_ref, small_out.at[me], send.at[na, j], recv.at[na, j], (*chip, c)))
        for cp in sends:
            cp.start()
        for j, chip in enumerate(chips):
            kj = 2 * chip[0] + chip[1]
            for a in range(na):
                landed = outs[a].at[kj, half(a, c)]
                _remote(landed, landed, send.at[a, j], recv.at[a, j], (*chip, c)).wait_recv()
                fwd = _remote(landed, landed, send.at[a, 3 + j], recv.at[a, 3 + j], (x, y, 1 - c))
                fwd.start()
                sends.append(fwd)
            _remote(small_ref, small_out.at[kj], send.at[na, j], recv.at[na, j], (*chip, c)).wait_recv()
        for j, chip in enumerate(chips):
            kj = 2 * chip[0] + chip[1]
            for a in range(na):
                other = outs[a].at[kj, half(a, 1 - c)]
                _remote(other, other, send.at[a, 3 + j], recv.at[a, 3 + j], (x, y, 1 - c)).wait_recv()
        for cp in sends:
            cp.wait_send()

    out_shape = [_sds((N_CHIPS,) + a.shape, a.dtype) for a in arrs] + [_sds((N_CHIPS,) + small.shape, small.dtype)]
    got = pl.pallas_call(
        body, name="gather_shards", in_specs=[ANY] * (na + 1), out_specs=[ANY] * (na + 1), out_shape=out_shape,
        scratch_shapes=[pltpu.SemaphoreType.DMA((na + 1, 6)), pltpu.SemaphoreType.DMA((na + 1, 6))],
    )(*arrs, small)
    xp, yp, _ = _mesh_pos()
    return [_set_block(g, a, 2 * xp + yp) for g, a in zip(got, list(arrs) + [small])]


def _allgather_rows(v):
    r, n = v.shape

    def body(v_ref, out_ref, send, recv, loc):
        x, y, c = _mesh_pos()
        me, sibling = (x, y, c), (x, y, 1 - c)
        chips = _other_chips(x, y)

        def rows(px, py, pc):
            return out_ref.at[4 * px + 2 * py + pc]

        def copy(k, block, to, src=None):
            return _remote(rows(*block) if src is None else src, rows(*block), send.at[k], recv.at[k], to)

        mine = pltpu.make_async_copy(v_ref, rows(*me), loc)
        mine.start()
        first = [copy(0, me, sibling, src=v_ref)] + [copy(1 + j, me, (*chip, c), src=v_ref) for j, chip in enumerate(chips)]
        for cp in first:
            cp.start()
        passed = [copy(4 + j, (*chip, c), sibling) for j, chip in enumerate(chips)]
        for j, chip in enumerate(chips):
            copy(1 + j, (*chip, c), me).wait_recv()
            passed[j].start()
        copy(0, sibling, me).wait_recv()
        for j, chip in enumerate(chips):
            copy(4 + j, (*chip, 1 - c), me).wait_recv()
        for cp in first + passed:
            cp.wait_send()
        mine.wait()

    return pl.pallas_call(
        body, name="allgather_rows", in_specs=[pl.BlockSpec(memory_space=pltpu.VMEM)],
        out_specs=pl.BlockSpec(memory_space=pltpu.VMEM), out_shape=_sds((N_DEV, r, n), v.dtype),
        scratch_shapes=[pltpu.SemaphoreType.DMA((7,)), pltpu.SemaphoreType.DMA((7,)), pltpu.SemaphoreType.DMA],
    )(v)


def _swap_other_half(g):
    nb, r, n = g.shape
    rh = r // 2

    def body(g_ref, out_ref, send, recv):
        x, y, c = _mesh_pos()
        cp = _remote(g_ref.at[:, pl.ds((1 - c) * rh, rh), :], out_ref, send, recv, (x, y, 1 - c))
        cp.start()
        cp.wait()

    return pl.pallas_call(
        body, name="swap_other_half", in_specs=[ANY], out_specs=ANY, out_shape=_sds((nb, rh, n), g.dtype),
        scratch_shapes=[pltpu.SemaphoreType.DMA, pltpu.SemaphoreType.DMA],
    )(g)


def _scatter_to_chips(p):
    def body(p_ref, out_ref, send, recv):
        x, y, c = _mesh_pos()
        me = 2 * x + y
        chips = _other_chips(x, y)
        sends = [_remote(p_ref.at[2 * chip[0] + chip[1]], out_ref.at[me], send.at[j], recv.at[j], (*chip, c))
                 for j, chip in enumerate(chips)]
        for cp in sends:
            cp.start()
        for j, chip in enumerate(chips):
            kj = 2 * chip[0] + chip[1]
            _remote(p_ref.at[kj], out_ref.at[kj], send.at[j], recv.at[j], (*chip, c)).wait_recv()
        for cp in sends:
            cp.wait_send()

    got = pl.pallas_call(
        body, name="scatter_to_chips", in_specs=[ANY], out_specs=ANY, out_shape=_sds(p.shape, p.dtype),
        scratch_shapes=[pltpu.SemaphoreType.DMA((3,)), pltpu.SemaphoreType.DMA((3,))],
    )(p)
    xp, yp, _ = _mesh_pos()
    me = 2 * xp + yp
    return _set_block(got, lax.dynamic_index_in_dim(p, me, axis=0, keepdims=False), me)


def _join_halves(f):
    def body(f_ref, out_ref, send, recv):
        x, y, c = _mesh_pos()
        cp = _remote(f_ref, out_ref, send, recv, (x, y, 1 - c))
        cp.start()
        cp.wait()

    other = pl.pallas_call(
        body, name="join_halves", in_specs=[ANY], out_specs=ANY, out_shape=_sds(f.shape, f.dtype),
        scratch_shapes=[pltpu.SemaphoreType.DMA, pltpu.SemaphoreType.DMA],
    )(f)
    first = lax.axis_index("c") == 0
    return jnp.concatenate([jnp.where(first, f, other), jnp.where(first, other, f)], axis=0)


def _row_tile(rows, cap=512):
    t = cap - cap % 8
    while rows % t:
        t -= 8
    return t


def _add_blocks(a, b, out_dtype):
    nb, r, n = a.shape
    tr = _row_tile(r)

    def body(a_ref, b_ref, o_ref):
        o_ref[...] = (a_ref[...] + b_ref[...]).astype(out_dtype)

    spec = pl.BlockSpec((None, tr, n), lambda k, i: (k, i, 0))
    return _call(body, name="add_blocks", grid=(nb, r // tr), in_specs=[spec, spec], out_specs=spec,
                 out_shape=_sds(a.shape, out_dtype), sem=("parallel", "parallel"))(a, b)


def _sum_leading(a, *, name):
    nk, r, n = a.shape
    tr = _row_tile(r)

    def body(a_ref, o_ref):
        acc = a_ref[0].astype(F32)
        for k in range(1, nk):
            acc = acc + a_ref[k].astype(F32)
        o_ref[...] = acc

    return _call(body, name=name, grid=(r // tr,), in_specs=[pl.BlockSpec((nk, tr, n), lambda i: (0, i, 0))],
                 out_specs=pl.BlockSpec((tr, n), lambda i: (i, 0)), out_shape=_sds((r, n)), sem=("parallel",))(a)


def _silu_outer(a, b):
    kdim, n = a.shape[1], b.shape[1]

    def body(a_ref, b_ref, o_ref):
        av = a_ref[...]
        av = av * _sigmoid(av)
        bv = b_ref[...]
        ah, bh = av.astype(BF16), bv.astype(BF16)
        al, bl = (av - ah.astype(F32)).astype(BF16), (bv - bh.astype(F32)).astype(BF16)
        o_ref[...] = _dot_tn(ah, bh) + (_dot_tn(ah, bl) + _dot_tn(al, bh))

    return _call(body, name="dw_mod", grid=(1,), in_specs=[_full(a.shape), _full(b.shape)], out_specs=_full((kdim, n)),
                 out_shape=_sds((kdim, n)))(a, b)


def _adamw(w, g, m, v):
    r, n = w.shape
    tr = _row_tile(r)

    def body(w_ref, g_ref, m_ref, v_ref, d_ref, nm_ref, nv_ref):
        gv = g_ref[...]
        nm = ADAM_B1 * m_ref[...] + (1.0 - ADAM_B1) * gv
        nv = ADAM_B2 * v_ref[...] + (1.0 - ADAM_B2) * (gv * gv)
        m_hat = nm / (1.0 - ADAM_B1 ** ADAM_STEP)
        v_hat = nv / (1.0 - ADAM_B2 ** ADAM_STEP)
        d_ref[...] = -ADAM_LR * (m_hat / (jnp.sqrt(v_hat) + ADAM_EPS) + ADAM_WD * w_ref[...])
        nm_ref[...] = nm
        nv_ref[...] = nv

    spec = pl.BlockSpec((tr, n), lambda i: (i, 0))
    return _call(body, name="adamw", grid=(r // tr,), in_specs=[spec] * 4, out_specs=[spec] * 3,
                 out_shape=[_sds((r, n))] * 3, sem=("parallel",))(w, g, m, v)


BIG = ("w_in", "w_branch_a", "w_branch_b", "w_out", "w_up", "w_down", "conv_w")
BIG_ROWS = 3584
SMALL = ("b_mod", "b_in", "conv_b", "ln1_g", "ln1_b", "ln2_g", "ln2_b", "c_ctx", "attn_sink", "q_norm_g", "k_norm_g")
SMALL_ROWS = 8 * len(SMALL)


def _rows(a, n_rows):
    flat = a.reshape(-1)
    return jnp.pad(flat, (0, n_rows * D_MODEL - flat.shape[0])).reshape(n_rows, D_MODEL)


def _group8(a):
    return _rep8(_rows(a, 1)) if a.size <= D_MODEL else _rows(a, 8)


def _ungroup8(p, shape):
    size = math.prod(shape)
    return (p[0, :size] if size <= D_MODEL else p.reshape(-1)[:size]).reshape(shape)


def _pack_big(t):
    parts = [t[n].reshape(-1, D_MODEL) for n in BIG[:-1]] + [_rows(t["conv_w"], 8)]
    used = sum(p.shape[0] for p in parts)
    return jnp.concatenate(parts + [jnp.zeros((BIG_ROWS - used, D_MODEL), F32)], axis=0)


def _unpack_big(p, like):
    out, r = {}, 0
    for n in BIG:
        size = math.prod(like[n].shape)
        nr = size // D_MODEL if n != "conv_w" else 8
        out[n] = p[r:r + nr].reshape(-1)[:size].reshape(like[n].shape)
        r += nr
    return out


def _pack_small(t):
    return jnp.concatenate([_group8(t[n]) for n in SMALL], axis=0)


def _unpack_small(p, like):
    return {n: _ungroup8(p[8 * i:8 * i + 8], like[n].shape) for i, n in enumerate(SMALL)}


WEIGHTS = ("c_ctx", "w_mod", "b_mod", "w_in", "b_in", "attn_sink", "q_norm_g", "k_norm_g", "w_branch_a", "w_branch_b",
           "w_out", "ln1_g", "ln1_b", "w_up", "conv_w", "conv_b", "w_down", "ln2_g", "ln2_b")


def kernel(x, c, ctx, c_ctx, w_mod, b_mod, w_in, b_in, attn_sink, q_norm_g, k_norm_g, w_branch_a, w_branch_b, w_out, ln1_g, ln1_b, w_up, conv_w, conv_b, w_down, ln2_g, ln2_b, loss_target, m_c_ctx, m_w_mod, m_b_mod, m_w_in, m_b_in, m_attn_sink, m_q_norm_g, m_k_norm_g, m_w_branch_a, m_w_branch_b, m_w_out, m_ln1_g, m_ln1_b, m_w_up, m_conv_w, m_conv_b, m_w_down, m_ln2_g, m_ln2_b, v_c_ctx, v_w_mod, v_b_mod, v_w_in, v_b_in, v_attn_sink, v_q_norm_g, v_k_norm_g, v_w_branch_a, v_w_branch_b, v_w_out, v_ln1_g, v_ln1_b, v_w_up, v_conv_w, v_conv_b, v_w_down, v_ln2_g, v_ln2_b):
    w = dict(c_ctx=c_ctx, w_mod=w_mod, b_mod=b_mod, w_in=w_in, b_in=b_in, attn_sink=attn_sink, q_norm_g=q_norm_g,
             k_norm_g=k_norm_g, w_branch_a=w_branch_a, w_branch_b=w_branch_b, w_out=w_out, ln1_g=ln1_g, ln1_b=ln1_b,
             w_up=w_up, conv_w=conv_w, conv_b=conv_b, w_down=w_down, ln2_g=ln2_g, ln2_b=ln2_b)
    m = dict(c_ctx=m_c_ctx, w_mod=m_w_mod, b_mod=m_b_mod, w_in=m_w_in, b_in=m_b_in, attn_sink=m_attn_sink,
             q_norm_g=m_q_norm_g, k_norm_g=m_k_norm_g, w_branch_a=m_w_branch_a, w_branch_b=m_w_branch_b, w_out=m_w_out,
             ln1_g=m_ln1_g, ln1_b=m_ln1_b, w_up=m_w_up, conv_w=m_conv_w, conv_b=m_conv_b, w_down=m_w_down,
             ln2_g=m_ln2_g, ln2_b=m_ln2_b)
    v = dict(c_ctx=v_c_ctx, w_mod=v_w_mod, b_mod=v_b_mod, w_in=v_w_in, b_in=v_b_in, attn_sink=v_attn_sink,
             q_norm_g=v_q_norm_g, k_norm_g=v_k_norm_g, w_branch_a=v_w_branch_a, w_branch_b=v_w_branch_b, w_out=v_w_out,
             ln1_g=v_ln1_g, ln1_b=v_ln1_b, w_up=v_w_up, conv_w=v_conv_w, conv_b=v_conv_b, w_down=v_w_down,
             ln2_g=v_ln2_g, ln2_b=v_ln2_b)
    xp, yp, _ = _mesh_pos()
    me = 2 * xp + yp

    branches = jnp.concatenate([w_branch_a[0], w_branch_b[0]], axis=0)
    wide = jnp.concatenate([w_mod[0], w_in[0], w_up[0], branches], axis=1).astype(BF16)
    tall = jnp.concatenate([w_out[0], w_down[0]], axis=0).astype(BF16)
    wide4, tall4, cw4 = _gather_shards([wide, tall], conv_w[0])
    n_mod, n_in, n_up = w_mod.shape[-1], w_in.shape[-1], w_up.shape[-1]
    wmod4 = wide4[:, :, :n_mod]
    win4 = wide4[:, :, n_mod:n_mod + n_in]
    wup4 = wide4[:, :, n_mod + n_in:n_mod + n_in + n_up]
    br4 = wide4[:, :, n_mod + n_in + n_up:]
    n_br = w_branch_a.shape[1]
    wba = br4[:, :n_br].transpose(1, 0, 2).reshape(n_br, D_MODEL)
    wbb = br4[:, n_br:].transpose(1, 0, 2).reshape(n_br, D_MODEL)
    n_out = w_out.shape[1]
    w_out_full = tall4[:, :n_out].reshape(D_MODEL, D_MODEL)
    w_down_full = tall4[:, n_out:].reshape(D_FF, D_MODEL)
    cw_full = cw4.transpose(1, 0, 2).reshape(3, 2 * D_FF)

    loss, grad_x, g = _local_step(
        x[0], c, ctx[0], c_ctx[None], wmod4, b_mod, win4, b_in, attn_sink[0], q_norm_g, k_norm_g, wba, wbb, w_out_full,
        ln1_g, ln1_b, wup4, cw_full, conv_b, w_down_full, ln2_g, ln2_b, loss_target[0])
    loss = lax.psum(loss, ("x", "y", "c"))

    sent = dict(c=c, dmod=g["dmod"], dmodc=g["dmodc"], b_in=g["b_in"], conv_b=g["conv_b"], ln1_g=g["ln1_g"],
                ln1_b=g["ln1_b"], ln2_g=g["ln2_g"], ln2_b=g["ln2_b"], c_ctx=g["c_ctx"], attn_sink=g["sink"],
                q_norm_g=g["qn"], k_norm_g=g["kn"])
    every = _allgather_rows(jnp.concatenate([_group8(a) for a in sent.values()], axis=0))
    total = _sum_leading(every, name="sum_devices")
    slot = {n: slice(8 * i, 8 * i + 8) for i, n in enumerate(sent)}
    gs = {n: _ungroup8(total[slot[n]], sent[n].shape) for n in SMALL if n in sent}
    dmodc_sum = jnp.concatenate([_ungroup8(total[slot["dmodc"]], (1, 2 * D_MODEL)), jnp.zeros((1, 4 * D_MODEL), F32)],
                                axis=1)
    gs["b_mod"] = _ungroup8(total[slot["dmod"]], b_mod.shape) + dmodc_sum
    acts = jnp.concatenate([every[:, slot["c"].start], _rep8(c_ctx)], axis=0)
    dmods = jnp.concatenate([every[:, slot["dmod"]].reshape(N_DEV, -1)[:, :6 * D_MODEL], _first_row(dmodc_sum)], axis=0)
    g_w_mod = _silu_outer(acts, lax.dynamic_slice_in_dim(dmods, me * n_mod, n_mod, axis=1))

    cw_g4 = _to_blocks4(g["conv_w"])
    packed = jnp.concatenate([
        g["w_in4"].reshape(N_CHIPS, -1, D_MODEL), _to_blocks4(g["wba"]).reshape(N_CHIPS, -1, D_MODEL),
        _to_blocks4(g["wbb"]).reshape(N_CHIPS, -1, D_MODEL), g["w_out"].reshape(N_CHIPS, -1, D_MODEL),
        g["w_up4"].reshape(N_CHIPS, -1, D_MODEL), g["w_down"].reshape(N_CHIPS, -1, D_MODEL),
        jnp.pad(cw_g4.reshape(N_CHIPS, -1), ((0, 0), (0, 8 * D_MODEL - cw_g4.shape[1] * cw_g4.shape[2]))).reshape(
            N_CHIPS, 8, D_MODEL),
        jnp.zeros((N_CHIPS, BIG_ROWS - 3528, D_MODEL), F32)], axis=1)
    rh = BIG_ROWS // 2
    cpos = lax.axis_index("c")
    my_half = lax.dynamic_slice_in_dim(packed, cpos * rh, rh, axis=1)
    chip_sum = _add_blocks(my_half, _swap_other_half(packed), BF16)
    half_sum = _sum_leading(_scatter_to_chips(chip_sum), name="sum_chips")
    g_big = _unpack_big(_join_halves(half_sum), w)

    grads = dict(gs, w_mod=g_w_mod, **g_big)

    def pack_all(t):
        rows = jnp.concatenate([_pack_big(t), t["w_mod"].reshape(-1, D_MODEL), _pack_small(t)], axis=0)
        return jnp.pad(rows, ((0, -rows.shape[0] % 256), (0, 0)))

    delta_p, new_m_p, new_v_p = _adamw(pack_all(w), pack_all(grads), pack_all(m), pack_all(v))

    def unpack_all(p):
        r_mod = BIG_ROWS + w_mod.size // D_MODEL
        out = _unpack_big(p[:BIG_ROWS], w)
        out["w_mod"] = p[BIG_ROWS:r_mod].reshape(w_mod.shape)
        out.update(_unpack_small(p[r_mod:r_mod + SMALL_ROWS], w))
        return out

    grads = {n: grads[n].reshape(w[n].shape) for n in WEIGHTS}
    delta, new_m, new_v = unpack_all(delta_p), unpack_all(new_m_p), unpack_all(new_v_p)
    return (loss, grad_x[None], *[grads[n] for n in WEIGHTS], *[delta[n] for n in WEIGHTS],
            *[new_m[n] for n in WEIGHTS], *[new_v[n] for n in WEIGHTS])
```

```python
import functools
import math

import jax
import jax.numpy as jnp
from jax import lax
from jax.experimental import pallas as pl
from jax.experimental.pallas import tpu as pltpu

F32 = jnp.float32
BF16 = jnp.bfloat16

D_MODEL = 1024
HEAD_DIM = 64
N_HEADS = 8
N_KV = 2
WINDOW = 128
GRID_W = 64
ROPE_THETA = 10000.0
D_FF = 2816
LN_EPS = 1e-5
QK_EPS = 1e-6
ALPHA = 2.0 ** 0.25
Q_SCALE = HEAD_DIM ** -0.5
OFF_GA = 1536
IN_COLS = 3584
ADAM_LR, ADAM_B1, ADAM_B2, ADAM_EPS, ADAM_WD, ADAM_STEP = 0.001, 0.9, 0.999, 1e-8, 0.01, 10

LANES = 128
VMEM_BUDGET = 52 * 1024 * 1024
N_CHIPS = 4
N_DEV = 8
NEG = -1e30
MESH = pl.DeviceIdType.MESH


def _sigmoid(x):
    return 1.0 / (1.0 + jnp.exp(-x))


def _dot(a, b):
    return jnp.dot(a, b, preferred_element_type=F32)


def _dot_nt(a, b):
    return lax.dot_general(a, b, (((1,), (1,)), ((), ())), preferred_element_type=F32)


def _dot_tn(a, b):
    return lax.dot_general(a, b, (((0,), (0,)), ((), ())), preferred_element_type=F32)


def _call(body, *, name, grid, in_specs, out_specs, out_shape, scratch=(), sem=None, **kw):
    params = dict(vmem_limit_bytes=VMEM_BUDGET)
    if sem is not None:
        params["dimension_semantics"] = sem
    return pl.pallas_call(body, name=name, grid=grid, in_specs=in_specs, out_specs=out_specs,
                          out_shape=out_shape, scratch_shapes=list(scratch),
                          compiler_params=pltpu.CompilerParams(**params), **kw)


def _full(shape):
    n = len(shape)
    return pl.BlockSpec(shape, lambda *_: (0,) * n)


def _sds(shape, dtype=F32):
    return jax.ShapeDtypeStruct(shape, dtype)


def _mm_nn4(a, shift, scale, w4, bias, *, mode, split_out, out_dtype, tm, name):
    m, kdim = a.shape
    nb, _, ns = w4.shape

    def body(a_ref, sh_ref, sc_ref, w_ref, b_ref, o_ref):
        av = a_ref[...]
        if mode == "modulate":
            av = av * (1.0 + sc_ref[...]) + sh_ref[...]
        else:
            av = av * _sigmoid(av)
        o_ref[...] = (_dot(av.astype(BF16), w_ref[...]) + b_ref[...]).astype(out_dtype)

    if split_out:
        out_shape = _sds((2, m, 2 * ns), out_dtype)
        out_spec = pl.BlockSpec((None, tm, ns), lambda i, k: (k // 2, i, k % 2))
    else:
        out_shape = _sds((m, nb * ns), out_dtype)
        out_spec = pl.BlockSpec((tm, ns), lambda i, k: (i, k))
    return _call(
        body, name=name, grid=(m // tm, nb),
        in_specs=[pl.BlockSpec((tm, kdim), lambda i, k: (i, 0)),
                  pl.BlockSpec((1, kdim), lambda i, k: (0, 0)),
                  pl.BlockSpec((1, kdim), lambda i, k: (0, 0)),
                  pl.BlockSpec((None, kdim, ns), lambda i, k: (k, 0, 0)),
                  pl.BlockSpec((1, ns), lambda i, k: (0, k))],
        out_specs=out_spec, out_shape=out_shape, sem=("parallel", "arbitrary"),
    )(a, shift, scale, w4, bias)


def _mm_tn(a, b, *, a_spec, b_spec, grid, out_shape, out_spec, name, mod=None, init=None, colsum_spec=None,
           colsum_shape=None, a_is_t=False):
    red = len(grid) - 1
    has_mod, has_init, has_cs = mod is not None, init is not None, colsum_spec is not None

    def body(*refs):
        refs = list(refs)
        a_ref, b_ref = refs[0], refs[1]
        pos = 2
        if has_mod:
            sh_ref, sc_ref = refs[2], refs[3]
            pos = 4
        if has_init:
            init_ref = refs[pos]
            pos += 1
        o_ref = refs[pos]
        cs_ref = refs[pos + 1] if has_cs else None
        s = pl.program_id(red)

        @pl.when(s == 0)
        def _():
            o_ref[...] = init_ref[...] if has_init else jnp.zeros(o_ref.shape, F32)
            if has_cs:
                cs_ref[...] = jnp.zeros(cs_ref.shape, F32)

        av = a_ref[...]
        if has_mod:
            av = av * (1.0 + sc_ref[...]) + sh_ref[...]
        bv = b_ref[...]
        o_ref[...] += (_dot if a_is_t else _dot_tn)(av.astype(BF16), bv)
        if has_cs:
            cs_ref[...] += jnp.broadcast_to(jnp.sum(bv.astype(F32), axis=0, keepdims=True), cs_ref.shape)

    ins, in_specs = [a, b], [a_spec, b_spec]
    if has_mod:
        kdim = mod[0].shape[-1]
        ins += list(mod)
        in_specs += [_full((1, kdim)), _full((1, kdim))]
    if has_init:
        ins.append(init)
        in_specs.append(out_spec)
    out_specs, out_shapes = out_spec, out_shape
    if has_cs:
        out_specs, out_shapes = [out_spec, colsum_spec], [out_shape, colsum_shape]
    sem = ("parallel",) * red + ("arbitrary",)
    return _call(body, name=name, grid=grid, in_specs=in_specs, out_specs=out_specs, out_shape=out_shapes,
                 sem=sem)(*ins)


def _rope_tables(n_tok):
    pos = jnp.arange(n_tok, dtype=jnp.int32)
    rows = (pos // GRID_W).astype(F32)
    cols = (pos % GRID_W).astype(F32)
    n_freq = HEAD_DIM // 4
    inv_freq = ROPE_THETA ** (-jnp.arange(n_freq, dtype=F32) / n_freq)
    ang_r = rows[:, None] * inv_freq
    ang_c = cols[:, None] * inv_freq
    cos = jnp.concatenate([jnp.cos(ang_r)] * 2 + [jnp.cos(ang_c)] * 2, axis=-1)
    sin = jnp.concatenate([-jnp.sin(ang_r), jnp.sin(ang_r), -jnp.sin(ang_c), jnp.sin(ang_c)], axis=-1)
    return jnp.tile(cos, (1, 2)), jnp.tile(sin, (1, 2))


def _lane(shape):
    return lax.broadcasted_iota(jnp.int32, shape, 1)


def _rope_partner(t, lane):
    return jnp.where((lane % 32) < 16, pltpu.roll(t, LANES - 16, 1), pltpu.roll(t, 16, 1))


def _half_mean(s, lane):
    lo = jnp.sum(jnp.where(lane < HEAD_DIM, s, 0.0), axis=-1, keepdims=True)
    hi = jnp.sum(jnp.where(lane < HEAD_DIM, 0.0, s), axis=-1, keepdims=True)
    return jnp.where(lane < HEAD_DIM, lo, hi) * (1.0 / HEAD_DIM)


def _prep(proj, cos, sin, qg, kg, *, tm, name):
    m = proj.shape[0]

    def body(p_ref, cos_ref, sin_ref, qg_ref, kg_ref, qa_ref, ka_ref, va_ref, qb_ref, kb_ref, vb_ref):
        lane = _lane((tm, LANES))
        cosv, sinv = cos_ref[...], sin_ref[...]
        low = lane < HEAD_DIM

        def rope(t):
            return t * cosv + _rope_partner(t, lane) * sinv

        def rms(t, g):
            return t * lax.rsqrt(_half_mean(t * t, lane) + QK_EPS) * g

        def place(q_ref, j, chunk):
            sw = pltpu.roll(chunk, HEAD_DIM, 1)
            if j < 2:
                h0, h1 = jnp.where(low, chunk, 0.0), jnp.where(low, sw, 0.0)
            else:
                h0, h1 = jnp.where(low, 0.0, sw), jnp.where(low, 0.0, chunk)
            q_ref[2 * j] = h0.astype(BF16)
            q_ref[2 * j + 1] = h1.astype(BF16)

        for j in range(4):
            place(qa_ref, j, rope(p_ref[:, j * LANES:(j + 1) * LANES]) * Q_SCALE)
            place(qb_ref, j, rope(rms(p_ref[:, 768 + j * LANES:768 + (j + 1) * LANES], qg_ref[...])) * Q_SCALE)
        ka_ref[...] = rope(p_ref[:, 512:640]).astype(BF16)
        va_ref[...] = p_ref[:, 640:768].astype(BF16)
        kb_ref[...] = rope(rms(p_ref[:, 1280:1408], kg_ref[...])).astype(BF16)
        vb_ref[...] = p_ref[:, 1408:1536].astype(BF16)

    row = pl.BlockSpec((tm, LANES), lambda i: (i, 0))
    qspec = pl.BlockSpec((N_HEADS, tm, LANES), lambda i: (0, i, 0))
    return _call(
        body, name=name, grid=(m // tm,),
        in_specs=[pl.BlockSpec((tm, OFF_GA), lambda i: (i, 0)), row, row, _full((1, LANES)), _full((1, LANES))],
        out_specs=[qspec, row, row, qspec, row, row],
        out_shape=[_sds((N_HEADS, m, LANES), BF16), _sds((m, LANES), BF16), _sds((m, LANES), BF16),
                   _sds((N_HEADS, m, LANES), BF16), _sds((m, LANES), BF16), _sds((m, LANES), BF16)],
        sem=("parallel",),
    )(proj, cos, sin, qg, kg)


def _prep_bwd(dqa, dka, dva, dqb, dkb, dvb, proj, cos, sin, qg, kg, dgl, *, tm, name):
    m = proj.shape[0]

    def body(dqa_ref, dka_ref, dva_ref, dqb_ref, dkb_ref, dvb_ref, p_ref, cos_ref, sin_ref, qg_ref, kg_ref,
             dgl_ref, dp_ref, dqg_ref, dkg_ref):
        i = pl.program_id(0)
        lane = _lane((tm, LANES))
        cosv, sinv = cos_ref[...], sin_ref[...]
        low = lane < HEAD_DIM

        @pl.when(i == 0)
        def _():
            dqg_ref[...] = jnp.zeros(dqg_ref.shape, F32)
            dkg_ref[...] = jnp.zeros(dkg_ref.shape, F32)

        def unrope(d):
            return d * cosv - _rope_partner(d, lane) * sinv

        def unplace(dq_ref, j):
            d0, d1 = dq_ref[2 * j], dq_ref[2 * j + 1]
            if j < 2:
                return jnp.where(low, d0, pltpu.roll(d1, HEAD_DIM, 1))
            return jnp.where(low, pltpu.roll(d0, HEAD_DIM, 1), d1)

        def unrms(dtn, t, g):
            r = lax.rsqrt(_half_mean(t * t, lane) + QK_EPS)
            u = dtn * g
            dt = r * u - t * (r * r * r) * _half_mean(u * t, lane)
            return dt, jnp.sum(dtn * t * r, axis=0, keepdims=True)

        for j in range(4):
            dp_ref[:, j * LANES:(j + 1) * LANES] = (unrope(unplace(dqa_ref, j)) * Q_SCALE).astype(BF16)
            c0 = 768 + j * LANES
            dt, dg = unrms(unrope(unplace(dqb_ref, j)) * Q_SCALE, p_ref[:, c0:c0 + LANES], qg_ref[...])
            dp_ref[:, c0:c0 + LANES] = dt.astype(BF16)
            dqg_ref[:, j * LANES:(j + 1) * LANES] += dg
        dp_ref[:, 512:640] = unrope(dka_ref[...]).astype(BF16)
        dp_ref[:, 640:768] = dva_ref[...].astype(BF16)
        dt, dg = unrms(unrope(dkb_ref[...]), p_ref[:, 1280:1408], kg_ref[...])
        dp_ref[:, 1280:1408] = dt.astype(BF16)
        dkg_ref[...] += dg
        dp_ref[:, 1408:1536] = dvb_ref[...].astype(BF16)
        dp_ref[:, OFF_GA:] = dgl_ref[...]

    row = pl.BlockSpec((tm, LANES), lambda i: (i, 0))
    qspec = pl.BlockSpec((N_HEADS, tm, LANES), lambda i: (0, i, 0))
    return _call(
        body, name=name, grid=(m // tm,),
        in_specs=[qspec, row, row, qspec, row, row, pl.BlockSpec((tm, OFF_GA), lambda i: (i, 0)), row, row,
                  _full((1, LANES)), _full((1, LANES)), pl.BlockSpec((tm, IN_COLS - OFF_GA), lambda i: (i, 0))],
        out_specs=[pl.BlockSpec((tm, IN_COLS), lambda i: (i, 0)), _full((1, 512)), _full((1, LANES))],
        out_shape=[_sds((m, IN_COLS), BF16), _sds((1, 512)), _sds((1, LANES))],
        sem=("arbitrary",),
    )(dqa, dka, dva, dqb, dkb, dvb, proj, cos, sin, qg, kg, dgl)


def _attn_glob_fwd(qt, k, vt, kc, vct, *, tq, tk):
    nh, _, s = qt.shape
    nc = kc.shape[0]
    n_chunks = s // tk
    half = LANES // 2

    def body(qt_ref, k_ref, vt_ref, kc_ref, vct_ref, ot_ref, lse_ref, mrun_ref, p_hbm,
             acc_sc, st_sc, stage_sc, stagec_sc, sems, semc):
        h, i = pl.program_id(0), pl.program_id(1)
        qtv = qt_ref[...]
        acc_sc[...] = jnp.zeros(acc_sc.shape, F32)
        qcols = pl.ds(pl.multiple_of(i * tq, tq), tq)

        def p_out(slot, c):
            return pltpu.make_async_copy(stage_sc.at[slot], p_hbm.at[h, pl.ds(pl.multiple_of(c * tk, tk), tk), qcols],
                                         sems.at[slot])

        def update(st, vtv, m_old):
            m_new = jnp.maximum(m_old, jnp.max(st, axis=0, keepdims=True))
            pb = jnp.exp(st - m_new).astype(BF16)
            acc_sc[...] = acc_sc[...] * jnp.exp(m_old - m_new) + _dot(vtv, pb)
            return m_new, pb

        m, pbc = update(_dot(kc_ref[...], qtv), vct_ref[...], jnp.full((1, tq), NEG, F32))
        mrun_ref[pl.ds(n_chunks, 1), :] = m
        stagec_sc[...] = pbc
        ctx_out = pltpu.make_async_copy(stagec_sc, p_hbm.at[h, pl.ds(s, nc), qcols], semc)
        ctx_out.start()

        def step(c, st, m_old):
            slot = c % 2
            off = pl.multiple_of(c * tk, tk)
            nxt = pl.multiple_of(jnp.minimum(c + 1, n_chunks - 1) * tk, tk)
            st_next = _dot(k_ref[pl.ds(nxt, tk), :], qtv)
            m_new, pb = update(st, vt_ref[:, pl.ds(off, tk)], m_old)
            mrun_ref[pl.ds(c, 1), :] = m_new
            stage_sc[slot] = pb
            p_out(slot, c).start()
            return st_next, m_new

        def loop(c, m_old):
            st_next, m_new = step(c, st_sc[...], m_old)
            p_out(1 - c % 2, c - 1).wait()
            st_sc[...] = st_next
            return m_new

        stage_sc[1] = jnp.zeros((tk, tq), BF16)
        pltpu.make_async_copy(stage_sc.at[1], p_hbm.at[h, pl.ds(s + nc, tk), qcols], sems.at[1]).start()
        st_sc[...] = _dot(k_ref[pl.ds(0, tk), :], qtv)
        m = lax.fori_loop(0, n_chunks, loop, m)
        p_out((n_chunks - 1) % 2, n_chunks - 1).wait()
        ctx_out.wait()
        acc = acc_sc[...]
        l = jnp.where(h < nh // N_KV, acc[half:half + 1], acc[0:1])
        ot_ref[...] = (acc / l).astype(BF16)
        lse_ref[...] = m + jnp.log(l)

    grp = nh // N_KV
    return _call(
        body, name="attn_glob_fwd", grid=(nh, s // tq),
        in_specs=[pl.BlockSpec((None, LANES, tq), lambda h, i: (h, 0, i)), _full((s, LANES)),
                  pl.BlockSpec((None, LANES, s), lambda h, i: (h // grp, 0, 0)), _full((nc, LANES)),
                  pl.BlockSpec((None, LANES, nc), lambda h, i: (h // grp, 0, 0))],
        out_specs=[pl.BlockSpec((None, LANES, tq), lambda h, i: (h, 0, i)),
                   pl.BlockSpec((None, 1, tq), lambda h, i: (h, 0, i)),
                   pl.BlockSpec((None, n_chunks + 1, tq), lambda h, i: (h, 0, i)), ANY],
        out_shape=[_sds((nh, LANES, s), BF16), _sds((nh, 1, s)), _sds((nh, n_chunks + 1, s)),
                   _sds((nh, s + nc + tk, s), BF16)],
        scratch=[pltpu.VMEM((LANES, tq), F32), pltpu.VMEM((tk, tq), F32), pltpu.VMEM((2, tk, tq), BF16),
                 pltpu.VMEM((nc, tq), BF16), pltpu.SemaphoreType.DMA((2,)), pltpu.SemaphoreType.DMA],
        sem=("parallel", "parallel"),
    )(qt, k, vt, kc, vct)


def _attn_glob_bwd(qt, dot, ot, lse, mrun, p, kt, v, kct, vc, *, tq, tk):
    nh, _, s = qt.shape
    nc = vc.shape[0]
    n_q = s // tq
    n_chunks = s // tk
    n_run = mrun.shape[1] - 1
    per_run = n_chunks // n_run

    def body(qt_ref, dot_ref, ot_ref, lse_ref, mrun_ref, p_hbm, kt_ref, v_ref, kct_ref, vc_ref,
             dqt_ref, dkt_ref, dvt_ref, dkct_ref, dvct_ref, acc_sc, dp_sc, dkt_sc, dvt_sc, p_sc, pc_sc, sems, semc):
        h, i = pl.program_id(0), pl.program_id(1)

        @pl.when(jnp.logical_and(h == 0, i == 0))
        def _():
            dkct_ref[...] = jnp.zeros(dkct_ref.shape, F32)
            dvct_ref[...] = jnp.zeros(dvct_ref.shape, F32)
            dkt_sc[...] = jnp.zeros(dkt_sc.shape, F32)
            dvt_sc[...] = jnp.zeros(dvt_sc.shape, F32)

        qcols = pl.ds(pl.multiple_of(i * tq, tq), tq)

        def p_in(slot, c):
            return pltpu.make_async_copy(p_hbm.at[h, pl.ds(pl.multiple_of(c * tk, tk), tk), qcols], p_sc.at[slot],
                                         sems.at[slot])

        ctx_in = pltpu.make_async_copy(p_hbm.at[h, pl.ds(s, nc), qcols], pc_sc, semc)
        ctx_in.start()
        p_in(0, 0).start()
        qtv, dotv, lse = qt_ref[...], dot_ref[...], lse_ref[...]
        delta = jnp.sum(dotv.astype(F32) * ot_ref[...].astype(F32), axis=0, keepdims=True)

        def grads(pt_stored, m_row, dpt):
            pt = pt_stored.astype(F32) * jnp.exp(m_row - lse)
            return pt.astype(BF16), (pt * (dpt - delta)).astype(BF16)

        dp_sc[...] = _dot(v_ref[pl.ds(0, tk), :], dotv)
        ctx_in.wait()
        pb, dsb = grads(pc_sc[...], mrun_ref[pl.ds(n_run, 1), :], _dot(vc_ref[...], dotv))
        acc_sc[...] = _dot(kct_ref[...], dsb)
        dkct_ref[...] += _dot_nt(qtv, dsb)
        dvct_ref[...] += _dot_nt(dotv, pb)

        def loop(c, carry):
            slot = c % 2
            off = pl.multiple_of(c * tk, tk)
            nxt = pl.multiple_of(jnp.minimum(c + 1, n_chunks - 1) * tk, tk)
            p_in(slot, c).wait()
            p_in(1 - slot, jnp.minimum(c + 1, n_chunks - 1)).start()
            dpt = dp_sc[...]
            dp_next = _dot(v_ref[pl.ds(nxt, tk), :], dotv)
            pb, dsb = grads(p_sc[slot], mrun_ref[pl.ds(c // per_run, 1), :], dpt)
            acc_sc[...] += _dot(kt_ref[:, pl.ds(off, tk)], dsb)
            dkt_sc[:, pl.ds(off, tk)] += _dot_nt(qtv, dsb)
            dvt_sc[:, pl.ds(off, tk)] += _dot_nt(dotv, pb)
            dp_sc[...] = dp_next
            return carry

        lax.fori_loop(0, n_chunks, loop, 0)
        p_in(n_chunks % 2, n_chunks - 1).wait()
        dqt_ref[...] = acc_sc[...]

        @pl.when(jnp.logical_and(h == nh - 1, i == n_q - 1))
        def _():
            pltpu.sync_copy(dkt_sc, dkt_ref)
            pltpu.sync_copy(dvt_sc, dvt_ref)

    qs = pl.BlockSpec((None, LANES, tq), lambda h, i: (h, 0, i))
    rs = pl.BlockSpec((None, 1, tq), lambda h, i: (h, 0, i))
    return _call(
        body, name="attn_glob_bwd", grid=(nh, n_q),
        in_specs=[qs, qs, qs, rs, pl.BlockSpec((None, n_run + 1, tq), lambda h, i: (h, 0, i)), ANY,
                  _full((LANES, s)), _full((s, LANES)), _full((LANES, nc)), _full((nc, LANES))],
        out_specs=[qs, ANY, ANY, _full((LANES, nc)), _full((LANES, nc))],
        out_shape=[_sds((nh, LANES, s)), _sds((LANES, s)), _sds((LANES, s)), _sds((LANES, nc)), _sds((LANES, nc))],
        scratch=[pltpu.VMEM((LANES, tq), F32), pltpu.VMEM((tk, tq), F32), pltpu.VMEM((LANES, s), F32),
                 pltpu.VMEM((LANES, s), F32), pltpu.VMEM((2, tk, tq), BF16), pltpu.VMEM((nc, tq), BF16),
                 pltpu.SemaphoreType.DMA((2,)), pltpu.SemaphoreType.DMA],
        sem=("arbitrary", "arbitrary"),
    )(qt, dot, ot, lse, mrun, p, kt, v, kct, vc)


WIN_SPAN = 2 * WINDOW


def _band(rows0, cols0, shape):
    r = rows0 + lax.broadcasted_iota(jnp.int32, shape, 0)
    c = cols0 + lax.broadcasted_iota(jnp.int32, shape, 1)
    return jnp.abs(r - c) <= WINDOW


def _win_start(blk, t, s):
    return pl.multiple_of(jnp.clip(blk * t - WINDOW, 0, s - t - WIN_SPAN), WINDOW)


def _attn_win_fwd(q, k, v, kc, vc, sink, *, tq):
    nh, s, _ = q.shape
    nc = kc.shape[0]
    tw = tq + WIN_SPAN

    def body(sink_ref, q_ref, k_ref, v_ref, kc_ref, vc_ref, o_ref, lse_ref):
        h, i = pl.program_id(0), pl.program_id(1)
        k0 = _win_start(i, tq, s)
        qv = q_ref[...]
        kv, vv = k_ref[pl.ds(k0, tw), :], v_ref[pl.ds(k0, tw), :]
        sc = jnp.where(_band(i * tq, k0, (tq, tw)), _dot_nt(qv, kv), NEG)
        scc = _dot_nt(qv, kc_ref[...])
        snk = sink_ref[h]
        m = jnp.maximum(jnp.maximum(jnp.max(sc, axis=-1, keepdims=True), jnp.max(scc, axis=-1, keepdims=True)), snk)
        p, pc = jnp.exp(sc - m), jnp.exp(scc - m)
        l = jnp.sum(p, axis=-1, keepdims=True) + jnp.sum(pc, axis=-1, keepdims=True) + jnp.exp(snk - m)
        acc = _dot(p.astype(BF16), vv) + _dot(pc.astype(BF16), vc_ref[...])
        o_ref[...] = (acc / l).astype(BF16)
        lse_ref[...] = m + jnp.log(l)

    return _call(
        body, name="attn_win_fwd", grid=(nh, s // tq),
        in_specs=[pl.BlockSpec(memory_space=pltpu.SMEM),
                  pl.BlockSpec((None, tq, LANES), lambda h, i: (h, i, 0)),
                  _full((s, LANES)), _full((s, LANES)), _full((nc, LANES)), _full((nc, LANES))],
        out_specs=[pl.BlockSpec((None, tq, LANES), lambda h, i: (h, i, 0)),
                   pl.BlockSpec((None, tq, 1), lambda h, i: (h, i, 0))],
        out_shape=[_sds((nh, s, LANES), BF16), _sds((nh, s, 1))],
        sem=("parallel", "parallel"),
    )(sink, q, k, v, kc, vc)


def _attn_win_dq(q, do, o, lse, k, v, kc, vc, sink, *, tq):
    nh, s, _ = q.shape
    nc = kc.shape[0]
    tw = tq + WIN_SPAN
    nq = s // tq

    def body(sink_ref, q_ref, do_ref, o_ref, lse_ref, k_ref, v_ref, kc_ref, vc_ref,
             dq_ref, dl_ref, dkc_ref, dvc_ref, dsk_ref):
        h, i = pl.program_id(0), pl.program_id(1)

        @pl.when(jnp.logical_and(h == 0, i == 0))
        def _():
            dkc_ref[...] = jnp.zeros(dkc_ref.shape, F32)
            dvc_ref[...] = jnp.zeros(dvc_ref.shape, F32)

        k0 = _win_start(i, tq, s)
        qv, dov, lse = q_ref[...], do_ref[...], lse_ref[...]
        kv, vv = k_ref[pl.ds(k0, tw), :], v_ref[pl.ds(k0, tw), :]
        kcv, vcv = kc_ref[...], vc_ref[...]
        delta = jnp.sum(dov.astype(F32) * o_ref[...].astype(F32), axis=-1, keepdims=True)
        dl_ref[...] = delta
        p = jnp.where(_band(i * tq, k0, (tq, tw)), jnp.exp(_dot_nt(qv, kv) - lse), 0.0)
        ds = (p * (_dot_nt(dov, vv) - delta)).astype(BF16)
        pc = jnp.exp(_dot_nt(qv, kcv) - lse)
        dsc = (pc * (_dot_nt(dov, vcv) - delta)).astype(BF16)
        dq_ref[...] = _dot(ds, kv) + _dot(dsc, kcv)
        dkc_ref[...] += _dot_tn(dsc, qv)
        dvc_ref[...] += _dot_tn(pc.astype(BF16), dov)
        dsk = -jnp.sum(jnp.exp(sink_ref[h] - lse) * delta)
        dsk_ref[...] = jnp.full(dsk_ref.shape, dsk, F32)

    qs = pl.BlockSpec((None, tq, LANES), lambda h, i: (h, i, 0))
    cs = pl.BlockSpec((None, tq, 1), lambda h, i: (h, i, 0))
    return _call(
        body, name="attn_win_dq", grid=(nh, nq),
        in_specs=[pl.BlockSpec(memory_space=pltpu.SMEM), qs, qs, qs, cs,
                  _full((s, LANES)), _full((s, LANES)), _full((nc, LANES)), _full((nc, LANES))],
        out_specs=[qs, cs, _full((nc, LANES)), _full((nc, LANES)),
                   pl.BlockSpec((None, None, 8, LANES), lambda h, i: (h, i, 0, 0))],
        out_shape=[_sds((nh, s, LANES)), _sds((nh, s, 1)), _sds((nc, LANES)), _sds((nc, LANES)),
                   _sds((nh, nq, 8, LANES))],
        sem=("arbitrary", "arbitrary"),
    )(sink, q, do, o, lse, k, v, kc, vc)


def _attn_win_dkv(q, do, lse_row, dl_row, k, v, *, tk):
    nh, s, _ = q.shape
    tw = tk + WIN_SPAN

    def body(k_ref, v_ref, q_ref, do_ref, lse_ref, dl_ref, dk_ref, dv_ref, dk_sc, dv_sc):
        h, j = pl.program_id(0), pl.program_id(1)
        q0 = _win_start(j, tk, s)
        kv, vv = k_ref[...], v_ref[...]
        qv, dov = q_ref[pl.ds(q0, tw), :], do_ref[pl.ds(q0, tw), :]
        pt = jnp.where(_band(j * tk, q0, (tk, tw)), jnp.exp(_dot_nt(kv, qv) - lse_ref[:, pl.ds(q0, tw)]), 0.0)
        dst = pt * (_dot_nt(vv, dov) - dl_ref[:, pl.ds(q0, tw)])
        dk, dv = _dot(dst.astype(BF16), qv), _dot(pt.astype(BF16), dov)
        rows = pl.ds(pl.multiple_of(j * tk, tk), tk)

        @pl.when(h == 0)
        def _():
            dk_sc[rows, :] = dk
            dv_sc[rows, :] = dv

        @pl.when(h > 0)
        def _():
            dk_sc[rows, :] += dk
            dv_sc[rows, :] += dv

        @pl.when(jnp.logical_and(h == nh - 1, j == s // tk - 1))
        def _():
            pltpu.sync_copy(dk_sc, dk_ref)
            pltpu.sync_copy(dv_sc, dv_ref)

    ks = pl.BlockSpec((tk, LANES), lambda h, j: (j, 0))
    qs = pl.BlockSpec((None, s, LANES), lambda h, j: (h, 0, 0))
    rs = pl.BlockSpec((None, 1, s), lambda h, j: (h, 0, 0))
    return _call(
        body, name="attn_win_dkv", grid=(nh, s // tk),
        in_specs=[ks, ks, qs, qs, rs, rs], out_specs=[ANY, ANY],
        out_shape=[_sds((s, LANES)), _sds((s, LANES))],
        scratch=[pltpu.VMEM((s, LANES), F32), pltpu.VMEM((s, LANES), F32)],
        sem=("arbitrary", "arbitrary"),
    )(k, v, q, do, lse_row, dl_row)


def _ln_fwd(z, g, b):
    mu = jnp.mean(z, axis=-1, keepdims=True)
    zc = z - mu
    r = lax.rsqrt(jnp.mean(zc * zc, axis=-1, keepdims=True) + LN_EPS)
    return zc * r * g + b, mu, r


def _ln_bwd(dy, xhat, r, g):
    dxh = dy * g
    return r * (dxh - jnp.mean(dxh, axis=-1, keepdims=True) - xhat * jnp.mean(dxh * xhat, axis=-1, keepdims=True))


def _heads_matmul(o_ref, w_ref):
    acc = _dot(o_ref[0], w_ref[0])
    for h in range(1, N_HEADS):
        acc += _dot(o_ref[h], w_ref[h])
    return acc


def _gate_specs(tm):
    return [pl.BlockSpec((tm, 512), functools.partial(lambda i, b: (i, b), b=OFF_GA // 512 + b)) for b in range(4)]


def _merge_fwd(oa, ob, proj, x, gate1, wba, wbb, w_out, ln_g, ln_b, *, tm):
    s = x.shape[0]

    def body(oa_ref, ob_ref, g0, g1, g2, g3, x_ref, gt_ref, wba_ref, wbb_ref, wo_ref, lg_ref, lb_ref,
             x1_ref, y_ref, mu_ref, r_ref):
        ga = _sigmoid(jnp.concatenate([g0[...], g1[...]], axis=1))
        gb = _sigmoid(jnp.concatenate([g2[...], g3[...]], axis=1))
        merged = ga * _heads_matmul(oa_ref, wba_ref) + gb * _heads_matmul(ob_ref, wbb_ref)
        y = _dot(merged.astype(BF16), wo_ref[...])
        x1, mu, r = _ln_fwd(ALPHA * x_ref[...] + gt_ref[...] * y, lg_ref[...], lb_ref[...])
        x1_ref[...] = x1
        y_ref[...] = y
        mu_ref[...] = mu
        r_ref[...] = r

    hs = pl.BlockSpec((N_HEADS, tm, LANES), lambda i: (0, i, 0))
    row = pl.BlockSpec((tm, D_MODEL), lambda i: (i, 0))
    col = pl.BlockSpec((tm, 1), lambda i: (i, 0))
    vec = _full((1, D_MODEL))
    wh = _full((N_HEADS, LANES, D_MODEL))
    return _call(
        body, name="merge_fwd", grid=(s // tm,),
        in_specs=[hs, hs, *_gate_specs(tm), row, vec, wh, wh, _full((D_MODEL, D_MODEL)), vec, vec],
        out_specs=[row, row, col, col],
        out_shape=[_sds((s, D_MODEL)), _sds((s, D_MODEL)), _sds((s, 1)), _sds((s, 1))],
        sem=("parallel",),
    )(oa, ob, proj, proj, proj, proj, x, gate1, wba, wbb, w_out, ln_g, ln_b)


def _merge_bwd(dy, oa, ob, oat, obt, proj, wba, wbb, w_out, *, tm):
    s = dy.shape[0]

    def body(dy_ref, oa_ref, ob_ref, oat_ref, obt_ref, g0, g1, g2, g3, wba_ref, wbb_ref, wo_ref,
             dgl_ref, doa_ref, dobt_ref, mg_ref, dwa_ref, dwb_ref):
        @pl.when(pl.program_id(0) == 0)
        def _():
            dwa_ref[...] = jnp.zeros(dwa_ref.shape, F32)
            dwb_ref[...] = jnp.zeros(dwb_ref.shape, F32)

        dm = _dot_nt(dy_ref[...], wo_ref[...])
        ga = _sigmoid(jnp.concatenate([g0[...], g1[...]], axis=1))
        gb = _sigmoid(jnp.concatenate([g2[...], g3[...]], axis=1))
        pa, pb = _heads_matmul(oa_ref, wba_ref), _heads_matmul(ob_ref, wbb_ref)
        mg_ref[...] = (ga * pa + gb * pb).astype(BF16)
        dgl_ref[:, :D_MODEL] = (dm * pa * ga * (1.0 - ga)).astype(BF16)
        dgl_ref[:, D_MODEL:] = (dm * pb * gb * (1.0 - gb)).astype(BF16)
        dpa, dpb = (dm * ga).astype(BF16), (dm * gb).astype(BF16)
        for h in range(N_HEADS):
            doa_ref[h] = _dot_nt(dpa, wba_ref[h]).astype(BF16)
            dobt_ref[h] = _dot_nt(wbb_ref[h], dpb).astype(BF16)
            dwa_ref[h] += _dot(oat_ref[h], dpa)
            dwb_ref[h] += _dot(obt_ref[h], dpb)

    hs = pl.BlockSpec((N_HEADS, tm, LANES), lambda i: (0, i, 0))
    hts = pl.BlockSpec((N_HEADS, LANES, tm), lambda i: (0, 0, i))
    row = pl.BlockSpec((tm, D_MODEL), lambda i: (i, 0))
    wh = _full((N_HEADS, LANES, D_MODEL))
    return _call(
        body, name="merge_bwd", grid=(s // tm,),
        in_specs=[row, hs, hs, hts, hts, *_gate_specs(tm), wh, wh, _full((D_MODEL, D_MODEL))],
        out_specs=[pl.BlockSpec((tm, 2 * D_MODEL), lambda i: (i, 0)), hs, hts, row, wh, wh],
        out_shape=[_sds((s, 2 * D_MODEL), BF16), _sds((N_HEADS, s, LANES), BF16), _sds((N_HEADS, LANES, s), BF16),
                   _sds((s, D_MODEL), BF16), _sds((N_HEADS, LANES, D_MODEL)), _sds((N_HEADS, LANES, D_MODEL))],
        sem=("arbitrary",),
    )(dy, oa, ob, oat, obt, proj, proj, proj, proj, wba, wbb, w_out)


FF_TC = 256


def _shift_rows(t, prev_row, next_row):
    n = t.shape[0]
    r = lax.broadcasted_iota(jnp.int32, t.shape, 0)
    up = jnp.where(r == 0, prev_row, pltpu.roll(t, 1, 0))
    dn = jnp.where(r == n - 1, next_row, pltpu.roll(t, n - 1, 0))
    return up, dn


def _halo_specs(tm, s, tc):
    nb8 = s // 8
    main = pl.BlockSpec((2, tm, tc), lambda j, i: (0, i, j))
    prev = pl.BlockSpec((2, 8, tc), lambda j, i: (0, jnp.maximum(i * (tm // 8) - 1, 0), j))
    nxt = pl.BlockSpec((2, 8, tc), lambda j, i: (0, jnp.minimum((i + 1) * (tm // 8), nb8 - 1), j))
    return main, prev, nxt


def _halo_rows(prev_ref, next_ref, half, i, n_i):
    prev_row = jnp.where(i == 0, 0.0, prev_ref[half, 7:8, :].astype(F32))
    next_row = jnp.where(i == n_i - 1, 0.0, next_ref[half, 0:1, :].astype(F32))
    return prev_row, next_row


def _conv(t, prev_row, next_row, w, b):
    up, dn = _shift_rows(t, prev_row, next_row)
    return w[0:1, :] * up + w[1:2, :] * t + w[2:3, :] * dn + b


def _ffn_act_fwd(u, cw, cb, *, tm):
    _, s, ff = u.shape
    n_i = s // tm

    def body(u_ref, up_ref, un_ref, cw_ref, cb_ref, a_ref):
        i = pl.program_id(1)
        gc = _conv(u_ref[0], *_halo_rows(up_ref, un_ref, 0, i, n_i), cw_ref[0], cb_ref[0])
        vc = _conv(u_ref[1], *_halo_rows(up_ref, un_ref, 1, i, n_i), cw_ref[1], cb_ref[1])
        a_ref[...] = (gc * _sigmoid(gc) * vc).astype(BF16)

    main, prev, nxt = _halo_specs(tm, s, FF_TC)
    return _call(
        body, name="ffn_act_fwd", grid=(ff // FF_TC, n_i),
        in_specs=[main, prev, nxt, pl.BlockSpec((2, 3, FF_TC), lambda j, i: (0, 0, j)),
                  pl.BlockSpec((2, 1, FF_TC), lambda j, i: (0, 0, j))],
        out_specs=pl.BlockSpec((tm, FF_TC), lambda j, i: (i, j)),
        out_shape=_sds((s, ff), BF16), sem=("parallel", "parallel"),
    )(u, u, u, cw, cb)


def _ffn_act_bwd(dy2, w_down, u, cw, cb, *, tm):
    _, s, ff = u.shape
    n_i = s // tm

    def body(dy_ref, wd_ref, u_ref, up_ref, un_ref, cw_ref, cb_ref, dc_ref, dcw_ref, dcb_ref):
        i = pl.program_id(1)

        @pl.when(i == 0)
        def _():
            dcw_ref[...] = jnp.zeros(dcw_ref.shape, F32)
            dcb_ref[...] = jnp.zeros(dcb_ref.shape, F32)

        da = _dot_nt(dy_ref[...], wd_ref[...])
        ug, uv = u_ref[0], u_ref[1]
        ugp, ugn = _shift_rows(ug, *_halo_rows(up_ref, un_ref, 0, i, n_i))
        uvp, uvn = _shift_rows(uv, *_halo_rows(up_ref, un_ref, 1, i, n_i))
        wg, wv = cw_ref[0], cw_ref[1]
        gc = wg[0:1, :] * ugp + wg[1:2, :] * ug + wg[2:3, :] * ugn + cb_ref[0]
        vc = wv[0:1, :] * uvp + wv[1:2, :] * uv + wv[2:3, :] * uvn + cb_ref[1]
        sg = _sigmoid(gc)
        dg = da * vc * sg * (1.0 + gc * (1.0 - sg))
        dv = da * gc * sg
        dc_ref[0] = dg
        dc_ref[1] = dv
        for half, (d, taps) in enumerate(((dg, (ugp, ug, ugn)), (dv, (uvp, uv, uvn)))):
            for tap in range(3):
                dcw_ref[half, tap:tap + 1, :] += jnp.sum(d * taps[tap], axis=0, keepdims=True)
            dcb_ref[half] += jnp.sum(d, axis=0, keepdims=True)

    main, prev, nxt = _halo_specs(tm, s, FF_TC)
    return _call(
        body, name="ffn_act_bwd", grid=(ff // FF_TC, n_i),
        in_specs=[pl.BlockSpec((tm, D_MODEL), lambda j, i: (i, 0)), pl.BlockSpec((FF_TC, D_MODEL), lambda j, i: (j, 0)),
                  main, prev, nxt, pl.BlockSpec((2, 3, FF_TC), lambda j, i: (0, 0, j)),
                  pl.BlockSpec((2, 1, FF_TC), lambda j, i: (0, 0, j))],
        out_specs=[main, pl.BlockSpec((2, 3, FF_TC), lambda j, i: (0, 0, j)),
                   pl.BlockSpec((2, 1, FF_TC), lambda j, i: (0, 0, j))],
        out_shape=[_sds((2, s, ff)), _sds((2, 3, ff)), _sds((2, 1, ff))],
        sem=("parallel", "arbitrary"),
    )(dy2, w_down, u, u, u, cw, cb)


def _conv_bwd_input(dc, cw, *, tm):
    _, s, ff = dc.shape
    n_i = s // tm

    def body(d_ref, dp_ref, dn_ref, cw_ref, du_ref):
        i = pl.program_id(1)
        for half in range(2):
            up, dn = _shift_rows(d_ref[half], *_halo_rows(dp_ref, dn_ref, half, i, n_i))
            w = cw_ref[half]
            du_ref[half] = (w[0:1, :] * dn + w[1:2, :] * d_ref[half] + w[2:3, :] * up).astype(BF16)

    main, prev, nxt = _halo_specs(tm, s, FF_TC)
    return _call(
        body, name="conv_bwd_input", grid=(ff // FF_TC, n_i),
        in_specs=[main, prev, nxt, pl.BlockSpec((2, 3, FF_TC), lambda j, i: (0, 0, j))],
        out_specs=main, out_shape=_sds((2, s, ff), BF16), sem=("parallel", "parallel"),
    )(dc, dc, dc, cw)


def _ffn_down_loss(a, w_down, x1, target, gate2, ln_g, ln_b, *, tm):
    s, ff = a.shape
    n_i = s // tm

    def body(a_ref, wd_ref, x1_ref, tg_ref, gt_ref, lg_ref, lb_ref, ls_ref, dy_ref, dx_ref, dg_ref, db_ref, dgt_ref):
        @pl.when(pl.program_id(0) == 0)
        def _():
            dg_ref[...] = jnp.zeros(dg_ref.shape, F32)
            db_ref[...] = jnp.zeros(db_ref.shape, F32)
            dgt_ref[...] = jnp.zeros(dgt_ref.shape, F32)

        y2 = _dot(a_ref[...], wd_ref[...])
        z = ALPHA * x1_ref[...] + gt_ref[...] * y2
        mu = jnp.mean(z, axis=-1, keepdims=True)
        zc = z - mu
        r = lax.rsqrt(jnp.mean(zc * zc, axis=-1, keepdims=True) + LN_EPS)
        xhat = zc * r
        diff = xhat * lg_ref[...] + lb_ref[...] - tg_ref[...]
        ls_ref[...] = jnp.full(ls_ref.shape, 0.5 / D_MODEL * jnp.sum(diff * diff), F32)
        dx2 = diff * (1.0 / D_MODEL)
        dg_ref[...] += jnp.sum(dx2 * xhat, axis=0, keepdims=True)
        db_ref[...] += jnp.sum(dx2, axis=0, keepdims=True)
        dz = _ln_bwd(dx2, xhat, r, lg_ref[...])
        dgt_ref[...] += jnp.sum(dz * y2, axis=0, keepdims=True)
        dy_ref[...] = (gt_ref[...] * dz).astype(BF16)
        dx_ref[...] = ALPHA * dz

    row = pl.BlockSpec((tm, D_MODEL), lambda i: (i, 0))
    vec = _full((1, D_MODEL))
    return _call(
        body, name="ffn_down_loss", grid=(n_i,),
        in_specs=[pl.BlockSpec((tm, ff), lambda i: (i, 0)), _full((ff, D_MODEL)), row, row, vec, vec, vec],
        out_specs=[pl.BlockSpec((None, 8, LANES), lambda i: (i, 0, 0)), row, row, vec, vec, vec],
        out_shape=[_sds((n_i, 8, LANES)), _sds((s, D_MODEL), BF16), _sds((s, D_MODEL)),
                   _sds((1, D_MODEL)), _sds((1, D_MODEL)), _sds((1, D_MODEL))],
        sem=("arbitrary",),
    )(a, w_down, x1, target, gate2, ln_g, ln_b)


def _ffn_up_bwd(du, wup4, dx1a, x1, scale2, x, y, mu1, r1, gate1, ln_g, *, tm):
    s = x.shape[0]
    nb, _, ns = wup4.shape

    def body(du_ref, w_ref, dxa_ref, x1_ref, sc_ref, x_ref, y_ref, mu_ref, r_ref, gt_ref, lg_ref,
             dxo_ref, dy_ref, dsc_ref, dsh_ref, dg_ref, db_ref, dgt_ref, acc):
        i, k = pl.program_id(0), pl.program_id(1)

        @pl.when(jnp.logical_and(i == 0, k == 0))
        def _():
            for ref in (dsc_ref, dsh_ref, dg_ref, db_ref, dgt_ref):
                ref[...] = jnp.zeros(ref.shape, F32)

        @pl.when(k == 0)
        def _():
            acc[...] = jnp.zeros(acc.shape, F32)

        acc[...] += _dot_nt(du_ref[...], w_ref[...])

        @pl.when(k == nb - 1)
        def _():
            dh = acc[...]
            x1 = x1_ref[...]
            dsc_ref[...] += jnp.sum(dh * x1, axis=0, keepdims=True)
            dsh_ref[...] += jnp.sum(dh, axis=0, keepdims=True)
            dx1 = dxa_ref[...] + dh * (1.0 + sc_ref[...])
            yv = y_ref[...]
            xhat = (ALPHA * x_ref[...] + gt_ref[...] * yv - mu_ref[...]) * r_ref[...]
            dg_ref[...] += jnp.sum(dx1 * xhat, axis=0, keepdims=True)
            db_ref[...] += jnp.sum(dx1, axis=0, keepdims=True)
            dz = _ln_bwd(dx1, xhat, r_ref[...], lg_ref[...])
            dgt_ref[...] += jnp.sum(dz * yv, axis=0, keepdims=True)
            dy_ref[...] = (gt_ref[...] * dz).astype(BF16)
            dxo_ref[...] = ALPHA * dz

    row = pl.BlockSpec((tm, D_MODEL), lambda i, k: (i, 0))
    col = pl.BlockSpec((tm, 1), lambda i, k: (i, 0))
    vec = _full((1, D_MODEL))
    return _call(
        body, name="ffn_up_bwd", grid=(s // tm, nb),
        in_specs=[pl.BlockSpec((None, tm, ns), lambda i, k: (k // 2, i, k % 2)),
                  pl.BlockSpec((None, D_MODEL, ns), lambda i, k: (k, 0, 0)),
                  row, row, vec, row, row, col, col, vec, vec],
        out_specs=[row, row, vec, vec, vec, vec, vec],
        out_shape=[_sds((s, D_MODEL)), _sds((s, D_MODEL), BF16)] + [_sds((1, D_MODEL))] * 5,
        scratch=[pltpu.VMEM((tm, D_MODEL), F32)],
        sem=("arbitrary", "arbitrary"),
    )(du, wup4, dx1a, x1, scale2, x, y, mu1, r1, gate1, ln_g)


def _mm_nt4_mod_bwd(dp, w4, dxa, x, scale, *, tm, name):
    m = x.shape[0]
    nb, kdim, ns = w4.shape

    def body(dp_ref, w_ref, dxa_ref, x_ref, sc_ref, dx_ref, dsc_ref, dsh_ref, acc):
        i, k = pl.program_id(0), pl.program_id(1)

        @pl.when(jnp.logical_and(i == 0, k == 0))
        def _():
            dsc_ref[...] = jnp.zeros(dsc_ref.shape, F32)
            dsh_ref[...] = jnp.zeros(dsh_ref.shape, F32)

        @pl.when(k == 0)
        def _():
            acc[...] = jnp.zeros(acc.shape, F32)

        acc[...] += _dot_nt(dp_ref[...], w_ref[...])

        @pl.when(k == nb - 1)
        def _():
            dh = acc[...]
            dsc_ref[...] += jnp.sum(dh * x_ref[...], axis=0, keepdims=True)
            dsh_ref[...] += jnp.sum(dh, axis=0, keepdims=True)
            dx_ref[...] = dxa_ref[...] + dh * (1.0 + sc_ref[...])

    row = pl.BlockSpec((tm, kdim), lambda i, k: (i, 0))
    vec = _full((1, kdim))
    return _call(
        body, name=name, grid=(m // tm, nb),
        in_specs=[pl.BlockSpec((tm, ns), lambda i, k: (i, k)), pl.BlockSpec((None, kdim, ns), lambda i, k: (k, 0, 0)),
                  row, row, vec],
        out_specs=[row, vec, vec],
        out_shape=[_sds((m, kdim)), _sds((1, kdim)), _sds((1, kdim))],
        scratch=[pltpu.VMEM((tm, kdim), F32)],
        sem=("arbitrary", "arbitrary"),
    )(dp, w4, dxa, x, scale)


def _pad_heads_w(w):
    w8 = w.reshape(N_HEADS, HEAD_DIM, w.shape[-1])
    z = jnp.zeros_like(w8)
    first = (jnp.arange(N_HEADS) < N_HEADS // N_KV)[:, None, None]
    return jnp.where(first, jnp.concatenate([w8, z], axis=1), jnp.concatenate([z, w8], axis=1))


def _unpad_heads_w(g):
    first = (jnp.arange(N_HEADS) < N_HEADS // N_KV)[:, None, None]
    return jnp.where(first, g[:, :HEAD_DIM], g[:, HEAD_DIM:]).reshape(N_HEADS * HEAD_DIM, g.shape[-1])


def _ones_beside(vt):
    half = vt.shape[0] // 2
    ones = jnp.ones((half, vt.shape[1]), vt.dtype)
    return jnp.stack([jnp.concatenate([vt[:half], ones], axis=0), jnp.concatenate([ones, vt[half:]], axis=0)])


def _rep8(a):
    return jnp.broadcast_to(a.reshape(1, -1), (8, a.size))


def _first_row(a):
    r8 = _rep8(a)
    return jnp.where(lax.broadcasted_iota(jnp.int32, r8.shape, 0) == 0, r8, 0.0)


def _to_blocks4(w):
    k, n = w.shape
    return w.reshape(k, N_CHIPS, n // N_CHIPS).transpose(1, 0, 2)


def _local_step(x, c, ctx, c_ctx, wmod4, b_mod, win4, b_in, sink, qn, kn, wba, wbb, w_out, ln1_g, ln1_b,
                wup4, cw, cb, w_down, ln2_g, ln2_b, target):
    s, nc = x.shape[0], ctx.shape[0]
    tm = min(512, s)
    tm2 = min(256, s)
    zvec = jnp.zeros((1, D_MODEL), F32)

    cc = jnp.concatenate([_rep8(c), _rep8(c_ctx)], axis=0)
    mods = _mm_nn4(cc, zvec, zvec, wmod4, b_mod, mode="silu", split_out=False, out_dtype=F32, tm=16, name="mod_vectors")
    shift1, scale1, gate1, shift2, scale2, gate2 = [mods[0:1, i * D_MODEL:(i + 1) * D_MODEL] for i in range(6)]
    shift_c, scale_c = mods[8:9, :D_MODEL], mods[8:9, D_MODEL:2 * D_MODEL]

    cos, sin = _rope_tables(s)
    cos_c, sin_c = jnp.ones((nc, LANES), F32), jnp.zeros((nc, LANES), F32)
    qg, kg = jnp.tile(qn, (1, 2)), jnp.tile(kn, (1, 2))

    proj_c = _mm_nn4(ctx, shift_c, scale_c, win4, b_in, mode="modulate", split_out=False, out_dtype=F32, tm=nc,
                     name="in_proj_ctx")
    _, kac, vac, _, kbc, vbc = _prep(proj_c, cos_c, sin_c, qg, kg, tm=nc, name="prep_ctx")
    proj = _mm_nn4(x, shift1, scale1, win4, b_in, mode="modulate", split_out=False, out_dtype=F32, tm=tm, name="in_proj")
    qa, ka, va, qb, kb, vb = _prep(proj, cos, sin, qg, kg, tm=tm, name="prep")
    oa, lse_a = _attn_win_fwd(qa, ka, va, kac, vac, sink, tq=tm)
    qbt = jnp.swapaxes(qb, 1, 2)
    obt, lse_b, mrun_b, pbt = _attn_glob_fwd(qbt, kb, _ones_beside(vb.T), kbc, _ones_beside(vbc.T), tq=tm,
                                             tk=min(1024, s))
    ob = jnp.swapaxes(obt, 1, 2)
    wba_p, wbb_p = _pad_heads_w(wba), _pad_heads_w(wbb)
    x1, y, mu1, r1 = _merge_fwd(oa, ob, proj, x, gate1, wba_p, wbb_p, w_out, ln1_g, ln1_b, tm=tm2)
    u = _mm_nn4(x1, shift2, scale2, wup4, jnp.zeros((1, 2 * D_FF), F32), mode="modulate", split_out=True,
                out_dtype=F32, tm=tm, name="ffn_up")
    cw2 = cw.reshape(3, 2, D_FF).transpose(1, 0, 2)
    cb2 = cb.reshape(2, 1, D_FF)
    a = _ffn_act_fwd(u, cw2, cb2, tm=tm)
    ls, dy2, dx1a, dln2_g, dln2_b, dgate2 = _ffn_down_loss(a, w_down, x1, target, gate2, ln2_g, ln2_b, tm=tm2)
    loss = jnp.sum(ls[:, 0, 0])

    n_s = s // tm
    dw_down = _mm_tn(a, dy2, a_spec=pl.BlockSpec((tm, D_FF), lambda t: (t, 0)),
                     b_spec=pl.BlockSpec((tm, D_MODEL), lambda t: (t, 0)), grid=(n_s,),
                     out_shape=_sds((D_FF, D_MODEL)), out_spec=_full((D_FF, D_MODEL)), name="dw_down")
    dc, dcw2, dcb2 = _ffn_act_bwd(dy2, w_down, u, cw2, cb2, tm=tm)
    du = _conv_bwd_input(dc, cw2, tm=tm)
    dxz1, dy, dscale2, dshift2, dln1_g, dln1_b, dgate1 = _ffn_up_bwd(
        du, wup4, dx1a, x1, scale2, x, y, mu1, r1, gate1, ln1_g, tm=tm2)
    ns_up = wup4.shape[-1]
    dw_up4 = _mm_tn(x1, du, a_spec=pl.BlockSpec((tm, D_MODEL), lambda k, t: (t, 0)),
                    b_spec=pl.BlockSpec((None, tm, ns_up), lambda k, t: (k // 2, t, k % 2)), grid=(N_CHIPS, n_s),
                    out_shape=_sds((N_CHIPS, D_MODEL, ns_up)),
                    out_spec=pl.BlockSpec((None, D_MODEL, ns_up), lambda k, t: (k, 0, 0)),
                    mod=(shift2, scale2), name="dw_up")

    dgl, doa, dobt, merged, dwba_p, dwbb_p = _merge_bwd(dy, oa, ob, jnp.swapaxes(oa, 1, 2), obt, proj, wba_p, wbb_p,
                                                        w_out, tm=tm2)
    dwba, dwbb = _unpad_heads_w(dwba_p), _unpad_heads_w(dwbb_p)
    rowspec = pl.BlockSpec((tm, D_MODEL), lambda t: (t, 0))
    dw_out = _mm_tn(merged, dy, a_spec=rowspec, b_spec=rowspec, grid=(n_s,), out_shape=_sds((D_MODEL, D_MODEL)),
                    out_spec=_full((D_MODEL, D_MODEL)), name="dw_out")

    dqa, dla, dkac, dvac, dsk = _attn_win_dq(qa, doa, oa, lse_a, ka, va, kac, vac, sink, tq=tm)
    dka, dva = _attn_win_dkv(qa, doa, lse_a.reshape(N_HEADS, 1, s), dla.reshape(N_HEADS, 1, s), ka, va, tk=tm)
    dqbt, dkbt, dvbt, dkbct, dvbct = _attn_glob_bwd(qbt, dobt, obt, lse_b, mrun_b, pbt, kb.T, vb, kbc.T, vbc, tq=tm, tk=tm)
    dqb, dkb, dvb, dkbc, dvbc = jnp.swapaxes(dqbt, 1, 2), dkbt.T, dvbt.T, dkbct.T, dvbct.T
    dsink = jnp.sum(dsk[:, :, 0, 0], axis=1)

    dproj, dqg, dkg = _prep_bwd(dqa, dka, dva, dqb, dkb, dvb, proj, cos, sin, qg, kg, dgl, tm=tm, name="prep_bwd")
    grad_x, dscale1, dshift1 = _mm_nt4_mod_bwd(dproj, win4, dxz1, x, scale1, tm=tm, name="in_proj_bwd")
    ns_in = win4.shape[-1]
    win_spec = dict(b_spec=pl.BlockSpec((None, None, ns_in), lambda k, t: (0, 0, k)),
                    out_shape=_sds((N_CHIPS, D_MODEL, ns_in)),
                    out_spec=pl.BlockSpec((None, D_MODEL, ns_in), lambda k, t: (k, 0, 0)),
                    colsum_spec=pl.BlockSpec((8, ns_in), lambda k, t: (0, k)), colsum_shape=_sds((8, IN_COLS)))
    win_spec["b_spec"] = pl.BlockSpec((tm, ns_in), lambda k, t: (t, k))
    dw_in4, db_in = _mm_tn(x, dproj, a_spec=pl.BlockSpec((tm, D_MODEL), lambda k, t: (t, 0)), grid=(N_CHIPS, n_s),
                           mod=(shift1, scale1), name="dw_in", **win_spec)

    zq = jnp.zeros((N_HEADS, nc, LANES), F32)
    dproj_c, _, dkg_c = _prep_bwd(zq, dkac, dvac, zq, dkbc, dvbc, proj_c, cos_c, sin_c, qg, kg,
                                  jnp.zeros((nc, IN_COLS - OFF_GA), BF16), tm=nc, name="prep_bwd_ctx")
    _, dscale_c, dshift_c = _mm_nt4_mod_bwd(dproj_c, win4, jnp.zeros((nc, D_MODEL), F32), ctx, scale_c, tm=nc,
                                            name="in_proj_bwd_ctx")
    win_spec["b_spec"] = pl.BlockSpec((nc, ns_in), lambda k, t: (t, k))
    dw_in4, db_in_c = _mm_tn(ctx, dproj_c, a_spec=pl.BlockSpec((nc, D_MODEL), lambda k, t: (t, 0)), grid=(N_CHIPS, 1),
                             mod=(shift_c, scale_c), init=dw_in4, name="dw_in_ctx", **win_spec)

    dmod = jnp.concatenate([dshift1, dscale1, dgate1, dshift2, dscale2, dgate2], axis=1)
    dmodc = jnp.concatenate([dshift_c, dscale_c], axis=1)
    dmodc_pad = jnp.concatenate([dmodc, jnp.zeros((1, 4 * D_MODEL), F32)], axis=1)
    dmodc8 = _first_row(dmodc_pad).astype(BF16)
    z8 = jnp.zeros((8, D_MODEL), F32)
    dsilu_c, _, _ = _mm_nt4_mod_bwd(dmodc8, wmod4, z8, z8, zvec, tm=8, name="c_ctx_bwd")
    sg = _sigmoid(c_ctx)
    dc_ctx = dsilu_c[0:1] * sg * (1.0 + c_ctx * (1.0 - sg))

    dqn = jnp.sum(dqg.reshape(N_HEADS, HEAD_DIM), axis=0, keepdims=True)
    dkn = jnp.sum((dkg + dkg_c).reshape(N_KV, HEAD_DIM), axis=0, keepdims=True)
    grads = dict(
        w_in4=dw_in4, b_in=db_in[0:1] + db_in_c[0:1], sink=dsink, qn=dqn, kn=dkn, wba=dwba, wbb=dwbb, w_out=dw_out,
        ln1_g=dln1_g, ln1_b=dln1_b, w_up4=dw_up4, conv_w=dcw2.transpose(1, 0, 2).reshape(3, 2 * D_FF),
        conv_b=dcb2.reshape(1, 2 * D_FF), w_down=dw_down, ln2_g=dln2_g, ln2_b=dln2_b,
        c_ctx=dc_ctx, dmod=dmod, dmodc=dmodc)
    return loss, grad_x, grads


ANY = pl.BlockSpec(memory_space=pl.ANY)


def _mesh_pos():
    return lax.axis_index("x"), lax.axis_index("y"), lax.axis_index("c")


def _other_chips(x, y):
    return [(1 - x, y), (x, 1 - y), (1 - x, 1 - y)]


def _remote(src, dst, send, recv, dev):
    return pltpu.make_async_remote_copy(src_ref=src, dst_ref=dst, send_sem=send, recv_sem=recv, device_id=dev,
                                        device_id_type=MESH)


def _set_block(stack, block, k):
    return lax.dynamic_update_slice(stack, block[None], (k,) + (0,) * block.ndim)


def _gather_shards(arrs, small):
    na = len(arrs)
    halves = [a.shape[0] // 2 for a in arrs]

    def body(*refs):
        ins, small_ref = refs[:na], refs[na]
        outs, small_out = refs[na + 1:2 * na + 1], refs[2 * na + 1]
        send, recv = refs[2 * na + 2:]
        x, y, c = _mesh_pos()
        me = 2 * x + y
        chips = _other_chips(x, y)

        def half(a, cc):
            return pl.ds(cc * halves[a], halves[a])

        sends = []
        for j, chip in enumerate(chips):
            for a in range(na):
                sends.append(_remote(ins[a].at[half(a, c)], outs[a].at[me, half(a, c)], send.at[a, j], recv.at[a, j],
                                     (*chip, c)))
            sends.append(_remote(small_ref, small_out.at[me], send.at[na, j], recv.at[na, j], (*chip, c)))
        for cp in sends:
            cp.start()
        for j, chip in enumerate(chips):
            kj = 2 * chip[0] + chip[1]
            for a in range(na):
                landed = outs[a].at[kj, half(a, c)]
                _remote(landed, landed, send.at[a, j], recv.at[a, j], (*chip, c)).wait_recv()
                fwd = _remote(landed, landed, send.at[a, 3 + j], recv.at[a, 3 + j], (x, y, 1 - c))
                fwd.start()
                sends.append(fwd)
            _remote(small_ref, small_out.at[kj], send.at[na, j], recv.at[na, j], (*chip, c)).wait_recv()
        for j, chip in enumerate(chips):
            kj = 2 * chip[0] + chip[1]
            for a in range(na):
                other = outs[a].at[kj, half(a, 1 - c)]
                _remote(other, other, send.at[a, 3 + j], recv.at[a, 3 + j], (x, y, 1 - c)).wait_recv()
        for cp in sends:
            cp.wait_send()

    out_shape = [_sds((N_CHIPS,) + a.shape, a.dtype) for a in arrs] + [_sds((N_CHIPS,) + small.shape, small.dtype)]
    got = pl.pallas_call(
        body, name="gather_shards", in_specs=[ANY] * (na + 1), out_specs=[ANY] * (na + 1), out_shape=out_shape,
        scratch_shapes=[pltpu.SemaphoreType.DMA((na + 1, 6)), pltpu.SemaphoreType.DMA((na + 1, 6))],
    )(*arrs, small)
    xp, yp, _ = _mesh_pos()
    return [_set_block(g, a, 2 * xp + yp) for g, a in zip(got, list(arrs) + [small])]


def _allgather_rows(v):
    r, n = v.shape

    def body(v_ref, out_ref, send, recv, loc):
        x, y, c = _mesh_pos()
        me, sibling = (x, y, c), (x, y, 1 - c)
        chips = _other_chips(x, y)

        def rows(px, py, pc):
            return out_ref.at[4 * px + 2 * py + pc]

        def copy(k, block, to, src=None):
            return _remote(rows(*block) if src is None else src, rows(*block), send.at[k], recv.at[k], to)

        mine = pltpu.make_async_copy(v_ref, rows(*me), loc)
        mine.start()
        first = [copy(0, me, sibling, src=v_ref)] + [copy(1 + j, me, (*chip, c), src=v_ref) for j, chip in enumerate(chips)]
        for cp in first:
            cp.start()
        passed = [copy(4 + j, (*chip, c), sibling) for j, chip in enumerate(chips)]
        for j, chip in enumerate(chips):
            copy(1 + j, (*chip, c), me).wait_recv()
            passed[j].start()
        copy(0, sibling, me).wait_recv()
        for j, chip in enumerate(chips):
            copy(4 + j, (*chip, 1 - c), me).wait_recv()
        for cp in first + passed:
            cp.wait_send()
        mine.wait()

    return pl.pallas_call(
        body, name="allgather_rows", in_specs=[pl.BlockSpec(memory_space=pltpu.VMEM)],
        out_specs=pl.BlockSpec(memory_space=pltpu.VMEM), out_shape=_sds((N_DEV, r, n), v.dtype),
        scratch_shapes=[pltpu.SemaphoreType.DMA((7,)), pltpu.SemaphoreType.DMA((7,)), pltpu.SemaphoreType.DMA],
    )(v)


def _swap_other_half(g):
    nb, r, n = g.shape
    rh = r // 2

    def body(g_ref, out_ref, send, recv):
        x, y, c = _mesh_pos()
        cp = _remote(g_ref.at[:, pl.ds((1 - c) * rh, rh), :], out_ref, send, recv, (x, y, 1 - c))
        cp.start()
        cp.wait()

    return pl.pallas_call(
        body, name="swap_other_half", in_specs=[ANY], out_specs=ANY, out_shape=_sds((nb, rh, n), g.dtype),
        scratch_shapes=[pltpu.SemaphoreType.DMA, pltpu.SemaphoreType.DMA],
    )(g)


def _scatter_to_chips(p):
    def body(p_ref, out_ref, send, recv):
        x, y, c = _mesh_pos()
        me = 2 * x + y
        chips = _other_chips(x, y)
        sends = [_remote(p_ref.at[2 * chip[0] + chip[1]], out_ref.at[me], send.at[j], recv.at[j], (*chip, c))
                 for j, chip in enumerate(chips)]
        for cp in sends:
            cp.start()
        for j, chip in enumerate(chips):
            kj = 2 * chip[0] + chip[1]
            _remote(p_ref.at[kj], out_ref.at[kj], send.at[j], recv.at[j], (*chip, c)).wait_recv()
        for cp in sends:
            cp.wait_send()

    got = pl.pallas_call(
        body, name="scatter_to_chips", in_specs=[ANY], out_specs=ANY, out_shape=_sds(p.shape, p.dtype),
        scratch_shapes=[pltpu.SemaphoreType.DMA((3,)), pltpu.SemaphoreType.DMA((3,))],
    )(p)
    xp, yp, _ = _mesh_pos()
    me = 2 * xp + yp
    return _set_block(got, lax.dynamic_index_in_dim(p, me, axis=0, keepdims=False), me)


def _join_halves(f):
    def body(f_ref, out_ref, send, recv):
        x, y, c = _mesh_pos()
        cp = _remote(f_ref, out_ref, send, recv, (x, y, 1 - c))
        cp.start()
        cp.wait()

    other = pl.pallas_call(
        body, name="join_halves", in_specs=[ANY], out_specs=ANY, out_shape=_sds(f.shape, f.dtype),
        scratch_shapes=[pltpu.SemaphoreType.DMA, pltpu.SemaphoreType.DMA],
    )(f)
    first = lax.axis_index("c") == 0
    return jnp.concatenate([jnp.where(first, f, other), jnp.where(first, other, f)], axis=0)


def _row_tile(rows, cap=512):
    t = cap - cap % 8
    while rows % t:
        t -= 8
    return t


def _add_blocks(a, b, out_dtype):
    nb, r, n = a.shape
    tr = _row_tile(r)

    def body(a_ref, b_ref, o_ref):
        o_ref[...] = (a_ref[...] + b_ref[...]).astype(out_dtype)

    spec = pl.BlockSpec((None, tr, n), lambda k, i: (k, i, 0))
    return _call(body, name="add_blocks", grid=(nb, r // tr), in_specs=[spec, spec], out_specs=spec,
                 out_shape=_sds(a.shape, out_dtype), sem=("parallel", "parallel"))(a, b)


def _sum_leading(a, *, name):
    nk, r, n = a.shape
    tr = _row_tile(r)

    def body(a_ref, o_ref):
        acc = a_ref[0].astype(F32)
        for k in range(1, nk):
            acc = acc + a_ref[k].astype(F32)
        o_ref[...] = acc

    return _call(body, name=name, grid=(r // tr,), in_specs=[pl.BlockSpec((nk, tr, n), lambda i: (0, i, 0))],
                 out_specs=pl.BlockSpec((tr, n), lambda i: (i, 0)), out_shape=_sds((r, n)), sem=("parallel",))(a)


def _silu_outer(a, b):
    kdim, n = a.shape[1], b.shape[1]

    def body(a_ref, b_ref, o_ref):
        av = a_ref[...]
        av = av * _sigmoid(av)
        bv = b_ref[...]
        ah, bh = av.astype(BF16), bv.astype(BF16)
        al, bl = (av - ah.astype(F32)).astype(BF16), (bv - bh.astype(F32)).astype(BF16)
        o_ref[...] = _dot_tn(ah, bh) + (_dot_tn(ah, bl) + _dot_tn(al, bh))

    return _call(body, name="dw_mod", grid=(1,), in_specs=[_full(a.shape), _full(b.shape)], out_specs=_full((kdim, n)),
                 out_shape=_sds((kdim, n)))(a, b)


def _adamw(w, g, m, v):
    r, n = w.shape
    tr = _row_tile(r)

    def body(w_ref, g_ref, m_ref, v_ref, d_ref, nm_ref, nv_ref):
        gv = g_ref[...]
        nm = ADAM_B1 * m_ref[...] + (1.0 - ADAM_B1) * gv
        nv = ADAM_B2 * v_ref[...] + (1.0 - ADAM_B2) * (gv * gv)
        m_hat = nm / (1.0 - ADAM_B1 ** ADAM_STEP)
        v_hat = nv / (1.0 - ADAM_B2 ** ADAM_STEP)
        d_ref[...] = -ADAM_LR * (m_hat / (jnp.sqrt(v_hat) + ADAM_EPS) + ADAM_WD * w_ref[...])
        nm_ref[...] = nm
        nv_ref[...] = nv

    spec = pl.BlockSpec((tr, n), lambda i: (i, 0))
    return _call(body, name="adamw", grid=(r // tr,), in_specs=[spec] * 4, out_specs=[spec] * 3,
                 out_shape=[_sds((r, n))] * 3, sem=("parallel",))(w, g, m, v)


BIG = ("w_in", "w_branch_a", "w_branch_b", "w_out", "w_up", "w_down", "conv_w")
BIG_ROWS = 3584
SMALL = ("b_mod", "b_in", "conv_b", "ln1_g", "ln1_b", "ln2_g", "ln2_b", "c_ctx", "attn_sink", "q_norm_g", "k_norm_g")
SMALL_ROWS = 8 * len(SMALL)


def _rows(a, n_rows):
    flat = a.reshape(-1)
    return jnp.pad(flat, (0, n_rows * D_MODEL - flat.shape[0])).reshape(n_rows, D_MODEL)


def _group8(a):
    return _rep8(_rows(a, 1)) if a.size <= D_MODEL else _rows(a, 8)


def _ungroup8(p, shape):
    size = math.prod(shape)
    return (p[0, :size] if size <= D_MODEL else p.reshape(-1)[:size]).reshape(shape)


def _pack_big(t):
    parts = [t[n].reshape(-1, D_MODEL) for n in BIG[:-1]] + [_rows(t["conv_w"], 8)]
    used = sum(p.shape[0] for p in parts)
    return jnp.concatenate(parts + [jnp.zeros((BIG_ROWS - used, D_MODEL), F32)], axis=0)


def _unpack_big(p, like):
    out, r = {}, 0
    for n in BIG:
        size = math.prod(like[n].shape)
        nr = size // D_MODEL if n != "conv_w" else 8
        out[n] = p[r:r + nr].reshape(-1)[:size].reshape(like[n].shape)
        r += nr
    return out


def _pack_small(t):
    return jnp.concatenate([_group8(t[n]) for n in SMALL], axis=0)


def _unpack_small(p, like):
    return {n: _ungroup8(p[8 * i:8 * i + 8], like[n].shape) for i, n in enumerate(SMALL)}


WEIGHTS = ("c_ctx", "w_mod", "b_mod", "w_in", "b_in", "attn_sink", "q_norm_g", "k_norm_g", "w_branch_a", "w_branch_b",
           "w_out", "ln1_g", "ln1_b", "w_up", "conv_w", "conv_b", "w_down", "ln2_g", "ln2_b")


def kernel(x, c, ctx, c_ctx, w_mod, b_mod, w_in, b_in, attn_sink, q_norm_g, k_norm_g, w_branch_a, w_branch_b, w_out, ln1_g, ln1_b, w_up, conv_w, conv_b, w_down, ln2_g, ln2_b, loss_target, m_c_ctx, m_w_mod, m_b_mod, m_w_in, m_b_in, m_attn_sink, m_q_norm_g, m_k_norm_g, m_w_branch_a, m_w_branch_b, m_w_out, m_ln1_g, m_ln1_b, m_w_up, m_conv_w, m_conv_b, m_w_down, m_ln2_g, m_ln2_b, v_c_ctx, v_w_mod, v_b_mod, v_w_in, v_b_in, v_attn_sink, v_q_norm_g, v_k_norm_g, v_w_branch_a, v_w_branch_b, v_w_out, v_ln1_g, v_ln1_b, v_w_up, v_conv_w, v_conv_b, v_w_down, v_ln2_g, v_ln2_b):
    w = dict(c_ctx=c_ctx, w_mod=w_mod, b_mod=b_mod, w_in=w_in, b_in=b_in, attn_sink=attn_sink, q_norm_g=q_norm_g,
             k_norm_g=k_norm_g, w_branch_a=w_branch_a, w_branch_b=w_branch_b, w_out=w_out, ln1_g=ln1_g, ln1_b=ln1_b,
             w_up=w_up, conv_w=conv_w, conv_b=conv_b, w_down=w_down, ln2_g=ln2_g, ln2_b=ln2_b)
    m = dict(c_ctx=m_c_ctx, w_mod=m_w_mod, b_mod=m_b_mod, w_in=m_w_in, b_in=m_b_in, attn_sink=m_attn_sink,
             q_norm_g=m_q_norm_g, k_norm_g=m_k_norm_g, w_branch_a=m_w_branch_a, w_branch_b=m_w_branch_b, w_out=m_w_out,
             ln1_g=m_ln1_g, ln1_b=m_ln1_b, w_up=m_w_up, conv_w=m_conv_w, conv_b=m_conv_b, w_down=m_w_down,
             ln2_g=m_ln2_g, ln2_b=m_ln2_b)
    v = dict(c_ctx=v_c_ctx, w_mod=v_w_mod, b_mod=v_b_mod, w_in=v_w_in, b_in=v_b_in, attn_sink=v_attn_sink,
             q_norm_g=v_q_norm_g, k_norm_g=v_k_norm_g, w_branch_a=v_w_branch_a, w_branch_b=v_w_branch_b, w_out=v_w_out,
             ln1_g=v_ln1_g, ln1_b=v_ln1_b, w_up=v_w_up, conv_w=v_conv_w, conv_b=v_conv_b, w_down=v_w_down,
             ln2_g=v_ln2_g, ln2_b=v_ln2_b)
    xp, yp, _ = _mesh_pos()
    me = 2 * xp + yp

    branches = jnp.concatenate([w_branch_a[0], w_branch_b[0]], axis=0)
    wide = jnp.concatenate([w_mod[0], w_in[0], w_up[0], branches], axis=1).astype(BF16)
    tall = jnp.concatenate([w_out[0], w_down[0]], axis=0).astype(BF16)
    wide4, tall4, cw4 = _gather_shards([wide, tall], conv_w[0])
    n_mod, n_in, n_up = w_mod.shape[-1], w_in.shape[-1], w_up.shape[-1]
    wmod4 = wide4[:, :, :n_mod]
    win4 = wide4[:, :, n_mod:n_mod + n_in]
    wup4 = wide4[:, :, n_mod + n_in:n_mod + n_in + n_up]
    br4 = wide4[:, :, n_mod + n_in + n_up:]
    n_br = w_branch_a.shape[1]
    wba = br4[:, :n_br].transpose(1, 0, 2).reshape(n_br, D_MODEL)
    wbb = br4[:, n_br:].transpose(1, 0, 2).reshape(n_br, D_MODEL)
    n_out = w_out.shape[1]
    w_out_full = tall4[:, :n_out].reshape(D_MODEL, D_MODEL)
    w_down_full = tall4[:, n_out:].reshape(D_FF, D_MODEL)
    cw_full = cw4.transpose(1, 0, 2).reshape(3, 2 * D_FF)

    loss, grad_x, g = _local_step(
        x[0], c, ctx[0], c_ctx[None], wmod4, b_mod, win4, b_in, attn_sink[0], q_norm_g, k_norm_g, wba, wbb, w_out_full,
        ln1_g, ln1_b, wup4, cw_full, conv_b, w_down_full, ln2_g, ln2_b, loss_target[0])
    loss = lax.psum(loss, ("x", "y", "c"))

    sent = dict(c=c, dmod=g["dmod"], dmodc=g["dmodc"], b_in=g["b_in"], conv_b=g["conv_b"], ln1_g=g["ln1_g"],
                ln1_b=g["ln1_b"], ln2_g=g["ln2_g"], ln2_b=g["ln2_b"], c_ctx=g["c_ctx"], attn_sink=g["sink"],
                q_norm_g=g["qn"], k_norm_g=g["kn"])
    every = _allgather_rows(jnp.concatenate([_group8(a) for a in sent.values()], axis=0))
    total = _sum_leading(every, name="sum_devices")
    slot = {n: slice(8 * i, 8 * i + 8) for i, n in enumerate(sent)}
    gs = {n: _ungroup8(total[slot[n]], sent[n].shape) for n in SMALL if n in sent}
    dmodc_sum = jnp.concatenate([_ungroup8(total[slot["dmodc"]], (1, 2 * D_MODEL)), jnp.zeros((1, 4 * D_MODEL), F32)],
                                axis=1)
    gs["b_mod"] = _ungroup8(total[slot["dmod"]], b_mod.shape) + dmodc_sum
    acts = jnp.concatenate([every[:, slot["c"].start], _rep8(c_ctx)], axis=0)
    dmods = jnp.concatenate([every[:, slot["dmod"]].reshape(N_DEV, -1)[:, :6 * D_MODEL], _first_row(dmodc_sum)], axis=0)
    g_w_mod = _silu_outer(acts, lax.dynamic_slice_in_dim(dmods, me * n_mod, n_mod, axis=1))

    cw_g4 = _to_blocks4(g["conv_w"])
    packed = jnp.concatenate([
        g["w_in4"].reshape(N_CHIPS, -1, D_MODEL), _to_blocks4(g["wba"]).reshape(N_CHIPS, -1, D_MODEL),
        _to_blocks4(g["wbb"]).reshape(N_CHIPS, -1, D_MODEL), g["w_out"].reshape(N_CHIPS, -1, D_MODEL),
        g["w_up4"].reshape(N_CHIPS, -1, D_MODEL), g["w_down"].reshape(N_CHIPS, -1, D_MODEL),
        jnp.pad(cw_g4.reshape(N_CHIPS, -1), ((0, 0), (0, 8 * D_MODEL - cw_g4.shape[1] * cw_g4.shape[2]))).reshape(
            N_CHIPS, 8, D_MODEL),
        jnp.zeros((N_CHIPS, BIG_ROWS - 3528, D_MODEL), F32)], axis=1)
    rh = BIG_ROWS // 2
    cpos = lax.axis_index("c")
    my_half = lax.dynamic_slice_in_dim(packed, cpos * rh, rh, axis=1)
    chip_sum = _add_blocks(my_half, _swap_other_half(packed), BF16)
    half_sum = _sum_leading(_scatter_to_chips(chip_sum), name="sum_chips")
    g_big = _unpack_big(_join_halves(half_sum), w)

    grads = dict(gs, w_mod=g_w_mod, **g_big)

    def pack_all(t):
        rows = jnp.concatenate([_pack_big(t), t["w_mod"].reshape(-1, D_MODEL), _pack_small(t)], axis=0)
        return jnp.pad(rows, ((0, -rows.shape[0] % 256), (0, 0)))

    delta_p, new_m_p, new_v_p = _adamw(pack_all(w), pack_all(grads), pack_all(m), pack_all(v))

    def unpack_all(p):
        r_mod = BIG_ROWS + w_mod.size // D_MODEL
        out = _unpack_big(p[:BIG_ROWS], w)
        out["w_mod"] = p[BIG_ROWS:r_mod].reshape(w_mod.shape)
        out.update(_unpack_small(p[r_mod:r_mod + SMALL_ROWS], w))
        return out

    grads = {n: grads[n].reshape(w[n].shape) for n in WEIGHTS}
    delta, new_m, new_v = unpack_all(delta_p), unpack_all(new_m_p), unpack_all(new_v_p)
    return (loss, grad_x[None], *[grads[n] for n in WEIGHTS], *[delta[n] for n in WEIGHTS],
            *[new_m[n] for n in WEIGHTS], *[new_v[n] for n in WEIGHTS])
```

```python
import functools
import math

import jax
import jax.numpy as jnp
from jax import lax
from jax.experimental import pallas as pl
from jax.experimental.pallas import tpu as pltpu

F32 = jnp.float32
BF16 = jnp.bfloat16

D_MODEL = 1024
HEAD_DIM = 64
N_HEADS = 8
N_KV = 2
WINDOW = 128
GRID_W = 64
ROPE_THETA = 10000.0
D_FF = 2816
LN_EPS = 1e-5
QK_EPS = 1e-6
ALPHA = 2.0 ** 0.25
Q_SCALE = HEAD_DIM ** -0.5
OFF_GA = 1536
IN_COLS = 3584
ADAM_LR, ADAM_B1, ADAM_B2, ADAM_EPS, ADAM_WD, ADAM_STEP = 0.001, 0.9, 0.999, 1e-8, 0.01, 10

LANES = 128
VMEM_BUDGET = 52 * 1024 * 1024
N_CHIPS = 4
N_DEV = 8
NEG = -1e30
MESH = pl.DeviceIdType.MESH


def _sigmoid(x):
    return 1.0 / (1.0 + jnp.exp(-x))


def _dot(a, b):
    return jnp.dot(a, b, preferred_element_type=F32)


def _dot_nt(a, b):
    return lax.dot_general(a, b, (((1,), (1,)), ((), ())), preferred_element_type=F32)


def _dot_tn(a, b):
    return lax.dot_general(a, b, (((0,), (0,)), ((), ())), preferred_element_type=F32)


def _call(body, *, name, grid, in_specs, out_specs, out_shape, scratch=(), sem=None, **kw):
    params = dict(vmem_limit_bytes=VMEM_BUDGET)
    if sem is not None:
        params["dimension_semantics"] = sem
    return pl.pallas_call(body, name=name, grid=grid, in_specs=in_specs, out_specs=out_specs,
                          out_shape=out_shape, scratch_shapes=list(scratch),
                          compiler_params=pltpu.CompilerParams(**params), **kw)


def _full(shape):
    n = len(shape)
    return pl.BlockSpec(shape, lambda *_: (0,) * n)


def _sds(shape, dtype=F32):
    return jax.ShapeDtypeStruct(shape, dtype)


def _mm_nn4(a, shift, scale, w4, bias, *, mode, split_out, out_dtype, tm, name):
    m, kdim = a.shape
    nb, _, ns = w4.shape

    def body(a_ref, sh_ref, sc_ref, w_ref, b_ref, o_ref):
        av = a_ref[...]
        if mode == "modulate":
            av = av * (1.0 + sc_ref[...]) + sh_ref[...]
        else:
            av = av * _sigmoid(av)
        o_ref[...] = (_dot(av.astype(BF16), w_ref[...]) + b_ref[...]).astype(out_dtype)

    if split_out:
        out_shape = _sds((2, m, 2 * ns), out_dtype)
        out_spec = pl.BlockSpec((None, tm, ns), lambda i, k: (k // 2, i, k % 2))
    else:
        out_shape = _sds((m, nb * ns), out_dtype)
        out_spec = pl.BlockSpec((tm, ns), lambda i, k: (i, k))
    return _call(
        body, name=name, grid=(m // tm, nb),
        in_specs=[pl.BlockSpec((tm, kdim), lambda i, k: (i, 0)),
                  pl.BlockSpec((1, kdim), lambda i, k: (0, 0)),
                  pl.BlockSpec((1, kdim), lambda i, k: (0, 0)),
                  pl.BlockSpec((None, kdim, ns), lambda i, k: (k, 0, 0)),
                  pl.BlockSpec((1, ns), lambda i, k: (0, k))],
        out_specs=out_spec, out_shape=out_shape, sem=("parallel", "arbitrary"),
    )(a, shift, scale, w4, bias)


def _mm_tn(a, b, *, a_spec, b_spec, grid, out_shape, out_spec, name, mod=None, init=None, colsum_spec=None,
           colsum_shape=None, a_is_t=False):
    red = len(grid) - 1
    has_mod, has_init, has_cs = mod is not None, init is not None, colsum_spec is not None

    def body(*refs):
        refs = list(refs)
        a_ref, b_ref = refs[0], refs[1]
        pos = 2
        if has_mod:
            sh_ref, sc_ref = refs[2], refs[3]
            pos = 4
        if has_init:
            init_ref = refs[pos]
            pos += 1
        o_ref = refs[pos]
        cs_ref = refs[pos + 1] if has_cs else None
        s = pl.program_id(red)

        @pl.when(s == 0)
        def _():
            o_ref[...] = init_ref[...] if has_init else jnp.zeros(o_ref.shape, F32)
            if has_cs:
                cs_ref[...] = jnp.zeros(cs_ref.shape, F32)

        av = a_ref[...]
        if has_mod:
            av = av * (1.0 + sc_ref[...]) + sh_ref[...]
        bv = b_ref[...]
        o_ref[...] += (_dot if a_is_t else _dot_tn)(av.astype(BF16), bv)
        if has_cs:
            cs_ref[...] += jnp.broadcast_to(jnp.sum(bv.astype(F32), axis=0, keepdims=True), cs_ref.shape)

    ins, in_specs = [a, b], [a_spec, b_spec]
    if has_mod:
        kdim = mod[0].shape[-1]
        ins += list(mod)
        in_specs += [_full((1, kdim)), _full((1, kdim))]
    if has_init:
        ins.append(init)
        in_specs.append(out_spec)
    out_specs, out_shapes = out_spec, out_shape
    if has_cs:
        out_specs, out_shapes = [out_spec, colsum_spec], [out_shape, colsum_shape]
    sem = ("parallel",) * red + ("arbitrary",)
    return _call(body, name=name, grid=grid, in_specs=in_specs, out_specs=out_specs, out_shape=out_shapes,
                 sem=sem)(*ins)


def _rope_tables(n_tok):
    pos = jnp.arange(n_tok, dtype=jnp.int32)
    rows = (pos // GRID_W).astype(F32)
    cols = (pos % GRID_W).astype(F32)
    n_freq = HEAD_DIM // 4
    inv_freq = ROPE_THETA ** (-jnp.arange(n_freq, dtype=F32) / n_freq)
    ang_r = rows[:, None] * inv_freq
    ang_c = cols[:, None] * inv_freq
    cos = jnp.concatenate([jnp.cos(ang_r)] * 2 + [jnp.cos(ang_c)] * 2, axis=-1)
    sin = jnp.concatenate([-jnp.sin(ang_r), jnp.sin(ang_r), -jnp.sin(ang_c), jnp.sin(ang_c)], axis=-1)
    return jnp.tile(cos, (1, 2)), jnp.tile(sin, (1, 2))


def _lane(shape):
    return lax.broadcasted_iota(jnp.int32, shape, 1)


def _rope_partner(t, lane):
    return jnp.where((lane % 32) < 16, pltpu.roll(t, LANES - 16, 1), pltpu.roll(t, 16, 1))


def _half_mean(s, lane):
    lo = jnp.sum(jnp.where(lane < HEAD_DIM, s, 0.0), axis=-1, keepdims=True)
    hi = jnp.sum(jnp.where(lane < HEAD_DIM, 0.0, s), axis=-1, keepdims=True)
    return jnp.where(lane < HEAD_DIM, lo, hi) * (1.0 / HEAD_DIM)


def _prep(proj, cos, sin, qg, kg, *, tm, name):
    m = proj.shape[0]

    def body(p_ref, cos_ref, sin_ref, qg_ref, kg_ref, qa_ref, ka_ref, va_ref, qb_ref, kb_ref, vb_ref):
        lane = _lane((tm, LANES))
        cosv, sinv = cos_ref[...], sin_ref[...]
        low = lane < HEAD_DIM

        def rope(t):
            return t * cosv + _rope_partner(t, lane) * sinv

        def rms(t, g):
            return t * lax.rsqrt(_half_mean(t * t, lane) + QK_EPS) * g

        def place(q_ref, j, chunk):
            sw = pltpu.roll(chunk, HEAD_DIM, 1)
            if j < 2:
                h0, h1 = jnp.where(low, chunk, 0.0), jnp.where(low, sw, 0.0)
            else:
                h0, h1 = jnp.where(low, 0.0, sw), jnp.where(low, 0.0, chunk)
            q_ref[2 * j] = h0.astype(BF16)
            q_ref[2 * j + 1] = h1.astype(BF16)

        for j in range(4):
            place(qa_ref, j, rope(p_ref[:, j * LANES:(j + 1) * LANES]) * Q_SCALE)
            place(qb_ref, j, rope(rms(p_ref[:, 768 + j * LANES:768 + (j + 1) * LANES], qg_ref[...])) * Q_SCALE)
        ka_ref[...] = rope(p_ref[:, 512:640]).astype(BF16)
        va_ref[...] = p_ref[:, 640:768].astype(BF16)
        kb_ref[...] = rope(rms(p_ref[:, 1280:1408], kg_ref[...])).astype(BF16)
        vb_ref[...] = p_ref[:, 1408:1536].astype(BF16)

    row = pl.BlockSpec((tm, LANES), lambda i: (i, 0))
    qspec = pl.BlockSpec((N_HEADS, tm, LANES), lambda i: (0, i, 0))
    return _call(
        body, name=name, grid=(m // tm,),
        in_specs=[pl.BlockSpec((tm, OFF_GA), lambda i: (i, 0)), row, row, _full((1, LANES)), _full((1, LANES))],
        out_specs=[qspec, row, row, qspec, row, row],
        out_shape=[_sds((N_HEADS, m, LANES), BF16), _sds((m, LANES), BF16), _sds((m, LANES), BF16),
                   _sds((N_HEADS, m, LANES), BF16), _sds((m, LANES), BF16), _sds((m, LANES), BF16)],
        sem=("parallel",),
    )(proj, cos, sin, qg, kg)


def _prep_bwd(dqa, dka, dva, dqb, dkb, dvb, proj, cos, sin, qg, kg, dgl, *, tm, name):
    m = proj.shape[0]

    def body(dqa_ref, dka_ref, dva_ref, dqb_ref, dkb_ref, dvb_ref, p_ref, cos_ref, sin_ref, qg_ref, kg_ref,
             dgl_ref, dp_ref, dqg_ref, dkg_ref):
        i = pl.program_id(0)
        lane = _lane((tm, LANES))
        cosv, sinv = cos_ref[...], sin_ref[...]
        low = lane < HEAD_DIM

        @pl.when(i == 0)
        def _():
            dqg_ref[...] = jnp.zeros(dqg_ref.shape, F32)
            dkg_ref[...] = jnp.zeros(dkg_ref.shape, F32)

        def unrope(d):
            return d * cosv - _rope_partner(d, lane) * sinv

        def unplace(dq_ref, j):
            d0, d1 = dq_ref[2 * j], dq_ref[2 * j + 1]
            if j < 2:
                return jnp.where(low, d0, pltpu.roll(d1, HEAD_DIM, 1))
            return jnp.where(low, pltpu.roll(d0, HEAD_DIM, 1), d1)

        def unrms(dtn, t, g):
            r = lax.rsqrt(_half_mean(t * t, lane) + QK_EPS)
            u = dtn * g
            dt = r * u - t * (r * r * r) * _half_mean(u * t, lane)
            return dt, jnp.sum(dtn * t * r, axis=0, keepdims=True)

        for j in range(4):
            dp_ref[:, j * LANES:(j + 1) * LANES] = (unrope(unplace(dqa_ref, j)) * Q_SCALE).astype(BF16)
            c0 = 768 + j * LANES
            dt, dg = unrms(unrope(unplace(dqb_ref, j)) * Q_SCALE, p_ref[:, c0:c0 + LANES], qg_ref[...])
            dp_ref[:, c0:c0 + LANES] = dt.astype(BF16)
            dqg_ref[:, j * LANES:(j + 1) * LANES] += dg
        dp_ref[:, 512:640] = unrope(dka_ref[...]).astype(BF16)
        dp_ref[:, 640:768] = dva_ref[...].astype(BF16)
        dt, dg = unrms(unrope(dkb_ref[...]), p_ref[:, 1280:1408], kg_ref[...])
        dp_ref[:, 1280:1408] = dt.astype(BF16)
        dkg_ref[...] += dg
        dp_ref[:, 1408:1536] = dvb_ref[...].astype(BF16)
        dp_ref[:, OFF_GA:] = dgl_ref[...]

    row = pl.BlockSpec((tm, LANES), lambda i: (i, 0))
    qspec = pl.BlockSpec((N_HEADS, tm, LANES), lambda i: (0, i, 0))
    return _call(
        body, name=name, grid=(m // tm,),
        in_specs=[qspec, row, row, qspec, row, row, pl.BlockSpec((tm, OFF_GA), lambda i: (i, 0)), row, row,
                  _full((1, LANES)), _full((1, LANES)), pl.BlockSpec((tm, IN_COLS - OFF_GA), lambda i: (i, 0))],
        out_specs=[pl.BlockSpec((tm, IN_COLS), lambda i: (i, 0)), _full((1, 512)), _full((1, LANES))],
        out_shape=[_sds((m, IN_COLS), BF16), _sds((1, 512)), _sds((1, LANES))],
        sem=("arbitrary",),
    )(dqa, dka, dva, dqb, dkb, dvb, proj, cos, sin, qg, kg, dgl)


def _attn_glob_fwd(qt, k, vt, kc, vct, *, tq, tk):
    nh, _, s = qt.shape
    nc = kc.shape[0]
    n_chunks = s // tk
    half = LANES // 2

    def body(qt_ref, k_ref, vt_ref, kc_ref, vct_ref, ot_ref, lse_ref, mrun_ref, p_hbm,
             acc_sc, st_sc, stage_sc, stagec_sc, sems, semc):
        h, i = pl.program_id(0), pl.program_id(1)
        qtv = qt_ref[...]
        acc_sc[...] = jnp.zeros(acc_sc.shape, F32)

        def p_out(slot, c):
            return pltpu.make_async_copy(stage_sc.at[slot], p_hbm.at[h, i, pl.ds(pl.multiple_of(c * tk, tk), tk), :],
                                         sems.at[slot])

        def update(st, vtv, m_old):
            m_new = jnp.maximum(m_old, jnp.max(st, axis=0, keepdims=True))
            pb = jnp.exp(st - m_new).astype(BF16)
            acc_sc[...] = acc_sc[...] * jnp.exp(m_old - m_new) + _dot(vtv, pb)
            return m_new, pb

        m, pbc = update(_dot(kc_ref[...], qtv), vct_ref[...], jnp.full((1, tq), NEG, F32))
        mrun_ref[pl.ds(n_chunks, 1), :] = m
        stagec_sc[...] = pbc
        ctx_out = pltpu.make_async_copy(stagec_sc, p_hbm.at[h, i, pl.ds(s, nc), :], semc)
        ctx_out.start()

        def step(c, st, m_old):
            slot = c % 2
            off = pl.multiple_of(c * tk, tk)
            nxt = pl.multiple_of(jnp.minimum(c + 1, n_chunks - 1) * tk, tk)
            st_next = _dot(k_ref[pl.ds(nxt, tk), :], qtv)
            m_new, pb = update(st, vt_ref[:, pl.ds(off, tk)], m_old)
            mrun_ref[pl.ds(c, 1), :] = m_new
            stage_sc[slot] = pb
            p_out(slot, c).start()
            return st_next, m_new

        def loop(c, m_old):
            st_next, m_new = step(c, st_sc[...], m_old)
            p_out(1 - c % 2, c - 1).wait()
            st_sc[...] = st_next
            return m_new

        stage_sc[1] = jnp.zeros((tk, tq), BF16)
        pltpu.make_async_copy(stage_sc.at[1], p_hbm.at[h, i, pl.ds(s + nc, tk), :], sems.at[1]).start()
        st_sc[...] = _dot(k_ref[pl.ds(0, tk), :], qtv)
        m = lax.fori_loop(0, n_chunks, loop, m)
        p_out((n_chunks - 1) % 2, n_chunks - 1).wait()
        ctx_out.wait()
        acc = acc_sc[...]
        l = jnp.where(h < nh // N_KV, acc[half:half + 1], acc[0:1])
        ot_ref[...] = (acc / l).astype(BF16)
        lse_ref[...] = m + jnp.log(l)

    grp = nh // N_KV
    return _call(
        body, name="attn_glob_fwd", grid=(nh, s // tq),
        in_specs=[pl.BlockSpec((None, LANES, tq), lambda h, i: (h, 0, i)), _full((s, LANES)),
                  pl.BlockSpec((None, LANES, s), lambda h, i: (h // grp, 0, 0)), _full((nc, LANES)),
                  pl.BlockSpec((None, LANES, nc), lambda h, i: (h // grp, 0, 0))],
        out_specs=[pl.BlockSpec((None, LANES, tq), lambda h, i: (h, 0, i)),
                   pl.BlockSpec((None, 1, tq), lambda h, i: (h, 0, i)),
                   pl.BlockSpec((None, n_chunks + 1, tq), lambda h, i: (h, 0, i)), ANY],
        out_shape=[_sds((nh, LANES, s), BF16), _sds((nh, 1, s)), _sds((nh, n_chunks + 1, s)),
                   _sds((nh, s // tq, s + nc + tk, tq), BF16)],
        scratch=[pltpu.VMEM((LANES, tq), F32), pltpu.VMEM((tk, tq), F32), pltpu.VMEM((2, tk, tq), BF16),
                 pltpu.VMEM((nc, tq), BF16), pltpu.SemaphoreType.DMA((2,)), pltpu.SemaphoreType.DMA],
        sem=("parallel", "parallel"),
    )(qt, k, vt, kc, vct)


def _attn_glob_bwd(qt, dot, ot, lse, mrun, p, kt, v, kct, vc, *, tq, tk):
    nh, _, s = qt.shape
    nc = vc.shape[0]
    n_q = s // tq
    n_chunks = s // tk
    n_run = mrun.shape[1] - 1
    per_run = n_chunks // n_run

    def body(qt_ref, dot_ref, ot_ref, lse_ref, mrun_ref, p_hbm, kt_ref, v_ref, kct_ref, vc_ref,
             dqt_ref, dkt_ref, dvt_ref, dkct_ref, dvct_ref, acc_sc, dp_sc, dkt_sc, dvt_sc, p_sc, pc_sc, sems, semc):
        h, i = pl.program_id(0), pl.program_id(1)

        @pl.when(jnp.logical_and(h == 0, i == 0))
        def _():
            dkct_ref[...] = jnp.zeros(dkct_ref.shape, F32)
            dvct_ref[...] = jnp.zeros(dvct_ref.shape, F32)
            dkt_sc[...] = jnp.zeros(dkt_sc.shape, F32)
            dvt_sc[...] = jnp.zeros(dvt_sc.shape, F32)


        def p_in(slot, c):
            return pltpu.make_async_copy(p_hbm.at[h, i, pl.ds(pl.multiple_of(c * tk, tk), tk), :], p_sc.at[slot],
                                         sems.at[slot])

        ctx_in = pltpu.make_async_copy(p_hbm.at[h, i, pl.ds(s, nc), :], pc_sc, semc)
        ctx_in.start()
        p_in(0, 0).start()
        qtv, dotv, lse = qt_ref[...], dot_ref[...], lse_ref[...]
        delta = jnp.sum(dotv.astype(F32) * ot_ref[...].astype(F32), axis=0, keepdims=True)

        def grads(pt_stored, m_row, dpt):
            pt = pt_stored.astype(F32) * jnp.exp(m_row - lse)
            return pt.astype(BF16), (pt * (dpt - delta)).astype(BF16)

        dp_sc[...] = _dot(v_ref[pl.ds(0, tk), :], dotv)
        ctx_in.wait()
        pb, dsb = grads(pc_sc[...], mrun_ref[pl.ds(n_run, 1), :], _dot(vc_ref[...], dotv))
        acc_sc[...] = _dot(kct_ref[...], dsb)
        dkct_ref[...] += _dot_nt(qtv, dsb)
        dvct_ref[...] += _dot_nt(dotv, pb)

        def loop(c, carry):
            slot = c % 2
            off = pl.multiple_of(c * tk, tk)
            nxt = pl.multiple_of(jnp.minimum(c + 1, n_chunks - 1) * tk, tk)
            p_in(slot, c).wait()
            p_in(1 - slot, jnp.minimum(c + 1, n_chunks - 1)).start()
            dpt = dp_sc[...]
            dp_next = _dot(v_ref[pl.ds(nxt, tk), :], dotv)
            pb, dsb = grads(p_sc[slot], mrun_ref[pl.ds(c // per_run, 1), :], dpt)
            acc_sc[...] += _dot(kt_ref[:, pl.ds(off, tk)], dsb)
            dkt_sc[:, pl.ds(off, tk)] += _dot_nt(qtv, dsb)
            dvt_sc[:, pl.ds(off, tk)] += _dot_nt(dotv, pb)
            dp_sc[...] = dp_next
            return carry

        lax.fori_loop(0, n_chunks, loop, 0)
        p_in(n_chunks % 2, n_chunks - 1).wait()
        dqt_ref[...] = acc_sc[...]

        @pl.when(jnp.logical_and(h == nh - 1, i == n_q - 1))
        def _():
            pltpu.sync_copy(dkt_sc, dkt_ref)
            pltpu.sync_copy(dvt_sc, dvt_ref)

    qs = pl.BlockSpec((None, LANES, tq), lambda h, i: (h, 0, i))
    rs = pl.BlockSpec((None, 1, tq), lambda h, i: (h, 0, i))
    return _call(
        body, name="attn_glob_bwd", grid=(nh, n_q),
        in_specs=[qs, qs, qs, rs, pl.BlockSpec((None, n_run + 1, tq), lambda h, i: (h, 0, i)), ANY,
                  _full((LANES, s)), _full((s, LANES)), _full((LANES, nc)), _full((nc, LANES))],
        out_specs=[qs, ANY, ANY, _full((LANES, nc)), _full((LANES, nc))],
        out_shape=[_sds((nh, LANES, s)), _sds((LANES, s)), _sds((LANES, s)), _sds((LANES, nc)), _sds((LANES, nc))],
        scratch=[pltpu.VMEM((LANES, tq), F32), pltpu.VMEM((tk, tq), F32), pltpu.VMEM((LANES, s), F32),
                 pltpu.VMEM((LANES, s), F32), pltpu.VMEM((2, tk, tq), BF16), pltpu.VMEM((nc, tq), BF16),
                 pltpu.SemaphoreType.DMA((2,)), pltpu.SemaphoreType.DMA],
        sem=("arbitrary", "arbitrary"),
    )(qt, dot, ot, lse, mrun, p, kt, v, kct, vc)


WIN_SPAN = 2 * WINDOW


def _band(rows0, cols0, shape):
    r = rows0 + lax.broadcasted_iota(jnp.int32, shape, 0)
    c = cols0 + lax.broadcasted_iota(jnp.int32, shape, 1)
    return jnp.abs(r - c) <= WINDOW


def _win_start(blk, t, s):
    return pl.multiple_of(jnp.clip(blk * t - WINDOW, 0, s - t - WIN_SPAN), WINDOW)


def _attn_win_fwd(q, k, v, kc, vc, sink, *, tq):
    nh, s, _ = q.shape
    nc = kc.shape[0]
    tw = tq + WIN_SPAN

    def body(sink_ref, q_ref, k_ref, v_ref, kc_ref, vc_ref, o_ref, lse_ref):
        h, i = pl.program_id(0), pl.program_id(1)
        k0 = _win_start(i, tq, s)
        qv = q_ref[...]
        kv, vv = k_ref[pl.ds(k0, tw), :], v_ref[pl.ds(k0, tw), :]
        sc = jnp.where(_band(i * tq, k0, (tq, tw)), _dot_nt(qv, kv), NEG)
        scc = _dot_nt(qv, kc_ref[...])
        snk = sink_ref[h]
        m = jnp.maximum(jnp.maximum(jnp.max(sc, axis=-1, keepdims=True), jnp.max(scc, axis=-1, keepdims=True)), snk)
        p, pc = jnp.exp(sc - m), jnp.exp(scc - m)
        l = jnp.sum(p, axis=-1, keepdims=True) + jnp.sum(pc, axis=-1, keepdims=True) + jnp.exp(snk - m)
        acc = _dot(p.astype(BF16), vv) + _dot(pc.astype(BF16), vc_ref[...])
        o_ref[...] = (acc / l).astype(BF16)
        lse_ref[...] = m + jnp.log(l)

    return _call(
        body, name="attn_win_fwd", grid=(nh, s // tq),
        in_specs=[pl.BlockSpec(memory_space=pltpu.SMEM),
                  pl.BlockSpec((None, tq, LANES), lambda h, i: (h, i, 0)),
                  _full((s, LANES)), _full((s, LANES)), _full((nc, LANES)), _full((nc, LANES))],
        out_specs=[pl.BlockSpec((None, tq, LANES), lambda h, i: (h, i, 0)),
                   pl.BlockSpec((None, tq, 1), lambda h, i: (h, i, 0))],
        out_shape=[_sds((nh, s, LANES), BF16), _sds((nh, s, 1))],
        sem=("parallel", "parallel"),
    )(sink, q, k, v, kc, vc)


def _attn_win_dq(q, do, o, lse, k, v, kc, vc, sink, *, tq):
    nh, s, _ = q.shape
    nc = kc.shape[0]
    tw = tq + WIN_SPAN
    nq = s // tq

    def body(sink_ref, q_ref, do_ref, o_ref, lse_ref, k_ref, v_ref, kc_ref, vc_ref,
             dq_ref, dl_ref, dkc_ref, dvc_ref, dsk_ref):
        h, i = pl.program_id(0), pl.program_id(1)

        @pl.when(jnp.logical_and(h == 0, i == 0))
        def _():
            dkc_ref[...] = jnp.zeros(dkc_ref.shape, F32)
            dvc_ref[...] = jnp.zeros(dvc_ref.shape, F32)

        k0 = _win_start(i, tq, s)
        qv, dov, lse = q_ref[...], do_ref[...], lse_ref[...]
        kv, vv = k_ref[pl.ds(k0, tw), :], v_ref[pl.ds(k0, tw), :]
        kcv, vcv = kc_ref[...], vc_ref[...]
        delta = jnp.sum(dov.astype(F32) * o_ref[...].astype(F32), axis=-1, keepdims=True)
        dl_ref[...] = delta
        p = jnp.where(_band(i * tq, k0, (tq, tw)), jnp.exp(_dot_nt(qv, kv) - lse), 0.0)
        ds = (p * (_dot_nt(dov, vv) - delta)).astype(BF16)
        pc = jnp.exp(_dot_nt(qv, kcv) - lse)
        dsc = (pc * (_dot_nt(dov, vcv) - delta)).astype(BF16)
        dq_ref[...] = _dot(ds, kv) + _dot(dsc, kcv)
        dkc_ref[...] += _dot_tn(dsc, qv)
        dvc_ref[...] += _dot_tn(pc.astype(BF16), dov)
        dsk = -jnp.sum(jnp.exp(sink_ref[h] - lse) * delta)
        dsk_ref[...] = jnp.full(dsk_ref.shape, dsk, F32)

    qs = pl.BlockSpec((None, tq, LANES), lambda h, i: (h, i, 0))
    cs = pl.BlockSpec((None, tq, 1), lambda h, i: (h, i, 0))
    return _call(
        body, name="attn_win_dq", grid=(nh, nq),
        in_specs=[pl.BlockSpec(memory_space=pltpu.SMEM), qs, qs, qs, cs,
                  _full((s, LANES)), _full((s, LANES)), _full((nc, LANES)), _full((nc, LANES))],
        out_specs=[qs, cs, _full((nc, LANES)), _full((nc, LANES)),
                   pl.BlockSpec((None, None, 8, LANES), lambda h, i: (h, i, 0, 0))],
        out_shape=[_sds((nh, s, LANES)), _sds((nh, s, 1)), _sds((nc, LANES)), _sds((nc, LANES)),
                   _sds((nh, nq, 8, LANES))],
        sem=("arbitrary", "arbitrary"),
    )(sink, q, do, o, lse, k, v, kc, vc)


def _attn_win_dkv(q, do, lse_row, dl_row, k, v, *, tk):
    nh, s, _ = q.shape
    tw = tk + WIN_SPAN

    def body(k_ref, v_ref, q_ref, do_ref, lse_ref, dl_ref, dk_ref, dv_ref, dk_sc, dv_sc):
        h, j = pl.program_id(0), pl.program_id(1)
        q0 = _win_start(j, tk, s)
        kv, vv = k_ref[...], v_ref[...]
        qv, dov = q_ref[pl.ds(q0, tw), :], do_ref[pl.ds(q0, tw), :]
        pt = jnp.where(_band(j * tk, q0, (tk, tw)), jnp.exp(_dot_nt(kv, qv) - lse_ref[:, pl.ds(q0, tw)]), 0.0)
        dst = pt * (_dot_nt(vv, dov) - dl_ref[:, pl.ds(q0, tw)])
        dk, dv = _dot(dst.astype(BF16), qv), _dot(pt.astype(BF16), dov)
        rows = pl.ds(pl.multiple_of(j * tk, tk), tk)

        @pl.when(h == 0)
        def _():
            dk_sc[rows, :] = dk
            dv_sc[rows, :] = dv

        @pl.when(h > 0)
        def _():
            dk_sc[rows, :] += dk
            dv_sc[rows, :] += dv

        @pl.when(jnp.logical_and(h == nh - 1, j == s // tk - 1))
        def _():
            pltpu.sync_copy(dk_sc, dk_ref)
            pltpu.sync_copy(dv_sc, dv_ref)

    ks = pl.BlockSpec((tk, LANES), lambda h, j: (j, 0))
    qs = pl.BlockSpec((None, s, LANES), lambda h, j: (h, 0, 0))
    rs = pl.BlockSpec((None, 1, s), lambda h, j: (h, 0, 0))
    return _call(
        body, name="attn_win_dkv", grid=(nh, s // tk),
        in_specs=[ks, ks, qs, qs, rs, rs], out_specs=[ANY, ANY],
        out_shape=[_sds((s, LANES)), _sds((s, LANES))],
        scratch=[pltpu.VMEM((s, LANES), F32), pltpu.VMEM((s, LANES), F32)],
        sem=("arbitrary", "arbitrary"),
    )(k, v, q, do, lse_row, dl_row)


def _ln_fwd(z, g, b):
    mu = jnp.mean(z, axis=-1, keepdims=True)
    zc = z - mu
    r = lax.rsqrt(jnp.mean(zc * zc, axis=-1, keepdims=True) + LN_EPS)
    return zc * r * g + b, mu, r


def _ln_bwd(dy, xhat, r, g):
    dxh = dy * g
    return r * (dxh - jnp.mean(dxh, axis=-1, keepdims=True) - xhat * jnp.mean(dxh * xhat, axis=-1, keepdims=True))


def _heads_matmul(o_ref, w_ref):
    acc = _dot(o_ref[0], w_ref[0])
    for h in range(1, N_HEADS):
        acc += _dot(o_ref[h], w_ref[h])
    return acc


def _gate_specs(tm):
    return [pl.BlockSpec((tm, 512), functools.partial(lambda i, b: (i, b), b=OFF_GA // 512 + b)) for b in range(4)]


def _merge_fwd(oa, ob, proj, x, gate1, wba, wbb, w_out, ln_g, ln_b, *, tm):
    s = x.shape[0]

    def body(oa_ref, ob_ref, g0, g1, g2, g3, x_ref, gt_ref, wba_ref, wbb_ref, wo_ref, lg_ref, lb_ref,
             x1_ref, y_ref, mu_ref, r_ref):
        ga = _sigmoid(jnp.concatenate([g0[...], g1[...]], axis=1))
        gb = _sigmoid(jnp.concatenate([g2[...], g3[...]], axis=1))
        merged = ga * _heads_matmul(oa_ref, wba_ref) + gb * _heads_matmul(ob_ref, wbb_ref)
        y = _dot(merged.astype(BF16), wo_ref[...])
        x1, mu, r = _ln_fwd(ALPHA * x_ref[...] + gt_ref[...] * y, lg_ref[...], lb_ref[...])
        x1_ref[...] = x1
        y_ref[...] = y
        mu_ref[...] = mu
        r_ref[...] = r

    hs = pl.BlockSpec((N_HEADS, tm, LANES), lambda i: (0, i, 0))
    row = pl.BlockSpec((tm, D_MODEL), lambda i: (i, 0))
    col = pl.BlockSpec((tm, 1), lambda i: (i, 0))
    vec = _full((1, D_MODEL))
    wh = _full((N_HEADS, LANES, D_MODEL))
    return _call(
        body, name="merge_fwd", grid=(s // tm,),
        in_specs=[hs, hs, *_gate_specs(tm), row, vec, wh, wh, _full((D_MODEL, D_MODEL)), vec, vec],
        out_specs=[row, row, col, col],
        out_shape=[_sds((s, D_MODEL)), _sds((s, D_MODEL)), _sds((s, 1)), _sds((s, 1))],
        sem=("parallel",),
    )(oa, ob, proj, proj, proj, proj, x, gate1, wba, wbb, w_out, ln_g, ln_b)


def _merge_bwd(dy, oa, ob, oat, obt, proj, wba, wbb, w_out, *, tm):
    s = dy.shape[0]

    def body(dy_ref, oa_ref, ob_ref, oat_ref, obt_ref, g0, g1, g2, g3, wba_ref, wbb_ref, wo_ref,
             dgl_ref, doa_ref, dobt_ref, mg_ref, dwa_ref, dwb_ref):
        @pl.when(pl.program_id(0) == 0)
        def _():
            dwa_ref[...] = jnp.zeros(dwa_ref.shape, F32)
            dwb_ref[...] = jnp.zeros(dwb_ref.shape, F32)

        dm = _dot_nt(dy_ref[...], wo_ref[...])
        ga = _sigmoid(jnp.concatenate([g0[...], g1[...]], axis=1))
        gb = _sigmoid(jnp.concatenate([g2[...], g3[...]], axis=1))
        pa, pb = _heads_matmul(oa_ref, wba_ref), _heads_matmul(ob_ref, wbb_ref)
        mg_ref[...] = (ga * pa + gb * pb).astype(BF16)
        dgl_ref[:, :D_MODEL] = (dm * pa * ga * (1.0 - ga)).astype(BF16)
        dgl_ref[:, D_MODEL:] = (dm * pb * gb * (1.0 - gb)).astype(BF16)
        dpa, dpb = (dm * ga).astype(BF16), (dm * gb).astype(BF16)
        for h in range(N_HEADS):
            doa_ref[h] = _dot_nt(dpa, wba_ref[h]).astype(BF16)
            dobt_ref[h] = _dot_nt(wbb_ref[h], dpb).astype(BF16)
            dwa_ref[h] += _dot(oat_ref[h], dpa)
            dwb_ref[h] += _dot(obt_ref[h], dpb)

    hs = pl.BlockSpec((N_HEADS, tm, LANES), lambda i: (0, i, 0))
    hts = pl.BlockSpec((N_HEADS, LANES, tm), lambda i: (0, 0, i))
    row = pl.BlockSpec((tm, D_MODEL), lambda i: (i, 0))
    wh = _full((N_HEADS, LANES, D_MODEL))
    return _call(
        body, name="merge_bwd", grid=(s // tm,),
        in_specs=[row, hs, hs, hts, hts, *_gate_specs(tm), wh, wh, _full((D_MODEL, D_MODEL))],
        out_specs=[pl.BlockSpec((tm, 2 * D_MODEL), lambda i: (i, 0)), hs, hts, row, wh, wh],
        out_shape=[_sds((s, 2 * D_MODEL), BF16), _sds((N_HEADS, s, LANES), BF16), _sds((N_HEADS, LANES, s), BF16),
                   _sds((s, D_MODEL), BF16), _sds((N_HEADS, LANES, D_MODEL)), _sds((N_HEADS, LANES, D_MODEL))],
        sem=("arbitrary",),
    )(dy, oa, ob, oat, obt, proj, proj, proj, proj, wba, wbb, w_out)


FF_TC = 256


def _shift_rows(t, prev_row, next_row):
    n = t.shape[0]
    r = lax.broadcasted_iota(jnp.int32, t.shape, 0)
    up = jnp.where(r == 0, prev_row, pltpu.roll(t, 1, 0))
    dn = jnp.where(r == n - 1, next_row, pltpu.roll(t, n - 1, 0))
    return up, dn


def _halo_specs(tm, s, tc):
    nb8 = s // 8
    main = pl.BlockSpec((2, tm, tc), lambda j, i: (0, i, j))
    prev = pl.BlockSpec((2, 8, tc), lambda j, i: (0, jnp.maximum(i * (tm // 8) - 1, 0), j))
    nxt = pl.BlockSpec((2, 8, tc), lambda j, i: (0, jnp.minimum((i + 1) * (tm // 8), nb8 - 1), j))
    return main, prev, nxt


def _halo_rows(prev_ref, next_ref, half, i, n_i):
    prev_row = jnp.where(i == 0, 0.0, prev_ref[half, 7:8, :].astype(F32))
    next_row = jnp.where(i == n_i - 1, 0.0, next_ref[half, 0:1, :].astype(F32))
    return prev_row, next_row


def _conv(t, prev_row, next_row, w, b):
    up, dn = _shift_rows(t, prev_row, next_row)
    return w[0:1, :] * up + w[1:2, :] * t + w[2:3, :] * dn + b


def _ffn_act_fwd(u, cw, cb, *, tm):
    _, s, ff = u.shape
    n_i = s // tm

    def body(u_ref, up_ref, un_ref, cw_ref, cb_ref, a_ref):
        i = pl.program_id(1)
        gc = _conv(u_ref[0], *_halo_rows(up_ref, un_ref, 0, i, n_i), cw_ref[0], cb_ref[0])
        vc = _conv(u_ref[1], *_halo_rows(up_ref, un_ref, 1, i, n_i), cw_ref[1], cb_ref[1])
        a_ref[...] = (gc * _sigmoid(gc) * vc).astype(BF16)

    main, prev, nxt = _halo_specs(tm, s, FF_TC)
    return _call(
        body, name="ffn_act_fwd", grid=(ff // FF_TC, n_i),
        in_specs=[main, prev, nxt, pl.BlockSpec((2, 3, FF_TC), lambda j, i: (0, 0, j)),
                  pl.BlockSpec((2, 1, FF_TC), lambda j, i: (0, 0, j))],
        out_specs=pl.BlockSpec((tm, FF_TC), lambda j, i: (i, j)),
        out_shape=_sds((s, ff), BF16), sem=("parallel", "parallel"),
    )(u, u, u, cw, cb)


def _ffn_act_bwd(dy2, w_down, u, cw, cb, *, tm):
    _, s, ff = u.shape
    n_i = s // tm

    def body(dy_ref, wd_ref, u_ref, up_ref, un_ref, cw_ref, cb_ref, dc_ref, dcw_ref, dcb_ref):
        i = pl.program_id(1)

        @pl.when(i == 0)
        def _():
            dcw_ref[...] = jnp.zeros(dcw_ref.shape, F32)
            dcb_ref[...] = jnp.zeros(dcb_ref.shape, F32)

        da = _dot_nt(dy_ref[...], wd_ref[...])
        ug, uv = u_ref[0], u_ref[1]
        ugp, ugn = _shift_rows(ug, *_halo_rows(up_ref, un_ref, 0, i, n_i))
        uvp, uvn = _shift_rows(uv, *_halo_rows(up_ref, un_ref, 1, i, n_i))
        wg, wv = cw_ref[0], cw_ref[1]
        gc = wg[0:1, :] * ugp + wg[1:2, :] * ug + wg[2:3, :] * ugn + cb_ref[0]
        vc = wv[0:1, :] * uvp + wv[1:2, :] * uv + wv[2:3, :] * uvn + cb_ref[1]
        sg = _sigmoid(gc)
        dg = da * vc * sg * (1.0 + gc * (1.0 - sg))
        dv = da * gc * sg
        dc_ref[0] = dg
        dc_ref[1] = dv
        for half, (d, taps) in enumerate(((dg, (ugp, ug, ugn)), (dv, (uvp, uv, uvn)))):
            for tap in range(3):
                dcw_ref[half, tap:tap + 1, :] += jnp.sum(d * taps[tap], axis=0, keepdims=True)
            dcb_ref[half] += jnp.sum(d, axis=0, keepdims=True)

    main, prev, nxt = _halo_specs(tm, s, FF_TC)
    return _call(
        body, name="ffn_act_bwd", grid=(ff // FF_TC, n_i),
        in_specs=[pl.BlockSpec((tm, D_MODEL), lambda j, i: (i, 0)), pl.BlockSpec((FF_TC, D_MODEL), lambda j, i: (j, 0)),
                  main, prev, nxt, pl.BlockSpec((2, 3, FF_TC), lambda j, i: (0, 0, j)),
                  pl.BlockSpec((2, 1, FF_TC), lambda j, i: (0, 0, j))],
        out_specs=[main, pl.BlockSpec((2, 3, FF_TC), lambda j, i: (0, 0, j)),
                   pl.BlockSpec((2, 1, FF_TC), lambda j, i: (0, 0, j))],
        out_shape=[_sds((2, s, ff)), _sds((2, 3, ff)), _sds((2, 1, ff))],
        sem=("parallel", "arbitrary"),
    )(dy2, w_down, u, u, u, cw, cb)


def _conv_bwd_input(dc, cw, *, tm):
    _, s, ff = dc.shape
    n_i = s // tm

    def body(d_ref, dp_ref, dn_ref, cw_ref, du_ref):
        i = pl.program_id(1)
        for half in range(2):
            up, dn = _shift_rows(d_ref[half], *_halo_rows(dp_ref, dn_ref, half, i, n_i))
            w = cw_ref[half]
            du_ref[half] = (w[0:1, :] * dn + w[1:2, :] * d_ref[half] + w[2:3, :] * up).astype(BF16)

    main, prev, nxt = _halo_specs(tm, s, FF_TC)
    return _call(
        body, name="conv_bwd_input", grid=(ff // FF_TC, n_i),
        in_specs=[main, prev, nxt, pl.BlockSpec((2, 3, FF_TC), lambda j, i: (0, 0, j))],
        out_specs=main, out_shape=_sds((2, s, ff), BF16), sem=("parallel", "parallel"),
    )(dc, dc, dc, cw)


def _ffn_down_loss(a, w_down, x1, target, gate2, ln_g, ln_b, *, tm):
    s, ff = a.shape
    n_i = s // tm

    def body(a_ref, wd_ref, x1_ref, tg_ref, gt_ref, lg_ref, lb_ref, ls_ref, dy_ref, dx_ref, dg_ref, db_ref, dgt_ref):
        @pl.when(pl.program_id(0) == 0)
        def _():
            dg_ref[...] = jnp.zeros(dg_ref.shape, F32)
            db_ref[...] = jnp.zeros(db_ref.shape, F32)
            dgt_ref[...] = jnp.zeros(dgt_ref.shape, F32)

        y2 = _dot(a_ref[...], wd_ref[...])
        z = ALPHA * x1_ref[...] + gt_ref[...] * y2
        mu = jnp.mean(z, axis=-1, keepdims=True)
        zc = z - mu
        r = lax.rsqrt(jnp.mean(zc * zc, axis=-1, keepdims=True) + LN_EPS)
        xhat = zc * r
        diff = xhat * lg_ref[...] + lb_ref[...] - tg_ref[...]
        ls_ref[...] = jnp.full(ls_ref.shape, 0.5 / D_MODEL * jnp.sum(diff * diff), F32)
        dx2 = diff * (1.0 / D_MODEL)
        dg_ref[...] += jnp.sum(dx2 * xhat, axis=0, keepdims=True)
        db_ref[...] += jnp.sum(dx2, axis=0, keepdims=True)
        dz = _ln_bwd(dx2, xhat, r, lg_ref[...])
        dgt_ref[...] += jnp.sum(dz * y2, axis=0, keepdims=True)
        dy_ref[...] = (gt_ref[...] * dz).astype(BF16)
        dx_ref[...] = ALPHA * dz

    row = pl.BlockSpec((tm, D_MODEL), lambda i: (i, 0))
    vec = _full((1, D_MODEL))
    return _call(
        body, name="ffn_down_loss", grid=(n_i,),
        in_specs=[pl.BlockSpec((tm, ff), lambda i: (i, 0)), _full((ff, D_MODEL)), row, row, vec, vec, vec],
        out_specs=[pl.BlockSpec((None, 8, LANES), lambda i: (i, 0, 0)), row, row, vec, vec, vec],
        out_shape=[_sds((n_i, 8, LANES)), _sds((s, D_MODEL), BF16), _sds((s, D_MODEL)),
                   _sds((1, D_MODEL)), _sds((1, D_MODEL)), _sds((1, D_MODEL))],
        sem=("arbitrary",),
    )(a, w_down, x1, target, gate2, ln_g, ln_b)


def _ffn_up_bwd(du, wup4, dx1a, x1, scale2, x, y, mu1, r1, gate1, ln_g, *, tm):
    s = x.shape[0]
    nb, _, ns = wup4.shape

    def body(du_ref, w_ref, dxa_ref, x1_ref, sc_ref, x_ref, y_ref, mu_ref, r_ref, gt_ref, lg_ref,
             dxo_ref, dy_ref, dsc_ref, dsh_ref, dg_ref, db_ref, dgt_ref, acc):
        i, k = pl.program_id(0), pl.program_id(1)

        @pl.when(jnp.logical_and(i == 0, k == 0))
        def _():
            for ref in (dsc_ref, dsh_ref, dg_ref, db_ref, dgt_ref):
                ref[...] = jnp.zeros(ref.shape, F32)

        @pl.when(k == 0)
        def _():
            acc[...] = jnp.zeros(acc.shape, F32)

        acc[...] += _dot_nt(du_ref[...], w_ref[...])

        @pl.when(k == nb - 1)
        def _():
            dh = acc[...]
            x1 = x1_ref[...]
            dsc_ref[...] += jnp.sum(dh * x1, axis=0, keepdims=True)
            dsh_ref[...] += jnp.sum(dh, axis=0, keepdims=True)
            dx1 = dxa_ref[...] + dh * (1.0 + sc_ref[...])
            yv = y_ref[...]
            xhat = (ALPHA * x_ref[...] + gt_ref[...] * yv - mu_ref[...]) * r_ref[...]
            dg_ref[...] += jnp.sum(dx1 * xhat, axis=0, keepdims=True)
            db_ref[...] += jnp.sum(dx1, axis=0, keepdims=True)
            dz = _ln_bwd(dx1, xhat, r_ref[...], lg_ref[...])
            dgt_ref[...] += jnp.sum(dz * yv, axis=0, keepdims=True)
            dy_ref[...] = (gt_ref[...] * dz).astype(BF16)
            dxo_ref[...] = ALPHA * dz

    row = pl.BlockSpec((tm, D_MODEL), lambda i, k: (i, 0))
    col = pl.BlockSpec((tm, 1), lambda i, k: (i, 0))
    vec = _full((1, D_MODEL))
    return _call(
        body, name="ffn_up_bwd", grid=(s // tm, nb),
        in_specs=[pl.BlockSpec((None, tm, ns), lambda i, k: (k // 2, i, k % 2)),
                  pl.BlockSpec((None, D_MODEL, ns), lambda i, k: (k, 0, 0)),
                  row, row, vec, row, row, col, col, vec, vec],
        out_specs=[row, row, vec, vec, vec, vec, vec],
        out_shape=[_sds((s, D_MODEL)), _sds((s, D_MODEL), BF16)] + [_sds((1, D_MODEL))] * 5,
        scratch=[pltpu.VMEM((tm, D_MODEL), F32)],
        sem=("arbitrary", "arbitrary"),
    )(du, wup4, dx1a, x1, scale2, x, y, mu1, r1, gate1, ln_g)


def _mm_nt4_mod_bwd(dp, w4, dxa, x, scale, *, tm, name):
    m = x.shape[0]
    nb, kdim, ns = w4.shape

    def body(dp_ref, w_ref, dxa_ref, x_ref, sc_ref, dx_ref, dsc_ref, dsh_ref, acc):
        i, k = pl.program_id(0), pl.program_id(1)

        @pl.when(jnp.logical_and(i == 0, k == 0))
        def _():
            dsc_ref[...] = jnp.zeros(dsc_ref.shape, F32)
            dsh_ref[...] = jnp.zeros(dsh_ref.shape, F32)

        @pl.when(k == 0)
        def _():
            acc[...] = jnp.zeros(acc.shape, F32)

        acc[...] += _dot_nt(dp_ref[...], w_ref[...])

        @pl.when(k == nb - 1)
        def _():
            dh = acc[...]
            dsc_ref[...] += jnp.sum(dh * x_ref[...], axis=0, keepdims=True)
            dsh_ref[...] += jnp.sum(dh, axis=0, keepdims=True)
            dx_ref[...] = dxa_ref[...] + dh * (1.0 + sc_ref[...])

    row = pl.BlockSpec((tm, kdim), lambda i, k: (i, 0))
    vec = _full((1, kdim))
    return _call(
        body, name=name, grid=(m // tm, nb),
        in_specs=[pl.BlockSpec((tm, ns), lambda i, k: (i, k)), pl.BlockSpec((None, kdim, ns), lambda i, k: (k, 0, 0)),
                  row, row, vec],
        out_specs=[row, vec, vec],
        out_shape=[_sds((m, kdim)), _sds((1, kdim)), _sds((1, kdim))],
        scratch=[pltpu.VMEM((tm, kdim), F32)],
        sem=("arbitrary", "arbitrary"),
    )(dp, w4, dxa, x, scale)


def _pad_heads_w(w):
    w8 = w.reshape(N_HEADS, HEAD_DIM, w.shape[-1])
    z = jnp.zeros_like(w8)
    first = (jnp.arange(N_HEADS) < N_HEADS // N_KV)[:, None, None]
    return jnp.where(first, jnp.concatenate([w8, z], axis=1), jnp.concatenate([z, w8], axis=1))


def _unpad_heads_w(g):
    first = (jnp.arange(N_HEADS) < N_HEADS // N_KV)[:, None, None]
    return jnp.where(first, g[:, :HEAD_DIM], g[:, HEAD_DIM:]).reshape(N_HEADS * HEAD_DIM, g.shape[-1])


def _ones_beside(vt):
    half = vt.shape[0] // 2
    ones = jnp.ones((half, vt.shape[1]), vt.dtype)
    return jnp.stack([jnp.concatenate([vt[:half], ones], axis=0), jnp.concatenate([ones, vt[half:]], axis=0)])


def _rep8(a):
    return jnp.broadcast_to(a.reshape(1, -1), (8, a.size))


def _first_row(a):
    r8 = _rep8(a)
    return jnp.where(lax.broadcasted_iota(jnp.int32, r8.shape, 0) == 0, r8, 0.0)


def _to_blocks4(w):
    k, n = w.shape
    return w.reshape(k, N_CHIPS, n // N_CHIPS).transpose(1, 0, 2)


def _local_step(x, c, ctx, c_ctx, wmod4, b_mod, win4, b_in, sink, qn, kn, wba, wbb, w_out, ln1_g, ln1_b,
                wup4, cw, cb, w_down, ln2_g, ln2_b, target):
    s, nc = x.shape[0], ctx.shape[0]
    tm = min(512, s)
    tm2 = min(256, s)
    zvec = jnp.zeros((1, D_MODEL), F32)

    cc = jnp.concatenate([_rep8(c), _rep8(c_ctx)], axis=0)
    mods = _mm_nn4(cc, zvec, zvec, wmod4, b_mod, mode="silu", split_out=False, out_dtype=F32, tm=16, name="mod_vectors")
    shift1, scale1, gate1, shift2, scale2, gate2 = [mods[0:1, i * D_MODEL:(i + 1) * D_MODEL] for i in range(6)]
    shift_c, scale_c = mods[8:9, :D_MODEL], mods[8:9, D_MODEL:2 * D_MODEL]

    cos, sin = _rope_tables(s)
    cos_c, sin_c = jnp.ones((nc, LANES), F32), jnp.zeros((nc, LANES), F32)
    qg, kg = jnp.tile(qn, (1, 2)), jnp.tile(kn, (1, 2))

    proj_c = _mm_nn4(ctx, shift_c, scale_c, win4, b_in, mode="modulate", split_out=False, out_dtype=F32, tm=nc,
                     name="in_proj_ctx")
    _, kac, vac, _, kbc, vbc = _prep(proj_c, cos_c, sin_c, qg, kg, tm=nc, name="prep_ctx")
    proj = _mm_nn4(x, shift1, scale1, win4, b_in, mode="modulate", split_out=False, out_dtype=F32, tm=tm, name="in_proj")
    qa, ka, va, qb, kb, vb = _prep(proj, cos, sin, qg, kg, tm=tm, name="prep")
    oa, lse_a = _attn_win_fwd(qa, ka, va, kac, vac, sink, tq=tm)
    qbt = jnp.swapaxes(qb, 1, 2)
    obt, lse_b, mrun_b, pbt = _attn_glob_fwd(qbt, kb, _ones_beside(vb.T), kbc, _ones_beside(vbc.T), tq=tm,
                                             tk=min(1024, s))
    ob = jnp.swapaxes(obt, 1, 2)
    wba_p, wbb_p = _pad_heads_w(wba), _pad_heads_w(wbb)
    x1, y, mu1, r1 = _merge_fwd(oa, ob, proj, x, gate1, wba_p, wbb_p, w_out, ln1_g, ln1_b, tm=tm2)
    u = _mm_nn4(x1, shift2, scale2, wup4, jnp.zeros((1, 2 * D_FF), F32), mode="modulate", split_out=True,
                out_dtype=F32, tm=tm, name="ffn_up")
    cw2 = cw.reshape(3, 2, D_FF).transpose(1, 0, 2)
    cb2 = cb.reshape(2, 1, D_FF)
    a = _ffn_act_fwd(u, cw2, cb2, tm=tm)
    ls, dy2, dx1a, dln2_g, dln2_b, dgate2 = _ffn_down_loss(a, w_down, x1, target, gate2, ln2_g, ln2_b, tm=tm2)
    loss = jnp.sum(ls[:, 0, 0])

    n_s = s // tm
    dw_down = _mm_tn(a, dy2, a_spec=pl.BlockSpec((tm, D_FF), lambda t: (t, 0)),
                     b_spec=pl.BlockSpec((tm, D_MODEL), lambda t: (t, 0)), grid=(n_s,),
                     out_shape=_sds((D_FF, D_MODEL)), out_spec=_full((D_FF, D_MODEL)), name="dw_down")
    dc, dcw2, dcb2 = _ffn_act_bwd(dy2, w_down, u, cw2, cb2, tm=tm)
    du = _conv_bwd_input(dc, cw2, tm=tm)
    dxz1, dy, dscale2, dshift2, dln1_g, dln1_b, dgate1 = _ffn_up_bwd(
        du, wup4, dx1a, x1, scale2, x, y, mu1, r1, gate1, ln1_g, tm=tm2)
    ns_up = wup4.shape[-1]
    dw_up4 = _mm_tn(x1, du, a_spec=pl.BlockSpec((tm, D_MODEL), lambda k, t: (t, 0)),
                    b_spec=pl.BlockSpec((None, tm, ns_up), lambda k, t: (k // 2, t, k % 2)), grid=(N_CHIPS, n_s),
                    out_shape=_sds((N_CHIPS, D_MODEL, ns_up)),
                    out_spec=pl.BlockSpec((None, D_MODEL, ns_up), lambda k, t: (k, 0, 0)),
                    mod=(shift2, scale2), name="dw_up")

    dgl, doa, dobt, merged, dwba_p, dwbb_p = _merge_bwd(dy, oa, ob, jnp.swapaxes(oa, 1, 2), obt, proj, wba_p, wbb_p,
                                                        w_out, tm=tm2)
    dwba, dwbb = _unpad_heads_w(dwba_p), _unpad_heads_w(dwbb_p)
    rowspec = pl.BlockSpec((tm, D_MODEL), lambda t: (t, 0))
    dw_out = _mm_tn(merged, dy, a_spec=rowspec, b_spec=rowspec, grid=(n_s,), out_shape=_sds((D_MODEL, D_MODEL)),
                    out_spec=_full((D_MODEL, D_MODEL)), name="dw_out")

    dqa, dla, dkac, dvac, dsk = _attn_win_dq(qa, doa, oa, lse_a, ka, va, kac, vac, sink, tq=tm)
    dka, dva = _attn_win_dkv(qa, doa, lse_a.reshape(N_HEADS, 1, s), dla.reshape(N_HEADS, 1, s), ka, va, tk=tm)
    dqbt, dkbt, dvbt, dkbct, dvbct = _attn_glob_bwd(qbt, dobt, obt, lse_b, mrun_b, pbt, kb.T, vb, kbc.T, vbc, tq=tm, tk=tm)
    dqb, dkb, dvb, dkbc, dvbc = jnp.swapaxes(dqbt, 1, 2), dkbt.T, dvbt.T, dkbct.T, dvbct.T
    dsink = jnp.sum(dsk[:, :, 0, 0], axis=1)

    dproj, dqg, dkg = _prep_bwd(dqa, dka, dva, dqb, dkb, dvb, proj, cos, sin, qg, kg, dgl, tm=tm, name="prep_bwd")
    grad_x, dscale1, dshift1 = _mm_nt4_mod_bwd(dproj, win4, dxz1, x, scale1, tm=tm, name="in_proj_bwd")
    ns_in = win4.shape[-1]
    win_spec = dict(b_spec=pl.BlockSpec((None, None, ns_in), lambda k, t: (0, 0, k)),
                    out_shape=_sds((N_CHIPS, D_MODEL, ns_in)),
                    out_spec=pl.BlockSpec((None, D_MODEL, ns_in), lambda k, t: (k, 0, 0)),
                    colsum_spec=pl.BlockSpec((8, ns_in), lambda k, t: (0, k)), colsum_shape=_sds((8, IN_COLS)))
    win_spec["b_spec"] = pl.BlockSpec((tm, ns_in), lambda k, t: (t, k))
    dw_in4, db_in = _mm_tn(x, dproj, a_spec=pl.BlockSpec((tm, D_MODEL), lambda k, t: (t, 0)), grid=(N_CHIPS, n_s),
                           mod=(shift1, scale1), name="dw_in", **win_spec)

    zq = jnp.zeros((N_HEADS, nc, LANES), F32)
    dproj_c, _, dkg_c = _prep_bwd(zq, dkac, dvac, zq, dkbc, dvbc, proj_c, cos_c, sin_c, qg, kg,
                                  jnp.zeros((nc, IN_COLS - OFF_GA), BF16), tm=nc, name="prep_bwd_ctx")
    _, dscale_c, dshift_c = _mm_nt4_mod_bwd(dproj_c, win4, jnp.zeros((nc, D_MODEL), F32), ctx, scale_c, tm=nc,
                                            name="in_proj_bwd_ctx")
    win_spec["b_spec"] = pl.BlockSpec((nc, ns_in), lambda k, t: (t, k))
    dw_in4, db_in_c = _mm_tn(ctx, dproj_c, a_spec=pl.BlockSpec((nc, D_MODEL), lambda k, t: (t, 0)), grid=(N_CHIPS, 1),
                             mod=(shift_c, scale_c), init=dw_in4, name="dw_in_ctx", **win_spec)

    dmod = jnp.concatenate([dshift1, dscale1, dgate1, dshift2, dscale2, dgate2], axis=1)
    dmodc = jnp.concatenate([dshift_c, dscale_c], axis=1)
    dmodc_pad = jnp.concatenate([dmodc, jnp.zeros((1, 4 * D_MODEL), F32)], axis=1)
    dmodc8 = _first_row(dmodc_pad).astype(BF16)
    z8 = jnp.zeros((8, D_MODEL), F32)
    dsilu_c, _, _ = _mm_nt4_mod_bwd(dmodc8, wmod4, z8, z8, zvec, tm=8, name="c_ctx_bwd")
    sg = _sigmoid(c_ctx)
    dc_ctx = dsilu_c[0:1] * sg * (1.0 + c_ctx * (1.0 - sg))

    dqn = jnp.sum(dqg.reshape(N_HEADS, HEAD_DIM), axis=0, keepdims=True)
    dkn = jnp.sum((dkg + dkg_c).reshape(N_KV, HEAD_DIM), axis=0, keepdims=True)
    grads = dict(
        w_in4=dw_in4, b_in=db_in[0:1] + db_in_c[0:1], sink=dsink, qn=dqn, kn=dkn, wba=dwba, wbb=dwbb, w_out=dw_out,
        ln1_g=dln1_g, ln1_b=dln1_b, w_up4=dw_up4, conv_w=dcw2.transpose(1, 0, 2).reshape(3, 2 * D_FF),
        conv_b=dcb2.reshape(1, 2 * D_FF), w_down=dw_down, ln2_g=dln2_g, ln2_b=dln2_b,
        c_ctx=dc_ctx, dmod=dmod, dmodc=dmodc)
    return loss, grad_x, grads


ANY = pl.BlockSpec(memory_space=pl.ANY)


def _mesh_pos():
    return lax.axis_index("x"), lax.axis_index("y"), lax.axis_index("c")


def _other_chips(x, y):
    return [(1 - x, y), (x, 1 - y), (1 - x, 1 - y)]


def _remote(src, dst, send, recv, dev):
    return pltpu.make_async_remote_copy(src_ref=src, dst_ref=dst, send_sem=send, recv_sem=recv, device_id=dev,
                                        device_id_type=MESH)


def _set_block(stack, block, k):
    return lax.dynamic_update_slice(stack, block[None], (k,) + (0,) * block.ndim)


def _gather_shards(arrs, small):
    na = len(arrs)
    halves = [a.shape[0] // 2 for a in arrs]

    def body(*refs):
        ins, small_ref = refs[:na], refs[na]
        outs, small_out = refs[na + 1:2 * na + 1], refs[2 * na + 1]
        send, recv = refs[2 * na + 2:]
        x, y, c = _mesh_pos()
        me = 2 * x + y
        chips = _other_chips(x, y)

        def half(a, cc):
            return pl.ds(cc * halves[a], halves[a])

        sends = []
        for j, chip in enumerate(chips):
            for a in range(na):
                sends.append(_remote(ins[a].at[half(a, c)], outs[a].at[me, half(a, c)], send.at[a, j], recv.at[a, j],
                                     (*chip, c)))
            sends.append(_remote(small_ref, small_out.at[me], send.at[na, j], recv.at[na, j], (*chip, c)))
        for cp in sends:
            cp.start()
        for j, chip in enumerate(chips):
            kj = 2 * chip[0] + chip[1]
            for a in range(na):
                landed = outs[a].at[kj, half(a, c)]
                _remote(landed, landed, send.at[a, j], recv.at[a, j], (*chip, c)).wait_recv()
                fwd = _remote(landed, landed, send.at[a, 3 + j], recv.at[a, 3 + j], (x, y, 1 - c))
                fwd.start()
                sends.append(fwd)
            _remote(small_ref, small_out.at[kj], send.at[na, j], recv.at[na, j], (*chip, c)).wait_recv()
        for j, chip in enumerate(chips):
            kj = 2 * chip[0] + chip[1]
            for a in range(na):
                other = outs[a].at[kj, half(a, 1 - c)]
                _remote(other, other, send.at[a, 3 + j], recv.at[a, 3 + j], (x, y, 1 - c)).wait_recv()
        for cp in sends:
            cp.wait_send()

    out_shape = [_sds((N_CHIPS,) + a.shape, a.dtype) for a in arrs] + [_sds((N_CHIPS,) + small.shape, small.dtype)]
    got = pl.pallas_call(
        body, name="gather_shards", in_specs=[ANY] * (na + 1), out_specs=[ANY] * (na + 1), out_shape=out_shape,
        scratch_shapes=[pltpu.SemaphoreType.DMA((na + 1, 6)), pltpu.SemaphoreType.DMA((na + 1, 6))],
    )(*arrs, small)
    xp, yp, _ = _mesh_pos()
    return [_set_block(g, a, 2 * xp + yp) for g, a in zip(got, list(arrs) + [small])]


def _allgather_rows(v):
    r, n = v.shape

    def body(v_ref, out_ref, send, recv, loc):
        x, y, c = _mesh_pos()
        me, sibling = (x, y, c), (x, y, 1 - c)
        chips = _other_chips(x, y)

        def rows(px, py, pc):
            return out_ref.at[4 * px + 2 * py + pc]

        def copy(k, block, to, src=None):
            return _remote(rows(*block) if src is None else src, rows(*block), send.at[k], recv.at[k], to)

        mine = pltpu.make_async_copy(v_ref, rows(*me), loc)
        mine.start()
        first = [copy(0, me, sibling, src=v_ref)] + [copy(1 + j, me, (*chip, c), src=v_ref) for j, chip in enumerate(chips)]
        for cp in first:
            cp.start()
        passed = [copy(4 + j, (*chip, c), sibling) for j, chip in enumerate(chips)]
        for j, chip in enumerate(chips):
            copy(1 + j, (*chip, c), me).wait_recv()
            passed[j].start()
        copy(0, sibling, me).wait_recv()
        for j, chip in enumerate(chips):
            copy(4 + j, (*chip, 1 - c), me).wait_recv()
        for cp in first + passed:
            cp.wait_send()
        mine.wait()

    return pl.pallas_call(
        body, name="allgather_rows", in_specs=[pl.BlockSpec(memory_space=pltpu.VMEM)],
        out_specs=pl.BlockSpec(memory_space=pltpu.VMEM), out_shape=_sds((N_DEV, r, n), v.dtype),
        scratch_shapes=[pltpu.SemaphoreType.DMA((7,)), pltpu.SemaphoreType.DMA((7,)), pltpu.SemaphoreType.DMA],
    )(v)


def _swap_other_half(g):
    nb, r, n = g.shape
    rh = r // 2

    def body(g_ref, out_ref, send, recv):
        x, y, c = _mesh_pos()
        cp = _remote(g_ref.at[:, pl.ds((1 - c) * rh, rh), :], out_ref, send, recv, (x, y, 1 - c))
        cp.start()
        cp.wait()

    return pl.pallas_call(
        body, name="swap_other_half", in_specs=[ANY], out_specs=ANY, out_shape=_sds((nb, rh, n), g.dtype),
        scratch_shapes=[pltpu.SemaphoreType.DMA, pltpu.SemaphoreType.DMA],
    )(g)


def _scatter_to_chips(p):
    def body(p_ref, out_ref, send, recv):
        x, y, c = _mesh_pos()
        me = 2 * x + y
        chips = _other_chips(x, y)
        sends = [_remote(p_ref.at[2 * chip[0] + chip[1]], out_ref.at[me], send.at[j], recv.at[j], (*chip, c))
                 for j, chip in enumerate(chips)]
        for cp in sends:
            cp.start()
        for j, chip in enumerate(chips):
            kj = 2 * chip[0] + chip[1]
            _remote(p_ref.at[kj], out_ref.at[kj], send.at[j], recv.at[j], (*chip, c)).wait_recv()
        for cp in sends:
            cp.wait_send()

    got = pl.pallas_call(
        body, name="scatter_to_chips", in_specs=[ANY], out_specs=ANY, out_shape=_sds(p.shape, p.dtype),
        scratch_shapes=[pltpu.SemaphoreType.DMA((3,)), pltpu.SemaphoreType.DMA((3,))],
    )(p)
    xp, yp, _ = _mesh_pos()
    me = 2 * xp + yp
    return _set_block(got, lax.dynamic_index_in_dim(p, me, axis=0, keepdims=False), me)


def _join_halves(f):
    def body(f_ref, out_ref, send, recv):
        x, y, c = _mesh_pos()
        cp = _remote(f_ref, out_ref, send, recv, (x, y, 1 - c))
        cp.start()
        cp.wait()

    other = pl.pallas_call(
        body, name="join_halves", in_specs=[ANY], out_specs=ANY, out_shape=_sds(f.shape, f.dtype),
        scratch_shapes=[pltpu.SemaphoreType.DMA, pltpu.SemaphoreType.DMA],
    )(f)
    first = lax.axis_index("c") == 0
    return jnp.concatenate([jnp.where(first, f, other), jnp.where(first, other, f)], axis=0)


def _row_tile(rows, cap=512):
    t = cap - cap % 8
    while rows % t:
        t -= 8
    return t


def _add_blocks(a, b, out_dtype):
    nb, r, n = a.shape
    tr = _row_tile(r)

    def body(a_ref, b_ref, o_ref):
        o_ref[...] = (a_ref[...] + b_ref[...]).astype(out_dtype)

    spec = pl.BlockSpec((None, tr, n), lambda k, i: (k, i, 0))
    return _call(body, name="add_blocks", grid=(nb, r // tr), in_specs=[spec, spec], out_specs=spec,
                 out_shape=_sds(a.shape, out_dtype), sem=("parallel", "parallel"))(a, b)


def _sum_leading(a, *, name):
    nk, r, n = a.shape
    tr = _row_tile(r)

    def body(a_ref, o_ref):
        acc = a_ref[0].astype(F32)
        for k in range(1, nk):
            acc = acc + a_ref[k].astype(F32)
        o_ref[...] = acc

    return _call(body, name=name, grid=(r // tr,), in_specs=[pl.BlockSpec((nk, tr, n), lambda i: (0, i, 0))],
                 out_specs=pl.BlockSpec((tr, n), lambda i: (i, 0)), out_shape=_sds((r, n)), sem=("parallel",))(a)


def _silu_outer(a, b):
    kdim, n = a.shape[1], b.shape[1]

    def body(a_ref, b_ref, o_ref):
        av = a_ref[...]
        av = av * _sigmoid(av)
        bv = b_ref[...]
        ah, bh = av.astype(BF16), bv.astype(BF16)
        al, bl = (av - ah.astype(F32)).astype(BF16), (bv - bh.astype(F32)).astype(BF16)
        o_ref[...] = _dot_tn(ah, bh) + (_dot_tn(ah, bl) + _dot_tn(al, bh))

    return _call(body, name="dw_mod", grid=(1,), in_specs=[_full(a.shape), _full(b.shape)], out_specs=_full((kdim, n)),
                 out_shape=_sds((kdim, n)))(a, b)


def _adamw(w, g, m, v):
    r, n = w.shape
    tr = _row_tile(r)

    def body(w_ref, g_ref, m_ref, v_ref, d_ref, nm_ref, nv_ref):
        gv = g_ref[...]
        nm = ADAM_B1 * m_ref[...] + (1.0 - ADAM_B1) * gv
        nv = ADAM_B2 * v_ref[...] + (1.0 - ADAM_B2) * (gv * gv)
        m_hat = nm / (1.0 - ADAM_B1 ** ADAM_STEP)
        v_hat = nv / (1.0 - ADAM_B2 ** ADAM_STEP)
        d_ref[...] = -ADAM_LR * (m_hat / (jnp.sqrt(v_hat) + ADAM_EPS) + ADAM_WD * w_ref[...])
        nm_ref[...] = nm
        nv_ref[...] = nv

    spec = pl.BlockSpec((tr, n), lambda i: (i, 0))
    return _call(body, name="adamw", grid=(r // tr,), in_specs=[spec] * 4, out_specs=[spec] * 3,
                 out_shape=[_sds((r, n))] * 3, sem=("parallel",))(w, g, m, v)


BIG = ("w_in", "w_branch_a", "w_branch_b", "w_out", "w_up", "w_down", "conv_w")
BIG_ROWS = 3584
SMALL = ("b_mod", "b_in", "conv_b", "ln1_g", "ln1_b", "ln2_g", "ln2_b", "c_ctx", "attn_sink", "q_norm_g", "k_norm_g")
SMALL_ROWS = 8 * len(SMALL)


def _rows(a, n_rows):
    flat = a.reshape(-1)
    return jnp.pad(flat, (0, n_rows * D_MODEL - flat.shape[0])).reshape(n_rows, D_MODEL)


def _group8(a):
    return _rep8(_rows(a, 1)) if a.size <= D_MODEL else _rows(a, 8)


def _ungroup8(p, shape):
    size = math.prod(shape)
    return (p[0, :size] if size <= D_MODEL else p.reshape(-1)[:size]).reshape(shape)


def _pack_big(t):
    parts = [t[n].reshape(-1, D_MODEL) for n in BIG[:-1]] + [_rows(t["conv_w"], 8)]
    used = sum(p.shape[0] for p in parts)
    return jnp.concatenate(parts + [jnp.zeros((BIG_ROWS - used, D_MODEL), F32)], axis=0)


def _unpack_big(p, like):
    out, r = {}, 0
    for n in BIG:
        size = math.prod(like[n].shape)
        nr = size // D_MODEL if n != "conv_w" else 8
        out[n] = p[r:r + nr].reshape(-1)[:size].reshape(like[n].shape)
        r += nr
    return out


def _pack_small(t):
    return jnp.concatenate([_group8(t[n]) for n in SMALL], axis=0)


def _unpack_small(p, like):
    return {n: _ungroup8(p[8 * i:8 * i + 8], like[n].shape) for i, n in enumerate(SMALL)}


WEIGHTS = ("c_ctx", "w_mod", "b_mod", "w_in", "b_in", "attn_sink", "q_norm_g", "k_norm_g", "w_branch_a", "w_branch_b",
           "w_out", "ln1_g", "ln1_b", "w_up", "conv_w", "conv_b", "w_down", "ln2_g", "ln2_b")


def kernel(x, c, ctx, c_ctx, w_mod, b_mod, w_in, b_in, attn_sink, q_norm_g, k_norm_g, w_branch_a, w_branch_b, w_out, ln1_g, ln1_b, w_up, conv_w, conv_b, w_down, ln2_g, ln2_b, loss_target, m_c_ctx, m_w_mod, m_b_mod, m_w_in, m_b_in, m_attn_sink, m_q_norm_g, m_k_norm_g, m_w_branch_a, m_w_branch_b, m_w_out, m_ln1_g, m_ln1_b, m_w_up, m_conv_w, m_conv_b, m_w_down, m_ln2_g, m_ln2_b, v_c_ctx, v_w_mod, v_b_mod, v_w_in, v_b_in, v_attn_sink, v_q_norm_g, v_k_norm_g, v_w_branch_a, v_w_branch_b, v_w_out, v_ln1_g, v_ln1_b, v_w_up, v_conv_w, v_conv_b, v_w_down, v_ln2_g, v_ln2_b):
    w = dict(c_ctx=c_ctx, w_mod=w_mod, b_mod=b_mod, w_in=w_in, b_in=b_in, attn_sink=attn_sink, q_norm_g=q_norm_g,
             k_norm_g=k_norm_g, w_branch_a=w_branch_a, w_branch_b=w_branch_b, w_out=w_out, ln1_g=ln1_g, ln1_b=ln1_b,
             w_up=w_up, conv_w=conv_w, conv_b=conv_b, w_down=w_down, ln2_g=ln2_g, ln2_b=ln2_b)
    m = dict(c_ctx=m_c_ctx, w_mod=m_w_mod, b_mod=m_b_mod, w_in=m_w_in, b_in=m_b_in, attn_sink=m_attn_sink,
             q_norm_g=m_q_norm_g, k_norm_g=m_k_norm_g, w_branch_a=m_w_branch_a, w_branch_b=m_w_branch_b, w_out=m_w_out,
             ln1_g=m_ln1_g, ln1_b=m_ln1_b, w_up=m_w_up, conv_w=m_conv_w, conv_b=m_conv_b, w_down=m_w_down,
             ln2_g=m_ln2_g, ln2_b=m_ln2_b)
    v = dict(c_ctx=v_c_ctx, w_mod=v_w_mod, b_mod=v_b_mod, w_in=v_w_in, b_in=v_b_in, attn_sink=v_attn_sink,
             q_norm_g=v_q_norm_g, k_norm_g=v_k_norm_g, w_branch_a=v_w_branch_a, w_branch_b=v_w_branch_b, w_out=v_w_out,
             ln1_g=v_ln1_g, ln1_b=v_ln1_b, w_up=v_w_up, conv_w=v_conv_w, conv_b=v_conv_b, w_down=v_w_down,
             ln2_g=v_ln2_g, ln2_b=v_ln2_b)
    xp, yp, _ = _mesh_pos()
    me = 2 * xp + yp

    branches = jnp.concatenate([w_branch_a[0], w_branch_b[0]], axis=0)
    wide = jnp.concatenate([w_mod[0], w_in[0], w_up[0], branches], axis=1).astype(BF16)
    tall = jnp.concatenate([w_out[0], w_down[0]], axis=0).astype(BF16)
    wide4, tall4, cw4 = _gather_shards([wide, tall], conv_w[0])
    n_mod, n_in, n_up = w_mod.shape[-1], w_in.shape[-1], w_up.shape[-1]
    wmod4 = wide4[:, :, :n_mod]
    win4 = wide4[:, :, n_mod:n_mod + n_in]
    wup4 = wide4[:, :, n_mod + n_in:n_mod + n_in + n_up]
    br4 = wide4[:, :, n_mod + n_in + n_up:]
    n_br = w_branch_a.shape[1]
    wba = br4[:, :n_br].transpose(1, 0, 2).reshape(n_br, D_MODEL)
    wbb = br4[:, n_br:].transpose(1, 0, 2).reshape(n_br, D_MODEL)
    n_out = w_out.shape[1]
    w_out_full = tall4[:, :n_out].reshape(D_MODEL, D_MODEL)
    w_down_full = tall4[:, n_out:].reshape(D_FF, D_MODEL)
    cw_full = cw4.transpose(1, 0, 2).reshape(3, 2 * D_FF)

    loss, grad_x, g = _local_step(
        x[0], c, ctx[0], c_ctx[None], wmod4, b_mod, win4, b_in, attn_sink[0], q_norm_g, k_norm_g, wba, wbb, w_out_full,
        ln1_g, ln1_b, wup4, cw_full, conv_b, w_down_full, ln2_g, ln2_b, loss_target[0])
    loss = lax.psum(loss, ("x", "y", "c"))

    sent = dict(c=c, dmod=g["dmod"], dmodc=g["dmodc"], b_in=g["b_in"], conv_b=g["conv_b"], ln1_g=g["ln1_g"],
                ln1_b=g["ln1_b"], ln2_g=g["ln2_g"], ln2_b=g["ln2_b"], c_ctx=g["c_ctx"], attn_sink=g["sink"],
                q_norm_g=g["qn"], k_norm_g=g["kn"])
    every = _allgather_rows(jnp.concatenate([_group8(a) for a in sent.values()], axis=0))
    total = _sum_leading(every, name="sum_devices")
    slot = {n: slice(8 * i, 8 * i + 8) for i, n in enumerate(sent)}
    gs = {n: _ungroup8(total[slot[n]], sent[n].shape) for n in SMALL if n in sent}
    dmodc_sum = jnp.concatenate([_ungroup8(total[slot["dmodc"]], (1, 2 * D_MODEL)), jnp.zeros((1, 4 * D_MODEL), F32)],
                                axis=1)
    gs["b_mod"] = _ungroup8(total[slot["dmod"]], b_mod.shape) + dmodc_sum
    acts = jnp.concatenate([every[:, slot["c"].start], _rep8(c_ctx)], axis=0)
    dmods = jnp.concatenate([every[:, slot["dmod"]].reshape(N_DEV, -1)[:, :6 * D_MODEL], _first_row(dmodc_sum)], axis=0)
    g_w_mod = _silu_outer(acts, lax.dynamic_slice_in_dim(dmods, me * n_mod, n_mod, axis=1))

    cw_g4 = _to_blocks4(g["conv_w"])
    packed = jnp.concatenate([
        g["w_in4"].reshape(N_CHIPS, -1, D_MODEL), _to_blocks4(g["wba"]).reshape(N_CHIPS, -1, D_MODEL),
        _to_blocks4(g["wbb"]).reshape(N_CHIPS, -1, D_MODEL), g["w_out"].reshape(N_CHIPS, -1, D_MODEL),
        g["w_up4"].reshape(N_CHIPS, -1, D_MODEL), g["w_down"].reshape(N_CHIPS, -1, D_MODEL),
        jnp.pad(cw_g4.reshape(N_CHIPS, -1), ((0, 0), (0, 8 * D_MODEL - cw_g4.shape[1] * cw_g4.shape[2]))).reshape(
            N_CHIPS, 8, D_MODEL),
        jnp.zeros((N_CHIPS, BIG_ROWS - 3528, D_MODEL), F32)], axis=1)
    rh = BIG_ROWS // 2
    cpos = lax.axis_index("c")
    my_half = lax.dynamic_slice_in_dim(packed, cpos * rh, rh, axis=1)
    chip_sum = _add_blocks(my_half, _swap_other_half(packed), BF16)
    half_sum = _sum_leading(_scatter_to_chips(chip_sum), name="sum_chips")
    g_big = _unpack_big(_join_halves(half_sum), w)

    grads = dict(gs, w_mod=g_w_mod, **g_big)

    def pack_all(t):
        rows = jnp.concatenate([_pack_big(t), t["w_mod"].reshape(-1, D_MODEL), _pack_small(t)], axis=0)
        return jnp.pad(rows, ((0, -rows.shape[0] % 256), (0, 0)))

    delta_p, new_m_p, new_v_p = _adamw(pack_all(w), pack_all(grads), pack_all(m), pack_all(v))

    def unpack_all(p):
        r_mod = BIG_ROWS + w_mod.size // D_MODEL
        out = _unpack_big(p[:BIG_ROWS], w)
        out["w_mod"] = p[BIG_ROWS:r_mod].reshape(w_mod.shape)
        out.update(_unpack_small(p[r_mod:r_mod + SMALL_ROWS], w))
        return out

    grads = {n: grads[n].reshape(w[n].shape) for n in WEIGHTS}
    delta, new_m, new_v = unpack_all(delta_p), unpack_all(new_m_p), unpack_all(new_v_p)
    return (loss, grad_x[None], *[grads[n] for n in WEIGHTS], *[delta[n] for n in WEIGHTS],
            *[new_m[n] for n in WEIGHTS], *[new_v[n] for n in WEIGHTS])
```

```python
import functools
import math

import jax
import jax.numpy as jnp
from jax import lax
from jax.experimental import pallas as pl
from jax.experimental.pallas import tpu as pltpu

F32 = jnp.float32
BF16 = jnp.bfloat16

D_MODEL = 1024
HEAD_DIM = 64
N_HEADS = 8
N_KV = 2
WINDOW = 128
GRID_W = 64
ROPE_THETA = 10000.0
D_FF = 2816
LN_EPS = 1e-5
QK_EPS = 1e-6
ALPHA = 2.0 ** 0.25
Q_SCALE = HEAD_DIM ** -0.5
OFF_GA = 1536
IN_COLS = 3584
ADAM_LR, ADAM_B1, ADAM_B2, ADAM_EPS, ADAM_WD, ADAM_STEP = 0.001, 0.9, 0.999, 1e-8, 0.01, 10

LANES = 128
VMEM_BUDGET = 52 * 1024 * 1024
N_CHIPS = 4
N_DEV = 8
NEG = -1e30
MESH = pl.DeviceIdType.MESH


def _sigmoid(x):
    return 1.0 / (1.0 + jnp.exp(-x))


def _dot(a, b):
    return jnp.dot(a, b, preferred_element_type=F32)


def _dot_nt(a, b):
    return lax.dot_general(a, b, (((1,), (1,)), ((), ())), preferred_element_type=F32)


def _dot_tn(a, b):
    return lax.dot_general(a, b, (((0,), (0,)), ((), ())), preferred_element_type=F32)


def _call(body, *, name, grid, in_specs, out_specs, out_shape, scratch=(), sem=None, **kw):
    params = dict(vmem_limit_bytes=VMEM_BUDGET)
    if sem is not None:
        params["dimension_semantics"] = sem
    return pl.pallas_call(body, name=name, grid=grid, in_specs=in_specs, out_specs=out_specs,
                          out_shape=out_shape, scratch_shapes=list(scratch),
                          compiler_params=pltpu.CompilerParams(**params), **kw)


def _full(shape):
    n = len(shape)
    return pl.BlockSpec(shape, lambda *_: (0,) * n)


def _sds(shape, dtype=F32):
    return jax.ShapeDtypeStruct(shape, dtype)


def _mm_nn4(a, shift, scale, w4, bias, *, mode, split_out, out_dtype, tm, name):
    m, kdim = a.shape
    nb, _, ns = w4.shape

    def body(a_ref, sh_ref, sc_ref, w_ref, b_ref, o_ref):
        av = a_ref[...]
        if mode == "modulate":
            av = av * (1.0 + sc_ref[...]) + sh_ref[...]
        else:
            av = av * _sigmoid(av)
        o_ref[...] = (_dot(av.astype(BF16), w_ref[...]) + b_ref[...]).astype(out_dtype)

    if split_out:
        out_shape = _sds((2, m, 2 * ns), out_dtype)
        out_spec = pl.BlockSpec((None, tm, ns), lambda i, k: (k // 2, i, k % 2))
    else:
        out_shape = _sds((m, nb * ns), out_dtype)
        out_spec = pl.BlockSpec((tm, ns), lambda i, k: (i, k))
    return _call(
        body, name=name, grid=(m // tm, nb),
        in_specs=[pl.BlockSpec((tm, kdim), lambda i, k: (i, 0)),
                  pl.BlockSpec((1, kdim), lambda i, k: (0, 0)),
                  pl.BlockSpec((1, kdim), lambda i, k: (0, 0)),
                  pl.BlockSpec((None, kdim, ns), lambda i, k: (k, 0, 0)),
                  pl.BlockSpec((1, ns), lambda i, k: (0, k))],
        out_specs=out_spec, out_shape=out_shape, sem=("parallel", "arbitrary"),
    )(a, shift, scale, w4, bias)


def _mm_tn(a, b, *, a_spec, b_spec, grid, out_shape, out_spec, name, mod=None, init=None, colsum_spec=None,
           colsum_shape=None, a_is_t=False):
    red = len(grid) - 1
    has_mod, has_init, has_cs = mod is not None, init is not None, colsum_spec is not None

    def body(*refs):
        refs = list(refs)
        a_ref, b_ref = refs[0], refs[1]
        pos = 2
        if has_mod:
            sh_ref, sc_ref = refs[2], refs[3]
            pos = 4
        if has_init:
            init_ref = refs[pos]
            pos += 1
        o_ref = refs[pos]
        cs_ref = refs[pos + 1] if has_cs else None
        s = pl.program_id(red)

        @pl.when(s == 0)
        def _():
            o_ref[...] = init_ref[...] if has_init else jnp.zeros(o_ref.shape, F32)
            if has_cs:
                cs_ref[...] = jnp.zeros(cs_ref.shape, F32)

        av = a_ref[...]
        if has_mod:
            av = av * (1.0 + sc_ref[...]) + sh_ref[...]
        bv = b_ref[...]
        o_ref[...] += (_dot if a_is_t else _dot_tn)(av.astype(BF16), bv)
        if has_cs:
            cs_ref[...] += jnp.broadcast_to(jnp.sum(bv.astype(F32), axis=0, keepdims=True), cs_ref.shape)

    ins, in_specs = [a, b], [a_spec, b_spec]
    if has_mod:
        kdim = mod[0].shape[-1]
        ins += list(mod)
        in_specs += [_full((1, kdim)), _full((1, kdim))]
    if has_init:
        ins.append(init)
        in_specs.append(out_spec)
    out_specs, out_shapes = out_spec, out_shape
    if has_cs:
        out_specs, out_shapes = [out_spec, colsum_spec], [out_shape, colsum_shape]
    sem = ("parallel",) * red + ("arbitrary",)
    return _call(body, name=name, grid=grid, in_specs=in_specs, out_specs=out_specs, out_shape=out_shapes,
                 sem=sem)(*ins)


def _rope_tables(n_tok):
    pos = jnp.arange(n_tok, dtype=jnp.int32)
    rows = (pos // GRID_W).astype(F32)
    cols = (pos % GRID_W).astype(F32)
    n_freq = HEAD_DIM // 4
    inv_freq = ROPE_THETA ** (-jnp.arange(n_freq, dtype=F32) / n_freq)
    ang_r = rows[:, None] * inv_freq
    ang_c = cols[:, None] * inv_freq
    cos = jnp.concatenate([jnp.cos(ang_r)] * 2 + [jnp.cos(ang_c)] * 2, axis=-1)
    sin = jnp.concatenate([-jnp.sin(ang_r), jnp.sin(ang_r), -jnp.sin(ang_c), jnp.sin(ang_c)], axis=-1)
    return jnp.tile(cos, (1, 2)), jnp.tile(sin, (1, 2))


def _lane(shape):
    return lax.broadcasted_iota(jnp.int32, shape, 1)


def _rope_partner(t, lane):
    return jnp.where((lane % 32) < 16, pltpu.roll(t, LANES - 16, 1), pltpu.roll(t, 16, 1))


def _half_mean(s, lane):
    lo = jnp.sum(jnp.where(lane < HEAD_DIM, s, 0.0), axis=-1, keepdims=True)
    hi = jnp.sum(jnp.where(lane < HEAD_DIM, 0.0, s), axis=-1, keepdims=True)
    return jnp.where(lane < HEAD_DIM, lo, hi) * (1.0 / HEAD_DIM)


def _prep(proj, cos, sin, qg, kg, *, tm, name):
    m = proj.shape[0]

    def body(p_ref, cos_ref, sin_ref, qg_ref, kg_ref, qa_ref, ka_ref, va_ref, qb_ref, kb_ref, vb_ref):
        lane = _lane((tm, LANES))
        cosv, sinv = cos_ref[...], sin_ref[...]
        low = lane < HEAD_DIM

        def rope(t):
            return t * cosv + _rope_partner(t, lane) * sinv

        def rms(t, g):
            return t * lax.rsqrt(_half_mean(t * t, lane) + QK_EPS) * g

        def place(q_ref, j, chunk):
            sw = pltpu.roll(chunk, HEAD_DIM, 1)
            if j < 2:
                h0, h1 = jnp.where(low, chunk, 0.0), jnp.where(low, sw, 0.0)
            else:
                h0, h1 = jnp.where(low, 0.0, sw), jnp.where(low, 0.0, chunk)
            q_ref[2 * j] = h0.astype(BF16)
            q_ref[2 * j + 1] = h1.astype(BF16)

        for j in range(4):
            place(qa_ref, j, rope(p_ref[:, j * LANES:(j + 1) * LANES]) * Q_SCALE)
            place(qb_ref, j, rope(rms(p_ref[:, 768 + j * LANES:768 + (j + 1) * LANES], qg_ref[...])) * Q_SCALE)
        ka_ref[...] = rope(p_ref[:, 512:640]).astype(BF16)
        va_ref[...] = p_ref[:, 640:768].astype(BF16)
        kb_ref[...] = rope(rms(p_ref[:, 1280:1408], kg_ref[...])).astype(BF16)
        vb_ref[...] = p_ref[:, 1408:1536].astype(BF16)

    row = pl.BlockSpec((tm, LANES), lambda i: (i, 0))
    qspec = pl.BlockSpec((N_HEADS, tm, LANES), lambda i: (0, i, 0))
    return _call(
        body, name=name, grid=(m // tm,),
        in_specs=[pl.BlockSpec((tm, OFF_GA), lambda i: (i, 0)), row, row, _full((1, LANES)), _full((1, LANES))],
        out_specs=[qspec, row, row, qspec, row, row],
        out_shape=[_sds((N_HEADS, m, LANES), BF16), _sds((m, LANES), BF16), _sds((m, LANES), BF16),
                   _sds((N_HEADS, m, LANES), BF16), _sds((m, LANES), BF16), _sds((m, LANES), BF16)],
        sem=("parallel",),
    )(proj, cos, sin, qg, kg)


def _prep_bwd(dqa, dka, dva, dqb, dkb, dvb, proj, cos, sin, qg, kg, dgl, *, tm, name):
    m = proj.shape[0]

    def body(dqa_ref, dka_ref, dva_ref, dqb_ref, dkb_ref, dvb_ref, p_ref, cos_ref, sin_ref, qg_ref, kg_ref,
             dgl_ref, dp_ref, dqg_ref, dkg_ref):
        i = pl.program_id(0)
        lane = _lane((tm, LANES))
        cosv, sinv = cos_ref[...], sin_ref[...]
        low = lane < HEAD_DIM

        @pl.when(i == 0)
        def _():
            dqg_ref[...] = jnp.zeros(dqg_ref.shape, F32)
            dkg_ref[...] = jnp.zeros(dkg_ref.shape, F32)

        def unrope(d):
            return d * cosv - _rope_partner(d, lane) * sinv

        def unplace(dq_ref, j):
            d0, d1 = dq_ref[2 * j], dq_ref[2 * j + 1]
            if j < 2:
                return jnp.where(low, d0, pltpu.roll(d1, HEAD_DIM, 1))
            return jnp.where(low, pltpu.roll(d0, HEAD_DIM, 1), d1)

        def unrms(dtn, t, g):
            r = lax.rsqrt(_half_mean(t * t, lane) + QK_EPS)
            u = dtn * g
            dt = r * u - t * (r * r * r) * _half_mean(u * t, lane)
            return dt, jnp.sum(dtn * t * r, axis=0, keepdims=True)

        for j in range(4):
            dp_ref[:, j * LANES:(j + 1) * LANES] = (unrope(unplace(dqa_ref, j)) * Q_SCALE).astype(BF16)
            c0 = 768 + j * LANES
            dt, dg = unrms(unrope(unplace(dqb_ref, j)) * Q_SCALE, p_ref[:, c0:c0 + LANES], qg_ref[...])
            dp_ref[:, c0:c0 + LANES] = dt.astype(BF16)
            dqg_ref[:, j * LANES:(j + 1) * LANES] += dg
        dp_ref[:, 512:640] = unrope(dka_ref[...]).astype(BF16)
        dp_ref[:, 640:768] = dva_ref[...].astype(BF16)
        dt, dg = unrms(unrope(dkb_ref[...]), p_ref[:, 1280:1408], kg_ref[...])
        dp_ref[:, 1280:1408] = dt.astype(BF16)
        dkg_ref[...] += dg
        dp_ref[:, 1408:1536] = dvb_ref[...].astype(BF16)
        dp_ref[:, OFF_GA:] = dgl_ref[...]

    row = pl.BlockSpec((tm, LANES), lambda i: (i, 0))
    qspec = pl.BlockSpec((N_HEADS, tm, LANES), lambda i: (0, i, 0))
    return _call(
        body, name=name, grid=(m // tm,),
        in_specs=[qspec, row, row, qspec, row, row, pl.BlockSpec((tm, OFF_GA), lambda i: (i, 0)), row, row,
                  _full((1, LANES)), _full((1, LANES)), pl.BlockSpec((tm, IN_COLS - OFF_GA), lambda i: (i, 0))],
        out_specs=[pl.BlockSpec((tm, IN_COLS), lambda i: (i, 0)), _full((1, 512)), _full((1, LANES))],
        out_shape=[_sds((m, IN_COLS), BF16), _sds((1, 512)), _sds((1, LANES))],
        sem=("arbitrary",),
    )(dqa, dka, dva, dqb, dkb, dvb, proj, cos, sin, qg, kg, dgl)


def _attn_glob_fwd(qt, k, vt, kc, vct, *, tq, tk):
    nh, _, s = qt.shape
    nc = kc.shape[0]
    n_chunks = s // tk
    half = LANES // 2

    def body(qt_ref, k_ref, vt_ref, kc_ref, vct_ref, ot_ref, lse_ref, mrun_ref, p_hbm,
             acc_sc, st_sc, stage_sc, stagec_sc, sems, semc):
        h, i = pl.program_id(0), pl.program_id(1)
        qtv = qt_ref[...]
        acc_sc[...] = jnp.zeros(acc_sc.shape, F32)

        def p_out(slot, c):
            return pltpu.make_async_copy(stage_sc.at[slot], p_hbm.at[h, i, pl.ds(pl.multiple_of(c * tk, tk), tk), :],
                                         sems.at[slot])

        def update(st, vtv, m_old):
            m_new = jnp.maximum(m_old, jnp.max(st, axis=0, keepdims=True))
            pb = jnp.exp(st - m_new).astype(BF16)
            acc_sc[...] = acc_sc[...] * jnp.exp(m_old - m_new) + _dot(vtv, pb)
            return m_new, pb

        m, pbc = update(_dot(kc_ref[...], qtv), vct_ref[...], jnp.full((1, tq), NEG, F32))
        mrun_ref[pl.ds(n_chunks, 1), :] = m
        stagec_sc[...] = pbc
        ctx_out = pltpu.make_async_copy(stagec_sc, p_hbm.at[h, i, pl.ds(s, nc), :], semc)
        ctx_out.start()

        def step(c, st, m_old):
            slot = c % 2
            off = pl.multiple_of(c * tk, tk)
            nxt = pl.multiple_of(jnp.minimum(c + 1, n_chunks - 1) * tk, tk)
            st_next = _dot(k_ref[pl.ds(nxt, tk), :], qtv)
            m_new, pb = update(st, vt_ref[:, pl.ds(off, tk)], m_old)
            mrun_ref[pl.ds(c, 1), :] = m_new
            stage_sc[slot] = pb
            p_out(slot, c).start()
            return st_next, m_new

        def loop(c, m_old):
            st_next, m_new = step(c, st_sc[...], m_old)
            p_out(1 - c % 2, c - 1).wait()
            st_sc[...] = st_next
            return m_new

        stage_sc[1] = jnp.zeros((tk, tq), BF16)
        pltpu.make_async_copy(stage_sc.at[1], p_hbm.at[h, i, pl.ds(s + nc, tk), :], sems.at[1]).start()
        st_sc[...] = _dot(k_ref[pl.ds(0, tk), :], qtv)
        m = lax.fori_loop(0, n_chunks, loop, m)
        p_out((n_chunks - 1) % 2, n_chunks - 1).wait()
        ctx_out.wait()
        acc = acc_sc[...]
        l = jnp.where(h < nh // N_KV, acc[half:half + 1], acc[0:1])
        ot_ref[...] = (acc / l).astype(BF16)
        lse_ref[...] = m + jnp.log(l)

    grp = nh // N_KV
    return _call(
        body, name="attn_glob_fwd", grid=(nh, s // tq),
        in_specs=[pl.BlockSpec((None, LANES, tq), lambda h, i: (h, 0, i)), _full((s, LANES)),
                  pl.BlockSpec((None, LANES, s), lambda h, i: (h // grp, 0, 0)), _full((nc, LANES)),
                  pl.BlockSpec((None, LANES, nc), lambda h, i: (h // grp, 0, 0))],
        out_specs=[pl.BlockSpec((None, LANES, tq), lambda h, i: (h, 0, i)),
                   pl.BlockSpec((None, 1, tq), lambda h, i: (h, 0, i)),
                   pl.BlockSpec((None, n_chunks + 1, tq), lambda h, i: (h, 0, i)), ANY],
        out_shape=[_sds((nh, LANES, s), BF16), _sds((nh, 1, s)), _sds((nh, n_chunks + 1, s)),
                   _sds((nh, s // tq, s + nc + tk, tq), BF16)],
        scratch=[pltpu.VMEM((LANES, tq), F32), pltpu.VMEM((tk, tq), F32), pltpu.VMEM((2, tk, tq), BF16),
                 pltpu.VMEM((nc, tq), BF16), pltpu.SemaphoreType.DMA((2,)), pltpu.SemaphoreType.DMA],
        sem=("parallel", "parallel"),
    )(qt, k, vt, kc, vct)


P_AHEAD = 3


def _attn_glob_bwd(qt, dot, ot, lse, mrun, p, kt, v, kct, vc, *, tq, tk):
    nh, _, s = qt.shape
    nc = vc.shape[0]
    n_q = s // tq
    n_chunks = s // tk
    n_run = mrun.shape[1] - 1
    per_run = n_chunks // n_run

    def body(qt_ref, dot_ref, ot_ref, lse_ref, mrun_ref, p_hbm, kt_ref, v_ref, kct_ref, vc_ref,
             dqt_ref, dkt_ref, dvt_ref, dkct_ref, dvct_ref, acc_sc, dp_sc, dkt_sc, dvt_sc, p_sc, pc_sc, sems, semc):
        h, i = pl.program_id(0), pl.program_id(1)

        @pl.when(jnp.logical_and(h == 0, i == 0))
        def _():
            dkct_ref[...] = jnp.zeros(dkct_ref.shape, F32)
            dvct_ref[...] = jnp.zeros(dvct_ref.shape, F32)
            dkt_sc[...] = jnp.zeros(dkt_sc.shape, F32)
            dvt_sc[...] = jnp.zeros(dvt_sc.shape, F32)


        def p_in(slot, c):
            return pltpu.make_async_copy(p_hbm.at[h, i, pl.ds(pl.multiple_of(c * tk, tk), tk), :], p_sc.at[slot],
                                         sems.at[slot])

        ctx_in = pltpu.make_async_copy(p_hbm.at[h, i, pl.ds(s, nc), :], pc_sc, semc)
        ctx_in.start()
        for c in range(P_AHEAD):
            p_in(c, min(c, n_chunks - 1)).start()
        qtv, dotv, lse = qt_ref[...], dot_ref[...], lse_ref[...]
        delta = jnp.sum(dotv.astype(F32) * ot_ref[...].astype(F32), axis=0, keepdims=True)

        def grads(pt_stored, m_row, dpt):
            pt = pt_stored.astype(F32) * jnp.exp(m_row - lse)
            return pt.astype(BF16), (pt * (dpt - delta)).astype(BF16)

        dp_sc[...] = _dot(v_ref[pl.ds(0, tk), :], dotv)
        ctx_in.wait()
        pb, dsb = grads(pc_sc[...], mrun_ref[pl.ds(n_run, 1), :], _dot(vc_ref[...], dotv))
        acc_sc[...] = _dot(kct_ref[...], dsb)
        dkct_ref[...] += _dot_nt(qtv, dsb)
        dvct_ref[...] += _dot_nt(dotv, pb)

        def loop(c, carry):
            slot = c % (P_AHEAD + 1)
            off = pl.multiple_of(c * tk, tk)
            nxt = pl.multiple_of(jnp.minimum(c + 1, n_chunks - 1) * tk, tk)
            p_in(slot, c).wait()
            p_in((c + P_AHEAD) % (P_AHEAD + 1), jnp.minimum(c + P_AHEAD, n_chunks - 1)).start()
            dpt = dp_sc[...]
            dp_next = _dot(v_ref[pl.ds(nxt, tk), :], dotv)
            pb, dsb = grads(p_sc[slot], mrun_ref[pl.ds(c // per_run, 1), :], dpt)
            acc_sc[...] += _dot(kt_ref[:, pl.ds(off, tk)], dsb)
            dkt_sc[:, pl.ds(off, tk)] += _dot_nt(qtv, dsb)
            dvt_sc[:, pl.ds(off, tk)] += _dot_nt(dotv, pb)
            dp_sc[...] = dp_next
            return carry

        lax.fori_loop(0, n_chunks, loop, 0)
        for c in range(n_chunks, n_chunks + P_AHEAD):
            p_in(c % (P_AHEAD + 1), n_chunks - 1).wait()
        dqt_ref[...] = acc_sc[...]

        @pl.when(jnp.logical_and(h == nh - 1, i == n_q - 1))
        def _():
            pltpu.sync_copy(dkt_sc, dkt_ref)
            pltpu.sync_copy(dvt_sc, dvt_ref)

    qs = pl.BlockSpec((None, LANES, tq), lambda h, i: (h, 0, i))
    rs = pl.BlockSpec((None, 1, tq), lambda h, i: (h, 0, i))
    return _call(
        body, name="attn_glob_bwd", grid=(nh, n_q),
        in_specs=[qs, qs, qs, rs, pl.BlockSpec((None, n_run + 1, tq), lambda h, i: (h, 0, i)), ANY,
                  _full((LANES, s)), _full((s, LANES)), _full((LANES, nc)), _full((nc, LANES))],
        out_specs=[qs, ANY, ANY, _full((LANES, nc)), _full((LANES, nc))],
        out_shape=[_sds((nh, LANES, s)), _sds((LANES, s)), _sds((LANES, s)), _sds((LANES, nc)), _sds((LANES, nc))],
        scratch=[pltpu.VMEM((LANES, tq), F32), pltpu.VMEM((tk, tq), F32), pltpu.VMEM((LANES, s), F32),
                 pltpu.VMEM((LANES, s), F32), pltpu.VMEM((P_AHEAD + 1, tk, tq), BF16), pltpu.VMEM((nc, tq), BF16),
                 pltpu.SemaphoreType.DMA((P_AHEAD + 1,)), pltpu.SemaphoreType.DMA],
        sem=("arbitrary", "arbitrary"),
    )(qt, dot, ot, lse, mrun, p, kt, v, kct, vc)


WIN_SPAN = 2 * WINDOW


def _band(rows0, cols0, shape):
    r = rows0 + lax.broadcasted_iota(jnp.int32, shape, 0)
    c = cols0 + lax.broadcasted_iota(jnp.int32, shape, 1)
    return jnp.abs(r - c) <= WINDOW


def _win_start(blk, t, s):
    return pl.multiple_of(jnp.clip(blk * t - WINDOW, 0, s - t - WIN_SPAN), WINDOW)


def _attn_win_fwd(q, k, v, kc, vc, sink, *, tq):
    nh, s, _ = q.shape
    nc = kc.shape[0]
    tw = tq + WIN_SPAN

    def body(sink_ref, q_ref, k_ref, v_ref, kc_ref, vc_ref, o_ref, lse_ref):
        h, i = pl.program_id(0), pl.program_id(1)
        k0 = _win_start(i, tq, s)
        qv = q_ref[...]
        kv, vv = k_ref[pl.ds(k0, tw), :], v_ref[pl.ds(k0, tw), :]
        sc = jnp.where(_band(i * tq, k0, (tq, tw)), _dot_nt(qv, kv), NEG)
        scc = _dot_nt(qv, kc_ref[...])
        snk = sink_ref[h]
        m = jnp.maximum(jnp.maximum(jnp.max(sc, axis=-1, keepdims=True), jnp.max(scc, axis=-1, keepdims=True)), snk)
        p, pc = jnp.exp(sc - m), jnp.exp(scc - m)
        l = jnp.sum(p, axis=-1, keepdims=True) + jnp.sum(pc, axis=-1, keepdims=True) + jnp.exp(snk - m)
        acc = _dot(p.astype(BF16), vv) + _dot(pc.astype(BF16), vc_ref[...])
        o_ref[...] = (acc / l).astype(BF16)
        lse_ref[...] = m + jnp.log(l)

    return _call(
        body, name="attn_win_fwd", grid=(nh, s // tq),
        in_specs=[pl.BlockSpec(memory_space=pltpu.SMEM),
                  pl.BlockSpec((None, tq, LANES), lambda h, i: (h, i, 0)),
                  _full((s, LANES)), _full((s, LANES)), _full((nc, LANES)), _full((nc, LANES))],
        out_specs=[pl.BlockSpec((None, tq, LANES), lambda h, i: (h, i, 0)),
                   pl.BlockSpec((None, tq, 1), lambda h, i: (h, i, 0))],
        out_shape=[_sds((nh, s, LANES), BF16), _sds((nh, s, 1))],
        sem=("parallel", "parallel"),
    )(sink, q, k, v, kc, vc)


def _attn_win_dq(q, do, o, lse, k, v, kc, vc, sink, *, tq):
    nh, s, _ = q.shape
    nc = kc.shape[0]
    tw = tq + WIN_SPAN
    nq = s // tq

    def body(sink_ref, q_ref, do_ref, o_ref, lse_ref, k_ref, v_ref, kc_ref, vc_ref,
             dq_ref, dl_ref, dkc_ref, dvc_ref, dsk_ref):
        h, i = pl.program_id(0), pl.program_id(1)

        @pl.when(jnp.logical_and(h == 0, i == 0))
        def _():
            dkc_ref[...] = jnp.zeros(dkc_ref.shape, F32)
            dvc_ref[...] = jnp.zeros(dvc_ref.shape, F32)

        k0 = _win_start(i, tq, s)
        qv, dov, lse = q_ref[...], do_ref[...], lse_ref[...]
        kv, vv = k_ref[pl.ds(k0, tw), :], v_ref[pl.ds(k0, tw), :]
        kcv, vcv = kc_ref[...], vc_ref[...]
        delta = jnp.sum(dov.astype(F32) * o_ref[...].astype(F32), axis=-1, keepdims=True)
        dl_ref[...] = delta
        p = jnp.where(_band(i * tq, k0, (tq, tw)), jnp.exp(_dot_nt(qv, kv) - lse), 0.0)
        ds = (p * (_dot_nt(dov, vv) - delta)).astype(BF16)
        pc = jnp.exp(_dot_nt(qv, kcv) - lse)
        dsc = (pc * (_dot_nt(dov, vcv) - delta)).astype(BF16)
        dq_ref[...] = _dot(ds, kv) + _dot(dsc, kcv)
        dkc_ref[...] += _dot_tn(dsc, qv)
        dvc_ref[...] += _dot_tn(pc.astype(BF16), dov)
        dsk = -jnp.sum(jnp.exp(sink_ref[h] - lse) * delta)
        dsk_ref[...] = jnp.full(dsk_ref.shape, dsk, F32)

    qs = pl.BlockSpec((None, tq, LANES), lambda h, i: (h, i, 0))
    cs = pl.BlockSpec((None, tq, 1), lambda h, i: (h, i, 0))
    return _call(
        body, name="attn_win_dq", grid=(nh, nq),
        in_specs=[pl.BlockSpec(memory_space=pltpu.SMEM), qs, qs, qs, cs,
                  _full((s, LANES)), _full((s, LANES)), _full((nc, LANES)), _full((nc, LANES))],
        out_specs=[qs, cs, _full((nc, LANES)), _full((nc, LANES)),
                   pl.BlockSpec((None, None, 8, LANES), lambda h, i: (h, i, 0, 0))],
        out_shape=[_sds((nh, s, LANES)), _sds((nh, s, 1)), _sds((nc, LANES)), _sds((nc, LANES)),
                   _sds((nh, nq, 8, LANES))],
        sem=("arbitrary", "arbitrary"),
    )(sink, q, do, o, lse, k, v, kc, vc)


def _attn_win_dkv(q, do, lse_row, dl_row, k, v, *, tk):
    nh, s, _ = q.shape
    tw = tk + WIN_SPAN

    def body(k_ref, v_ref, q_ref, do_ref, lse_ref, dl_ref, dk_ref, dv_ref, dk_sc, dv_sc):
        h, j = pl.program_id(0), pl.program_id(1)
        q0 = _win_start(j, tk, s)
        kv, vv = k_ref[...], v_ref[...]
        qv, dov = q_ref[pl.ds(q0, tw), :], do_ref[pl.ds(q0, tw), :]
        pt = jnp.where(_band(j * tk, q0, (tk, tw)), jnp.exp(_dot_nt(kv, qv) - lse_ref[:, pl.ds(q0, tw)]), 0.0)
        dst = pt * (_dot_nt(vv, dov) - dl_ref[:, pl.ds(q0, tw)])
        dk, dv = _dot(dst.astype(BF16), qv), _dot(pt.astype(BF16), dov)
        rows = pl.ds(pl.multiple_of(j * tk, tk), tk)

        @pl.when(h == 0)
        def _():
            dk_sc[rows, :] = dk
            dv_sc[rows, :] = dv

        @pl.when(h > 0)
        def _():
            dk_sc[rows, :] += dk
            dv_sc[rows, :] += dv

        @pl.when(jnp.logical_and(h == nh - 1, j == s // tk - 1))
        def _():
            pltpu.sync_copy(dk_sc, dk_ref)
            pltpu.sync_copy(dv_sc, dv_ref)

    ks = pl.BlockSpec((tk, LANES), lambda h, j: (j, 0))
    qs = pl.BlockSpec((None, s, LANES), lambda h, j: (h, 0, 0))
    rs = pl.BlockSpec((None, 1, s), lambda h, j: (h, 0, 0))
    return _call(
        body, name="attn_win_dkv", grid=(nh, s // tk),
        in_specs=[ks, ks, qs, qs, rs, rs], out_specs=[ANY, ANY],
        out_shape=[_sds((s, LANES)), _sds((s, LANES))],
        scratch=[pltpu.VMEM((s, LANES), F32), pltpu.VMEM((s, LANES), F32)],
        sem=("arbitrary", "arbitrary"),
    )(k, v, q, do, lse_row, dl_row)


def _ln_fwd(z, g, b):
    mu = jnp.mean(z, axis=-1, keepdims=True)
    zc = z - mu
    r = lax.rsqrt(jnp.mean(zc * zc, axis=-1, keepdims=True) + LN_EPS)
    return zc * r * g + b, mu, r


def _ln_bwd(dy, xhat, r, g):
    dxh = dy * g
    return r * (dxh - jnp.mean(dxh, axis=-1, keepdims=True) - xhat * jnp.mean(dxh * xhat, axis=-1, keepdims=True))


def _heads_matmul(o_ref, w_ref):
    acc = _dot(o_ref[0], w_ref[0])
    for h in range(1, N_HEADS):
        acc += _dot(o_ref[h], w_ref[h])
    return acc


def _gate_specs(tm):
    return [pl.BlockSpec((tm, 512), functools.partial(lambda i, b: (i, b), b=OFF_GA // 512 + b)) for b in range(4)]


def _merge_fwd(oa, ob, proj, x, gate1, wba, wbb, w_out, ln_g, ln_b, *, tm):
    s = x.shape[0]

    def body(oa_ref, ob_ref, g0, g1, g2, g3, x_ref, gt_ref, wba_ref, wbb_ref, wo_ref, lg_ref, lb_ref,
             x1_ref, y_ref, mu_ref, r_ref):
        ga = _sigmoid(jnp.concatenate([g0[...], g1[...]], axis=1))
        gb = _sigmoid(jnp.concatenate([g2[...], g3[...]], axis=1))
        merged = ga * _heads_matmul(oa_ref, wba_ref) + gb * _heads_matmul(ob_ref, wbb_ref)
        y = _dot(merged.astype(BF16), wo_ref[...])
        x1, mu, r = _ln_fwd(ALPHA * x_ref[...] + gt_ref[...] * y, lg_ref[...], lb_ref[...])
        x1_ref[...] = x1
        y_ref[...] = y
        mu_ref[...] = mu
        r_ref[...] = r

    hs = pl.BlockSpec((N_HEADS, tm, LANES), lambda i: (0, i, 0))
    row = pl.BlockSpec((tm, D_MODEL), lambda i: (i, 0))
    col = pl.BlockSpec((tm, 1), lambda i: (i, 0))
    vec = _full((1, D_MODEL))
    wh = _full((N_HEADS, LANES, D_MODEL))
    return _call(
        body, name="merge_fwd", grid=(s // tm,),
        in_specs=[hs, hs, *_gate_specs(tm), row, vec, wh, wh, _full((D_MODEL, D_MODEL)), vec, vec],
        out_specs=[row, row, col, col],
        out_shape=[_sds((s, D_MODEL)), _sds((s, D_MODEL)), _sds((s, 1)), _sds((s, 1))],
        sem=("parallel",),
    )(oa, ob, proj, proj, proj, proj, x, gate1, wba, wbb, w_out, ln_g, ln_b)


def _merge_bwd(dy, oa, ob, oat, obt, proj, wba, wbb, w_out, *, tm):
    s = dy.shape[0]

    def body(dy_ref, oa_ref, ob_ref, oat_ref, obt_ref, g0, g1, g2, g3, wba_ref, wbb_ref, wo_ref,
             dgl_ref, doa_ref, dobt_ref, mg_ref, dwa_ref, dwb_ref):
        @pl.when(pl.program_id(0) == 0)
        def _():
            dwa_ref[...] = jnp.zeros(dwa_ref.shape, F32)
            dwb_ref[...] = jnp.zeros(dwb_ref.shape, F32)

        dm = _dot_nt(dy_ref[...], wo_ref[...])
        ga = _sigmoid(jnp.concatenate([g0[...], g1[...]], axis=1))
        gb = _sigmoid(jnp.concatenate([g2[...], g3[...]], axis=1))
        pa, pb = _heads_matmul(oa_ref, wba_ref), _heads_matmul(ob_ref, wbb_ref)
        mg_ref[...] = (ga * pa + gb * pb).astype(BF16)
        dgl_ref[:, :D_MODEL] = (dm * pa * ga * (1.0 - ga)).astype(BF16)
        dgl_ref[:, D_MODEL:] = (dm * pb * gb * (1.0 - gb)).astype(BF16)
        dpa, dpb = (dm * ga).astype(BF16), (dm * gb).astype(BF16)
        for h in range(N_HEADS):
            doa_ref[h] = _dot_nt(dpa, wba_ref[h]).astype(BF16)
            dobt_ref[h] = _dot_nt(wbb_ref[h], dpb).astype(BF16)
            dwa_ref[h] += _dot(oat_ref[h], dpa)
            dwb_ref[h] += _dot(obt_ref[h], dpb)

    hs = pl.BlockSpec((N_HEADS, tm, LANES), lambda i: (0, i, 0))
    hts = pl.BlockSpec((N_HEADS, LANES, tm), lambda i: (0, 0, i))
    row = pl.BlockSpec((tm, D_MODEL), lambda i: (i, 0))
    wh = _full((N_HEADS, LANES, D_MODEL))
    return _call(
        body, name="merge_bwd", grid=(s // tm,),
        in_specs=[row, hs, hs, hts, hts, *_gate_specs(tm), wh, wh, _full((D_MODEL, D_MODEL))],
        out_specs=[pl.BlockSpec((tm, 2 * D_MODEL), lambda i: (i, 0)), hs, hts, row, wh, wh],
        out_shape=[_sds((s, 2 * D_MODEL), BF16), _sds((N_HEADS, s, LANES), BF16), _sds((N_HEADS, LANES, s), BF16),
                   _sds((s, D_MODEL), BF16), _sds((N_HEADS, LANES, D_MODEL)), _sds((N_HEADS, LANES, D_MODEL))],
        sem=("arbitrary",),
    )(dy, oa, ob, oat, obt, proj, proj, proj, proj, wba, wbb, w_out)


FF_TC = 256


def _shift_rows(t, prev_row, next_row):
    n = t.shape[0]
    r = lax.broadcasted_iota(jnp.int32, t.shape, 0)
    up = jnp.where(r == 0, prev_row, pltpu.roll(t, 1, 0))
    dn = jnp.where(r == n - 1, next_row, pltpu.roll(t, n - 1, 0))
    return up, dn


def _halo_specs(tm, s, tc):
    nb8 = s // 8
    main = pl.BlockSpec((2, tm, tc), lambda j, i: (0, i, j))
    prev = pl.BlockSpec((2, 8, tc), lambda j, i: (0, jnp.maximum(i * (tm // 8) - 1, 0), j))
    nxt = pl.BlockSpec((2, 8, tc), lambda j, i: (0, jnp.minimum((i + 1) * (tm // 8), nb8 - 1), j))
    return main, prev, nxt


def _halo_rows(prev_ref, next_ref, half, i, n_i):
    prev_row = jnp.where(i == 0, 0.0, prev_ref[half, 7:8, :].astype(F32))
    next_row = jnp.where(i == n_i - 1, 0.0, next_ref[half, 0:1, :].astype(F32))
    return prev_row, next_row


def _conv(t, prev_row, next_row, w, b):
    up, dn = _shift_rows(t, prev_row, next_row)
    return w[0:1, :] * up + w[1:2, :] * t + w[2:3, :] * dn + b


def _ffn_act_fwd(u, cw, cb, *, tm):
    _, s, ff = u.shape
    n_i = s // tm

    def body(u_ref, up_ref, un_ref, cw_ref, cb_ref, a_ref):
        i = pl.program_id(1)
        gc = _conv(u_ref[0], *_halo_rows(up_ref, un_ref, 0, i, n_i), cw_ref[0], cb_ref[0])
        vc = _conv(u_ref[1], *_halo_rows(up_ref, un_ref, 1, i, n_i), cw_ref[1], cb_ref[1])
        a_ref[...] = (gc * _sigmoid(gc) * vc).astype(BF16)

    main, prev, nxt = _halo_specs(tm, s, FF_TC)
    return _call(
        body, name="ffn_act_fwd", grid=(ff // FF_TC, n_i),
        in_specs=[main, prev, nxt, pl.BlockSpec((2, 3, FF_TC), lambda j, i: (0, 0, j)),
                  pl.BlockSpec((2, 1, FF_TC), lambda j, i: (0, 0, j))],
        out_specs=pl.BlockSpec((tm, FF_TC), lambda j, i: (i, j)),
        out_shape=_sds((s, ff), BF16), sem=("parallel", "parallel"),
    )(u, u, u, cw, cb)


def _ffn_act_bwd(dy2, w_down, u, cw, cb, *, tm):
    _, s, ff = u.shape
    n_i = s // tm

    def body(dy_ref, wd_ref, u_ref, up_ref, un_ref, cw_ref, cb_ref, dc_ref, dcw_ref, dcb_ref):
        i = pl.program_id(1)

        @pl.when(i == 0)
        def _():
            dcw_ref[...] = jnp.zeros(dcw_ref.shape, F32)
            dcb_ref[...] = jnp.zeros(dcb_ref.shape, F32)

        da = _dot_nt(dy_ref[...], wd_ref[...])
        ug, uv = u_ref[0], u_ref[1]
        ugp, ugn = _shift_rows(ug, *_halo_rows(up_ref, un_ref, 0, i, n_i))
        uvp, uvn = _shift_rows(uv, *_halo_rows(up_ref, un_ref, 1, i, n_i))
        wg, wv = cw_ref[0], cw_ref[1]
        gc = wg[0:1, :] * ugp + wg[1:2, :] * ug + wg[2:3, :] * ugn + cb_ref[0]
        vc = wv[0:1, :] * uvp + wv[1:2, :] * uv + wv[2:3, :] * uvn + cb_ref[1]
        sg = _sigmoid(gc)
        dg = da * vc * sg * (1.0 + gc * (1.0 - sg))
        dv = da * gc * sg
        dc_ref[0] = dg
        dc_ref[1] = dv
        for half, (d, taps) in enumerate(((dg, (ugp, ug, ugn)), (dv, (uvp, uv, uvn)))):
            for tap in range(3):
                dcw_ref[half, tap:tap + 1, :] += jnp.sum(d * taps[tap], axis=0, keepdims=True)
            dcb_ref[half] += jnp.sum(d, axis=0, keepdims=True)

    main, prev, nxt = _halo_specs(tm, s, FF_TC)
    return _call(
        body, name="ffn_act_bwd", grid=(ff // FF_TC, n_i),
        in_specs=[pl.BlockSpec((tm, D_MODEL), lambda j, i: (i, 0)), pl.BlockSpec((FF_TC, D_MODEL), lambda j, i: (j, 0)),
                  main, prev, nxt, pl.BlockSpec((2, 3, FF_TC), lambda j, i: (0, 0, j)),
                  pl.BlockSpec((2, 1, FF_TC), lambda j, i: (0, 0, j))],
        out_specs=[main, pl.BlockSpec((2, 3, FF_TC), lambda j, i: (0, 0, j)),
                   pl.BlockSpec((2, 1, FF_TC), lambda j, i: (0, 0, j))],
        out_shape=[_sds((2, s, ff)), _sds((2, 3, ff)), _sds((2, 1, ff))],
        sem=("parallel", "arbitrary"),
    )(dy2, w_down, u, u, u, cw, cb)


def _conv_bwd_input(dc, cw, *, tm):
    _, s, ff = dc.shape
    n_i = s // tm

    def body(d_ref, dp_ref, dn_ref, cw_ref, du_ref):
        i = pl.program_id(1)
        for half in range(2):
            up, dn = _shift_rows(d_ref[half], *_halo_rows(dp_ref, dn_ref, half, i, n_i))
            w = cw_ref[half]
            du_ref[half] = (w[0:1, :] * dn + w[1:2, :] * d_ref[half] + w[2:3, :] * up).astype(BF16)

    main, prev, nxt = _halo_specs(tm, s, FF_TC)
    return _call(
        body, name="conv_bwd_input", grid=(ff // FF_TC, n_i),
        in_specs=[main, prev, nxt, pl.BlockSpec((2, 3, FF_TC), lambda j, i: (0, 0, j))],
        out_specs=main, out_shape=_sds((2, s, ff), BF16), sem=("parallel", "parallel"),
    )(dc, dc, dc, cw)


def _ffn_down_loss(a, w_down, x1, target, gate2, ln_g, ln_b, *, tm):
    s, ff = a.shape
    n_i = s // tm

    def body(a_ref, wd_ref, x1_ref, tg_ref, gt_ref, lg_ref, lb_ref, ls_ref, dy_ref, dx_ref, dg_ref, db_ref, dgt_ref):
        @pl.when(pl.program_id(0) == 0)
        def _():
            dg_ref[...] = jnp.zeros(dg_ref.shape, F32)
            db_ref[...] = jnp.zeros(db_ref.shape, F32)
            dgt_ref[...] = jnp.zeros(dgt_ref.shape, F32)

        y2 = _dot(a_ref[...], wd_ref[...])
        z = ALPHA * x1_ref[...] + gt_ref[...] * y2
        mu = jnp.mean(z, axis=-1, keepdims=True)
        zc = z - mu
        r = lax.rsqrt(jnp.mean(zc * zc, axis=-1, keepdims=True) + LN_EPS)
        xhat = zc * r
        diff = xhat * lg_ref[...] + lb_ref[...] - tg_ref[...]
        ls_ref[...] = jnp.full(ls_ref.shape, 0.5 / D_MODEL * jnp.sum(diff * diff), F32)
        dx2 = diff * (1.0 / D_MODEL)
        dg_ref[...] += jnp.sum(dx2 * xhat, axis=0, keepdims=True)
        db_ref[...] += jnp.sum(dx2, axis=0, keepdims=True)
        dz = _ln_bwd(dx2, xhat, r, lg_ref[...])
        dgt_ref[...] += jnp.sum(dz * y2, axis=0, keepdims=True)
        dy_ref[...] = (gt_ref[...] * dz).astype(BF16)
        dx_ref[...] = ALPHA * dz

    row = pl.BlockSpec((tm, D_MODEL), lambda i: (i, 0))
    vec = _full((1, D_MODEL))
    return _call(
        body, name="ffn_down_loss", grid=(n_i,),
        in_specs=[pl.BlockSpec((tm, ff), lambda i: (i, 0)), _full((ff, D_MODEL)), row, row, vec, vec, vec],
        out_specs=[pl.BlockSpec((None, 8, LANES), lambda i: (i, 0, 0)), row, row, vec, vec, vec],
        out_shape=[_sds((n_i, 8, LANES)), _sds((s, D_MODEL), BF16), _sds((s, D_MODEL)),
                   _sds((1, D_MODEL)), _sds((1, D_MODEL)), _sds((1, D_MODEL))],
        sem=("arbitrary",),
    )(a, w_down, x1, target, gate2, ln_g, ln_b)


def _ffn_up_bwd(du, wup4, dx1a, x1, scale2, x, y, mu1, r1, gate1, ln_g, *, tm):
    s = x.shape[0]
    nb, _, ns = wup4.shape

    def body(du_ref, w_ref, dxa_ref, x1_ref, sc_ref, x_ref, y_ref, mu_ref, r_ref, gt_ref, lg_ref,
             dxo_ref, dy_ref, dsc_ref, dsh_ref, dg_ref, db_ref, dgt_ref, acc):
        i, k = pl.program_id(0), pl.program_id(1)

        @pl.when(jnp.logical_and(i == 0, k == 0))
        def _():
            for ref in (dsc_ref, dsh_ref, dg_ref, db_ref, dgt_ref):
                ref[...] = jnp.zeros(ref.shape, F32)

        @pl.when(k == 0)
        def _():
            acc[...] = jnp.zeros(acc.shape, F32)

        acc[...] += _dot_nt(du_ref[...], w_ref[...])

        @pl.when(k == nb - 1)
        def _():
            dh = acc[...]
            x1 = x1_ref[...]
            dsc_ref[...] += jnp.sum(dh * x1, axis=0, keepdims=True)
            dsh_ref[...] += jnp.sum(dh, axis=0, keepdims=True)
            dx1 = dxa_ref[...] + dh * (1.0 + sc_ref[...])
            yv = y_ref[...]
            xhat = (ALPHA * x_ref[...] + gt_ref[...] * yv - mu_ref[...]) * r_ref[...]
            dg_ref[...] += jnp.sum(dx1 * xhat, axis=0, keepdims=True)
            db_ref[...] += jnp.sum(dx1, axis=0, keepdims=True)
            dz = _ln_bwd(dx1, xhat, r_ref[...], lg_ref[...])
            dgt_ref[...] += jnp.sum(dz * yv, axis=0, keepdims=True)
            dy_ref[...] = (gt_ref[...] * dz).astype(BF16)
            dxo_ref[...] = ALPHA * dz

    row = pl.BlockSpec((tm, D_MODEL), lambda i, k: (i, 0))
    col = pl.BlockSpec((tm, 1), lambda i, k: (i, 0))
    vec = _full((1, D_MODEL))
    return _call(
        body, name="ffn_up_bwd", grid=(s // tm, nb),
        in_specs=[pl.BlockSpec((None, tm, ns), lambda i, k: (k // 2, i, k % 2)),
                  pl.BlockSpec((None, D_MODEL, ns), lambda i, k: (k, 0, 0)),
                  row, row, vec, row, row, col, col, vec, vec],
        out_specs=[row, row, vec, vec, vec, vec, vec],
        out_shape=[_sds((s, D_MODEL)), _sds((s, D_MODEL), BF16)] + [_sds((1, D_MODEL))] * 5,
        scratch=[pltpu.VMEM((tm, D_MODEL), F32)],
        sem=("arbitrary", "arbitrary"),
    )(du, wup4, dx1a, x1, scale2, x, y, mu1, r1, gate1, ln_g)


def _mm_nt4_mod_bwd(dp, w4, dxa, x, scale, *, tm, name):
    m = x.shape[0]
    nb, kdim, ns = w4.shape

    def body(dp_ref, w_ref, dxa_ref, x_ref, sc_ref, dx_ref, dsc_ref, dsh_ref, acc):
        i, k = pl.program_id(0), pl.program_id(1)

        @pl.when(jnp.logical_and(i == 0, k == 0))
        def _():
            dsc_ref[...] = jnp.zeros(dsc_ref.shape, F32)
            dsh_ref[...] = jnp.zeros(dsh_ref.shape, F32)

        @pl.when(k == 0)
        def _():
            acc[...] = jnp.zeros(acc.shape, F32)

        acc[...] += _dot_nt(dp_ref[...], w_ref[...])

        @pl.when(k == nb - 1)
        def _():
            dh = acc[...]
            dsc_ref[...] += jnp.sum(dh * x_ref[...], axis=0, keepdims=True)
            dsh_ref[...] += jnp.sum(dh, axis=0, keepdims=True)
            dx_ref[...] = dxa_ref[...] + dh * (1.0 + sc_ref[...])

    row = pl.BlockSpec((tm, kdim), lambda i, k: (i, 0))
    vec = _full((1, kdim))
    return _call(
        body, name=name, grid=(m // tm, nb),
        in_specs=[pl.BlockSpec((tm, ns), lambda i, k: (i, k)), pl.BlockSpec((None, kdim, ns), lambda i, k: (k, 0, 0)),
                  row, row, vec],
        out_specs=[row, vec, vec],
        out_shape=[_sds((m, kdim)), _sds((1, kdim)), _sds((1, kdim))],
        scratch=[pltpu.VMEM((tm, kdim), F32)],
        sem=("arbitrary", "arbitrary"),
    )(dp, w4, dxa, x, scale)


def _pad_heads_w(w):
    w8 = w.reshape(N_HEADS, HEAD_DIM, w.shape[-1])
    z = jnp.zeros_like(w8)
    first = (jnp.arange(N_HEADS) < N_HEADS // N_KV)[:, None, None]
    return jnp.where(first, jnp.concatenate([w8, z], axis=1), jnp.concatenate([z, w8], axis=1))


def _unpad_heads_w(g):
    first = (jnp.arange(N_HEADS) < N_HEADS // N_KV)[:, None, None]
    return jnp.where(first, g[:, :HEAD_DIM], g[:, HEAD_DIM:]).reshape(N_HEADS * HEAD_DIM, g.shape[-1])


def _ones_beside(vt):
    half = vt.shape[0] // 2
    ones = jnp.ones((half, vt.shape[1]), vt.dtype)
    return jnp.stack([jnp.concatenate([vt[:half], ones], axis=0), jnp.concatenate([ones, vt[half:]], axis=0)])


def _rep8(a):
    return jnp.broadcast_to(a.reshape(1, -1), (8, a.size))


def _first_row(a):
    r8 = _rep8(a)
    return jnp.where(lax.broadcasted_iota(jnp.int32, r8.shape, 0) == 0, r8, 0.0)


def _to_blocks4(w):
    k, n = w.shape
    return w.reshape(k, N_CHIPS, n // N_CHIPS).transpose(1, 0, 2)


def _local_step(x, c, ctx, c_ctx, wmod4, b_mod, win4, b_in, sink, qn, kn, wba, wbb, w_out, ln1_g, ln1_b,
                wup4, cw, cb, w_down, ln2_g, ln2_b, target):
    s, nc = x.shape[0], ctx.shape[0]
    tm = min(512, s)
    tm2 = min(256, s)
    zvec = jnp.zeros((1, D_MODEL), F32)

    cc = jnp.concatenate([_rep8(c), _rep8(c_ctx)], axis=0)
    mods = _mm_nn4(cc, zvec, zvec, wmod4, b_mod, mode="silu", split_out=False, out_dtype=F32, tm=16, name="mod_vectors")
    shift1, scale1, gate1, shift2, scale2, gate2 = [mods[0:1, i * D_MODEL:(i + 1) * D_MODEL] for i in range(6)]
    shift_c, scale_c = mods[8:9, :D_MODEL], mods[8:9, D_MODEL:2 * D_MODEL]

    cos, sin = _rope_tables(s)
    cos_c, sin_c = jnp.ones((nc, LANES), F32), jnp.zeros((nc, LANES), F32)
    qg, kg = jnp.tile(qn, (1, 2)), jnp.tile(kn, (1, 2))

    proj_c = _mm_nn4(ctx, shift_c, scale_c, win4, b_in, mode="modulate", split_out=False, out_dtype=F32, tm=nc,
                     name="in_proj_ctx")
    _, kac, vac, _, kbc, vbc = _prep(proj_c, cos_c, sin_c, qg, kg, tm=nc, name="prep_ctx")
    proj = _mm_nn4(x, shift1, scale1, win4, b_in, mode="modulate", split_out=False, out_dtype=F32, tm=tm, name="in_proj")
    qa, ka, va, qb, kb, vb = _prep(proj, cos, sin, qg, kg, tm=tm, name="prep")
    oa, lse_a = _attn_win_fwd(qa, ka, va, kac, vac, sink, tq=tm)
    qbt = jnp.swapaxes(qb, 1, 2)
    obt, lse_b, mrun_b, pbt = _attn_glob_fwd(qbt, kb, _ones_beside(vb.T), kbc, _ones_beside(vbc.T), tq=tm,
                                             tk=min(1024, s))
    ob = jnp.swapaxes(obt, 1, 2)
    wba_p, wbb_p = _pad_heads_w(wba), _pad_heads_w(wbb)
    x1, y, mu1, r1 = _merge_fwd(oa, ob, proj, x, gate1, wba_p, wbb_p, w_out, ln1_g, ln1_b, tm=tm2)
    u = _mm_nn4(x1, shift2, scale2, wup4, jnp.zeros((1, 2 * D_FF), F32), mode="modulate", split_out=True,
                out_dtype=F32, tm=tm, name="ffn_up")
    cw2 = cw.reshape(3, 2, D_FF).transpose(1, 0, 2)
    cb2 = cb.reshape(2, 1, D_FF)
    a = _ffn_act_fwd(u, cw2, cb2, tm=tm)
    ls, dy2, dx1a, dln2_g, dln2_b, dgate2 = _ffn_down_loss(a, w_down, x1, target, gate2, ln2_g, ln2_b, tm=tm2)
    loss = jnp.sum(ls[:, 0, 0])

    n_s = s // tm
    dw_down = _mm_tn(a, dy2, a_spec=pl.BlockSpec((tm, D_FF), lambda t: (t, 0)),
                     b_spec=pl.BlockSpec((tm, D_MODEL), lambda t: (t, 0)), grid=(n_s,),
                     out_shape=_sds((D_FF, D_MODEL)), out_spec=_full((D_FF, D_MODEL)), name="dw_down")
    dc, dcw2, dcb2 = _ffn_act_bwd(dy2, w_down, u, cw2, cb2, tm=tm)
    du = _conv_bwd_input(dc, cw2, tm=tm)
    dxz1, dy, dscale2, dshift2, dln1_g, dln1_b, dgate1 = _ffn_up_bwd(
        du, wup4, dx1a, x1, scale2, x, y, mu1, r1, gate1, ln1_g, tm=tm2)
    ns_up = wup4.shape[-1]
    dw_up4 = _mm_tn(x1, du, a_spec=pl.BlockSpec((tm, D_MODEL), lambda k, t: (t, 0)),
                    b_spec=pl.BlockSpec((None, tm, ns_up), lambda k, t: (k // 2, t, k % 2)), grid=(N_CHIPS, n_s),
                    out_shape=_sds((N_CHIPS, D_MODEL, ns_up)),
                    out_spec=pl.BlockSpec((None, D_MODEL, ns_up), lambda k, t: (k, 0, 0)),
                    mod=(shift2, scale2), name="dw_up")

    dgl, doa, dobt, merged, dwba_p, dwbb_p = _merge_bwd(dy, oa, ob, jnp.swapaxes(oa, 1, 2), obt, proj, wba_p, wbb_p,
                                                        w_out, tm=tm2)
    dwba, dwbb = _unpad_heads_w(dwba_p), _unpad_heads_w(dwbb_p)
    rowspec = pl.BlockSpec((tm, D_MODEL), lambda t: (t, 0))
    dw_out = _mm_tn(merged, dy, a_spec=rowspec, b_spec=rowspec, grid=(n_s,), out_shape=_sds((D_MODEL, D_MODEL)),
                    out_spec=_full((D_MODEL, D_MODEL)), name="dw_out")

    dqa, dla, dkac, dvac, dsk = _attn_win_dq(qa, doa, oa, lse_a, ka, va, kac, vac, sink, tq=tm)
    dka, dva = _attn_win_dkv(qa, doa, lse_a.reshape(N_HEADS, 1, s), dla.reshape(N_HEADS, 1, s), ka, va, tk=tm)
    dqbt, dkbt, dvbt, dkbct, dvbct = _attn_glob_bwd(qbt, dobt, obt, lse_b, mrun_b, pbt, kb.T, vb, kbc.T, vbc, tq=tm, tk=tm)
    dqb, dkb, dvb, dkbc, dvbc = jnp.swapaxes(dqbt, 1, 2), dkbt.T, dvbt.T, dkbct.T, dvbct.T
    dsink = jnp.sum(dsk[:, :, 0, 0], axis=1)

    dproj, dqg, dkg = _prep_bwd(dqa, dka, dva, dqb, dkb, dvb, proj, cos, sin, qg, kg, dgl, tm=tm, name="prep_bwd")
    grad_x, dscale1, dshift1 = _mm_nt4_mod_bwd(dproj, win4, dxz1, x, scale1, tm=tm, name="in_proj_bwd")
    ns_in = win4.shape[-1]
    win_spec = dict(b_spec=pl.BlockSpec((None, None, ns_in), lambda k, t: (0, 0, k)),
                    out_shape=_sds((N_CHIPS, D_MODEL, ns_in)),
                    out_spec=pl.BlockSpec((None, D_MODEL, ns_in), lambda k, t: (k, 0, 0)),
                    colsum_spec=pl.BlockSpec((8, ns_in), lambda k, t: (0, k)), colsum_shape=_sds((8, IN_COLS)))
    win_spec["b_spec"] = pl.BlockSpec((tm, ns_in), lambda k, t: (t, k))
    dw_in4, db_in = _mm_tn(x, dproj, a_spec=pl.BlockSpec((tm, D_MODEL), lambda k, t: (t, 0)), grid=(N_CHIPS, n_s),
                           mod=(shift1, scale1), name="dw_in", **win_spec)

    zq = jnp.zeros((N_HEADS, nc, LANES), F32)
    dproj_c, _, dkg_c = _prep_bwd(zq, dkac, dvac, zq, dkbc, dvbc, proj_c, cos_c, sin_c, qg, kg,
                                  jnp.zeros((nc, IN_COLS - OFF_GA), BF16), tm=nc, name="prep_bwd_ctx")
    _, dscale_c, dshift_c = _mm_nt4_mod_bwd(dproj_c, win4, jnp.zeros((nc, D_MODEL), F32), ctx, scale_c, tm=nc,
                                            name="in_proj_bwd_ctx")
    win_spec["b_spec"] = pl.BlockSpec((nc, ns_in), lambda k, t: (t, k))
    dw_in4, db_in_c = _mm_tn(ctx, dproj_c, a_spec=pl.BlockSpec((nc, D_MODEL), lambda k, t: (t, 0)), grid=(N_CHIPS, 1),
                             mod=(shift_c, scale_c), init=dw_in4, name="dw_in_ctx", **win_spec)

    dmod = jnp.concatenate([dshift1, dscale1, dgate1, dshift2, dscale2, dgate2], axis=1)
    dmodc = jnp.concatenate([dshift_c, dscale_c], axis=1)
    dmodc_pad = jnp.concatenate([dmodc, jnp.zeros((1, 4 * D_MODEL), F32)], axis=1)
    dmodc8 = _first_row(dmodc_pad).astype(BF16)
    z8 = jnp.zeros((8, D_MODEL), F32)
    dsilu_c, _, _ = _mm_nt4_mod_bwd(dmodc8, wmod4, z8, z8, zvec, tm=8, name="c_ctx_bwd")
    sg = _sigmoid(c_ctx)
    dc_ctx = dsilu_c[0:1] * sg * (1.0 + c_ctx * (1.0 - sg))

    dqn = jnp.sum(dqg.reshape(N_HEADS, HEAD_DIM), axis=0, keepdims=True)
    dkn = jnp.sum((dkg + dkg_c).reshape(N_KV, HEAD_DIM), axis=0, keepdims=True)
    grads = dict(
        w_in4=dw_in4, b_in=db_in[0:1] + db_in_c[0:1], sink=dsink, qn=dqn, kn=dkn, wba=dwba, wbb=dwbb, w_out=dw_out,
        ln1_g=dln1_g, ln1_b=dln1_b, w_up4=dw_up4, conv_w=dcw2.transpose(1, 0, 2).reshape(3, 2 * D_FF),
        conv_b=dcb2.reshape(1, 2 * D_FF), w_down=dw_down, ln2_g=dln2_g, ln2_b=dln2_b,
        c_ctx=dc_ctx, dmod=dmod, dmodc=dmodc)
    return loss, grad_x, grads


ANY = pl.BlockSpec(memory_space=pl.ANY)


def _mesh_pos():
    return lax.axis_index("x"), lax.axis_index("y"), lax.axis_index("c")


def _other_chips(x, y):
    return [(1 - x, y), (x, 1 - y), (1 - x, 1 - y)]


def _remote(src, dst, send, recv, dev):
    return pltpu.make_async_remote_copy(src_ref=src, dst_ref=dst, send_sem=send, recv_sem=recv, device_id=dev,
                                        device_id_type=MESH)


def _set_block(stack, block, k):
    return lax.dynamic_update_slice(stack, block[None], (k,) + (0,) * block.ndim)


def _gather_shards(arrs, small):
    na = len(arrs)
    halves = [a.shape[0] // 2 for a in arrs]

    def body(*refs):
        ins, small_ref = refs[:na], refs[na]
        outs, small_out = refs[na + 1:2 * na + 1], refs[2 * na + 1]
        send, recv = refs[2 * na + 2:]
        x, y, c = _mesh_pos()
        me = 2 * x + y
        chips = _other_chips(x, y)

        def half(a, cc):
            return pl.ds(cc * halves[a], halves[a])

        sends = []
        for j, chip in enumerate(chips):
            for a in range(na):
                sends.append(_remote(ins[a].at[half(a, c)], outs[a].at[me, half(a, c)], send.at[a, j], recv.at[a, j],
                                     (*chip, c)))
            sends.append(_remote(small_ref, small_out.at[me], send.at[na, j], recv.at[na, j], (*chip, c)))
        for cp in sends:
            cp.start()
        for j, chip in enumerate(chips):
            kj = 2 * chip[0] + chip[1]
            for a in range(na):
                landed = outs[a].at[kj, half(a, c)]
                _remote(landed, landed, send.at[a, j], recv.at[a, j], (*chip, c)).wait_recv()
                fwd = _remote(landed, landed, send.at[a, 3 + j], recv.at[a, 3 + j], (x, y, 1 - c))
                fwd.start()
                sends.append(fwd)
            _remote(small_ref, small_out.at[kj], send.at[na, j], recv.at[na, j], (*chip, c)).wait_recv()
        for j, chip in enumerate(chips):
            kj = 2 * chip[0] + chip[1]
            for a in range(na):
                other = outs[a].at[kj, half(a, 1 - c)]
                _remote(other, other, send.at[a, 3 + j], recv.at[a, 3 + j], (x, y, 1 - c)).wait_recv()
        for cp in sends:
            cp.wait_send()

    out_shape = [_sds((N_CHIPS,) + a.shape, a.dtype) for a in arrs] + [_sds((N_CHIPS,) + small.shape, small.dtype)]
    got = pl.pallas_call(
        body, name="gather_shards", in_specs=[ANY] * (na + 1), out_specs=[ANY] * (na + 1), out_shape=out_shape,
        scratch_shapes=[pltpu.SemaphoreType.DMA((na + 1, 6)), pltpu.SemaphoreType.DMA((na + 1, 6))],
    )(*arrs, small)
    xp, yp, _ = _mesh_pos()
    return [_set_block(g, a, 2 * xp + yp) for g, a in zip(got, list(arrs) + [small])]


def _allgather_rows(v):
    r, n = v.shape

    def body(v_ref, out_ref, send, recv, loc):
        x, y, c = _mesh_pos()
        me, sibling = (x, y, c), (x, y, 1 - c)
        chips = _other_chips(x, y)

        def rows(px, py, pc):
            return out_ref.at[4 * px + 2 * py + pc]

        def copy(k, block, to, src=None):
            return _remote(rows(*block) if src is None else src, rows(*block), send.at[k], recv.at[k], to)

        mine = pltpu.make_async_copy(v_ref, rows(*me), loc)
        mine.start()
        first = [copy(0, me, sibling, src=v_ref)] + [copy(1 + j, me, (*chip, c), src=v_ref) for j, chip in enumerate(chips)]
        for cp in first:
            cp.start()
        passed = [copy(4 + j, (*chip, c), sibling) for j, chip in enumerate(chips)]
        for j, chip in enumerate(chips):
            copy(1 + j, (*chip, c), me).wait_recv()
            passed[j].start()
        copy(0, sibling, me).wait_recv()
        for j, chip in enumerate(chips):
            copy(4 + j, (*chip, 1 - c), me).wait_recv()
        for cp in first + passed:
            cp.wait_send()
        mine.wait()

    return pl.pallas_call(
        body, name="allgather_rows", in_specs=[pl.BlockSpec(memory_space=pltpu.VMEM)],
        out_specs=pl.BlockSpec(memory_space=pltpu.VMEM), out_shape=_sds((N_DEV, r, n), v.dtype),
        scratch_shapes=[pltpu.SemaphoreType.DMA((7,)), pltpu.SemaphoreType.DMA((7,)), pltpu.SemaphoreType.DMA],
    )(v)


def _swap_other_half(g):
    nb, r, n = g.shape
    rh = r // 2

    def body(g_ref, out_ref, send, recv):
        x, y, c = _mesh_pos()
        cp = _remote(g_ref.at[:, pl.ds((1 - c) * rh, rh), :], out_ref, send, recv, (x, y, 1 - c))
        cp.start()
        cp.wait()

    return pl.pallas_call(
        body, name="swap_other_half", in_specs=[ANY], out_specs=ANY, out_shape=_sds((nb, rh, n), g.dtype),
        scratch_shapes=[pltpu.SemaphoreType.DMA, pltpu.SemaphoreType.DMA],
    )(g)


def _scatter_to_chips(p):
    def body(p_ref, out_ref, send, recv):
        x, y, c = _mesh_pos()
        me = 2 * x + y
        chips = _other_chips(x, y)
        sends = [_remote(p_ref.at[2 * chip[0] + chip[1]], out_ref.at[me], send.at[j], recv.at[j], (*chip, c))
                 for j, chip in enumerate(chips)]
        for cp in sends:
            cp.start()
        for j, chip in enumerate(chips):
            kj = 2 * chip[0] + chip[1]
            _remote(p_ref.at[kj], out_ref.at[kj], send.at[j], recv.at[j], (*chip, c)).wait_recv()
        for cp in sends:
            cp.wait_send()

    got = pl.pallas_call(
        body, name="scatter_to_chips", in_specs=[ANY], out_specs=ANY, out_shape=_sds(p.shape, p.dtype),
        scratch_shapes=[pltpu.SemaphoreType.DMA((3,)), pltpu.SemaphoreType.DMA((3,))],
    )(p)
    xp, yp, _ = _mesh_pos()
    me = 2 * xp + yp
    return _set_block(got, lax.dynamic_index_in_dim(p, me, axis=0, keepdims=False), me)


def _join_halves(f):
    def body(f_ref, out_ref, send, recv):
        x, y, c = _mesh_pos()
        cp = _remote(f_ref, out_ref, send, recv, (x, y, 1 - c))
        cp.start()
        cp.wait()

    other = pl.pallas_call(
        body, name="join_halves", in_specs=[ANY], out_specs=ANY, out_shape=_sds(f.shape, f.dtype),
        scratch_shapes=[pltpu.SemaphoreType.DMA, pltpu.SemaphoreType.DMA],
    )(f)
    first = lax.axis_index("c") == 0
    return jnp.concatenate([jnp.where(first, f, other), jnp.where(first, other, f)], axis=0)


def _row_tile(rows, cap=512):
    t = cap - cap % 8
    while rows % t:
        t -= 8
    return t


def _add_blocks(a, b, out_dtype):
    nb, r, n = a.shape
    tr = _row_tile(r)

    def body(a_ref, b_ref, o_ref):
        o_ref[...] = (a_ref[...] + b_ref[...]).astype(out_dtype)

    spec = pl.BlockSpec((None, tr, n), lambda k, i: (k, i, 0))
    return _call(body, name="add_blocks", grid=(nb, r // tr), in_specs=[spec, spec], out_specs=spec,
                 out_shape=_sds(a.shape, out_dtype), sem=("parallel", "parallel"))(a, b)


def _sum_leading(a, *, name):
    nk, r, n = a.shape
    tr = _row_tile(r)

    def body(a_ref, o_ref):
        acc = a_ref[0].astype(F32)
        for k in range(1, nk):
            acc = acc + a_ref[k].astype(F32)
        o_ref[...] = acc

    return _call(body, name=name, grid=(r // tr,), in_specs=[pl.BlockSpec((nk, tr, n), lambda i: (0, i, 0))],
                 out_specs=pl.BlockSpec((tr, n), lambda i: (i, 0)), out_shape=_sds((r, n)), sem=("parallel",))(a)


def _silu_outer(a, b):
    kdim, n = a.shape[1], b.shape[1]

    def body(a_ref, b_ref, o_ref):
        av = a_ref[...]
        av = av * _sigmoid(av)
        bv = b_ref[...]
        ah, bh = av.astype(BF16), bv.astype(BF16)
        al, bl = (av - ah.astype(F32)).astype(BF16), (bv - bh.astype(F32)).astype(BF16)
        o_ref[...] = _dot_tn(ah, bh) + (_dot_tn(ah, bl) + _dot_tn(al, bh))

    return _call(body, name="dw_mod", grid=(1,), in_specs=[_full(a.shape), _full(b.shape)], out_specs=_full((kdim, n)),
                 out_shape=_sds((kdim, n)))(a, b)


def _adamw(w, g, m, v):
    r, n = w.shape
    tr = _row_tile(r)

    def body(w_ref, g_ref, m_ref, v_ref, d_ref, nm_ref, nv_ref):
        gv = g_ref[...]
        nm = ADAM_B1 * m_ref[...] + (1.0 - ADAM_B1) * gv
        nv = ADAM_B2 * v_ref[...] + (1.0 - ADAM_B2) * (gv * gv)
        m_hat = nm / (1.0 - ADAM_B1 ** ADAM_STEP)
        v_hat = nv / (1.0 - ADAM_B2 ** ADAM_STEP)
        d_ref[...] = -ADAM_LR * (m_hat / (jnp.sqrt(v_hat) + ADAM_EPS) + ADAM_WD * w_ref[...])
        nm_ref[...] = nm
        nv_ref[...] = nv

    spec = pl.BlockSpec((tr, n), lambda i: (i, 0))
    return _call(body, name="adamw", grid=(r // tr,), in_specs=[spec] * 4, out_specs=[spec] * 3,
                 out_shape=[_sds((r, n))] * 3, sem=("parallel",))(w, g, m, v)


BIG = ("w_in", "w_branch_a", "w_branch_b", "w_out", "w_up", "w_down", "conv_w")
BIG_ROWS = 3584
SMALL = ("b_mod", "b_in", "conv_b", "ln1_g", "ln1_b", "ln2_g", "ln2_b", "c_ctx", "attn_sink", "q_norm_g", "k_norm_g")
SMALL_ROWS = 8 * len(SMALL)


def _rows(a, n_rows):
    flat = a.reshape(-1)
    return jnp.pad(flat, (0, n_rows * D_MODEL - flat.shape[0])).reshape(n_rows, D_MODEL)


def _group8(a):
    return _rep8(_rows(a, 1)) if a.size <= D_MODEL else _rows(a, 8)


def _ungroup8(p, shape):
    size = math.prod(shape)
    return (p[0, :size] if size <= D_MODEL else p.reshape(-1)[:size]).reshape(shape)


def _pack_big(t):
    parts = [t[n].reshape(-1, D_MODEL) for n in BIG[:-1]] + [_rows(t["conv_w"], 8)]
    used = sum(p.shape[0] for p in parts)
    return jnp.concatenate(parts + [jnp.zeros((BIG_ROWS - used, D_MODEL), F32)], axis=0)


def _unpack_big(p, like):
    out, r = {}, 0
    for n in BIG:
        size = math.prod(like[n].shape)
        nr = size // D_MODEL if n != "conv_w" else 8
        out[n] = p[r:r + nr].reshape(-1)[:size].reshape(like[n].shape)
        r += nr
    return out


def _pack_small(t):
    return jnp.concatenate([_group8(t[n]) for n in SMALL], axis=0)


def _unpack_small(p, like):
    return {n: _ungroup8(p[8 * i:8 * i + 8], like[n].shape) for i, n in enumerate(SMALL)}


WEIGHTS = ("c_ctx", "w_mod", "b_mod", "w_in", "b_in", "attn_sink", "q_norm_g", "k_norm_g", "w_branch_a", "w_branch_b",
           "w_out", "ln1_g", "ln1_b", "w_up", "conv_w", "conv_b", "w_down", "ln2_g", "ln2_b")


def kernel(x, c, ctx, c_ctx, w_mod, b_mod, w_in, b_in, attn_sink, q_norm_g, k_norm_g, w_branch_a, w_branch_b, w_out, ln1_g, ln1_b, w_up, conv_w, conv_b, w_down, ln2_g, ln2_b, loss_target, m_c_ctx, m_w_mod, m_b_mod, m_w_in, m_b_in, m_attn_sink, m_q_norm_g, m_k_norm_g, m_w_branch_a, m_w_branch_b, m_w_out, m_ln1_g, m_ln1_b, m_w_up, m_conv_w, m_conv_b, m_w_down, m_ln2_g, m_ln2_b, v_c_ctx, v_w_mod, v_b_mod, v_w_in, v_b_in, v_attn_sink, v_q_norm_g, v_k_norm_g, v_w_branch_a, v_w_branch_b, v_w_out, v_ln1_g, v_ln1_b, v_w_up, v_conv_w, v_conv_b, v_w_down, v_ln2_g, v_ln2_b):
    w = dict(c_ctx=c_ctx, w_mod=w_mod, b_mod=b_mod, w_in=w_in, b_in=b_in, attn_sink=attn_sink, q_norm_g=q_norm_g,
             k_norm_g=k_norm_g, w_branch_a=w_branch_a, w_branch_b=w_branch_b, w_out=w_out, ln1_g=ln1_g, ln1_b=ln1_b,
             w_up=w_up, conv_w=conv_w, conv_b=conv_b, w_down=w_down, ln2_g=ln2_g, ln2_b=ln2_b)
    m = dict(c_ctx=m_c_ctx, w_mod=m_w_mod, b_mod=m_b_mod, w_in=m_w_in, b_in=m_b_in, attn_sink=m_attn_sink,
             q_norm_g=m_q_norm_g, k_norm_g=m_k_norm_g, w_branch_a=m_w_branch_a, w_branch_b=m_w_branch_b, w_out=m_w_out,
             ln1_g=m_ln1_g, ln1_b=m_ln1_b, w_up=m_w_up, conv_w=m_conv_w, conv_b=m_conv_b, w_down=m_w_down,
             ln2_g=m_ln2_g, ln2_b=m_ln2_b)
    v = dict(c_ctx=v_c_ctx, w_mod=v_w_mod, b_mod=v_b_mod, w_in=v_w_in, b_in=v_b_in, attn_sink=v_attn_sink,
             q_norm_g=v_q_norm_g, k_norm_g=v_k_norm_g, w_branch_a=v_w_branch_a, w_branch_b=v_w_branch_b, w_out=v_w_out,
             ln1_g=v_ln1_g, ln1_b=v_ln1_b, w_up=v_w_up, conv_w=v_conv_w, conv_b=v_conv_b, w_down=v_w_down,
             ln2_g=v_ln2_g, ln2_b=v_ln2_b)
    xp, yp, _ = _mesh_pos()
    me = 2 * xp + yp

    branches = jnp.concatenate([w_branch_a[0], w_branch_b[0]], axis=0)
    wide = jnp.concatenate([w_mod[0], w_in[0], w_up[0], branches], axis=1).astype(BF16)
    tall = jnp.concatenate([w_out[0], w_down[0]], axis=0).astype(BF16)
    wide4, tall4, cw4 = _gather_shards([wide, tall], conv_w[0])
    n_mod, n_in, n_up = w_mod.shape[-1], w_in.shape[-1], w_up.shape[-1]
    wmod4 = wide4[:, :, :n_mod]
    win4 = wide4[:, :, n_mod:n_mod + n_in]
    wup4 = wide4[:, :, n_mod + n_in:n_mod + n_in + n_up]
    br4 = wide4[:, :, n_mod + n_in + n_up:]
    n_br = w_branch_a.shape[1]
    wba = br4[:, :n_br].transpose(1, 0, 2).reshape(n_br, D_MODEL)
    wbb = br4[:, n_br:].transpose(1, 0, 2).reshape(n_br, D_MODEL)
    n_out = w_out.shape[1]
    w_out_full = tall4[:, :n_out].reshape(D_MODEL, D_MODEL)
    w_down_full = tall4[:, n_out:].reshape(D_FF, D_MODEL)
    cw_full = cw4.transpose(1, 0, 2).reshape(3, 2 * D_FF)

    loss, grad_x, g = _local_step(
        x[0], c, ctx[0], c_ctx[None], wmod4, b_mod, win4, b_in, attn_sink[0], q_norm_g, k_norm_g, wba, wbb, w_out_full,
        ln1_g, ln1_b, wup4, cw_full, conv_b, w_down_full, ln2_g, ln2_b, loss_target[0])
    loss = lax.psum(loss, ("x", "y", "c"))

    sent = dict(c=c, dmod=g["dmod"], dmodc=g["dmodc"], b_in=g["b_in"], conv_b=g["conv_b"], ln1_g=g["ln1_g"],
                ln1_b=g["ln1_b"], ln2_g=g["ln2_g"], ln2_b=g["ln2_b"], c_ctx=g["c_ctx"], attn_sink=g["sink"],
                q_norm_g=g["qn"], k_norm_g=g["kn"])
    every = _allgather_rows(jnp.concatenate([_group8(a) for a in sent.values()], axis=0))
    total = _sum_leading(every, name="sum_devices")
    slot = {n: slice(8 * i, 8 * i + 8) for i, n in enumerate(sent)}
    gs = {n: _ungroup8(total[slot[n]], sent[n].shape) for n in SMALL if n in sent}
    dmodc_sum = jnp.concatenate([_ungroup8(total[slot["dmodc"]], (1, 2 * D_MODEL)), jnp.zeros((1, 4 * D_MODEL), F32)],
                                axis=1)
    gs["b_mod"] = _ungroup8(total[slot["dmod"]], b_mod.shape) + dmodc_sum
    acts = jnp.concatenate([every[:, slot["c"].start], _rep8(c_ctx)], axis=0)
    dmods = jnp.concatenate([every[:, slot["dmod"]].reshape(N_DEV, -1)[:, :6 * D_MODEL], _first_row(dmodc_sum)], axis=0)
    g_w_mod = _silu_outer(acts, lax.dynamic_slice_in_dim(dmods, me * n_mod, n_mod, axis=1))

    cw_g4 = _to_blocks4(g["conv_w"])
    packed = jnp.concatenate([
        g["w_in4"].reshape(N_CHIPS, -1, D_MODEL), _to_blocks4(g["wba"]).reshape(N_CHIPS, -1, D_MODEL),
        _to_blocks4(g["wbb"]).reshape(N_CHIPS, -1, D_MODEL), g["w_out"].reshape(N_CHIPS, -1, D_MODEL),
        g["w_up4"].reshape(N_CHIPS, -1, D_MODEL), g["w_down"].reshape(N_CHIPS, -1, D_MODEL),
        jnp.pad(cw_g4.reshape(N_CHIPS, -1), ((0, 0), (0, 8 * D_MODEL - cw_g4.shape[1] * cw_g4.shape[2]))).reshape(
            N_CHIPS, 8, D_MODEL),
        jnp.zeros((N_CHIPS, BIG_ROWS - 3528, D_MODEL), F32)], axis=1)
    rh = BIG_ROWS // 2
    cpos = lax.axis_index("c")
    my_half = lax.dynamic_slice_in_dim(packed, cpos * rh, rh, axis=1)
    chip_sum = _add_blocks(my_half, _swap_other_half(packed), BF16)
    half_sum = _sum_leading(_scatter_to_chips(chip_sum), name="sum_chips")
    g_big = _unpack_big(_join_halves(half_sum), w)

    grads = dict(gs, w_mod=g_w_mod, **g_big)

    def pack_all(t):
        rows = jnp.concatenate([_pack_big(t), t["w_mod"].reshape(-1, D_MODEL), _pack_small(t)], axis=0)
        return jnp.pad(rows, ((0, -rows.shape[0] % 256), (0, 0)))

    delta_p, new_m_p, new_v_p = _adamw(pack_all(w), pack_all(grads), pack_all(m), pack_all(v))

    def unpack_all(p):
        r_mod = BIG_ROWS + w_mod.size // D_MODEL
        out = _unpack_big(p[:BIG_ROWS], w)
        out["w_mod"] = p[BIG_ROWS:r_mod].reshape(w_mod.shape)
        out.update(_unpack_small(p[r_mod:r_mod + SMALL_ROWS], w))
        return out

    grads = {n: grads[n].reshape(w[n].shape) for n in WEIGHTS}
    delta, new_m, new_v = unpack_all(delta_p), unpack_all(new_m_p), unpack_all(new_v_p)
    return (loss, grad_x[None], *[grads[n] for n in WEIGHTS], *[delta[n] for n in WEIGHTS],
            *[new_m[n] for n in WEIGHTS], *[new_v[n] for n in WEIGHTS])
```

```python
import functools
import math

import jax
import jax.numpy as jnp
from jax import lax
from jax.experimental import pallas as pl
from jax.experimental.pallas import tpu as pltpu

F32 = jnp.float32
BF16 = jnp.bfloat16

D_MODEL = 1024
HEAD_DIM = 64
N_HEADS = 8
N_KV = 2
WINDOW = 128
GRID_W = 64
ROPE_THETA = 10000.0
D_FF = 2816
LN_EPS = 1e-5
QK_EPS = 1e-6
ALPHA = 2.0 ** 0.25
Q_SCALE = HEAD_DIM ** -0.5
OFF_GA = 1536
IN_COLS = 3584
ADAM_LR, ADAM_B1, ADAM_B2, ADAM_EPS, ADAM_WD, ADAM_STEP = 0.001, 0.9, 0.999, 1e-8, 0.01, 10

LANES = 128
VMEM_BUDGET = 52 * 1024 * 1024
N_CHIPS = 4
N_DEV = 8
NEG = -1e30
MESH = pl.DeviceIdType.MESH


def _sigmoid(x):
    return 1.0 / (1.0 + jnp.exp(-x))


def _dot(a, b):
    return jnp.dot(a, b, preferred_element_type=F32)


def _dot_nt(a, b):
    return lax.dot_general(a, b, (((1,), (1,)), ((), ())), preferred_element_type=F32)


def _dot_tn(a, b):
    return lax.dot_general(a, b, (((0,), (0,)), ((), ())), preferred_element_type=F32)


def _call(body, *, name, grid, in_specs, out_specs, out_shape, scratch=(), sem=None, **kw):
    params = dict(vmem_limit_bytes=VMEM_BUDGET)
    if sem is not None:
        params["dimension_semantics"] = sem
    return pl.pallas_call(body, name=name, grid=grid, in_specs=in_specs, out_specs=out_specs,
                          out_shape=out_shape, scratch_shapes=list(scratch),
                          compiler_params=pltpu.CompilerParams(**params), **kw)


def _full(shape):
    n = len(shape)
    return pl.BlockSpec(shape, lambda *_: (0,) * n)


def _sds(shape, dtype=F32):
    return jax.ShapeDtypeStruct(shape, dtype)


def _mm_nn4(a, shift, scale, w4, bias, *, mode, split_out, out_dtype, tm, name):
    m, kdim = a.shape
    nb, _, ns = w4.shape

    def body(a_ref, sh_ref, sc_ref, w_ref, b_ref, o_ref):
        av = a_ref[...]
        if mode == "modulate":
            av = av * (1.0 + sc_ref[...]) + sh_ref[...]
        else:
            av = av * _sigmoid(av)
        o_ref[...] = (_dot(av.astype(BF16), w_ref[...]) + b_ref[...]).astype(out_dtype)

    if split_out:
        out_shape = _sds((2, m, 2 * ns), out_dtype)
        out_spec = pl.BlockSpec((None, tm, ns), lambda i, k: (k // 2, i, k % 2))
    else:
        out_shape = _sds((m, nb * ns), out_dtype)
        out_spec = pl.BlockSpec((tm, ns), lambda i, k: (i, k))
    return _call(
        body, name=name, grid=(m // tm, nb),
        in_specs=[pl.BlockSpec((tm, kdim), lambda i, k: (i, 0)),
                  pl.BlockSpec((1, kdim), lambda i, k: (0, 0)),
                  pl.BlockSpec((1, kdim), lambda i, k: (0, 0)),
                  pl.BlockSpec((None, kdim, ns), lambda i, k: (k, 0, 0)),
                  pl.BlockSpec((1, ns), lambda i, k: (0, k))],
        out_specs=out_spec, out_shape=out_shape, sem=("parallel", "arbitrary"),
    )(a, shift, scale, w4, bias)


def _mm_tn(a, b, *, a_spec, b_spec, grid, out_shape, out_spec, name, mod=None, init=None, colsum_spec=None,
           colsum_shape=None):
    red = len(grid) - 1
    has_mod, has_init, has_cs = mod is not None, init is not None, colsum_spec is not None

    def body(*refs):
        refs = list(refs)
        a_ref, b_ref = refs[0], refs[1]
        pos = 2
        if has_mod:
            sh_ref, sc_ref = refs[2], refs[3]
            pos = 4
        if has_init:
            init_ref = refs[pos]
            pos += 1
        o_ref = refs[pos]
        cs_ref = refs[pos + 1] if has_cs else None
        s = pl.program_id(red)

        @pl.when(s == 0)
        def _():
            o_ref[...] = init_ref[...] if has_init else jnp.zeros(o_ref.shape, F32)
            if has_cs:
                cs_ref[...] = jnp.zeros(cs_ref.shape, F32)

        av = a_ref[...]
        if has_mod:
            av = av * (1.0 + sc_ref[...]) + sh_ref[...]
        bv = b_ref[...]
        o_ref[...] += _dot_tn(av.astype(BF16), bv)
        if has_cs:
            cs_ref[...] += jnp.broadcast_to(jnp.sum(bv.astype(F32), axis=0, keepdims=True), cs_ref.shape)

    ins, in_specs = [a, b], [a_spec, b_spec]
    if has_mod:
        kdim = mod[0].shape[-1]
        ins += list(mod)
        in_specs += [_full((1, kdim)), _full((1, kdim))]
    if has_init:
        ins.append(init)
        in_specs.append(out_spec)
    out_specs, out_shapes = out_spec, out_shape
    if has_cs:
        out_specs, out_shapes = [out_spec, colsum_spec], [out_shape, colsum_shape]
    sem = ("parallel",) * red + ("arbitrary",)
    return _call(body, name=name, grid=grid, in_specs=in_specs, out_specs=out_specs, out_shape=out_shapes,
                 sem=sem)(*ins)


def _rope_tables(n_tok):
    pos = jnp.arange(n_tok, dtype=jnp.int32)
    rows = (pos // GRID_W).astype(F32)
    cols = (pos % GRID_W).astype(F32)
    n_freq = HEAD_DIM // 4
    inv_freq = ROPE_THETA ** (-jnp.arange(n_freq, dtype=F32) / n_freq)
    ang_r = rows[:, None] * inv_freq
    ang_c = cols[:, None] * inv_freq
    cos = jnp.concatenate([jnp.cos(ang_r)] * 2 + [jnp.cos(ang_c)] * 2, axis=-1)
    sin = jnp.concatenate([-jnp.sin(ang_r), jnp.sin(ang_r), -jnp.sin(ang_c), jnp.sin(ang_c)], axis=-1)
    return jnp.tile(cos, (1, 2)), jnp.tile(sin, (1, 2))


def _lane(shape):
    return lax.broadcasted_iota(jnp.int32, shape, 1)


def _rope_partner(t, lane):
    return jnp.where((lane % 32) < 16, pltpu.roll(t, LANES - 16, 1), pltpu.roll(t, 16, 1))


def _half_mean(s, lane):
    lo = jnp.sum(jnp.where(lane < HEAD_DIM, s, 0.0), axis=-1, keepdims=True)
    hi = jnp.sum(jnp.where(lane < HEAD_DIM, 0.0, s), axis=-1, keepdims=True)
    return jnp.where(lane < HEAD_DIM, lo, hi) * (1.0 / HEAD_DIM)


def _prep(proj, cos, sin, qg, kg, *, tm, name):
    m = proj.shape[0]

    def body(p_ref, cos_ref, sin_ref, qg_ref, kg_ref, qa_ref, ka_ref, va_ref, qb_ref, kb_ref, vb_ref):
        lane = _lane((tm, LANES))
        cosv, sinv = cos_ref[...], sin_ref[...]
        low = lane < HEAD_DIM

        def rope(t):
            return t * cosv + _rope_partner(t, lane) * sinv

        def rms(t, g):
            return t * lax.rsqrt(_half_mean(t * t, lane) + QK_EPS) * g

        def place(q_ref, j, chunk):
            sw = pltpu.roll(chunk, HEAD_DIM, 1)
            if j < 2:
                h0, h1 = jnp.where(low, chunk, 0.0), jnp.where(low, sw, 0.0)
            else:
                h0, h1 = jnp.where(low, 0.0, sw), jnp.where(low, 0.0, chunk)
            q_ref[2 * j] = h0.astype(BF16)
            q_ref[2 * j + 1] = h1.astype(BF16)

        for j in range(4):
            place(qa_ref, j, rope(p_ref[:, j * LANES:(j + 1) * LANES]) * Q_SCALE)
            place(qb_ref, j, rope(rms(p_ref[:, 768 + j * LANES:768 + (j + 1) * LANES], qg_ref[...])) * Q_SCALE)
        ka_ref[...] = rope(p_ref[:, 512:640]).astype(BF16)
        va_ref[...] = p_ref[:, 640:768].astype(BF16)
        kb_ref[...] = rope(rms(p_ref[:, 1280:1408], kg_ref[...])).astype(BF16)
        vb_ref[...] = p_ref[:, 1408:1536].astype(BF16)

    row = pl.BlockSpec((tm, LANES), lambda i: (i, 0))
    qspec = pl.BlockSpec((N_HEADS, tm, LANES), lambda i: (0, i, 0))
    return _call(
        body, name=name, grid=(m // tm,),
        in_specs=[pl.BlockSpec((tm, OFF_GA), lambda i: (i, 0)), row, row, _full((1, LANES)), _full((1, LANES))],
        out_specs=[qspec, row, row, qspec, row, row],
        out_shape=[_sds((N_HEADS, m, LANES), BF16), _sds((m, LANES), BF16), _sds((m, LANES), BF16),
                   _sds((N_HEADS, m, LANES), BF16), _sds((m, LANES), BF16), _sds((m, LANES), BF16)],
        sem=("parallel",),
    )(proj, cos, sin, qg, kg)


def _prep_bwd(dqa, dka, dva, dqb, dkb, dvb, proj, cos, sin, qg, kg, dgl, *, tm, name):
    m = proj.shape[0]

    def body(dqa_ref, dka_ref, dva_ref, dqb_ref, dkb_ref, dvb_ref, p_ref, cos_ref, sin_ref, qg_ref, kg_ref,
             dgl_ref, dp_ref, dqg_ref, dkg_ref):
        i = pl.program_id(0)
        lane = _lane((tm, LANES))
        cosv, sinv = cos_ref[...], sin_ref[...]
        low = lane < HEAD_DIM

        @pl.when(i == 0)
        def _():
            dqg_ref[...] = jnp.zeros(dqg_ref.shape, F32)
            dkg_ref[...] = jnp.zeros(dkg_ref.shape, F32)

        def unrope(d):
            return d * cosv - _rope_partner(d, lane) * sinv

        def unplace(dq_ref, j):
            d0, d1 = dq_ref[2 * j], dq_ref[2 * j + 1]
            if j < 2:
                return jnp.where(low, d0, pltpu.roll(d1, HEAD_DIM, 1))
            return jnp.where(low, pltpu.roll(d0, HEAD_DIM, 1), d1)

        def unrms(dtn, t, g):
            r = lax.rsqrt(_half_mean(t * t, lane) + QK_EPS)
            u = dtn * g
            dt = r * u - t * (r * r * r) * _half_mean(u * t, lane)
            return dt, jnp.sum(dtn * t * r, axis=0, keepdims=True)

        for j in range(4):
            dp_ref[:, j * LANES:(j + 1) * LANES] = (unrope(unplace(dqa_ref, j)) * Q_SCALE).astype(BF16)
            c0 = 768 + j * LANES
            dt, dg = unrms(unrope(unplace(dqb_ref, j)) * Q_SCALE, p_ref[:, c0:c0 + LANES], qg_ref[...])
            dp_ref[:, c0:c0 + LANES] = dt.astype(BF16)
            dqg_ref[:, j * LANES:(j + 1) * LANES] += dg
        dp_ref[:, 512:640] = unrope(dka_ref[...]).astype(BF16)
        dp_ref[:, 640:768] = dva_ref[...].astype(BF16)
        dt, dg = unrms(unrope(dkb_ref[...]), p_ref[:, 1280:1408], kg_ref[...])
        dp_ref[:, 1280:1408] = dt.astype(BF16)
        dkg_ref[...] += dg
        dp_ref[:, 1408:1536] = dvb_ref[...].astype(BF16)
        dp_ref[:, OFF_GA:] = dgl_ref[...]

    row = pl.BlockSpec((tm, LANES), lambda i: (i, 0))
    qspec = pl.BlockSpec((N_HEADS, tm, LANES), lambda i: (0, i, 0))
    return _call(
        body, name=name, grid=(m // tm,),
        in_specs=[qspec, row, row, qspec, row, row, pl.BlockSpec((tm, OFF_GA), lambda i: (i, 0)), row, row,
                  _full((1, LANES)), _full((1, LANES)), pl.BlockSpec((tm, IN_COLS - OFF_GA), lambda i: (i, 0))],
        out_specs=[pl.BlockSpec((tm, IN_COLS), lambda i: (i, 0)), _full((1, 512)), _full((1, LANES))],
        out_shape=[_sds((m, IN_COLS), BF16), _sds((1, 512)), _sds((1, LANES))],
        sem=("arbitrary",),
    )(dqa, dka, dva, dqb, dkb, dvb, proj, cos, sin, qg, kg, dgl)


def _attn_glob_fwd(qt, k, vt, kc, vct, *, tq, tk):
    nh, _, s = qt.shape
    nc = kc.shape[0]
    n_chunks = s // tk
    half = LANES // 2

    def body(qt_ref, k_ref, vt_ref, kc_ref, vct_ref, ot_ref, lse_ref, mrun_ref, p_hbm,
             acc_sc, st_sc, stage_sc, stagec_sc, sems, semc):
        h, i = pl.program_id(0), pl.program_id(1)
        qtv = qt_ref[...]
        acc_sc[...] = jnp.zeros(acc_sc.shape, F32)

        def p_out(slot, c):
            return pltpu.make_async_copy(stage_sc.at[slot], p_hbm.at[h, i, pl.ds(pl.multiple_of(c * tk, tk), tk), :],
                                         sems.at[slot])

        def update(st, vtv, m_old):
            m_new = jnp.maximum(m_old, jnp.max(st, axis=0, keepdims=True))
            pb = jnp.exp(st - m_new).astype(BF16)
            acc_sc[...] = acc_sc[...] * jnp.exp(m_old - m_new) + _dot(vtv, pb)
            return m_new, pb

        m, pbc = update(_dot(kc_ref[...], qtv), vct_ref[...], jnp.full((1, tq), NEG, F32))
        mrun_ref[pl.ds(n_chunks, 1), :] = m
        stagec_sc[...] = pbc
        ctx_out = pltpu.make_async_copy(stagec_sc, p_hbm.at[h, i, pl.ds(s, nc), :], semc)
        ctx_out.start()

        def step(c, st, m_old):
            slot = c % 2
            off = pl.multiple_of(c * tk, tk)
            nxt = pl.multiple_of(jnp.minimum(c + 1, n_chunks - 1) * tk, tk)
            st_next = _dot(k_ref[pl.ds(nxt, tk), :], qtv)
            m_new, pb = update(st, vt_ref[:, pl.ds(off, tk)], m_old)
            mrun_ref[pl.ds(c, 1), :] = m_new
            stage_sc[slot] = pb
            p_out(slot, c).start()
            return st_next, m_new

        def loop(c, m_old):
            st_next, m_new = step(c, st_sc[...], m_old)
            p_out(1 - c % 2, c - 1).wait()
            st_sc[...] = st_next
            return m_new

        stage_sc[1] = jnp.zeros((tk, tq), BF16)
        pltpu.make_async_copy(stage_sc.at[1], p_hbm.at[h, i, pl.ds(s + nc, tk), :], sems.at[1]).start()
        st_sc[...] = _dot(k_ref[pl.ds(0, tk), :], qtv)
        m = lax.fori_loop(0, n_chunks, loop, m)
        p_out((n_chunks - 1) % 2, n_chunks - 1).wait()
        ctx_out.wait()
        acc = acc_sc[...]
        l = jnp.where(h < nh // N_KV, acc[half:half + 1], acc[0:1])
        ot_ref[...] = (acc / l).astype(BF16)
        lse_ref[...] = m + jnp.log(l)

    grp = nh // N_KV
    return _call(
        body, name="attn_glob_fwd", grid=(nh, s // tq),
        in_specs=[pl.BlockSpec((None, LANES, tq), lambda h, i: (h, 0, i)), _full((s, LANES)),
                  pl.BlockSpec((None, LANES, s), lambda h, i: (h // grp, 0, 0)), _full((nc, LANES)),
                  pl.BlockSpec((None, LANES, nc), lambda h, i: (h // grp, 0, 0))],
        out_specs=[pl.BlockSpec((None, LANES, tq), lambda h, i: (h, 0, i)),
                   pl.BlockSpec((None, 1, tq), lambda h, i: (h, 0, i)),
                   pl.BlockSpec((None, n_chunks + 1, tq), lambda h, i: (h, 0, i)), ANY],
        out_shape=[_sds((nh, LANES, s), BF16), _sds((nh, 1, s)), _sds((nh, n_chunks + 1, s)),
                   _sds((nh, s // tq, s + nc + tk, tq), BF16)],
        scratch=[pltpu.VMEM((LANES, tq), F32), pltpu.VMEM((tk, tq), F32), pltpu.VMEM((2, tk, tq), BF16),
                 pltpu.VMEM((nc, tq), BF16), pltpu.SemaphoreType.DMA((2,)), pltpu.SemaphoreType.DMA],
        sem=("parallel", "parallel"),
    )(qt, k, vt, kc, vct)


P_AHEAD = 3


def _attn_glob_bwd(qt, dot, ot, lse, mrun, p, kt, v, kct, vc, *, tq, tk):
    nh, _, s = qt.shape
    nc = vc.shape[0]
    n_q = s // tq
    n_chunks = s // tk
    n_run = mrun.shape[1] - 1
    per_run = n_chunks // n_run

    def body(qt_ref, dot_ref, ot_ref, lse_ref, mrun_ref, p_hbm, kt_ref, v_ref, kct_ref, vc_ref,
             dqt_ref, dkt_ref, dvt_ref, dkct_ref, dvct_ref, acc_sc, dp_sc, dkt_sc, dvt_sc, p_sc, pc_sc, sems, semc):
        h, i = pl.program_id(0), pl.program_id(1)

        @pl.when(jnp.logical_and(h == 0, i == 0))
        def _():
            dkct_ref[...] = jnp.zeros(dkct_ref.shape, F32)
            dvct_ref[...] = jnp.zeros(dvct_ref.shape, F32)
            dkt_sc[...] = jnp.zeros(dkt_sc.shape, F32)
            dvt_sc[...] = jnp.zeros(dvt_sc.shape, F32)


        def p_in(slot, c):
            return pltpu.make_async_copy(p_hbm.at[h, i, pl.ds(pl.multiple_of(c * tk, tk), tk), :], p_sc.at[slot],
                                         sems.at[slot])

        ctx_in = pltpu.make_async_copy(p_hbm.at[h, i, pl.ds(s, nc), :], pc_sc, semc)
        ctx_in.start()
        for c in range(P_AHEAD):
            p_in(c, min(c, n_chunks - 1)).start()
        qtv, dotv, lse = qt_ref[...], dot_ref[...], lse_ref[...]
        delta = jnp.sum(dotv.astype(F32) * ot_ref[...].astype(F32), axis=0, keepdims=True)

        def grads(pt_stored, m_row, dpt):
            pt = pt_stored.astype(F32) * jnp.exp(m_row - lse)
            return pt.astype(BF16), (pt * (dpt - delta)).astype(BF16)

        dp_sc[...] = _dot(v_ref[pl.ds(0, tk), :], dotv)
        ctx_in.wait()
        pb, dsb = grads(pc_sc[...], mrun_ref[pl.ds(n_run, 1), :], _dot(vc_ref[...], dotv))
        acc_sc[...] = _dot(kct_ref[...], dsb)
        dkct_ref[...] += _dot_nt(qtv, dsb)
        dvct_ref[...] += _dot_nt(dotv, pb)

        def loop(c, carry):
            slot = c % (P_AHEAD + 1)
            off = pl.multiple_of(c * tk, tk)
            nxt = pl.multiple_of(jnp.minimum(c + 1, n_chunks - 1) * tk, tk)
            p_in(slot, c).wait()
            p_in((c + P_AHEAD) % (P_AHEAD + 1), jnp.minimum(c + P_AHEAD, n_chunks - 1)).start()
            dpt = dp_sc[...]
            dp_next = _dot(v_ref[pl.ds(nxt, tk), :], dotv)
            pb, dsb = grads(p_sc[slot], mrun_ref[pl.ds(c // per_run, 1), :], dpt)
            acc_sc[...] += _dot(kt_ref[:, pl.ds(off, tk)], dsb)
            dkt_sc[:, pl.ds(off, tk)] += _dot_nt(qtv, dsb)
            dvt_sc[:, pl.ds(off, tk)] += _dot_nt(dotv, pb)
            dp_sc[...] = dp_next
            return carry

        lax.fori_loop(0, n_chunks, loop, 0)
        for c in range(n_chunks, n_chunks + P_AHEAD):
            p_in(c % (P_AHEAD + 1), n_chunks - 1).wait()
        dqt_ref[...] = acc_sc[...]

        @pl.when(jnp.logical_and(h == nh - 1, i == n_q - 1))
        def _():
            pltpu.sync_copy(dkt_sc, dkt_ref)
            pltpu.sync_copy(dvt_sc, dvt_ref)

    qs = pl.BlockSpec((None, LANES, tq), lambda h, i: (h, 0, i))
    rs = pl.BlockSpec((None, 1, tq), lambda h, i: (h, 0, i))
    return _call(
        body, name="attn_glob_bwd", grid=(nh, n_q),
        in_specs=[qs, qs, qs, rs, pl.BlockSpec((None, n_run + 1, tq), lambda h, i: (h, 0, i)), ANY,
                  _full((LANES, s)), _full((s, LANES)), _full((LANES, nc)), _full((nc, LANES))],
        out_specs=[qs, ANY, ANY, _full((LANES, nc)), _full((LANES, nc))],
        out_shape=[_sds((nh, LANES, s)), _sds((LANES, s)), _sds((LANES, s)), _sds((LANES, nc)), _sds((LANES, nc))],
        scratch=[pltpu.VMEM((LANES, tq), F32), pltpu.VMEM((tk, tq), F32), pltpu.VMEM((LANES, s), F32),
                 pltpu.VMEM((LANES, s), F32), pltpu.VMEM((P_AHEAD + 1, tk, tq), BF16), pltpu.VMEM((nc, tq), BF16),
                 pltpu.SemaphoreType.DMA((P_AHEAD + 1,)), pltpu.SemaphoreType.DMA],
        sem=("arbitrary", "arbitrary"),
    )(qt, dot, ot, lse, mrun, p, kt, v, kct, vc)


WIN_SPAN = 2 * WINDOW


def _band(rows0, cols0, shape):
    r = rows0 + lax.broadcasted_iota(jnp.int32, shape, 0)
    c = cols0 + lax.broadcasted_iota(jnp.int32, shape, 1)
    return jnp.abs(r - c) <= WINDOW


def _win_start(blk, t, s):
    return pl.multiple_of(jnp.clip(blk * t - WINDOW, 0, s - t - WIN_SPAN), WINDOW)


def _attn_win_fwd(q, k, v, kc, vc, sink, *, tq):
    nh, s, _ = q.shape
    nc = kc.shape[0]
    tw = tq + WIN_SPAN

    def body(sink_ref, q_ref, k_ref, v_ref, kc_ref, vc_ref, o_ref, lse_ref):
        h, i = pl.program_id(0), pl.program_id(1)
        k0 = _win_start(i, tq, s)
        qv = q_ref[...]
        kv, vv = k_ref[pl.ds(k0, tw), :], v_ref[pl.ds(k0, tw), :]
        sc = jnp.where(_band(i * tq, k0, (tq, tw)), _dot_nt(qv, kv), NEG)
        scc = _dot_nt(qv, kc_ref[...])
        snk = sink_ref[h]
        m = jnp.maximum(jnp.maximum(jnp.max(sc, axis=-1, keepdims=True), jnp.max(scc, axis=-1, keepdims=True)), snk)
        p, pc = jnp.exp(sc - m), jnp.exp(scc - m)
        l = jnp.sum(p, axis=-1, keepdims=True) + jnp.sum(pc, axis=-1, keepdims=True) + jnp.exp(snk - m)
        acc = _dot(p.astype(BF16), vv) + _dot(pc.astype(BF16), vc_ref[...])
        o_ref[...] = (acc / l).astype(BF16)
        lse_ref[...] = m + jnp.log(l)

    return _call(
        body, name="attn_win_fwd", grid=(nh, s // tq),
        in_specs=[pl.BlockSpec(memory_space=pltpu.SMEM),
                  pl.BlockSpec((None, tq, LANES), lambda h, i: (h, i, 0)),
                  _full((s, LANES)), _full((s, LANES)), _full((nc, LANES)), _full((nc, LANES))],
        out_specs=[pl.BlockSpec((None, tq, LANES), lambda h, i: (h, i, 0)),
                   pl.BlockSpec((None, tq, 1), lambda h, i: (h, i, 0))],
        out_shape=[_sds((nh, s, LANES), BF16), _sds((nh, s, 1))],
        sem=("parallel", "parallel"),
    )(sink, q, k, v, kc, vc)


def _attn_win_dq(q, do, o, lse, k, v, kc, vc, sink, *, tq):
    nh, s, _ = q.shape
    nc = kc.shape[0]
    tw = tq + WIN_SPAN
    nq = s // tq

    def body(sink_ref, q_ref, do_ref, o_ref, lse_ref, k_ref, v_ref, kc_ref, vc_ref,
             dq_ref, dl_ref, dkc_ref, dvc_ref, dsk_ref):
        h, i = pl.program_id(0), pl.program_id(1)

        @pl.when(jnp.logical_and(h == 0, i == 0))
        def _():
            dkc_ref[...] = jnp.zeros(dkc_ref.shape, F32)
            dvc_ref[...] = jnp.zeros(dvc_ref.shape, F32)

        k0 = _win_start(i, tq, s)
        qv, dov, lse = q_ref[...], do_ref[...], lse_ref[...]
        kv, vv = k_ref[pl.ds(k0, tw), :], v_ref[pl.ds(k0, tw), :]
        kcv, vcv = kc_ref[...], vc_ref[...]
        delta = jnp.sum(dov.astype(F32) * o_ref[...].astype(F32), axis=-1, keepdims=True)
        dl_ref[...] = delta
        p = jnp.where(_band(i * tq, k0, (tq, tw)), jnp.exp(_dot_nt(qv, kv) - lse), 0.0)
        ds = (p * (_dot_nt(dov, vv) - delta)).astype(BF16)
        pc = jnp.exp(_dot_nt(qv, kcv) - lse)
        dsc = (pc * (_dot_nt(dov, vcv) - delta)).astype(BF16)
        dq_ref[...] = _dot(ds, kv) + _dot(dsc, kcv)
        dkc_ref[...] += _dot_tn(dsc, qv)
        dvc_ref[...] += _dot_tn(pc.astype(BF16), dov)
        dsk = -jnp.sum(jnp.exp(sink_ref[h] - lse) * delta)
        dsk_ref[...] = jnp.full(dsk_ref.shape, dsk, F32)

    qs = pl.BlockSpec((None, tq, LANES), lambda h, i: (h, i, 0))
    cs = pl.BlockSpec((None, tq, 1), lambda h, i: (h, i, 0))
    return _call(
        body, name="attn_win_dq", grid=(nh, nq),
        in_specs=[pl.BlockSpec(memory_space=pltpu.SMEM), qs, qs, qs, cs,
                  _full((s, LANES)), _full((s, LANES)), _full((nc, LANES)), _full((nc, LANES))],
        out_specs=[qs, cs, _full((nc, LANES)), _full((nc, LANES)),
                   pl.BlockSpec((None, None, 8, LANES), lambda h, i: (h, i, 0, 0))],
        out_shape=[_sds((nh, s, LANES)), _sds((nh, s, 1)), _sds((nc, LANES)), _sds((nc, LANES)),
                   _sds((nh, nq, 8, LANES))],
        sem=("arbitrary", "arbitrary"),
    )(sink, q, do, o, lse, k, v, kc, vc)


def _attn_win_dkv(q, do, lse_row, dl_row, k, v, *, tk):
    nh, s, _ = q.shape
    tw = tk + WIN_SPAN

    def body(k_ref, v_ref, q_ref, do_ref, lse_ref, dl_ref, dk_ref, dv_ref, dk_sc, dv_sc):
        h, j = pl.program_id(0), pl.program_id(1)
        q0 = _win_start(j, tk, s)
        kv, vv = k_ref[...], v_ref[...]
        qv, dov = q_ref[pl.ds(q0, tw), :], do_ref[pl.ds(q0, tw), :]
        pt = jnp.where(_band(j * tk, q0, (tk, tw)), jnp.exp(_dot_nt(kv, qv) - lse_ref[:, pl.ds(q0, tw)]), 0.0)
        dst = pt * (_dot_nt(vv, dov) - dl_ref[:, pl.ds(q0, tw)])
        dk, dv = _dot(dst.astype(BF16), qv), _dot(pt.astype(BF16), dov)
        rows = pl.ds(pl.multiple_of(j * tk, tk), tk)

        @pl.when(h == 0)
        def _():
            dk_sc[rows, :] = dk
            dv_sc[rows, :] = dv

        @pl.when(h > 0)
        def _():
            dk_sc[rows, :] += dk
            dv_sc[rows, :] += dv

        @pl.when(jnp.logical_and(h == nh - 1, j == s // tk - 1))
        def _():
            pltpu.sync_copy(dk_sc, dk_ref)
            pltpu.sync_copy(dv_sc, dv_ref)

    ks = pl.BlockSpec((tk, LANES), lambda h, j: (j, 0))
    qs = pl.BlockSpec((None, s, LANES), lambda h, j: (h, 0, 0))
    rs = pl.BlockSpec((None, 1, s), lambda h, j: (h, 0, 0))
    return _call(
        body, name="attn_win_dkv", grid=(nh, s // tk),
        in_specs=[ks, ks, qs, qs, rs, rs], out_specs=[ANY, ANY],
        out_shape=[_sds((s, LANES)), _sds((s, LANES))],
        scratch=[pltpu.VMEM((s, LANES), F32), pltpu.VMEM((s, LANES), F32)],
        sem=("arbitrary", "arbitrary"),
    )(k, v, q, do, lse_row, dl_row)


def _ln_fwd(z, g, b):
    mu = jnp.mean(z, axis=-1, keepdims=True)
    zc = z - mu
    r = lax.rsqrt(jnp.mean(zc * zc, axis=-1, keepdims=True) + LN_EPS)
    return zc * r * g + b, mu, r


def _ln_bwd(dy, xhat, r, g):
    dxh = dy * g
    return r * (dxh - jnp.mean(dxh, axis=-1, keepdims=True) - xhat * jnp.mean(dxh * xhat, axis=-1, keepdims=True))


def _heads_matmul(o_ref, w_ref):
    acc = _dot(o_ref[0], w_ref[0])
    for h in range(1, N_HEADS):
        acc += _dot(o_ref[h], w_ref[h])
    return acc


def _gate_specs(tm):
    return [pl.BlockSpec((tm, 512), functools.partial(lambda i, b: (i, b), b=OFF_GA // 512 + b)) for b in range(4)]


def _merge_fwd(oa, ob, proj, x, gate1, wba, wbb, w_out, ln_g, ln_b, *, tm):
    s = x.shape[0]

    def body(oa_ref, ob_ref, g0, g1, g2, g3, x_ref, gt_ref, wba_ref, wbb_ref, wo_ref, lg_ref, lb_ref,
             x1_ref, y_ref, mu_ref, r_ref):
        ga = _sigmoid(jnp.concatenate([g0[...], g1[...]], axis=1))
        gb = _sigmoid(jnp.concatenate([g2[...], g3[...]], axis=1))
        merged = ga * _heads_matmul(oa_ref, wba_ref) + gb * _heads_matmul(ob_ref, wbb_ref)
        y = _dot(merged.astype(BF16), wo_ref[...])
        x1, mu, r = _ln_fwd(ALPHA * x_ref[...] + gt_ref[...] * y, lg_ref[...], lb_ref[...])
        x1_ref[...] = x1
        y_ref[...] = y
        mu_ref[...] = mu
        r_ref[...] = r

    hs = pl.BlockSpec((N_HEADS, tm, LANES), lambda i: (0, i, 0))
    row = pl.BlockSpec((tm, D_MODEL), lambda i: (i, 0))
    col = pl.BlockSpec((tm, 1), lambda i: (i, 0))
    vec = _full((1, D_MODEL))
    wh = _full((N_HEADS, LANES, D_MODEL))
    return _call(
        body, name="merge_fwd", grid=(s // tm,),
        in_specs=[hs, hs, *_gate_specs(tm), row, vec, wh, wh, _full((D_MODEL, D_MODEL)), vec, vec],
        out_specs=[row, row, col, col],
        out_shape=[_sds((s, D_MODEL)), _sds((s, D_MODEL)), _sds((s, 1)), _sds((s, 1))],
        sem=("parallel",),
    )(oa, ob, proj, proj, proj, proj, x, gate1, wba, wbb, w_out, ln_g, ln_b)


def _merge_bwd(dy, oa, ob, oat, obt, proj, wba, wbb, w_out, *, tm):
    s = dy.shape[0]

    def body(dy_ref, oa_ref, ob_ref, oat_ref, obt_ref, g0, g1, g2, g3, wba_ref, wbb_ref, wo_ref,
             dgl_ref, doa_ref, dobt_ref, mg_ref, dwa_ref, dwb_ref):
        @pl.when(pl.program_id(0) == 0)
        def _():
            dwa_ref[...] = jnp.zeros(dwa_ref.shape, F32)
            dwb_ref[...] = jnp.zeros(dwb_ref.shape, F32)

        dm = _dot_nt(dy_ref[...], wo_ref[...])
        ga = _sigmoid(jnp.concatenate([g0[...], g1[...]], axis=1))
        gb = _sigmoid(jnp.concatenate([g2[...], g3[...]], axis=1))
        pa, pb = _heads_matmul(oa_ref, wba_ref), _heads_matmul(ob_ref, wbb_ref)
        mg_ref[...] = (ga * pa + gb * pb).astype(BF16)
        dgl_ref[:, :D_MODEL] = (dm * pa * ga * (1.0 - ga)).astype(BF16)
        dgl_ref[:, D_MODEL:] = (dm * pb * gb * (1.0 - gb)).astype(BF16)
        dpa, dpb = (dm * ga).astype(BF16), (dm * gb).astype(BF16)
        for h in range(N_HEADS):
            doa_ref[h] = _dot_nt(dpa, wba_ref[h]).astype(BF16)
            dobt_ref[h] = _dot_nt(wbb_ref[h], dpb).astype(BF16)
            dwa_ref[h] += _dot(oat_ref[h], dpa)
            dwb_ref[h] += _dot(obt_ref[h], dpb)

    hs = pl.BlockSpec((N_HEADS, tm, LANES), lambda i: (0, i, 0))
    hts = pl.BlockSpec((N_HEADS, LANES, tm), lambda i: (0, 0, i))
    row = pl.BlockSpec((tm, D_MODEL), lambda i: (i, 0))
    wh = _full((N_HEADS, LANES, D_MODEL))
    return _call(
        body, name="merge_bwd", grid=(s // tm,),
        in_specs=[row, hs, hs, hts, hts, *_gate_specs(tm), wh, wh, _full((D_MODEL, D_MODEL))],
        out_specs=[pl.BlockSpec((tm, 2 * D_MODEL), lambda i: (i, 0)), hs, hts, row, wh, wh],
        out_shape=[_sds((s, 2 * D_MODEL), BF16), _sds((N_HEADS, s, LANES), BF16), _sds((N_HEADS, LANES, s), BF16),
                   _sds((s, D_MODEL), BF16), _sds((N_HEADS, LANES, D_MODEL)), _sds((N_HEADS, LANES, D_MODEL))],
        sem=("arbitrary",),
    )(dy, oa, ob, oat, obt, proj, proj, proj, proj, wba, wbb, w_out)


FF_TC = 256


def _shift_rows(t, prev_row, next_row):
    n = t.shape[0]
    r = lax.broadcasted_iota(jnp.int32, t.shape, 0)
    up = jnp.where(r == 0, prev_row, pltpu.roll(t, 1, 0))
    dn = jnp.where(r == n - 1, next_row, pltpu.roll(t, n - 1, 0))
    return up, dn


def _halo_specs(tm, s, tc):
    nb8 = s // 8
    main = pl.BlockSpec((2, tm, tc), lambda j, i: (0, i, j))
    prev = pl.BlockSpec((2, 8, tc), lambda j, i: (0, jnp.maximum(i * (tm // 8) - 1, 0), j))
    nxt = pl.BlockSpec((2, 8, tc), lambda j, i: (0, jnp.minimum((i + 1) * (tm // 8), nb8 - 1), j))
    return main, prev, nxt


def _halo_rows(prev_ref, next_ref, half, i, n_i):
    prev_row = jnp.where(i == 0, 0.0, prev_ref[half, 7:8, :].astype(F32))
    next_row = jnp.where(i == n_i - 1, 0.0, next_ref[half, 0:1, :].astype(F32))
    return prev_row, next_row


def _conv(t, prev_row, next_row, w, b):
    up, dn = _shift_rows(t, prev_row, next_row)
    return w[0:1, :] * up + w[1:2, :] * t + w[2:3, :] * dn + b


def _ffn_act_fwd(u, cw, cb, *, tm):
    _, s, ff = u.shape
    n_i = s // tm

    def body(u_ref, up_ref, un_ref, cw_ref, cb_ref, a_ref):
        i = pl.program_id(1)
        gc = _conv(u_ref[0], *_halo_rows(up_ref, un_ref, 0, i, n_i), cw_ref[0], cb_ref[0])
        vc = _conv(u_ref[1], *_halo_rows(up_ref, un_ref, 1, i, n_i), cw_ref[1], cb_ref[1])
        a_ref[...] = (gc * _sigmoid(gc) * vc).astype(BF16)

    main, prev, nxt = _halo_specs(tm, s, FF_TC)
    return _call(
        body, name="ffn_act_fwd", grid=(ff // FF_TC, n_i),
        in_specs=[main, prev, nxt, pl.BlockSpec((2, 3, FF_TC), lambda j, i: (0, 0, j)),
                  pl.BlockSpec((2, 1, FF_TC), lambda j, i: (0, 0, j))],
        out_specs=pl.BlockSpec((tm, FF_TC), lambda j, i: (i, j)),
        out_shape=_sds((s, ff), BF16), sem=("parallel", "parallel"),
    )(u, u, u, cw, cb)


def _ffn_act_bwd(dy2, w_down, u, cw, cb, *, tm):
    _, s, ff = u.shape
    n_i = s // tm

    def body(dy_ref, wd_ref, u_ref, up_ref, un_ref, cw_ref, cb_ref, dc_ref, dcw_ref, dcb_ref):
        i = pl.program_id(1)

        @pl.when(i == 0)
        def _():
            dcw_ref[...] = jnp.zeros(dcw_ref.shape, F32)
            dcb_ref[...] = jnp.zeros(dcb_ref.shape, F32)

        da = _dot_nt(dy_ref[...], wd_ref[...])
        ug, uv = u_ref[0], u_ref[1]
        ugp, ugn = _shift_rows(ug, *_halo_rows(up_ref, un_ref, 0, i, n_i))
        uvp, uvn = _shift_rows(uv, *_halo_rows(up_ref, un_ref, 1, i, n_i))
        wg, wv = cw_ref[0], cw_ref[1]
        gc = wg[0:1, :] * ugp + wg[1:2, :] * ug + wg[2:3, :] * ugn + cb_ref[0]
        vc = wv[0:1, :] * uvp + wv[1:2, :] * uv + wv[2:3, :] * uvn + cb_ref[1]
        sg = _sigmoid(gc)
        dg = da * vc * sg * (1.0 + gc * (1.0 - sg))
        dv = da * gc * sg
        dc_ref[0] = dg
        dc_ref[1] = dv
        for half, (d, taps) in enumerate(((dg, (ugp, ug, ugn)), (dv, (uvp, uv, uvn)))):
            for tap in range(3):
                dcw_ref[half, tap:tap + 1, :] += jnp.sum(d * taps[tap], axis=0, keepdims=True)
            dcb_ref[half] += jnp.sum(d, axis=0, keepdims=True)

    main, prev, nxt = _halo_specs(tm, s, FF_TC)
    return _call(
        body, name="ffn_act_bwd", grid=(ff // FF_TC, n_i),
        in_specs=[pl.BlockSpec((tm, D_MODEL), lambda j, i: (i, 0)), pl.BlockSpec((FF_TC, D_MODEL), lambda j, i: (j, 0)),
                  main, prev, nxt, pl.BlockSpec((2, 3, FF_TC), lambda j, i: (0, 0, j)),
                  pl.BlockSpec((2, 1, FF_TC), lambda j, i: (0, 0, j))],
        out_specs=[main, pl.BlockSpec((2, 3, FF_TC), lambda j, i: (0, 0, j)),
                   pl.BlockSpec((2, 1, FF_TC), lambda j, i: (0, 0, j))],
        out_shape=[_sds((2, s, ff)), _sds((2, 3, ff)), _sds((2, 1, ff))],
        sem=("parallel", "arbitrary"),
    )(dy2, w_down, u, u, u, cw, cb)


def _conv_bwd_input(dc, cw, *, tm):
    _, s, ff = dc.shape
    n_i = s // tm

    def body(d_ref, dp_ref, dn_ref, cw_ref, du_ref):
        i = pl.program_id(1)
        for half in range(2):
            up, dn = _shift_rows(d_ref[half], *_halo_rows(dp_ref, dn_ref, half, i, n_i))
            w = cw_ref[half]
            du_ref[half] = (w[0:1, :] * dn + w[1:2, :] * d_ref[half] + w[2:3, :] * up).astype(BF16)

    main, prev, nxt = _halo_specs(tm, s, FF_TC)
    return _call(
        body, name="conv_bwd_input", grid=(ff // FF_TC, n_i),
        in_specs=[main, prev, nxt, pl.BlockSpec((2, 3, FF_TC), lambda j, i: (0, 0, j))],
        out_specs=main, out_shape=_sds((2, s, ff), BF16), sem=("parallel", "parallel"),
    )(dc, dc, dc, cw)


def _ffn_down_loss(a, w_down, x1, target, gate2, ln_g, ln_b, *, tm):
    s, ff = a.shape
    n_i = s // tm

    def body(a_ref, wd_ref, x1_ref, tg_ref, gt_ref, lg_ref, lb_ref, ls_ref, dy_ref, dx_ref, dg_ref, db_ref, dgt_ref):
        @pl.when(pl.program_id(0) == 0)
        def _():
            dg_ref[...] = jnp.zeros(dg_ref.shape, F32)
            db_ref[...] = jnp.zeros(db_ref.shape, F32)
            dgt_ref[...] = jnp.zeros(dgt_ref.shape, F32)

        y2 = _dot(a_ref[...], wd_ref[...])
        z = ALPHA * x1_ref[...] + gt_ref[...] * y2
        mu = jnp.mean(z, axis=-1, keepdims=True)
        zc = z - mu
        r = lax.rsqrt(jnp.mean(zc * zc, axis=-1, keepdims=True) + LN_EPS)
        xhat = zc * r
        diff = xhat * lg_ref[...] + lb_ref[...] - tg_ref[...]
        ls_ref[...] = jnp.full(ls_ref.shape, 0.5 / D_MODEL * jnp.sum(diff * diff), F32)
        dx2 = diff * (1.0 / D_MODEL)
        dg_ref[...] += jnp.sum(dx2 * xhat, axis=0, keepdims=True)
        db_ref[...] += jnp.sum(dx2, axis=0, keepdims=True)
        dz = _ln_bwd(dx2, xhat, r, lg_ref[...])
        dgt_ref[...] += jnp.sum(dz * y2, axis=0, keepdims=True)
        dy_ref[...] = (gt_ref[...] * dz).astype(BF16)
        dx_ref[...] = ALPHA * dz

    row = pl.BlockSpec((tm, D_MODEL), lambda i: (i, 0))
    vec = _full((1, D_MODEL))
    return _call(
        body, name="ffn_down_loss", grid=(n_i,),
        in_specs=[pl.BlockSpec((tm, ff), lambda i: (i, 0)), _full((ff, D_MODEL)), row, row, vec, vec, vec],
        out_specs=[pl.BlockSpec((None, 8, LANES), lambda i: (i, 0, 0)), row, row, vec, vec, vec],
        out_shape=[_sds((n_i, 8, LANES)), _sds((s, D_MODEL), BF16), _sds((s, D_MODEL)),
                   _sds((1, D_MODEL)), _sds((1, D_MODEL)), _sds((1, D_MODEL))],
        sem=("arbitrary",),
    )(a, w_down, x1, target, gate2, ln_g, ln_b)


def _ffn_up_bwd(du, wup4, dx1a, x1, scale2, x, y, mu1, r1, gate1, ln_g, *, tm):
    s = x.shape[0]
    nb, _, ns = wup4.shape

    def body(du_ref, w_ref, dxa_ref, x1_ref, sc_ref, x_ref, y_ref, mu_ref, r_ref, gt_ref, lg_ref,
             dxo_ref, dy_ref, dsc_ref, dsh_ref, dg_ref, db_ref, dgt_ref, acc):
        i, k = pl.program_id(0), pl.program_id(1)

        @pl.when(jnp.logical_and(i == 0, k == 0))
        def _():
            for ref in (dsc_ref, dsh_ref, dg_ref, db_ref, dgt_ref):
                ref[...] = jnp.zeros(ref.shape, F32)

        @pl.when(k == 0)
        def _():
            acc[...] = jnp.zeros(acc.shape, F32)

        acc[...] += _dot_nt(du_ref[...], w_ref[...])

        @pl.when(k == nb - 1)
        def _():
            dh = acc[...]
            x1 = x1_ref[...]
            dsc_ref[...] += jnp.sum(dh * x1, axis=0, keepdims=True)
            dsh_ref[...] += jnp.sum(dh, axis=0, keepdims=True)
            dx1 = dxa_ref[...] + dh * (1.0 + sc_ref[...])
            yv = y_ref[...]
            xhat = (ALPHA * x_ref[...] + gt_ref[...] * yv - mu_ref[...]) * r_ref[...]
            dg_ref[...] += jnp.sum(dx1 * xhat, axis=0, keepdims=True)
            db_ref[...] += jnp.sum(dx1, axis=0, keepdims=True)
            dz = _ln_bwd(dx1, xhat, r_ref[...], lg_ref[...])
            dgt_ref[...] += jnp.sum(dz * yv, axis=0, keepdims=True)
            dy_ref[...] = (gt_ref[...] * dz).astype(BF16)
            dxo_ref[...] = ALPHA * dz

    row = pl.BlockSpec((tm, D_MODEL), lambda i, k: (i, 0))
    col = pl.BlockSpec((tm, 1), lambda i, k: (i, 0))
    vec = _full((1, D_MODEL))
    return _call(
        body, name="ffn_up_bwd", grid=(s // tm, nb),
        in_specs=[pl.BlockSpec((None, tm, ns), lambda i, k: (k // 2, i, k % 2)),
                  pl.BlockSpec((None, D_MODEL, ns), lambda i, k: (k, 0, 0)),
                  row, row, vec, row, row, col, col, vec, vec],
        out_specs=[row, row, vec, vec, vec, vec, vec],
        out_shape=[_sds((s, D_MODEL)), _sds((s, D_MODEL), BF16)] + [_sds((1, D_MODEL))] * 5,
        scratch=[pltpu.VMEM((tm, D_MODEL), F32)],
        sem=("arbitrary", "arbitrary"),
    )(du, wup4, dx1a, x1, scale2, x, y, mu1, r1, gate1, ln_g)


def _mm_nt4_mod_bwd(dp, w4, dxa, x, scale, *, tm, name):
    m = x.shape[0]
    nb, kdim, ns = w4.shape

    def body(dp_ref, w_ref, dxa_ref, x_ref, sc_ref, dx_ref, dsc_ref, dsh_ref, acc):
        i, k = pl.program_id(0), pl.program_id(1)

        @pl.when(jnp.logical_and(i == 0, k == 0))
        def _():
            dsc_ref[...] = jnp.zeros(dsc_ref.shape, F32)
            dsh_ref[...] = jnp.zeros(dsh_ref.shape, F32)

        @pl.when(k == 0)
        def _():
            acc[...] = jnp.zeros(acc.shape, F32)

        acc[...] += _dot_nt(dp_ref[...], w_ref[...])

        @pl.when(k == nb - 1)
        def _():
            dh = acc[...]
            dsc_ref[...] += jnp.sum(dh * x_ref[...], axis=0, keepdims=True)
            dsh_ref[...] += jnp.sum(dh, axis=0, keepdims=True)
            dx_ref[...] = dxa_ref[...] + dh * (1.0 + sc_ref[...])

    row = pl.BlockSpec((tm, kdim), lambda i, k: (i, 0))
    vec = _full((1, kdim))
    return _call(
        body, name=name, grid=(m // tm, nb),
        in_specs=[pl.BlockSpec((tm, ns), lambda i, k: (i, k)), pl.BlockSpec((None, kdim, ns), lambda i, k: (k, 0, 0)),
                  row, row, vec],
        out_specs=[row, vec, vec],
        out_shape=[_sds((m, kdim)), _sds((1, kdim)), _sds((1, kdim))],
        scratch=[pltpu.VMEM((tm, kdim), F32)],
        sem=("arbitrary", "arbitrary"),
    )(dp, w4, dxa, x, scale)


def _pad_heads_w(w):
    w8 = w.reshape(N_HEADS, HEAD_DIM, w.shape[-1])
    z = jnp.zeros_like(w8)
    first = (jnp.arange(N_HEADS) < N_HEADS // N_KV)[:, None, None]
    return jnp.where(first, jnp.concatenate([w8, z], axis=1), jnp.concatenate([z, w8], axis=1))


def _unpad_heads_w(g):
    first = (jnp.arange(N_HEADS) < N_HEADS // N_KV)[:, None, None]
    return jnp.where(first, g[:, :HEAD_DIM], g[:, HEAD_DIM:]).reshape(N_HEADS * HEAD_DIM, g.shape[-1])


def _ones_beside(vt):
    half = vt.shape[0] // 2
    ones = jnp.ones((half, vt.shape[1]), vt.dtype)
    return jnp.stack([jnp.concatenate([vt[:half], ones], axis=0), jnp.concatenate([ones, vt[half:]], axis=0)])


def _rep8(a):
    return jnp.broadcast_to(a.reshape(1, -1), (8, a.size))


def _first_row(a):
    r8 = _rep8(a)
    return jnp.where(lax.broadcasted_iota(jnp.int32, r8.shape, 0) == 0, r8, 0.0)


def _to_blocks4(w):
    k, n = w.shape
    return w.reshape(k, N_CHIPS, n // N_CHIPS).transpose(1, 0, 2)


def _local_step(x, c, ctx, c_ctx, wmod4, b_mod, win4, b_in, sink, qn, kn, wba, wbb, w_out, ln1_g, ln1_b,
                wup4, cw, cb, w_down, ln2_g, ln2_b, target):
    s, nc = x.shape[0], ctx.shape[0]
    tm = min(512, s)
    tm2 = min(256, s)
    zvec = jnp.zeros((1, D_MODEL), F32)

    cc = jnp.concatenate([_rep8(c), _rep8(c_ctx)], axis=0)
    mods = _mm_nn4(cc, zvec, zvec, wmod4, b_mod, mode="silu", split_out=False, out_dtype=F32, tm=16, name="mod_vectors")
    shift1, scale1, gate1, shift2, scale2, gate2 = [mods[0:1, i * D_MODEL:(i + 1) * D_MODEL] for i in range(6)]
    shift_c, scale_c = mods[8:9, :D_MODEL], mods[8:9, D_MODEL:2 * D_MODEL]

    cos, sin = _rope_tables(s)
    cos_c, sin_c = jnp.ones((nc, LANES), F32), jnp.zeros((nc, LANES), F32)
    qg, kg = jnp.tile(qn, (1, 2)), jnp.tile(kn, (1, 2))

    proj_c = _mm_nn4(ctx, shift_c, scale_c, win4, b_in, mode="modulate", split_out=False, out_dtype=F32, tm=nc,
                     name="in_proj_ctx")
    _, kac, vac, _, kbc, vbc = _prep(proj_c, cos_c, sin_c, qg, kg, tm=nc, name="prep_ctx")
    proj = _mm_nn4(x, shift1, scale1, win4, b_in, mode="modulate", split_out=False, out_dtype=F32, tm=tm, name="in_proj")
    qa, ka, va, qb, kb, vb = _prep(proj, cos, sin, qg, kg, tm=tm, name="prep")
    oa, lse_a = _attn_win_fwd(qa, ka, va, kac, vac, sink, tq=tm)
    qbt = jnp.swapaxes(qb, 1, 2)
    obt, lse_b, mrun_b, pbt = _attn_glob_fwd(qbt, kb, _ones_beside(vb.T), kbc, _ones_beside(vbc.T), tq=tm,
                                             tk=min(1024, s))
    ob = jnp.swapaxes(obt, 1, 2)
    wba_p, wbb_p = _pad_heads_w(wba), _pad_heads_w(wbb)
    x1, y, mu1, r1 = _merge_fwd(oa, ob, proj, x, gate1, wba_p, wbb_p, w_out, ln1_g, ln1_b, tm=tm)
    u = _mm_nn4(x1, shift2, scale2, wup4, jnp.zeros((1, 2 * D_FF), F32), mode="modulate", split_out=True,
                out_dtype=F32, tm=tm, name="ffn_up")
    cw2 = cw.reshape(3, 2, D_FF).transpose(1, 0, 2)
    cb2 = cb.reshape(2, 1, D_FF)
    a = _ffn_act_fwd(u, cw2, cb2, tm=tm)
    ls, dy2, dx1a, dln2_g, dln2_b, dgate2 = _ffn_down_loss(a, w_down, x1, target, gate2, ln2_g, ln2_b, tm=tm)
    loss = jnp.sum(ls[:, 0, 0])

    n_s = s // tm
    dw_down = _mm_tn(a, dy2, a_spec=pl.BlockSpec((tm, D_FF), lambda t: (t, 0)),
                     b_spec=pl.BlockSpec((tm, D_MODEL), lambda t: (t, 0)), grid=(n_s,),
                     out_shape=_sds((D_FF, D_MODEL)), out_spec=_full((D_FF, D_MODEL)), name="dw_down")
    dc, dcw2, dcb2 = _ffn_act_bwd(dy2, w_down, u, cw2, cb2, tm=tm)
    du = _conv_bwd_input(dc, cw2, tm=tm)
    dxz1, dy, dscale2, dshift2, dln1_g, dln1_b, dgate1 = _ffn_up_bwd(
        du, wup4, dx1a, x1, scale2, x, y, mu1, r1, gate1, ln1_g, tm=tm)
    ns_up = wup4.shape[-1]
    dw_up4 = _mm_tn(x1, du, a_spec=pl.BlockSpec((tm, D_MODEL), lambda k, t: (t, 0)),
                    b_spec=pl.BlockSpec((None, tm, ns_up), lambda k, t: (k // 2, t, k % 2)), grid=(N_CHIPS, n_s),
                    out_shape=_sds((N_CHIPS, D_MODEL, ns_up)),
                    out_spec=pl.BlockSpec((None, D_MODEL, ns_up), lambda k, t: (k, 0, 0)),
                    mod=(shift2, scale2), name="dw_up")

    dgl, doa, dobt, merged, dwba_p, dwbb_p = _merge_bwd(dy, oa, ob, jnp.swapaxes(oa, 1, 2), obt, proj, wba_p, wbb_p,
                                                        w_out, tm=tm2)
    dwba, dwbb = _unpad_heads_w(dwba_p), _unpad_heads_w(dwbb_p)
    rowspec = pl.BlockSpec((tm, D_MODEL), lambda t: (t, 0))
    dw_out = _mm_tn(merged, dy, a_spec=rowspec, b_spec=rowspec, grid=(n_s,), out_shape=_sds((D_MODEL, D_MODEL)),
                    out_spec=_full((D_MODEL, D_MODEL)), name="dw_out")

    dqa, dla, dkac, dvac, dsk = _attn_win_dq(qa, doa, oa, lse_a, ka, va, kac, vac, sink, tq=tm)
    dka, dva = _attn_win_dkv(qa, doa, lse_a.reshape(N_HEADS, 1, s), dla.reshape(N_HEADS, 1, s), ka, va, tk=tm)
    dqbt, dkbt, dvbt, dkbct, dvbct = _attn_glob_bwd(qbt, dobt, obt, lse_b, mrun_b, pbt, kb.T, vb, kbc.T, vbc, tq=tm, tk=tm)
    dqb, dkb, dvb, dkbc, dvbc = jnp.swapaxes(dqbt, 1, 2), dkbt.T, dvbt.T, dkbct.T, dvbct.T
    dsink = jnp.sum(dsk[:, :, 0, 0], axis=1)

    dproj, dqg, dkg = _prep_bwd(dqa, dka, dva, dqb, dkb, dvb, proj, cos, sin, qg, kg, dgl, tm=tm, name="prep_bwd")
    grad_x, dscale1, dshift1 = _mm_nt4_mod_bwd(dproj, win4, dxz1, x, scale1, tm=tm, name="in_proj_bwd")
    ns_in = win4.shape[-1]
    win_spec = dict(b_spec=pl.BlockSpec((None, None, ns_in), lambda k, t: (0, 0, k)),
                    out_shape=_sds((N_CHIPS, D_MODEL, ns_in)),
                    out_spec=pl.BlockSpec((None, D_MODEL, ns_in), lambda k, t: (k, 0, 0)),
                    colsum_spec=pl.BlockSpec((8, ns_in), lambda k, t: (0, k)), colsum_shape=_sds((8, IN_COLS)))
    win_spec["b_spec"] = pl.BlockSpec((tm, ns_in), lambda k, t: (t, k))
    dw_in4, db_in = _mm_tn(x, dproj, a_spec=pl.BlockSpec((tm, D_MODEL), lambda k, t: (t, 0)), grid=(N_CHIPS, n_s),
                           mod=(shift1, scale1), name="dw_in", **win_spec)

    zq = jnp.zeros((N_HEADS, nc, LANES), F32)
    dproj_c, _, dkg_c = _prep_bwd(zq, dkac, dvac, zq, dkbc, dvbc, proj_c, cos_c, sin_c, qg, kg,
                                  jnp.zeros((nc, IN_COLS - OFF_GA), BF16), tm=nc, name="prep_bwd_ctx")
    _, dscale_c, dshift_c = _mm_nt4_mod_bwd(dproj_c, win4, jnp.zeros((nc, D_MODEL), F32), ctx, scale_c, tm=nc,
                                            name="in_proj_bwd_ctx")
    win_spec["b_spec"] = pl.BlockSpec((nc, ns_in), lambda k, t: (t, k))
    dw_in4, db_in_c = _mm_tn(ctx, dproj_c, a_spec=pl.BlockSpec((nc, D_MODEL), lambda k, t: (t, 0)), grid=(N_CHIPS, 1),
                             mod=(shift_c, scale_c), init=dw_in4, name="dw_in_ctx", **win_spec)

    dmod = jnp.concatenate([dshift1, dscale1, dgate1, dshift2, dscale2, dgate2], axis=1)
    dmodc = jnp.concatenate([dshift_c, dscale_c], axis=1)
    dmodc_pad = jnp.concatenate([dmodc, jnp.zeros((1, 4 * D_MODEL), F32)], axis=1)
    dmodc8 = _first_row(dmodc_pad).astype(BF16)
    z8 = jnp.zeros((8, D_MODEL), F32)
    dsilu_c, _, _ = _mm_nt4_mod_bwd(dmodc8, wmod4, z8, z8, zvec, tm=8, name="c_ctx_bwd")
    sg = _sigmoid(c_ctx)
    dc_ctx = dsilu_c[0:1] * sg * (1.0 + c_ctx * (1.0 - sg))

    dqn = jnp.sum(dqg.reshape(N_HEADS, HEAD_DIM), axis=0, keepdims=True)
    dkn = jnp.sum((dkg + dkg_c).reshape(N_KV, HEAD_DIM), axis=0, keepdims=True)
    grads = dict(
        w_in4=dw_in4, b_in=db_in[0:1] + db_in_c[0:1], sink=dsink, qn=dqn, kn=dkn, wba=dwba, wbb=dwbb, w_out=dw_out,
        ln1_g=dln1_g, ln1_b=dln1_b, w_up4=dw_up4, conv_w=dcw2.transpose(1, 0, 2).reshape(3, 2 * D_FF),
        conv_b=dcb2.reshape(1, 2 * D_FF), w_down=dw_down, ln2_g=dln2_g, ln2_b=dln2_b,
        c_ctx=dc_ctx, dmod=dmod, dmodc=dmodc)
    return loss, grad_x, grads


ANY = pl.BlockSpec(memory_space=pl.ANY)


def _mesh_pos():
    return lax.axis_index("x"), lax.axis_index("y"), lax.axis_index("c")


def _other_chips(x, y):
    return [(1 - x, y), (x, 1 - y), (1 - x, 1 - y)]


def _remote(src, dst, send, recv, dev):
    return pltpu.make_async_remote_copy(src_ref=src, dst_ref=dst, send_sem=send, recv_sem=recv, device_id=dev,
                                        device_id_type=MESH)


def _set_block(stack, block, k):
    return lax.dynamic_update_slice(stack, block[None], (k,) + (0,) * block.ndim)


def _gather_shards(arrs, small):
    na = len(arrs)
    halves = [a.shape[0] // 2 for a in arrs]

    def body(*refs):
        ins, small_ref = refs[:na], refs[na]
        outs, small_out = refs[na + 1:2 * na + 1], refs[2 * na + 1]
        send, recv = refs[2 * na + 2:]
        x, y, c = _mesh_pos()
        me = 2 * x + y
        chips = _other_chips(x, y)

        def half(a, cc):
            return pl.ds(cc * halves[a], halves[a])

        sends = []
        for j, chip in enumerate(chips):
            for a in range(na):
                sends.append(_remote(ins[a].at[half(a, c)], outs[a].at[me, half(a, c)], send.at[a, j], recv.at[a, j],
                                     (*chip, c)))
            sends.append(_remote(small_ref, small_out.at[me], send.at[na, j], recv.at[na, j], (*chip, c)))
        for cp in sends:
            cp.start()
        for j, chip in enumerate(chips):
            kj = 2 * chip[0] + chip[1]
            for a in range(na):
                landed = outs[a].at[kj, half(a, c)]
                _remote(landed, landed, send.at[a, j], recv.at[a, j], (*chip, c)).wait_recv()
                fwd = _remote(landed, landed, send.at[a, 3 + j], recv.at[a, 3 + j], (x, y, 1 - c))
                fwd.start()
                sends.append(fwd)
            _remote(small_ref, small_out.at[kj], send.at[na, j], recv.at[na, j], (*chip, c)).wait_recv()
        for j, chip in enumerate(chips):
            kj = 2 * chip[0] + chip[1]
            for a in range(na):
                other = outs[a].at[kj, half(a, 1 - c)]
                _remote(other, other, send.at[a, 3 + j], recv.at[a, 3 + j], (x, y, 1 - c)).wait_recv()
        for cp in sends:
            cp.wait_send()

    out_shape = [_sds((N_CHIPS,) + a.shape, a.dtype) for a in arrs] + [_sds((N_CHIPS,) + small.shape, small.dtype)]
    got = pl.pallas_call(
        body, name="gather_shards", in_specs=[ANY] * (na + 1), out_specs=[ANY] * (na + 1), out_shape=out_shape,
        scratch_shapes=[pltpu.SemaphoreType.DMA((na + 1, 6)), pltpu.SemaphoreType.DMA((na + 1, 6))],
    )(*arrs, small)
    xp, yp, _ = _mesh_pos()
    return [_set_block(g, a, 2 * xp + yp) for g, a in zip(got, list(arrs) + [small])]


def _allgather_rows(v):
    r, n = v.shape

    def body(v_ref, out_ref, send, recv, loc):
        x, y, c = _mesh_pos()
        me, sibling = (x, y, c), (x, y, 1 - c)
        chips = _other_chips(x, y)

        def rows(px, py, pc):
            return out_ref.at[4 * px + 2 * py + pc]

        def copy(k, block, to, src=None):
            return _remote(rows(*block) if src is None else src, rows(*block), send.at[k], recv.at[k], to)

        mine = pltpu.make_async_copy(v_ref, rows(*me), loc)
        mine.start()
        first = [copy(0, me, sibling, src=v_ref)] + [copy(1 + j, me, (*chip, c), src=v_ref) for j, chip in enumerate(chips)]
        for cp in first:
            cp.start()
        passed = [copy(4 + j, (*chip, c), sibling) for j, chip in enumerate(chips)]
        for j, chip in enumerate(chips):
            copy(1 + j, (*chip, c), me).wait_recv()
            passed[j].start()
        copy(0, sibling, me).wait_recv()
        for j, chip in enumerate(chips):
            copy(4 + j, (*chip, 1 - c), me).wait_recv()
        for cp in first + passed:
            cp.wait_send()
        mine.wait()

    return pl.pallas_call(
        body, name="allgather_rows", in_specs=[pl.BlockSpec(memory_space=pltpu.VMEM)],
        out_specs=pl.BlockSpec(memory_space=pltpu.VMEM), out_shape=_sds((N_DEV, r, n), v.dtype),
        scratch_shapes=[pltpu.SemaphoreType.DMA((7,)), pltpu.SemaphoreType.DMA((7,)), pltpu.SemaphoreType.DMA],
    )(v)


def _swap_other_half(g):
    nb, r, n = g.shape
    rh = r // 2

    def body(g_ref, out_ref, send, recv):
        x, y, c = _mesh_pos()
        cp = _remote(g_ref.at[:, pl.ds((1 - c) * rh, rh), :], out_ref, send, recv, (x, y, 1 - c))
        cp.start()
        cp.wait()

    return pl.pallas_call(
        body, name="swap_other_half", in_specs=[ANY], out_specs=ANY, out_shape=_sds((nb, rh, n), g.dtype),
        scratch_shapes=[pltpu.SemaphoreType.DMA, pltpu.SemaphoreType.DMA],
    )(g)


def _scatter_to_chips(p):
    def body(p_ref, out_ref, send, recv):
        x, y, c = _mesh_pos()
        me = 2 * x + y
        chips = _other_chips(x, y)
        sends = [_remote(p_ref.at[2 * chip[0] + chip[1]], out_ref.at[me], send.at[j], recv.at[j], (*chip, c))
                 for j, chip in enumerate(chips)]
        for cp in sends:
            cp.start()
        for j, chip in enumerate(chips):
            kj = 2 * chip[0] + chip[1]
            _remote(p_ref.at[kj], out_ref.at[kj], send.at[j], recv.at[j], (*chip, c)).wait_recv()
        for cp in sends:
            cp.wait_send()

    got = pl.pallas_call(
        body, name="scatter_to_chips", in_specs=[ANY], out_specs=ANY, out_shape=_sds(p.shape, p.dtype),
        scratch_shapes=[pltpu.SemaphoreType.DMA((3,)), pltpu.SemaphoreType.DMA((3,))],
    )(p)
    xp, yp, _ = _mesh_pos()
    me = 2 * xp + yp
    return _set_block(got, lax.dynamic_index_in_dim(p, me, axis=0, keepdims=False), me)


def _join_halves(f):
    def body(f_ref, out_ref, send, recv):
        x, y, c = _mesh_pos()
        cp = _remote(f_ref, out_ref, send, recv, (x, y, 1 - c))
        cp.start()
        cp.wait()

    other = pl.pallas_call(
        body, name="join_halves", in_specs=[ANY], out_specs=ANY, out_shape=_sds(f.shape, f.dtype),
        scratch_shapes=[pltpu.SemaphoreType.DMA, pltpu.SemaphoreType.DMA],
    )(f)
    first = lax.axis_index("c") == 0
    return jnp.concatenate([jnp.where(first, f, other), jnp.where(first, other, f)], axis=0)


def _row_tile(rows, cap=512):
    t = cap - cap % 8
    while rows % t:
        t -= 8
    return t


def _add_blocks(a, b, out_dtype):
    nb, r, n = a.shape
    tr = _row_tile(r)

    def body(a_ref, b_ref, o_ref):
        o_ref[...] = (a_ref[...] + b_ref[...]).astype(out_dtype)

    spec = pl.BlockSpec((None, tr, n), lambda k, i: (k, i, 0))
    return _call(body, name="add_blocks", grid=(nb, r // tr), in_specs=[spec, spec], out_specs=spec,
                 out_shape=_sds(a.shape, out_dtype), sem=("parallel", "parallel"))(a, b)


def _sum_leading(a, *, name):
    nk, r, n = a.shape
    tr = _row_tile(r)

    def body(a_ref, o_ref):
        acc = a_ref[0].astype(F32)
        for k in range(1, nk):
            acc = acc + a_ref[k].astype(F32)
        o_ref[...] = acc

    return _call(body, name=name, grid=(r // tr,), in_specs=[pl.BlockSpec((nk, tr, n), lambda i: (0, i, 0))],
                 out_specs=pl.BlockSpec((tr, n), lambda i: (i, 0)), out_shape=_sds((r, n)), sem=("parallel",))(a)


def _silu_outer(a, b):
    kdim, n = a.shape[1], b.shape[1]

    def body(a_ref, b_ref, o_ref):
        av = a_ref[...]
        av = av * _sigmoid(av)
        bv = b_ref[...]
        ah, bh = av.astype(BF16), bv.astype(BF16)
        al, bl = (av - ah.astype(F32)).astype(BF16), (bv - bh.astype(F32)).astype(BF16)
        o_ref[...] = _dot_tn(ah, bh) + (_dot_tn(ah, bl) + _dot_tn(al, bh))

    return _call(body, name="dw_mod", grid=(1,), in_specs=[_full(a.shape), _full(b.shape)], out_specs=_full((kdim, n)),
                 out_shape=_sds((kdim, n)))(a, b)


def _adamw(w, g, m, v):
    r, n = w.shape
    tr = _row_tile(r)

    def body(w_ref, g_ref, m_ref, v_ref, d_ref, nm_ref, nv_ref):
        gv = g_ref[...]
        nm = ADAM_B1 * m_ref[...] + (1.0 - ADAM_B1) * gv
        nv = ADAM_B2 * v_ref[...] + (1.0 - ADAM_B2) * (gv * gv)
        m_hat = nm / (1.0 - ADAM_B1 ** ADAM_STEP)
        v_hat = nv / (1.0 - ADAM_B2 ** ADAM_STEP)
        d_ref[...] = -ADAM_LR * (m_hat / (jnp.sqrt(v_hat) + ADAM_EPS) + ADAM_WD * w_ref[...])
        nm_ref[...] = nm
        nv_ref[...] = nv

    spec = pl.BlockSpec((tr, n), lambda i: (i, 0))
    return _call(body, name="adamw", grid=(r // tr,), in_specs=[spec] * 4, out_specs=[spec] * 3,
                 out_shape=[_sds((r, n))] * 3, sem=("parallel",))(w, g, m, v)


BIG = ("w_in", "w_branch_a", "w_branch_b", "w_out", "w_up", "w_down", "conv_w")
BIG_ROWS = 3584
MATRICES = ("w_mod", "w_in", "w_branch_a", "w_branch_b", "w_out", "w_up", "w_down")
SMALL = ("b_mod", "b_in", "conv_b", "ln1_g", "ln1_b", "ln2_g", "ln2_b", "c_ctx", "attn_sink", "q_norm_g", "k_norm_g", "conv_w")
SMALL_ROWS = 8 * len(SMALL)


def _rows(a, n_rows):
    flat = a.reshape(-1)
    return jnp.pad(flat, (0, n_rows * D_MODEL - flat.shape[0])).reshape(n_rows, D_MODEL)


def _group8(a):
    return _rep8(_rows(a, 1)) if a.size <= D_MODEL else _rows(a, 8)


def _ungroup8(p, shape):
    size = math.prod(shape)
    return (p[0, :size] if size <= D_MODEL else p.reshape(-1)[:size]).reshape(shape)


def _unpack_big(p, like):
    out, r = {}, 0
    for n in BIG:
        size = math.prod(like[n].shape)
        nr = size // D_MODEL if n != "conv_w" else 8
        out[n] = p[r:r + nr].reshape(-1)[:size].reshape(like[n].shape)
        r += nr
    return out


def _pack_small(t):
    return jnp.concatenate([_group8(t[n]) for n in SMALL], axis=0)


def _unpack_small(p, like):
    return {n: _ungroup8(p[8 * i:8 * i + 8], like[n].shape) for i, n in enumerate(SMALL)}


WEIGHTS = ("c_ctx", "w_mod", "b_mod", "w_in", "b_in", "attn_sink", "q_norm_g", "k_norm_g", "w_branch_a", "w_branch_b",
           "w_out", "ln1_g", "ln1_b", "w_up", "conv_w", "conv_b", "w_down", "ln2_g", "ln2_b")


def kernel(x, c, ctx, c_ctx, w_mod, b_mod, w_in, b_in, attn_sink, q_norm_g, k_norm_g, w_branch_a, w_branch_b, w_out, ln1_g, ln1_b, w_up, conv_w, conv_b, w_down, ln2_g, ln2_b, loss_target, m_c_ctx, m_w_mod, m_b_mod, m_w_in, m_b_in, m_attn_sink, m_q_norm_g, m_k_norm_g, m_w_branch_a, m_w_branch_b, m_w_out, m_ln1_g, m_ln1_b, m_w_up, m_conv_w, m_conv_b, m_w_down, m_ln2_g, m_ln2_b, v_c_ctx, v_w_mod, v_b_mod, v_w_in, v_b_in, v_attn_sink, v_q_norm_g, v_k_norm_g, v_w_branch_a, v_w_branch_b, v_w_out, v_ln1_g, v_ln1_b, v_w_up, v_conv_w, v_conv_b, v_w_down, v_ln2_g, v_ln2_b):
    w = dict(c_ctx=c_ctx, w_mod=w_mod, b_mod=b_mod, w_in=w_in, b_in=b_in, attn_sink=attn_sink, q_norm_g=q_norm_g,
             k_norm_g=k_norm_g, w_branch_a=w_branch_a, w_branch_b=w_branch_b, w_out=w_out, ln1_g=ln1_g, ln1_b=ln1_b,
             w_up=w_up, conv_w=conv_w, conv_b=conv_b, w_down=w_down, ln2_g=ln2_g, ln2_b=ln2_b)
    m = dict(c_ctx=m_c_ctx, w_mod=m_w_mod, b_mod=m_b_mod, w_in=m_w_in, b_in=m_b_in, attn_sink=m_attn_sink,
             q_norm_g=m_q_norm_g, k_norm_g=m_k_norm_g, w_branch_a=m_w_branch_a, w_branch_b=m_w_branch_b, w_out=m_w_out,
             ln1_g=m_ln1_g, ln1_b=m_ln1_b, w_up=m_w_up, conv_w=m_conv_w, conv_b=m_conv_b, w_down=m_w_down,
             ln2_g=m_ln2_g, ln2_b=m_ln2_b)
    v = dict(c_ctx=v_c_ctx, w_mod=v_w_mod, b_mod=v_b_mod, w_in=v_w_in, b_in=v_b_in, attn_sink=v_attn_sink,
             q_norm_g=v_q_norm_g, k_norm_g=v_k_norm_g, w_branch_a=v_w_branch_a, w_branch_b=v_w_branch_b, w_out=v_w_out,
             ln1_g=v_ln1_g, ln1_b=v_ln1_b, w_up=v_w_up, conv_w=v_conv_w, conv_b=v_conv_b, w_down=v_w_down,
             ln2_g=v_ln2_g, ln2_b=v_ln2_b)
    xp, yp, _ = _mesh_pos()
    me = 2 * xp + yp

    branches = jnp.concatenate([w_branch_a[0], w_branch_b[0]], axis=0)
    wide = jnp.concatenate([w_mod[0], w_in[0], w_up[0], branches], axis=1).astype(BF16)
    tall = jnp.concatenate([w_out[0], w_down[0]], axis=0).astype(BF16)
    wide4, tall4, cw4 = _gather_shards([wide, tall], conv_w[0])
    n_mod, n_in, n_up = w_mod.shape[-1], w_in.shape[-1], w_up.shape[-1]
    wmod4 = wide4[:, :, :n_mod]
    win4 = wide4[:, :, n_mod:n_mod + n_in]
    wup4 = wide4[:, :, n_mod + n_in:n_mod + n_in + n_up]
    br4 = wide4[:, :, n_mod + n_in + n_up:]
    n_br = w_branch_a.shape[1]
    wba = br4[:, :n_br].transpose(1, 0, 2).reshape(n_br, D_MODEL)
    wbb = br4[:, n_br:].transpose(1, 0, 2).reshape(n_br, D_MODEL)
    n_out = w_out.shape[1]
    w_out_full = tall4[:, :n_out].reshape(D_MODEL, D_MODEL)
    w_down_full = tall4[:, n_out:].reshape(D_FF, D_MODEL)
    cw_full = cw4.transpose(1, 0, 2).reshape(3, 2 * D_FF)

    loss, grad_x, g = _local_step(
        x[0], c, ctx[0], c_ctx[None], wmod4, b_mod, win4, b_in, attn_sink[0], q_norm_g, k_norm_g, wba, wbb, w_out_full,
        ln1_g, ln1_b, wup4, cw_full, conv_b, w_down_full, ln2_g, ln2_b, loss_target[0])
    loss = lax.psum(loss, ("x", "y", "c"))

    sent = dict(c=c, dmod=g["dmod"], dmodc=g["dmodc"], b_in=g["b_in"], conv_b=g["conv_b"], ln1_g=g["ln1_g"],
                ln1_b=g["ln1_b"], ln2_g=g["ln2_g"], ln2_b=g["ln2_b"], c_ctx=g["c_ctx"], attn_sink=g["sink"],
                q_norm_g=g["qn"], k_norm_g=g["kn"])
    every = _allgather_rows(jnp.concatenate([_group8(a) for a in sent.values()], axis=0))
    total = _sum_leading(every, name="sum_devices")
    slot = {n: slice(8 * i, 8 * i + 8) for i, n in enumerate(sent)}
    gs = {n: _ungroup8(total[slot[n]], sent[n].shape) for n in SMALL if n in sent}
    dmodc_sum = jnp.concatenate([_ungroup8(total[slot["dmodc"]], (1, 2 * D_MODEL)), jnp.zeros((1, 4 * D_MODEL), F32)],
                                axis=1)
    gs["b_mod"] = _ungroup8(total[slot["dmod"]], b_mod.shape) + dmodc_sum
    acts = jnp.concatenate([every[:, slot["c"].start], _rep8(c_ctx)], axis=0)
    dmods = jnp.concatenate([every[:, slot["dmod"]].reshape(N_DEV, -1)[:, :6 * D_MODEL], _first_row(dmodc_sum)], axis=0)
    g_w_mod = _silu_outer(acts, lax.dynamic_slice_in_dim(dmods, me * n_mod, n_mod, axis=1))

    cw_g4 = _to_blocks4(g["conv_w"])
    packed = jnp.concatenate([
        g["w_in4"].reshape(N_CHIPS, -1, D_MODEL), _to_blocks4(g["wba"]).reshape(N_CHIPS, -1, D_MODEL),
        _to_blocks4(g["wbb"]).reshape(N_CHIPS, -1, D_MODEL), g["w_out"].reshape(N_CHIPS, -1, D_MODEL),
        g["w_up4"].reshape(N_CHIPS, -1, D_MODEL), g["w_down"].reshape(N_CHIPS, -1, D_MODEL),
        jnp.pad(cw_g4.reshape(N_CHIPS, -1), ((0, 0), (0, 8 * D_MODEL - cw_g4.shape[1] * cw_g4.shape[2]))).reshape(
            N_CHIPS, 8, D_MODEL),
        jnp.zeros((N_CHIPS, BIG_ROWS - 3528, D_MODEL), F32)], axis=1)
    rh = BIG_ROWS // 2
    cpos = lax.axis_index("c")
    my_half = lax.dynamic_slice_in_dim(packed, cpos * rh, rh, axis=1)
    chip_sum = _add_blocks(my_half, _swap_other_half(packed), BF16)
    half_sum = _sum_leading(_scatter_to_chips(chip_sum), name="sum_chips")
    g_big = _unpack_big(_join_halves(half_sum), w)

    grads = dict(gs, w_mod=g_w_mod, **g_big)
    grads = {n: grads[n].reshape(w[n].shape) for n in WEIGHTS}
    delta, new_m, new_v = {}, {}, {}
    for n in MATRICES:
        outs = _adamw(*[t[n][0] for t in (w, grads, m, v)])
        delta[n], new_m[n], new_v[n] = [o[None] for o in outs]
    outs = _adamw(*[_pack_small(t) for t in (w, grads, m, v)])
    for res, o in zip((delta, new_m, new_v), outs):
        res.update(_unpack_small(o, w))
    return (loss, grad_x[None], *[grads[n] for n in WEIGHTS], *[delta[n] for n in WEIGHTS],
            *[new_m[n] for n in WEIGHTS], *[new_v[n] for n in WEIGHTS])
```

```python
import functools
import math

import jax
import jax.numpy as jnp
from jax import lax
from jax.experimental import pallas as pl
from jax.experimental.pallas import tpu as pltpu

F32 = jnp.float32
BF16 = jnp.bfloat16

D_MODEL = 1024
HEAD_DIM = 64
N_HEADS = 8
N_KV = 2
WINDOW = 128
GRID_W = 64
ROPE_THETA = 10000.0
D_FF = 2816
LN_EPS = 1e-5
QK_EPS = 1e-6
ALPHA = 2.0 ** 0.25
Q_SCALE = HEAD_DIM ** -0.5
OFF_GA = 1536
IN_COLS = 3584
ADAM_LR, ADAM_B1, ADAM_B2, ADAM_EPS, ADAM_WD, ADAM_STEP = 0.001, 0.9, 0.999, 1e-8, 0.01, 10

LANES = 128
VMEM_BUDGET = 52 * 1024 * 1024
N_CHIPS = 4
N_DEV = 8
NEG = -1e30
MESH = pl.DeviceIdType.MESH


def _sigmoid(x):
    return 1.0 / (1.0 + jnp.exp(-x))


def _dot(a, b):
    return jnp.dot(a, b, preferred_element_type=F32)


def _dot_nt(a, b):
    return lax.dot_general(a, b, (((1,), (1,)), ((), ())), preferred_element_type=F32)


def _dot_tn(a, b):
    return lax.dot_general(a, b, (((0,), (0,)), ((), ())), preferred_element_type=F32)


def _call(body, *, name, grid, in_specs, out_specs, out_shape, scratch=(), sem=None, **kw):
    params = dict(vmem_limit_bytes=VMEM_BUDGET)
    if sem is not None:
        params["dimension_semantics"] = sem
    return pl.pallas_call(body, name=name, grid=grid, in_specs=in_specs, out_specs=out_specs,
                          out_shape=out_shape, scratch_shapes=list(scratch),
                          compiler_params=pltpu.CompilerParams(**params), **kw)


def _full(shape):
    n = len(shape)
    return pl.BlockSpec(shape, lambda *_: (0,) * n)


def _sds(shape, dtype=F32):
    return jax.ShapeDtypeStruct(shape, dtype)


def _mm_nn4(a, shift, scale, w4, bias, *, mode, split_out, out_dtype, tm, name):
    m, kdim = a.shape
    nb, _, ns = w4.shape

    def body(a_ref, sh_ref, sc_ref, w_ref, b_ref, o_ref):
        av = a_ref[...]
        if mode == "modulate":
            av = av * (1.0 + sc_ref[...]) + sh_ref[...]
        else:
            av = av * _sigmoid(av)
        o_ref[...] = (_dot(av.astype(BF16), w_ref[...]) + b_ref[...]).astype(out_dtype)

    if split_out:
        out_shape = _sds((2, m, 2 * ns), out_dtype)
        out_spec = pl.BlockSpec((None, tm, ns), lambda i, k: (k // 2, i, k % 2))
    else:
        out_shape = _sds((m, nb * ns), out_dtype)
        out_spec = pl.BlockSpec((tm, ns), lambda i, k: (i, k))
    return _call(
        body, name=name, grid=(m // tm, nb),
        in_specs=[pl.BlockSpec((tm, kdim), lambda i, k: (i, 0)),
                  pl.BlockSpec((1, kdim), lambda i, k: (0, 0)),
                  pl.BlockSpec((1, kdim), lambda i, k: (0, 0)),
                  pl.BlockSpec((None, kdim, ns), lambda i, k: (k, 0, 0)),
                  pl.BlockSpec((1, ns), lambda i, k: (0, k))],
        out_specs=out_spec, out_shape=out_shape, sem=("parallel", "arbitrary"),
    )(a, shift, scale, w4, bias)


def _mm_tn(a, b, *, a_spec, b_spec, grid, out_shape, out_spec, name, mod=None, init=None, colsum_spec=None,
           colsum_shape=None):
    red = len(grid) - 1
    has_mod, has_init, has_cs = mod is not None, init is not None, colsum_spec is not None

    def body(*refs):
        refs = list(refs)
        a_ref, b_ref = refs[0], refs[1]
        pos = 2
        if has_mod:
            sh_ref, sc_ref = refs[2], refs[3]
            pos = 4
        if has_init:
            init_ref = refs[pos]
            pos += 1
        o_ref = refs[pos]
        cs_ref = refs[pos + 1] if has_cs else None
        s = pl.program_id(red)

        @pl.when(s == 0)
        def _():
            o_ref[...] = init_ref[...] if has_init else jnp.zeros(o_ref.shape, F32)
            if has_cs:
                cs_ref[...] = jnp.zeros(cs_ref.shape, F32)

        av = a_ref[...]
        if has_mod:
            av = av * (1.0 + sc_ref[...]) + sh_ref[...]
        bv = b_ref[...]
        o_ref[...] += _dot_tn(av.astype(BF16), bv)
        if has_cs:
            cs_ref[...] += jnp.broadcast_to(jnp.sum(bv.astype(F32), axis=0, keepdims=True), cs_ref.shape)

    ins, in_specs = [a, b], [a_spec, b_spec]
    if has_mod:
        kdim = mod[0].shape[-1]
        ins += list(mod)
        in_specs += [_full((1, kdim)), _full((1, kdim))]
    if has_init:
        ins.append(init)
        in_specs.append(out_spec)
    out_specs, out_shapes = out_spec, out_shape
    if has_cs:
        out_specs, out_shapes = [out_spec, colsum_spec], [out_shape, colsum_shape]
    sem = ("parallel",) * red + ("arbitrary",)
    return _call(body, name=name, grid=grid, in_specs=in_specs, out_specs=out_specs, out_shape=out_shapes,
                 sem=sem)(*ins)


def _rope_tables(n_tok):
    pos = jnp.arange(n_tok, dtype=jnp.int32)
    rows = (pos // GRID_W).astype(F32)
    cols = (pos % GRID_W).astype(F32)
    n_freq = HEAD_DIM // 4
    inv_freq = ROPE_THETA ** (-jnp.arange(n_freq, dtype=F32) / n_freq)
    ang_r = rows[:, None] * inv_freq
    ang_c = cols[:, None] * inv_freq
    cos = jnp.concatenate([jnp.cos(ang_r)] * 2 + [jnp.cos(ang_c)] * 2, axis=-1)
    sin = jnp.concatenate([-jnp.sin(ang_r), jnp.sin(ang_r), -jnp.sin(ang_c), jnp.sin(ang_c)], axis=-1)
    return jnp.tile(cos, (1, 2)), jnp.tile(sin, (1, 2))


def _lane(shape):
    return lax.broadcasted_iota(jnp.int32, shape, 1)


def _rope_partner(t, lane):
    return jnp.where((lane % 32) < 16, pltpu.roll(t, LANES - 16, 1), pltpu.roll(t, 16, 1))


def _half_mean(s, lane):
    lo = jnp.sum(jnp.where(lane < HEAD_DIM, s, 0.0), axis=-1, keepdims=True)
    hi = jnp.sum(jnp.where(lane < HEAD_DIM, 0.0, s), axis=-1, keepdims=True)
    return jnp.where(lane < HEAD_DIM, lo, hi) * (1.0 / HEAD_DIM)


def _prep(proj, cos, sin, qg, kg, *, tm, name):
    m = proj.shape[0]

    def body(p_ref, cos_ref, sin_ref, qg_ref, kg_ref, qa_ref, ka_ref, va_ref, qb_ref, kb_ref, vb_ref):
        lane = _lane((tm, LANES))
        cosv, sinv = cos_ref[...], sin_ref[...]
        low = lane < HEAD_DIM

        def rope(t):
            return t * cosv + _rope_partner(t, lane) * sinv

        def rms(t, g):
            return t * lax.rsqrt(_half_mean(t * t, lane) + QK_EPS) * g

        def place(q_ref, j, chunk):
            sw = pltpu.roll(chunk, HEAD_DIM, 1)
            if j < 2:
                h0, h1 = jnp.where(low, chunk, 0.0), jnp.where(low, sw, 0.0)
            else:
                h0, h1 = jnp.where(low, 0.0, sw), jnp.where(low, 0.0, chunk)
            q_ref[2 * j] = h0.astype(BF16)
            q_ref[2 * j + 1] = h1.astype(BF16)

        for j in range(4):
            place(qa_ref, j, rope(p_ref[:, j * LANES:(j + 1) * LANES]) * Q_SCALE)
            place(qb_ref, j, rope(rms(p_ref[:, 768 + j * LANES:768 + (j + 1) * LANES], qg_ref[...])) * Q_SCALE)
        ka_ref[...] = rope(p_ref[:, 512:640]).astype(BF16)
        va_ref[...] = p_ref[:, 640:768].astype(BF16)
        kb_ref[...] = rope(rms(p_ref[:, 1280:1408], kg_ref[...])).astype(BF16)
        vb_ref[...] = p_ref[:, 1408:1536].astype(BF16)

    row = pl.BlockSpec((tm, LANES), lambda i: (i, 0))
    qspec = pl.BlockSpec((N_HEADS, tm, LANES), lambda i: (0, i, 0))
    return _call(
        body, name=name, grid=(m // tm,),
        in_specs=[pl.BlockSpec((tm, OFF_GA), lambda i: (i, 0)), row, row, _full((1, LANES)), _full((1, LANES))],
        out_specs=[qspec, row, row, qspec, row, row],
        out_shape=[_sds((N_HEADS, m, LANES), BF16), _sds((m, LANES), BF16), _sds((m, LANES), BF16),
                   _sds((N_HEADS, m, LANES), BF16), _sds((m, LANES), BF16), _sds((m, LANES), BF16)],
        sem=("parallel",),
    )(proj, cos, sin, qg, kg)


def _prep_bwd(dqa, dka, dva, dqb, dkb, dvb, proj, cos, sin, qg, kg, dgl, *, tm, name):
    m = proj.shape[0]

    def body(dqa_ref, dka_ref, dva_ref, dqb_ref, dkb_ref, dvb_ref, p_ref, cos_ref, sin_ref, qg_ref, kg_ref,
             dgl_ref, dp_ref, dqg_ref, dkg_ref):
        i = pl.program_id(0)
        lane = _lane((tm, LANES))
        cosv, sinv = cos_ref[...], sin_ref[...]
        low = lane < HEAD_DIM

        @pl.when(i == 0)
        def _():
            dqg_ref[...] = jnp.zeros(dqg_ref.shape, F32)
            dkg_ref[...] = jnp.zeros(dkg_ref.shape, F32)

        def unrope(d):
            return d * cosv - _rope_partner(d, lane) * sinv

        def unplace(dq_ref, j):
            d0, d1 = dq_ref[2 * j], dq_ref[2 * j + 1]
            if j < 2:
                return jnp.where(low, d0, pltpu.roll(d1, HEAD_DIM, 1))
            return jnp.where(low, pltpu.roll(d0, HEAD_DIM, 1), d1)

        def unrms(dtn, t, g):
            r = lax.rsqrt(_half_mean(t * t, lane) + QK_EPS)
            u = dtn * g
            dt = r * u - t * (r * r * r) * _half_mean(u * t, lane)
            return dt, jnp.sum(dtn * t * r, axis=0, keepdims=True)

        for j in range(4):
            dp_ref[:, j * LANES:(j + 1) * LANES] = (unrope(unplace(dqa_ref, j)) * Q_SCALE).astype(BF16)
            c0 = 768 + j * LANES
            dt, dg = unrms(unrope(unplace(dqb_ref, j)) * Q_SCALE, p_ref[:, c0:c0 + LANES], qg_ref[...])
            dp_ref[:, c0:c0 + LANES] = dt.astype(BF16)
            dqg_ref[:, j * LANES:(j + 1) * LANES] += dg
        dp_ref[:, 512:640] = unrope(dka_ref[...]).astype(BF16)
        dp_ref[:, 640:768] = dva_ref[...].astype(BF16)
        dt, dg = unrms(unrope(dkb_ref[...]), p_ref[:, 1280:1408], kg_ref[...])
        dp_ref[:, 1280:1408] = dt.astype(BF16)
        dkg_ref[...] += dg
        dp_ref[:, 1408:1536] = dvb_ref[...].astype(BF16)
        dp_ref[:, OFF_GA:] = dgl_ref[...]

    row = pl.BlockSpec((tm, LANES), lambda i: (i, 0))
    qspec = pl.BlockSpec((N_HEADS, tm, LANES), lambda i: (0, i, 0))
    return _call(
        body, name=name, grid=(m // tm,),
        in_specs=[qspec, row, row, qspec, row, row, pl.BlockSpec((tm, OFF_GA), lambda i: (i, 0)), row, row,
                  _full((1, LANES)), _full((1, LANES)), pl.BlockSpec((tm, IN_COLS - OFF_GA), lambda i: (i, 0))],
        out_specs=[pl.BlockSpec((tm, IN_COLS), lambda i: (i, 0)), _full((1, 512)), _full((1, LANES))],
        out_shape=[_sds((m, IN_COLS), BF16), _sds((1, 512)), _sds((1, LANES))],
        sem=("arbitrary",),
    )(dqa, dka, dva, dqb, dkb, dvb, proj, cos, sin, qg, kg, dgl)


def _attn_glob_fwd(qt, k, vt, kc, vct, *, tq, tk):
    nh, _, s = qt.shape
    nc = kc.shape[0]
    n_chunks = s // tk
    half = LANES // 2

    def body(qt_ref, k_ref, vt_ref, kc_ref, vct_ref, ot_ref, lse_ref, mrun_ref, p_hbm,
             acc_sc, st_sc, stage_sc, stagec_sc, sems, semc):
        h, i = pl.program_id(0), pl.program_id(1)
        qtv = qt_ref[...]
        acc_sc[...] = jnp.zeros(acc_sc.shape, F32)

        def p_out(slot, c):
            return pltpu.make_async_copy(stage_sc.at[slot], p_hbm.at[h, i, pl.ds(pl.multiple_of(c * tk, tk), tk), :],
                                         sems.at[slot])

        def update(st, vtv, m_old):
            m_new = jnp.maximum(m_old, jnp.max(st, axis=0, keepdims=True))
            pb = jnp.exp(st - m_new).astype(BF16)
            acc_sc[...] = acc_sc[...] * jnp.exp(m_old - m_new) + _dot(vtv, pb)
            return m_new, pb

        m, pbc = update(_dot(kc_ref[...], qtv), vct_ref[...], jnp.full((1, tq), NEG, F32))
        mrun_ref[pl.ds(n_chunks, 1), :] = m
        stagec_sc[...] = pbc
        ctx_out = pltpu.make_async_copy(stagec_sc, p_hbm.at[h, i, pl.ds(s, nc), :], semc)
        ctx_out.start()

        def step(c, st, m_old):
            slot = c % 2
            off = pl.multiple_of(c * tk, tk)
            nxt = pl.multiple_of(jnp.minimum(c + 1, n_chunks - 1) * tk, tk)
            st_next = _dot(k_ref[pl.ds(nxt, tk), :], qtv)
            m_new, pb = update(st, vt_ref[:, pl.ds(off, tk)], m_old)
            mrun_ref[pl.ds(c, 1), :] = m_new
            stage_sc[slot] = pb
            p_out(slot, c).start()
            return st_next, m_new

        def loop(c, m_old):
            st_next, m_new = step(c, st_sc[...], m_old)
            p_out(1 - c % 2, c - 1).wait()
            st_sc[...] = st_next
            return m_new

        stage_sc[1] = jnp.zeros((tk, tq), BF16)
        pltpu.make_async_copy(stage_sc.at[1], p_hbm.at[h, i, pl.ds(s + nc, tk), :], sems.at[1]).start()
        st_sc[...] = _dot(k_ref[pl.ds(0, tk), :], qtv)
        m = lax.fori_loop(0, n_chunks, loop, m)
        p_out((n_chunks - 1) % 2, n_chunks - 1).wait()
        ctx_out.wait()
        acc = acc_sc[...]
        l = jnp.where(h < nh // N_KV, acc[half:half + 1], acc[0:1])
        ot_ref[...] = (acc / l).astype(BF16)
        lse_ref[...] = m + jnp.log(l)

    grp = nh // N_KV
    return _call(
        body, name="attn_glob_fwd", grid=(nh, s // tq),
        in_specs=[pl.BlockSpec((None, LANES, tq), lambda h, i: (h, 0, i)), _full((s, LANES)),
                  pl.BlockSpec((None, LANES, s), lambda h, i: (h // grp, 0, 0)), _full((nc, LANES)),
                  pl.BlockSpec((None, LANES, nc), lambda h, i: (h // grp, 0, 0))],
        out_specs=[pl.BlockSpec((None, LANES, tq), lambda h, i: (h, 0, i)),
                   pl.BlockSpec((None, 1, tq), lambda h, i: (h, 0, i)),
                   pl.BlockSpec((None, n_chunks + 1, tq), lambda h, i: (h, 0, i)), ANY],
        out_shape=[_sds((nh, LANES, s), BF16), _sds((nh, 1, s)), _sds((nh, n_chunks + 1, s)),
                   _sds((nh, s // tq, s + nc + tk, tq), BF16)],
        scratch=[pltpu.VMEM((LANES, tq), F32), pltpu.VMEM((tk, tq), F32), pltpu.VMEM((2, tk, tq), BF16),
                 pltpu.VMEM((nc, tq), BF16), pltpu.SemaphoreType.DMA((2,)), pltpu.SemaphoreType.DMA],
        sem=("parallel", "parallel"),
    )(qt, k, vt, kc, vct)


P_AHEAD = 3


def _attn_glob_bwd(qt, dot, ot, lse, mrun, p, kt, v, kct, vc, *, tq, tk):
    nh, _, s = qt.shape
    nc = vc.shape[0]
    n_q = s // tq
    n_chunks = s // tk
    n_run = mrun.shape[1] - 1
    per_run = n_chunks // n_run

    def body(qt_ref, dot_ref, ot_ref, lse_ref, mrun_ref, p_hbm, kt_ref, v_ref, kct_ref, vc_ref,
             dqt_ref, dkt_ref, dvt_ref, dkct_ref, dvct_ref, acc_sc, dp_sc, dkt_sc, dvt_sc, p_sc, pc_sc, sems, semc):
        h, i = pl.program_id(0), pl.program_id(1)

        @pl.when(jnp.logical_and(h == 0, i == 0))
        def _():
            dkct_ref[...] = jnp.zeros(dkct_ref.shape, F32)
            dvct_ref[...] = jnp.zeros(dvct_ref.shape, F32)
            dkt_sc[...] = jnp.zeros(dkt_sc.shape, F32)
            dvt_sc[...] = jnp.zeros(dvt_sc.shape, F32)


        def p_in(slot, c):
            return pltpu.make_async_copy(p_hbm.at[h, i, pl.ds(pl.multiple_of(c * tk, tk), tk), :], p_sc.at[slot],
                                         sems.at[slot])

        ctx_in = pltpu.make_async_copy(p_hbm.at[h, i, pl.ds(s, nc), :], pc_sc, semc)
        ctx_in.start()
        for c in range(P_AHEAD):
            p_in(c, min(c, n_chunks - 1)).start()
        qtv, dotv, lse = qt_ref[...], dot_ref[...], lse_ref[...]
        delta = jnp.sum(dotv.astype(F32) * ot_ref[...].astype(F32), axis=0, keepdims=True)

        def grads(pt_stored, m_row, dpt):
            pt = pt_stored.astype(F32) * jnp.exp(m_row - lse)
            return pt.astype(BF16), (pt * (dpt - delta)).astype(BF16)

        dp_sc[...] = _dot(v_ref[pl.ds(0, tk), :], dotv)
        ctx_in.wait()
        pb, dsb = grads(pc_sc[...], mrun_ref[pl.ds(n_run, 1), :], _dot(vc_ref[...], dotv))
        acc_sc[...] = _dot(kct_ref[...], dsb)
        dkct_ref[...] += _dot_nt(qtv, dsb)
        dvct_ref[...] += _dot_nt(dotv, pb)

        def loop(c, carry):
            slot = c % (P_AHEAD + 1)
            off = pl.multiple_of(c * tk, tk)
            nxt = pl.multiple_of(jnp.minimum(c + 1, n_chunks - 1) * tk, tk)
            p_in(slot, c).wait()
            p_in((c + P_AHEAD) % (P_AHEAD + 1), jnp.minimum(c + P_AHEAD, n_chunks - 1)).start()
            dpt = dp_sc[...]
            dp_next = _dot(v_ref[pl.ds(nxt, tk), :], dotv)
            pb, dsb = grads(p_sc[slot], mrun_ref[pl.ds(c // per_run, 1), :], dpt)
            acc_sc[...] += _dot(kt_ref[:, pl.ds(off, tk)], dsb)
            dkt_sc[:, pl.ds(off, tk)] += _dot_nt(qtv, dsb)
            dvt_sc[:, pl.ds(off, tk)] += _dot_nt(dotv, pb)
            dp_sc[...] = dp_next
            return carry

        lax.fori_loop(0, n_chunks, loop, 0)
        for c in range(n_chunks, n_chunks + P_AHEAD):
            p_in(c % (P_AHEAD + 1), n_chunks - 1).wait()
        dqt_ref[...] = acc_sc[...]

        @pl.when(jnp.logical_and(h == nh - 1, i == n_q - 1))
        def _():
            pltpu.sync_copy(dkt_sc, dkt_ref)
            pltpu.sync_copy(dvt_sc, dvt_ref)

    qs = pl.BlockSpec((None, LANES, tq), lambda h, i: (h, 0, i))
    rs = pl.BlockSpec((None, 1, tq), lambda h, i: (h, 0, i))
    return _call(
        body, name="attn_glob_bwd", grid=(nh, n_q),
        in_specs=[qs, qs, qs, rs, pl.BlockSpec((None, n_run + 1, tq), lambda h, i: (h, 0, i)), ANY,
                  _full((LANES, s)), _full((s, LANES)), _full((LANES, nc)), _full((nc, LANES))],
        out_specs=[qs, ANY, ANY, _full((LANES, nc)), _full((LANES, nc))],
        out_shape=[_sds((nh, LANES, s)), _sds((LANES, s)), _sds((LANES, s)), _sds((LANES, nc)), _sds((LANES, nc))],
        scratch=[pltpu.VMEM((LANES, tq), F32), pltpu.VMEM((tk, tq), F32), pltpu.VMEM((LANES, s), F32),
                 pltpu.VMEM((LANES, s), F32), pltpu.VMEM((P_AHEAD + 1, tk, tq), BF16), pltpu.VMEM((nc, tq), BF16),
                 pltpu.SemaphoreType.DMA((P_AHEAD + 1,)), pltpu.SemaphoreType.DMA],
        sem=("arbitrary", "arbitrary"),
    )(qt, dot, ot, lse, mrun, p, kt, v, kct, vc)


WIN_SPAN = 2 * WINDOW


def _band(rows0, cols0, shape):
    r = rows0 + lax.broadcasted_iota(jnp.int32, shape, 0)
    c = cols0 + lax.broadcasted_iota(jnp.int32, shape, 1)
    return jnp.abs(r - c) <= WINDOW


def _win_start(blk, t, s):
    return pl.multiple_of(jnp.clip(blk * t - WINDOW, 0, s - t - WIN_SPAN), WINDOW)


def _attn_win_fwd(qt, k, vt, kc, vct, sink, *, tq):
    nh, _, s = qt.shape
    nc = kc.shape[0]
    tw = tq + WIN_SPAN
    half = LANES // 2
    grp = nh // N_KV

    def body(sink_ref, qt_ref, k_ref, vt_ref, kc_ref, vct_ref, ot_ref, lse_ref):
        h, i = pl.program_id(0), pl.program_id(1)
        k0 = _win_start(i, tq, s)
        qtv = qt_ref[...]
        st = jnp.where(_band(k0, i * tq, (tw, tq)), _dot(k_ref[pl.ds(k0, tw), :], qtv), NEG)
        stc = _dot(kc_ref[...], qtv)
        snk = sink_ref[h]
        m = jnp.maximum(jnp.maximum(jnp.max(st, axis=0, keepdims=True), jnp.max(stc, axis=0, keepdims=True)), snk)
        acc = (_dot(vt_ref[:, pl.ds(k0, tw)], jnp.exp(st - m).astype(BF16))
               + _dot(vct_ref[...], jnp.exp(stc - m).astype(BF16)))
        l = jnp.where(h < grp, acc[half:half + 1], acc[0:1]) + jnp.exp(snk - m)
        ot_ref[...] = (acc / l).astype(BF16)
        lse_ref[...] = m + jnp.log(l)

    return _call(
        body, name="attn_win_fwd", grid=(nh, s // tq),
        in_specs=[pl.BlockSpec(memory_space=pltpu.SMEM),
                  pl.BlockSpec((None, LANES, tq), lambda h, i: (h, 0, i)), _full((s, LANES)),
                  pl.BlockSpec((None, LANES, s), lambda h, i: (h // grp, 0, 0)), _full((nc, LANES)),
                  pl.BlockSpec((None, LANES, nc), lambda h, i: (h // grp, 0, 0))],
        out_specs=[pl.BlockSpec((None, LANES, tq), lambda h, i: (h, 0, i)),
                   pl.BlockSpec((None, 1, tq), lambda h, i: (h, 0, i))],
        out_shape=[_sds((nh, LANES, s), BF16), _sds((nh, 1, s))],
        sem=("parallel", "parallel"),
    )(sink, qt, k, vt, kc, vct)


def _attn_win_bwd(qt, dot, ot, lse, k, kt, v, kc, kct, vc, sink, *, tq):
    nh, _, s = qt.shape
    nc = kc.shape[0]
    tw = tq + WIN_SPAN
    nq = s // tq

    def body(sink_ref, qt_ref, dot_ref, ot_ref, lse_ref, k_ref, kt_ref, v_ref, kc_ref, kct_ref, vc_ref,
             dqt_ref, dkt_ref, dvt_ref, dkct_ref, dvct_ref, dsk_ref, dkt_sc, dvt_sc):
        h, i = pl.program_id(0), pl.program_id(1)

        @pl.when(jnp.logical_and(h == 0, i == 0))
        def _():
            dkct_ref[...] = jnp.zeros(dkct_ref.shape, F32)
            dvct_ref[...] = jnp.zeros(dvct_ref.shape, F32)
            dkt_sc[...] = jnp.zeros(dkt_sc.shape, F32)
            dvt_sc[...] = jnp.zeros(dvt_sc.shape, F32)

        k0 = _win_start(i, tq, s)
        span = pl.ds(k0, tw)
        qtv, dotv, lse = qt_ref[...], dot_ref[...], lse_ref[...]
        delta = jnp.sum(dotv.astype(F32) * ot_ref[...].astype(F32), axis=0, keepdims=True)
        pt = jnp.where(_band(k0, i * tq, (tw, tq)), jnp.exp(_dot(k_ref[span, :], qtv) - lse), 0.0)
        dsb = (pt * (_dot(v_ref[span, :], dotv) - delta)).astype(BF16)
        pct = jnp.exp(_dot(kc_ref[...], qtv) - lse)
        dscb = (pct * (_dot(vc_ref[...], dotv) - delta)).astype(BF16)
        dqt_ref[...] = _dot(kt_ref[:, span], dsb) + _dot(kct_ref[...], dscb)
        dkt_sc[:, span] += _dot_nt(qtv, dsb)
        dvt_sc[:, span] += _dot_nt(dotv, pt.astype(BF16))
        dkct_ref[...] += _dot_nt(qtv, dscb)
        dvct_ref[...] += _dot_nt(dotv, pct.astype(BF16))
        dsk = -jnp.sum(jnp.exp(sink_ref[h] - lse) * delta)
        dsk_ref[...] = jnp.full(dsk_ref.shape, dsk, F32)

        @pl.when(jnp.logical_and(h == nh - 1, i == nq - 1))
        def _():
            pltpu.sync_copy(dkt_sc, dkt_ref)
            pltpu.sync_copy(dvt_sc, dvt_ref)

    qs = pl.BlockSpec((None, LANES, tq), lambda h, i: (h, 0, i))
    rs = pl.BlockSpec((None, 1, tq), lambda h, i: (h, 0, i))
    return _call(
        body, name="attn_win_bwd", grid=(nh, nq),
        in_specs=[pl.BlockSpec(memory_space=pltpu.SMEM), qs, qs, qs, rs, _full((s, LANES)), _full((LANES, s)),
                  _full((s, LANES)), _full((nc, LANES)), _full((LANES, nc)), _full((nc, LANES))],
        out_specs=[qs, ANY, ANY, _full((LANES, nc)), _full((LANES, nc)),
                   pl.BlockSpec((None, None, 8, LANES), lambda h, i: (h, i, 0, 0))],
        out_shape=[_sds((nh, LANES, s)), _sds((LANES, s)), _sds((LANES, s)), _sds((LANES, nc)), _sds((LANES, nc)),
                   _sds((nh, nq, 8, LANES))],
        scratch=[pltpu.VMEM((LANES, s), F32), pltpu.VMEM((LANES, s), F32)],
        sem=("arbitrary", "arbitrary"),
    )(sink, qt, dot, ot, lse, k, kt, v, kc, kct, vc)


def _ln_fwd(z, g, b):
    mu = jnp.mean(z, axis=-1, keepdims=True)
    zc = z - mu
    r = lax.rsqrt(jnp.mean(zc * zc, axis=-1, keepdims=True) + LN_EPS)
    return zc * r * g + b, mu, r


def _ln_bwd(dy, xhat, r, g):
    dxh = dy * g
    return r * (dxh - jnp.mean(dxh, axis=-1, keepdims=True) - xhat * jnp.mean(dxh * xhat, axis=-1, keepdims=True))


def _heads_matmul(o_ref, w_ref):
    acc = _dot(o_ref[0], w_ref[0])
    for h in range(1, N_HEADS):
        acc += _dot(o_ref[h], w_ref[h])
    return acc


def _gate_specs(tm):
    return [pl.BlockSpec((tm, 512), functools.partial(lambda i, b: (i, b), b=OFF_GA // 512 + b)) for b in range(4)]


def _merge_fwd(oa, ob, proj, x, gate1, wba, wbb, w_out, ln_g, ln_b, *, tm):
    s = x.shape[0]

    def body(oa_ref, ob_ref, g0, g1, g2, g3, x_ref, gt_ref, wba_ref, wbb_ref, wo_ref, lg_ref, lb_ref,
             x1_ref, y_ref, mu_ref, r_ref):
        ga = _sigmoid(jnp.concatenate([g0[...], g1[...]], axis=1))
        gb = _sigmoid(jnp.concatenate([g2[...], g3[...]], axis=1))
        merged = ga * _heads_matmul(oa_ref, wba_ref) + gb * _heads_matmul(ob_ref, wbb_ref)
        y = _dot(merged.astype(BF16), wo_ref[...])
        x1, mu, r = _ln_fwd(ALPHA * x_ref[...] + gt_ref[...] * y, lg_ref[...], lb_ref[...])
        x1_ref[...] = x1
        y_ref[...] = y
        mu_ref[...] = mu
        r_ref[...] = r

    hs = pl.BlockSpec((N_HEADS, tm, LANES), lambda i: (0, i, 0))
    row = pl.BlockSpec((tm, D_MODEL), lambda i: (i, 0))
    col = pl.BlockSpec((tm, 1), lambda i: (i, 0))
    vec = _full((1, D_MODEL))
    wh = _full((N_HEADS, LANES, D_MODEL))
    return _call(
        body, name="merge_fwd", grid=(s // tm,),
        in_specs=[hs, hs, *_gate_specs(tm), row, vec, wh, wh, _full((D_MODEL, D_MODEL)), vec, vec],
        out_specs=[row, row, col, col],
        out_shape=[_sds((s, D_MODEL)), _sds((s, D_MODEL)), _sds((s, 1)), _sds((s, 1))],
        sem=("parallel",),
    )(oa, ob, proj, proj, proj, proj, x, gate1, wba, wbb, w_out, ln_g, ln_b)


def _merge_bwd(dy, oa, ob, oat, obt, proj, wba, wbb, w_out, *, tm):
    s = dy.shape[0]

    def body(dy_ref, oa_ref, ob_ref, oat_ref, obt_ref, g0, g1, g2, g3, wba_ref, wbb_ref, wo_ref,
             dgl_ref, doat_ref, dobt_ref, mg_ref, dwa_ref, dwb_ref):
        @pl.when(pl.program_id(0) == 0)
        def _():
            dwa_ref[...] = jnp.zeros(dwa_ref.shape, F32)
            dwb_ref[...] = jnp.zeros(dwb_ref.shape, F32)

        dm = _dot_nt(dy_ref[...], wo_ref[...])
        ga = _sigmoid(jnp.concatenate([g0[...], g1[...]], axis=1))
        gb = _sigmoid(jnp.concatenate([g2[...], g3[...]], axis=1))
        pa, pb = _heads_matmul(oa_ref, wba_ref), _heads_matmul(ob_ref, wbb_ref)
        mg_ref[...] = (ga * pa + gb * pb).astype(BF16)
        dgl_ref[:, :D_MODEL] = (dm * pa * ga * (1.0 - ga)).astype(BF16)
        dgl_ref[:, D_MODEL:] = (dm * pb * gb * (1.0 - gb)).astype(BF16)
        dpa, dpb = (dm * ga).astype(BF16), (dm * gb).astype(BF16)
        for h in range(N_HEADS):
            doat_ref[h] = _dot_nt(wba_ref[h], dpa).astype(BF16)
            dobt_ref[h] = _dot_nt(wbb_ref[h], dpb).astype(BF16)
            dwa_ref[h] += _dot(oat_ref[h], dpa)
            dwb_ref[h] += _dot(obt_ref[h], dpb)

    hs = pl.BlockSpec((N_HEADS, tm, LANES), lambda i: (0, i, 0))
    hts = pl.BlockSpec((N_HEADS, LANES, tm), lambda i: (0, 0, i))
    row = pl.BlockSpec((tm, D_MODEL), lambda i: (i, 0))
    wh = _full((N_HEADS, LANES, D_MODEL))
    return _call(
        body, name="merge_bwd", grid=(s // tm,),
        in_specs=[row, hs, hs, hts, hts, *_gate_specs(tm), wh, wh, _full((D_MODEL, D_MODEL))],
        out_specs=[pl.BlockSpec((tm, 2 * D_MODEL), lambda i: (i, 0)), hts, hts, row, wh, wh],
        out_shape=[_sds((s, 2 * D_MODEL), BF16), _sds((N_HEADS, LANES, s), BF16), _sds((N_HEADS, LANES, s), BF16),
                   _sds((s, D_MODEL), BF16), _sds((N_HEADS, LANES, D_MODEL)), _sds((N_HEADS, LANES, D_MODEL))],
        sem=("arbitrary",),
    )(dy, oa, ob, oat, obt, proj, proj, proj, proj, wba, wbb, w_out)


FF_TC = 256


def _shift_rows(t, prev_row, next_row):
    n = t.shape[0]
    r = lax.broadcasted_iota(jnp.int32, t.shape, 0)
    up = jnp.where(r == 0, prev_row, pltpu.roll(t, 1, 0))
    dn = jnp.where(r == n - 1, next_row, pltpu.roll(t, n - 1, 0))
    return up, dn


def _halo_specs(tm, s, tc):
    nb8 = s // 8
    main = pl.BlockSpec((2, tm, tc), lambda j, i: (0, i, j))
    prev = pl.BlockSpec((2, 8, tc), lambda j, i: (0, jnp.maximum(i * (tm // 8) - 1, 0), j))
    nxt = pl.BlockSpec((2, 8, tc), lambda j, i: (0, jnp.minimum((i + 1) * (tm // 8), nb8 - 1), j))
    return main, prev, nxt


def _halo_rows(prev_ref, next_ref, half, i, n_i):
    prev_row = jnp.where(i == 0, 0.0, prev_ref[half, 7:8, :].astype(F32))
    next_row = jnp.where(i == n_i - 1, 0.0, next_ref[half, 0:1, :].astype(F32))
    return prev_row, next_row


def _conv(t, prev_row, next_row, w, b):
    up, dn = _shift_rows(t, prev_row, next_row)
    return w[0:1, :] * up + w[1:2, :] * t + w[2:3, :] * dn + b


def _ffn_act_fwd(u, cw, cb, *, tm):
    _, s, ff = u.shape
    n_i = s // tm

    def body(u_ref, up_ref, un_ref, cw_ref, cb_ref, a_ref):
        i = pl.program_id(1)
        gc = _conv(u_ref[0], *_halo_rows(up_ref, un_ref, 0, i, n_i), cw_ref[0], cb_ref[0])
        vc = _conv(u_ref[1], *_halo_rows(up_ref, un_ref, 1, i, n_i), cw_ref[1], cb_ref[1])
        a_ref[...] = (gc * _sigmoid(gc) * vc).astype(BF16)

    main, prev, nxt = _halo_specs(tm, s, FF_TC)
    return _call(
        body, name="ffn_act_fwd", grid=(ff // FF_TC, n_i),
        in_specs=[main, prev, nxt, pl.BlockSpec((2, 3, FF_TC), lambda j, i: (0, 0, j)),
                  pl.BlockSpec((2, 1, FF_TC), lambda j, i: (0, 0, j))],
        out_specs=pl.BlockSpec((tm, FF_TC), lambda j, i: (i, j)),
        out_shape=_sds((s, ff), BF16), sem=("parallel", "parallel"),
    )(u, u, u, cw, cb)


def _ffn_act_bwd(dy2, w_down, u, cw, cb, *, tm):
    _, s, ff = u.shape
    n_i = s // tm

    def body(dy_ref, wd_ref, u_ref, up_ref, un_ref, cw_ref, cb_ref, dc_ref, dcw_ref, dcb_ref):
        i = pl.program_id(1)

        @pl.when(i == 0)
        def _():
            dcw_ref[...] = jnp.zeros(dcw_ref.shape, F32)
            dcb_ref[...] = jnp.zeros(dcb_ref.shape, F32)

        da = _dot_nt(dy_ref[...], wd_ref[...])
        ug, uv = u_ref[0], u_ref[1]
        ugp, ugn = _shift_rows(ug, *_halo_rows(up_ref, un_ref, 0, i, n_i))
        uvp, uvn = _shift_rows(uv, *_halo_rows(up_ref, un_ref, 1, i, n_i))
        wg, wv = cw_ref[0], cw_ref[1]
        gc = wg[0:1, :] * ugp + wg[1:2, :] * ug + wg[2:3, :] * ugn + cb_ref[0]
        vc = wv[0:1, :] * uvp + wv[1:2, :] * uv + wv[2:3, :] * uvn + cb_ref[1]
        sg = _sigmoid(gc)
        dg = da * vc * sg * (1.0 + gc * (1.0 - sg))
        dv = da * gc * sg
        dc_ref[0] = dg
        dc_ref[1] = dv
        for half, (d, taps) in enumerate(((dg, (ugp, ug, ugn)), (dv, (uvp, uv, uvn)))):
            for tap in range(3):
                dcw_ref[half, tap:tap + 1, :] += jnp.sum(d * taps[tap], axis=0, keepdims=True)
            dcb_ref[half] += jnp.sum(d, axis=0, keepdims=True)

    main, prev, nxt = _halo_specs(tm, s, FF_TC)
    return _call(
        body, name="ffn_act_bwd", grid=(ff // FF_TC, n_i),
        in_specs=[pl.BlockSpec((tm, D_MODEL), lambda j, i: (i, 0)), pl.BlockSpec((FF_TC, D_MODEL), lambda j, i: (j, 0)),
                  main, prev, nxt, pl.BlockSpec((2, 3, FF_TC), lambda j, i: (0, 0, j)),
                  pl.BlockSpec((2, 1, FF_TC), lambda j, i: (0, 0, j))],
        out_specs=[main, pl.BlockSpec((2, 3, FF_TC), lambda j, i: (0, 0, j)),
                   pl.BlockSpec((2, 1, FF_TC), lambda j, i: (0, 0, j))],
        out_shape=[_sds((2, s, ff)), _sds((2, 3, ff)), _sds((2, 1, ff))],
        sem=("parallel", "arbitrary"),
    )(dy2, w_down, u, u, u, cw, cb)


def _conv_bwd_input(dc, cw, *, tm):
    _, s, ff = dc.shape
    n_i = s // tm

    def body(d_ref, dp_ref, dn_ref, cw_ref, du_ref):
        i = pl.program_id(1)
        for half in range(2):
            up, dn = _shift_rows(d_ref[half], *_halo_rows(dp_ref, dn_ref, half, i, n_i))
            w = cw_ref[half]
            du_ref[half] = (w[0:1, :] * dn + w[1:2, :] * d_ref[half] + w[2:3, :] * up).astype(BF16)

    main, prev, nxt = _halo_specs(tm, s, FF_TC)
    return _call(
        body, name="conv_bwd_input", grid=(ff // FF_TC, n_i),
        in_specs=[main, prev, nxt, pl.BlockSpec((2, 3, FF_TC), lambda j, i: (0, 0, j))],
        out_specs=main, out_shape=_sds((2, s, ff), BF16), sem=("parallel", "parallel"),
    )(dc, dc, dc, cw)


def _ffn_down_loss(a, w_down, x1, target, gate2, ln_g, ln_b, *, tm):
    s, ff = a.shape
    n_i = s // tm

    def body(a_ref, wd_ref, x1_ref, tg_ref, gt_ref, lg_ref, lb_ref, ls_ref, dy_ref, dx_ref, dg_ref, db_ref, dgt_ref):
        @pl.when(pl.program_id(0) == 0)
        def _():
            dg_ref[...] = jnp.zeros(dg_ref.shape, F32)
            db_ref[...] = jnp.zeros(db_ref.shape, F32)
            dgt_ref[...] = jnp.zeros(dgt_ref.shape, F32)

        y2 = _dot(a_ref[...], wd_ref[...])
        z = ALPHA * x1_ref[...] + gt_ref[...] * y2
        mu = jnp.mean(z, axis=-1, keepdims=True)
        zc = z - mu
        r = lax.rsqrt(jnp.mean(zc * zc, axis=-1, keepdims=True) + LN_EPS)
        xhat = zc * r
        diff = xhat * lg_ref[...] + lb_ref[...] - tg_ref[...]
        ls_ref[...] = jnp.full(ls_ref.shape, 0.5 / D_MODEL * jnp.sum(diff * diff), F32)
        dx2 = diff * (1.0 / D_MODEL)
        dg_ref[...] += jnp.sum(dx2 * xhat, axis=0, keepdims=True)
        db_ref[...] += jnp.sum(dx2, axis=0, keepdims=True)
        dz = _ln_bwd(dx2, xhat, r, lg_ref[...])
        dgt_ref[...] += jnp.sum(dz * y2, axis=0, keepdims=True)
        dy_ref[...] = (gt_ref[...] * dz).astype(BF16)
        dx_ref[...] = ALPHA * dz

    row = pl.BlockSpec((tm, D_MODEL), lambda i: (i, 0))
    vec = _full((1, D_MODEL))
    return _call(
        body, name="ffn_down_loss", grid=(n_i,),
        in_specs=[pl.BlockSpec((tm, ff), lambda i: (i, 0)), _full((ff, D_MODEL)), row, row, vec, vec, vec],
        out_specs=[pl.BlockSpec((None, 8, LANES), lambda i: (i, 0, 0)), row, row, vec, vec, vec],
        out_shape=[_sds((n_i, 8, LANES)), _sds((s, D_MODEL), BF16), _sds((s, D_MODEL)),
                   _sds((1, D_MODEL)), _sds((1, D_MODEL)), _sds((1, D_MODEL))],
        sem=("arbitrary",),
    )(a, w_down, x1, target, gate2, ln_g, ln_b)


def _ffn_up_bwd(du, wup4, dx1a, x1, scale2, x, y, mu1, r1, gate1, ln_g, *, tm):
    s = x.shape[0]
    nb, _, ns = wup4.shape

    def body(du_ref, w_ref, dxa_ref, x1_ref, sc_ref, x_ref, y_ref, mu_ref, r_ref, gt_ref, lg_ref,
             dxo_ref, dy_ref, dsc_ref, dsh_ref, dg_ref, db_ref, dgt_ref, acc):
        i, k = pl.program_id(0), pl.program_id(1)

        @pl.when(jnp.logical_and(i == 0, k == 0))
        def _():
            for ref in (dsc_ref, dsh_ref, dg_ref, db_ref, dgt_ref):
                ref[...] = jnp.zeros(ref.shape, F32)

        @pl.when(k == 0)
        def _():
            acc[...] = jnp.zeros(acc.shape, F32)

        acc[...] += _dot_nt(du_ref[...], w_ref[...])

        @pl.when(k == nb - 1)
        def _():
            dh = acc[...]
            x1 = x1_ref[...]
            dsc_ref[...] += jnp.sum(dh * x1, axis=0, keepdims=True)
            dsh_ref[...] += jnp.sum(dh, axis=0, keepdims=True)
            dx1 = dxa_ref[...] + dh * (1.0 + sc_ref[...])
            yv = y_ref[...]
            xhat = (ALPHA * x_ref[...] + gt_ref[...] * yv - mu_ref[...]) * r_ref[...]
            dg_ref[...] += jnp.sum(dx1 * xhat, axis=0, keepdims=True)
            db_ref[...] += jnp.sum(dx1, axis=0, keepdims=True)
            dz = _ln_bwd(dx1, xhat, r_ref[...], lg_ref[...])
            dgt_ref[...] += jnp.sum(dz * yv, axis=0, keepdims=True)
            dy_ref[...] = (gt_ref[...] * dz).astype(BF16)
            dxo_ref[...] = ALPHA * dz

    row = pl.BlockSpec((tm, D_MODEL), lambda i, k: (i, 0))
    col = pl.BlockSpec((tm, 1), lambda i, k: (i, 0))
    vec = _full((1, D_MODEL))
    return _call(
        body, name="ffn_up_bwd", grid=(s // tm, nb),
        in_specs=[pl.BlockSpec((None, tm, ns), lambda i, k: (k // 2, i, k % 2)),
                  pl.BlockSpec((None, D_MODEL, ns), lambda i, k: (k, 0, 0)),
                  row, row, vec, row, row, col, col, vec, vec],
        out_specs=[row, row, vec, vec, vec, vec, vec],
        out_shape=[_sds((s, D_MODEL)), _sds((s, D_MODEL), BF16)] + [_sds((1, D_MODEL))] * 5,
        scratch=[pltpu.VMEM((tm, D_MODEL), F32)],
        sem=("arbitrary", "arbitrary"),
    )(du, wup4, dx1a, x1, scale2, x, y, mu1, r1, gate1, ln_g)


def _mm_nt4_mod_bwd(dp, w4, dxa, x, scale, *, tm, name):
    m = x.shape[0]
    nb, kdim, ns = w4.shape

    def body(dp_ref, w_ref, dxa_ref, x_ref, sc_ref, dx_ref, dsc_ref, dsh_ref, acc):
        i, k = pl.program_id(0), pl.program_id(1)

        @pl.when(jnp.logical_and(i == 0, k == 0))
        def _():
            dsc_ref[...] = jnp.zeros(dsc_ref.shape, F32)
            dsh_ref[...] = jnp.zeros(dsh_ref.shape, F32)

        @pl.when(k == 0)
        def _():
            acc[...] = jnp.zeros(acc.shape, F32)

        acc[...] += _dot_nt(dp_ref[...], w_ref[...])

        @pl.when(k == nb - 1)
        def _():
            dh = acc[...]
            dsc_ref[...] += jnp.sum(dh * x_ref[...], axis=0, keepdims=True)
            dsh_ref[...] += jnp.sum(dh, axis=0, keepdims=True)
            dx_ref[...] = dxa_ref[...] + dh * (1.0 + sc_ref[...])

    row = pl.BlockSpec((tm, kdim), lambda i, k: (i, 0))
    vec = _full((1, kdim))
    return _call(
        body, name=name, grid=(m // tm, nb),
        in_specs=[pl.BlockSpec((tm, ns), lambda i, k: (i, k)), pl.BlockSpec((None, kdim, ns), lambda i, k: (k, 0, 0)),
                  row, row, vec],
        out_specs=[row, vec, vec],
        out_shape=[_sds((m, kdim)), _sds((1, kdim)), _sds((1, kdim))],
        scratch=[pltpu.VMEM((tm, kdim), F32)],
        sem=("arbitrary", "arbitrary"),
    )(dp, w4, dxa, x, scale)


def _pad_heads_w(w):
    w8 = w.reshape(N_HEADS, HEAD_DIM, w.shape[-1])
    z = jnp.zeros_like(w8)
    first = (jnp.arange(N_HEADS) < N_HEADS // N_KV)[:, None, None]
    return jnp.where(first, jnp.concatenate([w8, z], axis=1), jnp.concatenate([z, w8], axis=1))


def _unpad_heads_w(g):
    first = (jnp.arange(N_HEADS) < N_HEADS // N_KV)[:, None, None]
    return jnp.where(first, g[:, :HEAD_DIM], g[:, HEAD_DIM:]).reshape(N_HEADS * HEAD_DIM, g.shape[-1])


def _ones_beside(vt):
    half = vt.shape[0] // 2
    ones = jnp.ones((half, vt.shape[1]), vt.dtype)
    return jnp.stack([jnp.concatenate([vt[:half], ones], axis=0), jnp.concatenate([ones, vt[half:]], axis=0)])


def _rep8(a):
    return jnp.broadcast_to(a.reshape(1, -1), (8, a.size))


def _first_row(a):
    r8 = _rep8(a)
    return jnp.where(lax.broadcasted_iota(jnp.int32, r8.shape, 0) == 0, r8, 0.0)


def _to_blocks4(w):
    k, n = w.shape
    return w.reshape(k, N_CHIPS, n // N_CHIPS).transpose(1, 0, 2)


def _local_step(x, c, ctx, c_ctx, wmod4, b_mod, win4, b_in, sink, qn, kn, wba, wbb, w_out, ln1_g, ln1_b,
                wup4, cw, cb, w_down, ln2_g, ln2_b, target):
    s, nc = x.shape[0], ctx.shape[0]
    tm = min(512, s)
    tm2 = min(256, s)
    zvec = jnp.zeros((1, D_MODEL), F32)

    cc = jnp.concatenate([_rep8(c), _rep8(c_ctx)], axis=0)
    mods = _mm_nn4(cc, zvec, zvec, wmod4, b_mod, mode="silu", split_out=False, out_dtype=F32, tm=16, name="mod_vectors")
    shift1, scale1, gate1, shift2, scale2, gate2 = [mods[0:1, i * D_MODEL:(i + 1) * D_MODEL] for i in range(6)]
    shift_c, scale_c = mods[8:9, :D_MODEL], mods[8:9, D_MODEL:2 * D_MODEL]

    cos, sin = _rope_tables(s)
    cos_c, sin_c = jnp.ones((nc, LANES), F32), jnp.zeros((nc, LANES), F32)
    qg, kg = jnp.tile(qn, (1, 2)), jnp.tile(kn, (1, 2))

    proj_c = _mm_nn4(ctx, shift_c, scale_c, win4, b_in, mode="modulate", split_out=False, out_dtype=F32, tm=nc,
                     name="in_proj_ctx")
    _, kac, vac, _, kbc, vbc = _prep(proj_c, cos_c, sin_c, qg, kg, tm=nc, name="prep_ctx")
    proj = _mm_nn4(x, shift1, scale1, win4, b_in, mode="modulate", split_out=False, out_dtype=F32, tm=tm, name="in_proj")
    qa, ka, va, qb, kb, vb = _prep(proj, cos, sin, qg, kg, tm=tm, name="prep")
    qat = jnp.swapaxes(qa, 1, 2)
    oat, lse_a = _attn_win_fwd(qat, ka, _ones_beside(va.T), kac, _ones_beside(vac.T), sink, tq=tm)
    oa = jnp.swapaxes(oat, 1, 2)
    qbt = jnp.swapaxes(qb, 1, 2)
    obt, lse_b, mrun_b, pbt = _attn_glob_fwd(qbt, kb, _ones_beside(vb.T), kbc, _ones_beside(vbc.T), tq=tm,
                                             tk=min(1024, s))
    ob = jnp.swapaxes(obt, 1, 2)
    wba_p, wbb_p = _pad_heads_w(wba), _pad_heads_w(wbb)
    x1, y, mu1, r1 = _merge_fwd(oa, ob, proj, x, gate1, wba_p, wbb_p, w_out, ln1_g, ln1_b, tm=tm)
    u = _mm_nn4(x1, shift2, scale2, wup4, jnp.zeros((1, 2 * D_FF), F32), mode="modulate", split_out=True,
                out_dtype=F32, tm=tm, name="ffn_up")
    cw2 = cw.reshape(3, 2, D_FF).transpose(1, 0, 2)
    cb2 = cb.reshape(2, 1, D_FF)
    a = _ffn_act_fwd(u, cw2, cb2, tm=tm)
    ls, dy2, dx1a, dln2_g, dln2_b, dgate2 = _ffn_down_loss(a, w_down, x1, target, gate2, ln2_g, ln2_b, tm=tm)
    loss = jnp.sum(ls[:, 0, 0])

    n_s = s // tm
    dw_down = _mm_tn(a, dy2, a_spec=pl.BlockSpec((tm, D_FF), lambda t: (t, 0)),
                     b_spec=pl.BlockSpec((tm, D_MODEL), lambda t: (t, 0)), grid=(n_s,),
                     out_shape=_sds((D_FF, D_MODEL)), out_spec=_full((D_FF, D_MODEL)), name="dw_down")
    dc, dcw2, dcb2 = _ffn_act_bwd(dy2, w_down, u, cw2, cb2, tm=tm)
    du = _conv_bwd_input(dc, cw2, tm=tm)
    dxz1, dy, dscale2, dshift2, dln1_g, dln1_b, dgate1 = _ffn_up_bwd(
        du, wup4, dx1a, x1, scale2, x, y, mu1, r1, gate1, ln1_g, tm=tm)
    ns_up = wup4.shape[-1]
    dw_up4 = _mm_tn(x1, du, a_spec=pl.BlockSpec((tm, D_MODEL), lambda k, t: (t, 0)),
                    b_spec=pl.BlockSpec((None, tm, ns_up), lambda k, t: (k // 2, t, k % 2)), grid=(N_CHIPS, n_s),
                    out_shape=_sds((N_CHIPS, D_MODEL, ns_up)),
                    out_spec=pl.BlockSpec((None, D_MODEL, ns_up), lambda k, t: (k, 0, 0)),
                    mod=(shift2, scale2), name="dw_up")

    dgl, doat, dobt, merged, dwba_p, dwbb_p = _merge_bwd(dy, oa, ob, oat, obt, proj, wba_p, wbb_p, w_out, tm=tm2)
    dwba, dwbb = _unpad_heads_w(dwba_p), _unpad_heads_w(dwbb_p)
    rowspec = pl.BlockSpec((tm, D_MODEL), lambda t: (t, 0))
    dw_out = _mm_tn(merged, dy, a_spec=rowspec, b_spec=rowspec, grid=(n_s,), out_shape=_sds((D_MODEL, D_MODEL)),
                    out_spec=_full((D_MODEL, D_MODEL)), name="dw_out")

    dqat, dkat, dvat, dkact, dvact, dsk = _attn_win_bwd(qat, doat, oat, lse_a, ka, ka.T, va, kac, kac.T, vac, sink, tq=tm)
    dqa, dka, dva, dkac, dvac = jnp.swapaxes(dqat, 1, 2), dkat.T, dvat.T, dkact.T, dvact.T
    dqbt, dkbt, dvbt, dkbct, dvbct = _attn_glob_bwd(qbt, dobt, obt, lse_b, mrun_b, pbt, kb.T, vb, kbc.T, vbc, tq=tm, tk=tm)
    dqb, dkb, dvb, dkbc, dvbc = jnp.swapaxes(dqbt, 1, 2), dkbt.T, dvbt.T, dkbct.T, dvbct.T
    dsink = jnp.sum(dsk[:, :, 0, 0], axis=1)

    dproj, dqg, dkg = _prep_bwd(dqa, dka, dva, dqb, dkb, dvb, proj, cos, sin, qg, kg, dgl, tm=tm, name="prep_bwd")
    grad_x, dscale1, dshift1 = _mm_nt4_mod_bwd(dproj, win4, dxz1, x, scale1, tm=tm, name="in_proj_bwd")
    ns_in = win4.shape[-1]
    win_spec = dict(b_spec=pl.BlockSpec((None, None, ns_in), lambda k, t: (0, 0, k)),
                    out_shape=_sds((N_CHIPS, D_MODEL, ns_in)),
                    out_spec=pl.BlockSpec((None, D_MODEL, ns_in), lambda k, t: (k, 0, 0)),
                    colsum_spec=pl.BlockSpec((8, ns_in), lambda k, t: (0, k)), colsum_shape=_sds((8, IN_COLS)))
    win_spec["b_spec"] = pl.BlockSpec((tm, ns_in), lambda k, t: (t, k))
    dw_in4, db_in = _mm_tn(x, dproj, a_spec=pl.BlockSpec((tm, D_MODEL), lambda k, t: (t, 0)), grid=(N_CHIPS, n_s),
                           mod=(shift1, scale1), name="dw_in", **win_spec)

    zq = jnp.zeros((N_HEADS, nc, LANES), F32)
    dproj_c, _, dkg_c = _prep_bwd(zq, dkac, dvac, zq, dkbc, dvbc, proj_c, cos_c, sin_c, qg, kg,
                                  jnp.zeros((nc, IN_COLS - OFF_GA), BF16), tm=nc, name="prep_bwd_ctx")
    _, dscale_c, dshift_c = _mm_nt4_mod_bwd(dproj_c, win4, jnp.zeros((nc, D_MODEL), F32), ctx, scale_c, tm=nc,
                                            name="in_proj_bwd_ctx")
    win_spec["b_spec"] = pl.BlockSpec((nc, ns_in), lambda k, t: (t, k))
    dw_in4, db_in_c = _mm_tn(ctx, dproj_c, a_spec=pl.BlockSpec((nc, D_MODEL), lambda k, t: (t, 0)), grid=(N_CHIPS, 1),
                             mod=(shift_c, scale_c), init=dw_in4, name="dw_in_ctx", **win_spec)

    dmod = jnp.concatenate([dshift1, dscale1, dgate1, dshift2, dscale2, dgate2], axis=1)
    dmodc = jnp.concatenate([dshift_c, dscale_c], axis=1)
    dmodc_pad = jnp.concatenate([dmodc, jnp.zeros((1, 4 * D_MODEL), F32)], axis=1)
    dmodc8 = _first_row(dmodc_pad).astype(BF16)
    z8 = jnp.zeros((8, D_MODEL), F32)
    dsilu_c, _, _ = _mm_nt4_mod_bwd(dmodc8, wmod4, z8, z8, zvec, tm=8, name="c_ctx_bwd")
    sg = _sigmoid(c_ctx)
    dc_ctx = dsilu_c[0:1] * sg * (1.0 + c_ctx * (1.0 - sg))

    dqn = jnp.sum(dqg.reshape(N_HEADS, HEAD_DIM), axis=0, keepdims=True)
    dkn = jnp.sum((dkg + dkg_c).reshape(N_KV, HEAD_DIM), axis=0, keepdims=True)
    grads = dict(
        w_in4=dw_in4, b_in=db_in[0:1] + db_in_c[0:1], sink=dsink, qn=dqn, kn=dkn, wba=dwba, wbb=dwbb, w_out=dw_out,
        ln1_g=dln1_g, ln1_b=dln1_b, w_up4=dw_up4, conv_w=dcw2.transpose(1, 0, 2).reshape(3, 2 * D_FF),
        conv_b=dcb2.reshape(1, 2 * D_FF), w_down=dw_down, ln2_g=dln2_g, ln2_b=dln2_b,
        c_ctx=dc_ctx, dmod=dmod, dmodc=dmodc)
    return loss, grad_x, grads


ANY = pl.BlockSpec(memory_space=pl.ANY)


def _mesh_pos():
    return lax.axis_index("x"), lax.axis_index("y"), lax.axis_index("c")


def _other_chips(x, y):
    return [(1 - x, y), (x, 1 - y), (1 - x, 1 - y)]


def _remote(src, dst, send, recv, dev):
    return pltpu.make_async_remote_copy(src_ref=src, dst_ref=dst, send_sem=send, recv_sem=recv, device_id=dev,
                                        device_id_type=MESH)


def _set_block(stack, block, k):
    return lax.dynamic_update_slice(stack, block[None], (k,) + (0,) * block.ndim)


def _gather_shards(arrs, small):
    na = len(arrs)
    halves = [a.shape[0] // 2 for a in arrs]

    def body(*refs):
        ins, small_ref = refs[:na], refs[na]
        outs, small_out = refs[na + 1:2 * na + 1], refs[2 * na + 1]
        send, recv = refs[2 * na + 2:]
        x, y, c = _mesh_pos()
        me = 2 * x + y
        chips = _other_chips(x, y)

        def half(a, cc):
            return pl.ds(cc * halves[a], halves[a])

        sends = []
        for j, chip in enumerate(chips):
            for a in range(na):
                sends.append(_remote(ins[a].at[half(a, c)], outs[a].at[me, half(a, c)], send.at[a, j], recv.at[a, j],
                                     (*chip, c)))
            sends.append(_remote(small_ref, small_out.at[me], send.at[na, j], recv.at[na, j], (*chip, c)))
        for cp in sends:
            cp.start()
        for j, chip in enumerate(chips):
            kj = 2 * chip[0] + chip[1]
            for a in range(na):
                landed = outs[a].at[kj, half(a, c)]
                _remote(landed, landed, send.at[a, j], recv.at[a, j], (*chip, c)).wait_recv()
                fwd = _remote(landed, landed, send.at[a, 3 + j], recv.at[a, 3 + j], (x, y, 1 - c))
                fwd.start()
                sends.append(fwd)
            _remote(small_ref, small_out.at[kj], send.at[na, j], recv.at[na, j], (*chip, c)).wait_recv()
        for j, chip in enumerate(chips):
            kj = 2 * chip[0] + chip[1]
            for a in range(na):
                other = outs[a].at[kj, half(a, 1 - c)]
                _remote(other, other, send.at[a, 3 + j], recv.at[a, 3 + j], (x, y, 1 - c)).wait_recv()
        for cp in sends:
            cp.wait_send()

    out_shape = [_sds((N_CHIPS,) + a.shape, a.dtype) for a in arrs] + [_sds((N_CHIPS,) + small.shape, small.dtype)]
    got = pl.pallas_call(
        body, name="gather_shards", in_specs=[ANY] * (na + 1), out_specs=[ANY] * (na + 1), out_shape=out_shape,
        scratch_shapes=[pltpu.SemaphoreType.DMA((na + 1, 6)), pltpu.SemaphoreType.DMA((na + 1, 6))],
    )(*arrs, small)
    xp, yp, _ = _mesh_pos()
    return [_set_block(g, a, 2 * xp + yp) for g, a in zip(got, list(arrs) + [small])]


def _allgather_rows(v):
    r, n = v.shape

    def body(v_ref, out_ref, send, recv, loc):
        x, y, c = _mesh_pos()
        me, sibling = (x, y, c), (x, y, 1 - c)
        chips = _other_chips(x, y)

        def rows(px, py, pc):
            return out_ref.at[4 * px + 2 * py + pc]

        def copy(k, block, to, src=None):
            return _remote(rows(*block) if src is None else src, rows(*block), send.at[k], recv.at[k], to)

        mine = pltpu.make_async_copy(v_ref, rows(*me), loc)
        mine.start()
        first = [copy(0, me, sibling, src=v_ref)] + [copy(1 + j, me, (*chip, c), src=v_ref) for j, chip in enumerate(chips)]
        for cp in first:
            cp.start()
        passed = [copy(4 + j, (*chip, c), sibling) for j, chip in enumerate(chips)]
        for j, chip in enumerate(chips):
            copy(1 + j, (*chip, c), me).wait_recv()
            passed[j].start()
        copy(0, sibling, me).wait_recv()
        for j, chip in enumerate(chips):
            copy(4 + j, (*chip, 1 - c), me).wait_recv()
        for cp in first + passed:
            cp.wait_send()
        mine.wait()

    return pl.pallas_call(
        body, name="allgather_rows", in_specs=[pl.BlockSpec(memory_space=pltpu.VMEM)],
        out_specs=pl.BlockSpec(memory_space=pltpu.VMEM), out_shape=_sds((N_DEV, r, n), v.dtype),
        scratch_shapes=[pltpu.SemaphoreType.DMA((7,)), pltpu.SemaphoreType.DMA((7,)), pltpu.SemaphoreType.DMA],
    )(v)


def _swap_other_half(g):
    nb, r, n = g.shape
    rh = r // 2

    def body(g_ref, out_ref, send, recv):
        x, y, c = _mesh_pos()
        cp = _remote(g_ref.at[:, pl.ds((1 - c) * rh, rh), :], out_ref, send, recv, (x, y, 1 - c))
        cp.start()
        cp.wait()

    return pl.pallas_call(
        body, name="swap_other_half", in_specs=[ANY], out_specs=ANY, out_shape=_sds((nb, rh, n), g.dtype),
        scratch_shapes=[pltpu.SemaphoreType.DMA, pltpu.SemaphoreType.DMA],
    )(g)


def _scatter_to_chips(p):
    def body(p_ref, out_ref, send, recv):
        x, y, c = _mesh_pos()
        me = 2 * x + y
        chips = _other_chips(x, y)
        sends = [_remote(p_ref.at[2 * chip[0] + chip[1]], out_ref.at[me], send.at[j], recv.at[j], (*chip, c))
                 for j, chip in enumerate(chips)]
        for cp in sends:
            cp.start()
        for j, chip in enumerate(chips):
            kj = 2 * chip[0] + chip[1]
            _remote(p_ref.at[kj], out_ref.at[kj], send.at[j], recv.at[j], (*chip, c)).wait_recv()
        for cp in sends:
            cp.wait_send()

    got = pl.pallas_call(
        body, name="scatter_to_chips", in_specs=[ANY], out_specs=ANY, out_shape=_sds(p.shape, p.dtype),
        scratch_shapes=[pltpu.SemaphoreType.DMA((3,)), pltpu.SemaphoreType.DMA((3,))],
    )(p)
    xp, yp, _ = _mesh_pos()
    me = 2 * xp + yp
    return _set_block(got, lax.dynamic_index_in_dim(p, me, axis=0, keepdims=False), me)


def _join_halves(f):
    def body(f_ref, out_ref, send, recv):
        x, y, c = _mesh_pos()
        cp = _remote(f_ref, out_ref, send, recv, (x, y, 1 - c))
        cp.start()
        cp.wait()

    other = pl.pallas_call(
        body, name="join_halves", in_specs=[ANY], out_specs=ANY, out_shape=_sds(f.shape, f.dtype),
        scratch_shapes=[pltpu.SemaphoreType.DMA, pltpu.SemaphoreType.DMA],
    )(f)
    first = lax.axis_index("c") == 0
    return jnp.concatenate([jnp.where(first, f, other), jnp.where(first, other, f)], axis=0)


def _row_tile(rows, cap=512):
    t = cap - cap % 8
    while rows % t:
        t -= 8
    return t


def _add_blocks(a, b, out_dtype):
    nb, r, n = a.shape
    tr = _row_tile(r)

    def body(a_ref, b_ref, o_ref):
        o_ref[...] = (a_ref[...] + b_ref[...]).astype(out_dtype)

    spec = pl.BlockSpec((None, tr, n), lambda k, i: (k, i, 0))
    return _call(body, name="add_blocks", grid=(nb, r // tr), in_specs=[spec, spec], out_specs=spec,
                 out_shape=_sds(a.shape, out_dtype), sem=("parallel", "parallel"))(a, b)


def _sum_leading(a, *, name):
    nk, r, n = a.shape
    tr = _row_tile(r)

    def body(a_ref, o_ref):
        acc = a_ref[0].astype(F32)
        for k in range(1, nk):
            acc = acc + a_ref[k].astype(F32)
        o_ref[...] = acc

    return _call(body, name=name, grid=(r // tr,), in_specs=[pl.BlockSpec((nk, tr, n), lambda i: (0, i, 0))],
                 out_specs=pl.BlockSpec((tr, n), lambda i: (i, 0)), out_shape=_sds((r, n)), sem=("parallel",))(a)


def _silu_outer(a, b):
    kdim, n = a.shape[1], b.shape[1]

    def body(a_ref, b_ref, o_ref):
        av = a_ref[...]
        av = av * _sigmoid(av)
        bv = b_ref[...]
        ah, bh = av.astype(BF16), bv.astype(BF16)
        al, bl = (av - ah.astype(F32)).astype(BF16), (bv - bh.astype(F32)).astype(BF16)
        o_ref[...] = _dot_tn(ah, bh) + (_dot_tn(ah, bl) + _dot_tn(al, bh))

    return _call(body, name="dw_mod", grid=(1,), in_specs=[_full(a.shape), _full(b.shape)], out_specs=_full((kdim, n)),
                 out_shape=_sds((kdim, n)))(a, b)


def _adamw(w, g, m, v):
    r, n = w.shape
    tr = _row_tile(r)

    def body(w_ref, g_ref, m_ref, v_ref, d_ref, nm_ref, nv_ref):
        gv = g_ref[...]
        nm = ADAM_B1 * m_ref[...] + (1.0 - ADAM_B1) * gv
        nv = ADAM_B2 * v_ref[...] + (1.0 - ADAM_B2) * (gv * gv)
        m_hat = nm / (1.0 - ADAM_B1 ** ADAM_STEP)
        v_hat = nv / (1.0 - ADAM_B2 ** ADAM_STEP)
        d_ref[...] = -ADAM_LR * (m_hat / (jnp.sqrt(v_hat) + ADAM_EPS) + ADAM_WD * w_ref[...])
        nm_ref[...] = nm
        nv_ref[...] = nv

    spec = pl.BlockSpec((tr, n), lambda i: (i, 0))
    return _call(body, name="adamw", grid=(r // tr,), in_specs=[spec] * 4, out_specs=[spec] * 3,
                 out_shape=[_sds((r, n))] * 3, sem=("parallel",))(w, g, m, v)


BIG = ("w_in", "w_branch_a", "w_branch_b", "w_out", "w_up", "w_down", "conv_w")
BIG_ROWS = 3584
MATRICES = ("w_mod", "w_in", "w_branch_a", "w_branch_b", "w_out", "w_up", "w_down")
SMALL = ("b_mod", "b_in", "conv_b", "ln1_g", "ln1_b", "ln2_g", "ln2_b", "c_ctx", "attn_sink", "q_norm_g", "k_norm_g", "conv_w")
SMALL_ROWS = 8 * len(SMALL)


def _rows(a, n_rows):
    flat = a.reshape(-1)
    return jnp.pad(flat, (0, n_rows * D_MODEL - flat.shape[0])).reshape(n_rows, D_MODEL)


def _group8(a):
    return _rep8(_rows(a, 1)) if a.size <= D_MODEL else _rows(a, 8)


def _ungroup8(p, shape):
    size = math.prod(shape)
    return (p[0, :size] if size <= D_MODEL else p.reshape(-1)[:size]).reshape(shape)


def _unpack_big(p, like):
    out, r = {}, 0
    for n in BIG:
        size = math.prod(like[n].shape)
        nr = size // D_MODEL if n != "conv_w" else 8
        out[n] = p[r:r + nr].reshape(-1)[:size].reshape(like[n].shape)
        r += nr
    return out


def _pack_small(t):
    return jnp.concatenate([_group8(t[n]) for n in SMALL], axis=0)


def _unpack_small(p, like):
    return {n: _ungroup8(p[8 * i:8 * i + 8], like[n].shape) for i, n in enumerate(SMALL)}


WEIGHTS = ("c_ctx", "w_mod", "b_mod", "w_in", "b_in", "attn_sink", "q_norm_g", "k_norm_g", "w_branch_a", "w_branch_b",
           "w_out", "ln1_g", "ln1_b", "w_up", "conv_w", "conv_b", "w_down", "ln2_g", "ln2_b")


def kernel(x, c, ctx, c_ctx, w_mod, b_mod, w_in, b_in, attn_sink, q_norm_g, k_norm_g, w_branch_a, w_branch_b, w_out, ln1_g, ln1_b, w_up, conv_w, conv_b, w_down, ln2_g, ln2_b, loss_target, m_c_ctx, m_w_mod, m_b_mod, m_w_in, m_b_in, m_attn_sink, m_q_norm_g, m_k_norm_g, m_w_branch_a, m_w_branch_b, m_w_out, m_ln1_g, m_ln1_b, m_w_up, m_conv_w, m_conv_b, m_w_down, m_ln2_g, m_ln2_b, v_c_ctx, v_w_mod, v_b_mod, v_w_in, v_b_in, v_attn_sink, v_q_norm_g, v_k_norm_g, v_w_branch_a, v_w_branch_b, v_w_out, v_ln1_g, v_ln1_b, v_w_up, v_conv_w, v_conv_b, v_w_down, v_ln2_g, v_ln2_b):
    w = dict(c_ctx=c_ctx, w_mod=w_mod, b_mod=b_mod, w_in=w_in, b_in=b_in, attn_sink=attn_sink, q_norm_g=q_norm_g,
             k_norm_g=k_norm_g, w_branch_a=w_branch_a, w_branch_b=w_branch_b, w_out=w_out, ln1_g=ln1_g, ln1_b=ln1_b,
             w_up=w_up, conv_w=conv_w, conv_b=conv_b, w_down=w_down, ln2_g=ln2_g, ln2_b=ln2_b)
    m = dict(c_ctx=m_c_ctx, w_mod=m_w_mod, b_mod=m_b_mod, w_in=m_w_in, b_in=m_b_in, attn_sink=m_attn_sink,
             q_norm_g=m_q_norm_g, k_norm_g=m_k_norm_g, w_branch_a=m_w_branch_a, w_branch_b=m_w_branch_b, w_out=m_w_out,
             ln1_g=m_ln1_g, ln1_b=m_ln1_b, w_up=m_w_up, conv_w=m_conv_w, conv_b=m_conv_b, w_down=m_w_down,
             ln2_g=m_ln2_g, ln2_b=m_ln2_b)
    v = dict(c_ctx=v_c_ctx, w_mod=v_w_mod, b_mod=v_b_mod, w_in=v_w_in, b_in=v_b_in, attn_sink=v_attn_sink,
             q_norm_g=v_q_norm_g, k_norm_g=v_k_norm_g, w_branch_a=v_w_branch_a, w_branch_b=v_w_branch_b, w_out=v_w_out,
             ln1_g=v_ln1_g, ln1_b=v_ln1_b, w_up=v_w_up, conv_w=v_conv_w, conv_b=v_conv_b, w_down=v_w_down,
             ln2_g=v_ln2_g, ln2_b=v_ln2_b)
    xp, yp, _ = _mesh_pos()
    me = 2 * xp + yp

    branches = jnp.concatenate([w_branch_a[0], w_branch_b[0]], axis=0)
    wide = jnp.concatenate([w_mod[0], w_in[0], w_up[0], branches], axis=1).astype(BF16)
    tall = jnp.concatenate([w_out[0], w_down[0]], axis=0).astype(BF16)
    wide4, tall4, cw4 = _gather_shards([wide, tall], conv_w[0])
    n_mod, n_in, n_up = w_mod.shape[-1], w_in.shape[-1], w_up.shape[-1]
    wmod4 = wide4[:, :, :n_mod]
    win4 = wide4[:, :, n_mod:n_mod + n_in]
    wup4 = wide4[:, :, n_mod + n_in:n_mod + n_in + n_up]
    br4 = wide4[:, :, n_mod + n_in + n_up:]
    n_br = w_branch_a.shape[1]
    wba = br4[:, :n_br].transpose(1, 0, 2).reshape(n_br, D_MODEL)
    wbb = br4[:, n_br:].transpose(1, 0, 2).reshape(n_br, D_MODEL)
    n_out = w_out.shape[1]
    w_out_full = tall4[:, :n_out].reshape(D_MODEL, D_MODEL)
    w_down_full = tall4[:, n_out:].reshape(D_FF, D_MODEL)
    cw_full = cw4.transpose(1, 0, 2).reshape(3, 2 * D_FF)

    loss, grad_x, g = _local_step(
        x[0], c, ctx[0], c_ctx[None], wmod4, b_mod, win4, b_in, attn_sink[0], q_norm_g, k_norm_g, wba, wbb, w_out_full,
        ln1_g, ln1_b, wup4, cw_full, conv_b, w_down_full, ln2_g, ln2_b, loss_target[0])
    loss = lax.psum(loss, ("x", "y", "c"))

    sent = dict(c=c, dmod=g["dmod"], dmodc=g["dmodc"], b_in=g["b_in"], conv_b=g["conv_b"], ln1_g=g["ln1_g"],
                ln1_b=g["ln1_b"], ln2_g=g["ln2_g"], ln2_b=g["ln2_b"], c_ctx=g["c_ctx"], attn_sink=g["sink"],
                q_norm_g=g["qn"], k_norm_g=g["kn"])
    every = _allgather_rows(jnp.concatenate([_group8(a) for a in sent.values()], axis=0))
    total = _sum_leading(every, name="sum_devices")
    slot = {n: slice(8 * i, 8 * i + 8) for i, n in enumerate(sent)}
    gs = {n: _ungroup8(total[slot[n]], sent[n].shape) for n in SMALL if n in sent}
    dmodc_sum = jnp.concatenate([_ungroup8(total[slot["dmodc"]], (1, 2 * D_MODEL)), jnp.zeros((1, 4 * D_MODEL), F32)],
                                axis=1)
    gs["b_mod"] = _ungroup8(total[slot["dmod"]], b_mod.shape) + dmodc_sum
    acts = jnp.concatenate([every[:, slot["c"].start], _rep8(c_ctx)], axis=0)
    dmods = jnp.concatenate([every[:, slot["dmod"]].reshape(N_DEV, -1)[:, :6 * D_MODEL], _first_row(dmodc_sum)], axis=0)
    g_w_mod = _silu_outer(acts, lax.dynamic_slice_in_dim(dmods, me * n_mod, n_mod, axis=1))

    cw_g4 = _to_blocks4(g["conv_w"])
    packed = jnp.concatenate([
        g["w_in4"].reshape(N_CHIPS, -1, D_MODEL), _to_blocks4(g["wba"]).reshape(N_CHIPS, -1, D_MODEL),
        _to_blocks4(g["wbb"]).reshape(N_CHIPS, -1, D_MODEL), g["w_out"].reshape(N_CHIPS, -1, D_MODEL),
        g["w_up4"].reshape(N_CHIPS, -1, D_MODEL), g["w_down"].reshape(N_CHIPS, -1, D_MODEL),
        jnp.pad(cw_g4.reshape(N_CHIPS, -1), ((0, 0), (0, 8 * D_MODEL - cw_g4.shape[1] * cw_g4.shape[2]))).reshape(
            N_CHIPS, 8, D_MODEL),
        jnp.zeros((N_CHIPS, BIG_ROWS - 3528, D_MODEL), F32)], axis=1)
    rh = BIG_ROWS // 2
    cpos = lax.axis_index("c")
    my_half = lax.dynamic_slice_in_dim(packed, cpos * rh, rh, axis=1)
    chip_sum = _add_blocks(my_half, _swap_other_half(packed), BF16)
    half_sum = _sum_leading(_scatter_to_chips(chip_sum), name="sum_chips")
    g_big = _unpack_big(_join_halves(half_sum), w)

    grads = dict(gs, w_mod=g_w_mod, **g_big)
    grads = {n: grads[n].reshape(w[n].shape) for n in WEIGHTS}
    delta, new_m, new_v = {}, {}, {}
    for n in MATRICES:
        outs = _adamw(*[t[n][0] for t in (w, grads, m, v)])
        delta[n], new_m[n], new_v[n] = [o[None] for o in outs]
    outs = _adamw(*[_pack_small(t) for t in (w, grads, m, v)])
    for res, o in zip((delta, new_m, new_v), outs):
        res.update(_unpack_small(o, w))
    return (loss, grad_x[None], *[grads[n] for n in WEIGHTS], *[delta[n] for n in WEIGHTS],
            *[new_m[n] for n in WEIGHTS], *[new_v[n] for n in WEIGHTS])
```

```python
import functools
import math

import jax
import jax.numpy as jnp
from jax import lax
from jax.experimental import pallas as pl
from jax.experimental.pallas import tpu as pltpu

F32 = jnp.float32
BF16 = jnp.bfloat16

D_MODEL = 1024
HEAD_DIM = 64
N_HEADS = 8
N_KV = 2
WINDOW = 128
GRID_W = 64
ROPE_THETA = 10000.0
D_FF = 2816
LN_EPS = 1e-5
QK_EPS = 1e-6
ALPHA = 2.0 ** 0.25
Q_SCALE = HEAD_DIM ** -0.5
OFF_GA = 1536
IN_COLS = 3584
ADAM_LR, ADAM_B1, ADAM_B2, ADAM_EPS, ADAM_WD, ADAM_STEP = 0.001, 0.9, 0.999, 1e-8, 0.01, 10

LANES = 128
VMEM_BUDGET = 52 * 1024 * 1024
N_CHIPS = 4
N_DEV = 8
NEG = -1e30
MESH = pl.DeviceIdType.MESH


def _sigmoid(x):
    return 1.0 / (1.0 + jnp.exp(-x))


def _dot(a, b):
    return jnp.dot(a, b, preferred_element_type=F32)


def _dot_nt(a, b):
    return lax.dot_general(a, b, (((1,), (1,)), ((), ())), preferred_element_type=F32)


def _dot_tn(a, b):
    return lax.dot_general(a, b, (((0,), (0,)), ((), ())), preferred_element_type=F32)


def _call(body, *, name, grid, in_specs, out_specs, out_shape, scratch=(), sem=None, **kw):
    params = dict(vmem_limit_bytes=VMEM_BUDGET)
    if sem is not None:
        params["dimension_semantics"] = sem
    return pl.pallas_call(body, name=name, grid=grid, in_specs=in_specs, out_specs=out_specs,
                          out_shape=out_shape, scratch_shapes=list(scratch),
                          compiler_params=pltpu.CompilerParams(**params), **kw)


def _full(shape):
    n = len(shape)
    return pl.BlockSpec(shape, lambda *_: (0,) * n)


def _sds(shape, dtype=F32):
    return jax.ShapeDtypeStruct(shape, dtype)


def _mm_nn4(a, shift, scale, w4, bias, *, mode, split_out, out_dtype, tm, name):
    m, kdim = a.shape
    nb, _, ns = w4.shape

    def body(a_ref, sh_ref, sc_ref, w_ref, b_ref, o_ref):
        av = a_ref[...]
        if mode == "modulate":
            av = av * (1.0 + sc_ref[...]) + sh_ref[...]
        else:
            av = av * _sigmoid(av)
        o_ref[...] = (_dot(av.astype(BF16), w_ref[...]) + b_ref[...]).astype(out_dtype)

    if split_out:
        out_shape = _sds((2, m, 2 * ns), out_dtype)
        out_spec = pl.BlockSpec((None, tm, ns), lambda i, k: (k // 2, i, k % 2))
    else:
        out_shape = _sds((m, nb * ns), out_dtype)
        out_spec = pl.BlockSpec((tm, ns), lambda i, k: (i, k))
    return _call(
        body, name=name, grid=(m // tm, nb),
        in_specs=[pl.BlockSpec((tm, kdim), lambda i, k: (i, 0)),
                  pl.BlockSpec((1, kdim), lambda i, k: (0, 0)),
                  pl.BlockSpec((1, kdim), lambda i, k: (0, 0)),
                  pl.BlockSpec((None, kdim, ns), lambda i, k: (k, 0, 0)),
                  pl.BlockSpec((1, ns), lambda i, k: (0, k))],
        out_specs=out_spec, out_shape=out_shape, sem=("parallel", "arbitrary"),
    )(a, shift, scale, w4, bias)


def _mm_tn(a, b, *, a_spec, b_spec, grid, out_shape, out_spec, name, mod=None, init=None, colsum_spec=None,
           colsum_shape=None):
    red = len(grid) - 1
    has_mod, has_init, has_cs = mod is not None, init is not None, colsum_spec is not None

    def body(*refs):
        refs = list(refs)
        a_ref, b_ref = refs[0], refs[1]
        pos = 2
        if has_mod:
            sh_ref, sc_ref = refs[2], refs[3]
            pos = 4
        if has_init:
            init_ref = refs[pos]
            pos += 1
        o_ref = refs[pos]
        cs_ref = refs[pos + 1] if has_cs else None
        s = pl.program_id(red)

        @pl.when(s == 0)
        def _():
            o_ref[...] = init_ref[...] if has_init else jnp.zeros(o_ref.shape, F32)
            if has_cs:
                cs_ref[...] = jnp.zeros(cs_ref.shape, F32)

        av = a_ref[...]
        if has_mod:
            av = av * (1.0 + sc_ref[...]) + sh_ref[...]
        bv = b_ref[...]
        o_ref[...] += _dot_tn(av.astype(BF16), bv)
        if has_cs:
            cs_ref[...] += jnp.broadcast_to(jnp.sum(bv.astype(F32), axis=0, keepdims=True), cs_ref.shape)

    ins, in_specs = [a, b], [a_spec, b_spec]
    if has_mod:
        kdim = mod[0].shape[-1]
        ins += list(mod)
        in_specs += [_full((1, kdim)), _full((1, kdim))]
    if has_init:
        ins.append(init)
        in_specs.append(out_spec)
    out_specs, out_shapes = out_spec, out_shape
    if has_cs:
        out_specs, out_shapes = [out_spec, colsum_spec], [out_shape, colsum_shape]
    sem = ("parallel",) * red + ("arbitrary",)
    return _call(body, name=name, grid=grid, in_specs=in_specs, out_specs=out_specs, out_shape=out_shapes,
                 sem=sem)(*ins)


def _rope_tables(n_tok):
    pos = jnp.arange(n_tok, dtype=jnp.int32)
    rows = (pos // GRID_W).astype(F32)
    cols = (pos % GRID_W).astype(F32)
    n_freq = HEAD_DIM // 4
    inv_freq = ROPE_THETA ** (-jnp.arange(n_freq, dtype=F32) / n_freq)
    ang_r = rows[:, None] * inv_freq
    ang_c = cols[:, None] * inv_freq
    cos = jnp.concatenate([jnp.cos(ang_r)] * 2 + [jnp.cos(ang_c)] * 2, axis=-1)
    sin = jnp.concatenate([-jnp.sin(ang_r), jnp.sin(ang_r), -jnp.sin(ang_c), jnp.sin(ang_c)], axis=-1)
    return jnp.tile(cos, (1, 2)), jnp.tile(sin, (1, 2))


def _lane(shape):
    return lax.broadcasted_iota(jnp.int32, shape, 1)


def _rope_partner(t, lane):
    return jnp.where((lane % 32) < 16, pltpu.roll(t, LANES - 16, 1), pltpu.roll(t, 16, 1))


def _half_mean(s, lane):
    lo = jnp.sum(jnp.where(lane < HEAD_DIM, s, 0.0), axis=-1, keepdims=True)
    hi = jnp.sum(jnp.where(lane < HEAD_DIM, 0.0, s), axis=-1, keepdims=True)
    return jnp.where(lane < HEAD_DIM, lo, hi) * (1.0 / HEAD_DIM)


def _prep(proj, cos, sin, qg, kg, *, tm, name):
    m = proj.shape[0]

    def body(p_ref, cos_ref, sin_ref, qg_ref, kg_ref, qa_ref, ka_ref, va_ref, qb_ref, kb_ref, vb_ref):
        lane = _lane((tm, LANES))
        cosv, sinv = cos_ref[...], sin_ref[...]
        low = lane < HEAD_DIM

        def rope(t):
            return t * cosv + _rope_partner(t, lane) * sinv

        def rms(t, g):
            return t * lax.rsqrt(_half_mean(t * t, lane) + QK_EPS) * g

        def place(q_ref, j, chunk):
            sw = pltpu.roll(chunk, HEAD_DIM, 1)
            if j < 2:
                h0, h1 = jnp.where(low, chunk, 0.0), jnp.where(low, sw, 0.0)
            else:
                h0, h1 = jnp.where(low, 0.0, sw), jnp.where(low, 0.0, chunk)
            q_ref[2 * j] = h0.T.astype(BF16)
            q_ref[2 * j + 1] = h1.T.astype(BF16)

        for j in range(4):
            place(qa_ref, j, rope(p_ref[:, j * LANES:(j + 1) * LANES]) * Q_SCALE)
            place(qb_ref, j, rope(rms(p_ref[:, 768 + j * LANES:768 + (j + 1) * LANES], qg_ref[...])) * Q_SCALE)
        ka_ref[...] = rope(p_ref[:, 512:640]).astype(BF16)
        va_ref[...] = p_ref[:, 640:768].astype(BF16)
        kb_ref[...] = rope(rms(p_ref[:, 1280:1408], kg_ref[...])).astype(BF16)
        vb_ref[...] = p_ref[:, 1408:1536].astype(BF16)

    row = pl.BlockSpec((tm, LANES), lambda i: (i, 0))
    qspec = pl.BlockSpec((N_HEADS, LANES, tm), lambda i: (0, 0, i))
    return _call(
        body, name=name, grid=(m // tm,),
        in_specs=[pl.BlockSpec((tm, OFF_GA), lambda i: (i, 0)), row, row, _full((1, LANES)), _full((1, LANES))],
        out_specs=[qspec, row, row, qspec, row, row],
        out_shape=[_sds((N_HEADS, LANES, m), BF16), _sds((m, LANES), BF16), _sds((m, LANES), BF16),
                   _sds((N_HEADS, LANES, m), BF16), _sds((m, LANES), BF16), _sds((m, LANES), BF16)],
        sem=("parallel",),
    )(proj, cos, sin, qg, kg)


def _prep_bwd(dqa, dka, dva, dqb, dkb, dvb, proj, cos, sin, qg, kg, dgl, *, tm, name):
    m = proj.shape[0]

    def body(dqa_ref, dka_ref, dva_ref, dqb_ref, dkb_ref, dvb_ref, p_ref, cos_ref, sin_ref, qg_ref, kg_ref,
             dgl_ref, dp_ref, dqg_ref, dkg_ref):
        i = pl.program_id(0)
        lane = _lane((tm, LANES))
        cosv, sinv = cos_ref[...], sin_ref[...]
        low = lane < HEAD_DIM

        @pl.when(i == 0)
        def _():
            dqg_ref[...] = jnp.zeros(dqg_ref.shape, F32)
            dkg_ref[...] = jnp.zeros(dkg_ref.shape, F32)

        def unrope(d):
            return d * cosv - _rope_partner(d, lane) * sinv

        def unplace(dq_ref, j):
            d0, d1 = dq_ref[2 * j].T, dq_ref[2 * j + 1].T
            if j < 2:
                return jnp.where(low, d0, pltpu.roll(d1, HEAD_DIM, 1))
            return jnp.where(low, pltpu.roll(d0, HEAD_DIM, 1), d1)

        def unrms(dtn, t, g):
            r = lax.rsqrt(_half_mean(t * t, lane) + QK_EPS)
            u = dtn * g
            dt = r * u - t * (r * r * r) * _half_mean(u * t, lane)
            return dt, jnp.sum(dtn * t * r, axis=0, keepdims=True)

        for j in range(4):
            dp_ref[:, j * LANES:(j + 1) * LANES] = (unrope(unplace(dqa_ref, j)) * Q_SCALE).astype(BF16)
            c0 = 768 + j * LANES
            dt, dg = unrms(unrope(unplace(dqb_ref, j)) * Q_SCALE, p_ref[:, c0:c0 + LANES], qg_ref[...])
            dp_ref[:, c0:c0 + LANES] = dt.astype(BF16)
            dqg_ref[:, j * LANES:(j + 1) * LANES] += dg
        dp_ref[:, 512:640] = unrope(dka_ref[...]).astype(BF16)
        dp_ref[:, 640:768] = dva_ref[...].astype(BF16)
        dt, dg = unrms(unrope(dkb_ref[...]), p_ref[:, 1280:1408], kg_ref[...])
        dp_ref[:, 1280:1408] = dt.astype(BF16)
        dkg_ref[...] += dg
        dp_ref[:, 1408:1536] = dvb_ref[...].astype(BF16)
        dp_ref[:, OFF_GA:] = dgl_ref[...]

    row = pl.BlockSpec((tm, LANES), lambda i: (i, 0))
    qspec = pl.BlockSpec((N_HEADS, LANES, tm), lambda i: (0, 0, i))
    return _call(
        body, name=name, grid=(m // tm,),
        in_specs=[qspec, row, row, qspec, row, row, pl.BlockSpec((tm, OFF_GA), lambda i: (i, 0)), row, row,
                  _full((1, LANES)), _full((1, LANES)), pl.BlockSpec((tm, IN_COLS - OFF_GA), lambda i: (i, 0))],
        out_specs=[pl.BlockSpec((tm, IN_COLS), lambda i: (i, 0)), _full((1, 512)), _full((1, LANES))],
        out_shape=[_sds((m, IN_COLS), BF16), _sds((1, 512)), _sds((1, LANES))],
        sem=("arbitrary",),
    )(dqa, dka, dva, dqb, dkb, dvb, proj, cos, sin, qg, kg, dgl)


def _attn_glob_fwd(qt, k, vt, kc, vct, *, tq, tk):
    nh, _, s = qt.shape
    nc = kc.shape[0]
    n_chunks = s // tk
    half = LANES // 2

    def body(qt_ref, k_ref, vt_ref, kc_ref, vct_ref, ot_ref, lse_ref, mrun_ref, p_hbm,
             acc_sc, st_sc, stage_sc, stagec_sc, sems, semc):
        h, i = pl.program_id(0), pl.program_id(1)
        qtv = qt_ref[...]
        acc_sc[...] = jnp.zeros(acc_sc.shape, F32)

        def p_out(slot, c):
            return pltpu.make_async_copy(stage_sc.at[slot], p_hbm.at[h, i, pl.ds(pl.multiple_of(c * tk, tk), tk), :],
                                         sems.at[slot])

        def update(st, vtv, m_old):
            m_new = jnp.maximum(m_old, jnp.max(st, axis=0, keepdims=True))
            pb = jnp.exp(st - m_new).astype(BF16)
            acc_sc[...] = acc_sc[...] * jnp.exp(m_old - m_new) + _dot(vtv, pb)
            return m_new, pb

        m, pbc = update(_dot(kc_ref[...], qtv), vct_ref[...], jnp.full((1, tq), NEG, F32))
        mrun_ref[pl.ds(n_chunks, 1), :] = m
        stagec_sc[...] = pbc
        ctx_out = pltpu.make_async_copy(stagec_sc, p_hbm.at[h, i, pl.ds(s, nc), :], semc)
        ctx_out.start()

        def step(c, st, m_old):
            slot = c % 2
            off = pl.multiple_of(c * tk, tk)
            nxt = pl.multiple_of(jnp.minimum(c + 1, n_chunks - 1) * tk, tk)
            st_next = _dot(k_ref[pl.ds(nxt, tk), :], qtv)
            m_new, pb = update(st, vt_ref[:, pl.ds(off, tk)], m_old)
            mrun_ref[pl.ds(c, 1), :] = m_new
            stage_sc[slot] = pb
            p_out(slot, c).start()
            return st_next, m_new

        def loop(c, m_old):
            st_next, m_new = step(c, st_sc[...], m_old)
            p_out(1 - c % 2, c - 1).wait()
            st_sc[...] = st_next
            return m_new

        stage_sc[1] = jnp.zeros((tk, tq), BF16)
        pltpu.make_async_copy(stage_sc.at[1], p_hbm.at[h, i, pl.ds(s + nc, tk), :], sems.at[1]).start()
        st_sc[...] = _dot(k_ref[pl.ds(0, tk), :], qtv)
        m = lax.fori_loop(0, n_chunks, loop, m)
        p_out((n_chunks - 1) % 2, n_chunks - 1).wait()
        ctx_out.wait()
        acc = acc_sc[...]
        l = jnp.where(h < nh // N_KV, acc[half:half + 1], acc[0:1])
        ot_ref[...] = (acc / l).astype(BF16)
        lse_ref[...] = m + jnp.log(l)

    grp = nh // N_KV
    return _call(
        body, name="attn_glob_fwd", grid=(nh, s // tq),
        in_specs=[pl.BlockSpec((None, LANES, tq), lambda h, i: (h, 0, i)), _full((s, LANES)),
                  pl.BlockSpec((None, LANES, s), lambda h, i: (h // grp, 0, 0)), _full((nc, LANES)),
                  pl.BlockSpec((None, LANES, nc), lambda h, i: (h // grp, 0, 0))],
        out_specs=[pl.BlockSpec((None, LANES, tq), lambda h, i: (h, 0, i)),
                   pl.BlockSpec((None, 1, tq), lambda h, i: (h, 0, i)),
                   pl.BlockSpec((None, n_chunks + 1, tq), lambda h, i: (h, 0, i)), ANY],
        out_shape=[_sds((nh, LANES, s), BF16), _sds((nh, 1, s)), _sds((nh, n_chunks + 1, s)),
                   _sds((nh, s // tq, s + nc + tk, tq), BF16)],
        scratch=[pltpu.VMEM((LANES, tq), F32), pltpu.VMEM((tk, tq), F32), pltpu.VMEM((2, tk, tq), BF16),
                 pltpu.VMEM((nc, tq), BF16), pltpu.SemaphoreType.DMA((2,)), pltpu.SemaphoreType.DMA],
        sem=("parallel", "parallel"),
    )(qt, k, vt, kc, vct)


P_AHEAD = 3


def _attn_glob_bwd(qt, dot, ot, lse, mrun, p, kt, v, kct, vc, *, tq, tk):
    nh, _, s = qt.shape
    nc = vc.shape[0]
    n_q = s // tq
    n_chunks = s // tk
    n_run = mrun.shape[1] - 1
    per_run = n_chunks // n_run

    def body(qt_ref, dot_ref, ot_ref, lse_ref, mrun_ref, p_hbm, kt_ref, v_ref, kct_ref, vc_ref,
             dqt_ref, dkt_ref, dvt_ref, dkct_ref, dvct_ref, acc_sc, dp_sc, dkt_sc, dvt_sc, p_sc, pc_sc, sems, semc):
        h, i = pl.program_id(0), pl.program_id(1)

        @pl.when(jnp.logical_and(h == 0, i == 0))
        def _():
            dkct_ref[...] = jnp.zeros(dkct_ref.shape, F32)
            dvct_ref[...] = jnp.zeros(dvct_ref.shape, F32)
            dkt_sc[...] = jnp.zeros(dkt_sc.shape, F32)
            dvt_sc[...] = jnp.zeros(dvt_sc.shape, F32)


        def p_in(slot, c):
            return pltpu.make_async_copy(p_hbm.at[h, i, pl.ds(pl.multiple_of(c * tk, tk), tk), :], p_sc.at[slot],
                                         sems.at[slot])

        ctx_in = pltpu.make_async_copy(p_hbm.at[h, i, pl.ds(s, nc), :], pc_sc, semc)
        ctx_in.start()
        for c in range(P_AHEAD):
            p_in(c, min(c, n_chunks - 1)).start()
        qtv, dotv, lse = qt_ref[...], dot_ref[...], lse_ref[...]
        delta = jnp.sum(dotv.astype(F32) * ot_ref[...].astype(F32), axis=0, keepdims=True)

        def grads(pt_stored, m_row, dpt):
            pt = pt_stored.astype(F32) * jnp.exp(m_row - lse)
            return pt.astype(BF16), (pt * (dpt - delta)).astype(BF16)

        dp_sc[...] = _dot(v_ref[pl.ds(0, tk), :], dotv)
        ctx_in.wait()
        pb, dsb = grads(pc_sc[...], mrun_ref[pl.ds(n_run, 1), :], _dot(vc_ref[...], dotv))
        acc_sc[...] = _dot(kct_ref[...], dsb)
        dkct_ref[...] += _dot_nt(qtv, dsb)
        dvct_ref[...] += _dot_nt(dotv, pb)

        def loop(c, carry):
            slot = c % (P_AHEAD + 1)
            off = pl.multiple_of(c * tk, tk)
            nxt = pl.multiple_of(jnp.minimum(c + 1, n_chunks - 1) * tk, tk)
            p_in(slot, c).wait()
            p_in((c + P_AHEAD) % (P_AHEAD + 1), jnp.minimum(c + P_AHEAD, n_chunks - 1)).start()
            dpt = dp_sc[...]
            dp_next = _dot(v_ref[pl.ds(nxt, tk), :], dotv)
            pb, dsb = grads(p_sc[slot], mrun_ref[pl.ds(c // per_run, 1), :], dpt)
            acc_sc[...] += _dot(kt_ref[:, pl.ds(off, tk)], dsb)
            dkt_sc[:, pl.ds(off, tk)] += _dot_nt(qtv, dsb)
            dvt_sc[:, pl.ds(off, tk)] += _dot_nt(dotv, pb)
            dp_sc[...] = dp_next
            return carry

        lax.fori_loop(0, n_chunks, loop, 0)
        for c in range(n_chunks, n_chunks + P_AHEAD):
            p_in(c % (P_AHEAD + 1), n_chunks - 1).wait()
        dqt_ref[...] = acc_sc[...]

        @pl.when(jnp.logical_and(h == nh - 1, i == n_q - 1))
        def _():
            pltpu.sync_copy(dkt_sc, dkt_ref)
            pltpu.sync_copy(dvt_sc, dvt_ref)

    qs = pl.BlockSpec((None, LANES, tq), lambda h, i: (h, 0, i))
    rs = pl.BlockSpec((None, 1, tq), lambda h, i: (h, 0, i))
    return _call(
        body, name="attn_glob_bwd", grid=(nh, n_q),
        in_specs=[qs, qs, qs, rs, pl.BlockSpec((None, n_run + 1, tq), lambda h, i: (h, 0, i)), ANY,
                  _full((LANES, s)), _full((s, LANES)), _full((LANES, nc)), _full((nc, LANES))],
        out_specs=[qs, ANY, ANY, _full((LANES, nc)), _full((LANES, nc))],
        out_shape=[_sds((nh, LANES, s)), _sds((LANES, s)), _sds((LANES, s)), _sds((LANES, nc)), _sds((LANES, nc))],
        scratch=[pltpu.VMEM((LANES, tq), F32), pltpu.VMEM((tk, tq), F32), pltpu.VMEM((LANES, s), F32),
                 pltpu.VMEM((LANES, s), F32), pltpu.VMEM((P_AHEAD + 1, tk, tq), BF16), pltpu.VMEM((nc, tq), BF16),
                 pltpu.SemaphoreType.DMA((P_AHEAD + 1,)), pltpu.SemaphoreType.DMA],
        sem=("arbitrary", "arbitrary"),
    )(qt, dot, ot, lse, mrun, p, kt, v, kct, vc)


WIN_SPAN = 2 * WINDOW


def _band(rows0, cols0, shape):
    r = rows0 + lax.broadcasted_iota(jnp.int32, shape, 0)
    c = cols0 + lax.broadcasted_iota(jnp.int32, shape, 1)
    return jnp.abs(r - c) <= WINDOW


def _win_start(blk, t, s):
    return pl.multiple_of(jnp.clip(blk * t - WINDOW, 0, s - t - WIN_SPAN), WINDOW)


def _attn_win_fwd(qt, k, vt, kc, vct, sink, *, tq):
    nh, _, s = qt.shape
    nc = kc.shape[0]
    tw = tq + WIN_SPAN
    half = LANES // 2
    grp = nh // N_KV

    def body(sink_ref, qt_ref, k_ref, vt_ref, kc_ref, vct_ref, ot_ref, lse_ref):
        h, i = pl.program_id(0), pl.program_id(1)
        k0 = _win_start(i, tq, s)
        qtv = qt_ref[...]
        st = jnp.where(_band(k0, i * tq, (tw, tq)), _dot(k_ref[pl.ds(k0, tw), :], qtv), NEG)
        stc = _dot(kc_ref[...], qtv)
        snk = sink_ref[h]
        m = jnp.maximum(jnp.maximum(jnp.max(st, axis=0, keepdims=True), jnp.max(stc, axis=0, keepdims=True)), snk)
        acc = (_dot(vt_ref[:, pl.ds(k0, tw)], jnp.exp(st - m).astype(BF16))
               + _dot(vct_ref[...], jnp.exp(stc - m).astype(BF16)))
        l = jnp.where(h < grp, acc[half:half + 1], acc[0:1]) + jnp.exp(snk - m)
        ot_ref[...] = (acc / l).astype(BF16)
        lse_ref[...] = m + jnp.log(l)

    return _call(
        body, name="attn_win_fwd", grid=(nh, s // tq),
        in_specs=[pl.BlockSpec(memory_space=pltpu.SMEM),
                  pl.BlockSpec((None, LANES, tq), lambda h, i: (h, 0, i)), _full((s, LANES)),
                  pl.BlockSpec((None, LANES, s), lambda h, i: (h // grp, 0, 0)), _full((nc, LANES)),
                  pl.BlockSpec((None, LANES, nc), lambda h, i: (h // grp, 0, 0))],
        out_specs=[pl.BlockSpec((None, LANES, tq), lambda h, i: (h, 0, i)),
                   pl.BlockSpec((None, 1, tq), lambda h, i: (h, 0, i))],
        out_shape=[_sds((nh, LANES, s), BF16), _sds((nh, 1, s))],
        sem=("parallel", "parallel"),
    )(sink, qt, k, vt, kc, vct)


def _attn_win_bwd(qt, dot, ot, lse, k, kt, v, kc, kct, vc, sink, *, tq):
    nh, _, s = qt.shape
    nc = kc.shape[0]
    tw = tq + WIN_SPAN
    nq = s // tq

    def body(sink_ref, qt_ref, dot_ref, ot_ref, lse_ref, k_ref, kt_ref, v_ref, kc_ref, kct_ref, vc_ref,
             dqt_ref, dkt_ref, dvt_ref, dkct_ref, dvct_ref, dsk_ref, dkt_sc, dvt_sc):
        h, i = pl.program_id(0), pl.program_id(1)

        @pl.when(jnp.logical_and(h == 0, i == 0))
        def _():
            dkct_ref[...] = jnp.zeros(dkct_ref.shape, F32)
            dvct_ref[...] = jnp.zeros(dvct_ref.shape, F32)
            dkt_sc[...] = jnp.zeros(dkt_sc.shape, F32)
            dvt_sc[...] = jnp.zeros(dvt_sc.shape, F32)

        k0 = _win_start(i, tq, s)
        span = pl.ds(k0, tw)
        qtv, dotv, lse = qt_ref[...], dot_ref[...], lse_ref[...]
        delta = jnp.sum(dotv.astype(F32) * ot_ref[...].astype(F32), axis=0, keepdims=True)
        pt = jnp.where(_band(k0, i * tq, (tw, tq)), jnp.exp(_dot(k_ref[span, :], qtv) - lse), 0.0)
        dsb = (pt * (_dot(v_ref[span, :], dotv) - delta)).astype(BF16)
        pct = jnp.exp(_dot(kc_ref[...], qtv) - lse)
        dscb = (pct * (_dot(vc_ref[...], dotv) - delta)).astype(BF16)
        dqt_ref[...] = _dot(kt_ref[:, span], dsb) + _dot(kct_ref[...], dscb)
        dkt_sc[:, span] += _dot_nt(qtv, dsb)
        dvt_sc[:, span] += _dot_nt(dotv, pt.astype(BF16))
        dkct_ref[...] += _dot_nt(qtv, dscb)
        dvct_ref[...] += _dot_nt(dotv, pct.astype(BF16))
        dsk = -jnp.sum(jnp.exp(sink_ref[h] - lse) * delta)
        dsk_ref[...] = jnp.full(dsk_ref.shape, dsk, F32)

        @pl.when(jnp.logical_and(h == nh - 1, i == nq - 1))
        def _():
            pltpu.sync_copy(dkt_sc, dkt_ref)
            pltpu.sync_copy(dvt_sc, dvt_ref)

    qs = pl.BlockSpec((None, LANES, tq), lambda h, i: (h, 0, i))
    rs = pl.BlockSpec((None, 1, tq), lambda h, i: (h, 0, i))
    return _call(
        body, name="attn_win_bwd", grid=(nh, nq),
        in_specs=[pl.BlockSpec(memory_space=pltpu.SMEM), qs, qs, qs, rs, _full((s, LANES)), _full((LANES, s)),
                  _full((s, LANES)), _full((nc, LANES)), _full((LANES, nc)), _full((nc, LANES))],
        out_specs=[qs, ANY, ANY, _full((LANES, nc)), _full((LANES, nc)),
                   pl.BlockSpec((None, None, 8, LANES), lambda h, i: (h, i, 0, 0))],
        out_shape=[_sds((nh, LANES, s)), _sds((LANES, s)), _sds((LANES, s)), _sds((LANES, nc)), _sds((LANES, nc)),
                   _sds((nh, nq, 8, LANES))],
        scratch=[pltpu.VMEM((LANES, s), F32), pltpu.VMEM((LANES, s), F32)],
        sem=("arbitrary", "arbitrary"),
    )(sink, qt, dot, ot, lse, k, kt, v, kc, kct, vc)


def _ln_fwd(z, g, b):
    mu = jnp.mean(z, axis=-1, keepdims=True)
    zc = z - mu
    r = lax.rsqrt(jnp.mean(zc * zc, axis=-1, keepdims=True) + LN_EPS)
    return zc * r * g + b, mu, r


def _ln_bwd(dy, xhat, r, g):
    dxh = dy * g
    return r * (dxh - jnp.mean(dxh, axis=-1, keepdims=True) - xhat * jnp.mean(dxh * xhat, axis=-1, keepdims=True))


def _heads_matmul(ot_ref, w_ref):
    acc = _dot_tn(ot_ref[0], w_ref[0])
    for h in range(1, N_HEADS):
        acc += _dot_tn(ot_ref[h], w_ref[h])
    return acc


def _gate_specs(tm):
    return [pl.BlockSpec((tm, 512), functools.partial(lambda i, b: (i, b), b=OFF_GA // 512 + b)) for b in range(4)]


def _merge_fwd(oat, obt, proj, x, gate1, wba, wbb, w_out, ln_g, ln_b, *, tm):
    s = x.shape[0]

    def body(oa_ref, ob_ref, g0, g1, g2, g3, x_ref, gt_ref, wba_ref, wbb_ref, wo_ref, lg_ref, lb_ref,
             x1_ref, y_ref, mu_ref, r_ref):
        ga = _sigmoid(jnp.concatenate([g0[...], g1[...]], axis=1))
        gb = _sigmoid(jnp.concatenate([g2[...], g3[...]], axis=1))
        merged = ga * _heads_matmul(oa_ref, wba_ref) + gb * _heads_matmul(ob_ref, wbb_ref)
        y = _dot(merged.astype(BF16), wo_ref[...])
        x1, mu, r = _ln_fwd(ALPHA * x_ref[...] + gt_ref[...] * y, lg_ref[...], lb_ref[...])
        x1_ref[...] = x1
        y_ref[...] = y
        mu_ref[...] = mu
        r_ref[...] = r

    hts = pl.BlockSpec((N_HEADS, LANES, tm), lambda i: (0, 0, i))
    row = pl.BlockSpec((tm, D_MODEL), lambda i: (i, 0))
    col = pl.BlockSpec((tm, 1), lambda i: (i, 0))
    vec = _full((1, D_MODEL))
    wh = _full((N_HEADS, LANES, D_MODEL))
    return _call(
        body, name="merge_fwd", grid=(s // tm,),
        in_specs=[hts, hts, *_gate_specs(tm), row, vec, wh, wh, _full((D_MODEL, D_MODEL)), vec, vec],
        out_specs=[row, row, col, col],
        out_shape=[_sds((s, D_MODEL)), _sds((s, D_MODEL)), _sds((s, 1)), _sds((s, 1))],
        sem=("parallel",),
    )(oat, obt, proj, proj, proj, proj, x, gate1, wba, wbb, w_out, ln_g, ln_b)


def _merge_bwd(dy, oat, obt, proj, wba, wbb, w_out, *, tm):
    s = dy.shape[0]

    def body(dy_ref, oat_ref, obt_ref, g0, g1, g2, g3, wba_ref, wbb_ref, wo_ref,
             dgl_ref, doat_ref, dobt_ref, mg_ref, dwa_ref, dwb_ref):
        @pl.when(pl.program_id(0) == 0)
        def _():
            dwa_ref[...] = jnp.zeros(dwa_ref.shape, F32)
            dwb_ref[...] = jnp.zeros(dwb_ref.shape, F32)

        dm = _dot_nt(dy_ref[...], wo_ref[...])
        ga = _sigmoid(jnp.concatenate([g0[...], g1[...]], axis=1))
        gb = _sigmoid(jnp.concatenate([g2[...], g3[...]], axis=1))
        pa, pb = _heads_matmul(oat_ref, wba_ref), _heads_matmul(obt_ref, wbb_ref)
        mg_ref[...] = (ga * pa + gb * pb).astype(BF16)
        dgl_ref[:, :D_MODEL] = (dm * pa * ga * (1.0 - ga)).astype(BF16)
        dgl_ref[:, D_MODEL:] = (dm * pb * gb * (1.0 - gb)).astype(BF16)
        dpa, dpb = (dm * ga).astype(BF16), (dm * gb).astype(BF16)
        for h in range(N_HEADS):
            doat_ref[h] = _dot_nt(wba_ref[h], dpa).astype(BF16)
            dobt_ref[h] = _dot_nt(wbb_ref[h], dpb).astype(BF16)
            dwa_ref[h] += _dot(oat_ref[h], dpa)
            dwb_ref[h] += _dot(obt_ref[h], dpb)

    hts = pl.BlockSpec((N_HEADS, LANES, tm), lambda i: (0, 0, i))
    row = pl.BlockSpec((tm, D_MODEL), lambda i: (i, 0))
    wh = _full((N_HEADS, LANES, D_MODEL))
    return _call(
        body, name="merge_bwd", grid=(s // tm,),
        in_specs=[row, hts, hts, *_gate_specs(tm), wh, wh, _full((D_MODEL, D_MODEL))],
        out_specs=[pl.BlockSpec((tm, 2 * D_MODEL), lambda i: (i, 0)), hts, hts, row, wh, wh],
        out_shape=[_sds((s, 2 * D_MODEL), BF16), _sds((N_HEADS, LANES, s), BF16), _sds((N_HEADS, LANES, s), BF16),
                   _sds((s, D_MODEL), BF16), _sds((N_HEADS, LANES, D_MODEL)), _sds((N_HEADS, LANES, D_MODEL))],
        sem=("arbitrary",),
    )(dy, oat, obt, proj, proj, proj, proj, wba, wbb, w_out)


FF_TC = 256


def _shift_rows(t, prev_row, next_row):
    n = t.shape[0]
    r = lax.broadcasted_iota(jnp.int32, t.shape, 0)
    up = jnp.where(r == 0, prev_row, pltpu.roll(t, 1, 0))
    dn = jnp.where(r == n - 1, next_row, pltpu.roll(t, n - 1, 0))
    return up, dn


def _halo_specs(tm, s, tc):
    nb8 = s // 8
    main = pl.BlockSpec((2, tm, tc), lambda j, i: (0, i, j))
    prev = pl.BlockSpec((2, 8, tc), lambda j, i: (0, jnp.maximum(i * (tm // 8) - 1, 0), j))
    nxt = pl.BlockSpec((2, 8, tc), lambda j, i: (0, jnp.minimum((i + 1) * (tm // 8), nb8 - 1), j))
    return main, prev, nxt


def _halo_rows(prev_ref, next_ref, half, i, n_i):
    prev_row = jnp.where(i == 0, 0.0, prev_ref[half, 7:8, :].astype(F32))
    next_row = jnp.where(i == n_i - 1, 0.0, next_ref[half, 0:1, :].astype(F32))
    return prev_row, next_row


def _conv(t, prev_row, next_row, w, b):
    up, dn = _shift_rows(t, prev_row, next_row)
    return w[0:1, :] * up + w[1:2, :] * t + w[2:3, :] * dn + b


def _ffn_act_fwd(u, cw, cb, *, tm):
    _, s, ff = u.shape
    n_i = s // tm

    def body(u_ref, up_ref, un_ref, cw_ref, cb_ref, a_ref):
        i = pl.program_id(1)
        gc = _conv(u_ref[0], *_halo_rows(up_ref, un_ref, 0, i, n_i), cw_ref[0], cb_ref[0])
        vc = _conv(u_ref[1], *_halo_rows(up_ref, un_ref, 1, i, n_i), cw_ref[1], cb_ref[1])
        a_ref[...] = (gc * _sigmoid(gc) * vc).astype(BF16)

    main, prev, nxt = _halo_specs(tm, s, FF_TC)
    return _call(
        body, name="ffn_act_fwd", grid=(ff // FF_TC, n_i),
        in_specs=[main, prev, nxt, pl.BlockSpec((2, 3, FF_TC), lambda j, i: (0, 0, j)),
                  pl.BlockSpec((2, 1, FF_TC), lambda j, i: (0, 0, j))],
        out_specs=pl.BlockSpec((tm, FF_TC), lambda j, i: (i, j)),
        out_shape=_sds((s, ff), BF16), sem=("parallel", "parallel"),
    )(u, u, u, cw, cb)


def _ffn_act_bwd(dy2, w_down, u, cw, cb, *, tm):
    _, s, ff = u.shape
    n_i = s // tm

    def body(dy_ref, wd_ref, u_ref, up_ref, un_ref, cw_ref, cb_ref, dc_ref, dcw_ref, dcb_ref):
        i = pl.program_id(1)

        @pl.when(i == 0)
        def _():
            dcw_ref[...] = jnp.zeros(dcw_ref.shape, F32)
            dcb_ref[...] = jnp.zeros(dcb_ref.shape, F32)

        da = _dot_nt(dy_ref[...], wd_ref[...])
        ug, uv = u_ref[0], u_ref[1]
        ugp, ugn = _shift_rows(ug, *_halo_rows(up_ref, un_ref, 0, i, n_i))
        uvp, uvn = _shift_rows(uv, *_halo_rows(up_ref, un_ref, 1, i, n_i))
        wg, wv = cw_ref[0], cw_ref[1]
        gc = wg[0:1, :] * ugp + wg[1:2, :] * ug + wg[2:3, :] * ugn + cb_ref[0]
        vc = wv[0:1, :] * uvp + wv[1:2, :] * uv + wv[2:3, :] * uvn + cb_ref[1]
        sg = _sigmoid(gc)
        dg = da * vc * sg * (1.0 + gc * (1.0 - sg))
        dv = da * gc * sg
        dc_ref[0] = dg
        dc_ref[1] = dv
        for half, (d, taps) in enumerate(((dg, (ugp, ug, ugn)), (dv, (uvp, uv, uvn)))):
            for tap in range(3):
                dcw_ref[half, tap:tap + 1, :] += jnp.sum(d * taps[tap], axis=0, keepdims=True)
            dcb_ref[half] += jnp.sum(d, axis=0, keepdims=True)

    main, prev, nxt = _halo_specs(tm, s, FF_TC)
    return _call(
        body, name="ffn_act_bwd", grid=(ff // FF_TC, n_i),
        in_specs=[pl.BlockSpec((tm, D_MODEL), lambda j, i: (i, 0)), pl.BlockSpec((FF_TC, D_MODEL), lambda j, i: (j, 0)),
                  main, prev, nxt, pl.BlockSpec((2, 3, FF_TC), lambda j, i: (0, 0, j)),
                  pl.BlockSpec((2, 1, FF_TC), lambda j, i: (0, 0, j))],
        out_specs=[main, pl.BlockSpec((2, 3, FF_TC), lambda j, i: (0, 0, j)),
                   pl.BlockSpec((2, 1, FF_TC), lambda j, i: (0, 0, j))],
        out_shape=[_sds((2, s, ff)), _sds((2, 3, ff)), _sds((2, 1, ff))],
        sem=("parallel", "arbitrary"),
    )(dy2, w_down, u, u, u, cw, cb)


def _conv_bwd_input(dc, cw, *, tm):
    _, s, ff = dc.shape
    n_i = s // tm

    def body(d_ref, dp_ref, dn_ref, cw_ref, du_ref):
        i = pl.program_id(1)
        for half in range(2):
            up, dn = _shift_rows(d_ref[half], *_halo_rows(dp_ref, dn_ref, half, i, n_i))
            w = cw_ref[half]
            du_ref[half] = (w[0:1, :] * dn + w[1:2, :] * d_ref[half] + w[2:3, :] * up).astype(BF16)

    main, prev, nxt = _halo_specs(tm, s, FF_TC)
    return _call(
        body, name="conv_bwd_input", grid=(ff // FF_TC, n_i),
        in_specs=[main, prev, nxt, pl.BlockSpec((2, 3, FF_TC), lambda j, i: (0, 0, j))],
        out_specs=main, out_shape=_sds((2, s, ff), BF16), sem=("parallel", "parallel"),
    )(dc, dc, dc, cw)


def _ffn_down_loss(a, w_down, x1, target, gate2, ln_g, ln_b, *, tm):
    s, ff = a.shape
    n_i = s // tm

    def body(a_ref, wd_ref, x1_ref, tg_ref, gt_ref, lg_ref, lb_ref, ls_ref, dy_ref, dx_ref, dg_ref, db_ref, dgt_ref):
        @pl.when(pl.program_id(0) == 0)
        def _():
            dg_ref[...] = jnp.zeros(dg_ref.shape, F32)
            db_ref[...] = jnp.zeros(db_ref.shape, F32)
            dgt_ref[...] = jnp.zeros(dgt_ref.shape, F32)

        y2 = _dot(a_ref[...], wd_ref[...])
        z = ALPHA * x1_ref[...] + gt_ref[...] * y2
        mu = jnp.mean(z, axis=-1, keepdims=True)
        zc = z - mu
        r = lax.rsqrt(jnp.mean(zc * zc, axis=-1, keepdims=True) + LN_EPS)
        xhat = zc * r
        diff = xhat * lg_ref[...] + lb_ref[...] - tg_ref[...]
        ls_ref[...] = jnp.full(ls_ref.shape, 0.5 / D_MODEL * jnp.sum(diff * diff), F32)
        dx2 = diff * (1.0 / D_MODEL)
        dg_ref[...] += jnp.sum(dx2 * xhat, axis=0, keepdims=True)
        db_ref[...] += jnp.sum(dx2, axis=0, keepdims=True)
        dz = _ln_bwd(dx2, xhat, r, lg_ref[...])
        dgt_ref[...] += jnp.sum(dz * y2, axis=0, keepdims=True)
        dy_ref[...] = (gt_ref[...] * dz).astype(BF16)
        dx_ref[...] = ALPHA * dz

    row = pl.BlockSpec((tm, D_MODEL), lambda i: (i, 0))
    vec = _full((1, D_MODEL))
    return _call(
        body, name="ffn_down_loss", grid=(n_i,),
        in_specs=[pl.BlockSpec((tm, ff), lambda i: (i, 0)), _full((ff, D_MODEL)), row, row, vec, vec, vec],
        out_specs=[pl.BlockSpec((None, 8, LANES), lambda i: (i, 0, 0)), row, row, vec, vec, vec],
        out_shape=[_sds((n_i, 8, LANES)), _sds((s, D_MODEL), BF16), _sds((s, D_MODEL)),
                   _sds((1, D_MODEL)), _sds((1, D_MODEL)), _sds((1, D_MODEL))],
        sem=("arbitrary",),
    )(a, w_down, x1, target, gate2, ln_g, ln_b)


def _ffn_up_bwd(du, wup4, dx1a, x1, scale2, x, y, mu1, r1, gate1, ln_g, *, tm):
    s = x.shape[0]
    nb, _, ns = wup4.shape

    def body(du_ref, w_ref, dxa_ref, x1_ref, sc_ref, x_ref, y_ref, mu_ref, r_ref, gt_ref, lg_ref,
             dxo_ref, dy_ref, dsc_ref, dsh_ref, dg_ref, db_ref, dgt_ref, acc):
        i, k = pl.program_id(0), pl.program_id(1)

        @pl.when(jnp.logical_and(i == 0, k == 0))
        def _():
            for ref in (dsc_ref, dsh_ref, dg_ref, db_ref, dgt_ref):
                ref[...] = jnp.zeros(ref.shape, F32)

        @pl.when(k == 0)
        def _():
            acc[...] = jnp.zeros(acc.shape, F32)

        acc[...] += _dot_nt(du_ref[...], w_ref[...])

        @pl.when(k == nb - 1)
        def _():
            dh = acc[...]
            x1 = x1_ref[...]
            dsc_ref[...] += jnp.sum(dh * x1, axis=0, keepdims=True)
            dsh_ref[...] += jnp.sum(dh, axis=0, keepdims=True)
            dx1 = dxa_ref[...] + dh * (1.0 + sc_ref[...])
            yv = y_ref[...]
            xhat = (ALPHA * x_ref[...] + gt_ref[...] * yv - mu_ref[...]) * r_ref[...]
            dg_ref[...] += jnp.sum(dx1 * xhat, axis=0, keepdims=True)
            db_ref[...] += jnp.sum(dx1, axis=0, keepdims=True)
            dz = _ln_bwd(dx1, xhat, r_ref[...], lg_ref[...])
            dgt_ref[...] += jnp.sum(dz * yv, axis=0, keepdims=True)
            dy_ref[...] = (gt_ref[...] * dz).astype(BF16)
            dxo_ref[...] = ALPHA * dz

    row = pl.BlockSpec((tm, D_MODEL), lambda i, k: (i, 0))
    col = pl.BlockSpec((tm, 1), lambda i, k: (i, 0))
    vec = _full((1, D_MODEL))
    return _call(
        body, name="ffn_up_bwd", grid=(s // tm, nb),
        in_specs=[pl.BlockSpec((None, tm, ns), lambda i, k: (k // 2, i, k % 2)),
                  pl.BlockSpec((None, D_MODEL, ns), lambda i, k: (k, 0, 0)),
                  row, row, vec, row, row, col, col, vec, vec],
        out_specs=[row, row, vec, vec, vec, vec, vec],
        out_shape=[_sds((s, D_MODEL)), _sds((s, D_MODEL), BF16)] + [_sds((1, D_MODEL))] * 5,
        scratch=[pltpu.VMEM((tm, D_MODEL), F32)],
        sem=("arbitrary", "arbitrary"),
    )(du, wup4, dx1a, x1, scale2, x, y, mu1, r1, gate1, ln_g)


def _mm_nt4_mod_bwd(dp, w4, dxa, x, scale, *, tm, name):
    m = x.shape[0]
    nb, kdim, ns = w4.shape

    def body(dp_ref, w_ref, dxa_ref, x_ref, sc_ref, dx_ref, dsc_ref, dsh_ref, acc):
        i, k = pl.program_id(0), pl.program_id(1)

        @pl.when(jnp.logical_and(i == 0, k == 0))
        def _():
            dsc_ref[...] = jnp.zeros(dsc_ref.shape, F32)
            dsh_ref[...] = jnp.zeros(dsh_ref.shape, F32)

        @pl.when(k == 0)
        def _():
            acc[...] = jnp.zeros(acc.shape, F32)

        acc[...] += _dot_nt(dp_ref[...], w_ref[...])

        @pl.when(k == nb - 1)
        def _():
            dh = acc[...]
            dsc_ref[...] += jnp.sum(dh * x_ref[...], axis=0, keepdims=True)
            dsh_ref[...] += jnp.sum(dh, axis=0, keepdims=True)
            dx_ref[...] = dxa_ref[...] + dh * (1.0 + sc_ref[...])

    row = pl.BlockSpec((tm, kdim), lambda i, k: (i, 0))
    vec = _full((1, kdim))
    return _call(
        body, name=name, grid=(m // tm, nb),
        in_specs=[pl.BlockSpec((tm, ns), lambda i, k: (i, k)), pl.BlockSpec((None, kdim, ns), lambda i, k: (k, 0, 0)),
                  row, row, vec],
        out_specs=[row, vec, vec],
        out_shape=[_sds((m, kdim)), _sds((1, kdim)), _sds((1, kdim))],
        scratch=[pltpu.VMEM((tm, kdim), F32)],
        sem=("arbitrary", "arbitrary"),
    )(dp, w4, dxa, x, scale)


def _pad_heads_w(w):
    w8 = w.reshape(N_HEADS, HEAD_DIM, w.shape[-1])
    z = jnp.zeros_like(w8)
    first = (jnp.arange(N_HEADS) < N_HEADS // N_KV)[:, None, None]
    return jnp.where(first, jnp.concatenate([w8, z], axis=1), jnp.concatenate([z, w8], axis=1))


def _unpad_heads_w(g):
    first = (jnp.arange(N_HEADS) < N_HEADS // N_KV)[:, None, None]
    return jnp.where(first, g[:, :HEAD_DIM], g[:, HEAD_DIM:]).reshape(N_HEADS * HEAD_DIM, g.shape[-1])


def _ones_beside(vt):
    half = vt.shape[0] // 2
    ones = jnp.ones((half, vt.shape[1]), vt.dtype)
    return jnp.stack([jnp.concatenate([vt[:half], ones], axis=0), jnp.concatenate([ones, vt[half:]], axis=0)])


def _rep8(a):
    return jnp.broadcast_to(a.reshape(1, -1), (8, a.size))


def _first_row(a):
    r8 = _rep8(a)
    return jnp.where(lax.broadcasted_iota(jnp.int32, r8.shape, 0) == 0, r8, 0.0)


def _to_blocks4(w):
    k, n = w.shape
    return w.reshape(k, N_CHIPS, n // N_CHIPS).transpose(1, 0, 2)


def _local_step(x, c, ctx, c_ctx, wmod4, b_mod, win4, b_in, sink, qn, kn, wba, wbb, w_out, ln1_g, ln1_b,
                wup4, cw, cb, w_down, ln2_g, ln2_b, target):
    s, nc = x.shape[0], ctx.shape[0]
    tm = min(512, s)
    tm2 = min(256, s)
    zvec = jnp.zeros((1, D_MODEL), F32)

    cc = jnp.concatenate([_rep8(c), _rep8(c_ctx)], axis=0)
    mods = _mm_nn4(cc, zvec, zvec, wmod4, b_mod, mode="silu", split_out=False, out_dtype=F32, tm=16, name="mod_vectors")
    shift1, scale1, gate1, shift2, scale2, gate2 = [mods[0:1, i * D_MODEL:(i + 1) * D_MODEL] for i in range(6)]
    shift_c, scale_c = mods[8:9, :D_MODEL], mods[8:9, D_MODEL:2 * D_MODEL]

    cos, sin = _rope_tables(s)
    cos_c, sin_c = jnp.ones((nc, LANES), F32), jnp.zeros((nc, LANES), F32)
    qg, kg = jnp.tile(qn, (1, 2)), jnp.tile(kn, (1, 2))

    proj_c = _mm_nn4(ctx, shift_c, scale_c, win4, b_in, mode="modulate", split_out=False, out_dtype=F32, tm=nc,
                     name="in_proj_ctx")
    _, kac, vac, _, kbc, vbc = _prep(proj_c, cos_c, sin_c, qg, kg, tm=nc, name="prep_ctx")
    proj = _mm_nn4(x, shift1, scale1, win4, b_in, mode="modulate", split_out=False, out_dtype=F32, tm=tm, name="in_proj")
    qat, ka, va, qbt, kb, vb = _prep(proj, cos, sin, qg, kg, tm=tm, name="prep")
    oat, lse_a = _attn_win_fwd(qat, ka, _ones_beside(va.T), kac, _ones_beside(vac.T), sink, tq=tm)
    obt, lse_b, mrun_b, pbt = _attn_glob_fwd(qbt, kb, _ones_beside(vb.T), kbc, _ones_beside(vbc.T), tq=tm,
                                             tk=min(1024, s))
    wba_p, wbb_p = _pad_heads_w(wba), _pad_heads_w(wbb)
    x1, y, mu1, r1 = _merge_fwd(oat, obt, proj, x, gate1, wba_p, wbb_p, w_out, ln1_g, ln1_b, tm=tm)
    u = _mm_nn4(x1, shift2, scale2, wup4, jnp.zeros((1, 2 * D_FF), F32), mode="modulate", split_out=True,
                out_dtype=F32, tm=tm, name="ffn_up")
    cw2 = cw.reshape(3, 2, D_FF).transpose(1, 0, 2)
    cb2 = cb.reshape(2, 1, D_FF)
    a = _ffn_act_fwd(u, cw2, cb2, tm=tm)
    ls, dy2, dx1a, dln2_g, dln2_b, dgate2 = _ffn_down_loss(a, w_down, x1, target, gate2, ln2_g, ln2_b, tm=tm)
    loss = jnp.sum(ls[:, 0, 0])

    n_s = s // tm
    dw_down = _mm_tn(a, dy2, a_spec=pl.BlockSpec((tm, D_FF), lambda t: (t, 0)),
                     b_spec=pl.BlockSpec((tm, D_MODEL), lambda t: (t, 0)), grid=(n_s,),
                     out_shape=_sds((D_FF, D_MODEL)), out_spec=_full((D_FF, D_MODEL)), name="dw_down")
    dc, dcw2, dcb2 = _ffn_act_bwd(dy2, w_down, u, cw2, cb2, tm=tm)
    du = _conv_bwd_input(dc, cw2, tm=tm)
    dxz1, dy, dscale2, dshift2, dln1_g, dln1_b, dgate1 = _ffn_up_bwd(
        du, wup4, dx1a, x1, scale2, x, y, mu1, r1, gate1, ln1_g, tm=tm)
    ns_up = wup4.shape[-1]
    dw_up4 = _mm_tn(x1, du, a_spec=pl.BlockSpec((tm, D_MODEL), lambda k, t: (t, 0)),
                    b_spec=pl.BlockSpec((None, tm, ns_up), lambda k, t: (k // 2, t, k % 2)), grid=(N_CHIPS, n_s),
                    out_shape=_sds((N_CHIPS, D_MODEL, ns_up)),
                    out_spec=pl.BlockSpec((None, D_MODEL, ns_up), lambda k, t: (k, 0, 0)),
                    mod=(shift2, scale2), name="dw_up")

    dgl, doat, dobt, merged, dwba_p, dwbb_p = _merge_bwd(dy, oat, obt, proj, wba_p, wbb_p, w_out, tm=tm2)
    dwba, dwbb = _unpad_heads_w(dwba_p), _unpad_heads_w(dwbb_p)
    rowspec = pl.BlockSpec((tm, D_MODEL), lambda t: (t, 0))
    dw_out = _mm_tn(merged, dy, a_spec=rowspec, b_spec=rowspec, grid=(n_s,), out_shape=_sds((D_MODEL, D_MODEL)),
                    out_spec=_full((D_MODEL, D_MODEL)), name="dw_out")

    dqat, dkat, dvat, dkact, dvact, dsk = _attn_win_bwd(qat, doat, oat, lse_a, ka, ka.T, va, kac, kac.T, vac, sink, tq=tm)
    dka, dva, dkac, dvac = dkat.T, dvat.T, dkact.T, dvact.T
    dqbt, dkbt, dvbt, dkbct, dvbct = _attn_glob_bwd(qbt, dobt, obt, lse_b, mrun_b, pbt, kb.T, vb, kbc.T, vbc, tq=tm, tk=tm)
    dkb, dvb, dkbc, dvbc = dkbt.T, dvbt.T, dkbct.T, dvbct.T
    dsink = jnp.sum(dsk[:, :, 0, 0], axis=1)

    dproj, dqg, dkg = _prep_bwd(dqat, dka, dva, dqbt, dkb, dvb, proj, cos, sin, qg, kg, dgl, tm=tm, name="prep_bwd")
    grad_x, dscale1, dshift1 = _mm_nt4_mod_bwd(dproj, win4, dxz1, x, scale1, tm=tm, name="in_proj_bwd")
    ns_in = win4.shape[-1]
    win_spec = dict(b_spec=pl.BlockSpec((None, None, ns_in), lambda k, t: (0, 0, k)),
                    out_shape=_sds((N_CHIPS, D_MODEL, ns_in)),
                    out_spec=pl.BlockSpec((None, D_MODEL, ns_in), lambda k, t: (k, 0, 0)),
                    colsum_spec=pl.BlockSpec((8, ns_in), lambda k, t: (0, k)), colsum_shape=_sds((8, IN_COLS)))
    win_spec["b_spec"] = pl.BlockSpec((tm, ns_in), lambda k, t: (t, k))
    dw_in4, db_in = _mm_tn(x, dproj, a_spec=pl.BlockSpec((tm, D_MODEL), lambda k, t: (t, 0)), grid=(N_CHIPS, n_s),
                           mod=(shift1, scale1), name="dw_in", **win_spec)

    zq = jnp.zeros((N_HEADS, LANES, nc), F32)
    dproj_c, _, dkg_c = _prep_bwd(zq, dkac, dvac, zq, dkbc, dvbc, proj_c, cos_c, sin_c, qg, kg,
                                  jnp.zeros((nc, IN_COLS - OFF_GA), BF16), tm=nc, name="prep_bwd_ctx")
    _, dscale_c, dshift_c = _mm_nt4_mod_bwd(dproj_c, win4, jnp.zeros((nc, D_MODEL), F32), ctx, scale_c, tm=nc,
                                            name="in_proj_bwd_ctx")
    win_spec["b_spec"] = pl.BlockSpec((nc, ns_in), lambda k, t: (t, k))
    dw_in4, db_in_c = _mm_tn(ctx, dproj_c, a_spec=pl.BlockSpec((nc, D_MODEL), lambda k, t: (t, 0)), grid=(N_CHIPS, 1),
                             mod=(shift_c, scale_c), init=dw_in4, name="dw_in_ctx", **win_spec)

    dmod = jnp.concatenate([dshift1, dscale1, dgate1, dshift2, dscale2, dgate2], axis=1)
    dmodc = jnp.concatenate([dshift_c, dscale_c], axis=1)
    dmodc_pad = jnp.concatenate([dmodc, jnp.zeros((1, 4 * D_MODEL), F32)], axis=1)
    dmodc8 = _first_row(dmodc_pad).astype(BF16)
    z8 = jnp.zeros((8, D_MODEL), F32)
    dsilu_c, _, _ = _mm_nt4_mod_bwd(dmodc8, wmod4, z8, z8, zvec, tm=8, name="c_ctx_bwd")
    sg = _sigmoid(c_ctx)
    dc_ctx = dsilu_c[0:1] * sg * (1.0 + c_ctx * (1.0 - sg))

    dqn = jnp.sum(dqg.reshape(N_HEADS, HEAD_DIM), axis=0, keepdims=True)
    dkn = jnp.sum((dkg + dkg_c).reshape(N_KV, HEAD_DIM), axis=0, keepdims=True)
    grads = dict(
        w_in4=dw_in4, b_in=db_in[0:1] + db_in_c[0:1], sink=dsink, qn=dqn, kn=dkn, wba=dwba, wbb=dwbb, w_out=dw_out,
        ln1_g=dln1_g, ln1_b=dln1_b, w_up4=dw_up4, conv_w=dcw2.transpose(1, 0, 2).reshape(3, 2 * D_FF),
        conv_b=dcb2.reshape(1, 2 * D_FF), w_down=dw_down, ln2_g=dln2_g, ln2_b=dln2_b,
        c_ctx=dc_ctx, dmod=dmod, dmodc=dmodc)
    return loss, grad_x, grads


ANY = pl.BlockSpec(memory_space=pl.ANY)


def _mesh_pos():
    return lax.axis_index("x"), lax.axis_index("y"), lax.axis_index("c")


def _other_chips(x, y):
    return [(1 - x, y), (x, 1 - y), (1 - x, 1 - y)]


def _remote(src, dst, send, recv, dev):
    return pltpu.make_async_remote_copy(src_ref=src, dst_ref=dst, send_sem=send, recv_sem=recv, device_id=dev,
                                        device_id_type=MESH)


def _set_block(stack, block, k):
    return lax.dynamic_update_slice(stack, block[None], (k,) + (0,) * block.ndim)


def _gather_shards(arrs, small):
    na = len(arrs)
    halves = [a.shape[0] // 2 for a in arrs]

    def body(*refs):
        ins, small_ref = refs[:na], refs[na]
        outs, small_out = refs[na + 1:2 * na + 1], refs[2 * na + 1]
        send, recv = refs[2 * na + 2:]
        x, y, c = _mesh_pos()
        me = 2 * x + y
        chips = _other_chips(x, y)

        def half(a, cc):
            return pl.ds(cc * halves[a], halves[a])

        sends = []
        for j, chip in enumerate(chips):
            for a in range(na):
                sends.append(_remote(ins[a].at[half(a, c)], outs[a].at[me, half(a, c)], send.at[a, j], recv.at[a, j],
                                     (*chip, c)))
            sends.append(_remote(small_ref, small_out.at[me], send.at[na, j], recv.at[na, j], (*chip, c)))
        for cp in sends:
            cp.start()
        for j, chip in enumerate(chips):
            kj = 2 * chip[0] + chip[1]
            for a in range(na):
                landed = outs[a].at[kj, half(a, c)]
                _remote(landed, landed, send.at[a, j], recv.at[a, j], (*chip, c)).wait_recv()
                fwd = _remote(landed, landed, send.at[a, 3 + j], recv.at[a, 3 + j], (x, y, 1 - c))
                fwd.start()
                sends.append(fwd)
            _remote(small_ref, small_out.at[kj], send.at[na, j], recv.at[na, j], (*chip, c)).wait_recv()
        for j, chip in enumerate(chips):
            kj = 2 * chip[0] + chip[1]
            for a in range(na):
                other = outs[a].at[kj, half(a, 1 - c)]
                _remote(other, other, send.at[a, 3 + j], recv.at[a, 3 + j], (x, y, 1 - c)).wait_recv()
        for cp in sends:
            cp.wait_send()

    out_shape = [_sds((N_CHIPS,) + a.shape, a.dtype) for a in arrs] + [_sds((N_CHIPS,) + small.shape, small.dtype)]
    got = pl.pallas_call(
        body, name="gather_shards", in_specs=[ANY] * (na + 1), out_specs=[ANY] * (na + 1), out_shape=out_shape,
        scratch_shapes=[pltpu.SemaphoreType.DMA((na + 1, 6)), pltpu.SemaphoreType.DMA((na + 1, 6))],
    )(*arrs, small)
    xp, yp, _ = _mesh_pos()
    return [_set_block(g, a, 2 * xp + yp) for g, a in zip(got, list(arrs) + [small])]


def _allgather_rows(v):
    r, n = v.shape

    def body(v_ref, out_ref, send, recv, loc):
        x, y, c = _mesh_pos()
        me, sibling = (x, y, c), (x, y, 1 - c)
        chips = _other_chips(x, y)

        def rows(px, py, pc):
            return out_ref.at[4 * px + 2 * py + pc]

        def copy(k, block, to, src=None):
            return _remote(rows(*block) if src is None else src, rows(*block), send.at[k], recv.at[k], to)

        mine = pltpu.make_async_copy(v_ref, rows(*me), loc)
        mine.start()
        first = [copy(0, me, sibling, src=v_ref)] + [copy(1 + j, me, (*chip, c), src=v_ref) for j, chip in enumerate(chips)]
        for cp in first:
            cp.start()
        passed = [copy(4 + j, (*chip, c), sibling) for j, chip in enumerate(chips)]
        for j, chip in enumerate(chips):
            copy(1 + j, (*chip, c), me).wait_recv()
            passed[j].start()
        copy(0, sibling, me).wait_recv()
        for j, chip in enumerate(chips):
            copy(4 + j, (*chip, 1 - c), me).wait_recv()
        for cp in first + passed:
            cp.wait_send()
        mine.wait()

    return pl.pallas_call(
        body, name="allgather_rows", in_specs=[pl.BlockSpec(memory_space=pltpu.VMEM)],
        out_specs=pl.BlockSpec(memory_space=pltpu.VMEM), out_shape=_sds((N_DEV, r, n), v.dtype),
        scratch_shapes=[pltpu.SemaphoreType.DMA((7,)), pltpu.SemaphoreType.DMA((7,)), pltpu.SemaphoreType.DMA],
    )(v)


def _swap_other_half(g):
    nb, r, n = g.shape
    rh = r // 2

    def body(g_ref, out_ref, send, recv):
        x, y, c = _mesh_pos()
        cp = _remote(g_ref.at[:, pl.ds((1 - c) * rh, rh), :], out_ref, send, recv, (x, y, 1 - c))
        cp.start()
        cp.wait()

    return pl.pallas_call(
        body, name="swap_other_half", in_specs=[ANY], out_specs=ANY, out_shape=_sds((nb, rh, n), g.dtype),
        scratch_shapes=[pltpu.SemaphoreType.DMA, pltpu.SemaphoreType.DMA],
    )(g)


def _scatter_to_chips(p):
    def body(p_ref, out_ref, send, recv):
        x, y, c = _mesh_pos()
        me = 2 * x + y
        chips = _other_chips(x, y)
        sends = [_remote(p_ref.at[2 * chip[0] + chip[1]], out_ref.at[me], send.at[j], recv.at[j], (*chip, c))
                 for j, chip in enumerate(chips)]
        for cp in sends:
            cp.start()
        for j, chip in enumerate(chips):
            kj = 2 * chip[0] + chip[1]
            _remote(p_ref.at[kj], out_ref.at[kj], send.at[j], recv.at[j], (*chip, c)).wait_recv()
        for cp in sends:
            cp.wait_send()

    got = pl.pallas_call(
        body, name="scatter_to_chips", in_specs=[ANY], out_specs=ANY, out_shape=_sds(p.shape, p.dtype),
        scratch_shapes=[pltpu.SemaphoreType.DMA((3,)), pltpu.SemaphoreType.DMA((3,))],
    )(p)
    xp, yp, _ = _mesh_pos()
    me = 2 * xp + yp
    return _set_block(got, lax.dynamic_index_in_dim(p, me, axis=0, keepdims=False), me)


def _join_halves(f):
    def body(f_ref, out_ref, send, recv):
        x, y, c = _mesh_pos()
        cp = _remote(f_ref, out_ref, send, recv, (x, y, 1 - c))
        cp.start()
        cp.wait()

    other = pl.pallas_call(
        body, name="join_halves", in_specs=[ANY], out_specs=ANY, out_shape=_sds(f.shape, f.dtype),
        scratch_shapes=[pltpu.SemaphoreType.DMA, pltpu.SemaphoreType.DMA],
    )(f)
    first = lax.axis_index("c") == 0
    return jnp.concatenate([jnp.where(first, f, other), jnp.where(first, other, f)], axis=0)


def _row_tile(rows, cap=512):
    t = cap - cap % 8
    while rows % t:
        t -= 8
    return t


def _add_blocks(a, b, out_dtype):
    nb, r, n = a.shape
    tr = _row_tile(r)

    def body(a_ref, b_ref, o_ref):
        o_ref[...] = (a_ref[...] + b_ref[...]).astype(out_dtype)

    spec = pl.BlockSpec((None, tr, n), lambda k, i: (k, i, 0))
    return _call(body, name="add_blocks", grid=(nb, r // tr), in_specs=[spec, spec], out_specs=spec,
                 out_shape=_sds(a.shape, out_dtype), sem=("parallel", "parallel"))(a, b)


def _sum_leading(a, *, name):
    nk, r, n = a.shape
    tr = _row_tile(r)

    def body(a_ref, o_ref):
        acc = a_ref[0].astype(F32)
        for k in range(1, nk):
            acc = acc + a_ref[k].astype(F32)
        o_ref[...] = acc

    return _call(body, name=name, grid=(r // tr,), in_specs=[pl.BlockSpec((nk, tr, n), lambda i: (0, i, 0))],
                 out_specs=pl.BlockSpec((tr, n), lambda i: (i, 0)), out_shape=_sds((r, n)), sem=("parallel",))(a)


def _silu_outer(a, b):
    kdim, n = a.shape[1], b.shape[1]

    def body(a_ref, b_ref, o_ref):
        av = a_ref[...]
        av = av * _sigmoid(av)
        bv = b_ref[...]
        ah, bh = av.astype(BF16), bv.astype(BF16)
        al, bl = (av - ah.astype(F32)).astype(BF16), (bv - bh.astype(F32)).astype(BF16)
        o_ref[...] = _dot_tn(ah, bh) + (_dot_tn(ah, bl) + _dot_tn(al, bh))

    return _call(body, name="dw_mod", grid=(1,), in_specs=[_full(a.shape), _full(b.shape)], out_specs=_full((kdim, n)),
                 out_shape=_sds((kdim, n)))(a, b)


def _adamw(w, g, m, v):
    r, n = w.shape
    tr = _row_tile(r)

    def body(w_ref, g_ref, m_ref, v_ref, d_ref, nm_ref, nv_ref):
        gv = g_ref[...]
        nm = ADAM_B1 * m_ref[...] + (1.0 - ADAM_B1) * gv
        nv = ADAM_B2 * v_ref[...] + (1.0 - ADAM_B2) * (gv * gv)
        m_hat = nm / (1.0 - ADAM_B1 ** ADAM_STEP)
        v_hat = nv / (1.0 - ADAM_B2 ** ADAM_STEP)
        d_ref[...] = -ADAM_LR * (m_hat / (jnp.sqrt(v_hat) + ADAM_EPS) + ADAM_WD * w_ref[...])
        nm_ref[...] = nm
        nv_ref[...] = nv

    spec = pl.BlockSpec((tr, n), lambda i: (i, 0))
    return _call(body, name="adamw", grid=(r // tr,), in_specs=[spec] * 4, out_specs=[spec] * 3,
                 out_shape=[_sds((r, n))] * 3, sem=("parallel",))(w, g, m, v)


BIG = ("w_in", "w_branch_a", "w_branch_b", "w_out", "w_up", "w_down", "conv_w")
BIG_ROWS = 3584
MATRICES = ("w_mod", "w_in", "w_branch_a", "w_branch_b", "w_out", "w_up", "w_down")
SMALL = ("b_mod", "b_in", "conv_b", "ln1_g", "ln1_b", "ln2_g", "ln2_b", "c_ctx", "attn_sink", "q_norm_g", "k_norm_g", "conv_w")
SMALL_ROWS = 8 * len(SMALL)


def _rows(a, n_rows):
    flat = a.reshape(-1)
    return jnp.pad(flat, (0, n_rows * D_MODEL - flat.shape[0])).reshape(n_rows, D_MODEL)


def _group8(a):
    return _rep8(_rows(a, 1)) if a.size <= D_MODEL else _rows(a, 8)


def _ungroup8(p, shape):
    size = math.prod(shape)
    return (p[0, :size] if size <= D_MODEL else p.reshape(-1)[:size]).reshape(shape)


def _unpack_big(p, like):
    out, r = {}, 0
    for n in BIG:
        size = math.prod(like[n].shape)
        nr = size // D_MODEL if n != "conv_w" else 8
        out[n] = p[r:r + nr].reshape(-1)[:size].reshape(like[n].shape)
        r += nr
    return out


def _pack_small(t):
    return jnp.concatenate([_group8(t[n]) for n in SMALL], axis=0)


def _unpack_small(p, like):
    return {n: _ungroup8(p[8 * i:8 * i + 8], like[n].shape) for i, n in enumerate(SMALL)}


WEIGHTS = ("c_ctx", "w_mod", "b_mod", "w_in", "b_in", "attn_sink", "q_norm_g", "k_norm_g", "w_branch_a", "w_branch_b",
           "w_out", "ln1_g", "ln1_b", "w_up", "conv_w", "conv_b", "w_down", "ln2_g", "ln2_b")


def kernel(x, c, ctx, c_ctx, w_mod, b_mod, w_in, b_in, attn_sink, q_norm_g, k_norm_g, w_branch_a, w_branch_b, w_out, ln1_g, ln1_b, w_up, conv_w, conv_b, w_down, ln2_g, ln2_b, loss_target, m_c_ctx, m_w_mod, m_b_mod, m_w_in, m_b_in, m_attn_sink, m_q_norm_g, m_k_norm_g, m_w_branch_a, m_w_branch_b, m_w_out, m_ln1_g, m_ln1_b, m_w_up, m_conv_w, m_conv_b, m_w_down, m_ln2_g, m_ln2_b, v_c_ctx, v_w_mod, v_b_mod, v_w_in, v_b_in, v_attn_sink, v_q_norm_g, v_k_norm_g, v_w_branch_a, v_w_branch_b, v_w_out, v_ln1_g, v_ln1_b, v_w_up, v_conv_w, v_conv_b, v_w_down, v_ln2_g, v_ln2_b):
    w = dict(c_ctx=c_ctx, w_mod=w_mod, b_mod=b_mod, w_in=w_in, b_in=b_in, attn_sink=attn_sink, q_norm_g=q_norm_g,
             k_norm_g=k_norm_g, w_branch_a=w_branch_a, w_branch_b=w_branch_b, w_out=w_out, ln1_g=ln1_g, ln1_b=ln1_b,
             w_up=w_up, conv_w=conv_w, conv_b=conv_b, w_down=w_down, ln2_g=ln2_g, ln2_b=ln2_b)
    m = dict(c_ctx=m_c_ctx, w_mod=m_w_mod, b_mod=m_b_mod, w_in=m_w_in, b_in=m_b_in, attn_sink=m_attn_sink,
             q_norm_g=m_q_norm_g, k_norm_g=m_k_norm_g, w_branch_a=m_w_branch_a, w_branch_b=m_w_branch_b, w_out=m_w_out,
             ln1_g=m_ln1_g, ln1_b=m_ln1_b, w_up=m_w_up, conv_w=m_conv_w, conv_b=m_conv_b, w_down=m_w_down,
             ln2_g=m_ln2_g, ln2_b=m_ln2_b)
    v = dict(c_ctx=v_c_ctx, w_mod=v_w_mod, b_mod=v_b_mod, w_in=v_w_in, b_in=v_b_in, attn_sink=v_attn_sink,
             q_norm_g=v_q_norm_g, k_norm_g=v_k_norm_g, w_branch_a=v_w_branch_a, w_branch_b=v_w_branch_b, w_out=v_w_out,
             ln1_g=v_ln1_g, ln1_b=v_ln1_b, w_up=v_w_up, conv_w=v_conv_w, conv_b=v_conv_b, w_down=v_w_down,
             ln2_g=v_ln2_g, ln2_b=v_ln2_b)
    xp, yp, _ = _mesh_pos()
    me = 2 * xp + yp

    branches = jnp.concatenate([w_branch_a[0], w_branch_b[0]], axis=0)
    wide = jnp.concatenate([w_mod[0], w_in[0], w_up[0], branches], axis=1).astype(BF16)
    tall = jnp.concatenate([w_out[0], w_down[0]], axis=0).astype(BF16)
    wide4, tall4, cw4 = _gather_shards([wide, tall], conv_w[0])
    n_mod, n_in, n_up = w_mod.shape[-1], w_in.shape[-1], w_up.shape[-1]
    wmod4 = wide4[:, :, :n_mod]
    win4 = wide4[:, :, n_mod:n_mod + n_in]
    wup4 = wide4[:, :, n_mod + n_in:n_mod + n_in + n_up]
    br4 = wide4[:, :, n_mod + n_in + n_up:]
    n_br = w_branch_a.shape[1]
    wba = br4[:, :n_br].transpose(1, 0, 2).reshape(n_br, D_MODEL)
    wbb = br4[:, n_br:].transpose(1, 0, 2).reshape(n_br, D_MODEL)
    n_out = w_out.shape[1]
    w_out_full = tall4[:, :n_out].reshape(D_MODEL, D_MODEL)
    w_down_full = tall4[:, n_out:].reshape(D_FF, D_MODEL)
    cw_full = cw4.transpose(1, 0, 2).reshape(3, 2 * D_FF)

    loss, grad_x, g = _local_step(
        x[0], c, ctx[0], c_ctx[None], wmod4, b_mod, win4, b_in, attn_sink[0], q_norm_g, k_norm_g, wba, wbb, w_out_full,
        ln1_g, ln1_b, wup4, cw_full, conv_b, w_down_full, ln2_g, ln2_b, loss_target[0])
    loss = lax.psum(loss, ("x", "y", "c"))

    sent = dict(c=c, dmod=g["dmod"], dmodc=g["dmodc"], b_in=g["b_in"], conv_b=g["conv_b"], ln1_g=g["ln1_g"],
                ln1_b=g["ln1_b"], ln2_g=g["ln2_g"], ln2_b=g["ln2_b"], c_ctx=g["c_ctx"], attn_sink=g["sink"],
                q_norm_g=g["qn"], k_norm_g=g["kn"])
    every = _allgather_rows(jnp.concatenate([_group8(a) for a in sent.values()], axis=0))
    total = _sum_leading(every, name="sum_devices")
    slot = {n: slice(8 * i, 8 * i + 8) for i, n in enumerate(sent)}
    gs = {n: _ungroup8(total[slot[n]], sent[n].shape) for n in SMALL if n in sent}
    dmodc_sum = jnp.concatenate([_ungroup8(total[slot["dmodc"]], (1, 2 * D_MODEL)), jnp.zeros((1, 4 * D_MODEL), F32)],
                                axis=1)
    gs["b_mod"] = _ungroup8(total[slot["dmod"]], b_mod.shape) + dmodc_sum
    acts = jnp.concatenate([every[:, slot["c"].start], _rep8(c_ctx)], axis=0)
    dmods = jnp.concatenate([every[:, slot["dmod"]].reshape(N_DEV, -1)[:, :6 * D_MODEL], _first_row(dmodc_sum)], axis=0)
    g_w_mod = _silu_outer(acts, lax.dynamic_slice_in_dim(dmods, me * n_mod, n_mod, axis=1))

    cw_g4 = _to_blocks4(g["conv_w"])
    packed = jnp.concatenate([
        g["w_in4"].reshape(N_CHIPS, -1, D_MODEL), _to_blocks4(g["wba"]).reshape(N_CHIPS, -1, D_MODEL),
        _to_blocks4(g["wbb"]).reshape(N_CHIPS, -1, D_MODEL), g["w_out"].reshape(N_CHIPS, -1, D_MODEL),
        g["w_up4"].reshape(N_CHIPS, -1, D_MODEL), g["w_down"].reshape(N_CHIPS, -1, D_MODEL),
        jnp.pad(cw_g4.reshape(N_CHIPS, -1), ((0, 0), (0, 8 * D_MODEL - cw_g4.shape[1] * cw_g4.shape[2]))).reshape(
            N_CHIPS, 8, D_MODEL),
        jnp.zeros((N_CHIPS, BIG_ROWS - 3528, D_MODEL), F32)], axis=1)
    rh = BIG_ROWS // 2
    cpos = lax.axis_index("c")
    my_half = lax.dynamic_slice_in_dim(packed, cpos * rh, rh, axis=1)
    chip_sum = _add_blocks(my_half, _swap_other_half(packed), BF16)
    half_sum = _sum_leading(_scatter_to_chips(chip_sum), name="sum_chips")
    g_big = _unpack_big(_join_halves(half_sum), w)

    grads = dict(gs, w_mod=g_w_mod, **g_big)
    grads = {n: grads[n].reshape(w[n].shape) for n in WEIGHTS}
    delta, new_m, new_v = {}, {}, {}
    for n in MATRICES:
        outs = _adamw(*[t[n][0] for t in (w, grads, m, v)])
        delta[n], new_m[n], new_v[n] = [o[None] for o in outs]
    outs = _adamw(*[_pack_small(t) for t in (w, grads, m, v)])
    for res, o in zip((delta, new_m, new_v), outs):
        res.update(_unpack_small(o, w))
    return (loss, grad_x[None], *[grads[n] for n in WEIGHTS], *[delta[n] for n in WEIGHTS],
            *[new_m[n] for n in WEIGHTS], *[new_v[n] for n in WEIGHTS])
```

```python
import functools
import math

import jax
import jax.numpy as jnp
from jax import lax
from jax.experimental import pallas as pl
from jax.experimental.pallas import tpu as pltpu

F32 = jnp.float32
BF16 = jnp.bfloat16

D_MODEL = 1024
HEAD_DIM = 64
N_HEADS = 8
N_KV = 2
WINDOW = 128
GRID_W = 64
ROPE_THETA = 10000.0
D_FF = 2816
LN_EPS = 1e-5
QK_EPS = 1e-6
ALPHA = 2.0 ** 0.25
Q_SCALE = HEAD_DIM ** -0.5
OFF_GA = 1536
IN_COLS = 3584
ADAM_LR, ADAM_B1, ADAM_B2, ADAM_EPS, ADAM_WD, ADAM_STEP = 0.001, 0.9, 0.999, 1e-8, 0.01, 10

LANES = 128
VMEM_BUDGET = 52 * 1024 * 1024
N_CHIPS = 4
N_DEV = 8
NEG = -1e30
MESH = pl.DeviceIdType.MESH


def _sigmoid(x):
    return 1.0 / (1.0 + jnp.exp(-x))


def _dot(a, b):
    return jnp.dot(a, b, preferred_element_type=F32)


def _dot_nt(a, b):
    return lax.dot_general(a, b, (((1,), (1,)), ((), ())), preferred_element_type=F32)


def _dot_tn(a, b):
    return lax.dot_general(a, b, (((0,), (0,)), ((), ())), preferred_element_type=F32)


def _call(body, *, name, grid, in_specs, out_specs, out_shape, scratch=(), sem=None, **kw):
    params = dict(vmem_limit_bytes=VMEM_BUDGET)
    if sem is not None:
        params["dimension_semantics"] = sem
    return pl.pallas_call(body, name=name, grid=grid, in_specs=in_specs, out_specs=out_specs,
                          out_shape=out_shape, scratch_shapes=list(scratch),
                          compiler_params=pltpu.CompilerParams(**params), **kw)


def _full(shape):
    n = len(shape)
    return pl.BlockSpec(shape, lambda *_: (0,) * n)


def _sds(shape, dtype=F32):
    return jax.ShapeDtypeStruct(shape, dtype)


def _mm_nn4(a, shift, scale, w4, bias, *, mode, split_out, out_dtype, tm, name):
    m, kdim = a.shape
    nb, _, ns = w4.shape

    def body(a_ref, sh_ref, sc_ref, w_ref, b_ref, o_ref):
        av = a_ref[...]
        if mode == "modulate":
            av = av * (1.0 + sc_ref[...]) + sh_ref[...]
        else:
            av = av * _sigmoid(av)
        o_ref[...] = (_dot(av.astype(BF16), w_ref[...]) + b_ref[...]).astype(out_dtype)

    if split_out:
        out_shape = _sds((2, m, 2 * ns), out_dtype)
        out_spec = pl.BlockSpec((None, tm, ns), lambda i, k: (k // 2, i, k % 2))
    else:
        out_shape = _sds((m, nb * ns), out_dtype)
        out_spec = pl.BlockSpec((tm, ns), lambda i, k: (i, k))
    return _call(
        body, name=name, grid=(m // tm, nb),
        in_specs=[pl.BlockSpec((tm, kdim), lambda i, k: (i, 0)),
                  pl.BlockSpec((1, kdim), lambda i, k: (0, 0)),
                  pl.BlockSpec((1, kdim), lambda i, k: (0, 0)),
                  pl.BlockSpec((None, kdim, ns), lambda i, k: (k, 0, 0)),
                  pl.BlockSpec((1, ns), lambda i, k: (0, k))],
        out_specs=out_spec, out_shape=out_shape, sem=("parallel", "arbitrary"),
    )(a, shift, scale, w4, bias)


def _mm_tn(a, b, *, a_spec, b_spec, grid, out_shape, out_spec, name, mod=None, init=None, colsum_spec=None,
           colsum_shape=None):
    red = len(grid) - 1
    has_mod, has_init, has_cs = mod is not None, init is not None, colsum_spec is not None

    def body(*refs):
        refs = list(refs)
        a_ref, b_ref = refs[0], refs[1]
        pos = 2
        if has_mod:
            sh_ref, sc_ref = refs[2], refs[3]
            pos = 4
        if has_init:
            init_ref = refs[pos]
            pos += 1
        o_ref = refs[pos]
        cs_ref = refs[pos + 1] if has_cs else None
        s = pl.program_id(red)

        @pl.when(s == 0)
        def _():
            o_ref[...] = init_ref[...] if has_init else jnp.zeros(o_ref.shape, F32)
            if has_cs:
                cs_ref[...] = jnp.zeros(cs_ref.shape, F32)

        av = a_ref[...]
        if has_mod:
            av = av * (1.0 + sc_ref[...]) + sh_ref[...]
        bv = b_ref[...]
        o_ref[...] += _dot_tn(av.astype(BF16), bv)
        if has_cs:
            cs_ref[...] += jnp.broadcast_to(jnp.sum(bv.astype(F32), axis=0, keepdims=True), cs_ref.shape)

    ins, in_specs = [a, b], [a_spec, b_spec]
    if has_mod:
        kdim = mod[0].shape[-1]
        ins += list(mod)
        in_specs += [_full((1, kdim)), _full((1, kdim))]
    if has_init:
        ins.append(init)
        in_specs.append(out_spec)
    out_specs, out_shapes = out_spec, out_shape
    if has_cs:
        out_specs, out_shapes = [out_spec, colsum_spec], [out_shape, colsum_shape]
    sem = ("parallel",) * red + ("arbitrary",)
    return _call(body, name=name, grid=grid, in_specs=in_specs, out_specs=out_specs, out_shape=out_shapes,
                 sem=sem)(*ins)


def _rope_tables(n_tok):
    pos = jnp.arange(n_tok, dtype=jnp.int32)
    rows = (pos // GRID_W).astype(F32)
    cols = (pos % GRID_W).astype(F32)
    n_freq = HEAD_DIM // 4
    inv_freq = ROPE_THETA ** (-jnp.arange(n_freq, dtype=F32) / n_freq)
    ang_r = rows[:, None] * inv_freq
    ang_c = cols[:, None] * inv_freq
    cos = jnp.concatenate([jnp.cos(ang_r)] * 2 + [jnp.cos(ang_c)] * 2, axis=-1)
    sin = jnp.concatenate([-jnp.sin(ang_r), jnp.sin(ang_r), -jnp.sin(ang_c), jnp.sin(ang_c)], axis=-1)
    return jnp.tile(cos, (1, 2)), jnp.tile(sin, (1, 2))


def _lane(shape):
    return lax.broadcasted_iota(jnp.int32, shape, 1)


def _rope_partner(t, lane):
    return jnp.where((lane % 32) < 16, pltpu.roll(t, LANES - 16, 1), pltpu.roll(t, 16, 1))


def _half_mean(s, lane):
    lo = jnp.sum(jnp.where(lane < HEAD_DIM, s, 0.0), axis=-1, keepdims=True)
    hi = jnp.sum(jnp.where(lane < HEAD_DIM, 0.0, s), axis=-1, keepdims=True)
    return jnp.where(lane < HEAD_DIM, lo, hi) * (1.0 / HEAD_DIM)


def _prep(proj, cos, sin, qg, kg, *, tm, name):
    m = proj.shape[0]

    def body(p_ref, cos_ref, sin_ref, qg_ref, kg_ref, qa_ref, ka_ref, va_ref, qb_ref, kb_ref, vb_ref):
        lane = _lane((tm, LANES))
        cosv, sinv = cos_ref[...], sin_ref[...]
        low = lane < HEAD_DIM

        def rope(t):
            return t * cosv + _rope_partner(t, lane) * sinv

        def rms(t, g):
            return t * lax.rsqrt(_half_mean(t * t, lane) + QK_EPS) * g

        def place(q_ref, j, chunk):
            sw = pltpu.roll(chunk, HEAD_DIM, 1)
            if j < 2:
                h0, h1 = jnp.where(low, chunk, 0.0), jnp.where(low, sw, 0.0)
            else:
                h0, h1 = jnp.where(low, 0.0, sw), jnp.where(low, 0.0, chunk)
            q_ref[2 * j] = h0.T.astype(BF16)
            q_ref[2 * j + 1] = h1.T.astype(BF16)

        for j in range(4):
            place(qa_ref, j, rope(p_ref[:, j * LANES:(j + 1) * LANES]) * Q_SCALE)
            place(qb_ref, j, rope(rms(p_ref[:, 768 + j * LANES:768 + (j + 1) * LANES], qg_ref[...])) * Q_SCALE)
        ka_ref[...] = rope(p_ref[:, 512:640]).astype(BF16)
        va_ref[...] = p_ref[:, 640:768].astype(BF16)
        kb_ref[...] = rope(rms(p_ref[:, 1280:1408], kg_ref[...])).astype(BF16)
        vb_ref[...] = p_ref[:, 1408:1536].astype(BF16)

    row = pl.BlockSpec((tm, LANES), lambda i: (i, 0))
    qspec = pl.BlockSpec((N_HEADS, LANES, tm), lambda i: (0, 0, i))
    return _call(
        body, name=name, grid=(m // tm,),
        in_specs=[pl.BlockSpec((tm, OFF_GA), lambda i: (i, 0)), row, row, _full((1, LANES)), _full((1, LANES))],
        out_specs=[qspec, row, row, qspec, row, row],
        out_shape=[_sds((N_HEADS, LANES, m), BF16), _sds((m, LANES), BF16), _sds((m, LANES), BF16),
                   _sds((N_HEADS, LANES, m), BF16), _sds((m, LANES), BF16), _sds((m, LANES), BF16)],
        sem=("parallel",),
    )(proj, cos, sin, qg, kg)


def _prep_bwd(dqa, dka, dva, dqb, dkb, dvb, proj, cos, sin, qg, kg, dgl, *, tm, name):
    m = proj.shape[0]

    def body(dqa_ref, dka_ref, dva_ref, dqb_ref, dkb_ref, dvb_ref, p_ref, cos_ref, sin_ref, qg_ref, kg_ref,
             dgl_ref, dp_ref, dqg_ref, dkg_ref):
        i = pl.program_id(0)
        lane = _lane((tm, LANES))
        cosv, sinv = cos_ref[...], sin_ref[...]
        low = lane < HEAD_DIM

        @pl.when(i == 0)
        def _():
            dqg_ref[...] = jnp.zeros(dqg_ref.shape, F32)
            dkg_ref[...] = jnp.zeros(dkg_ref.shape, F32)

        def unrope(d):
            return d * cosv - _rope_partner(d, lane) * sinv

        def unplace(dq_ref, j):
            d0, d1 = dq_ref[2 * j].T, dq_ref[2 * j + 1].T
            if j < 2:
                return jnp.where(low, d0, pltpu.roll(d1, HEAD_DIM, 1))
            return jnp.where(low, pltpu.roll(d0, HEAD_DIM, 1), d1)

        def unrms(dtn, t, g):
            r = lax.rsqrt(_half_mean(t * t, lane) + QK_EPS)
            u = dtn * g
            dt = r * u - t * (r * r * r) * _half_mean(u * t, lane)
            return dt, jnp.sum(dtn * t * r, axis=0, keepdims=True)

        for j in range(4):
            dp_ref[:, j * LANES:(j + 1) * LANES] = (unrope(unplace(dqa_ref, j)) * Q_SCALE).astype(BF16)
            c0 = 768 + j * LANES
            dt, dg = unrms(unrope(unplace(dqb_ref, j)) * Q_SCALE, p_ref[:, c0:c0 + LANES], qg_ref[...])
            dp_ref[:, c0:c0 + LANES] = dt.astype(BF16)
            dqg_ref[:, j * LANES:(j + 1) * LANES] += dg
        dp_ref[:, 512:640] = unrope(dka_ref[...]).astype(BF16)
        dp_ref[:, 640:768] = dva_ref[...].astype(BF16)
        dt, dg = unrms(unrope(dkb_ref[...]), p_ref[:, 1280:1408], kg_ref[...])
        dp_ref[:, 1280:1408] = dt.astype(BF16)
        dkg_ref[...] += dg
        dp_ref[:, 1408:1536] = dvb_ref[...].astype(BF16)
        dp_ref[:, OFF_GA:] = dgl_ref[...]

    row = pl.BlockSpec((tm, LANES), lambda i: (i, 0))
    qspec = pl.BlockSpec((N_HEADS, LANES, tm), lambda i: (0, 0, i))
    return _call(
        body, name=name, grid=(m // tm,),
        in_specs=[qspec, row, row, qspec, row, row, pl.BlockSpec((tm, OFF_GA), lambda i: (i, 0)), row, row,
                  _full((1, LANES)), _full((1, LANES)), pl.BlockSpec((tm, IN_COLS - OFF_GA), lambda i: (i, 0))],
        out_specs=[pl.BlockSpec((tm, IN_COLS), lambda i: (i, 0)), _full((1, 512)), _full((1, LANES))],
        out_shape=[_sds((m, IN_COLS), BF16), _sds((1, 512)), _sds((1, LANES))],
        sem=("arbitrary",),
    )(dqa, dka, dva, dqb, dkb, dvb, proj, cos, sin, qg, kg, dgl)


def _attn_glob_fwd(qt, k, vt, kc, vct, *, tq, tk):
    nh, _, s = qt.shape
    nc = kc.shape[0]
    n_chunks = s // tk
    half = LANES // 2

    def body(qt_ref, k_ref, vt_ref, kc_ref, vct_ref, ot_ref, lse_ref, mrun_ref, p_hbm,
             acc_sc, st_sc, stage_sc, stagec_sc, sems, semc):
        h, i = pl.program_id(0), pl.program_id(1)
        qtv = qt_ref[...]
        acc_sc[...] = jnp.zeros(acc_sc.shape, F32)

        def p_out(slot, c):
            return pltpu.make_async_copy(stage_sc.at[slot], p_hbm.at[h, i, pl.ds(pl.multiple_of(c * tk, tk), tk), :],
                                         sems.at[slot])

        def update(st, vtv, m_old):
            m_new = jnp.maximum(m_old, jnp.max(st, axis=0, keepdims=True))
            pb = jnp.exp(st - m_new).astype(BF16)
            acc_sc[...] = acc_sc[...] * jnp.exp(m_old - m_new) + _dot(vtv, pb)
            return m_new, pb

        m, pbc = update(_dot(kc_ref[...], qtv), vct_ref[...], jnp.full((1, tq), NEG, F32))
        mrun_ref[pl.ds(n_chunks, 1), :] = m
        stagec_sc[...] = pbc
        ctx_out = pltpu.make_async_copy(stagec_sc, p_hbm.at[h, i, pl.ds(s, nc), :], semc)
        ctx_out.start()

        def step(c, st, m_old):
            slot = c % 2
            off = pl.multiple_of(c * tk, tk)
            nxt = pl.multiple_of(jnp.minimum(c + 1, n_chunks - 1) * tk, tk)
            st_next = _dot(k_ref[pl.ds(nxt, tk), :], qtv)
            m_new, pb = update(st, vt_ref[:, pl.ds(off, tk)], m_old)
            mrun_ref[pl.ds(c, 1), :] = m_new
            stage_sc[slot] = pb
            p_out(slot, c).start()
            return st_next, m_new

        def loop(c, m_old):
            st_next, m_new = step(c, st_sc[...], m_old)
            p_out(1 - c % 2, c - 1).wait()
            st_sc[...] = st_next
            return m_new

        stage_sc[1] = jnp.zeros((tk, tq), BF16)
        pltpu.make_async_copy(stage_sc.at[1], p_hbm.at[h, i, pl.ds(s + nc, tk), :], sems.at[1]).start()
        st_sc[...] = _dot(k_ref[pl.ds(0, tk), :], qtv)
        m = lax.fori_loop(0, n_chunks, loop, m)
        p_out((n_chunks - 1) % 2, n_chunks - 1).wait()
        ctx_out.wait()
        acc = acc_sc[...]
        l = jnp.where(h < nh // N_KV, acc[half:half + 1], acc[0:1])
        ot_ref[...] = (acc / l).astype(BF16)
        lse_ref[...] = m + jnp.log(l)

    grp = nh // N_KV
    return _call(
        body, name="attn_glob_fwd", grid=(nh, s // tq),
        in_specs=[pl.BlockSpec((None, LANES, tq), lambda h, i: (h, 0, i)), _full((s, LANES)),
                  pl.BlockSpec((None, LANES, s), lambda h, i: (h // grp, 0, 0)), _full((nc, LANES)),
                  pl.BlockSpec((None, LANES, nc), lambda h, i: (h // grp, 0, 0))],
        out_specs=[pl.BlockSpec((None, LANES, tq), lambda h, i: (h, 0, i)),
                   pl.BlockSpec((None, 1, tq), lambda h, i: (h, 0, i)),
                   pl.BlockSpec((None, n_chunks + 1, tq), lambda h, i: (h, 0, i)), ANY],
        out_shape=[_sds((nh, LANES, s), BF16), _sds((nh, 1, s)), _sds((nh, n_chunks + 1, s)),
                   _sds((nh, s // tq, s + nc + tk, tq), BF16)],
        scratch=[pltpu.VMEM((LANES, tq), F32), pltpu.VMEM((tk, tq), F32), pltpu.VMEM((2, tk, tq), BF16),
                 pltpu.VMEM((nc, tq), BF16), pltpu.SemaphoreType.DMA((2,)), pltpu.SemaphoreType.DMA],
        sem=("parallel", "parallel"),
    )(qt, k, vt, kc, vct)


P_AHEAD = 3


def _attn_glob_bwd(qt, dot, ot, lse, mrun, p, kt, v, kct, vc, *, tq, tk):
    nh, _, s = qt.shape
    nc = vc.shape[0]
    n_q = s // tq
    n_chunks = s // tk
    n_run = mrun.shape[1] - 1
    per_run = n_chunks // n_run

    def body(qt_ref, dot_ref, ot_ref, lse_ref, mrun_ref, p_hbm, kt_ref, v_ref, kct_ref, vc_ref,
             dqt_ref, dkt_ref, dvt_ref, dkct_ref, dvct_ref, acc_sc, dp_sc, dkt_sc, dvt_sc, p_sc, pc_sc, sems, semc):
        h, i = pl.program_id(0), pl.program_id(1)

        @pl.when(jnp.logical_and(h == 0, i == 0))
        def _():
            dkct_ref[...] = jnp.zeros(dkct_ref.shape, F32)
            dvct_ref[...] = jnp.zeros(dvct_ref.shape, F32)
            dkt_sc[...] = jnp.zeros(dkt_sc.shape, F32)
            dvt_sc[...] = jnp.zeros(dvt_sc.shape, F32)


        def p_in(slot, c):
            return pltpu.make_async_copy(p_hbm.at[h, i, pl.ds(pl.multiple_of(c * tk, tk), tk), :], p_sc.at[slot],
                                         sems.at[slot])

        ctx_in = pltpu.make_async_copy(p_hbm.at[h, i, pl.ds(s, nc), :], pc_sc, semc)
        ctx_in.start()
        for c in range(P_AHEAD):
            p_in(c, min(c, n_chunks - 1)).start()
        qtv, dotv, lse = qt_ref[...], dot_ref[...], lse_ref[...]
        delta = jnp.sum(dotv.astype(F32) * ot_ref[...].astype(F32), axis=0, keepdims=True)

        def grads(pt_stored, m_row, dpt):
            pt = pt_stored.astype(F32) * jnp.exp(m_row - lse)
            return pt.astype(BF16), (pt * (dpt - delta)).astype(BF16)

        dp_sc[...] = _dot(v_ref[pl.ds(0, tk), :], dotv)
        ctx_in.wait()
        pb, dsb = grads(pc_sc[...], mrun_ref[pl.ds(n_run, 1), :], _dot(vc_ref[...], dotv))
        acc_sc[...] = _dot(kct_ref[...], dsb)
        dkct_ref[...] += _dot_nt(qtv, dsb)
        dvct_ref[...] += _dot_nt(dotv, pb)

        def loop(c, carry):
            slot = c % (P_AHEAD + 1)
            off = pl.multiple_of(c * tk, tk)
            nxt = pl.multiple_of(jnp.minimum(c + 1, n_chunks - 1) * tk, tk)
            p_in(slot, c).wait()
            p_in((c + P_AHEAD) % (P_AHEAD + 1), jnp.minimum(c + P_AHEAD, n_chunks - 1)).start()
            dpt = dp_sc[...]
            dp_next = _dot(v_ref[pl.ds(nxt, tk), :], dotv)
            pb, dsb = grads(p_sc[slot], mrun_ref[pl.ds(c // per_run, 1), :], dpt)
            acc_sc[...] += _dot(kt_ref[:, pl.ds(off, tk)], dsb)
            dkt_sc[:, pl.ds(off, tk)] += _dot_nt(qtv, dsb)
            dvt_sc[:, pl.ds(off, tk)] += _dot_nt(dotv, pb)
            dp_sc[...] = dp_next
            return carry

        lax.fori_loop(0, n_chunks, loop, 0)
        for c in range(n_chunks, n_chunks + P_AHEAD):
            p_in(c % (P_AHEAD + 1), n_chunks - 1).wait()
        dqt_ref[...] = acc_sc[...]

        @pl.when(jnp.logical_and(h == nh - 1, i == n_q - 1))
        def _():
            pltpu.sync_copy(dkt_sc, dkt_ref)
            pltpu.sync_copy(dvt_sc, dvt_ref)

    qs = pl.BlockSpec((None, LANES, tq), lambda h, i: (h, 0, i))
    rs = pl.BlockSpec((None, 1, tq), lambda h, i: (h, 0, i))
    return _call(
        body, name="attn_glob_bwd", grid=(nh, n_q),
        in_specs=[qs, qs, qs, rs, pl.BlockSpec((None, n_run + 1, tq), lambda h, i: (h, 0, i)), ANY,
                  _full((LANES, s)), _full((s, LANES)), _full((LANES, nc)), _full((nc, LANES))],
        out_specs=[qs, ANY, ANY, _full((LANES, nc)), _full((LANES, nc))],
        out_shape=[_sds((nh, LANES, s)), _sds((LANES, s)), _sds((LANES, s)), _sds((LANES, nc)), _sds((LANES, nc))],
        scratch=[pltpu.VMEM((LANES, tq), F32), pltpu.VMEM((tk, tq), F32), pltpu.VMEM((LANES, s), F32),
                 pltpu.VMEM((LANES, s), F32), pltpu.VMEM((P_AHEAD + 1, tk, tq), BF16), pltpu.VMEM((nc, tq), BF16),
                 pltpu.SemaphoreType.DMA((P_AHEAD + 1,)), pltpu.SemaphoreType.DMA],
        sem=("arbitrary", "arbitrary"),
    )(qt, dot, ot, lse, mrun, p, kt, v, kct, vc)


WIN_SPAN = 2 * WINDOW


def _band(rows0, cols0, shape):
    r = rows0 + lax.broadcasted_iota(jnp.int32, shape, 0)
    c = cols0 + lax.broadcasted_iota(jnp.int32, shape, 1)
    return jnp.abs(r - c) <= WINDOW


def _win_start(blk, t, s):
    return pl.multiple_of(jnp.clip(blk * t - WINDOW, 0, s - t - WIN_SPAN), WINDOW)


def _attn_win_fwd(qt, k, vt, kc, vct, sink, *, tq):
    nh, _, s = qt.shape
    nc = kc.shape[0]
    tw = tq + WIN_SPAN
    half = LANES // 2
    grp = nh // N_KV

    def body(sink_ref, qt_ref, k_ref, vt_ref, kc_ref, vct_ref, ot_ref, lse_ref):
        h, i = pl.program_id(0), pl.program_id(1)
        k0 = _win_start(i, tq, s)
        qtv = qt_ref[...]
        st = jnp.where(_band(k0, i * tq, (tw, tq)), _dot(k_ref[pl.ds(k0, tw), :], qtv), NEG)
        stc = _dot(kc_ref[...], qtv)
        snk = sink_ref[h]
        m = jnp.maximum(jnp.maximum(jnp.max(st, axis=0, keepdims=True), jnp.max(stc, axis=0, keepdims=True)), snk)
        acc = (_dot(vt_ref[:, pl.ds(k0, tw)], jnp.exp(st - m).astype(BF16))
               + _dot(vct_ref[...], jnp.exp(stc - m).astype(BF16)))
        l = jnp.where(h < grp, acc[half:half + 1], acc[0:1]) + jnp.exp(snk - m)
        ot_ref[...] = (acc / l).astype(BF16)
        lse_ref[...] = m + jnp.log(l)

    return _call(
        body, name="attn_win_fwd", grid=(nh, s // tq),
        in_specs=[pl.BlockSpec(memory_space=pltpu.SMEM),
                  pl.BlockSpec((None, LANES, tq), lambda h, i: (h, 0, i)), _full((s, LANES)),
                  pl.BlockSpec((None, LANES, s), lambda h, i: (h // grp, 0, 0)), _full((nc, LANES)),
                  pl.BlockSpec((None, LANES, nc), lambda h, i: (h // grp, 0, 0))],
        out_specs=[pl.BlockSpec((None, LANES, tq), lambda h, i: (h, 0, i)),
                   pl.BlockSpec((None, 1, tq), lambda h, i: (h, 0, i))],
        out_shape=[_sds((nh, LANES, s), BF16), _sds((nh, 1, s))],
        sem=("parallel", "parallel"),
    )(sink, qt, k, vt, kc, vct)


def _attn_win_bwd(qt, dot, ot, lse, k, kt, v, kc, kct, vc, sink, *, tq):
    nh, _, s = qt.shape
    nc = kc.shape[0]
    tw = tq + WIN_SPAN
    nq = s // tq

    def body(sink_ref, qt_ref, dot_ref, ot_ref, lse_ref, k_ref, kt_ref, v_ref, kc_ref, kct_ref, vc_ref,
             dqt_ref, dkt_ref, dvt_ref, dkct_ref, dvct_ref, dsk_ref, dkt_sc, dvt_sc):
        h, i = pl.program_id(0), pl.program_id(1)

        @pl.when(jnp.logical_and(h == 0, i == 0))
        def _():
            dkct_ref[...] = jnp.zeros(dkct_ref.shape, F32)
            dvct_ref[...] = jnp.zeros(dvct_ref.shape, F32)
            dkt_sc[...] = jnp.zeros(dkt_sc.shape, F32)
            dvt_sc[...] = jnp.zeros(dvt_sc.shape, F32)

        k0 = _win_start(i, tq, s)
        span = pl.ds(k0, tw)
        qtv, dotv, lse = qt_ref[...], dot_ref[...], lse_ref[...]
        delta = jnp.sum(dotv.astype(F32) * ot_ref[...].astype(F32), axis=0, keepdims=True)
        pt = jnp.where(_band(k0, i * tq, (tw, tq)), jnp.exp(_dot(k_ref[span, :], qtv) - lse), 0.0)
        dsb = (pt * (_dot(v_ref[span, :], dotv) - delta)).astype(BF16)
        pct = jnp.exp(_dot(kc_ref[...], qtv) - lse)
        dscb = (pct * (_dot(vc_ref[...], dotv) - delta)).astype(BF16)
        dqt_ref[...] = _dot(kt_ref[:, span], dsb) + _dot(kct_ref[...], dscb)
        dkt_sc[:, span] += _dot_nt(qtv, dsb)
        dvt_sc[:, span] += _dot_nt(dotv, pt.astype(BF16))
        dkct_ref[...] += _dot_nt(qtv, dscb)
        dvct_ref[...] += _dot_nt(dotv, pct.astype(BF16))
        dsk = -jnp.sum(jnp.exp(sink_ref[h] - lse) * delta)
        dsk_ref[...] = jnp.full(dsk_ref.shape, dsk, F32)

        @pl.when(jnp.logical_and(h == nh - 1, i == nq - 1))
        def _():
            pltpu.sync_copy(dkt_sc, dkt_ref)
            pltpu.sync_copy(dvt_sc, dvt_ref)

    qs = pl.BlockSpec((None, LANES, tq), lambda h, i: (h, 0, i))
    rs = pl.BlockSpec((None, 1, tq), lambda h, i: (h, 0, i))
    return _call(
        body, name="attn_win_bwd", grid=(nh, nq),
        in_specs=[pl.BlockSpec(memory_space=pltpu.SMEM), qs, qs, qs, rs, _full((s, LANES)), _full((LANES, s)),
                  _full((s, LANES)), _full((nc, LANES)), _full((LANES, nc)), _full((nc, LANES))],
        out_specs=[qs, ANY, ANY, _full((LANES, nc)), _full((LANES, nc)),
                   pl.BlockSpec((None, None, 8, LANES), lambda h, i: (h, i, 0, 0))],
        out_shape=[_sds((nh, LANES, s)), _sds((LANES, s)), _sds((LANES, s)), _sds((LANES, nc)), _sds((LANES, nc)),
                   _sds((nh, nq, 8, LANES))],
        scratch=[pltpu.VMEM((LANES, s), F32), pltpu.VMEM((LANES, s), F32)],
        sem=("arbitrary", "arbitrary"),
    )(sink, qt, dot, ot, lse, k, kt, v, kc, kct, vc)


def _ln_fwd(z, g, b):
    mu = jnp.mean(z, axis=-1, keepdims=True)
    zc = z - mu
    r = lax.rsqrt(jnp.mean(zc * zc, axis=-1, keepdims=True) + LN_EPS)
    return zc * r * g + b, mu, r


def _ln_bwd(dy, xhat, r, g):
    dxh = dy * g
    return r * (dxh - jnp.mean(dxh, axis=-1, keepdims=True) - xhat * jnp.mean(dxh * xhat, axis=-1, keepdims=True))


def _heads_matmul(ot_ref, w_ref):
    acc = _dot_tn(ot_ref[0], w_ref[0])
    for h in range(1, N_HEADS):
        acc += _dot_tn(ot_ref[h], w_ref[h])
    return acc


def _gate_specs(tm):
    return [pl.BlockSpec((tm, 512), functools.partial(lambda i, b: (i, b), b=OFF_GA // 512 + b)) for b in range(4)]


def _merge_fwd(oat, obt, proj, x, gate1, wba, wbb, w_out, ln_g, ln_b, *, tm):
    s = x.shape[0]

    def body(oa_ref, ob_ref, g0, g1, g2, g3, x_ref, gt_ref, wba_ref, wbb_ref, wo_ref, lg_ref, lb_ref,
             x1_ref, y_ref, mu_ref, r_ref):
        ga = _sigmoid(jnp.concatenate([g0[...], g1[...]], axis=1))
        gb = _sigmoid(jnp.concatenate([g2[...], g3[...]], axis=1))
        merged = ga * _heads_matmul(oa_ref, wba_ref) + gb * _heads_matmul(ob_ref, wbb_ref)
        y = _dot(merged.astype(BF16), wo_ref[...])
        x1, mu, r = _ln_fwd(ALPHA * x_ref[...] + gt_ref[...] * y, lg_ref[...], lb_ref[...])
        x1_ref[...] = x1
        y_ref[...] = y
        mu_ref[...] = mu
        r_ref[...] = r

    hts = pl.BlockSpec((N_HEADS, LANES, tm), lambda i: (0, 0, i))
    row = pl.BlockSpec((tm, D_MODEL), lambda i: (i, 0))
    col = pl.BlockSpec((tm, 1), lambda i: (i, 0))
    vec = _full((1, D_MODEL))
    wh = _full((N_HEADS, LANES, D_MODEL))
    return _call(
        body, name="merge_fwd", grid=(s // tm,),
        in_specs=[hts, hts, *_gate_specs(tm), row, vec, wh, wh, _full((D_MODEL, D_MODEL)), vec, vec],
        out_specs=[row, row, col, col],
        out_shape=[_sds((s, D_MODEL)), _sds((s, D_MODEL)), _sds((s, 1)), _sds((s, 1))],
        sem=("parallel",),
    )(oat, obt, proj, proj, proj, proj, x, gate1, wba, wbb, w_out, ln_g, ln_b)


def _merge_bwd(dy, oat, obt, proj, wba, wbb, w_out, *, tm):
    s = dy.shape[0]

    def body(dy_ref, oat_ref, obt_ref, g0, g1, g2, g3, wba_ref, wbb_ref, wo_ref,
             dgl_ref, doat_ref, dobt_ref, mg_ref, dwa_ref, dwb_ref):
        @pl.when(pl.program_id(0) == 0)
        def _():
            dwa_ref[...] = jnp.zeros(dwa_ref.shape, F32)
            dwb_ref[...] = jnp.zeros(dwb_ref.shape, F32)

        dm = _dot_nt(dy_ref[...], wo_ref[...])
        ga = _sigmoid(jnp.concatenate([g0[...], g1[...]], axis=1))
        gb = _sigmoid(jnp.concatenate([g2[...], g3[...]], axis=1))
        pa, pb = _heads_matmul(oat_ref, wba_ref), _heads_matmul(obt_ref, wbb_ref)
        mg_ref[...] = (ga * pa + gb * pb).astype(BF16)
        dgl_ref[:, :D_MODEL] = (dm * pa * ga * (1.0 - ga)).astype(BF16)
        dgl_ref[:, D_MODEL:] = (dm * pb * gb * (1.0 - gb)).astype(BF16)
        dpa, dpb = (dm * ga).astype(BF16), (dm * gb).astype(BF16)
        for h in range(N_HEADS):
            doat_ref[h] = _dot_nt(wba_ref[h], dpa).astype(BF16)
            dobt_ref[h] = _dot_nt(wbb_ref[h], dpb).astype(BF16)
            dwa_ref[h] += _dot(oat_ref[h], dpa)
            dwb_ref[h] += _dot(obt_ref[h], dpb)

    hts = pl.BlockSpec((N_HEADS, LANES, tm), lambda i: (0, 0, i))
    row = pl.BlockSpec((tm, D_MODEL), lambda i: (i, 0))
    wh = _full((N_HEADS, LANES, D_MODEL))
    return _call(
        body, name="merge_bwd", grid=(s // tm,),
        in_specs=[row, hts, hts, *_gate_specs(tm), wh, wh, _full((D_MODEL, D_MODEL))],
        out_specs=[pl.BlockSpec((tm, 2 * D_MODEL), lambda i: (i, 0)), hts, hts, row, wh, wh],
        out_shape=[_sds((s, 2 * D_MODEL), BF16), _sds((N_HEADS, LANES, s), BF16), _sds((N_HEADS, LANES, s), BF16),
                   _sds((s, D_MODEL), BF16), _sds((N_HEADS, LANES, D_MODEL)), _sds((N_HEADS, LANES, D_MODEL))],
        sem=("arbitrary",),
    )(dy, oat, obt, proj, proj, proj, proj, wba, wbb, w_out)


FF_TC = 256


def _shift_rows(t, prev_row, next_row):
    n = t.shape[0]
    r = lax.broadcasted_iota(jnp.int32, t.shape, 0)
    up = jnp.where(r == 0, prev_row, pltpu.roll(t, 1, 0))
    dn = jnp.where(r == n - 1, next_row, pltpu.roll(t, n - 1, 0))
    return up, dn


HALO = 16


def _halo_specs(tm, s, tc):
    nb = s // HALO
    main = pl.BlockSpec((2, tm, tc), lambda j, i: (0, i, j))
    prev = pl.BlockSpec((2, HALO, tc), lambda j, i: (0, jnp.maximum(i * (tm // HALO) - 1, 0), j))
    nxt = pl.BlockSpec((2, HALO, tc), lambda j, i: (0, jnp.minimum((i + 1) * (tm // HALO), nb - 1), j))
    return main, prev, nxt


def _halo_rows(prev_ref, next_ref, half, i, n_i):
    prev_row = jnp.where(i == 0, 0.0, prev_ref[half, HALO - 1:HALO, :].astype(F32))
    next_row = jnp.where(i == n_i - 1, 0.0, next_ref[half, 0:1, :].astype(F32))
    return prev_row, next_row


def _conv(t, prev_row, next_row, w, b):
    up, dn = _shift_rows(t, prev_row, next_row)
    return w[0:1, :] * up + w[1:2, :] * t + w[2:3, :] * dn + b


def _ffn_act_fwd(u, cw, cb, *, tm):
    _, s, ff = u.shape
    n_i = s // tm

    def body(u_ref, up_ref, un_ref, cw_ref, cb_ref, a_ref):
        i = pl.program_id(1)
        gc = _conv(u_ref[0].astype(F32), *_halo_rows(up_ref, un_ref, 0, i, n_i), cw_ref[0], cb_ref[0])
        vc = _conv(u_ref[1].astype(F32), *_halo_rows(up_ref, un_ref, 1, i, n_i), cw_ref[1], cb_ref[1])
        a_ref[...] = (gc * _sigmoid(gc) * vc).astype(BF16)

    main, prev, nxt = _halo_specs(tm, s, FF_TC)
    return _call(
        body, name="ffn_act_fwd", grid=(ff // FF_TC, n_i),
        in_specs=[main, prev, nxt, pl.BlockSpec((2, 3, FF_TC), lambda j, i: (0, 0, j)),
                  pl.BlockSpec((2, 1, FF_TC), lambda j, i: (0, 0, j))],
        out_specs=pl.BlockSpec((tm, FF_TC), lambda j, i: (i, j)),
        out_shape=_sds((s, ff), BF16), sem=("parallel", "parallel"),
    )(u, u, u, cw, cb)


def _ffn_act_bwd(dy2, w_down, u, cw, cb, *, tm):
    _, s, ff = u.shape
    n_i = s // tm

    def body(dy_ref, wd_ref, u_ref, up_ref, un_ref, cw_ref, cb_ref, dc_ref, dcw_ref, dcb_ref):
        i = pl.program_id(1)

        @pl.when(i == 0)
        def _():
            dcw_ref[...] = jnp.zeros(dcw_ref.shape, F32)
            dcb_ref[...] = jnp.zeros(dcb_ref.shape, F32)

        da = _dot_nt(dy_ref[...], wd_ref[...])
        ug, uv = u_ref[0].astype(F32), u_ref[1].astype(F32)
        ugp, ugn = _shift_rows(ug, *_halo_rows(up_ref, un_ref, 0, i, n_i))
        uvp, uvn = _shift_rows(uv, *_halo_rows(up_ref, un_ref, 1, i, n_i))
        wg, wv = cw_ref[0], cw_ref[1]
        gc = wg[0:1, :] * ugp + wg[1:2, :] * ug + wg[2:3, :] * ugn + cb_ref[0]
        vc = wv[0:1, :] * uvp + wv[1:2, :] * uv + wv[2:3, :] * uvn + cb_ref[1]
        sg = _sigmoid(gc)
        dg = da * vc * sg * (1.0 + gc * (1.0 - sg))
        dv = da * gc * sg
        dc_ref[0] = dg.astype(BF16)
        dc_ref[1] = dv.astype(BF16)
        for half, (d, taps) in enumerate(((dg, (ugp, ug, ugn)), (dv, (uvp, uv, uvn)))):
            for tap in range(3):
                dcw_ref[half, tap:tap + 1, :] += jnp.sum(d * taps[tap], axis=0, keepdims=True)
            dcb_ref[half] += jnp.sum(d, axis=0, keepdims=True)

    main, prev, nxt = _halo_specs(tm, s, FF_TC)
    return _call(
        body, name="ffn_act_bwd", grid=(ff // FF_TC, n_i),
        in_specs=[pl.BlockSpec((tm, D_MODEL), lambda j, i: (i, 0)), pl.BlockSpec((FF_TC, D_MODEL), lambda j, i: (j, 0)),
                  main, prev, nxt, pl.BlockSpec((2, 3, FF_TC), lambda j, i: (0, 0, j)),
                  pl.BlockSpec((2, 1, FF_TC), lambda j, i: (0, 0, j))],
        out_specs=[main, pl.BlockSpec((2, 3, FF_TC), lambda j, i: (0, 0, j)),
                   pl.BlockSpec((2, 1, FF_TC), lambda j, i: (0, 0, j))],
        out_shape=[_sds((2, s, ff), BF16), _sds((2, 3, ff)), _sds((2, 1, ff))],
        sem=("parallel", "arbitrary"),
    )(dy2, w_down, u, u, u, cw, cb)


def _conv_bwd_input(dc, cw, *, tm):
    _, s, ff = dc.shape
    n_i = s // tm

    def body(d_ref, dp_ref, dn_ref, cw_ref, du_ref):
        i = pl.program_id(1)
        for half in range(2):
            d = d_ref[half].astype(F32)
            up, dn = _shift_rows(d, *_halo_rows(dp_ref, dn_ref, half, i, n_i))
            w = cw_ref[half]
            du_ref[half] = (w[0:1, :] * dn + w[1:2, :] * d + w[2:3, :] * up).astype(BF16)

    main, prev, nxt = _halo_specs(tm, s, FF_TC)
    return _call(
        body, name="conv_bwd_input", grid=(ff // FF_TC, n_i),
        in_specs=[main, prev, nxt, pl.BlockSpec((2, 3, FF_TC), lambda j, i: (0, 0, j))],
        out_specs=main, out_shape=_sds((2, s, ff), BF16), sem=("parallel", "parallel"),
    )(dc, dc, dc, cw)


def _ffn_down_loss(a, w_down, x1, target, gate2, ln_g, ln_b, *, tm):
    s, ff = a.shape
    n_i = s // tm

    def body(a_ref, wd_ref, x1_ref, tg_ref, gt_ref, lg_ref, lb_ref, ls_ref, dy_ref, dx_ref, dg_ref, db_ref, dgt_ref):
        @pl.when(pl.program_id(0) == 0)
        def _():
            dg_ref[...] = jnp.zeros(dg_ref.shape, F32)
            db_ref[...] = jnp.zeros(db_ref.shape, F32)
            dgt_ref[...] = jnp.zeros(dgt_ref.shape, F32)

        y2 = _dot(a_ref[...], wd_ref[...])
        z = ALPHA * x1_ref[...] + gt_ref[...] * y2
        mu = jnp.mean(z, axis=-1, keepdims=True)
        zc = z - mu
        r = lax.rsqrt(jnp.mean(zc * zc, axis=-1, keepdims=True) + LN_EPS)
        xhat = zc * r
        diff = xhat * lg_ref[...] + lb_ref[...] - tg_ref[...]
        ls_ref[...] = jnp.full(ls_ref.shape, 0.5 / D_MODEL * jnp.sum(diff * diff), F32)
        dx2 = diff * (1.0 / D_MODEL)
        dg_ref[...] += jnp.sum(dx2 * xhat, axis=0, keepdims=True)
        db_ref[...] += jnp.sum(dx2, axis=0, keepdims=True)
        dz = _ln_bwd(dx2, xhat, r, lg_ref[...])
        dgt_ref[...] += jnp.sum(dz * y2, axis=0, keepdims=True)
        dy_ref[...] = (gt_ref[...] * dz).astype(BF16)
        dx_ref[...] = ALPHA * dz

    row = pl.BlockSpec((tm, D_MODEL), lambda i: (i, 0))
    vec = _full((1, D_MODEL))
    return _call(
        body, name="ffn_down_loss", grid=(n_i,),
        in_specs=[pl.BlockSpec((tm, ff), lambda i: (i, 0)), _full((ff, D_MODEL)), row, row, vec, vec, vec],
        out_specs=[pl.BlockSpec((None, 8, LANES), lambda i: (i, 0, 0)), row, row, vec, vec, vec],
        out_shape=[_sds((n_i, 8, LANES)), _sds((s, D_MODEL), BF16), _sds((s, D_MODEL)),
                   _sds((1, D_MODEL)), _sds((1, D_MODEL)), _sds((1, D_MODEL))],
        sem=("arbitrary",),
    )(a, w_down, x1, target, gate2, ln_g, ln_b)


def _ffn_up_bwd(du, wup4, dx1a, x1, scale2, x, y, mu1, r1, gate1, ln_g, *, tm):
    s = x.shape[0]
    nb, _, ns = wup4.shape

    def body(du_ref, w_ref, dxa_ref, x1_ref, sc_ref, x_ref, y_ref, mu_ref, r_ref, gt_ref, lg_ref,
             dxo_ref, dy_ref, dsc_ref, dsh_ref, dg_ref, db_ref, dgt_ref, acc):
        i, k = pl.program_id(0), pl.program_id(1)

        @pl.when(jnp.logical_and(i == 0, k == 0))
        def _():
            for ref in (dsc_ref, dsh_ref, dg_ref, db_ref, dgt_ref):
                ref[...] = jnp.zeros(ref.shape, F32)

        @pl.when(k == 0)
        def _():
            acc[...] = jnp.zeros(acc.shape, F32)

        acc[...] += _dot_nt(du_ref[...], w_ref[...])

        @pl.when(k == nb - 1)
        def _():
            dh = acc[...]
            x1 = x1_ref[...]
            dsc_ref[...] += jnp.sum(dh * x1, axis=0, keepdims=True)
            dsh_ref[...] += jnp.sum(dh, axis=0, keepdims=True)
            dx1 = dxa_ref[...] + dh * (1.0 + sc_ref[...])
            yv = y_ref[...]
            xhat = (ALPHA * x_ref[...] + gt_ref[...] * yv - mu_ref[...]) * r_ref[...]
            dg_ref[...] += jnp.sum(dx1 * xhat, axis=0, keepdims=True)
            db_ref[...] += jnp.sum(dx1, axis=0, keepdims=True)
            dz = _ln_bwd(dx1, xhat, r_ref[...], lg_ref[...])
            dgt_ref[...] += jnp.sum(dz * yv, axis=0, keepdims=True)
            dy_ref[...] = (gt_ref[...] * dz).astype(BF16)
            dxo_ref[...] = ALPHA * dz

    row = pl.BlockSpec((tm, D_MODEL), lambda i, k: (i, 0))
    col = pl.BlockSpec((tm, 1), lambda i, k: (i, 0))
    vec = _full((1, D_MODEL))
    return _call(
        body, name="ffn_up_bwd", grid=(s // tm, nb),
        in_specs=[pl.BlockSpec((None, tm, ns), lambda i, k: (k // 2, i, k % 2)),
                  pl.BlockSpec((None, D_MODEL, ns), lambda i, k: (k, 0, 0)),
                  row, row, vec, row, row, col, col, vec, vec],
        out_specs=[row, row, vec, vec, vec, vec, vec],
        out_shape=[_sds((s, D_MODEL)), _sds((s, D_MODEL), BF16)] + [_sds((1, D_MODEL))] * 5,
        scratch=[pltpu.VMEM((tm, D_MODEL), F32)],
        sem=("arbitrary", "arbitrary"),
    )(du, wup4, dx1a, x1, scale2, x, y, mu1, r1, gate1, ln_g)


def _mm_nt4_mod_bwd(dp, w4, dxa, x, scale, *, tm, name):
    m = x.shape[0]
    nb, kdim, ns = w4.shape

    def body(dp_ref, w_ref, dxa_ref, x_ref, sc_ref, dx_ref, dsc_ref, dsh_ref, acc):
        i, k = pl.program_id(0), pl.program_id(1)

        @pl.when(jnp.logical_and(i == 0, k == 0))
        def _():
            dsc_ref[...] = jnp.zeros(dsc_ref.shape, F32)
            dsh_ref[...] = jnp.zeros(dsh_ref.shape, F32)

        @pl.when(k == 0)
        def _():
            acc[...] = jnp.zeros(acc.shape, F32)

        acc[...] += _dot_nt(dp_ref[...], w_ref[...])

        @pl.when(k == nb - 1)
        def _():
            dh = acc[...]
            dsc_ref[...] += jnp.sum(dh * x_ref[...], axis=0, keepdims=True)
            dsh_ref[...] += jnp.sum(dh, axis=0, keepdims=True)
            dx_ref[...] = dxa_ref[...] + dh * (1.0 + sc_ref[...])

    row = pl.BlockSpec((tm, kdim), lambda i, k: (i, 0))
    vec = _full((1, kdim))
    return _call(
        body, name=name, grid=(m // tm, nb),
        in_specs=[pl.BlockSpec((tm, ns), lambda i, k: (i, k)), pl.BlockSpec((None, kdim, ns), lambda i, k: (k, 0, 0)),
                  row, row, vec],
        out_specs=[row, vec, vec],
        out_shape=[_sds((m, kdim)), _sds((1, kdim)), _sds((1, kdim))],
        scratch=[pltpu.VMEM((tm, kdim), F32)],
        sem=("arbitrary", "arbitrary"),
    )(dp, w4, dxa, x, scale)


def _pad_heads_w(w):
    w8 = w.reshape(N_HEADS, HEAD_DIM, w.shape[-1])
    z = jnp.zeros_like(w8)
    first = (jnp.arange(N_HEADS) < N_HEADS // N_KV)[:, None, None]
    return jnp.where(first, jnp.concatenate([w8, z], axis=1), jnp.concatenate([z, w8], axis=1))


def _unpad_heads_w(g):
    first = (jnp.arange(N_HEADS) < N_HEADS // N_KV)[:, None, None]
    return jnp.where(first, g[:, :HEAD_DIM], g[:, HEAD_DIM:]).reshape(N_HEADS * HEAD_DIM, g.shape[-1])


def _ones_beside(vt):
    half = vt.shape[0] // 2
    ones = jnp.ones((half, vt.shape[1]), vt.dtype)
    return jnp.stack([jnp.concatenate([vt[:half], ones], axis=0), jnp.concatenate([ones, vt[half:]], axis=0)])


def _rep8(a):
    return jnp.broadcast_to(a.reshape(1, -1), (8, a.size))


def _first_row(a):
    r8 = _rep8(a)
    return jnp.where(lax.broadcasted_iota(jnp.int32, r8.shape, 0) == 0, r8, 0.0)


def _to_blocks4(w):
    k, n = w.shape
    return w.reshape(k, N_CHIPS, n // N_CHIPS).transpose(1, 0, 2)


def _local_step(x, c, ctx, c_ctx, wmod4, b_mod, win4, b_in, sink, qn, kn, wba, wbb, w_out, ln1_g, ln1_b,
                wup4, cw, cb, w_down, ln2_g, ln2_b, target):
    s, nc = x.shape[0], ctx.shape[0]
    tm = min(512, s)
    tm2 = min(256, s)
    zvec = jnp.zeros((1, D_MODEL), F32)

    cc = jnp.concatenate([_rep8(c), _rep8(c_ctx)], axis=0)
    mods = _mm_nn4(cc, zvec, zvec, wmod4, b_mod, mode="silu", split_out=False, out_dtype=F32, tm=16, name="mod_vectors")
    shift1, scale1, gate1, shift2, scale2, gate2 = [mods[0:1, i * D_MODEL:(i + 1) * D_MODEL] for i in range(6)]
    shift_c, scale_c = mods[8:9, :D_MODEL], mods[8:9, D_MODEL:2 * D_MODEL]

    cos, sin = _rope_tables(s)
    cos_c, sin_c = jnp.ones((nc, LANES), F32), jnp.zeros((nc, LANES), F32)
    qg, kg = jnp.tile(qn, (1, 2)), jnp.tile(kn, (1, 2))

    proj_c = _mm_nn4(ctx, shift_c, scale_c, win4, b_in, mode="modulate", split_out=False, out_dtype=F32, tm=nc,
                     name="in_proj_ctx")
    _, kac, vac, _, kbc, vbc = _prep(proj_c, cos_c, sin_c, qg, kg, tm=nc, name="prep_ctx")
    proj = _mm_nn4(x, shift1, scale1, win4, b_in, mode="modulate", split_out=False, out_dtype=F32, tm=tm, name="in_proj")
    qat, ka, va, qbt, kb, vb = _prep(proj, cos, sin, qg, kg, tm=tm, name="prep")
    oat, lse_a = _attn_win_fwd(qat, ka, _ones_beside(va.T), kac, _ones_beside(vac.T), sink, tq=tm)
    obt, lse_b, mrun_b, pbt = _attn_glob_fwd(qbt, kb, _ones_beside(vb.T), kbc, _ones_beside(vbc.T), tq=tm,
                                             tk=min(1024, s))
    wba_p, wbb_p = _pad_heads_w(wba), _pad_heads_w(wbb)
    x1, y, mu1, r1 = _merge_fwd(oat, obt, proj, x, gate1, wba_p, wbb_p, w_out, ln1_g, ln1_b, tm=tm)
    u = _mm_nn4(x1, shift2, scale2, wup4, jnp.zeros((1, 2 * D_FF), F32), mode="modulate", split_out=True,
                out_dtype=BF16, tm=tm, name="ffn_up")
    cw2 = cw.reshape(3, 2, D_FF).transpose(1, 0, 2)
    cb2 = cb.reshape(2, 1, D_FF)
    a = _ffn_act_fwd(u, cw2, cb2, tm=tm)
    ls, dy2, dx1a, dln2_g, dln2_b, dgate2 = _ffn_down_loss(a, w_down, x1, target, gate2, ln2_g, ln2_b, tm=tm)
    loss = jnp.sum(ls[:, 0, 0])

    n_s = s // tm
    dw_down = _mm_tn(a, dy2, a_spec=pl.BlockSpec((tm, D_FF), lambda t: (t, 0)),
                     b_spec=pl.BlockSpec((tm, D_MODEL), lambda t: (t, 0)), grid=(n_s,),
                     out_shape=_sds((D_FF, D_MODEL)), out_spec=_full((D_FF, D_MODEL)), name="dw_down")
    dc, dcw2, dcb2 = _ffn_act_bwd(dy2, w_down, u, cw2, cb2, tm=tm)
    du = _conv_bwd_input(dc, cw2, tm=tm)
    dxz1, dy, dscale2, dshift2, dln1_g, dln1_b, dgate1 = _ffn_up_bwd(
        du, wup4, dx1a, x1, scale2, x, y, mu1, r1, gate1, ln1_g, tm=tm)
    ns_up = wup4.shape[-1]
    dw_up4 = _mm_tn(x1, du, a_spec=pl.BlockSpec((tm, D_MODEL), lambda k, t: (t, 0)),
                    b_spec=pl.BlockSpec((None, tm, ns_up), lambda k, t: (k // 2, t, k % 2)), grid=(N_CHIPS, n_s),
                    out_shape=_sds((N_CHIPS, D_MODEL, ns_up)),
                    out_spec=pl.BlockSpec((None, D_MODEL, ns_up), lambda k, t: (k, 0, 0)),
                    mod=(shift2, scale2), name="dw_up")

    dgl, doat, dobt, merged, dwba_p, dwbb_p = _merge_bwd(dy, oat, obt, proj, wba_p, wbb_p, w_out, tm=tm2)
    dwba, dwbb = _unpad_heads_w(dwba_p), _unpad_heads_w(dwbb_p)
    rowspec = pl.BlockSpec((tm, D_MODEL), lambda t: (t, 0))
    dw_out = _mm_tn(merged, dy, a_spec=rowspec, b_spec=rowspec, grid=(n_s,), out_shape=_sds((D_MODEL, D_MODEL)),
                    out_spec=_full((D_MODEL, D_MODEL)), name="dw_out")

    dqat, dkat, dvat, dkact, dvact, dsk = _attn_win_bwd(qat, doat, oat, lse_a, ka, ka.T, va, kac, kac.T, vac, sink, tq=tm)
    dka, dva, dkac, dvac = dkat.T, dvat.T, dkact.T, dvact.T
    dqbt, dkbt, dvbt, dkbct, dvbct = _attn_glob_bwd(qbt, dobt, obt, lse_b, mrun_b, pbt, kb.T, vb, kbc.T, vbc, tq=tm, tk=tm)
    dkb, dvb, dkbc, dvbc = dkbt.T, dvbt.T, dkbct.T, dvbct.T
    dsink = jnp.sum(dsk[:, :, 0, 0], axis=1)

    dproj, dqg, dkg = _prep_bwd(dqat, dka, dva, dqbt, dkb, dvb, proj, cos, sin, qg, kg, dgl, tm=tm, name="prep_bwd")
    grad_x, dscale1, dshift1 = _mm_nt4_mod_bwd(dproj, win4, dxz1, x, scale1, tm=tm, name="in_proj_bwd")
    ns_in = win4.shape[-1]
    win_spec = dict(b_spec=pl.BlockSpec((None, None, ns_in), lambda k, t: (0, 0, k)),
                    out_shape=_sds((N_CHIPS, D_MODEL, ns_in)),
                    out_spec=pl.BlockSpec((None, D_MODEL, ns_in), lambda k, t: (k, 0, 0)),
                    colsum_spec=pl.BlockSpec((8, ns_in), lambda k, t: (0, k)), colsum_shape=_sds((8, IN_COLS)))
    win_spec["b_spec"] = pl.BlockSpec((tm, ns_in), lambda k, t: (t, k))
    dw_in4, db_in = _mm_tn(x, dproj, a_spec=pl.BlockSpec((tm, D_MODEL), lambda k, t: (t, 0)), grid=(N_CHIPS, n_s),
                           mod=(shift1, scale1), name="dw_in", **win_spec)

    zq = jnp.zeros((N_HEADS, LANES, nc), F32)
    dproj_c, _, dkg_c = _prep_bwd(zq, dkac, dvac, zq, dkbc, dvbc, proj_c, cos_c, sin_c, qg, kg,
                                  jnp.zeros((nc, IN_COLS - OFF_GA), BF16), tm=nc, name="prep_bwd_ctx")
    _, dscale_c, dshift_c = _mm_nt4_mod_bwd(dproj_c, win4, jnp.zeros((nc, D_MODEL), F32), ctx, scale_c, tm=nc,
                                            name="in_proj_bwd_ctx")
    win_spec["b_spec"] = pl.BlockSpec((nc, ns_in), lambda k, t: (t, k))
    dw_in4, db_in_c = _mm_tn(ctx, dproj_c, a_spec=pl.BlockSpec((nc, D_MODEL), lambda k, t: (t, 0)), grid=(N_CHIPS, 1),
                             mod=(shift_c, scale_c), init=dw_in4, name="dw_in_ctx", **win_spec)

    dmod = jnp.concatenate([dshift1, dscale1, dgate1, dshift2, dscale2, dgate2], axis=1)
    dmodc = jnp.concatenate([dshift_c, dscale_c], axis=1)
    dmodc_pad = jnp.concatenate([dmodc, jnp.zeros((1, 4 * D_MODEL), F32)], axis=1)
    dmodc8 = _first_row(dmodc_pad).astype(BF16)
    z8 = jnp.zeros((8, D_MODEL), F32)
    dsilu_c, _, _ = _mm_nt4_mod_bwd(dmodc8, wmod4, z8, z8, zvec, tm=8, name="c_ctx_bwd")
    sg = _sigmoid(c_ctx)
    dc_ctx = dsilu_c[0:1] * sg * (1.0 + c_ctx * (1.0 - sg))

    dqn = jnp.sum(dqg.reshape(N_HEADS, HEAD_DIM), axis=0, keepdims=True)
    dkn = jnp.sum((dkg + dkg_c).reshape(N_KV, HEAD_DIM), axis=0, keepdims=True)
    grads = dict(
        w_in4=dw_in4, b_in=db_in[0:1] + db_in_c[0:1], sink=dsink, qn=dqn, kn=dkn, wba=dwba, wbb=dwbb, w_out=dw_out,
        ln1_g=dln1_g, ln1_b=dln1_b, w_up4=dw_up4, conv_w=dcw2.transpose(1, 0, 2).reshape(3, 2 * D_FF),
        conv_b=dcb2.reshape(1, 2 * D_FF), w_down=dw_down, ln2_g=dln2_g, ln2_b=dln2_b,
        c_ctx=dc_ctx, dmod=dmod, dmodc=dmodc)
    return loss, grad_x, grads


ANY = pl.BlockSpec(memory_space=pl.ANY)


def _mesh_pos():
    return lax.axis_index("x"), lax.axis_index("y"), lax.axis_index("c")


def _other_chips(x, y):
    return [(1 - x, y), (x, 1 - y), (1 - x, 1 - y)]


def _remote(src, dst, send, recv, dev):
    return pltpu.make_async_remote_copy(src_ref=src, dst_ref=dst, send_sem=send, recv_sem=recv, device_id=dev,
                                        device_id_type=MESH)


def _set_block(stack, block, k):
    return lax.dynamic_update_slice(stack, block[None], (k,) + (0,) * block.ndim)


def _gather_shards(arrs, small):
    na = len(arrs)
    halves = [a.shape[0] // 2 for a in arrs]

    def body(*refs):
        ins, small_ref = refs[:na], refs[na]
        outs, small_out = refs[na + 1:2 * na + 1], refs[2 * na + 1]
        send, recv = refs[2 * na + 2:]
        x, y, c = _mesh_pos()
        me = 2 * x + y
        chips = _other_chips(x, y)

        def half(a, cc):
            return pl.ds(cc * halves[a], halves[a])

        sends = []
        for j, chip in enumerate(chips):
            for a in range(na):
                sends.append(_remote(ins[a].at[half(a, c)], outs[a].at[me, half(a, c)], send.at[a, j], recv.at[a, j],
                                     (*chip, c)))
            sends.append(_remote(small_ref, small_out.at[me], send.at[na, j], recv.at[na, j], (*chip, c)))
        for cp in sends:
            cp.start()
        for j, chip in enumerate(chips):
            kj = 2 * chip[0] + chip[1]
            for a in range(na):
                landed = outs[a].at[kj, half(a, c)]
                _remote(landed, landed, send.at[a, j], recv.at[a, j], (*chip, c)).wait_recv()
                fwd = _remote(landed, landed, send.at[a, 3 + j], recv.at[a, 3 + j], (x, y, 1 - c))
                fwd.start()
                sends.append(fwd)
            _remote(small_ref, small_out.at[kj], send.at[na, j], recv.at[na, j], (*chip, c)).wait_recv()
        for j, chip in enumerate(chips):
            kj = 2 * chip[0] + chip[1]
            for a in range(na):
                other = outs[a].at[kj, half(a, 1 - c)]
                _remote(other, other, send.at[a, 3 + j], recv.at[a, 3 + j], (x, y, 1 - c)).wait_recv()
        for cp in sends:
            cp.wait_send()

    out_shape = [_sds((N_CHIPS,) + a.shape, a.dtype) for a in arrs] + [_sds((N_CHIPS,) + small.shape, small.dtype)]
    got = pl.pallas_call(
        body, name="gather_shards", in_specs=[ANY] * (na + 1), out_specs=[ANY] * (na + 1), out_shape=out_shape,
        scratch_shapes=[pltpu.SemaphoreType.DMA((na + 1, 6)), pltpu.SemaphoreType.DMA((na + 1, 6))],
    )(*arrs, small)
    xp, yp, _ = _mesh_pos()
    return [_set_block(g, a, 2 * xp + yp) for g, a in zip(got, list(arrs) + [small])]


def _allgather_rows(v):
    r, n = v.shape

    def body(v_ref, out_ref, send, recv, loc):
        x, y, c = _mesh_pos()
        me, sibling = (x, y, c), (x, y, 1 - c)
        chips = _other_chips(x, y)

        def rows(px, py, pc):
            return out_ref.at[4 * px + 2 * py + pc]

        def copy(k, block, to, src=None):
            return _remote(rows(*block) if src is None else src, rows(*block), send.at[k], recv.at[k], to)

        mine = pltpu.make_async_copy(v_ref, rows(*me), loc)
        mine.start()
        first = [copy(0, me, sibling, src=v_ref)] + [copy(1 + j, me, (*chip, c), src=v_ref) for j, chip in enumerate(chips)]
        for cp in first:
            cp.start()
        passed = [copy(4 + j, (*chip, c), sibling) for j, chip in enumerate(chips)]
        for j, chip in enumerate(chips):
            copy(1 + j, (*chip, c), me).wait_recv()
            passed[j].start()
        copy(0, sibling, me).wait_recv()
        for j, chip in enumerate(chips):
            copy(4 + j, (*chip, 1 - c), me).wait_recv()
        for cp in first + passed:
            cp.wait_send()
        mine.wait()

    return pl.pallas_call(
        body, name="allgather_rows", in_specs=[pl.BlockSpec(memory_space=pltpu.VMEM)],
        out_specs=pl.BlockSpec(memory_space=pltpu.VMEM), out_shape=_sds((N_DEV, r, n), v.dtype),
        scratch_shapes=[pltpu.SemaphoreType.DMA((7,)), pltpu.SemaphoreType.DMA((7,)), pltpu.SemaphoreType.DMA],
    )(v)


def _swap_other_half(g):
    nb, r, n = g.shape
    rh = r // 2

    def body(g_ref, out_ref, send, recv):
        x, y, c = _mesh_pos()
        cp = _remote(g_ref.at[:, pl.ds((1 - c) * rh, rh), :], out_ref, send, recv, (x, y, 1 - c))
        cp.start()
        cp.wait()

    return pl.pallas_call(
        body, name="swap_other_half", in_specs=[ANY], out_specs=ANY, out_shape=_sds((nb, rh, n), g.dtype),
        scratch_shapes=[pltpu.SemaphoreType.DMA, pltpu.SemaphoreType.DMA],
    )(g)


def _scatter_to_chips(p):
    def body(p_ref, out_ref, send, recv):
        x, y, c = _mesh_pos()
        me = 2 * x + y
        chips = _other_chips(x, y)
        sends = [_remote(p_ref.at[2 * chip[0] + chip[1]], out_ref.at[me], send.at[j], recv.at[j], (*chip, c))
                 for j, chip in enumerate(chips)]
        for cp in sends:
            cp.start()
        for j, chip in enumerate(chips):
            kj = 2 * chip[0] + chip[1]
            _remote(p_ref.at[kj], out_ref.at[kj], send.at[j], recv.at[j], (*chip, c)).wait_recv()
        for cp in sends:
            cp.wait_send()

    got = pl.pallas_call(
        body, name="scatter_to_chips", in_specs=[ANY], out_specs=ANY, out_shape=_sds(p.shape, p.dtype),
        scratch_shapes=[pltpu.SemaphoreType.DMA((3,)), pltpu.SemaphoreType.DMA((3,))],
    )(p)
    xp, yp, _ = _mesh_pos()
    me = 2 * xp + yp
    return _set_block(got, lax.dynamic_index_in_dim(p, me, axis=0, keepdims=False), me)


def _join_halves(f):
    def body(f_ref, out_ref, send, recv):
        x, y, c = _mesh_pos()
        cp = _remote(f_ref, out_ref, send, recv, (x, y, 1 - c))
        cp.start()
        cp.wait()

    other = pl.pallas_call(
        body, name="join_halves", in_specs=[ANY], out_specs=ANY, out_shape=_sds(f.shape, f.dtype),
        scratch_shapes=[pltpu.SemaphoreType.DMA, pltpu.SemaphoreType.DMA],
    )(f)
    first = lax.axis_index("c") == 0
    return jnp.concatenate([jnp.where(first, f, other), jnp.where(first, other, f)], axis=0)


def _row_tile(rows, cap=512):
    t = cap - cap % 8
    while rows % t:
        t -= 8
    return t


def _add_blocks(a, b, out_dtype):
    nb, r, n = a.shape
    tr = _row_tile(r)

    def body(a_ref, b_ref, o_ref):
        o_ref[...] = (a_ref[...] + b_ref[...]).astype(out_dtype)

    spec = pl.BlockSpec((None, tr, n), lambda k, i: (k, i, 0))
    return _call(body, name="add_blocks", grid=(nb, r // tr), in_specs=[spec, spec], out_specs=spec,
                 out_shape=_sds(a.shape, out_dtype), sem=("parallel", "parallel"))(a, b)


def _sum_leading(a, *, name):
    nk, r, n = a.shape
    tr = _row_tile(r)

    def body(a_ref, o_ref):
        acc = a_ref[0].astype(F32)
        for k in range(1, nk):
            acc = acc + a_ref[k].astype(F32)
        o_ref[...] = acc

    return _call(body, name=name, grid=(r // tr,), in_specs=[pl.BlockSpec((nk, tr, n), lambda i: (0, i, 0))],
                 out_specs=pl.BlockSpec((tr, n), lambda i: (i, 0)), out_shape=_sds((r, n)), sem=("parallel",))(a)


def _silu_outer(a, b):
    kdim, n = a.shape[1], b.shape[1]

    def body(a_ref, b_ref, o_ref):
        av = a_ref[...]
        av = av * _sigmoid(av)
        bv = b_ref[...]
        ah, bh = av.astype(BF16), bv.astype(BF16)
        al, bl = (av - ah.astype(F32)).astype(BF16), (bv - bh.astype(F32)).astype(BF16)
        o_ref[...] = _dot_tn(ah, bh) + (_dot_tn(ah, bl) + _dot_tn(al, bh))

    return _call(body, name="dw_mod", grid=(1,), in_specs=[_full(a.shape), _full(b.shape)], out_specs=_full((kdim, n)),
                 out_shape=_sds((kdim, n)))(a, b)


def _adamw(w, g, m, v):
    r, n = w.shape
    tr = _row_tile(r)

    def body(w_ref, g_ref, m_ref, v_ref, d_ref, nm_ref, nv_ref):
        gv = g_ref[...]
        nm = ADAM_B1 * m_ref[...] + (1.0 - ADAM_B1) * gv
        nv = ADAM_B2 * v_ref[...] + (1.0 - ADAM_B2) * (gv * gv)
        m_hat = nm / (1.0 - ADAM_B1 ** ADAM_STEP)
        v_hat = nv / (1.0 - ADAM_B2 ** ADAM_STEP)
        d_ref[...] = -ADAM_LR * (m_hat / (jnp.sqrt(v_hat) + ADAM_EPS) + ADAM_WD * w_ref[...])
        nm_ref[...] = nm
        nv_ref[...] = nv

    spec = pl.BlockSpec((tr, n), lambda i: (i, 0))
    return _call(body, name="adamw", grid=(r // tr,), in_specs=[spec] * 4, out_specs=[spec] * 3,
                 out_shape=[_sds((r, n))] * 3, sem=("parallel",))(w, g, m, v)


BIG = ("w_in", "w_branch_a", "w_branch_b", "w_out", "w_up", "w_down", "conv_w")
BIG_ROWS = 3584
MATRICES = ("w_mod", "w_in", "w_branch_a", "w_branch_b", "w_out", "w_up", "w_down")
SMALL = ("b_mod", "b_in", "conv_b", "ln1_g", "ln1_b", "ln2_g", "ln2_b", "c_ctx", "attn_sink", "q_norm_g", "k_norm_g", "conv_w")
SMALL_ROWS = 8 * len(SMALL)


def _rows(a, n_rows):
    flat = a.reshape(-1)
    return jnp.pad(flat, (0, n_rows * D_MODEL - flat.shape[0])).reshape(n_rows, D_MODEL)


def _group8(a):
    return _rep8(_rows(a, 1)) if a.size <= D_MODEL else _rows(a, 8)


def _ungroup8(p, shape):
    size = math.prod(shape)
    return (p[0, :size] if size <= D_MODEL else p.reshape(-1)[:size]).reshape(shape)


def _unpack_big(p, like):
    out, r = {}, 0
    for n in BIG:
        size = math.prod(like[n].shape)
        nr = size // D_MODEL if n != "conv_w" else 8
        out[n] = p[r:r + nr].reshape(-1)[:size].reshape(like[n].shape)
        r += nr
    return out


def _pack_small(t):
    return jnp.concatenate([_group8(t[n]) for n in SMALL], axis=0)


def _unpack_small(p, like):
    return {n: _ungroup8(p[8 * i:8 * i + 8], like[n].shape) for i, n in enumerate(SMALL)}


WEIGHTS = ("c_ctx", "w_mod", "b_mod", "w_in", "b_in", "attn_sink", "q_norm_g", "k_norm_g", "w_branch_a", "w_branch_b",
           "w_out", "ln1_g", "ln1_b", "w_up", "conv_w", "conv_b", "w_down", "ln2_g", "ln2_b")


def kernel(x, c, ctx, c_ctx, w_mod, b_mod, w_in, b_in, attn_sink, q_norm_g, k_norm_g, w_branch_a, w_branch_b, w_out, ln1_g, ln1_b, w_up, conv_w, conv_b, w_down, ln2_g, ln2_b, loss_target, m_c_ctx, m_w_mod, m_b_mod, m_w_in, m_b_in, m_attn_sink, m_q_norm_g, m_k_norm_g, m_w_branch_a, m_w_branch_b, m_w_out, m_ln1_g, m_ln1_b, m_w_up, m_conv_w, m_conv_b, m_w_down, m_ln2_g, m_ln2_b, v_c_ctx, v_w_mod, v_b_mod, v_w_in, v_b_in, v_attn_sink, v_q_norm_g, v_k_norm_g, v_w_branch_a, v_w_branch_b, v_w_out, v_ln1_g, v_ln1_b, v_w_up, v_conv_w, v_conv_b, v_w_down, v_ln2_g, v_ln2_b):
    w = dict(c_ctx=c_ctx, w_mod=w_mod, b_mod=b_mod, w_in=w_in, b_in=b_in, attn_sink=attn_sink, q_norm_g=q_norm_g,
             k_norm_g=k_norm_g, w_branch_a=w_branch_a, w_branch_b=w_branch_b, w_out=w_out, ln1_g=ln1_g, ln1_b=ln1_b,
             w_up=w_up, conv_w=conv_w, conv_b=conv_b, w_down=w_down, ln2_g=ln2_g, ln2_b=ln2_b)
    m = dict(c_ctx=m_c_ctx, w_mod=m_w_mod, b_mod=m_b_mod, w_in=m_w_in, b_in=m_b_in, attn_sink=m_attn_sink,
             q_norm_g=m_q_norm_g, k_norm_g=m_k_norm_g, w_branch_a=m_w_branch_a, w_branch_b=m_w_branch_b, w_out=m_w_out,
             ln1_g=m_ln1_g, ln1_b=m_ln1_b, w_up=m_w_up, conv_w=m_conv_w, conv_b=m_conv_b, w_down=m_w_down,
             ln2_g=m_ln2_g, ln2_b=m_ln2_b)
    v = dict(c_ctx=v_c_ctx, w_mod=v_w_mod, b_mod=v_b_mod, w_in=v_w_in, b_in=v_b_in, attn_sink=v_attn_sink,
             q_norm_g=v_q_norm_g, k_norm_g=v_k_norm_g, w_branch_a=v_w_branch_a, w_branch_b=v_w_branch_b, w_out=v_w_out,
             ln1_g=v_ln1_g, ln1_b=v_ln1_b, w_up=v_w_up, conv_w=v_conv_w, conv_b=v_conv_b, w_down=v_w_down,
             ln2_g=v_ln2_g, ln2_b=v_ln2_b)
    xp, yp, _ = _mesh_pos()
    me = 2 * xp + yp

    branches = jnp.concatenate([w_branch_a[0], w_branch_b[0]], axis=0)
    wide = jnp.concatenate([w_mod[0], w_in[0], w_up[0], branches], axis=1).astype(BF16)
    tall = jnp.concatenate([w_out[0], w_down[0]], axis=0).astype(BF16)
    wide4, tall4, cw4 = _gather_shards([wide, tall], conv_w[0])
    n_mod, n_in, n_up = w_mod.shape[-1], w_in.shape[-1], w_up.shape[-1]
    wmod4 = wide4[:, :, :n_mod]
    win4 = wide4[:, :, n_mod:n_mod + n_in]
    wup4 = wide4[:, :, n_mod + n_in:n_mod + n_in + n_up]
    br4 = wide4[:, :, n_mod + n_in + n_up:]
    n_br = w_branch_a.shape[1]
    wba = br4[:, :n_br].transpose(1, 0, 2).reshape(n_br, D_MODEL)
    wbb = br4[:, n_br:].transpose(1, 0, 2).reshape(n_br, D_MODEL)
    n_out = w_out.shape[1]
    w_out_full = tall4[:, :n_out].reshape(D_MODEL, D_MODEL)
    w_down_full = tall4[:, n_out:].reshape(D_FF, D_MODEL)
    cw_full = cw4.transpose(1, 0, 2).reshape(3, 2 * D_FF)

    loss, grad_x, g = _local_step(
        x[0], c, ctx[0], c_ctx[None], wmod4, b_mod, win4, b_in, attn_sink[0], q_norm_g, k_norm_g, wba, wbb, w_out_full,
        ln1_g, ln1_b, wup4, cw_full, conv_b, w_down_full, ln2_g, ln2_b, loss_target[0])
    loss = lax.psum(loss, ("x", "y", "c"))

    sent = dict(c=c, dmod=g["dmod"], dmodc=g["dmodc"], b_in=g["b_in"], conv_b=g["conv_b"], ln1_g=g["ln1_g"],
                ln1_b=g["ln1_b"], ln2_g=g["ln2_g"], ln2_b=g["ln2_b"], c_ctx=g["c_ctx"], attn_sink=g["sink"],
                q_norm_g=g["qn"], k_norm_g=g["kn"])
    every = _allgather_rows(jnp.concatenate([_group8(a) for a in sent.values()], axis=0))
    total = _sum_leading(every, name="sum_devices")
    slot = {n: slice(8 * i, 8 * i + 8) for i, n in enumerate(sent)}
    gs = {n: _ungroup8(total[slot[n]], sent[n].shape) for n in SMALL if n in sent}
    dmodc_sum = jnp.concatenate([_ungroup8(total[slot["dmodc"]], (1, 2 * D_MODEL)), jnp.zeros((1, 4 * D_MODEL), F32)],
                                axis=1)
    gs["b_mod"] = _ungroup8(total[slot["dmod"]], b_mod.shape) + dmodc_sum
    acts = jnp.concatenate([every[:, slot["c"].start], _rep8(c_ctx)], axis=0)
    dmods = jnp.concatenate([every[:, slot["dmod"]].reshape(N_DEV, -1)[:, :6 * D_MODEL], _first_row(dmodc_sum)], axis=0)
    g_w_mod = _silu_outer(acts, lax.dynamic_slice_in_dim(dmods, me * n_mod, n_mod, axis=1))

    cw_g4 = _to_blocks4(g["conv_w"])
    packed = jnp.concatenate([
        g["w_in4"].reshape(N_CHIPS, -1, D_MODEL), _to_blocks4(g["wba"]).reshape(N_CHIPS, -1, D_MODEL),
        _to_blocks4(g["wbb"]).reshape(N_CHIPS, -1, D_MODEL), g["w_out"].reshape(N_CHIPS, -1, D_MODEL),
        g["w_up4"].reshape(N_CHIPS, -1, D_MODEL), g["w_down"].reshape(N_CHIPS, -1, D_MODEL),
        jnp.pad(cw_g4.reshape(N_CHIPS, -1), ((0, 0), (0, 8 * D_MODEL - cw_g4.shape[1] * cw_g4.shape[2]))).reshape(
            N_CHIPS, 8, D_MODEL),
        jnp.zeros((N_CHIPS, BIG_ROWS - 3528, D_MODEL), F32)], axis=1)
    rh = BIG_ROWS // 2
    cpos = lax.axis_index("c")
    my_half = lax.dynamic_slice_in_dim(packed, cpos * rh, rh, axis=1)
    chip_sum = _add_blocks(my_half, _swap_other_half(packed), BF16)
    half_sum = _sum_leading(_scatter_to_chips(chip_sum), name="sum_chips")
    g_big = _unpack_big(_join_halves(half_sum), w)

    grads = dict(gs, w_mod=g_w_mod, **g_big)
    grads = {n: grads[n].reshape(w[n].shape) for n in WEIGHTS}
    delta, new_m, new_v = {}, {}, {}
    for n in MATRICES:
        outs = _adamw(*[t[n][0] for t in (w, grads, m, v)])
        delta[n], new_m[n], new_v[n] = [o[None] for o in outs]
    outs = _adamw(*[_pack_small(t) for t in (w, grads, m, v)])
    for res, o in zip((delta, new_m, new_v), outs):
        res.update(_unpack_small(o, w))
    return (loss, grad_x[None], *[grads[n] for n in WEIGHTS], *[delta[n] for n in WEIGHTS],
            *[new_m[n] for n in WEIGHTS], *[new_v[n] for n in WEIGHTS])
```

```python
import functools
import math

import jax
import jax.numpy as jnp
from jax import lax
from jax.experimental import pallas as pl
from jax.experimental.pallas import tpu as pltpu

F32 = jnp.float32
BF16 = jnp.bfloat16

D_MODEL = 1024
HEAD_DIM = 64
N_HEADS = 8
N_KV = 2
WINDOW = 128
GRID_W = 64
ROPE_THETA = 10000.0
D_FF = 2816
LN_EPS = 1e-5
QK_EPS = 1e-6
ALPHA = 2.0 ** 0.25
Q_SCALE = HEAD_DIM ** -0.5
OFF_GA = 1536
IN_COLS = 3584
ADAM_LR, ADAM_B1, ADAM_B2, ADAM_EPS, ADAM_WD, ADAM_STEP = 0.001, 0.9, 0.999, 1e-8, 0.01, 10

LANES = 128
VMEM_BUDGET = 52 * 1024 * 1024
N_CHIPS = 4
N_DEV = 8
NEG = -1e30
MESH = pl.DeviceIdType.MESH


def _sigmoid(x):
    return 1.0 / (1.0 + jnp.exp(-x))


def _dot(a, b):
    return jnp.dot(a, b, preferred_element_type=F32)


def _dot_nt(a, b):
    return lax.dot_general(a, b, (((1,), (1,)), ((), ())), preferred_element_type=F32)


def _dot_tn(a, b):
    return lax.dot_general(a, b, (((0,), (0,)), ((), ())), preferred_element_type=F32)


def _call(body, *, name, grid, in_specs, out_specs, out_shape, scratch=(), sem=None, **kw):
    params = dict(vmem_limit_bytes=VMEM_BUDGET)
    if sem is not None:
        params["dimension_semantics"] = sem
    return pl.pallas_call(body, name=name, grid=grid, in_specs=in_specs, out_specs=out_specs,
                          out_shape=out_shape, scratch_shapes=list(scratch),
                          compiler_params=pltpu.CompilerParams(**params), **kw)


def _full(shape):
    n = len(shape)
    return pl.BlockSpec(shape, lambda *_: (0,) * n)


def _sds(shape, dtype=F32):
    return jax.ShapeDtypeStruct(shape, dtype)


def _mm_nn4(a, shift, scale, w4, bias, *, mode, split_out, out_dtype, tm, name):
    m, kdim = a.shape
    nb, _, ns = w4.shape

    def body(a_ref, sh_ref, sc_ref, w_ref, b_ref, o_ref):
        av = a_ref[...]
        if mode == "modulate":
            av = av * (1.0 + sc_ref[...]) + sh_ref[...]
        else:
            av = av * _sigmoid(av)
        o_ref[...] = (_dot(av.astype(BF16), w_ref[...]) + b_ref[...]).astype(out_dtype)

    if split_out:
        out_shape = _sds((2, m, 2 * ns), out_dtype)
        out_spec = pl.BlockSpec((None, tm, ns), lambda i, k: (k // 2, i, k % 2))
    else:
        out_shape = _sds((m, nb * ns), out_dtype)
        out_spec = pl.BlockSpec((tm, ns), lambda i, k: (i, k))
    return _call(
        body, name=name, grid=(m // tm, nb),
        in_specs=[pl.BlockSpec((tm, kdim), lambda i, k: (i, 0)),
                  pl.BlockSpec((1, kdim), lambda i, k: (0, 0)),
                  pl.BlockSpec((1, kdim), lambda i, k: (0, 0)),
                  pl.BlockSpec((None, kdim, ns), lambda i, k: (k, 0, 0)),
                  pl.BlockSpec((1, ns), lambda i, k: (0, k))],
        out_specs=out_spec, out_shape=out_shape, sem=("parallel", "arbitrary"),
    )(a, shift, scale, w4, bias)


def _mm_tn(a, b, *, a_spec, b_spec, grid, out_shape, out_spec, name, mod=None, init=None, colsum_spec=None,
           colsum_shape=None):
    red = len(grid) - 1
    has_mod, has_init, has_cs = mod is not None, init is not None, colsum_spec is not None

    def body(*refs):
        refs = list(refs)
        a_ref, b_ref = refs[0], refs[1]
        pos = 2
        if has_mod:
            sh_ref, sc_ref = refs[2], refs[3]
            pos = 4
        if has_init:
            init_ref = refs[pos]
            pos += 1
        o_ref = refs[pos]
        cs_ref = refs[pos + 1] if has_cs else None
        s = pl.program_id(red)

        @pl.when(s == 0)
        def _():
            o_ref[...] = init_ref[...] if has_init else jnp.zeros(o_ref.shape, F32)
            if has_cs:
                cs_ref[...] = jnp.zeros(cs_ref.shape, F32)

        av = a_ref[...]
        if has_mod:
            av = av * (1.0 + sc_ref[...]) + sh_ref[...]
        bv = b_ref[...]
        o_ref[...] += _dot_tn(av.astype(BF16), bv)
        if has_cs:
            cs_ref[...] += jnp.broadcast_to(jnp.sum(bv.astype(F32), axis=0, keepdims=True), cs_ref.shape)

    ins, in_specs = [a, b], [a_spec, b_spec]
    if has_mod:
        kdim = mod[0].shape[-1]
        ins += list(mod)
        in_specs += [_full((1, kdim)), _full((1, kdim))]
    if has_init:
        ins.append(init)
        in_specs.append(out_spec)
    out_specs, out_shapes = out_spec, out_shape
    if has_cs:
        out_specs, out_shapes = [out_spec, colsum_spec], [out_shape, colsum_shape]
    sem = ("parallel",) * red + ("arbitrary",)
    return _call(body, name=name, grid=grid, in_specs=in_specs, out_specs=out_specs, out_shape=out_shapes,
                 sem=sem)(*ins)


def _rope_tables(n_tok):
    pos = jnp.arange(n_tok, dtype=jnp.int32)
    rows = (pos // GRID_W).astype(F32)
    cols = (pos % GRID_W).astype(F32)
    n_freq = HEAD_DIM // 4
    inv_freq = ROPE_THETA ** (-jnp.arange(n_freq, dtype=F32) / n_freq)
    ang_r = rows[:, None] * inv_freq
    ang_c = cols[:, None] * inv_freq
    cos = jnp.concatenate([jnp.cos(ang_r)] * 2 + [jnp.cos(ang_c)] * 2, axis=-1)
    sin = jnp.concatenate([-jnp.sin(ang_r), jnp.sin(ang_r), -jnp.sin(ang_c), jnp.sin(ang_c)], axis=-1)
    return jnp.tile(cos, (1, 2)), jnp.tile(sin, (1, 2))


def _lane(shape):
    return lax.broadcasted_iota(jnp.int32, shape, 1)


def _rope_partner(t, lane):
    return jnp.where((lane % 32) < 16, pltpu.roll(t, LANES - 16, 1), pltpu.roll(t, 16, 1))


def _half_mean(s, lane):
    lo = jnp.sum(jnp.where(lane < HEAD_DIM, s, 0.0), axis=-1, keepdims=True)
    hi = jnp.sum(jnp.where(lane < HEAD_DIM, 0.0, s), axis=-1, keepdims=True)
    return jnp.where(lane < HEAD_DIM, lo, hi) * (1.0 / HEAD_DIM)


def _prep(proj, cos, sin, qg, kg, *, tm, name):
    m = proj.shape[0]

    def body(p_ref, cos_ref, sin_ref, qg_ref, kg_ref, qa_ref, ka_ref, va_ref, qb_ref, kb_ref, vb_ref):
        lane = _lane((tm, LANES))
        cosv, sinv = cos_ref[...], sin_ref[...]
        low = lane < HEAD_DIM

        def rope(t):
            return t * cosv + _rope_partner(t, lane) * sinv

        def rms(t, g):
            return t * lax.rsqrt(_half_mean(t * t, lane) + QK_EPS) * g

        def place(q_ref, j, chunk):
            sw = pltpu.roll(chunk, HEAD_DIM, 1)
            if j < 2:
                h0, h1 = jnp.where(low, chunk, 0.0), jnp.where(low, sw, 0.0)
            else:
                h0, h1 = jnp.where(low, 0.0, sw), jnp.where(low, 0.0, chunk)
            q_ref[2 * j] = h0.T.astype(BF16)
            q_ref[2 * j + 1] = h1.T.astype(BF16)

        for j in range(4):
            place(qa_ref, j, rope(p_ref[:, j * LANES:(j + 1) * LANES]) * Q_SCALE)
            place(qb_ref, j, rope(rms(p_ref[:, 768 + j * LANES:768 + (j + 1) * LANES], qg_ref[...])) * Q_SCALE)
        ka_ref[...] = rope(p_ref[:, 512:640]).astype(BF16)
        va_ref[...] = p_ref[:, 640:768].astype(BF16)
        kb_ref[...] = rope(rms(p_ref[:, 1280:1408], kg_ref[...])).astype(BF16)
        vb_ref[...] = p_ref[:, 1408:1536].astype(BF16)

    row = pl.BlockSpec((tm, LANES), lambda i: (i, 0))
    qspec = pl.BlockSpec((N_HEADS, LANES, tm), lambda i: (0, 0, i))
    return _call(
        body, name=name, grid=(m // tm,),
        in_specs=[pl.BlockSpec((tm, OFF_GA), lambda i: (i, 0)), row, row, _full((1, LANES)), _full((1, LANES))],
        out_specs=[qspec, row, row, qspec, row, row],
        out_shape=[_sds((N_HEADS, LANES, m), BF16), _sds((m, LANES), BF16), _sds((m, LANES), BF16),
                   _sds((N_HEADS, LANES, m), BF16), _sds((m, LANES), BF16), _sds((m, LANES), BF16)],
        sem=("parallel",),
    )(proj, cos, sin, qg, kg)


def _prep_bwd(dqa, dka, dva, dqb, dkb, dvb, proj, cos, sin, qg, kg, dgl, *, tm, name):
    m = proj.shape[0]

    def body(dqa_ref, dka_ref, dva_ref, dqb_ref, dkb_ref, dvb_ref, p_ref, cos_ref, sin_ref, qg_ref, kg_ref,
             dgl_ref, dp_ref, dqg_ref, dkg_ref):
        i = pl.program_id(0)
        lane = _lane((tm, LANES))
        cosv, sinv = cos_ref[...], sin_ref[...]
        low = lane < HEAD_DIM

        @pl.when(i == 0)
        def _():
            dqg_ref[...] = jnp.zeros(dqg_ref.shape, F32)
            dkg_ref[...] = jnp.zeros(dkg_ref.shape, F32)

        def unrope(d):
            return d * cosv - _rope_partner(d, lane) * sinv

        def unplace(dq_ref, j):
            d0, d1 = dq_ref[2 * j].T, dq_ref[2 * j + 1].T
            if j < 2:
                return jnp.where(low, d0, pltpu.roll(d1, HEAD_DIM, 1))
            return jnp.where(low, pltpu.roll(d0, HEAD_DIM, 1), d1)

        def unrms(dtn, t, g):
            r = lax.rsqrt(_half_mean(t * t, lane) + QK_EPS)
            u = dtn * g
            dt = r * u - t * (r * r * r) * _half_mean(u * t, lane)
            return dt, jnp.sum(dtn * t * r, axis=0, keepdims=True)

        for j in range(4):
            dp_ref[:, j * LANES:(j + 1) * LANES] = (unrope(unplace(dqa_ref, j)) * Q_SCALE).astype(BF16)
            c0 = 768 + j * LANES
            dt, dg = unrms(unrope(unplace(dqb_ref, j)) * Q_SCALE, p_ref[:, c0:c0 + LANES], qg_ref[...])
            dp_ref[:, c0:c0 + LANES] = dt.astype(BF16)
            dqg_ref[:, j * LANES:(j + 1) * LANES] += dg
        dp_ref[:, 512:640] = unrope(dka_ref[...]).astype(BF16)
        dp_ref[:, 640:768] = dva_ref[...].astype(BF16)
        dt, dg = unrms(unrope(dkb_ref[...]), p_ref[:, 1280:1408], kg_ref[...])
        dp_ref[:, 1280:1408] = dt.astype(BF16)
        dkg_ref[...] += dg
        dp_ref[:, 1408:1536] = dvb_ref[...].astype(BF16)
        dp_ref[:, OFF_GA:] = dgl_ref[...]

    row = pl.BlockSpec((tm, LANES), lambda i: (i, 0))
    qspec = pl.BlockSpec((N_HEADS, LANES, tm), lambda i: (0, 0, i))
    return _call(
        body, name=name, grid=(m // tm,),
        in_specs=[qspec, row, row, qspec, row, row, pl.BlockSpec((tm, OFF_GA), lambda i: (i, 0)), row, row,
                  _full((1, LANES)), _full((1, LANES)), pl.BlockSpec((tm, IN_COLS - OFF_GA), lambda i: (i, 0))],
        out_specs=[pl.BlockSpec((tm, IN_COLS), lambda i: (i, 0)), _full((1, 512)), _full((1, LANES))],
        out_shape=[_sds((m, IN_COLS), BF16), _sds((1, 512)), _sds((1, LANES))],
        sem=("arbitrary",),
    )(dqa, dka, dva, dqb, dkb, dvb, proj, cos, sin, qg, kg, dgl)


def _attn_glob_fwd(qt, k, vt, kc, vct, *, tq, tk):
    nh, _, s = qt.shape
    nc = kc.shape[0]
    n_chunks = s // tk
    half = LANES // 2

    def body(qt_ref, k_ref, vt_ref, kc_ref, vct_ref, ot_ref, lse_ref, mrun_ref, p_hbm,
             acc_sc, st_sc, stage_sc, stagec_sc, sems, semc):
        h, i = pl.program_id(0), pl.program_id(1)
        qtv = qt_ref[...]
        acc_sc[...] = jnp.zeros(acc_sc.shape, F32)

        def p_out(slot, c):
            return pltpu.make_async_copy(stage_sc.at[slot], p_hbm.at[h, i, pl.ds(pl.multiple_of(c * tk, tk), tk), :],
                                         sems.at[slot])

        def update(st, vtv, m_old):
            m_new = jnp.maximum(m_old, jnp.max(st, axis=0, keepdims=True))
            pb = jnp.exp(st - m_new).astype(BF16)
            acc_sc[...] = acc_sc[...] * jnp.exp(m_old - m_new) + _dot(vtv, pb)
            return m_new, pb

        m, pbc = update(_dot(kc_ref[...], qtv), vct_ref[...], jnp.full((1, tq), NEG, F32))
        mrun_ref[pl.ds(n_chunks, 1), :] = m
        stagec_sc[...] = pbc
        ctx_out = pltpu.make_async_copy(stagec_sc, p_hbm.at[h, i, pl.ds(s, nc), :], semc)
        ctx_out.start()

        def step(c, st, m_old):
            slot = c % 2
            off = pl.multiple_of(c * tk, tk)
            nxt = pl.multiple_of(jnp.minimum(c + 1, n_chunks - 1) * tk, tk)
            st_next = _dot(k_ref[pl.ds(nxt, tk), :], qtv)
            m_new, pb = update(st, vt_ref[:, pl.ds(off, tk)], m_old)
            mrun_ref[pl.ds(c, 1), :] = m_new
            stage_sc[slot] = pb
            p_out(slot, c).start()
            return st_next, m_new

        def loop(c, m_old):
            st_next, m_new = step(c, st_sc[...], m_old)
            p_out(1 - c % 2, c - 1).wait()
            st_sc[...] = st_next
            return m_new

        stage_sc[1] = jnp.zeros((tk, tq), BF16)
        pltpu.make_async_copy(stage_sc.at[1], p_hbm.at[h, i, pl.ds(s + nc, tk), :], sems.at[1]).start()
        st_sc[...] = _dot(k_ref[pl.ds(0, tk), :], qtv)
        m = lax.fori_loop(0, n_chunks, loop, m)
        p_out((n_chunks - 1) % 2, n_chunks - 1).wait()
        ctx_out.wait()
        acc = acc_sc[...]
        l = jnp.where(h < nh // N_KV, acc[half:half + 1], acc[0:1])
        ot_ref[...] = (acc / l).astype(BF16)
        lse_ref[...] = m + jnp.log(l)

    grp = nh // N_KV
    return _call(
        body, name="attn_glob_fwd", grid=(nh, s // tq),
        in_specs=[pl.BlockSpec((None, LANES, tq), lambda h, i: (h, 0, i)), _full((s, LANES)),
                  pl.BlockSpec((None, LANES, s), lambda h, i: (h // grp, 0, 0)), _full((nc, LANES)),
                  pl.BlockSpec((None, LANES, nc), lambda h, i: (h // grp, 0, 0))],
        out_specs=[pl.BlockSpec((None, LANES, tq), lambda h, i: (h, 0, i)),
                   pl.BlockSpec((None, 1, tq), lambda h, i: (h, 0, i)),
                   pl.BlockSpec((None, n_chunks + 1, tq), lambda h, i: (h, 0, i)), ANY],
        out_shape=[_sds((nh, LANES, s), BF16), _sds((nh, 1, s)), _sds((nh, n_chunks + 1, s)),
                   _sds((nh, s // tq, s + nc + tk, tq), BF16)],
        scratch=[pltpu.VMEM((LANES, tq), F32), pltpu.VMEM((tk, tq), F32), pltpu.VMEM((2, tk, tq), BF16),
                 pltpu.VMEM((nc, tq), BF16), pltpu.SemaphoreType.DMA((2,)), pltpu.SemaphoreType.DMA],
        sem=("parallel", "parallel"),
    )(qt, k, vt, kc, vct)


P_AHEAD = 3


def _attn_glob_bwd(qt, dot, ot, lse, mrun, p, kt, v, kct, vc, *, tq, tk):
    nh, _, s = qt.shape
    nc = vc.shape[0]
    n_q = s // tq
    n_chunks = s // tk
    n_run = mrun.shape[1] - 1
    per_run = n_chunks // n_run

    def body(qt_ref, dot_ref, ot_ref, lse_ref, mrun_ref, p_hbm, kt_ref, v_ref, kct_ref, vc_ref,
             dqt_ref, dkt_ref, dvt_ref, dkct_ref, dvct_ref, acc_sc, dp_sc, dkt_sc, dvt_sc, p_sc, pc_sc, sems, semc):
        h, i = pl.program_id(0), pl.program_id(1)

        @pl.when(jnp.logical_and(h == 0, i == 0))
        def _():
            dkct_ref[...] = jnp.zeros(dkct_ref.shape, F32)
            dvct_ref[...] = jnp.zeros(dvct_ref.shape, F32)
            dkt_sc[...] = jnp.zeros(dkt_sc.shape, F32)
            dvt_sc[...] = jnp.zeros(dvt_sc.shape, F32)


        def p_in(slot, c):
            return pltpu.make_async_copy(p_hbm.at[h, i, pl.ds(pl.multiple_of(c * tk, tk), tk), :], p_sc.at[slot],
                                         sems.at[slot])

        ctx_in = pltpu.make_async_copy(p_hbm.at[h, i, pl.ds(s, nc), :], pc_sc, semc)
        ctx_in.start()
        for c in range(P_AHEAD):
            p_in(c, min(c, n_chunks - 1)).start()
        qtv, dotv, lse = qt_ref[...], dot_ref[...], lse_ref[...]
        delta = jnp.sum(dotv.astype(F32) * ot_ref[...].astype(F32), axis=0, keepdims=True)

        def grads(pt_stored, m_row, dpt):
            pt = pt_stored.astype(F32) * jnp.exp(m_row - lse)
            return pt.astype(BF16), (pt * (dpt - delta)).astype(BF16)

        dp_sc[...] = _dot(v_ref[pl.ds(0, tk), :], dotv)
        ctx_in.wait()
        pb, dsb = grads(pc_sc[...], mrun_ref[pl.ds(n_run, 1), :], _dot(vc_ref[...], dotv))
        acc_sc[...] = _dot(kct_ref[...], dsb)
        dkct_ref[...] += _dot_nt(qtv, dsb)
        dvct_ref[...] += _dot_nt(dotv, pb)

        def loop(c, carry):
            slot = c % (P_AHEAD + 1)
            off = pl.multiple_of(c * tk, tk)
            nxt = pl.multiple_of(jnp.minimum(c + 1, n_chunks - 1) * tk, tk)
            p_in(slot, c).wait()
            p_in((c + P_AHEAD) % (P_AHEAD + 1), jnp.minimum(c + P_AHEAD, n_chunks - 1)).start()
            dpt = dp_sc[...]
            dp_next = _dot(v_ref[pl.ds(nxt, tk), :], dotv)
            pb, dsb = grads(p_sc[slot], mrun_ref[pl.ds(c // per_run, 1), :], dpt)
            acc_sc[...] += _dot(kt_ref[:, pl.ds(off, tk)], dsb)
            dkt_sc[:, pl.ds(off, tk)] += _dot_nt(qtv, dsb)
            dvt_sc[:, pl.ds(off, tk)] += _dot_nt(dotv, pb)
            dp_sc[...] = dp_next
            return carry

        lax.fori_loop(0, n_chunks, loop, 0)
        for c in range(n_chunks, n_chunks + P_AHEAD):
            p_in(c % (P_AHEAD + 1), n_chunks - 1).wait()
        dqt_ref[...] = acc_sc[...]

        @pl.when(jnp.logical_and(h == nh - 1, i == n_q - 1))
        def _():
            pltpu.sync_copy(dkt_sc, dkt_ref)
            pltpu.sync_copy(dvt_sc, dvt_ref)

    qs = pl.BlockSpec((None, LANES, tq), lambda h, i: (h, 0, i))
    rs = pl.BlockSpec((None, 1, tq), lambda h, i: (h, 0, i))
    return _call(
        body, name="attn_glob_bwd", grid=(nh, n_q),
        in_specs=[qs, qs, qs, rs, pl.BlockSpec((None, n_run + 1, tq), lambda h, i: (h, 0, i)), ANY,
                  _full((LANES, s)), _full((s, LANES)), _full((LANES, nc)), _full((nc, LANES))],
        out_specs=[qs, ANY, ANY, _full((LANES, nc)), _full((LANES, nc))],
        out_shape=[_sds((nh, LANES, s)), _sds((LANES, s)), _sds((LANES, s)), _sds((LANES, nc)), _sds((LANES, nc))],
        scratch=[pltpu.VMEM((LANES, tq), F32), pltpu.VMEM((tk, tq), F32), pltpu.VMEM((LANES, s), F32),
                 pltpu.VMEM((LANES, s), F32), pltpu.VMEM((P_AHEAD + 1, tk, tq), BF16), pltpu.VMEM((nc, tq), BF16),
                 pltpu.SemaphoreType.DMA((P_AHEAD + 1,)), pltpu.SemaphoreType.DMA],
        sem=("arbitrary", "arbitrary"),
    )(qt, dot, ot, lse, mrun, p, kt, v, kct, vc)


WIN_SPAN = 2 * WINDOW


def _band(rows0, cols0, shape):
    r = rows0 + lax.broadcasted_iota(jnp.int32, shape, 0)
    c = cols0 + lax.broadcasted_iota(jnp.int32, shape, 1)
    return jnp.abs(r - c) <= WINDOW


def _win_start(blk, t, s):
    return pl.multiple_of(jnp.clip(blk * t - WINDOW, 0, s - t - WIN_SPAN), WINDOW)


def _attn_win_fwd(qt, k, vt, kc, vct, sink, *, tq):
    nh, _, s = qt.shape
    nc = kc.shape[0]
    tw = tq + WIN_SPAN
    half = LANES // 2
    grp = nh // N_KV

    def body(sink_ref, qt_ref, k_ref, vt_ref, kc_ref, vct_ref, ot_ref, lse_ref):
        h, i = pl.program_id(0), pl.program_id(1)
        k0 = _win_start(i, tq, s)
        qtv = qt_ref[...]
        st = jnp.where(_band(k0, i * tq, (tw, tq)), _dot(k_ref[pl.ds(k0, tw), :], qtv), NEG)
        stc = _dot(kc_ref[...], qtv)
        snk = sink_ref[h]
        m = jnp.maximum(jnp.maximum(jnp.max(st, axis=0, keepdims=True), jnp.max(stc, axis=0, keepdims=True)), snk)
        acc = (_dot(vt_ref[:, pl.ds(k0, tw)], jnp.exp(st - m).astype(BF16))
               + _dot(vct_ref[...], jnp.exp(stc - m).astype(BF16)))
        l = jnp.where(h < grp, acc[half:half + 1], acc[0:1]) + jnp.exp(snk - m)
        ot_ref[...] = (acc / l).astype(BF16)
        lse_ref[...] = m + jnp.log(l)

    return _call(
        body, name="attn_win_fwd", grid=(nh, s // tq),
        in_specs=[pl.BlockSpec(memory_space=pltpu.SMEM),
                  pl.BlockSpec((None, LANES, tq), lambda h, i: (h, 0, i)), _full((s, LANES)),
                  pl.BlockSpec((None, LANES, s), lambda h, i: (h // grp, 0, 0)), _full((nc, LANES)),
                  pl.BlockSpec((None, LANES, nc), lambda h, i: (h // grp, 0, 0))],
        out_specs=[pl.BlockSpec((None, LANES, tq), lambda h, i: (h, 0, i)),
                   pl.BlockSpec((None, 1, tq), lambda h, i: (h, 0, i))],
        out_shape=[_sds((nh, LANES, s), BF16), _sds((nh, 1, s))],
        sem=("parallel", "parallel"),
    )(sink, qt, k, vt, kc, vct)


def _attn_win_bwd(qt, dot, ot, lse, k, kt, v, kc, kct, vc, sink, *, tq):
    nh, _, s = qt.shape
    nc = kc.shape[0]
    tw = tq + WIN_SPAN
    nq = s // tq

    def body(sink_ref, qt_ref, dot_ref, ot_ref, lse_ref, k_ref, kt_ref, v_ref, kc_ref, kct_ref, vc_ref,
             dqt_ref, dkt_ref, dvt_ref, dkct_ref, dvct_ref, dsk_ref, dkt_sc, dvt_sc):
        h, i = pl.program_id(0), pl.program_id(1)

        @pl.when(jnp.logical_and(h == 0, i == 0))
        def _():
            dkct_ref[...] = jnp.zeros(dkct_ref.shape, F32)
            dvct_ref[...] = jnp.zeros(dvct_ref.shape, F32)
            dkt_sc[...] = jnp.zeros(dkt_sc.shape, F32)
            dvt_sc[...] = jnp.zeros(dvt_sc.shape, F32)

        k0 = _win_start(i, tq, s)
        span = pl.ds(k0, tw)
        qtv, dotv, lse = qt_ref[...], dot_ref[...], lse_ref[...]
        delta = jnp.sum(dotv.astype(F32) * ot_ref[...].astype(F32), axis=0, keepdims=True)
        pt = jnp.where(_band(k0, i * tq, (tw, tq)), jnp.exp(_dot(k_ref[span, :], qtv) - lse), 0.0)
        dsb = (pt * (_dot(v_ref[span, :], dotv) - delta)).astype(BF16)
        pct = jnp.exp(_dot(kc_ref[...], qtv) - lse)
        dscb = (pct * (_dot(vc_ref[...], dotv) - delta)).astype(BF16)
        dqt_ref[...] = _dot(kt_ref[:, span], dsb) + _dot(kct_ref[...], dscb)
        dkt_sc[:, span] += _dot_nt(qtv, dsb)
        dvt_sc[:, span] += _dot_nt(dotv, pt.astype(BF16))
        dkct_ref[...] += _dot_nt(qtv, dscb)
        dvct_ref[...] += _dot_nt(dotv, pct.astype(BF16))
        dsk = -jnp.sum(jnp.exp(sink_ref[h] - lse) * delta)
        dsk_ref[...] = jnp.full(dsk_ref.shape, dsk, F32)

        @pl.when(jnp.logical_and(h == nh - 1, i == nq - 1))
        def _():
            pltpu.sync_copy(dkt_sc, dkt_ref)
            pltpu.sync_copy(dvt_sc, dvt_ref)

    qs = pl.BlockSpec((None, LANES, tq), lambda h, i: (h, 0, i))
    rs = pl.BlockSpec((None, 1, tq), lambda h, i: (h, 0, i))
    return _call(
        body, name="attn_win_bwd", grid=(nh, nq),
        in_specs=[pl.BlockSpec(memory_space=pltpu.SMEM), qs, qs, qs, rs, _full((s, LANES)), _full((LANES, s)),
                  _full((s, LANES)), _full((nc, LANES)), _full((LANES, nc)), _full((nc, LANES))],
        out_specs=[qs, ANY, ANY, _full((LANES, nc)), _full((LANES, nc)),
                   pl.BlockSpec((None, None, 8, LANES), lambda h, i: (h, i, 0, 0))],
        out_shape=[_sds((nh, LANES, s)), _sds((LANES, s)), _sds((LANES, s)), _sds((LANES, nc)), _sds((LANES, nc)),
                   _sds((nh, nq, 8, LANES))],
        scratch=[pltpu.VMEM((LANES, s), F32), pltpu.VMEM((LANES, s), F32)],
        sem=("arbitrary", "arbitrary"),
    )(sink, qt, dot, ot, lse, k, kt, v, kc, kct, vc)


def _ln_fwd(z, g, b):
    mu = jnp.mean(z, axis=-1, keepdims=True)
    zc = z - mu
    r = lax.rsqrt(jnp.mean(zc * zc, axis=-1, keepdims=True) + LN_EPS)
    return zc * r * g + b, mu, r


def _ln_bwd(dy, xhat, r, g):
    dxh = dy * g
    return r * (dxh - jnp.mean(dxh, axis=-1, keepdims=True) - xhat * jnp.mean(dxh * xhat, axis=-1, keepdims=True))


def _heads_matmul(ot_ref, w_ref):
    acc = _dot_tn(ot_ref[0], w_ref[0])
    for h in range(1, N_HEADS):
        acc += _dot_tn(ot_ref[h], w_ref[h])
    return acc


def _gate_specs(tm):
    return [pl.BlockSpec((tm, 512), functools.partial(lambda i, b: (i, b), b=OFF_GA // 512 + b)) for b in range(4)]


def _merge_fwd(oat, obt, proj, x, gate1, wba, wbb, w_out, ln_g, ln_b, *, tm):
    s = x.shape[0]

    def body(oa_ref, ob_ref, g0, g1, g2, g3, x_ref, gt_ref, wba_ref, wbb_ref, wo_ref, lg_ref, lb_ref,
             x1_ref, y_ref, mu_ref, r_ref):
        ga = _sigmoid(jnp.concatenate([g0[...], g1[...]], axis=1))
        gb = _sigmoid(jnp.concatenate([g2[...], g3[...]], axis=1))
        merged = ga * _heads_matmul(oa_ref, wba_ref) + gb * _heads_matmul(ob_ref, wbb_ref)
        y = _dot(merged.astype(BF16), wo_ref[...])
        x1, mu, r = _ln_fwd(ALPHA * x_ref[...] + gt_ref[...] * y, lg_ref[...], lb_ref[...])
        x1_ref[...] = x1
        y_ref[...] = y
        mu_ref[...] = mu
        r_ref[...] = r

    hts = pl.BlockSpec((N_HEADS, LANES, tm), lambda i: (0, 0, i))
    row = pl.BlockSpec((tm, D_MODEL), lambda i: (i, 0))
    col = pl.BlockSpec((tm, 1), lambda i: (i, 0))
    vec = _full((1, D_MODEL))
    wh = _full((N_HEADS, LANES, D_MODEL))
    return _call(
        body, name="merge_fwd", grid=(s // tm,),
        in_specs=[hts, hts, *_gate_specs(tm), row, vec, wh, wh, _full((D_MODEL, D_MODEL)), vec, vec],
        out_specs=[row, row, col, col],
        out_shape=[_sds((s, D_MODEL)), _sds((s, D_MODEL)), _sds((s, 1)), _sds((s, 1))],
        sem=("parallel",),
    )(oat, obt, proj, proj, proj, proj, x, gate1, wba, wbb, w_out, ln_g, ln_b)


def _merge_bwd(dy, oat, obt, proj, wba, wbb, w_out, *, tm):
    s = dy.shape[0]

    def body(dy_ref, oat_ref, obt_ref, g0, g1, g2, g3, wba_ref, wbb_ref, wo_ref,
             dgl_ref, doat_ref, dobt_ref, mg_ref, dwa_ref, dwb_ref):
        @pl.when(pl.program_id(0) == 0)
        def _():
            dwa_ref[...] = jnp.zeros(dwa_ref.shape, F32)
            dwb_ref[...] = jnp.zeros(dwb_ref.shape, F32)

        dm = _dot_nt(dy_ref[...], wo_ref[...])
        ga = _sigmoid(jnp.concatenate([g0[...], g1[...]], axis=1))
        gb = _sigmoid(jnp.concatenate([g2[...], g3[...]], axis=1))
        pa, pb = _heads_matmul(oat_ref, wba_ref), _heads_matmul(obt_ref, wbb_ref)
        mg_ref[...] = (ga * pa + gb * pb).astype(BF16)
        dgl_ref[:, :D_MODEL] = (dm * pa * ga * (1.0 - ga)).astype(BF16)
        dgl_ref[:, D_MODEL:] = (dm * pb * gb * (1.0 - gb)).astype(BF16)
        dpa, dpb = (dm * ga).astype(BF16), (dm * gb).astype(BF16)
        for h in range(N_HEADS):
            doat_ref[h] = _dot_nt(wba_ref[h], dpa).astype(BF16)
            dobt_ref[h] = _dot_nt(wbb_ref[h], dpb).astype(BF16)
            dwa_ref[h] += _dot(oat_ref[h], dpa)
            dwb_ref[h] += _dot(obt_ref[h], dpb)

    hts = pl.BlockSpec((N_HEADS, LANES, tm), lambda i: (0, 0, i))
    row = pl.BlockSpec((tm, D_MODEL), lambda i: (i, 0))
    wh = _full((N_HEADS, LANES, D_MODEL))
    return _call(
        body, name="merge_bwd", grid=(s // tm,),
        in_specs=[row, hts, hts, *_gate_specs(tm), wh, wh, _full((D_MODEL, D_MODEL))],
        out_specs=[pl.BlockSpec((tm, 2 * D_MODEL), lambda i: (i, 0)), hts, hts, row, wh, wh],
        out_shape=[_sds((s, 2 * D_MODEL), BF16), _sds((N_HEADS, LANES, s), BF16), _sds((N_HEADS, LANES, s), BF16),
                   _sds((s, D_MODEL), BF16), _sds((N_HEADS, LANES, D_MODEL)), _sds((N_HEADS, LANES, D_MODEL))],
        sem=("arbitrary",),
    )(dy, oat, obt, proj, proj, proj, proj, wba, wbb, w_out)


FF_TC = 256


def _shift_rows(t, prev_row, next_row):
    n = t.shape[0]
    r = lax.broadcasted_iota(jnp.int32, t.shape, 0)
    up = jnp.where(r == 0, prev_row, pltpu.roll(t, 1, 0))
    dn = jnp.where(r == n - 1, next_row, pltpu.roll(t, n - 1, 0))
    return up, dn


HALO = 16


def _halo_specs(tm, s, tc):
    nb = s // HALO
    main = pl.BlockSpec((2, tm, tc), lambda j, i: (0, i, j))
    prev = pl.BlockSpec((2, HALO, tc), lambda j, i: (0, jnp.maximum(i * (tm // HALO) - 1, 0), j))
    nxt = pl.BlockSpec((2, HALO, tc), lambda j, i: (0, jnp.minimum((i + 1) * (tm // HALO), nb - 1), j))
    return main, prev, nxt


def _halo_rows(prev_ref, next_ref, half, i, n_i):
    prev_row = jnp.where(i == 0, 0.0, prev_ref[half, HALO - 1:HALO, :].astype(F32))
    next_row = jnp.where(i == n_i - 1, 0.0, next_ref[half, 0:1, :].astype(F32))
    return prev_row, next_row


def _conv(t, prev_row, next_row, w, b):
    up, dn = _shift_rows(t, prev_row, next_row)
    return w[0:1, :] * up + w[1:2, :] * t + w[2:3, :] * dn + b


def _ffn_act_fwd(u, cw, cb, *, tm):
    _, s, ff = u.shape
    n_i = s // tm

    def body(u_ref, up_ref, un_ref, cw_ref, cb_ref, a_ref):
        i = pl.program_id(1)
        gc = _conv(u_ref[0].astype(F32), *_halo_rows(up_ref, un_ref, 0, i, n_i), cw_ref[0], cb_ref[0])
        vc = _conv(u_ref[1].astype(F32), *_halo_rows(up_ref, un_ref, 1, i, n_i), cw_ref[1], cb_ref[1])
        a_ref[...] = (gc * _sigmoid(gc) * vc).astype(BF16)

    main, prev, nxt = _halo_specs(tm, s, FF_TC)
    return _call(
        body, name="ffn_act_fwd", grid=(ff // FF_TC, n_i),
        in_specs=[main, prev, nxt, pl.BlockSpec((2, 3, FF_TC), lambda j, i: (0, 0, j)),
                  pl.BlockSpec((2, 1, FF_TC), lambda j, i: (0, 0, j))],
        out_specs=pl.BlockSpec((tm, FF_TC), lambda j, i: (i, j)),
        out_shape=_sds((s, ff), BF16), sem=("parallel", "parallel"),
    )(u, u, u, cw, cb)


def _ffn_act_bwd(dy2, w_down, u, cw, cb, *, tm):
    _, s, ff = u.shape
    n_i = s // tm

    def body(dy_ref, wd_ref, u_ref, up_ref, un_ref, cw_ref, cb_ref, dc_ref, dcw_ref, dcb_ref):
        i = pl.program_id(1)

        @pl.when(i == 0)
        def _():
            dcw_ref[...] = jnp.zeros(dcw_ref.shape, F32)
            dcb_ref[...] = jnp.zeros(dcb_ref.shape, F32)

        da = _dot_nt(dy_ref[...], wd_ref[...])
        ug, uv = u_ref[0].astype(F32), u_ref[1].astype(F32)
        ugp, ugn = _shift_rows(ug, *_halo_rows(up_ref, un_ref, 0, i, n_i))
        uvp, uvn = _shift_rows(uv, *_halo_rows(up_ref, un_ref, 1, i, n_i))
        wg, wv = cw_ref[0], cw_ref[1]
        gc = wg[0:1, :] * ugp + wg[1:2, :] * ug + wg[2:3, :] * ugn + cb_ref[0]
        vc = wv[0:1, :] * uvp + wv[1:2, :] * uv + wv[2:3, :] * uvn + cb_ref[1]
        sg = _sigmoid(gc)
        dg = da * vc * sg * (1.0 + gc * (1.0 - sg))
        dv = da * gc * sg
        dc_ref[0] = dg.astype(BF16)
        dc_ref[1] = dv.astype(BF16)
        for half, (d, taps) in enumerate(((dg, (ugp, ug, ugn)), (dv, (uvp, uv, uvn)))):
            for tap in range(3):
                dcw_ref[half, tap:tap + 1, :] += jnp.sum(d * taps[tap], axis=0, keepdims=True)
            dcb_ref[half] += jnp.sum(d, axis=0, keepdims=True)

    main, prev, nxt = _halo_specs(tm, s, FF_TC)
    return _call(
        body, name="ffn_act_bwd", grid=(ff // FF_TC, n_i),
        in_specs=[pl.BlockSpec((tm, D_MODEL), lambda j, i: (i, 0)), pl.BlockSpec((FF_TC, D_MODEL), lambda j, i: (j, 0)),
                  main, prev, nxt, pl.BlockSpec((2, 3, FF_TC), lambda j, i: (0, 0, j)),
                  pl.BlockSpec((2, 1, FF_TC), lambda j, i: (0, 0, j))],
        out_specs=[main, pl.BlockSpec((2, 3, FF_TC), lambda j, i: (0, 0, j)),
                   pl.BlockSpec((2, 1, FF_TC), lambda j, i: (0, 0, j))],
        out_shape=[_sds((2, s, ff), BF16), _sds((2, 3, ff)), _sds((2, 1, ff))],
        sem=("parallel", "arbitrary"),
    )(dy2, w_down, u, u, u, cw, cb)


def _conv_bwd_input(dc, cw, *, tm):
    _, s, ff = dc.shape
    n_i = s // tm

    def body(d_ref, dp_ref, dn_ref, cw_ref, du_ref):
        i = pl.program_id(1)
        for half in range(2):
            d = d_ref[half].astype(F32)
            up, dn = _shift_rows(d, *_halo_rows(dp_ref, dn_ref, half, i, n_i))
            w = cw_ref[half]
            du_ref[half] = (w[0:1, :] * dn + w[1:2, :] * d + w[2:3, :] * up).astype(BF16)

    main, prev, nxt = _halo_specs(tm, s, FF_TC)
    return _call(
        body, name="conv_bwd_input", grid=(ff // FF_TC, n_i),
        in_specs=[main, prev, nxt, pl.BlockSpec((2, 3, FF_TC), lambda j, i: (0, 0, j))],
        out_specs=main, out_shape=_sds((2, s, ff), BF16), sem=("parallel", "parallel"),
    )(dc, dc, dc, cw)


def _ffn_down_loss(a, w_down, x1, target, gate2, ln_g, ln_b, *, tm):
    s, ff = a.shape
    n_i = s // tm

    def body(a_ref, wd_ref, x1_ref, tg_ref, gt_ref, lg_ref, lb_ref, ls_ref, dy_ref, dx_ref, dg_ref, db_ref, dgt_ref):
        @pl.when(pl.program_id(0) == 0)
        def _():
            dg_ref[...] = jnp.zeros(dg_ref.shape, F32)
            db_ref[...] = jnp.zeros(db_ref.shape, F32)
            dgt_ref[...] = jnp.zeros(dgt_ref.shape, F32)

        y2 = _dot(a_ref[...], wd_ref[...])
        z = ALPHA * x1_ref[...] + gt_ref[...] * y2
        mu = jnp.mean(z, axis=-1, keepdims=True)
        zc = z - mu
        r = lax.rsqrt(jnp.mean(zc * zc, axis=-1, keepdims=True) + LN_EPS)
        xhat = zc * r
        diff = xhat * lg_ref[...] + lb_ref[...] - tg_ref[...]
        ls_ref[...] = jnp.full(ls_ref.shape, 0.5 / D_MODEL * jnp.sum(diff * diff), F32)
        dx2 = diff * (1.0 / D_MODEL)
        dg_ref[...] += jnp.sum(dx2 * xhat, axis=0, keepdims=True)
        db_ref[...] += jnp.sum(dx2, axis=0, keepdims=True)
        dz = _ln_bwd(dx2, xhat, r, lg_ref[...])
        dgt_ref[...] += jnp.sum(dz * y2, axis=0, keepdims=True)
        dy_ref[...] = (gt_ref[...] * dz).astype(BF16)
        dx_ref[...] = ALPHA * dz

    row = pl.BlockSpec((tm, D_MODEL), lambda i: (i, 0))
    vec = _full((1, D_MODEL))
    return _call(
        body, name="ffn_down_loss", grid=(n_i,),
        in_specs=[pl.BlockSpec((tm, ff), lambda i: (i, 0)), _full((ff, D_MODEL)), row, row, vec, vec, vec],
        out_specs=[pl.BlockSpec((None, 8, LANES), lambda i: (i, 0, 0)), row, row, vec, vec, vec],
        out_shape=[_sds((n_i, 8, LANES)), _sds((s, D_MODEL), BF16), _sds((s, D_MODEL)),
                   _sds((1, D_MODEL)), _sds((1, D_MODEL)), _sds((1, D_MODEL))],
        sem=("arbitrary",),
    )(a, w_down, x1, target, gate2, ln_g, ln_b)


def _ffn_up_bwd(du, wup4, dx1a, x1, scale2, x, y, mu1, r1, gate1, ln_g, *, tm):
    s = x.shape[0]
    nb, _, ns = wup4.shape

    def body(du_ref, w_ref, dxa_ref, x1_ref, sc_ref, x_ref, y_ref, mu_ref, r_ref, gt_ref, lg_ref,
             dxo_ref, dy_ref, dsc_ref, dsh_ref, dg_ref, db_ref, dgt_ref, acc):
        i, k = pl.program_id(0), pl.program_id(1)

        @pl.when(jnp.logical_and(i == 0, k == 0))
        def _():
            for ref in (dsc_ref, dsh_ref, dg_ref, db_ref, dgt_ref):
                ref[...] = jnp.zeros(ref.shape, F32)

        @pl.when(k == 0)
        def _():
            acc[...] = jnp.zeros(acc.shape, F32)

        acc[...] += _dot_nt(du_ref[...], w_ref[...])

        @pl.when(k == nb - 1)
        def _():
            dh = acc[...]
            x1 = x1_ref[...]
            dsc_ref[...] += jnp.sum(dh * x1, axis=0, keepdims=True)
            dsh_ref[...] += jnp.sum(dh, axis=0, keepdims=True)
            dx1 = dxa_ref[...] + dh * (1.0 + sc_ref[...])
            yv = y_ref[...]
            xhat = (ALPHA * x_ref[...] + gt_ref[...] * yv - mu_ref[...]) * r_ref[...]
            dg_ref[...] += jnp.sum(dx1 * xhat, axis=0, keepdims=True)
            db_ref[...] += jnp.sum(dx1, axis=0, keepdims=True)
            dz = _ln_bwd(dx1, xhat, r_ref[...], lg_ref[...])
            dgt_ref[...] += jnp.sum(dz * yv, axis=0, keepdims=True)
            dy_ref[...] = (gt_ref[...] * dz).astype(BF16)
            dxo_ref[...] = ALPHA * dz

    row = pl.BlockSpec((tm, D_MODEL), lambda i, k: (i, 0))
    col = pl.BlockSpec((tm, 1), lambda i, k: (i, 0))
    vec = _full((1, D_MODEL))
    return _call(
        body, name="ffn_up_bwd", grid=(s // tm, nb),
        in_specs=[pl.BlockSpec((None, tm, ns), lambda i, k: (k // 2, i, k % 2)),
                  pl.BlockSpec((None, D_MODEL, ns), lambda i, k: (k, 0, 0)),
                  row, row, vec, row, row, col, col, vec, vec],
        out_specs=[row, row, vec, vec, vec, vec, vec],
        out_shape=[_sds((s, D_MODEL)), _sds((s, D_MODEL), BF16)] + [_sds((1, D_MODEL))] * 5,
        scratch=[pltpu.VMEM((tm, D_MODEL), F32)],
        sem=("arbitrary", "arbitrary"),
    )(du, wup4, dx1a, x1, scale2, x, y, mu1, r1, gate1, ln_g)


def _mm_nt4_mod_bwd(dp, w4, dxa, x, scale, *, tm, name):
    m = x.shape[0]
    nb, kdim, ns = w4.shape

    def body(dp_ref, w_ref, dxa_ref, x_ref, sc_ref, dx_ref, dsc_ref, dsh_ref, acc):
        i, k = pl.program_id(0), pl.program_id(1)

        @pl.when(jnp.logical_and(i == 0, k == 0))
        def _():
            dsc_ref[...] = jnp.zeros(dsc_ref.shape, F32)
            dsh_ref[...] = jnp.zeros(dsh_ref.shape, F32)

        @pl.when(k == 0)
        def _():
            acc[...] = jnp.zeros(acc.shape, F32)

        acc[...] += _dot_nt(dp_ref[...], w_ref[...])

        @pl.when(k == nb - 1)
        def _():
            dh = acc[...]
            dsc_ref[...] += jnp.sum(dh * x_ref[...], axis=0, keepdims=True)
            dsh_ref[...] += jnp.sum(dh, axis=0, keepdims=True)
            dx_ref[...] = dxa_ref[...] + dh * (1.0 + sc_ref[...])

    row = pl.BlockSpec((tm, kdim), lambda i, k: (i, 0))
    vec = _full((1, kdim))
    return _call(
        body, name=name, grid=(m // tm, nb),
        in_specs=[pl.BlockSpec((tm, ns), lambda i, k: (i, k)), pl.BlockSpec((None, kdim, ns), lambda i, k: (k, 0, 0)),
                  row, row, vec],
        out_specs=[row, vec, vec],
        out_shape=[_sds((m, kdim)), _sds((1, kdim)), _sds((1, kdim))],
        scratch=[pltpu.VMEM((tm, kdim), F32)],
        sem=("arbitrary", "arbitrary"),
    )(dp, w4, dxa, x, scale)


def _pad_heads_w(w):
    w8 = w.reshape(N_HEADS, HEAD_DIM, w.shape[-1])
    z = jnp.zeros_like(w8)
    first = (jnp.arange(N_HEADS) < N_HEADS // N_KV)[:, None, None]
    return jnp.where(first, jnp.concatenate([w8, z], axis=1), jnp.concatenate([z, w8], axis=1))


def _unpad_heads_w(g):
    first = (jnp.arange(N_HEADS) < N_HEADS // N_KV)[:, None, None]
    return jnp.where(first, g[:, :HEAD_DIM], g[:, HEAD_DIM:]).reshape(N_HEADS * HEAD_DIM, g.shape[-1])


def _ones_beside(vt):
    half = vt.shape[0] // 2
    ones = jnp.ones((half, vt.shape[1]), vt.dtype)
    return jnp.stack([jnp.concatenate([vt[:half], ones], axis=0), jnp.concatenate([ones, vt[half:]], axis=0)])


def _rep8(a):
    return jnp.broadcast_to(a.reshape(1, -1), (8, a.size))


def _first_row(a):
    r8 = _rep8(a)
    return jnp.where(lax.broadcasted_iota(jnp.int32, r8.shape, 0) == 0, r8, 0.0)


def _to_blocks4(w):
    k, n = w.shape
    return w.reshape(k, N_CHIPS, n // N_CHIPS).transpose(1, 0, 2)


def _local_step(x, c, ctx, c_ctx, wmod4, b_mod, win4, b_in, sink, qn, kn, wba, wbb, w_out, ln1_g, ln1_b,
                wup4, cw, cb, w_down, ln2_g, ln2_b, target):
    s, nc = x.shape[0], ctx.shape[0]
    tm = min(512, s)
    tm2 = min(256, s)
    tl = min(1024, s)
    zvec = jnp.zeros((1, D_MODEL), F32)

    cc = jnp.concatenate([_rep8(c), _rep8(c_ctx)], axis=0)
    mods = _mm_nn4(cc, zvec, zvec, wmod4, b_mod, mode="silu", split_out=False, out_dtype=F32, tm=16, name="mod_vectors")
    shift1, scale1, gate1, shift2, scale2, gate2 = [mods[0:1, i * D_MODEL:(i + 1) * D_MODEL] for i in range(6)]
    shift_c, scale_c = mods[8:9, :D_MODEL], mods[8:9, D_MODEL:2 * D_MODEL]

    cos, sin = _rope_tables(s)
    cos_c, sin_c = jnp.ones((nc, LANES), F32), jnp.zeros((nc, LANES), F32)
    qg, kg = jnp.tile(qn, (1, 2)), jnp.tile(kn, (1, 2))

    proj_c = _mm_nn4(ctx, shift_c, scale_c, win4, b_in, mode="modulate", split_out=False, out_dtype=F32, tm=nc,
                     name="in_proj_ctx")
    _, kac, vac, _, kbc, vbc = _prep(proj_c, cos_c, sin_c, qg, kg, tm=nc, name="prep_ctx")
    proj = _mm_nn4(x, shift1, scale1, win4, b_in, mode="modulate", split_out=False, out_dtype=F32, tm=tl, name="in_proj")
    qat, ka, va, qbt, kb, vb = _prep(proj, cos, sin, qg, kg, tm=tm, name="prep")
    oat, lse_a = _attn_win_fwd(qat, ka, _ones_beside(va.T), kac, _ones_beside(vac.T), sink, tq=tm)
    obt, lse_b, mrun_b, pbt = _attn_glob_fwd(qbt, kb, _ones_beside(vb.T), kbc, _ones_beside(vbc.T), tq=tm,
                                             tk=min(1024, s))
    wba_p, wbb_p = _pad_heads_w(wba), _pad_heads_w(wbb)
    x1, y, mu1, r1 = _merge_fwd(oat, obt, proj, x, gate1, wba_p, wbb_p, w_out, ln1_g, ln1_b, tm=tm)
    u = _mm_nn4(x1, shift2, scale2, wup4, jnp.zeros((1, 2 * D_FF), F32), mode="modulate", split_out=True,
                out_dtype=BF16, tm=tl, name="ffn_up")
    cw2 = cw.reshape(3, 2, D_FF).transpose(1, 0, 2)
    cb2 = cb.reshape(2, 1, D_FF)
    a = _ffn_act_fwd(u, cw2, cb2, tm=tl)
    ls, dy2, dx1a, dln2_g, dln2_b, dgate2 = _ffn_down_loss(a, w_down, x1, target, gate2, ln2_g, ln2_b, tm=tm)
    loss = jnp.sum(ls[:, 0, 0])

    n_s = s // tm
    dw_down = _mm_tn(a, dy2, a_spec=pl.BlockSpec((tm, D_FF), lambda t: (t, 0)),
                     b_spec=pl.BlockSpec((tm, D_MODEL), lambda t: (t, 0)), grid=(n_s,),
                     out_shape=_sds((D_FF, D_MODEL)), out_spec=_full((D_FF, D_MODEL)), name="dw_down")
    dc, dcw2, dcb2 = _ffn_act_bwd(dy2, w_down, u, cw2, cb2, tm=tl)
    du = _conv_bwd_input(dc, cw2, tm=tl)
    dxz1, dy, dscale2, dshift2, dln1_g, dln1_b, dgate1 = _ffn_up_bwd(
        du, wup4, dx1a, x1, scale2, x, y, mu1, r1, gate1, ln1_g, tm=tm)
    ns_up = wup4.shape[-1]
    dw_up4 = _mm_tn(x1, du, a_spec=pl.BlockSpec((tm, D_MODEL), lambda k, t: (t, 0)),
                    b_spec=pl.BlockSpec((None, tm, ns_up), lambda k, t: (k // 2, t, k % 2)), grid=(N_CHIPS, n_s),
                    out_shape=_sds((N_CHIPS, D_MODEL, ns_up)),
                    out_spec=pl.BlockSpec((None, D_MODEL, ns_up), lambda k, t: (k, 0, 0)),
                    mod=(shift2, scale2), name="dw_up")

    dgl, doat, dobt, merged, dwba_p, dwbb_p = _merge_bwd(dy, oat, obt, proj, wba_p, wbb_p, w_out, tm=tm2)
    dwba, dwbb = _unpad_heads_w(dwba_p), _unpad_heads_w(dwbb_p)
    rowspec = pl.BlockSpec((tm, D_MODEL), lambda t: (t, 0))
    dw_out = _mm_tn(merged, dy, a_spec=rowspec, b_spec=rowspec, grid=(n_s,), out_shape=_sds((D_MODEL, D_MODEL)),
                    out_spec=_full((D_MODEL, D_MODEL)), name="dw_out")

    dqat, dkat, dvat, dkact, dvact, dsk = _attn_win_bwd(qat, doat, oat, lse_a, ka, ka.T, va, kac, kac.T, vac, sink, tq=tm)
    dka, dva, dkac, dvac = dkat.T, dvat.T, dkact.T, dvact.T
    dqbt, dkbt, dvbt, dkbct, dvbct = _attn_glob_bwd(qbt, dobt, obt, lse_b, mrun_b, pbt, kb.T, vb, kbc.T, vbc, tq=tm, tk=tm)
    dkb, dvb, dkbc, dvbc = dkbt.T, dvbt.T, dkbct.T, dvbct.T
    dsink = jnp.sum(dsk[:, :, 0, 0], axis=1)

    dproj, dqg, dkg = _prep_bwd(dqat, dka, dva, dqbt, dkb, dvb, proj, cos, sin, qg, kg, dgl, tm=tm, name="prep_bwd")
    grad_x, dscale1, dshift1 = _mm_nt4_mod_bwd(dproj, win4, dxz1, x, scale1, tm=tl, name="in_proj_bwd")
    ns_in = win4.shape[-1]
    win_spec = dict(b_spec=pl.BlockSpec((None, None, ns_in), lambda k, t: (0, 0, k)),
                    out_shape=_sds((N_CHIPS, D_MODEL, ns_in)),
                    out_spec=pl.BlockSpec((None, D_MODEL, ns_in), lambda k, t: (k, 0, 0)),
                    colsum_spec=pl.BlockSpec((8, ns_in), lambda k, t: (0, k)), colsum_shape=_sds((8, IN_COLS)))
    win_spec["b_spec"] = pl.BlockSpec((tm, ns_in), lambda k, t: (t, k))
    dw_in4, db_in = _mm_tn(x, dproj, a_spec=pl.BlockSpec((tm, D_MODEL), lambda k, t: (t, 0)), grid=(N_CHIPS, n_s),
                           mod=(shift1, scale1), name="dw_in", **win_spec)

    zq = jnp.zeros((N_HEADS, LANES, nc), F32)
    dproj_c, _, dkg_c = _prep_bwd(zq, dkac, dvac, zq, dkbc, dvbc, proj_c, cos_c, sin_c, qg, kg,
                                  jnp.zeros((nc, IN_COLS - OFF_GA), BF16), tm=nc, name="prep_bwd_ctx")
    _, dscale_c, dshift_c = _mm_nt4_mod_bwd(dproj_c, win4, jnp.zeros((nc, D_MODEL), F32), ctx, scale_c, tm=nc,
                                            name="in_proj_bwd_ctx")
    win_spec["b_spec"] = pl.BlockSpec((nc, ns_in), lambda k, t: (t, k))
    dw_in4, db_in_c = _mm_tn(ctx, dproj_c, a_spec=pl.BlockSpec((nc, D_MODEL), lambda k, t: (t, 0)), grid=(N_CHIPS, 1),
                             mod=(shift_c, scale_c), init=dw_in4, name="dw_in_ctx", **win_spec)

    dmod = jnp.concatenate([dshift1, dscale1, dgate1, dshift2, dscale2, dgate2], axis=1)
    dmodc = jnp.concatenate([dshift_c, dscale_c], axis=1)
    dmodc_pad = jnp.concatenate([dmodc, jnp.zeros((1, 4 * D_MODEL), F32)], axis=1)
    dmodc8 = _first_row(dmodc_pad).astype(BF16)
    z8 = jnp.zeros((8, D_MODEL), F32)
    dsilu_c, _, _ = _mm_nt4_mod_bwd(dmodc8, wmod4, z8, z8, zvec, tm=8, name="c_ctx_bwd")
    sg = _sigmoid(c_ctx)
    dc_ctx = dsilu_c[0:1] * sg * (1.0 + c_ctx * (1.0 - sg))

    dqn = jnp.sum(dqg.reshape(N_HEADS, HEAD_DIM), axis=0, keepdims=True)
    dkn = jnp.sum((dkg + dkg_c).reshape(N_KV, HEAD_DIM), axis=0, keepdims=True)
    grads = dict(
        w_in4=dw_in4, b_in=db_in[0:1] + db_in_c[0:1], sink=dsink, qn=dqn, kn=dkn, wba=dwba, wbb=dwbb, w_out=dw_out,
        ln1_g=dln1_g, ln1_b=dln1_b, w_up4=dw_up4, conv_w=dcw2.transpose(1, 0, 2).reshape(3, 2 * D_FF),
        conv_b=dcb2.reshape(1, 2 * D_FF), w_down=dw_down, ln2_g=dln2_g, ln2_b=dln2_b,
        c_ctx=dc_ctx, dmod=dmod, dmodc=dmodc)
    return loss, grad_x, grads


ANY = pl.BlockSpec(memory_space=pl.ANY)


def _mesh_pos():
    return lax.axis_index("x"), lax.axis_index("y"), lax.axis_index("c")


def _other_chips(x, y):
    return [(1 - x, y), (x, 1 - y), (1 - x, 1 - y)]


def _remote(src, dst, send, recv, dev):
    return pltpu.make_async_remote_copy(src_ref=src, dst_ref=dst, send_sem=send, recv_sem=recv, device_id=dev,
                                        device_id_type=MESH)


def _set_block(stack, block, k):
    return lax.dynamic_update_slice(stack, block[None], (k,) + (0,) * block.ndim)


def _gather_shards(arrs, small):
    na = len(arrs)
    halves = [a.shape[0] // 2 for a in arrs]

    def body(*refs):
        ins, small_ref = refs[:na], refs[na]
        outs, small_out = refs[na + 1:2 * na + 1], refs[2 * na + 1]
        send, recv = refs[2 * na + 2:]
        x, y, c = _mesh_pos()
        me = 2 * x + y
        chips = _other_chips(x, y)

        def half(a, cc):
            return pl.ds(cc * halves[a], halves[a])

        sends = []
        for j, chip in enumerate(chips):
            for a in range(na):
                sends.append(_remote(ins[a].at[half(a, c)], outs[a].at[me, half(a, c)], send.at[a, j], recv.at[a, j],
                                     (*chip, c)))
            sends.append(_remote(small_ref, small_out.at[me], send.at[na, j], recv.at[na, j], (*chip, c)))
        for cp in sends:
            cp.start()
        for j, chip in enumerate(chips):
            kj = 2 * chip[0] + chip[1]
            for a in range(na):
                landed = outs[a].at[kj, half(a, c)]
                _remote(landed, landed, send.at[a, j], recv.at[a, j], (*chip, c)).wait_recv()
                fwd = _remote(landed, landed, send.at[a, 3 + j], recv.at[a, 3 + j], (x, y, 1 - c))
                fwd.start()
                sends.append(fwd)
            _remote(small_ref, small_out.at[kj], send.at[na, j], recv.at[na, j], (*chip, c)).wait_recv()
        for j, chip in enumerate(chips):
            kj = 2 * chip[0] + chip[1]
            for a in range(na):
                other = outs[a].at[kj, half(a, 1 - c)]
                _remote(other, other, send.at[a, 3 + j], recv.at[a, 3 + j], (x, y, 1 - c)).wait_recv()
        for cp in sends:
            cp.wait_send()

    out_shape = [_sds((N_CHIPS,) + a.shape, a.dtype) for a in arrs] + [_sds((N_CHIPS,) + small.shape, small.dtype)]
    got = pl.pallas_call(
        body, name="gather_shards", in_specs=[ANY] * (na + 1), out_specs=[ANY] * (na + 1), out_shape=out_shape,
        scratch_shapes=[pltpu.SemaphoreType.DMA((na + 1, 6)), pltpu.SemaphoreType.DMA((na + 1, 6))],
    )(*arrs, small)
    xp, yp, _ = _mesh_pos()
    return [_set_block(g, a, 2 * xp + yp) for g, a in zip(got, list(arrs) + [small])]


def _allgather_rows(v):
    r, n = v.shape

    def body(v_ref, out_ref, send, recv, loc):
        x, y, c = _mesh_pos()
        me, sibling = (x, y, c), (x, y, 1 - c)
        chips = _other_chips(x, y)

        def rows(px, py, pc):
            return out_ref.at[4 * px + 2 * py + pc]

        def copy(k, block, to, src=None):
            return _remote(rows(*block) if src is None else src, rows(*block), send.at[k], recv.at[k], to)

        mine = pltpu.make_async_copy(v_ref, rows(*me), loc)
        mine.start()
        first = [copy(0, me, sibling, src=v_ref)] + [copy(1 + j, me, (*chip, c), src=v_ref) for j, chip in enumerate(chips)]
        for cp in first:
            cp.start()
        passed = [copy(4 + j, (*chip, c), sibling) for j, chip in enumerate(chips)]
        for j, chip in enumerate(chips):
            copy(1 + j, (*chip, c), me).wait_recv()
            passed[j].start()
        copy(0, sibling, me).wait_recv()
        for j, chip in enumerate(chips):
            copy(4 + j, (*chip, 1 - c), me).wait_recv()
        for cp in first + passed:
            cp.wait_send()
        mine.wait()

    return pl.pallas_call(
        body, name="allgather_rows", in_specs=[pl.BlockSpec(memory_space=pltpu.VMEM)],
        out_specs=pl.BlockSpec(memory_space=pltpu.VMEM), out_shape=_sds((N_DEV, r, n), v.dtype),
        scratch_shapes=[pltpu.SemaphoreType.DMA((7,)), pltpu.SemaphoreType.DMA((7,)), pltpu.SemaphoreType.DMA],
    )(v)


def _swap_other_half(g):
    nb, r, n = g.shape
    rh = r // 2

    def body(g_ref, out_ref, send, recv):
        x, y, c = _mesh_pos()
        cp = _remote(g_ref.at[:, pl.ds((1 - c) * rh, rh), :], out_ref, send, recv, (x, y, 1 - c))
        cp.start()
        cp.wait()

    return pl.pallas_call(
        body, name="swap_other_half", in_specs=[ANY], out_specs=ANY, out_shape=_sds((nb, rh, n), g.dtype),
        scratch_shapes=[pltpu.SemaphoreType.DMA, pltpu.SemaphoreType.DMA],
    )(g)


def _scatter_to_chips(p):
    def body(p_ref, out_ref, send, recv):
        x, y, c = _mesh_pos()
        me = 2 * x + y
        chips = _other_chips(x, y)
        sends = [_remote(p_ref.at[2 * chip[0] + chip[1]], out_ref.at[me], send.at[j], recv.at[j], (*chip, c))
                 for j, chip in enumerate(chips)]
        for cp in sends:
            cp.start()
        for j, chip in enumerate(chips):
            kj = 2 * chip[0] + chip[1]
            _remote(p_ref.at[kj], out_ref.at[kj], send.at[j], recv.at[j], (*chip, c)).wait_recv()
        for cp in sends:
            cp.wait_send()

    got = pl.pallas_call(
        body, name="scatter_to_chips", in_specs=[ANY], out_specs=ANY, out_shape=_sds(p.shape, p.dtype),
        scratch_shapes=[pltpu.SemaphoreType.DMA((3,)), pltpu.SemaphoreType.DMA((3,))],
    )(p)
    xp, yp, _ = _mesh_pos()
    me = 2 * xp + yp
    return _set_block(got, lax.dynamic_index_in_dim(p, me, axis=0, keepdims=False), me)


def _join_halves(f):
    def body(f_ref, out_ref, send, recv):
        x, y, c = _mesh_pos()
        cp = _remote(f_ref, out_ref, send, recv, (x, y, 1 - c))
        cp.start()
        cp.wait()

    other = pl.pallas_call(
        body, name="join_halves", in_specs=[ANY], out_specs=ANY, out_shape=_sds(f.shape, f.dtype),
        scratch_shapes=[pltpu.SemaphoreType.DMA, pltpu.SemaphoreType.DMA],
    )(f)
    first = lax.axis_index("c") == 0
    return jnp.concatenate([jnp.where(first, f, other), jnp.where(first, other, f)], axis=0)


def _row_tile(rows, cap=512):
    t = cap - cap % 8
    while rows % t:
        t -= 8
    return t


def _add_blocks(a, b, out_dtype):
    nb, r, n = a.shape
    tr = _row_tile(r)

    def body(a_ref, b_ref, o_ref):
        o_ref[...] = (a_ref[...] + b_ref[...]).astype(out_dtype)

    spec = pl.BlockSpec((None, tr, n), lambda k, i: (k, i, 0))
    return _call(body, name="add_blocks", grid=(nb, r // tr), in_specs=[spec, spec], out_specs=spec,
                 out_shape=_sds(a.shape, out_dtype), sem=("parallel", "parallel"))(a, b)


def _sum_leading(a, *, name):
    nk, r, n = a.shape
    tr = _row_tile(r)

    def body(a_ref, o_ref):
        acc = a_ref[0].astype(F32)
        for k in range(1, nk):
            acc = acc + a_ref[k].astype(F32)
        o_ref[...] = acc

    return _call(body, name=name, grid=(r // tr,), in_specs=[pl.BlockSpec((nk, tr, n), lambda i: (0, i, 0))],
                 out_specs=pl.BlockSpec((tr, n), lambda i: (i, 0)), out_shape=_sds((r, n)), sem=("parallel",))(a)


def _silu_outer(a, b):
    kdim, n = a.shape[1], b.shape[1]

    def body(a_ref, b_ref, o_ref):
        av = a_ref[...]
        av = av * _sigmoid(av)
        bv = b_ref[...]
        ah, bh = av.astype(BF16), bv.astype(BF16)
        al, bl = (av - ah.astype(F32)).astype(BF16), (bv - bh.astype(F32)).astype(BF16)
        o_ref[...] = _dot_tn(ah, bh) + (_dot_tn(ah, bl) + _dot_tn(al, bh))

    return _call(body, name="dw_mod", grid=(1,), in_specs=[_full(a.shape), _full(b.shape)], out_specs=_full((kdim, n)),
                 out_shape=_sds((kdim, n)))(a, b)


def _adamw(w, g, m, v):
    r, n = w.shape
    tr = _row_tile(r)

    def body(w_ref, g_ref, m_ref, v_ref, d_ref, nm_ref, nv_ref):
        gv = g_ref[...]
        nm = ADAM_B1 * m_ref[...] + (1.0 - ADAM_B1) * gv
        nv = ADAM_B2 * v_ref[...] + (1.0 - ADAM_B2) * (gv * gv)
        m_hat = nm / (1.0 - ADAM_B1 ** ADAM_STEP)
        v_hat = nv / (1.0 - ADAM_B2 ** ADAM_STEP)
        d_ref[...] = -ADAM_LR * (m_hat / (jnp.sqrt(v_hat) + ADAM_EPS) + ADAM_WD * w_ref[...])
        nm_ref[...] = nm
        nv_ref[...] = nv

    spec = pl.BlockSpec((tr, n), lambda i: (i, 0))
    return _call(body, name="adamw", grid=(r // tr,), in_specs=[spec] * 4, out_specs=[spec] * 3,
                 out_shape=[_sds((r, n))] * 3, sem=("parallel",))(w, g, m, v)


BIG = ("w_in", "w_branch_a", "w_branch_b", "w_out", "w_up", "w_down", "conv_w")
BIG_ROWS = 3584
MATRICES = ("w_mod", "w_in", "w_branch_a", "w_branch_b", "w_out", "w_up", "w_down")
SMALL = ("b_mod", "b_in", "conv_b", "ln1_g", "ln1_b", "ln2_g", "ln2_b", "c_ctx", "attn_sink", "q_norm_g", "k_norm_g", "conv_w")
SMALL_ROWS = 8 * len(SMALL)


def _rows(a, n_rows):
    flat = a.reshape(-1)
    return jnp.pad(flat, (0, n_rows * D_MODEL - flat.shape[0])).reshape(n_rows, D_MODEL)


def _group8(a):
    return _rep8(_rows(a, 1)) if a.size <= D_MODEL else _rows(a, 8)


def _ungroup8(p, shape):
    size = math.prod(shape)
    return (p[0, :size] if size <= D_MODEL else p.reshape(-1)[:size]).reshape(shape)


def _unpack_big(p, like):
    out, r = {}, 0
    for n in BIG:
        size = math.prod(like[n].shape)
        nr = size // D_MODEL if n != "conv_w" else 8
        out[n] = p[r:r + nr].reshape(-1)[:size].reshape(like[n].shape)
        r += nr
    return out


def _pack_small(t):
    return jnp.concatenate([_group8(t[n]) for n in SMALL], axis=0)


def _unpack_small(p, like):
    return {n: _ungroup8(p[8 * i:8 * i + 8], like[n].shape) for i, n in enumerate(SMALL)}


WEIGHTS = ("c_ctx", "w_mod", "b_mod", "w_in", "b_in", "attn_sink", "q_norm_g", "k_norm_g", "w_branch_a", "w_branch_b",
           "w_out", "ln1_g", "ln1_b", "w_up", "conv_w", "conv_b", "w_down", "ln2_g", "ln2_b")


def kernel(x, c, ctx, c_ctx, w_mod, b_mod, w_in, b_in, attn_sink, q_norm_g, k_norm_g, w_branch_a, w_branch_b, w_out, ln1_g, ln1_b, w_up, conv_w, conv_b, w_down, ln2_g, ln2_b, loss_target, m_c_ctx, m_w_mod, m_b_mod, m_w_in, m_b_in, m_attn_sink, m_q_norm_g, m_k_norm_g, m_w_branch_a, m_w_branch_b, m_w_out, m_ln1_g, m_ln1_b, m_w_up, m_conv_w, m_conv_b, m_w_down, m_ln2_g, m_ln2_b, v_c_ctx, v_w_mod, v_b_mod, v_w_in, v_b_in, v_attn_sink, v_q_norm_g, v_k_norm_g, v_w_branch_a, v_w_branch_b, v_w_out, v_ln1_g, v_ln1_b, v_w_up, v_conv_w, v_conv_b, v_w_down, v_ln2_g, v_ln2_b):
    w = dict(c_ctx=c_ctx, w_mod=w_mod, b_mod=b_mod, w_in=w_in, b_in=b_in, attn_sink=attn_sink, q_norm_g=q_norm_g,
             k_norm_g=k_norm_g, w_branch_a=w_branch_a, w_branch_b=w_branch_b, w_out=w_out, ln1_g=ln1_g, ln1_b=ln1_b,
             w_up=w_up, conv_w=conv_w, conv_b=conv_b, w_down=w_down, ln2_g=ln2_g, ln2_b=ln2_b)
    m = dict(c_ctx=m_c_ctx, w_mod=m_w_mod, b_mod=m_b_mod, w_in=m_w_in, b_in=m_b_in, attn_sink=m_attn_sink,
             q_norm_g=m_q_norm_g, k_norm_g=m_k_norm_g, w_branch_a=m_w_branch_a, w_branch_b=m_w_branch_b, w_out=m_w_out,
             ln1_g=m_ln1_g, ln1_b=m_ln1_b, w_up=m_w_up, conv_w=m_conv_w, conv_b=m_conv_b, w_down=m_w_down,
             ln2_g=m_ln2_g, ln2_b=m_ln2_b)
    v = dict(c_ctx=v_c_ctx, w_mod=v_w_mod, b_mod=v_b_mod, w_in=v_w_in, b_in=v_b_in, attn_sink=v_attn_sink,
             q_norm_g=v_q_norm_g, k_norm_g=v_k_norm_g, w_branch_a=v_w_branch_a, w_branch_b=v_w_branch_b, w_out=v_w_out,
             ln1_g=v_ln1_g, ln1_b=v_ln1_b, w_up=v_w_up, conv_w=v_conv_w, conv_b=v_conv_b, w_down=v_w_down,
             ln2_g=v_ln2_g, ln2_b=v_ln2_b)
    xp, yp, _ = _mesh_pos()
    me = 2 * xp + yp

    branches = jnp.concatenate([w_branch_a[0], w_branch_b[0]], axis=0)
    wide = jnp.concatenate([w_mod[0], w_in[0], w_up[0], branches], axis=1).astype(BF16)
    tall = jnp.concatenate([w_out[0], w_down[0]], axis=0).astype(BF16)
    wide4, tall4, cw4 = _gather_shards([wide, tall], conv_w[0])
    n_mod, n_in, n_up = w_mod.shape[-1], w_in.shape[-1], w_up.shape[-1]
    wmod4 = wide4[:, :, :n_mod]
    win4 = wide4[:, :, n_mod:n_mod + n_in]
    wup4 = wide4[:, :, n_mod + n_in:n_mod + n_in + n_up]
    br4 = wide4[:, :, n_mod + n_in + n_up:]
    n_br = w_branch_a.shape[1]
    wba = br4[:, :n_br].transpose(1, 0, 2).reshape(n_br, D_MODEL)
    wbb = br4[:, n_br:].transpose(1, 0, 2).reshape(n_br, D_MODEL)
    n_out = w_out.shape[1]
    w_out_full = tall4[:, :n_out].reshape(D_MODEL, D_MODEL)
    w_down_full = tall4[:, n_out:].reshape(D_FF, D_MODEL)
    cw_full = cw4.transpose(1, 0, 2).reshape(3, 2 * D_FF)

    loss, grad_x, g = _local_step(
        x[0], c, ctx[0], c_ctx[None], wmod4, b_mod, win4, b_in, attn_sink[0], q_norm_g, k_norm_g, wba, wbb, w_out_full,
        ln1_g, ln1_b, wup4, cw_full, conv_b, w_down_full, ln2_g, ln2_b, loss_target[0])
    loss = lax.psum(loss, ("x", "y", "c"))

    sent = dict(c=c, dmod=g["dmod"], dmodc=g["dmodc"], b_in=g["b_in"], conv_b=g["conv_b"], ln1_g=g["ln1_g"],
                ln1_b=g["ln1_b"], ln2_g=g["ln2_g"], ln2_b=g["ln2_b"], c_ctx=g["c_ctx"], attn_sink=g["sink"],
                q_norm_g=g["qn"], k_norm_g=g["kn"])
    every = _allgather_rows(jnp.concatenate([_group8(a) for a in sent.values()], axis=0))
    total = _sum_leading(every, name="sum_devices")
    slot = {n: slice(8 * i, 8 * i + 8) for i, n in enumerate(sent)}
    gs = {n: _ungroup8(total[slot[n]], sent[n].shape) for n in SMALL if n in sent}
    dmodc_sum = jnp.concatenate([_ungroup8(total[slot["dmodc"]], (1, 2 * D_MODEL)), jnp.zeros((1, 4 * D_MODEL), F32)],
                                axis=1)
    gs["b_mod"] = _ungroup8(total[slot["dmod"]], b_mod.shape) + dmodc_sum
    acts = jnp.concatenate([every[:, slot["c"].start], _rep8(c_ctx)], axis=0)
    dmods = jnp.concatenate([every[:, slot["dmod"]].reshape(N_DEV, -1)[:, :6 * D_MODEL], _first_row(dmodc_sum)], axis=0)
    g_w_mod = _silu_outer(acts, lax.dynamic_slice_in_dim(dmods, me * n_mod, n_mod, axis=1))

    cw_g4 = _to_blocks4(g["conv_w"])
    parts = [
        g["w_in4"].reshape(N_CHIPS, -1, D_MODEL), _to_blocks4(g["wba"]).reshape(N_CHIPS, -1, D_MODEL),
        _to_blocks4(g["wbb"]).reshape(N_CHIPS, -1, D_MODEL), g["w_out"].reshape(N_CHIPS, -1, D_MODEL),
        g["w_up4"].reshape(N_CHIPS, -1, D_MODEL), g["w_down"].reshape(N_CHIPS, -1, D_MODEL),
        jnp.pad(cw_g4.reshape(N_CHIPS, -1), ((0, 0), (0, 8 * D_MODEL - cw_g4.shape[1] * cw_g4.shape[2]))).reshape(
            N_CHIPS, 8, D_MODEL)]
    used = sum(p.shape[1] for p in parts)
    packed = jnp.concatenate(parts + [jnp.zeros((N_CHIPS, BIG_ROWS - used, D_MODEL), F32)], axis=1)
    rh = BIG_ROWS // 2
    cpos = lax.axis_index("c")
    my_half = lax.dynamic_slice_in_dim(packed, cpos * rh, rh, axis=1)
    chip_sum = _add_blocks(my_half, _swap_other_half(packed), BF16)
    half_sum = _sum_leading(_scatter_to_chips(chip_sum), name="sum_chips")
    g_big = _unpack_big(_join_halves(half_sum), w)

    grads = dict(gs, w_mod=g_w_mod, **g_big)
    grads = {n: grads[n].reshape(w[n].shape) for n in WEIGHTS}
    delta, new_m, new_v = {}, {}, {}
    for n in MATRICES:
        outs = _adamw(*[t[n][0] for t in (w, grads, m, v)])
        delta[n], new_m[n], new_v[n] = [o[None] for o in outs]
    outs = _adamw(*[_pack_small(t) for t in (w, grads, m, v)])
    for res, o in zip((delta, new_m, new_v), outs):
        res.update(_unpack_small(o, w))
    return (loss, grad_x[None], *[grads[n] for n in WEIGHTS], *[delta[n] for n in WEIGHTS],
            *[new_m[n] for n in WEIGHTS], *[new_v[n] for n in WEIGHTS])
```

```python
import functools
import math

import jax
import jax.numpy as jnp
from jax import lax
from jax.experimental import pallas as pl
from jax.experimental.pallas import tpu as pltpu

F32 = jnp.float32
BF16 = jnp.bfloat16

D_MODEL = 1024
HEAD_DIM = 64
N_HEADS = 8
N_KV = 2
WINDOW = 128
GRID_W = 64
ROPE_THETA = 10000.0
D_FF = 2816
LN_EPS = 1e-5
QK_EPS = 1e-6
ALPHA = 2.0 ** 0.25
Q_SCALE = HEAD_DIM ** -0.5
OFF_GA = 1536
IN_COLS = 3584
ADAM_LR, ADAM_B1, ADAM_B2, ADAM_EPS, ADAM_WD, ADAM_STEP = 0.001, 0.9, 0.999, 1e-8, 0.01, 10

LANES = 128
VMEM_BUDGET = 52 * 1024 * 1024
N_CHIPS = 4
N_DEV = 8
NEG = -1e30
MESH = pl.DeviceIdType.MESH


def _sigmoid(x):
    return 1.0 / (1.0 + jnp.exp(-x))


def _dot(a, b):
    return jnp.dot(a, b, preferred_element_type=F32)


def _dot_nt(a, b):
    return lax.dot_general(a, b, (((1,), (1,)), ((), ())), preferred_element_type=F32)


def _dot_tn(a, b):
    return lax.dot_general(a, b, (((0,), (0,)), ((), ())), preferred_element_type=F32)


def _call(body, *, name, grid, in_specs, out_specs, out_shape, scratch=(), sem=None, **kw):
    params = dict(vmem_limit_bytes=VMEM_BUDGET)
    if sem is not None:
        params["dimension_semantics"] = sem
    return pl.pallas_call(body, name=name, grid=grid, in_specs=in_specs, out_specs=out_specs,
                          out_shape=out_shape, scratch_shapes=list(scratch),
                          compiler_params=pltpu.CompilerParams(**params), **kw)


def _full(shape):
    n = len(shape)
    return pl.BlockSpec(shape, lambda *_: (0,) * n)


def _sds(shape, dtype=F32):
    return jax.ShapeDtypeStruct(shape, dtype)


def _mm_nn4(a, shift, scale, w4, bias, *, mode, split_out, out_dtype, tm, name):
    m, kdim = a.shape
    nb, _, ns = w4.shape

    def body(a_ref, sh_ref, sc_ref, w_ref, b_ref, o_ref):
        av = a_ref[...]
        if mode == "modulate":
            av = av * (1.0 + sc_ref[...]) + sh_ref[...]
        else:
            av = av * _sigmoid(av)
        o_ref[...] = (_dot(av.astype(BF16), w_ref[...]) + b_ref[...]).astype(out_dtype)

    if split_out:
        out_shape = _sds((2, m, 2 * ns), out_dtype)
        out_spec = pl.BlockSpec((None, tm, ns), lambda i, k: (k // 2, i, k % 2))
    else:
        out_shape = _sds((m, nb * ns), out_dtype)
        out_spec = pl.BlockSpec((tm, ns), lambda i, k: (i, k))
    return _call(
        body, name=name, grid=(m // tm, nb),
        in_specs=[pl.BlockSpec((tm, kdim), lambda i, k: (i, 0)),
                  pl.BlockSpec((1, kdim), lambda i, k: (0, 0)),
                  pl.BlockSpec((1, kdim), lambda i, k: (0, 0)),
                  pl.BlockSpec((None, kdim, ns), lambda i, k: (k, 0, 0)),
                  pl.BlockSpec((1, ns), lambda i, k: (0, k))],
        out_specs=out_spec, out_shape=out_shape, sem=("parallel", "arbitrary"),
    )(a, shift, scale, w4, bias)


def _mm_tn(a, b, *, a_spec, b_spec, grid, out_shape, out_spec, name, mod=None, init=None, colsum_spec=None,
           colsum_shape=None):
    red = len(grid) - 1
    has_mod, has_init, has_cs = mod is not None, init is not None, colsum_spec is not None

    def body(*refs):
        refs = list(refs)
        a_ref, b_ref = refs[0], refs[1]
        pos = 2
        if has_mod:
            sh_ref, sc_ref = refs[2], refs[3]
            pos = 4
        if has_init:
            init_ref = refs[pos]
            pos += 1
        o_ref = refs[pos]
        cs_ref = refs[pos + 1] if has_cs else None
        s = pl.program_id(red)

        @pl.when(s == 0)
        def _():
            o_ref[...] = init_ref[...] if has_init else jnp.zeros(o_ref.shape, F32)
            if has_cs:
                cs_ref[...] = jnp.zeros(cs_ref.shape, F32)

        av = a_ref[...]
        if has_mod:
            av = av * (1.0 + sc_ref[...]) + sh_ref[...]
        bv = b_ref[...]
        o_ref[...] += _dot_tn(av.astype(BF16), bv)
        if has_cs:
            cs_ref[...] += jnp.broadcast_to(jnp.sum(bv.astype(F32), axis=0, keepdims=True), cs_ref.shape)

    ins, in_specs = [a, b], [a_spec, b_spec]
    if has_mod:
        kdim = mod[0].shape[-1]
        ins += list(mod)
        in_specs += [_full((1, kdim)), _full((1, kdim))]
    if has_init:
        ins.append(init)
        in_specs.append(out_spec)
    out_specs, out_shapes = out_spec, out_shape
    if has_cs:
        out_specs, out_shapes = [out_spec, colsum_spec], [out_shape, colsum_shape]
    sem = ("parallel",) * red + ("arbitrary",)
    return _call(body, name=name, grid=grid, in_specs=in_specs, out_specs=out_specs, out_shape=out_shapes,
                 sem=sem)(*ins)


def _rope_tables(n_tok):
    pos = jnp.arange(n_tok, dtype=jnp.int32)
    rows = (pos // GRID_W).astype(F32)
    cols = (pos % GRID_W).astype(F32)
    n_freq = HEAD_DIM // 4
    inv_freq = ROPE_THETA ** (-jnp.arange(n_freq, dtype=F32) / n_freq)
    ang_r = rows[:, None] * inv_freq
    ang_c = cols[:, None] * inv_freq
    cos = jnp.concatenate([jnp.cos(ang_r)] * 2 + [jnp.cos(ang_c)] * 2, axis=-1)
    sin = jnp.concatenate([-jnp.sin(ang_r), jnp.sin(ang_r), -jnp.sin(ang_c), jnp.sin(ang_c)], axis=-1)
    return jnp.tile(cos, (1, 2)), jnp.tile(sin, (1, 2))


def _lane(shape):
    return lax.broadcasted_iota(jnp.int32, shape, 1)


def _rope_partner(t, lane):
    return jnp.where((lane % 32) < 16, pltpu.roll(t, LANES - 16, 1), pltpu.roll(t, 16, 1))


def _half_mean(s, lane):
    lo = jnp.sum(jnp.where(lane < HEAD_DIM, s, 0.0), axis=-1, keepdims=True)
    hi = jnp.sum(jnp.where(lane < HEAD_DIM, 0.0, s), axis=-1, keepdims=True)
    return jnp.where(lane < HEAD_DIM, lo, hi) * (1.0 / HEAD_DIM)


def _prep(proj, cos, sin, qg, kg, *, tm, name):
    m = proj.shape[0]

    def body(p_ref, cos_ref, sin_ref, qg_ref, kg_ref, qa_ref, ka_ref, va_ref, qb_ref, kb_ref, vb_ref):
        lane = _lane((tm, LANES))
        cosv, sinv = cos_ref[...], sin_ref[...]
        low = lane < HEAD_DIM

        def rope(t):
            return t * cosv + _rope_partner(t, lane) * sinv

        def rms(t, g):
            return t * lax.rsqrt(_half_mean(t * t, lane) + QK_EPS) * g

        def place(q_ref, j, chunk):
            sw = pltpu.roll(chunk, HEAD_DIM, 1)
            if j < 2:
                h0, h1 = jnp.where(low, chunk, 0.0), jnp.where(low, sw, 0.0)
            else:
                h0, h1 = jnp.where(low, 0.0, sw), jnp.where(low, 0.0, chunk)
            q_ref[2 * j] = h0.T.astype(BF16)
            q_ref[2 * j + 1] = h1.T.astype(BF16)

        for j in range(4):
            place(qa_ref, j, rope(p_ref[:, j * LANES:(j + 1) * LANES]) * Q_SCALE)
            place(qb_ref, j, rope(rms(p_ref[:, 768 + j * LANES:768 + (j + 1) * LANES], qg_ref[...])) * Q_SCALE)
        ka_ref[...] = rope(p_ref[:, 512:640]).astype(BF16)
        va_ref[...] = p_ref[:, 640:768].astype(BF16)
        kb_ref[...] = rope(rms(p_ref[:, 1280:1408], kg_ref[...])).astype(BF16)
        vb_ref[...] = p_ref[:, 1408:1536].astype(BF16)

    row = pl.BlockSpec((tm, LANES), lambda i: (i, 0))
    qspec = pl.BlockSpec((N_HEADS, LANES, tm), lambda i: (0, 0, i))
    return _call(
        body, name=name, grid=(m // tm,),
        in_specs=[pl.BlockSpec((tm, OFF_GA), lambda i: (i, 0)), row, row, _full((1, LANES)), _full((1, LANES))],
        out_specs=[qspec, row, row, qspec, row, row],
        out_shape=[_sds((N_HEADS, LANES, m), BF16), _sds((m, LANES), BF16), _sds((m, LANES), BF16),
                   _sds((N_HEADS, LANES, m), BF16), _sds((m, LANES), BF16), _sds((m, LANES), BF16)],
        sem=("parallel",),
    )(proj, cos, sin, qg, kg)


def _prep_bwd(dqa, dka, dva, dqb, dkb, dvb, proj, cos, sin, qg, kg, dgl, *, tm, name):
    m = proj.shape[0]

    def body(dqa_ref, dka_ref, dva_ref, dqb_ref, dkb_ref, dvb_ref, p_ref, cos_ref, sin_ref, qg_ref, kg_ref,
             dgl_ref, dp_ref, dqg_ref, dkg_ref):
        i = pl.program_id(0)
        lane = _lane((tm, LANES))
        cosv, sinv = cos_ref[...], sin_ref[...]
        low = lane < HEAD_DIM

        @pl.when(i == 0)
        def _():
            dqg_ref[...] = jnp.zeros(dqg_ref.shape, F32)
            dkg_ref[...] = jnp.zeros(dkg_ref.shape, F32)

        def unrope(d):
            return d * cosv - _rope_partner(d, lane) * sinv

        def unplace(dq_ref, j):
            d0, d1 = dq_ref[2 * j].T, dq_ref[2 * j + 1].T
            if j < 2:
                return jnp.where(low, d0, pltpu.roll(d1, HEAD_DIM, 1))
            return jnp.where(low, pltpu.roll(d0, HEAD_DIM, 1), d1)

        def unrms(dtn, t, g):
            r = lax.rsqrt(_half_mean(t * t, lane) + QK_EPS)
            u = dtn * g
            dt = r * u - t * (r * r * r) * _half_mean(u * t, lane)
            return dt, jnp.sum(dtn * t * r, axis=0, keepdims=True)

        for j in range(4):
            dp_ref[:, j * LANES:(j + 1) * LANES] = (unrope(unplace(dqa_ref, j)) * Q_SCALE).astype(BF16)
            c0 = 768 + j * LANES
            dt, dg = unrms(unrope(unplace(dqb_ref, j)) * Q_SCALE, p_ref[:, c0:c0 + LANES], qg_ref[...])
            dp_ref[:, c0:c0 + LANES] = dt.astype(BF16)
            dqg_ref[:, j * LANES:(j + 1) * LANES] += dg
        dp_ref[:, 512:640] = unrope(dka_ref[...]).astype(BF16)
        dp_ref[:, 640:768] = dva_ref[...].astype(BF16)
        dt, dg = unrms(unrope(dkb_ref[...]), p_ref[:, 1280:1408], kg_ref[...])
        dp_ref[:, 1280:1408] = dt.astype(BF16)
        dkg_ref[...] += dg
        dp_ref[:, 1408:1536] = dvb_ref[...].astype(BF16)
        dp_ref[:, OFF_GA:] = dgl_ref[...]

    row = pl.BlockSpec((tm, LANES), lambda i: (i, 0))
    qspec = pl.BlockSpec((N_HEADS, LANES, tm), lambda i: (0, 0, i))
    return _call(
        body, name=name, grid=(m // tm,),
        in_specs=[qspec, row, row, qspec, row, row, pl.BlockSpec((tm, OFF_GA), lambda i: (i, 0)), row, row,
                  _full((1, LANES)), _full((1, LANES)), pl.BlockSpec((tm, IN_COLS - OFF_GA), lambda i: (i, 0))],
        out_specs=[pl.BlockSpec((tm, IN_COLS), lambda i: (i, 0)), _full((1, 512)), _full((1, LANES))],
        out_shape=[_sds((m, IN_COLS), BF16), _sds((1, 512)), _sds((1, LANES))],
        sem=("arbitrary",),
    )(dqa, dka, dva, dqb, dkb, dvb, proj, cos, sin, qg, kg, dgl)


def _attn_glob_fwd(qt, k, vt, kc, vct, *, tq, tk):
    nh, _, s = qt.shape
    nc = kc.shape[0]
    n_chunks = s // tk
    half = LANES // 2

    def body(qt_ref, k_ref, vt_ref, kc_ref, vct_ref, ot_ref, lse_ref, mrun_ref, p_hbm,
             acc_sc, st_sc, stage_sc, stagec_sc, sems, semc):
        h, i = pl.program_id(0), pl.program_id(1)
        qtv = qt_ref[...]
        acc_sc[...] = jnp.zeros(acc_sc.shape, F32)

        def p_out(slot, c):
            return pltpu.make_async_copy(stage_sc.at[slot], p_hbm.at[h, i, pl.ds(pl.multiple_of(c * tk, tk), tk), :],
                                         sems.at[slot])

        def update(st, vtv, m_old):
            m_new = jnp.maximum(m_old, jnp.max(st, axis=0, keepdims=True))
            pb = jnp.exp(st - m_new).astype(BF16)
            acc_sc[...] = acc_sc[...] * jnp.exp(m_old - m_new) + _dot(vtv, pb)
            return m_new, pb

        m, pbc = update(_dot(kc_ref[...], qtv), vct_ref[...], jnp.full((1, tq), NEG, F32))
        mrun_ref[pl.ds(n_chunks, 1), :] = m
        stagec_sc[...] = pbc
        ctx_out = pltpu.make_async_copy(stagec_sc, p_hbm.at[h, i, pl.ds(s, nc), :], semc)
        ctx_out.start()

        def step(c, st, m_old):
            slot = c % 2
            off = pl.multiple_of(c * tk, tk)
            nxt = pl.multiple_of(jnp.minimum(c + 1, n_chunks - 1) * tk, tk)
            st_next = _dot(k_ref[pl.ds(nxt, tk), :], qtv)
            m_new, pb = update(st, vt_ref[:, pl.ds(off, tk)], m_old)
            mrun_ref[pl.ds(c, 1), :] = m_new
            stage_sc[slot] = pb
            p_out(slot, c).start()
            return st_next, m_new

        def loop(c, m_old):
            st_next, m_new = step(c, st_sc[...], m_old)
            p_out(1 - c % 2, c - 1).wait()
            st_sc[...] = st_next
            return m_new

        stage_sc[1] = jnp.zeros((tk, tq), BF16)
        pltpu.make_async_copy(stage_sc.at[1], p_hbm.at[h, i, pl.ds(s + nc, tk), :], sems.at[1]).start()
        st_sc[...] = _dot(k_ref[pl.ds(0, tk), :], qtv)
        m = lax.fori_loop(0, n_chunks, loop, m)
        p_out((n_chunks - 1) % 2, n_chunks - 1).wait()
        ctx_out.wait()
        acc = acc_sc[...]
        l = jnp.where(h < nh // N_KV, acc[half:half + 1], acc[0:1])
        ot_ref[...] = (acc / l).astype(BF16)
        lse_ref[...] = m + jnp.log(l)

    grp = nh // N_KV
    return _call(
        body, name="attn_glob_fwd", grid=(nh, s // tq),
        in_specs=[pl.BlockSpec((None, LANES, tq), lambda h, i: (h, 0, i)), _full((s, LANES)),
                  pl.BlockSpec((None, LANES, s), lambda h, i: (h // grp, 0, 0)), _full((nc, LANES)),
                  pl.BlockSpec((None, LANES, nc), lambda h, i: (h // grp, 0, 0))],
        out_specs=[pl.BlockSpec((None, LANES, tq), lambda h, i: (h, 0, i)),
                   pl.BlockSpec((None, 1, tq), lambda h, i: (h, 0, i)),
                   pl.BlockSpec((None, n_chunks + 1, tq), lambda h, i: (h, 0, i)), ANY],
        out_shape=[_sds((nh, LANES, s), BF16), _sds((nh, 1, s)), _sds((nh, n_chunks + 1, s)),
                   _sds((nh, s // tq, s + nc + tk, tq), BF16)],
        scratch=[pltpu.VMEM((LANES, tq), F32), pltpu.VMEM((tk, tq), F32), pltpu.VMEM((2, tk, tq), BF16),
                 pltpu.VMEM((nc, tq), BF16), pltpu.SemaphoreType.DMA((2,)), pltpu.SemaphoreType.DMA],
        sem=("parallel", "parallel"),
    )(qt, k, vt, kc, vct)


P_AHEAD = 3


def _attn_glob_bwd(qt, dot, ot, lse, mrun, p, kt, v, kct, vc, *, tq, tk):
    nh, _, s = qt.shape
    nc = vc.shape[0]
    n_q = s // tq
    n_chunks = s // tk
    n_run = mrun.shape[1] - 1
    per_run = n_chunks // n_run

    def body(qt_ref, dot_ref, ot_ref, lse_ref, mrun_ref, p_hbm, kt_ref, v_ref, kct_ref, vc_ref,
             dqt_ref, dkt_ref, dvt_ref, dkct_ref, dvct_ref, acc_sc, dp_sc, dkt_sc, dvt_sc, p_sc, pc_sc, sems, semc):
        h, i = pl.program_id(0), pl.program_id(1)

        @pl.when(jnp.logical_and(h == 0, i == 0))
        def _():
            dkct_ref[...] = jnp.zeros(dkct_ref.shape, F32)
            dvct_ref[...] = jnp.zeros(dvct_ref.shape, F32)
            dkt_sc[...] = jnp.zeros(dkt_sc.shape, F32)
            dvt_sc[...] = jnp.zeros(dvt_sc.shape, F32)


        def p_in(slot, c):
            return pltpu.make_async_copy(p_hbm.at[h, i, pl.ds(pl.multiple_of(c * tk, tk), tk), :], p_sc.at[slot],
                                         sems.at[slot])

        ctx_in = pltpu.make_async_copy(p_hbm.at[h, i, pl.ds(s, nc), :], pc_sc, semc)
        ctx_in.start()
        for c in range(P_AHEAD):
            p_in(c, min(c, n_chunks - 1)).start()
        qtv, dotv, lse = qt_ref[...], dot_ref[...], lse_ref[...]
        delta = jnp.sum(dotv.astype(F32) * ot_ref[...].astype(F32), axis=0, keepdims=True)

        def grads(pt_stored, m_row, dpt):
            pt = pt_stored.astype(F32) * jnp.exp(m_row - lse)
            return pt.astype(BF16), (pt * (dpt - delta)).astype(BF16)

        dp_sc[...] = _dot(v_ref[pl.ds(0, tk), :], dotv)
        ctx_in.wait()
        pb, dsb = grads(pc_sc[...], mrun_ref[pl.ds(n_run, 1), :], _dot(vc_ref[...], dotv))
        acc_sc[...] = _dot(kct_ref[...], dsb)
        dkct_ref[...] += _dot_nt(qtv, dsb)
        dvct_ref[...] += _dot_nt(dotv, pb)

        def loop(c, carry):
            slot = c % (P_AHEAD + 1)
            off = pl.multiple_of(c * tk, tk)
            nxt = pl.multiple_of(jnp.minimum(c + 1, n_chunks - 1) * tk, tk)
            p_in(slot, c).wait()
            p_in((c + P_AHEAD) % (P_AHEAD + 1), jnp.minimum(c + P_AHEAD, n_chunks - 1)).start()
            dpt = dp_sc[...]
            dp_next = _dot(v_ref[pl.ds(nxt, tk), :], dotv)
            pb, dsb = grads(p_sc[slot], mrun_ref[pl.ds(c // per_run, 1), :], dpt)
            acc_sc[...] += _dot(kt_ref[:, pl.ds(off, tk)], dsb)
            dkt_sc[:, pl.ds(off, tk)] += _dot_nt(qtv, dsb)
            dvt_sc[:, pl.ds(off, tk)] += _dot_nt(dotv, pb)
            dp_sc[...] = dp_next
            return carry

        lax.fori_loop(0, n_chunks, loop, 0)
        for c in range(n_chunks, n_chunks + P_AHEAD):
            p_in(c % (P_AHEAD + 1), n_chunks - 1).wait()
        dqt_ref[...] = acc_sc[...]

        @pl.when(jnp.logical_and(h == nh - 1, i == n_q - 1))
        def _():
            pltpu.sync_copy(dkt_sc, dkt_ref)
            pltpu.sync_copy(dvt_sc, dvt_ref)

    qs = pl.BlockSpec((None, LANES, tq), lambda h, i: (h, 0, i))
    rs = pl.BlockSpec((None, 1, tq), lambda h, i: (h, 0, i))
    return _call(
        body, name="attn_glob_bwd", grid=(nh, n_q),
        in_specs=[qs, qs, qs, rs, pl.BlockSpec((None, n_run + 1, tq), lambda h, i: (h, 0, i)), ANY,
                  _full((LANES, s)), _full((s, LANES)), _full((LANES, nc)), _full((nc, LANES))],
        out_specs=[qs, ANY, ANY, _full((LANES, nc)), _full((LANES, nc))],
        out_shape=[_sds((nh, LANES, s)), _sds((LANES, s)), _sds((LANES, s)), _sds((LANES, nc)), _sds((LANES, nc))],
        scratch=[pltpu.VMEM((LANES, tq), F32), pltpu.VMEM((tk, tq), F32), pltpu.VMEM((LANES, s), F32),
                 pltpu.VMEM((LANES, s), F32), pltpu.VMEM((P_AHEAD + 1, tk, tq), BF16), pltpu.VMEM((nc, tq), BF16),
                 pltpu.SemaphoreType.DMA((P_AHEAD + 1,)), pltpu.SemaphoreType.DMA],
        sem=("arbitrary", "arbitrary"),
    )(qt, dot, ot, lse, mrun, p, kt, v, kct, vc)


WIN_SPAN = 2 * WINDOW


def _band(rows0, cols0, shape):
    r = rows0 + lax.broadcasted_iota(jnp.int32, shape, 0)
    c = cols0 + lax.broadcasted_iota(jnp.int32, shape, 1)
    return jnp.abs(r - c) <= WINDOW


def _win_start(blk, t, s):
    return pl.multiple_of(jnp.clip(blk * t - WINDOW, 0, s - t - WIN_SPAN), WINDOW)


def _attn_win_fwd(qt, k, vt, kc, vct, sink, *, tq):
    nh, _, s = qt.shape
    nc = kc.shape[0]
    tw = tq + WIN_SPAN
    half = LANES // 2
    grp = nh // N_KV

    def body(sink_ref, qt_ref, k_ref, vt_ref, kc_ref, vct_ref, ot_ref, lse_ref):
        h, i = pl.program_id(0), pl.program_id(1)
        k0 = _win_start(i, tq, s)
        qtv = qt_ref[...]
        st = jnp.where(_band(k0, i * tq, (tw, tq)), _dot(k_ref[pl.ds(k0, tw), :], qtv), NEG)
        stc = _dot(kc_ref[...], qtv)
        snk = sink_ref[h]
        m = jnp.maximum(jnp.maximum(jnp.max(st, axis=0, keepdims=True), jnp.max(stc, axis=0, keepdims=True)), snk)
        acc = (_dot(vt_ref[:, pl.ds(k0, tw)], jnp.exp(st - m).astype(BF16))
               + _dot(vct_ref[...], jnp.exp(stc - m).astype(BF16)))
        l = jnp.where(h < grp, acc[half:half + 1], acc[0:1]) + jnp.exp(snk - m)
        ot_ref[...] = (acc / l).astype(BF16)
        lse_ref[...] = m + jnp.log(l)

    return _call(
        body, name="attn_win_fwd", grid=(nh, s // tq),
        in_specs=[pl.BlockSpec(memory_space=pltpu.SMEM),
                  pl.BlockSpec((None, LANES, tq), lambda h, i: (h, 0, i)), _full((s, LANES)),
                  pl.BlockSpec((None, LANES, s), lambda h, i: (h // grp, 0, 0)), _full((nc, LANES)),
                  pl.BlockSpec((None, LANES, nc), lambda h, i: (h // grp, 0, 0))],
        out_specs=[pl.BlockSpec((None, LANES, tq), lambda h, i: (h, 0, i)),
                   pl.BlockSpec((None, 1, tq), lambda h, i: (h, 0, i))],
        out_shape=[_sds((nh, LANES, s), BF16), _sds((nh, 1, s))],
        sem=("parallel", "parallel"),
    )(sink, qt, k, vt, kc, vct)


def _attn_win_bwd(qt, dot, ot, lse, k, kt, v, kc, kct, vc, sink, *, tq):
    nh, _, s = qt.shape
    nc = kc.shape[0]
    tw = tq + WIN_SPAN
    nq = s // tq

    def body(sink_ref, qt_ref, dot_ref, ot_ref, lse_ref, k_ref, kt_ref, v_ref, kc_ref, kct_ref, vc_ref,
             dqt_ref, dkt_ref, dvt_ref, dkct_ref, dvct_ref, dsk_ref, dkt_sc, dvt_sc):
        h, i = pl.program_id(0), pl.program_id(1)

        @pl.when(jnp.logical_and(h == 0, i == 0))
        def _():
            dkct_ref[...] = jnp.zeros(dkct_ref.shape, F32)
            dvct_ref[...] = jnp.zeros(dvct_ref.shape, F32)
            dkt_sc[...] = jnp.zeros(dkt_sc.shape, F32)
            dvt_sc[...] = jnp.zeros(dvt_sc.shape, F32)

        k0 = _win_start(i, tq, s)
        span = pl.ds(k0, tw)
        qtv, dotv, lse = qt_ref[...], dot_ref[...], lse_ref[...]
        delta = jnp.sum(dotv.astype(F32) * ot_ref[...].astype(F32), axis=0, keepdims=True)
        pt = jnp.where(_band(k0, i * tq, (tw, tq)), jnp.exp(_dot(k_ref[span, :], qtv) - lse), 0.0)
        dsb = (pt * (_dot(v_ref[span, :], dotv) - delta)).astype(BF16)
        pct = jnp.exp(_dot(kc_ref[...], qtv) - lse)
        dscb = (pct * (_dot(vc_ref[...], dotv) - delta)).astype(BF16)
        dqt_ref[...] = _dot(kt_ref[:, span], dsb) + _dot(kct_ref[...], dscb)
        dkt_sc[:, span] += _dot_nt(qtv, dsb)
        dvt_sc[:, span] += _dot_nt(dotv, pt.astype(BF16))
        dkct_ref[...] += _dot_nt(qtv, dscb)
        dvct_ref[...] += _dot_nt(dotv, pct.astype(BF16))
        dsk = -jnp.sum(jnp.exp(sink_ref[h] - lse) * delta)
        dsk_ref[...] = jnp.full(dsk_ref.shape, dsk, F32)

        @pl.when(jnp.logical_and(h == nh - 1, i == nq - 1))
        def _():
            pltpu.sync_copy(dkt_sc, dkt_ref)
            pltpu.sync_copy(dvt_sc, dvt_ref)

    qs = pl.BlockSpec((None, LANES, tq), lambda h, i: (h, 0, i))
    rs = pl.BlockSpec((None, 1, tq), lambda h, i: (h, 0, i))
    return _call(
        body, name="attn_win_bwd", grid=(nh, nq),
        in_specs=[pl.BlockSpec(memory_space=pltpu.SMEM), qs, qs, qs, rs, _full((s, LANES)), _full((LANES, s)),
                  _full((s, LANES)), _full((nc, LANES)), _full((LANES, nc)), _full((nc, LANES))],
        out_specs=[qs, ANY, ANY, _full((LANES, nc)), _full((LANES, nc)),
                   pl.BlockSpec((None, None, 8, LANES), lambda h, i: (h, i, 0, 0))],
        out_shape=[_sds((nh, LANES, s)), _sds((LANES, s)), _sds((LANES, s)), _sds((LANES, nc)), _sds((LANES, nc)),
                   _sds((nh, nq, 8, LANES))],
        scratch=[pltpu.VMEM((LANES, s), F32), pltpu.VMEM((LANES, s), F32)],
        sem=("arbitrary", "arbitrary"),
    )(sink, qt, dot, ot, lse, k, kt, v, kc, kct, vc)


def _ln_fwd(z, g, b):
    mu = jnp.mean(z, axis=-1, keepdims=True)
    zc = z - mu
    r = lax.rsqrt(jnp.mean(zc * zc, axis=-1, keepdims=True) + LN_EPS)
    return zc * r * g + b, mu, r


def _ln_bwd(dy, xhat, r, g):
    dxh = dy * g
    return r * (dxh - jnp.mean(dxh, axis=-1, keepdims=True) - xhat * jnp.mean(dxh * xhat, axis=-1, keepdims=True))


def _heads_matmul(ot_ref, w_ref):
    acc = _dot_tn(ot_ref[0], w_ref[0])
    for h in range(1, N_HEADS):
        acc += _dot_tn(ot_ref[h], w_ref[h])
    return acc


def _gate_specs(tm):
    return [pl.BlockSpec((tm, 512), functools.partial(lambda i, b: (i, b), b=OFF_GA // 512 + b)) for b in range(4)]


def _merge_fwd(oat, obt, proj, x, gate1, wba, wbb, w_out, ln_g, ln_b, *, tm):
    s = x.shape[0]

    def body(oa_ref, ob_ref, g0, g1, g2, g3, x_ref, gt_ref, wba_ref, wbb_ref, wo_ref, lg_ref, lb_ref,
             x1_ref, y_ref, mu_ref, r_ref):
        ga = _sigmoid(jnp.concatenate([g0[...], g1[...]], axis=1))
        gb = _sigmoid(jnp.concatenate([g2[...], g3[...]], axis=1))
        merged = ga * _heads_matmul(oa_ref, wba_ref) + gb * _heads_matmul(ob_ref, wbb_ref)
        y = _dot(merged.astype(BF16), wo_ref[...])
        x1, mu, r = _ln_fwd(ALPHA * x_ref[...] + gt_ref[...] * y, lg_ref[...], lb_ref[...])
        x1_ref[...] = x1
        y_ref[...] = y
        mu_ref[...] = mu
        r_ref[...] = r

    hts = pl.BlockSpec((N_HEADS, LANES, tm), lambda i: (0, 0, i))
    row = pl.BlockSpec((tm, D_MODEL), lambda i: (i, 0))
    col = pl.BlockSpec((tm, 1), lambda i: (i, 0))
    vec = _full((1, D_MODEL))
    wh = _full((N_HEADS, LANES, D_MODEL))
    return _call(
        body, name="merge_fwd", grid=(s // tm,),
        in_specs=[hts, hts, *_gate_specs(tm), row, vec, wh, wh, _full((D_MODEL, D_MODEL)), vec, vec],
        out_specs=[row, row, col, col],
        out_shape=[_sds((s, D_MODEL)), _sds((s, D_MODEL)), _sds((s, 1)), _sds((s, 1))],
        sem=("parallel",),
    )(oat, obt, proj, proj, proj, proj, x, gate1, wba, wbb, w_out, ln_g, ln_b)


def _merge_bwd(dy, oat, obt, proj, wba, wbb, w_out, *, tm):
    s = dy.shape[0]

    def body(dy_ref, oat_ref, obt_ref, g0, g1, g2, g3, wba_ref, wbb_ref, wo_ref,
             dgl_ref, doat_ref, dobt_ref, mg_ref, dwa_ref, dwb_ref):
        @pl.when(pl.program_id(0) == 0)
        def _():
            dwa_ref[...] = jnp.zeros(dwa_ref.shape, F32)
            dwb_ref[...] = jnp.zeros(dwb_ref.shape, F32)

        dm = _dot_nt(dy_ref[...], wo_ref[...])
        ga = _sigmoid(jnp.concatenate([g0[...], g1[...]], axis=1))
        gb = _sigmoid(jnp.concatenate([g2[...], g3[...]], axis=1))
        pa, pb = _heads_matmul(oat_ref, wba_ref), _heads_matmul(obt_ref, wbb_ref)
        mg_ref[...] = (ga * pa + gb * pb).astype(BF16)
        dgl_ref[:, :D_MODEL] = (dm * pa * ga * (1.0 - ga)).astype(BF16)
        dgl_ref[:, D_MODEL:] = (dm * pb * gb * (1.0 - gb)).astype(BF16)
        dpa, dpb = (dm * ga).astype(BF16), (dm * gb).astype(BF16)
        for h in range(N_HEADS):
            doat_ref[h] = _dot_nt(wba_ref[h], dpa).astype(BF16)
            dobt_ref[h] = _dot_nt(wbb_ref[h], dpb).astype(BF16)
            dwa_ref[h] += _dot(oat_ref[h], dpa)
            dwb_ref[h] += _dot(obt_ref[h], dpb)

    hts = pl.BlockSpec((N_HEADS, LANES, tm), lambda i: (0, 0, i))
    row = pl.BlockSpec((tm, D_MODEL), lambda i: (i, 0))
    wh = _full((N_HEADS, LANES, D_MODEL))
    return _call(
        body, name="merge_bwd", grid=(s // tm,),
        in_specs=[row, hts, hts, *_gate_specs(tm), wh, wh, _full((D_MODEL, D_MODEL))],
        out_specs=[pl.BlockSpec((tm, 2 * D_MODEL), lambda i: (i, 0)), hts, hts, row, wh, wh],
        out_shape=[_sds((s, 2 * D_MODEL), BF16), _sds((N_HEADS, LANES, s), BF16), _sds((N_HEADS, LANES, s), BF16),
                   _sds((s, D_MODEL), BF16), _sds((N_HEADS, LANES, D_MODEL)), _sds((N_HEADS, LANES, D_MODEL))],
        sem=("arbitrary",),
    )(dy, oat, obt, proj, proj, proj, proj, wba, wbb, w_out)


FF_TC = 256


def _shift_rows(t, prev_row, next_row):
    n = t.shape[0]
    r = lax.broadcasted_iota(jnp.int32, t.shape, 0)
    up = jnp.where(r == 0, prev_row, pltpu.roll(t, 1, 0))
    dn = jnp.where(r == n - 1, next_row, pltpu.roll(t, n - 1, 0))
    return up, dn


HALO = 16


def _halo_specs(tm, s, tc):
    nb = s // HALO
    main = pl.BlockSpec((2, tm, tc), lambda j, i: (0, i, j))
    prev = pl.BlockSpec((2, HALO, tc), lambda j, i: (0, jnp.maximum(i * (tm // HALO) - 1, 0), j))
    nxt = pl.BlockSpec((2, HALO, tc), lambda j, i: (0, jnp.minimum((i + 1) * (tm // HALO), nb - 1), j))
    return main, prev, nxt


def _halo_rows(prev_ref, next_ref, half, i, n_i):
    prev_row = jnp.where(i == 0, 0.0, prev_ref[half, HALO - 1:HALO, :].astype(F32))
    next_row = jnp.where(i == n_i - 1, 0.0, next_ref[half, 0:1, :].astype(F32))
    return prev_row, next_row


def _conv(t, prev_row, next_row, w, b):
    up, dn = _shift_rows(t, prev_row, next_row)
    return w[0:1, :] * up + w[1:2, :] * t + w[2:3, :] * dn + b


def _ffn_act_fwd(u, cw, cb, *, tm):
    _, s, ff = u.shape
    n_i = s // tm

    def body(u_ref, up_ref, un_ref, cw_ref, cb_ref, a_ref):
        i = pl.program_id(1)
        gc = _conv(u_ref[0].astype(F32), *_halo_rows(up_ref, un_ref, 0, i, n_i), cw_ref[0], cb_ref[0])
        vc = _conv(u_ref[1].astype(F32), *_halo_rows(up_ref, un_ref, 1, i, n_i), cw_ref[1], cb_ref[1])
        a_ref[...] = (gc * _sigmoid(gc) * vc).astype(BF16)

    main, prev, nxt = _halo_specs(tm, s, FF_TC)
    return _call(
        body, name="ffn_act_fwd", grid=(ff // FF_TC, n_i),
        in_specs=[main, prev, nxt, pl.BlockSpec((2, 3, FF_TC), lambda j, i: (0, 0, j)),
                  pl.BlockSpec((2, 1, FF_TC), lambda j, i: (0, 0, j))],
        out_specs=pl.BlockSpec((tm, FF_TC), lambda j, i: (i, j)),
        out_shape=_sds((s, ff), BF16), sem=("parallel", "parallel"),
    )(u, u, u, cw, cb)


def _ffn_act_bwd(dy2, w_down, u, cw, cb, *, tm):
    _, s, ff = u.shape
    n_i = s // tm

    def body(dy_ref, wd_ref, u_ref, up_ref, un_ref, cw_ref, cb_ref, dc_ref, dcw_ref, dcb_ref):
        i = pl.program_id(1)

        @pl.when(i == 0)
        def _():
            dcw_ref[...] = jnp.zeros(dcw_ref.shape, F32)
            dcb_ref[...] = jnp.zeros(dcb_ref.shape, F32)

        da = _dot_nt(dy_ref[...], wd_ref[...])
        ug, uv = u_ref[0].astype(F32), u_ref[1].astype(F32)
        ugp, ugn = _shift_rows(ug, *_halo_rows(up_ref, un_ref, 0, i, n_i))
        uvp, uvn = _shift_rows(uv, *_halo_rows(up_ref, un_ref, 1, i, n_i))
        wg, wv = cw_ref[0], cw_ref[1]
        gc = wg[0:1, :] * ugp + wg[1:2, :] * ug + wg[2:3, :] * ugn + cb_ref[0]
        vc = wv[0:1, :] * uvp + wv[1:2, :] * uv + wv[2:3, :] * uvn + cb_ref[1]
        sg = _sigmoid(gc)
        dg = da * vc * sg * (1.0 + gc * (1.0 - sg))
        dv = da * gc * sg
        dc_ref[0] = dg.astype(BF16)
        dc_ref[1] = dv.astype(BF16)
        for half, (d, taps) in enumerate(((dg, (ugp, ug, ugn)), (dv, (uvp, uv, uvn)))):
            for tap in range(3):
                dcw_ref[half, tap:tap + 1, :] += jnp.sum(d * taps[tap], axis=0, keepdims=True)
            dcb_ref[half] += jnp.sum(d, axis=0, keepdims=True)

    main, prev, nxt = _halo_specs(tm, s, FF_TC)
    return _call(
        body, name="ffn_act_bwd", grid=(ff // FF_TC, n_i),
        in_specs=[pl.BlockSpec((tm, D_MODEL), lambda j, i: (i, 0)), pl.BlockSpec((FF_TC, D_MODEL), lambda j, i: (j, 0)),
                  main, prev, nxt, pl.BlockSpec((2, 3, FF_TC), lambda j, i: (0, 0, j)),
                  pl.BlockSpec((2, 1, FF_TC), lambda j, i: (0, 0, j))],
        out_specs=[main, pl.BlockSpec((2, 3, FF_TC), lambda j, i: (0, 0, j)),
                   pl.BlockSpec((2, 1, FF_TC), lambda j, i: (0, 0, j))],
        out_shape=[_sds((2, s, ff), BF16), _sds((2, 3, ff)), _sds((2, 1, ff))],
        sem=("parallel", "arbitrary"),
    )(dy2, w_down, u, u, u, cw, cb)


def _conv_bwd_input(dc, cw, *, tm):
    _, s, ff = dc.shape
    n_i = s // tm

    def body(d_ref, dp_ref, dn_ref, cw_ref, du_ref):
        i = pl.program_id(1)
        for half in range(2):
            d = d_ref[half].astype(F32)
            up, dn = _shift_rows(d, *_halo_rows(dp_ref, dn_ref, half, i, n_i))
            w = cw_ref[half]
            du_ref[half] = (w[0:1, :] * dn + w[1:2, :] * d + w[2:3, :] * up).astype(BF16)

    main, prev, nxt = _halo_specs(tm, s, FF_TC)
    return _call(
        body, name="conv_bwd_input", grid=(ff // FF_TC, n_i),
        in_specs=[main, prev, nxt, pl.BlockSpec((2, 3, FF_TC), lambda j, i: (0, 0, j))],
        out_specs=main, out_shape=_sds((2, s, ff), BF16), sem=("parallel", "parallel"),
    )(dc, dc, dc, cw)


def _ffn_down_loss(a, w_down, x1, target, gate2, ln_g, ln_b, *, tm):
    s, ff = a.shape
    n_i = s // tm

    def body(a_ref, wd_ref, x1_ref, tg_ref, gt_ref, lg_ref, lb_ref, ls_ref, dy_ref, dx_ref, dg_ref, db_ref, dgt_ref):
        @pl.when(pl.program_id(0) == 0)
        def _():
            dg_ref[...] = jnp.zeros(dg_ref.shape, F32)
            db_ref[...] = jnp.zeros(db_ref.shape, F32)
            dgt_ref[...] = jnp.zeros(dgt_ref.shape, F32)

        y2 = _dot(a_ref[...], wd_ref[...])
        z = ALPHA * x1_ref[...] + gt_ref[...] * y2
        mu = jnp.mean(z, axis=-1, keepdims=True)
        zc = z - mu
        r = lax.rsqrt(jnp.mean(zc * zc, axis=-1, keepdims=True) + LN_EPS)
        xhat = zc * r
        diff = xhat * lg_ref[...] + lb_ref[...] - tg_ref[...]
        ls_ref[...] = jnp.full(ls_ref.shape, 0.5 / D_MODEL * jnp.sum(diff * diff), F32)
        dx2 = diff * (1.0 / D_MODEL)
        dg_ref[...] += jnp.sum(dx2 * xhat, axis=0, keepdims=True)
        db_ref[...] += jnp.sum(dx2, axis=0, keepdims=True)
        dz = _ln_bwd(dx2, xhat, r, lg_ref[...])
        dgt_ref[...] += jnp.sum(dz * y2, axis=0, keepdims=True)
        dy_ref[...] = (gt_ref[...] * dz).astype(BF16)
        dx_ref[...] = ALPHA * dz

    row = pl.BlockSpec((tm, D_MODEL), lambda i: (i, 0))
    vec = _full((1, D_MODEL))
    return _call(
        body, name="ffn_down_loss", grid=(n_i,),
        in_specs=[pl.BlockSpec((tm, ff), lambda i: (i, 0)), _full((ff, D_MODEL)), row, row, vec, vec, vec],
        out_specs=[pl.BlockSpec((None, 8, LANES), lambda i: (i, 0, 0)), row, row, vec, vec, vec],
        out_shape=[_sds((n_i, 8, LANES)), _sds((s, D_MODEL), BF16), _sds((s, D_MODEL)),
                   _sds((1, D_MODEL)), _sds((1, D_MODEL)), _sds((1, D_MODEL))],
        sem=("arbitrary",),
    )(a, w_down, x1, target, gate2, ln_g, ln_b)


def _ffn_up_bwd(du, wup4, dx1a, x1, scale2, x, y, mu1, r1, gate1, ln_g, *, tm):
    s = x.shape[0]
    nb, _, ns = wup4.shape

    def body(du_ref, w_ref, dxa_ref, x1_ref, sc_ref, x_ref, y_ref, mu_ref, r_ref, gt_ref, lg_ref,
             dxo_ref, dy_ref, dsc_ref, dsh_ref, dg_ref, db_ref, dgt_ref, acc):
        i, k = pl.program_id(0), pl.program_id(1)

        @pl.when(jnp.logical_and(i == 0, k == 0))
        def _():
            for ref in (dsc_ref, dsh_ref, dg_ref, db_ref, dgt_ref):
                ref[...] = jnp.zeros(ref.shape, F32)

        @pl.when(k == 0)
        def _():
            acc[...] = jnp.zeros(acc.shape, F32)

        acc[...] += _dot_nt(du_ref[...], w_ref[...])

        @pl.when(k == nb - 1)
        def _():
            dh = acc[...]
            x1 = x1_ref[...]
            dsc_ref[...] += jnp.sum(dh * x1, axis=0, keepdims=True)
            dsh_ref[...] += jnp.sum(dh, axis=0, keepdims=True)
            dx1 = dxa_ref[...] + dh * (1.0 + sc_ref[...])
            yv = y_ref[...]
            xhat = (ALPHA * x_ref[...] + gt_ref[...] * yv - mu_ref[...]) * r_ref[...]
            dg_ref[...] += jnp.sum(dx1 * xhat, axis=0, keepdims=True)
            db_ref[...] += jnp.sum(dx1, axis=0, keepdims=True)
            dz = _ln_bwd(dx1, xhat, r_ref[...], lg_ref[...])
            dgt_ref[...] += jnp.sum(dz * yv, axis=0, keepdims=True)
            dy_ref[...] = (gt_ref[...] * dz).astype(BF16)
            dxo_ref[...] = ALPHA * dz

    row = pl.BlockSpec((tm, D_MODEL), lambda i, k: (i, 0))
    col = pl.BlockSpec((tm, 1), lambda i, k: (i, 0))
    vec = _full((1, D_MODEL))
    return _call(
        body, name="ffn_up_bwd", grid=(s // tm, nb),
        in_specs=[pl.BlockSpec((None, tm, ns), lambda i, k: (k // 2, i, k % 2)),
                  pl.BlockSpec((None, D_MODEL, ns), lambda i, k: (k, 0, 0)),
                  row, row, vec, row, row, col, col, vec, vec],
        out_specs=[row, row, vec, vec, vec, vec, vec],
        out_shape=[_sds((s, D_MODEL)), _sds((s, D_MODEL), BF16)] + [_sds((1, D_MODEL))] * 5,
        scratch=[pltpu.VMEM((tm, D_MODEL), F32)],
        sem=("arbitrary", "arbitrary"),
    )(du, wup4, dx1a, x1, scale2, x, y, mu1, r1, gate1, ln_g)


def _mm_nt4_mod_bwd(dp, w4, dxa, x, scale, *, tm, name):
    m = x.shape[0]
    nb, kdim, ns = w4.shape

    def body(dp_ref, w_ref, dxa_ref, x_ref, sc_ref, dx_ref, dsc_ref, dsh_ref, acc):
        i, k = pl.program_id(0), pl.program_id(1)

        @pl.when(jnp.logical_and(i == 0, k == 0))
        def _():
            dsc_ref[...] = jnp.zeros(dsc_ref.shape, F32)
            dsh_ref[...] = jnp.zeros(dsh_ref.shape, F32)

        @pl.when(k == 0)
        def _():
            acc[...] = jnp.zeros(acc.shape, F32)

        acc[...] += _dot_nt(dp_ref[...], w_ref[...])

        @pl.when(k == nb - 1)
        def _():
            dh = acc[...]
            dsc_ref[...] += jnp.sum(dh * x_ref[...], axis=0, keepdims=True)
            dsh_ref[...] += jnp.sum(dh, axis=0, keepdims=True)
            dx_ref[...] = dxa_ref[...] + dh * (1.0 + sc_ref[...])

    row = pl.BlockSpec((tm, kdim), lambda i, k: (i, 0))
    vec = _full((1, kdim))
    return _call(
        body, name=name, grid=(m // tm, nb),
        in_specs=[pl.BlockSpec((tm, ns), lambda i, k: (i, k)), pl.BlockSpec((None, kdim, ns), lambda i, k: (k, 0, 0)),
                  row, row, vec],
        out_specs=[row, vec, vec],
        out_shape=[_sds((m, kdim)), _sds((1, kdim)), _sds((1, kdim))],
        scratch=[pltpu.VMEM((tm, kdim), F32)],
        sem=("arbitrary", "arbitrary"),
    )(dp, w4, dxa, x, scale)


def _pad_heads_w(w):
    w8 = w.reshape(N_HEADS, HEAD_DIM, w.shape[-1])
    z = jnp.zeros_like(w8)
    first = (jnp.arange(N_HEADS) < N_HEADS // N_KV)[:, None, None]
    return jnp.where(first, jnp.concatenate([w8, z], axis=1), jnp.concatenate([z, w8], axis=1))


def _unpad_heads_w(g):
    first = (jnp.arange(N_HEADS) < N_HEADS // N_KV)[:, None, None]
    return jnp.where(first, g[:, :HEAD_DIM], g[:, HEAD_DIM:]).reshape(N_HEADS * HEAD_DIM, g.shape[-1])


def _ones_beside(vt):
    half = vt.shape[0] // 2
    ones = jnp.ones((half, vt.shape[1]), vt.dtype)
    return jnp.stack([jnp.concatenate([vt[:half], ones], axis=0), jnp.concatenate([ones, vt[half:]], axis=0)])


def _rep8(a):
    return jnp.broadcast_to(a.reshape(1, -1), (8, a.size))


def _first_row(a):
    r8 = _rep8(a)
    return jnp.where(lax.broadcasted_iota(jnp.int32, r8.shape, 0) == 0, r8, 0.0)


def _to_blocks4(w):
    k, n = w.shape
    return w.reshape(k, N_CHIPS, n // N_CHIPS).transpose(1, 0, 2)


def _local_step(x, c, ctx, c_ctx, wmod4, b_mod, win4, b_in, sink, qn, kn, wba, wbb, w_out, ln1_g, ln1_b,
                wup4, cw, cb, w_down, ln2_g, ln2_b, target):
    s, nc = x.shape[0], ctx.shape[0]
    tm = min(512, s)
    tm2 = min(256, s)
    tl = min(1024, s)
    tx = min(2048, s)
    zvec = jnp.zeros((1, D_MODEL), F32)

    cc = jnp.concatenate([_rep8(c), _rep8(c_ctx)], axis=0)
    mods = _mm_nn4(cc, zvec, zvec, wmod4, b_mod, mode="silu", split_out=False, out_dtype=F32, tm=16, name="mod_vectors")
    shift1, scale1, gate1, shift2, scale2, gate2 = [mods[0:1, i * D_MODEL:(i + 1) * D_MODEL] for i in range(6)]
    shift_c, scale_c = mods[8:9, :D_MODEL], mods[8:9, D_MODEL:2 * D_MODEL]

    cos, sin = _rope_tables(s)
    cos_c, sin_c = jnp.ones((nc, LANES), F32), jnp.zeros((nc, LANES), F32)
    qg, kg = jnp.tile(qn, (1, 2)), jnp.tile(kn, (1, 2))

    proj_c = _mm_nn4(ctx, shift_c, scale_c, win4, b_in, mode="modulate", split_out=False, out_dtype=F32, tm=nc,
                     name="in_proj_ctx")
    _, kac, vac, _, kbc, vbc = _prep(proj_c, cos_c, sin_c, qg, kg, tm=nc, name="prep_ctx")
    proj = _mm_nn4(x, shift1, scale1, win4, b_in, mode="modulate", split_out=False, out_dtype=F32, tm=tl, name="in_proj")
    qat, ka, va, qbt, kb, vb = _prep(proj, cos, sin, qg, kg, tm=tl, name="prep")
    oat, lse_a = _attn_win_fwd(qat, ka, _ones_beside(va.T), kac, _ones_beside(vac.T), sink, tq=tm)
    obt, lse_b, mrun_b, pbt = _attn_glob_fwd(qbt, kb, _ones_beside(vb.T), kbc, _ones_beside(vbc.T), tq=tm,
                                             tk=min(1024, s))
    wba_p, wbb_p = _pad_heads_w(wba), _pad_heads_w(wbb)
    x1, y, mu1, r1 = _merge_fwd(oat, obt, proj, x, gate1, wba_p, wbb_p, w_out, ln1_g, ln1_b, tm=tm)
    u = _mm_nn4(x1, shift2, scale2, wup4, jnp.zeros((1, 2 * D_FF), F32), mode="modulate", split_out=True,
                out_dtype=BF16, tm=tl, name="ffn_up")
    cw2 = cw.reshape(3, 2, D_FF).transpose(1, 0, 2)
    cb2 = cb.reshape(2, 1, D_FF)
    a = _ffn_act_fwd(u, cw2, cb2, tm=tx)
    ls, dy2, dx1a, dln2_g, dln2_b, dgate2 = _ffn_down_loss(a, w_down, x1, target, gate2, ln2_g, ln2_b, tm=tm)
    loss = jnp.sum(ls[:, 0, 0])

    n_s = s // tl
    dw_down = _mm_tn(a, dy2, a_spec=pl.BlockSpec((tl, D_FF), lambda t: (t, 0)),
                     b_spec=pl.BlockSpec((tl, D_MODEL), lambda t: (t, 0)), grid=(n_s,),
                     out_shape=_sds((D_FF, D_MODEL)), out_spec=_full((D_FF, D_MODEL)), name="dw_down")
    dc, dcw2, dcb2 = _ffn_act_bwd(dy2, w_down, u, cw2, cb2, tm=tx)
    du = _conv_bwd_input(dc, cw2, tm=tx)
    dxz1, dy, dscale2, dshift2, dln1_g, dln1_b, dgate1 = _ffn_up_bwd(
        du, wup4, dx1a, x1, scale2, x, y, mu1, r1, gate1, ln1_g, tm=tm)
    ns_up = wup4.shape[-1]
    dw_up4 = _mm_tn(x1, du, a_spec=pl.BlockSpec((tl, D_MODEL), lambda k, t: (t, 0)),
                    b_spec=pl.BlockSpec((None, tl, ns_up), lambda k, t: (k // 2, t, k % 2)), grid=(N_CHIPS, n_s),
                    out_shape=_sds((N_CHIPS, D_MODEL, ns_up)),
                    out_spec=pl.BlockSpec((None, D_MODEL, ns_up), lambda k, t: (k, 0, 0)),
                    mod=(shift2, scale2), name="dw_up")

    dgl, doat, dobt, merged, dwba_p, dwbb_p = _merge_bwd(dy, oat, obt, proj, wba_p, wbb_p, w_out, tm=tm2)
    dwba, dwbb = _unpad_heads_w(dwba_p), _unpad_heads_w(dwbb_p)
    rowspec = pl.BlockSpec((tl, D_MODEL), lambda t: (t, 0))
    dw_out = _mm_tn(merged, dy, a_spec=rowspec, b_spec=rowspec, grid=(n_s,), out_shape=_sds((D_MODEL, D_MODEL)),
                    out_spec=_full((D_MODEL, D_MODEL)), name="dw_out")

    dqat, dkat, dvat, dkact, dvact, dsk = _attn_win_bwd(qat, doat, oat, lse_a, ka, ka.T, va, kac, kac.T, vac, sink, tq=tm)
    dka, dva, dkac, dvac = dkat.T, dvat.T, dkact.T, dvact.T
    dqbt, dkbt, dvbt, dkbct, dvbct = _attn_glob_bwd(qbt, dobt, obt, lse_b, mrun_b, pbt, kb.T, vb, kbc.T, vbc, tq=tm, tk=tm)
    dkb, dvb, dkbc, dvbc = dkbt.T, dvbt.T, dkbct.T, dvbct.T
    dsink = jnp.sum(dsk[:, :, 0, 0], axis=1)

    dproj, dqg, dkg = _prep_bwd(dqat, dka, dva, dqbt, dkb, dvb, proj, cos, sin, qg, kg, dgl, tm=tm, name="prep_bwd")
    grad_x, dscale1, dshift1 = _mm_nt4_mod_bwd(dproj, win4, dxz1, x, scale1, tm=tl, name="in_proj_bwd")
    ns_in = win4.shape[-1]
    win_spec = dict(b_spec=pl.BlockSpec((None, None, ns_in), lambda k, t: (0, 0, k)),
                    out_shape=_sds((N_CHIPS, D_MODEL, ns_in)),
                    out_spec=pl.BlockSpec((None, D_MODEL, ns_in), lambda k, t: (k, 0, 0)),
                    colsum_spec=pl.BlockSpec((8, ns_in), lambda k, t: (0, k)), colsum_shape=_sds((8, IN_COLS)))
    win_spec["b_spec"] = pl.BlockSpec((tl, ns_in), lambda k, t: (t, k))
    dw_in4, db_in = _mm_tn(x, dproj, a_spec=pl.BlockSpec((tl, D_MODEL), lambda k, t: (t, 0)), grid=(N_CHIPS, n_s),
                           mod=(shift1, scale1), name="dw_in", **win_spec)

    zq = jnp.zeros((N_HEADS, LANES, nc), F32)
    dproj_c, _, dkg_c = _prep_bwd(zq, dkac, dvac, zq, dkbc, dvbc, proj_c, cos_c, sin_c, qg, kg,
                                  jnp.zeros((nc, IN_COLS - OFF_GA), BF16), tm=nc, name="prep_bwd_ctx")
    _, dscale_c, dshift_c = _mm_nt4_mod_bwd(dproj_c, win4, jnp.zeros((nc, D_MODEL), F32), ctx, scale_c, tm=nc,
                                            name="in_proj_bwd_ctx")
    win_spec["b_spec"] = pl.BlockSpec((nc, ns_in), lambda k, t: (t, k))
    dw_in4, db_in_c = _mm_tn(ctx, dproj_c, a_spec=pl.BlockSpec((nc, D_MODEL), lambda k, t: (t, 0)), grid=(N_CHIPS, 1),
                             mod=(shift_c, scale_c), init=dw_in4, name="dw_in_ctx", **win_spec)

    dmod = jnp.concatenate([dshift1, dscale1, dgate1, dshift2, dscale2, dgate2], axis=1)
    dmodc = jnp.concatenate([dshift_c, dscale_c], axis=1)
    dmodc_pad = jnp.concatenate([dmodc, jnp.zeros((1, 4 * D_MODEL), F32)], axis=1)
    dmodc8 = _first_row(dmodc_pad).astype(BF16)
    z8 = jnp.zeros((8, D_MODEL), F32)
    dsilu_c, _, _ = _mm_nt4_mod_bwd(dmodc8, wmod4, z8, z8, zvec, tm=8, name="c_ctx_bwd")
    sg = _sigmoid(c_ctx)
    dc_ctx = dsilu_c[0:1] * sg * (1.0 + c_ctx * (1.0 - sg))

    dqn = jnp.sum(dqg.reshape(N_HEADS, HEAD_DIM), axis=0, keepdims=True)
    dkn = jnp.sum((dkg + dkg_c).reshape(N_KV, HEAD_DIM), axis=0, keepdims=True)
    grads = dict(
        w_in4=dw_in4, b_in=db_in[0:1] + db_in_c[0:1], sink=dsink, qn=dqn, kn=dkn, wba=dwba, wbb=dwbb, w_out=dw_out,
        ln1_g=dln1_g, ln1_b=dln1_b, w_up4=dw_up4, conv_w=dcw2.transpose(1, 0, 2).reshape(3, 2 * D_FF),
        conv_b=dcb2.reshape(1, 2 * D_FF), w_down=dw_down, ln2_g=dln2_g, ln2_b=dln2_b,
        c_ctx=dc_ctx, dmod=dmod, dmodc=dmodc)
    return loss, grad_x, grads


ANY = pl.BlockSpec(memory_space=pl.ANY)


def _mesh_pos():
    return lax.axis_index("x"), lax.axis_index("y"), lax.axis_index("c")


def _other_chips(x, y):
    return [(1 - x, y), (x, 1 - y), (1 - x, 1 - y)]


def _remote(src, dst, send, recv, dev):
    return pltpu.make_async_remote_copy(src_ref=src, dst_ref=dst, send_sem=send, recv_sem=recv, device_id=dev,
                                        device_id_type=MESH)


def _set_block(stack, block, k):
    return lax.dynamic_update_slice(stack, block[None], (k,) + (0,) * block.ndim)


def _gather_shards(arrs, small):
    na = len(arrs)
    halves = [a.shape[0] // 2 for a in arrs]

    def body(*refs):
        ins, small_ref = refs[:na], refs[na]
        outs, small_out = refs[na + 1:2 * na + 1], refs[2 * na + 1]
        send, recv = refs[2 * na + 2:]
        x, y, c = _mesh_pos()
        me = 2 * x + y
        chips = _other_chips(x, y)

        def half(a, cc):
            return pl.ds(cc * halves[a], halves[a])

        sends = []
        for j, chip in enumerate(chips):
            for a in range(na):
                sends.append(_remote(ins[a].at[half(a, c)], outs[a].at[me, half(a, c)], send.at[a, j], recv.at[a, j],
                                     (*chip, c)))
            sends.append(_remote(small_ref, small_out.at[me], send.at[na, j], recv.at[na, j], (*chip, c)))
        for cp in sends:
            cp.start()
        for j, chip in enumerate(chips):
            kj = 2 * chip[0] + chip[1]
            for a in range(na):
                landed = outs[a].at[kj, half(a, c)]
                _remote(landed, landed, send.at[a, j], recv.at[a, j], (*chip, c)).wait_recv()
                fwd = _remote(landed, landed, send.at[a, 3 + j], recv.at[a, 3 + j], (x, y, 1 - c))
                fwd.start()
                sends.append(fwd)
            _remote(small_ref, small_out.at[kj], send.at[na, j], recv.at[na, j], (*chip, c)).wait_recv()
        for j, chip in enumerate(chips):
            kj = 2 * chip[0] + chip[1]
            for a in range(na):
                other = outs[a].at[kj, half(a, 1 - c)]
                _remote(other, other, send.at[a, 3 + j], recv.at[a, 3 + j], (x, y, 1 - c)).wait_recv()
        for cp in sends:
            cp.wait_send()

    out_shape = [_sds((N_CHIPS,) + a.shape, a.dtype) for a in arrs] + [_sds((N_CHIPS,) + small.shape, small.dtype)]
    got = pl.pallas_call(
        body, name="gather_shards", in_specs=[ANY] * (na + 1), out_specs=[ANY] * (na + 1), out_shape=out_shape,
        scratch_shapes=[pltpu.SemaphoreType.DMA((na + 1, 6)), pltpu.SemaphoreType.DMA((na + 1, 6))],
    )(*arrs, small)
    xp, yp, _ = _mesh_pos()
    return [_set_block(g, a, 2 * xp + yp) for g, a in zip(got, list(arrs) + [small])]


def _allgather_rows(v):
    r, n = v.shape

    def body(v_ref, out_ref, send, recv, loc):
        x, y, c = _mesh_pos()
        me, sibling = (x, y, c), (x, y, 1 - c)
        chips = _other_chips(x, y)

        def rows(px, py, pc):
            return out_ref.at[4 * px + 2 * py + pc]

        def copy(k, block, to, src=None):
            return _remote(rows(*block) if src is None else src, rows(*block), send.at[k], recv.at[k], to)

        mine = pltpu.make_async_copy(v_ref, rows(*me), loc)
        mine.start()
        first = [copy(0, me, sibling, src=v_ref)] + [copy(1 + j, me, (*chip, c), src=v_ref) for j, chip in enumerate(chips)]
        for cp in first:
            cp.start()
        passed = [copy(4 + j, (*chip, c), sibling) for j, chip in enumerate(chips)]
        for j, chip in enumerate(chips):
            copy(1 + j, (*chip, c), me).wait_recv()
            passed[j].start()
        copy(0, sibling, me).wait_recv()
        for j, chip in enumerate(chips):
            copy(4 + j, (*chip, 1 - c), me).wait_recv()
        for cp in first + passed:
            cp.wait_send()
        mine.wait()

    return pl.pallas_call(
        body, name="allgather_rows", in_specs=[pl.BlockSpec(memory_space=pltpu.VMEM)],
        out_specs=pl.BlockSpec(memory_space=pltpu.VMEM), out_shape=_sds((N_DEV, r, n), v.dtype),
        scratch_shapes=[pltpu.SemaphoreType.DMA((7,)), pltpu.SemaphoreType.DMA((7,)), pltpu.SemaphoreType.DMA],
    )(v)


def _swap_other_half(g):
    nb, r, n = g.shape
    rh = r // 2

    def body(g_ref, out_ref, send, recv):
        x, y, c = _mesh_pos()
        cp = _remote(g_ref.at[:, pl.ds((1 - c) * rh, rh), :], out_ref, send, recv, (x, y, 1 - c))
        cp.start()
        cp.wait()

    return pl.pallas_call(
        body, name="swap_other_half", in_specs=[ANY], out_specs=ANY, out_shape=_sds((nb, rh, n), g.dtype),
        scratch_shapes=[pltpu.SemaphoreType.DMA, pltpu.SemaphoreType.DMA],
    )(g)


def _scatter_to_chips(p):
    def body(p_ref, out_ref, send, recv):
        x, y, c = _mesh_pos()
        me = 2 * x + y
        chips = _other_chips(x, y)
        sends = [_remote(p_ref.at[2 * chip[0] + chip[1]], out_ref.at[me], send.at[j], recv.at[j], (*chip, c))
                 for j, chip in enumerate(chips)]
        for cp in sends:
            cp.start()
        for j, chip in enumerate(chips):
            kj = 2 * chip[0] + chip[1]
            _remote(p_ref.at[kj], out_ref.at[kj], send.at[j], recv.at[j], (*chip, c)).wait_recv()
        for cp in sends:
            cp.wait_send()

    got = pl.pallas_call(
        body, name="scatter_to_chips", in_specs=[ANY], out_specs=ANY, out_shape=_sds(p.shape, p.dtype),
        scratch_shapes=[pltpu.SemaphoreType.DMA((3,)), pltpu.SemaphoreType.DMA((3,))],
    )(p)
    xp, yp, _ = _mesh_pos()
    me = 2 * xp + yp
    return _set_block(got, lax.dynamic_index_in_dim(p, me, axis=0, keepdims=False), me)


def _join_halves(f):
    def body(f_ref, out_ref, send, recv):
        x, y, c = _mesh_pos()
        cp = _remote(f_ref, out_ref, send, recv, (x, y, 1 - c))
        cp.start()
        cp.wait()

    other = pl.pallas_call(
        body, name="join_halves", in_specs=[ANY], out_specs=ANY, out_shape=_sds(f.shape, f.dtype),
        scratch_shapes=[pltpu.SemaphoreType.DMA, pltpu.SemaphoreType.DMA],
    )(f)
    first = lax.axis_index("c") == 0
    return jnp.concatenate([jnp.where(first, f, other), jnp.where(first, other, f)], axis=0)


def _row_tile(rows, cap=512):
    t = cap - cap % 8
    while rows % t:
        t -= 8
    return t


def _add_blocks(a, b, out_dtype):
    nb, r, n = a.shape
    tr = _row_tile(r)

    def body(a_ref, b_ref, o_ref):
        o_ref[...] = (a_ref[...] + b_ref[...]).astype(out_dtype)

    spec = pl.BlockSpec((None, tr, n), lambda k, i: (k, i, 0))
    return _call(body, name="add_blocks", grid=(nb, r // tr), in_specs=[spec, spec], out_specs=spec,
                 out_shape=_sds(a.shape, out_dtype), sem=("parallel", "parallel"))(a, b)


def _sum_leading(a, *, name):
    nk, r, n = a.shape
    tr = _row_tile(r)

    def body(a_ref, o_ref):
        acc = a_ref[0].astype(F32)
        for k in range(1, nk):
            acc = acc + a_ref[k].astype(F32)
        o_ref[...] = acc

    return _call(body, name=name, grid=(r // tr,), in_specs=[pl.BlockSpec((nk, tr, n), lambda i: (0, i, 0))],
                 out_specs=pl.BlockSpec((tr, n), lambda i: (i, 0)), out_shape=_sds((r, n)), sem=("parallel",))(a)


def _silu_outer(a, b):
    kdim, n = a.shape[1], b.shape[1]

    def body(a_ref, b_ref, o_ref):
        av = a_ref[...]
        av = av * _sigmoid(av)
        bv = b_ref[...]
        ah, bh = av.astype(BF16), bv.astype(BF16)
        al, bl = (av - ah.astype(F32)).astype(BF16), (bv - bh.astype(F32)).astype(BF16)
        o_ref[...] = _dot_tn(ah, bh) + (_dot_tn(ah, bl) + _dot_tn(al, bh))

    return _call(body, name="dw_mod", grid=(1,), in_specs=[_full(a.shape), _full(b.shape)], out_specs=_full((kdim, n)),
                 out_shape=_sds((kdim, n)))(a, b)


def _adamw(w, g, m, v):
    r, n = w.shape
    tr = _row_tile(r)

    def body(w_ref, g_ref, m_ref, v_ref, d_ref, nm_ref, nv_ref):
        gv = g_ref[...]
        nm = ADAM_B1 * m_ref[...] + (1.0 - ADAM_B1) * gv
        nv = ADAM_B2 * v_ref[...] + (1.0 - ADAM_B2) * (gv * gv)
        m_hat = nm / (1.0 - ADAM_B1 ** ADAM_STEP)
        v_hat = nv / (1.0 - ADAM_B2 ** ADAM_STEP)
        d_ref[...] = -ADAM_LR * (m_hat / (jnp.sqrt(v_hat) + ADAM_EPS) + ADAM_WD * w_ref[...])
        nm_ref[...] = nm
        nv_ref[...] = nv

    spec = pl.BlockSpec((tr, n), lambda i: (i, 0))
    return _call(body, name="adamw", grid=(r // tr,), in_specs=[spec] * 4, out_specs=[spec] * 3,
                 out_shape=[_sds((r, n))] * 3, sem=("parallel",))(w, g, m, v)


BIG = ("w_in", "w_branch_a", "w_branch_b", "w_out", "w_up", "w_down", "conv_w")
BIG_ROWS = 3584
MATRICES = ("w_mod", "w_in", "w_branch_a", "w_branch_b", "w_out", "w_up", "w_down")
SMALL = ("b_mod", "b_in", "conv_b", "ln1_g", "ln1_b", "ln2_g", "ln2_b", "c_ctx", "attn_sink", "q_norm_g", "k_norm_g", "conv_w")
SMALL_ROWS = 8 * len(SMALL)


def _rows(a, n_rows):
    flat = a.reshape(-1)
    return jnp.pad(flat, (0, n_rows * D_MODEL - flat.shape[0])).reshape(n_rows, D_MODEL)


def _group8(a):
    return _rep8(_rows(a, 1)) if a.size <= D_MODEL else _rows(a, 8)


def _ungroup8(p, shape):
    size = math.prod(shape)
    return (p[0, :size] if size <= D_MODEL else p.reshape(-1)[:size]).reshape(shape)


def _unpack_big(p, like):
    out, r = {}, 0
    for n in BIG:
        size = math.prod(like[n].shape)
        nr = size // D_MODEL if n != "conv_w" else 8
        out[n] = p[r:r + nr].reshape(-1)[:size].reshape(like[n].shape)
        r += nr
    return out


def _pack_small(t):
    return jnp.concatenate([_group8(t[n]) for n in SMALL], axis=0)


def _unpack_small(p, like):
    return {n: _ungroup8(p[8 * i:8 * i + 8], like[n].shape) for i, n in enumerate(SMALL)}


WEIGHTS = ("c_ctx", "w_mod", "b_mod", "w_in", "b_in", "attn_sink", "q_norm_g", "k_norm_g", "w_branch_a", "w_branch_b",
           "w_out", "ln1_g", "ln1_b", "w_up", "conv_w", "conv_b", "w_down", "ln2_g", "ln2_b")


def kernel(x, c, ctx, c_ctx, w_mod, b_mod, w_in, b_in, attn_sink, q_norm_g, k_norm_g, w_branch_a, w_branch_b, w_out, ln1_g, ln1_b, w_up, conv_w, conv_b, w_down, ln2_g, ln2_b, loss_target, m_c_ctx, m_w_mod, m_b_mod, m_w_in, m_b_in, m_attn_sink, m_q_norm_g, m_k_norm_g, m_w_branch_a, m_w_branch_b, m_w_out, m_ln1_g, m_ln1_b, m_w_up, m_conv_w, m_conv_b, m_w_down, m_ln2_g, m_ln2_b, v_c_ctx, v_w_mod, v_b_mod, v_w_in, v_b_in, v_attn_sink, v_q_norm_g, v_k_norm_g, v_w_branch_a, v_w_branch_b, v_w_out, v_ln1_g, v_ln1_b, v_w_up, v_conv_w, v_conv_b, v_w_down, v_ln2_g, v_ln2_b):
    w = dict(c_ctx=c_ctx, w_mod=w_mod, b_mod=b_mod, w_in=w_in, b_in=b_in, attn_sink=attn_sink, q_norm_g=q_norm_g,
             k_norm_g=k_norm_g, w_branch_a=w_branch_a, w_branch_b=w_branch_b, w_out=w_out, ln1_g=ln1_g, ln1_b=ln1_b,
             w_up=w_up, conv_w=conv_w, conv_b=conv_b, w_down=w_down, ln2_g=ln2_g, ln2_b=ln2_b)
    m = dict(c_ctx=m_c_ctx, w_mod=m_w_mod, b_mod=m_b_mod, w_in=m_w_in, b_in=m_b_in, attn_sink=m_attn_sink,
             q_norm_g=m_q_norm_g, k_norm_g=m_k_norm_g, w_branch_a=m_w_branch_a, w_branch_b=m_w_branch_b, w_out=m_w_out,
             ln1_g=m_ln1_g, ln1_b=m_ln1_b, w_up=m_w_up, conv_w=m_conv_w, conv_b=m_conv_b, w_down=m_w_down,
             ln2_g=m_ln2_g, ln2_b=m_ln2_b)
    v = dict(c_ctx=v_c_ctx, w_mod=v_w_mod, b_mod=v_b_mod, w_in=v_w_in, b_in=v_b_in, attn_sink=v_attn_sink,
             q_norm_g=v_q_norm_g, k_norm_g=v_k_norm_g, w_branch_a=v_w_branch_a, w_branch_b=v_w_branch_b, w_out=v_w_out,
             ln1_g=v_ln1_g, ln1_b=v_ln1_b, w_up=v_w_up, conv_w=v_conv_w, conv_b=v_conv_b, w_down=v_w_down,
             ln2_g=v_ln2_g, ln2_b=v_ln2_b)
    xp, yp, _ = _mesh_pos()
    me = 2 * xp + yp

    branches = jnp.concatenate([w_branch_a[0], w_branch_b[0]], axis=0)
    wide = jnp.concatenate([w_mod[0], w_in[0], w_up[0], branches], axis=1).astype(BF16)
    tall = jnp.concatenate([w_out[0], w_down[0]], axis=0).astype(BF16)
    wide4, tall4, cw4 = _gather_shards([wide, tall], conv_w[0])
    n_mod, n_in, n_up = w_mod.shape[-1], w_in.shape[-1], w_up.shape[-1]
    wmod4 = wide4[:, :, :n_mod]
    win4 = wide4[:, :, n_mod:n_mod + n_in]
    wup4 = wide4[:, :, n_mod + n_in:n_mod + n_in + n_up]
    br4 = wide4[:, :, n_mod + n_in + n_up:]
    n_br = w_branch_a.shape[1]
    wba = br4[:, :n_br].transpose(1, 0, 2).reshape(n_br, D_MODEL)
    wbb = br4[:, n_br:].transpose(1, 0, 2).reshape(n_br, D_MODEL)
    n_out = w_out.shape[1]
    w_out_full = tall4[:, :n_out].reshape(D_MODEL, D_MODEL)
    w_down_full = tall4[:, n_out:].reshape(D_FF, D_MODEL)
    cw_full = cw4.transpose(1, 0, 2).reshape(3, 2 * D_FF)

    loss, grad_x, g = _local_step(
        x[0], c, ctx[0], c_ctx[None], wmod4, b_mod, win4, b_in, attn_sink[0], q_norm_g, k_norm_g, wba, wbb, w_out_full,
        ln1_g, ln1_b, wup4, cw_full, conv_b, w_down_full, ln2_g, ln2_b, loss_target[0])
    loss = lax.psum(loss, ("x", "y", "c"))

    sent = dict(c=c, dmod=g["dmod"], dmodc=g["dmodc"], b_in=g["b_in"], conv_b=g["conv_b"], ln1_g=g["ln1_g"],
                ln1_b=g["ln1_b"], ln2_g=g["ln2_g"], ln2_b=g["ln2_b"], c_ctx=g["c_ctx"], attn_sink=g["sink"],
                q_norm_g=g["qn"], k_norm_g=g["kn"])
    every = _allgather_rows(jnp.concatenate([_group8(a) for a in sent.values()], axis=0))
    total = _sum_leading(every, name="sum_devices")
    slot = {n: slice(8 * i, 8 * i + 8) for i, n in enumerate(sent)}
    gs = {n: _ungroup8(total[slot[n]], sent[n].shape) for n in SMALL if n in sent}
    dmodc_sum = jnp.concatenate([_ungroup8(total[slot["dmodc"]], (1, 2 * D_MODEL)), jnp.zeros((1, 4 * D_MODEL), F32)],
                                axis=1)
    gs["b_mod"] = _ungroup8(total[slot["dmod"]], b_mod.shape) + dmodc_sum
    acts = jnp.concatenate([every[:, slot["c"].start], _rep8(c_ctx)], axis=0)
    dmods = jnp.concatenate([every[:, slot["dmod"]].reshape(N_DEV, -1)[:, :6 * D_MODEL], _first_row(dmodc_sum)], axis=0)
    g_w_mod = _silu_outer(acts, lax.dynamic_slice_in_dim(dmods, me * n_mod, n_mod, axis=1))

    cw_g4 = _to_blocks4(g["conv_w"])
    parts = [
        g["w_in4"].reshape(N_CHIPS, -1, D_MODEL), _to_blocks4(g["wba"]).reshape(N_CHIPS, -1, D_MODEL),
        _to_blocks4(g["wbb"]).reshape(N_CHIPS, -1, D_MODEL), g["w_out"].reshape(N_CHIPS, -1, D_MODEL),
        g["w_up4"].reshape(N_CHIPS, -1, D_MODEL), g["w_down"].reshape(N_CHIPS, -1, D_MODEL),
        jnp.pad(cw_g4.reshape(N_CHIPS, -1), ((0, 0), (0, 8 * D_MODEL - cw_g4.shape[1] * cw_g4.shape[2]))).reshape(
            N_CHIPS, 8, D_MODEL)]
    used = sum(p.shape[1] for p in parts)
    packed = jnp.concatenate(parts + [jnp.zeros((N_CHIPS, BIG_ROWS - used, D_MODEL), F32)], axis=1)
    rh = BIG_ROWS // 2
    cpos = lax.axis_index("c")
    my_half = lax.dynamic_slice_in_dim(packed, cpos * rh, rh, axis=1)
    chip_sum = _add_blocks(my_half, _swap_other_half(packed), BF16)
    half_sum = _sum_leading(_scatter_to_chips(chip_sum), name="sum_chips")
    g_big = _unpack_big(_join_halves(half_sum), w)

    grads = dict(gs, w_mod=g_w_mod, **g_big)
    grads = {n: grads[n].reshape(w[n].shape) for n in WEIGHTS}
    delta, new_m, new_v = {}, {}, {}
    for n in MATRICES:
        outs = _adamw(*[t[n][0] for t in (w, grads, m, v)])
        delta[n], new_m[n], new_v[n] = [o[None] for o in outs]
    outs = _adamw(*[_pack_small(t) for t in (w, grads, m, v)])
    for res, o in zip((delta, new_m, new_v), outs):
        res.update(_unpack_small(o, w))
    return (loss, grad_x[None], *[grads[n] for n in WEIGHTS], *[delta[n] for n in WEIGHTS],
            *[new_m[n] for n in WEIGHTS], *[new_v[n] for n in WEIGHTS])
```

```python
import functools
import math

import jax
import jax.numpy as jnp
from jax import lax
from jax.experimental import pallas as pl
from jax.experimental.pallas import tpu as pltpu

F32 = jnp.float32
BF16 = jnp.bfloat16

D_MODEL = 1024
HEAD_DIM = 64
N_HEADS = 8
N_KV = 2
WINDOW = 128
GRID_W = 64
ROPE_THETA = 10000.0
D_FF = 2816
LN_EPS = 1e-5
QK_EPS = 1e-6
ALPHA = 2.0 ** 0.25
Q_SCALE = HEAD_DIM ** -0.5
OFF_GA = 1536
IN_COLS = 3584
ADAM_LR, ADAM_B1, ADAM_B2, ADAM_EPS, ADAM_WD, ADAM_STEP = 0.001, 0.9, 0.999, 1e-8, 0.01, 10

LANES = 128
VMEM_BUDGET = 52 * 1024 * 1024
N_CHIPS = 4
N_DEV = 8
NEG = -1e30
MESH = pl.DeviceIdType.MESH


def _sigmoid(x):
    return 1.0 / (1.0 + jnp.exp(-x))


def _dot(a, b):
    return jnp.dot(a, b, preferred_element_type=F32)


def _dot_nt(a, b):
    return lax.dot_general(a, b, (((1,), (1,)), ((), ())), preferred_element_type=F32)


def _dot_tn(a, b):
    return lax.dot_general(a, b, (((0,), (0,)), ((), ())), preferred_element_type=F32)


def _call(body, *, name, grid, in_specs, out_specs, out_shape, scratch=(), sem=None, **kw):
    params = dict(vmem_limit_bytes=VMEM_BUDGET)
    if sem is not None:
        params["dimension_semantics"] = sem
    return pl.pallas_call(body, name=name, grid=grid, in_specs=in_specs, out_specs=out_specs,
                          out_shape=out_shape, scratch_shapes=list(scratch),
                          compiler_params=pltpu.CompilerParams(**params), **kw)


def _full(shape):
    n = len(shape)
    return pl.BlockSpec(shape, lambda *_: (0,) * n)


def _sds(shape, dtype=F32):
    return jax.ShapeDtypeStruct(shape, dtype)


def _mm_nn4(a, shift, scale, w4, bias, *, mode, split_out, out_dtype, tm, name):
    m, kdim = a.shape
    nb, _, ns = w4.shape

    def body(a_ref, sh_ref, sc_ref, w_ref, b_ref, o_ref):
        av = a_ref[...]
        if mode == "modulate":
            av = av * (1.0 + sc_ref[...]) + sh_ref[...]
        else:
            av = av * _sigmoid(av)
        o_ref[...] = (_dot(av.astype(BF16), w_ref[...]) + b_ref[...]).astype(out_dtype)

    if split_out:
        out_shape = _sds((2, m, 2 * ns), out_dtype)
        out_spec = pl.BlockSpec((None, tm, ns), lambda i, k: (k // 2, i, k % 2))
    else:
        out_shape = _sds((m, nb * ns), out_dtype)
        out_spec = pl.BlockSpec((tm, ns), lambda i, k: (i, k))
    return _call(
        body, name=name, grid=(m // tm, nb),
        in_specs=[pl.BlockSpec((tm, kdim), lambda i, k: (i, 0)),
                  pl.BlockSpec((1, kdim), lambda i, k: (0, 0)),
                  pl.BlockSpec((1, kdim), lambda i, k: (0, 0)),
                  pl.BlockSpec((None, kdim, ns), lambda i, k: (k, 0, 0)),
                  pl.BlockSpec((1, ns), lambda i, k: (0, k))],
        out_specs=out_spec, out_shape=out_shape, sem=("parallel", "arbitrary"),
    )(a, shift, scale, w4, bias)


def _mm_tn(a, b, *, a_spec, b_spec, grid, out_shape, out_spec, name, mod=None, init=None, colsum_spec=None,
           colsum_shape=None):
    red = len(grid) - 1
    has_mod, has_init, has_cs = mod is not None, init is not None, colsum_spec is not None

    def body(*refs):
        refs = list(refs)
        a_ref, b_ref = refs[0], refs[1]
        pos = 2
        if has_mod:
            sh_ref, sc_ref = refs[2], refs[3]
            pos = 4
        if has_init:
            init_ref = refs[pos]
            pos += 1
        o_ref = refs[pos]
        cs_ref = refs[pos + 1] if has_cs else None
        s = pl.program_id(red)

        @pl.when(s == 0)
        def _():
            o_ref[...] = init_ref[...] if has_init else jnp.zeros(o_ref.shape, F32)
            if has_cs:
                cs_ref[...] = jnp.zeros(cs_ref.shape, F32)

        av = a_ref[...]
        if has_mod:
            av = av * (1.0 + sc_ref[...]) + sh_ref[...]
        bv = b_ref[...]
        o_ref[...] += _dot_tn(av.astype(BF16), bv)
        if has_cs:
            cs_ref[...] += jnp.broadcast_to(jnp.sum(bv.astype(F32), axis=0, keepdims=True), cs_ref.shape)

    ins, in_specs = [a, b], [a_spec, b_spec]
    if has_mod:
        kdim = mod[0].shape[-1]
        ins += list(mod)
        in_specs += [_full((1, kdim)), _full((1, kdim))]
    if has_init:
        ins.append(init)
        in_specs.append(out_spec)
    out_specs, out_shapes = out_spec, out_shape
    if has_cs:
        out_specs, out_shapes = [out_spec, colsum_spec], [out_shape, colsum_shape]
    sem = ("parallel",) * red + ("arbitrary",)
    return _call(body, name=name, grid=grid, in_specs=in_specs, out_specs=out_specs, out_shape=out_shapes,
                 sem=sem)(*ins)


def _rope_tables(n_tok):
    pos = jnp.arange(n_tok, dtype=jnp.int32)
    rows = (pos // GRID_W).astype(F32)
    cols = (pos % GRID_W).astype(F32)
    n_freq = HEAD_DIM // 4
    inv_freq = ROPE_THETA ** (-jnp.arange(n_freq, dtype=F32) / n_freq)
    ang_r = rows[:, None] * inv_freq
    ang_c = cols[:, None] * inv_freq
    cos = jnp.concatenate([jnp.cos(ang_r)] * 2 + [jnp.cos(ang_c)] * 2, axis=-1)
    sin = jnp.concatenate([-jnp.sin(ang_r), jnp.sin(ang_r), -jnp.sin(ang_c), jnp.sin(ang_c)], axis=-1)
    return jnp.tile(cos, (1, 2)), jnp.tile(sin, (1, 2))


def _lane(shape):
    return lax.broadcasted_iota(jnp.int32, shape, 1)


def _rope_partner(t, lane):
    return jnp.where((lane % 32) < 16, pltpu.roll(t, LANES - 16, 1), pltpu.roll(t, 16, 1))


def _half_mean(s, lane):
    lo = jnp.sum(jnp.where(lane < HEAD_DIM, s, 0.0), axis=-1, keepdims=True)
    hi = jnp.sum(jnp.where(lane < HEAD_DIM, 0.0, s), axis=-1, keepdims=True)
    return jnp.where(lane < HEAD_DIM, lo, hi) * (1.0 / HEAD_DIM)


def _prep(proj, cos, sin, qg, kg, *, tm, name):
    m = proj.shape[0]

    def body(p_ref, cos_ref, sin_ref, qg_ref, kg_ref, qa_ref, ka_ref, va_ref, qb_ref, kb_ref, vb_ref):
        lane = _lane((tm, LANES))
        cosv, sinv = cos_ref[...], sin_ref[...]
        low = lane < HEAD_DIM

        def rope(t):
            return t * cosv + _rope_partner(t, lane) * sinv

        def rms(t, g):
            return t * lax.rsqrt(_half_mean(t * t, lane) + QK_EPS) * g

        def place(q_ref, j, chunk):
            sw = pltpu.roll(chunk, HEAD_DIM, 1)
            if j < 2:
                h0, h1 = jnp.where(low, chunk, 0.0), jnp.where(low, sw, 0.0)
            else:
                h0, h1 = jnp.where(low, 0.0, sw), jnp.where(low, 0.0, chunk)
            q_ref[2 * j] = h0.T.astype(BF16)
            q_ref[2 * j + 1] = h1.T.astype(BF16)

        for j in range(4):
            place(qa_ref, j, rope(p_ref[:, j * LANES:(j + 1) * LANES]) * Q_SCALE)
            place(qb_ref, j, rope(rms(p_ref[:, 768 + j * LANES:768 + (j + 1) * LANES], qg_ref[...])) * Q_SCALE)
        ka_ref[...] = rope(p_ref[:, 512:640]).astype(BF16)
        va_ref[...] = p_ref[:, 640:768].astype(BF16)
        kb_ref[...] = rope(rms(p_ref[:, 1280:1408], kg_ref[...])).astype(BF16)
        vb_ref[...] = p_ref[:, 1408:1536].astype(BF16)

    row = pl.BlockSpec((tm, LANES), lambda i: (i, 0))
    qspec = pl.BlockSpec((N_HEADS, LANES, tm), lambda i: (0, 0, i))
    return _call(
        body, name=name, grid=(m // tm,),
        in_specs=[pl.BlockSpec((tm, OFF_GA), lambda i: (i, 0)), row, row, _full((1, LANES)), _full((1, LANES))],
        out_specs=[qspec, row, row, qspec, row, row],
        out_shape=[_sds((N_HEADS, LANES, m), BF16), _sds((m, LANES), BF16), _sds((m, LANES), BF16),
                   _sds((N_HEADS, LANES, m), BF16), _sds((m, LANES), BF16), _sds((m, LANES), BF16)],
        sem=("parallel",),
    )(proj, cos, sin, qg, kg)


def _prep_bwd(dqa, dka, dva, dqb, dkb, dvb, proj, cos, sin, qg, kg, dgl, *, tm, name):
    m = proj.shape[0]

    def body(dqa_ref, dka_ref, dva_ref, dqb_ref, dkb_ref, dvb_ref, p_ref, cos_ref, sin_ref, qg_ref, kg_ref,
             dgl_ref, dp_ref, dqg_ref, dkg_ref):
        i = pl.program_id(0)
        lane = _lane((tm, LANES))
        cosv, sinv = cos_ref[...], sin_ref[...]
        low = lane < HEAD_DIM

        @pl.when(i == 0)
        def _():
            dqg_ref[...] = jnp.zeros(dqg_ref.shape, F32)
            dkg_ref[...] = jnp.zeros(dkg_ref.shape, F32)

        def unrope(d):
            return d * cosv - _rope_partner(d, lane) * sinv

        def unplace(dq_ref, j):
            d0, d1 = dq_ref[2 * j].T, dq_ref[2 * j + 1].T
            if j < 2:
                return jnp.where(low, d0, pltpu.roll(d1, HEAD_DIM, 1))
            return jnp.where(low, pltpu.roll(d0, HEAD_DIM, 1), d1)

        def unrms(dtn, t, g):
            r = lax.rsqrt(_half_mean(t * t, lane) + QK_EPS)
            u = dtn * g
            dt = r * u - t * (r * r * r) * _half_mean(u * t, lane)
            return dt, jnp.sum(dtn * t * r, axis=0, keepdims=True)

        for j in range(4):
            dp_ref[:, j * LANES:(j + 1) * LANES] = (unrope(unplace(dqa_ref, j)) * Q_SCALE).astype(BF16)
            c0 = 768 + j * LANES
            dt, dg = unrms(unrope(unplace(dqb_ref, j)) * Q_SCALE, p_ref[:, c0:c0 + LANES], qg_ref[...])
            dp_ref[:, c0:c0 + LANES] = dt.astype(BF16)
            dqg_ref[:, j * LANES:(j + 1) * LANES] += dg
        dp_ref[:, 512:640] = unrope(dka_ref[...]).astype(BF16)
        dp_ref[:, 640:768] = dva_ref[...].astype(BF16)
        dt, dg = unrms(unrope(dkb_ref[...]), p_ref[:, 1280:1408], kg_ref[...])
        dp_ref[:, 1280:1408] = dt.astype(BF16)
        dkg_ref[...] += dg
        dp_ref[:, 1408:1536] = dvb_ref[...].astype(BF16)
        dp_ref[:, OFF_GA:] = dgl_ref[...]

    row = pl.BlockSpec((tm, LANES), lambda i: (i, 0))
    qspec = pl.BlockSpec((N_HEADS, LANES, tm), lambda i: (0, 0, i))
    return _call(
        body, name=name, grid=(m // tm,),
        in_specs=[qspec, row, row, qspec, row, row, pl.BlockSpec((tm, OFF_GA), lambda i: (i, 0)), row, row,
                  _full((1, LANES)), _full((1, LANES)), pl.BlockSpec((tm, IN_COLS - OFF_GA), lambda i: (i, 0))],
        out_specs=[pl.BlockSpec((tm, IN_COLS), lambda i: (i, 0)), _full((1, 512)), _full((1, LANES))],
        out_shape=[_sds((m, IN_COLS), BF16), _sds((1, 512)), _sds((1, LANES))],
        sem=("arbitrary",),
    )(dqa, dka, dva, dqb, dkb, dvb, proj, cos, sin, qg, kg, dgl)


def _attn_glob_fwd(qt, k, vt, kc, vct, *, tq, tk):
    nh, _, s = qt.shape
    nc = kc.shape[0]
    n_chunks = s // tk
    half = LANES // 2

    def body(qt_ref, k_ref, vt_ref, kc_ref, vct_ref, ot_ref, lse_ref, mrun_ref, p_hbm,
             acc_sc, st_sc, stage_sc, stagec_sc, sems, semc):
        h, i = pl.program_id(0), pl.program_id(1)
        qtv = qt_ref[...]
        acc_sc[...] = jnp.zeros(acc_sc.shape, F32)

        def p_out(slot, c):
            return pltpu.make_async_copy(stage_sc.at[slot], p_hbm.at[h, i, pl.ds(pl.multiple_of(c * tk, tk), tk), :],
                                         sems.at[slot])

        def update(st, vtv, m_old):
            m_new = jnp.maximum(m_old, jnp.max(st, axis=0, keepdims=True))
            pb = jnp.exp(st - m_new).astype(BF16)
            acc_sc[...] = acc_sc[...] * jnp.exp(m_old - m_new) + _dot(vtv, pb)
            return m_new, pb

        m, pbc = update(_dot(kc_ref[...], qtv), vct_ref[...], jnp.full((1, tq), NEG, F32))
        mrun_ref[pl.ds(n_chunks, 1), :] = m
        stagec_sc[...] = pbc
        ctx_out = pltpu.make_async_copy(stagec_sc, p_hbm.at[h, i, pl.ds(s, nc), :], semc)
        ctx_out.start()

        def step(c, st, m_old):
            slot = c % 2
            off = pl.multiple_of(c * tk, tk)
            nxt = pl.multiple_of(jnp.minimum(c + 1, n_chunks - 1) * tk, tk)
            st_next = _dot(k_ref[pl.ds(nxt, tk), :], qtv)
            m_new, pb = update(st, vt_ref[:, pl.ds(off, tk)], m_old)
            mrun_ref[pl.ds(c, 1), :] = m_new
            stage_sc[slot] = pb
            p_out(slot, c).start()
            return st_next, m_new

        def loop(c, m_old):
            st_next, m_new = step(c, st_sc[...], m_old)
            p_out(1 - c % 2, c - 1).wait()
            st_sc[...] = st_next
            return m_new

        stage_sc[1] = jnp.zeros((tk, tq), BF16)
        pltpu.make_async_copy(stage_sc.at[1], p_hbm.at[h, i, pl.ds(s + nc, tk), :], sems.at[1]).start()
        st_sc[...] = _dot(k_ref[pl.ds(0, tk), :], qtv)
        m = lax.fori_loop(0, n_chunks, loop, m)
        p_out((n_chunks - 1) % 2, n_chunks - 1).wait()
        ctx_out.wait()
        acc = acc_sc[...]
        l = jnp.where(h < nh // N_KV, acc[half:half + 1], acc[0:1])
        ot_ref[...] = (acc / l).astype(BF16)
        lse_ref[...] = m + jnp.log(l)

    grp = nh // N_KV
    return _call(
        body, name="attn_glob_fwd", grid=(nh, s // tq),
        in_specs=[pl.BlockSpec((None, LANES, tq), lambda h, i: (h, 0, i)), _full((s, LANES)),
                  pl.BlockSpec((None, LANES, s), lambda h, i: (h // grp, 0, 0)), _full((nc, LANES)),
                  pl.BlockSpec((None, LANES, nc), lambda h, i: (h // grp, 0, 0))],
        out_specs=[pl.BlockSpec((None, LANES, tq), lambda h, i: (h, 0, i)),
                   pl.BlockSpec((None, 1, tq), lambda h, i: (h, 0, i)),
                   pl.BlockSpec((None, n_chunks + 1, tq), lambda h, i: (h, 0, i)), ANY],
        out_shape=[_sds((nh, LANES, s), BF16), _sds((nh, 1, s)), _sds((nh, n_chunks + 1, s)),
                   _sds((nh, s // tq, s + nc + tk, tq), BF16)],
        scratch=[pltpu.VMEM((LANES, tq), F32), pltpu.VMEM((tk, tq), F32), pltpu.VMEM((2, tk, tq), BF16),
                 pltpu.VMEM((nc, tq), BF16), pltpu.SemaphoreType.DMA((2,)), pltpu.SemaphoreType.DMA],
        sem=("parallel", "parallel"),
    )(qt, k, vt, kc, vct)


P_AHEAD = 3


def _attn_glob_bwd(qt, dot, ot, lse, mrun, p, kt, v, kct, vc, *, tq, tk):
    nh, _, s = qt.shape
    nc = vc.shape[0]
    n_q = s // tq
    n_chunks = s // tk
    n_run = mrun.shape[1] - 1
    per_run = n_chunks // n_run

    def body(qt_ref, dot_ref, ot_ref, lse_ref, mrun_ref, p_hbm, kt_ref, v_ref, kct_ref, vc_ref,
             dqt_ref, dkt_ref, dvt_ref, dkct_ref, dvct_ref, acc_sc, dp_sc, dkt_sc, dvt_sc, p_sc, pc_sc, sems, semc):
        h, i = pl.program_id(0), pl.program_id(1)

        @pl.when(jnp.logical_and(h == 0, i == 0))
        def _():
            dkct_ref[...] = jnp.zeros(dkct_ref.shape, F32)
            dvct_ref[...] = jnp.zeros(dvct_ref.shape, F32)
            dkt_sc[...] = jnp.zeros(dkt_sc.shape, F32)
            dvt_sc[...] = jnp.zeros(dvt_sc.shape, F32)


        def p_in(slot, c):
            return pltpu.make_async_copy(p_hbm.at[h, i, pl.ds(pl.multiple_of(c * tk, tk), tk), :], p_sc.at[slot],
                                         sems.at[slot])

        ctx_in = pltpu.make_async_copy(p_hbm.at[h, i, pl.ds(s, nc), :], pc_sc, semc)
        ctx_in.start()
        for c in range(P_AHEAD):
            p_in(c, min(c, n_chunks - 1)).start()
        qtv, dotv, lse = qt_ref[...], dot_ref[...], lse_ref[...]
        delta = jnp.sum(dotv.astype(F32) * ot_ref[...].astype(F32), axis=0, keepdims=True)

        def grads(pt_stored, m_row, dpt):
            pt = pt_stored.astype(F32) * jnp.exp(m_row - lse)
            return pt.astype(BF16), (pt * (dpt - delta)).astype(BF16)

        dp_sc[...] = _dot(v_ref[pl.ds(0, tk), :], dotv)
        ctx_in.wait()
        pb, dsb = grads(pc_sc[...], mrun_ref[pl.ds(n_run, 1), :], _dot(vc_ref[...], dotv))
        acc_sc[...] = _dot(kct_ref[...], dsb)
        dkct_ref[...] += _dot_nt(qtv, dsb)
        dvct_ref[...] += _dot_nt(dotv, pb)

        def loop(c, carry):
            slot = c % (P_AHEAD + 1)
            off = pl.multiple_of(c * tk, tk)
            nxt = pl.multiple_of(jnp.minimum(c + 1, n_chunks - 1) * tk, tk)
            p_in(slot, c).wait()
            p_in((c + P_AHEAD) % (P_AHEAD + 1), jnp.minimum(c + P_AHEAD, n_chunks - 1)).start()
            dpt = dp_sc[...]
            dp_next = _dot(v_ref[pl.ds(nxt, tk), :], dotv)
            pb, dsb = grads(p_sc[slot], mrun_ref[pl.ds(c // per_run, 1), :], dpt)
            acc_sc[...] += _dot(kt_ref[:, pl.ds(off, tk)], dsb)
            dkt_sc[:, pl.ds(off, tk)] += _dot_nt(qtv, dsb)
            dvt_sc[:, pl.ds(off, tk)] += _dot_nt(dotv, pb)
            dp_sc[...] = dp_next
            return carry

        lax.fori_loop(0, n_chunks, loop, 0)
        for c in range(n_chunks, n_chunks + P_AHEAD):
            p_in(c % (P_AHEAD + 1), n_chunks - 1).wait()
        dqt_ref[...] = acc_sc[...]

        @pl.when(jnp.logical_and(h == nh - 1, i == n_q - 1))
        def _():
            pltpu.sync_copy(dkt_sc, dkt_ref)
            pltpu.sync_copy(dvt_sc, dvt_ref)

    qs = pl.BlockSpec((None, LANES, tq), lambda h, i: (h, 0, i))
    rs = pl.BlockSpec((None, 1, tq), lambda h, i: (h, 0, i))
    return _call(
        body, name="attn_glob_bwd", grid=(nh, n_q),
        in_specs=[qs, qs, qs, rs, pl.BlockSpec((None, n_run + 1, tq), lambda h, i: (h, 0, i)), ANY,
                  _full((LANES, s)), _full((s, LANES)), _full((LANES, nc)), _full((nc, LANES))],
        out_specs=[qs, ANY, ANY, _full((LANES, nc)), _full((LANES, nc))],
        out_shape=[_sds((nh, LANES, s)), _sds((LANES, s)), _sds((LANES, s)), _sds((LANES, nc)), _sds((LANES, nc))],
        scratch=[pltpu.VMEM((LANES, tq), F32), pltpu.VMEM((tk, tq), F32), pltpu.VMEM((LANES, s), F32),
                 pltpu.VMEM((LANES, s), F32), pltpu.VMEM((P_AHEAD + 1, tk, tq), BF16), pltpu.VMEM((nc, tq), BF16),
                 pltpu.SemaphoreType.DMA((P_AHEAD + 1,)), pltpu.SemaphoreType.DMA],
        sem=("arbitrary", "arbitrary"),
    )(qt, dot, ot, lse, mrun, p, kt, v, kct, vc)


WIN_SPAN = 2 * WINDOW


def _band(rows0, cols0, shape):
    r = rows0 + lax.broadcasted_iota(jnp.int32, shape, 0)
    c = cols0 + lax.broadcasted_iota(jnp.int32, shape, 1)
    return jnp.abs(r - c) <= WINDOW


def _win_start(blk, t, s):
    return pl.multiple_of(jnp.clip(blk * t - WINDOW, 0, s - t - WIN_SPAN), WINDOW)


def _attn_win_fwd(qt, k, vt, kc, vct, sink, *, tq):
    nh, _, s = qt.shape
    nc = kc.shape[0]
    tw = tq + WIN_SPAN
    half = LANES // 2
    grp = nh // N_KV

    def body(sink_ref, qt_ref, k_ref, vt_ref, kc_ref, vct_ref, ot_ref, lse_ref):
        h, i = pl.program_id(0), pl.program_id(1)
        k0 = _win_start(i, tq, s)
        qtv = qt_ref[...]
        st = jnp.where(_band(k0, i * tq, (tw, tq)), _dot(k_ref[pl.ds(k0, tw), :], qtv), NEG)
        stc = _dot(kc_ref[...], qtv)
        snk = sink_ref[h]
        m = jnp.maximum(jnp.maximum(jnp.max(st, axis=0, keepdims=True), jnp.max(stc, axis=0, keepdims=True)), snk)
        acc = (_dot(vt_ref[:, pl.ds(k0, tw)], jnp.exp(st - m).astype(BF16))
               + _dot(vct_ref[...], jnp.exp(stc - m).astype(BF16)))
        l = jnp.where(h < grp, acc[half:half + 1], acc[0:1]) + jnp.exp(snk - m)
        ot_ref[...] = (acc / l).astype(BF16)
        lse_ref[...] = m + jnp.log(l)

    return _call(
        body, name="attn_win_fwd", grid=(nh, s // tq),
        in_specs=[pl.BlockSpec(memory_space=pltpu.SMEM),
                  pl.BlockSpec((None, LANES, tq), lambda h, i: (h, 0, i)), _full((s, LANES)),
                  pl.BlockSpec((None, LANES, s), lambda h, i: (h // grp, 0, 0)), _full((nc, LANES)),
                  pl.BlockSpec((None, LANES, nc), lambda h, i: (h // grp, 0, 0))],
        out_specs=[pl.BlockSpec((None, LANES, tq), lambda h, i: (h, 0, i)),
                   pl.BlockSpec((None, 1, tq), lambda h, i: (h, 0, i))],
        out_shape=[_sds((nh, LANES, s), BF16), _sds((nh, 1, s))],
        sem=("parallel", "parallel"),
    )(sink, qt, k, vt, kc, vct)


def _attn_win_bwd(qt, dot, ot, lse, k, kt, v, kc, kct, vc, sink, *, tq):
    nh, _, s = qt.shape
    nc = kc.shape[0]
    tw = tq + WIN_SPAN
    nq = s // tq

    def body(sink_ref, qt_ref, dot_ref, ot_ref, lse_ref, k_ref, kt_ref, v_ref, kc_ref, kct_ref, vc_ref,
             dqt_ref, dkt_ref, dvt_ref, dkct_ref, dvct_ref, dsk_ref, dkt_sc, dvt_sc):
        h, i = pl.program_id(0), pl.program_id(1)

        @pl.when(jnp.logical_and(h == 0, i == 0))
        def _():
            dkct_ref[...] = jnp.zeros(dkct_ref.shape, F32)
            dvct_ref[...] = jnp.zeros(dvct_ref.shape, F32)
            dkt_sc[...] = jnp.zeros(dkt_sc.shape, F32)
            dvt_sc[...] = jnp.zeros(dvt_sc.shape, F32)

        k0 = _win_start(i, tq, s)
        span = pl.ds(k0, tw)
        qtv, dotv, lse = qt_ref[...], dot_ref[...], lse_ref[...]
        delta = jnp.sum(dotv.astype(F32) * ot_ref[...].astype(F32), axis=0, keepdims=True)
        pt = jnp.where(_band(k0, i * tq, (tw, tq)), jnp.exp(_dot(k_ref[span, :], qtv) - lse), 0.0)
        dsb = (pt * (_dot(v_ref[span, :], dotv) - delta)).astype(BF16)
        pct = jnp.exp(_dot(kc_ref[...], qtv) - lse)
        dscb = (pct * (_dot(vc_ref[...], dotv) - delta)).astype(BF16)
        dqt_ref[...] = _dot(kt_ref[:, span], dsb) + _dot(kct_ref[...], dscb)
        dkt_sc[:, span] += _dot_nt(qtv, dsb)
        dvt_sc[:, span] += _dot_nt(dotv, pt.astype(BF16))
        dkct_ref[...] += _dot_nt(qtv, dscb)
        dvct_ref[...] += _dot_nt(dotv, pct.astype(BF16))
        dsk = -jnp.sum(jnp.exp(sink_ref[h] - lse) * delta)
        dsk_ref[...] = jnp.full(dsk_ref.shape, dsk, F32)

        @pl.when(jnp.logical_and(h == nh - 1, i == nq - 1))
        def _():
            pltpu.sync_copy(dkt_sc, dkt_ref)
            pltpu.sync_copy(dvt_sc, dvt_ref)

    qs = pl.BlockSpec((None, LANES, tq), lambda h, i: (h, 0, i))
    rs = pl.BlockSpec((None, 1, tq), lambda h, i: (h, 0, i))
    return _call(
        body, name="attn_win_bwd", grid=(nh, nq),
        in_specs=[pl.BlockSpec(memory_space=pltpu.SMEM), qs, qs, qs, rs, _full((s, LANES)), _full((LANES, s)),
                  _full((s, LANES)), _full((nc, LANES)), _full((LANES, nc)), _full((nc, LANES))],
        out_specs=[qs, ANY, ANY, _full((LANES, nc)), _full((LANES, nc)),
                   pl.BlockSpec((None, None, 8, LANES), lambda h, i: (h, i, 0, 0))],
        out_shape=[_sds((nh, LANES, s)), _sds((LANES, s)), _sds((LANES, s)), _sds((LANES, nc)), _sds((LANES, nc)),
                   _sds((nh, nq, 8, LANES))],
        scratch=[pltpu.VMEM((LANES, s), F32), pltpu.VMEM((LANES, s), F32)],
        sem=("arbitrary", "arbitrary"),
    )(sink, qt, dot, ot, lse, k, kt, v, kc, kct, vc)


def _ln_fwd(z, g, b):
    mu = jnp.mean(z, axis=-1, keepdims=True)
    zc = z - mu
    r = lax.rsqrt(jnp.mean(zc * zc, axis=-1, keepdims=True) + LN_EPS)
    return zc * r * g + b, mu, r


def _ln_bwd(dy, xhat, r, g):
    dxh = dy * g
    return r * (dxh - jnp.mean(dxh, axis=-1, keepdims=True) - xhat * jnp.mean(dxh * xhat, axis=-1, keepdims=True))


def _heads_matmul(ot_ref, w_ref):
    acc = _dot_tn(ot_ref[0], w_ref[0])
    for h in range(1, N_HEADS):
        acc += _dot_tn(ot_ref[h], w_ref[h])
    return acc


def _gate_specs(tm):
    return [pl.BlockSpec((tm, 512), functools.partial(lambda i, b: (i, b), b=OFF_GA // 512 + b)) for b in range(4)]


def _merge_fwd(oat, obt, proj, x, gate1, wba, wbb, w_out, ln_g, ln_b, *, tm):
    s = x.shape[0]

    def body(oa_ref, ob_ref, g0, g1, g2, g3, x_ref, gt_ref, wba_ref, wbb_ref, wo_ref, lg_ref, lb_ref,
             x1_ref, y_ref, mu_ref, r_ref, pa_ref, pb_ref, mg_ref):
        ga = _sigmoid(jnp.concatenate([g0[...], g1[...]], axis=1))
        gb = _sigmoid(jnp.concatenate([g2[...], g3[...]], axis=1))
        pa, pb = _heads_matmul(oa_ref, wba_ref), _heads_matmul(ob_ref, wbb_ref)
        merged = (ga * pa + gb * pb).astype(BF16)
        y = _dot(merged, wo_ref[...])
        x1, mu, r = _ln_fwd(ALPHA * x_ref[...] + gt_ref[...] * y, lg_ref[...], lb_ref[...])
        x1_ref[...] = x1
        y_ref[...] = y
        mu_ref[...] = mu
        r_ref[...] = r
        pa_ref[...] = pa.astype(BF16)
        pb_ref[...] = pb.astype(BF16)
        mg_ref[...] = merged

    hts = pl.BlockSpec((N_HEADS, LANES, tm), lambda i: (0, 0, i))
    row = pl.BlockSpec((tm, D_MODEL), lambda i: (i, 0))
    col = pl.BlockSpec((tm, 1), lambda i: (i, 0))
    vec = _full((1, D_MODEL))
    wh = _full((N_HEADS, LANES, D_MODEL))
    return _call(
        body, name="merge_fwd", grid=(s // tm,),
        in_specs=[hts, hts, *_gate_specs(tm), row, vec, wh, wh, _full((D_MODEL, D_MODEL)), vec, vec],
        out_specs=[row, row, col, col, row, row, row],
        out_shape=[_sds((s, D_MODEL)), _sds((s, D_MODEL)), _sds((s, 1)), _sds((s, 1)), _sds((s, D_MODEL), BF16),
                   _sds((s, D_MODEL), BF16), _sds((s, D_MODEL), BF16)],
        sem=("parallel",),
    )(oat, obt, proj, proj, proj, proj, x, gate1, wba, wbb, w_out, ln_g, ln_b)


def _merge_bwd(dy, oat, obt, pa, pb, proj, wba, wbb, w_out, *, tm):
    s = dy.shape[0]

    def body(dy_ref, oat_ref, obt_ref, pa_ref, pb_ref, g0, g1, g2, g3, wba_ref, wbb_ref, wo_ref,
             dgl_ref, doat_ref, dobt_ref, dwa_ref, dwb_ref):
        @pl.when(pl.program_id(0) == 0)
        def _():
            dwa_ref[...] = jnp.zeros(dwa_ref.shape, F32)
            dwb_ref[...] = jnp.zeros(dwb_ref.shape, F32)

        dm = _dot_nt(dy_ref[...], wo_ref[...])
        ga = _sigmoid(jnp.concatenate([g0[...], g1[...]], axis=1))
        gb = _sigmoid(jnp.concatenate([g2[...], g3[...]], axis=1))
        pa, pb = pa_ref[...].astype(F32), pb_ref[...].astype(F32)
        dgl_ref[:, :D_MODEL] = (dm * pa * ga * (1.0 - ga)).astype(BF16)
        dgl_ref[:, D_MODEL:] = (dm * pb * gb * (1.0 - gb)).astype(BF16)
        dpa, dpb = (dm * ga).astype(BF16), (dm * gb).astype(BF16)
        for h in range(N_HEADS):
            doat_ref[h] = _dot_nt(wba_ref[h], dpa).astype(BF16)
            dobt_ref[h] = _dot_nt(wbb_ref[h], dpb).astype(BF16)
            dwa_ref[h] += _dot(oat_ref[h], dpa)
            dwb_ref[h] += _dot(obt_ref[h], dpb)

    hts = pl.BlockSpec((N_HEADS, LANES, tm), lambda i: (0, 0, i))
    row = pl.BlockSpec((tm, D_MODEL), lambda i: (i, 0))
    wh = _full((N_HEADS, LANES, D_MODEL))
    return _call(
        body, name="merge_bwd", grid=(s // tm,),
        in_specs=[row, hts, hts, row, row, *_gate_specs(tm), wh, wh, _full((D_MODEL, D_MODEL))],
        out_specs=[pl.BlockSpec((tm, 2 * D_MODEL), lambda i: (i, 0)), hts, hts, wh, wh],
        out_shape=[_sds((s, 2 * D_MODEL), BF16), _sds((N_HEADS, LANES, s), BF16), _sds((N_HEADS, LANES, s), BF16),
                   _sds((N_HEADS, LANES, D_MODEL)), _sds((N_HEADS, LANES, D_MODEL))],
        sem=("arbitrary",),
    )(dy, oat, obt, pa, pb, proj, proj, proj, proj, wba, wbb, w_out)


FF_TC = 256


def _shift_rows(t, prev_row, next_row):
    n = t.shape[0]
    r = lax.broadcasted_iota(jnp.int32, t.shape, 0)
    up = jnp.where(r == 0, prev_row, pltpu.roll(t, 1, 0))
    dn = jnp.where(r == n - 1, next_row, pltpu.roll(t, n - 1, 0))
    return up, dn


HALO = 16


def _halo_specs(tm, s, tc):
    nb = s // HALO
    main = pl.BlockSpec((2, tm, tc), lambda j, i: (0, i, j))
    prev = pl.BlockSpec((2, HALO, tc), lambda j, i: (0, jnp.maximum(i * (tm // HALO) - 1, 0), j))
    nxt = pl.BlockSpec((2, HALO, tc), lambda j, i: (0, jnp.minimum((i + 1) * (tm // HALO), nb - 1), j))
    return main, prev, nxt


def _halo_rows(prev_ref, next_ref, half, i, n_i):
    prev_row = jnp.where(i == 0, 0.0, prev_ref[half, HALO - 1:HALO, :].astype(F32))
    next_row = jnp.where(i == n_i - 1, 0.0, next_ref[half, 0:1, :].astype(F32))
    return prev_row, next_row


def _conv(t, prev_row, next_row, w, b):
    up, dn = _shift_rows(t, prev_row, next_row)
    return w[0:1, :] * up + w[1:2, :] * t + w[2:3, :] * dn + b


def _ffn_act_fwd(u, cw, cb, *, tm):
    _, s, ff = u.shape
    n_i = s // tm

    def body(u_ref, up_ref, un_ref, cw_ref, cb_ref, a_ref):
        i = pl.program_id(1)
        gc = _conv(u_ref[0].astype(F32), *_halo_rows(up_ref, un_ref, 0, i, n_i), cw_ref[0], cb_ref[0])
        vc = _conv(u_ref[1].astype(F32), *_halo_rows(up_ref, un_ref, 1, i, n_i), cw_ref[1], cb_ref[1])
        a_ref[...] = (gc * _sigmoid(gc) * vc).astype(BF16)

    main, prev, nxt = _halo_specs(tm, s, FF_TC)
    return _call(
        body, name="ffn_act_fwd", grid=(ff // FF_TC, n_i),
        in_specs=[main, prev, nxt, pl.BlockSpec((2, 3, FF_TC), lambda j, i: (0, 0, j)),
                  pl.BlockSpec((2, 1, FF_TC), lambda j, i: (0, 0, j))],
        out_specs=pl.BlockSpec((tm, FF_TC), lambda j, i: (i, j)),
        out_shape=_sds((s, ff), BF16), sem=("parallel", "parallel"),
    )(u, u, u, cw, cb)


def _ffn_act_bwd(dy2, w_down, u, cw, cb, *, tm):
    _, s, ff = u.shape
    n_i = s // tm

    def body(dy_ref, wd_ref, u_ref, up_ref, un_ref, cw_ref, cb_ref, dc_ref, dcw_ref, dcb_ref):
        i = pl.program_id(1)

        @pl.when(i == 0)
        def _():
            dcw_ref[...] = jnp.zeros(dcw_ref.shape, F32)
            dcb_ref[...] = jnp.zeros(dcb_ref.shape, F32)

        da = _dot_nt(dy_ref[...], wd_ref[...])
        ug, uv = u_ref[0].astype(F32), u_ref[1].astype(F32)
        ugp, ugn = _shift_rows(ug, *_halo_rows(up_ref, un_ref, 0, i, n_i))
        uvp, uvn = _shift_rows(uv, *_halo_rows(up_ref, un_ref, 1, i, n_i))
        wg, wv = cw_ref[0], cw_ref[1]
        gc = wg[0:1, :] * ugp + wg[1:2, :] * ug + wg[2:3, :] * ugn + cb_ref[0]
        vc = wv[0:1, :] * uvp + wv[1:2, :] * uv + wv[2:3, :] * uvn + cb_ref[1]
        sg = _sigmoid(gc)
        dg = da * vc * sg * (1.0 + gc * (1.0 - sg))
        dv = da * gc * sg
        dc_ref[0] = dg.astype(BF16)
        dc_ref[1] = dv.astype(BF16)
        for half, (d, taps) in enumerate(((dg, (ugp, ug, ugn)), (dv, (uvp, uv, uvn)))):
            for tap in range(3):
                dcw_ref[half, tap:tap + 1, :] += jnp.sum(d * taps[tap], axis=0, keepdims=True)
            dcb_ref[half] += jnp.sum(d, axis=0, keepdims=True)

    main, prev, nxt = _halo_specs(tm, s, FF_TC)
    return _call(
        body, name="ffn_act_bwd", grid=(ff // FF_TC, n_i),
        in_specs=[pl.BlockSpec((tm, D_MODEL), lambda j, i: (i, 0)), pl.BlockSpec((FF_TC, D_MODEL), lambda j, i: (j, 0)),
                  main, prev, nxt, pl.BlockSpec((2, 3, FF_TC), lambda j, i: (0, 0, j)),
                  pl.BlockSpec((2, 1, FF_TC), lambda j, i: (0, 0, j))],
        out_specs=[main, pl.BlockSpec((2, 3, FF_TC), lambda j, i: (0, 0, j)),
                   pl.BlockSpec((2, 1, FF_TC), lambda j, i: (0, 0, j))],
        out_shape=[_sds((2, s, ff), BF16), _sds((2, 3, ff)), _sds((2, 1, ff))],
        sem=("parallel", "arbitrary"),
    )(dy2, w_down, u, u, u, cw, cb)


def _conv_bwd_input(dc, cw, *, tm):
    _, s, ff = dc.shape
    n_i = s // tm

    def body(d_ref, dp_ref, dn_ref, cw_ref, du_ref):
        i = pl.program_id(1)
        for half in range(2):
            d = d_ref[half].astype(F32)
            up, dn = _shift_rows(d, *_halo_rows(dp_ref, dn_ref, half, i, n_i))
            w = cw_ref[half]
            du_ref[half] = (w[0:1, :] * dn + w[1:2, :] * d + w[2:3, :] * up).astype(BF16)

    main, prev, nxt = _halo_specs(tm, s, FF_TC)
    return _call(
        body, name="conv_bwd_input", grid=(ff // FF_TC, n_i),
        in_specs=[main, prev, nxt, pl.BlockSpec((2, 3, FF_TC), lambda j, i: (0, 0, j))],
        out_specs=main, out_shape=_sds((2, s, ff), BF16), sem=("parallel", "parallel"),
    )(dc, dc, dc, cw)


def _ffn_down_loss(a, w_down, x1, target, gate2, ln_g, ln_b, *, tm):
    s, ff = a.shape
    n_i = s // tm

    def body(a_ref, wd_ref, x1_ref, tg_ref, gt_ref, lg_ref, lb_ref, ls_ref, dy_ref, dx_ref, dg_ref, db_ref, dgt_ref):
        @pl.when(pl.program_id(0) == 0)
        def _():
            dg_ref[...] = jnp.zeros(dg_ref.shape, F32)
            db_ref[...] = jnp.zeros(db_ref.shape, F32)
            dgt_ref[...] = jnp.zeros(dgt_ref.shape, F32)

        y2 = _dot(a_ref[...], wd_ref[...])
        z = ALPHA * x1_ref[...] + gt_ref[...] * y2
        mu = jnp.mean(z, axis=-1, keepdims=True)
        zc = z - mu
        r = lax.rsqrt(jnp.mean(zc * zc, axis=-1, keepdims=True) + LN_EPS)
        xhat = zc * r
        diff = xhat * lg_ref[...] + lb_ref[...] - tg_ref[...]
        ls_ref[...] = jnp.full(ls_ref.shape, 0.5 / D_MODEL * jnp.sum(diff * diff), F32)
        dx2 = diff * (1.0 / D_MODEL)
        dg_ref[...] += jnp.sum(dx2 * xhat, axis=0, keepdims=True)
        db_ref[...] += jnp.sum(dx2, axis=0, keepdims=True)
        dz = _ln_bwd(dx2, xhat, r, lg_ref[...])
        dgt_ref[...] += jnp.sum(dz * y2, axis=0, keepdims=True)
        dy_ref[...] = (gt_ref[...] * dz).astype(BF16)
        dx_ref[...] = ALPHA * dz

    row = pl.BlockSpec((tm, D_MODEL), lambda i: (i, 0))
    vec = _full((1, D_MODEL))
    return _call(
        body, name="ffn_down_loss", grid=(n_i,),
        in_specs=[pl.BlockSpec((tm, ff), lambda i: (i, 0)), _full((ff, D_MODEL)), row, row, vec, vec, vec],
        out_specs=[pl.BlockSpec((None, 8, LANES), lambda i: (i, 0, 0)), row, row, vec, vec, vec],
        out_shape=[_sds((n_i, 8, LANES)), _sds((s, D_MODEL), BF16), _sds((s, D_MODEL)),
                   _sds((1, D_MODEL)), _sds((1, D_MODEL)), _sds((1, D_MODEL))],
        sem=("arbitrary",),
    )(a, w_down, x1, target, gate2, ln_g, ln_b)


def _ffn_up_bwd(du, wup4, dx1a, x1, scale2, x, y, mu1, r1, gate1, ln_g, *, tm):
    s = x.shape[0]
    nb, _, ns = wup4.shape

    def body(du_ref, w_ref, dxa_ref, x1_ref, sc_ref, x_ref, y_ref, mu_ref, r_ref, gt_ref, lg_ref,
             dxo_ref, dy_ref, dsc_ref, dsh_ref, dg_ref, db_ref, dgt_ref, acc):
        i, k = pl.program_id(0), pl.program_id(1)

        @pl.when(jnp.logical_and(i == 0, k == 0))
        def _():
            for ref in (dsc_ref, dsh_ref, dg_ref, db_ref, dgt_ref):
                ref[...] = jnp.zeros(ref.shape, F32)

        @pl.when(k == 0)
        def _():
            acc[...] = jnp.zeros(acc.shape, F32)

        acc[...] += _dot_nt(du_ref[...], w_ref[...])

        @pl.when(k == nb - 1)
        def _():
            dh = acc[...]
            x1 = x1_ref[...]
            dsc_ref[...] += jnp.sum(dh * x1, axis=0, keepdims=True)
            dsh_ref[...] += jnp.sum(dh, axis=0, keepdims=True)
            dx1 = dxa_ref[...] + dh * (1.0 + sc_ref[...])
            yv = y_ref[...]
            xhat = (ALPHA * x_ref[...] + gt_ref[...] * yv - mu_ref[...]) * r_ref[...]
            dg_ref[...] += jnp.sum(dx1 * xhat, axis=0, keepdims=True)
            db_ref[...] += jnp.sum(dx1, axis=0, keepdims=True)
            dz = _ln_bwd(dx1, xhat, r_ref[...], lg_ref[...])
            dgt_ref[...] += jnp.sum(dz * yv, axis=0, keepdims=True)
            dy_ref[...] = (gt_ref[...] * dz).astype(BF16)
            dxo_ref[...] = ALPHA * dz

    row = pl.BlockSpec((tm, D_MODEL), lambda i, k: (i, 0))
    col = pl.BlockSpec((tm, 1), lambda i, k: (i, 0))
    vec = _full((1, D_MODEL))
    return _call(
        body, name="ffn_up_bwd", grid=(s // tm, nb),
        in_specs=[pl.BlockSpec((None, tm, ns), lambda i, k: (k // 2, i, k % 2)),
                  pl.BlockSpec((None, D_MODEL, ns), lambda i, k: (k, 0, 0)),
                  row, row, vec, row, row, col, col, vec, vec],
        out_specs=[row, row, vec, vec, vec, vec, vec],
        out_shape=[_sds((s, D_MODEL)), _sds((s, D_MODEL), BF16)] + [_sds((1, D_MODEL))] * 5,
        scratch=[pltpu.VMEM((tm, D_MODEL), F32)],
        sem=("arbitrary", "arbitrary"),
    )(du, wup4, dx1a, x1, scale2, x, y, mu1, r1, gate1, ln_g)


def _mm_nt4_mod_bwd(dp, w4, dxa, x, scale, *, tm, name):
    m = x.shape[0]
    nb, kdim, ns = w4.shape

    def body(dp_ref, w_ref, dxa_ref, x_ref, sc_ref, dx_ref, dsc_ref, dsh_ref, acc):
        i, k = pl.program_id(0), pl.program_id(1)

        @pl.when(jnp.logical_and(i == 0, k == 0))
        def _():
            dsc_ref[...] = jnp.zeros(dsc_ref.shape, F32)
            dsh_ref[...] = jnp.zeros(dsh_ref.shape, F32)

        @pl.when(k == 0)
        def _():
            acc[...] = jnp.zeros(acc.shape, F32)

        acc[...] += _dot_nt(dp_ref[...], w_ref[...])

        @pl.when(k == nb - 1)
        def _():
            dh = acc[...]
            dsc_ref[...] += jnp.sum(dh * x_ref[...], axis=0, keepdims=True)
            dsh_ref[...] += jnp.sum(dh, axis=0, keepdims=True)
            dx_ref[...] = dxa_ref[...] + dh * (1.0 + sc_ref[...])

    row = pl.BlockSpec((tm, kdim), lambda i, k: (i, 0))
    vec = _full((1, kdim))
    return _call(
        body, name=name, grid=(m // tm, nb),
        in_specs=[pl.BlockSpec((tm, ns), lambda i, k: (i, k)), pl.BlockSpec((None, kdim, ns), lambda i, k: (k, 0, 0)),
                  row, row, vec],
        out_specs=[row, vec, vec],
        out_shape=[_sds((m, kdim)), _sds((1, kdim)), _sds((1, kdim))],
        scratch=[pltpu.VMEM((tm, kdim), F32)],
        sem=("arbitrary", "arbitrary"),
    )(dp, w4, dxa, x, scale)


def _pad_heads_w(w):
    w8 = w.reshape(N_HEADS, HEAD_DIM, w.shape[-1])
    z = jnp.zeros_like(w8)
    first = (jnp.arange(N_HEADS) < N_HEADS // N_KV)[:, None, None]
    return jnp.where(first, jnp.concatenate([w8, z], axis=1), jnp.concatenate([z, w8], axis=1))


def _unpad_heads_w(g):
    first = (jnp.arange(N_HEADS) < N_HEADS // N_KV)[:, None, None]
    return jnp.where(first, g[:, :HEAD_DIM], g[:, HEAD_DIM:]).reshape(N_HEADS * HEAD_DIM, g.shape[-1])


def _ones_beside(vt):
    half = vt.shape[0] // 2
    ones = jnp.ones((half, vt.shape[1]), vt.dtype)
    return jnp.stack([jnp.concatenate([vt[:half], ones], axis=0), jnp.concatenate([ones, vt[half:]], axis=0)])


def _rep8(a):
    return jnp.broadcast_to(a.reshape(1, -1), (8, a.size))


def _first_row(a):
    r8 = _rep8(a)
    return jnp.where(lax.broadcasted_iota(jnp.int32, r8.shape, 0) == 0, r8, 0.0)


def _to_blocks4(w):
    k, n = w.shape
    return w.reshape(k, N_CHIPS, n // N_CHIPS).transpose(1, 0, 2)


def _local_step(x, c, ctx, c_ctx, wmod4, b_mod, win4, b_in, sink, qn, kn, wba, wbb, w_out, ln1_g, ln1_b,
                wup4, cw, cb, w_down, ln2_g, ln2_b, target):
    s, nc = x.shape[0], ctx.shape[0]
    tm = min(512, s)
    tm2 = min(256, s)
    tl = min(1024, s)
    tx = min(2048, s)
    zvec = jnp.zeros((1, D_MODEL), F32)

    cc = jnp.concatenate([_rep8(c), _rep8(c_ctx)], axis=0)
    mods = _mm_nn4(cc, zvec, zvec, wmod4, b_mod, mode="silu", split_out=False, out_dtype=F32, tm=16, name="mod_vectors")
    shift1, scale1, gate1, shift2, scale2, gate2 = [mods[0:1, i * D_MODEL:(i + 1) * D_MODEL] for i in range(6)]
    shift_c, scale_c = mods[8:9, :D_MODEL], mods[8:9, D_MODEL:2 * D_MODEL]

    cos, sin = _rope_tables(s)
    cos_c, sin_c = jnp.ones((nc, LANES), F32), jnp.zeros((nc, LANES), F32)
    qg, kg = jnp.tile(qn, (1, 2)), jnp.tile(kn, (1, 2))

    proj_c = _mm_nn4(ctx, shift_c, scale_c, win4, b_in, mode="modulate", split_out=False, out_dtype=F32, tm=nc,
                     name="in_proj_ctx")
    _, kac, vac, _, kbc, vbc = _prep(proj_c, cos_c, sin_c, qg, kg, tm=nc, name="prep_ctx")
    proj = _mm_nn4(x, shift1, scale1, win4, b_in, mode="modulate", split_out=False, out_dtype=F32, tm=tl, name="in_proj")
    qat, ka, va, qbt, kb, vb = _prep(proj, cos, sin, qg, kg, tm=tl, name="prep")
    oat, lse_a = _attn_win_fwd(qat, ka, _ones_beside(va.T), kac, _ones_beside(vac.T), sink, tq=tm)
    obt, lse_b, mrun_b, pbt = _attn_glob_fwd(qbt, kb, _ones_beside(vb.T), kbc, _ones_beside(vbc.T), tq=tm,
                                             tk=min(1024, s))
    wba_p, wbb_p = _pad_heads_w(wba), _pad_heads_w(wbb)
    x1, y, mu1, r1, pa, pb, merged = _merge_fwd(oat, obt, proj, x, gate1, wba_p, wbb_p, w_out, ln1_g, ln1_b, tm=tm)
    u = _mm_nn4(x1, shift2, scale2, wup4, jnp.zeros((1, 2 * D_FF), F32), mode="modulate", split_out=True,
                out_dtype=BF16, tm=tl, name="ffn_up")
    cw2 = cw.reshape(3, 2, D_FF).transpose(1, 0, 2)
    cb2 = cb.reshape(2, 1, D_FF)
    a = _ffn_act_fwd(u, cw2, cb2, tm=tx)
    ls, dy2, dx1a, dln2_g, dln2_b, dgate2 = _ffn_down_loss(a, w_down, x1, target, gate2, ln2_g, ln2_b, tm=tm)
    loss = jnp.sum(ls[:, 0, 0])

    n_s = s // tl
    dw_down = _mm_tn(a, dy2, a_spec=pl.BlockSpec((tl, D_FF), lambda t: (t, 0)),
                     b_spec=pl.BlockSpec((tl, D_MODEL), lambda t: (t, 0)), grid=(n_s,),
                     out_shape=_sds((D_FF, D_MODEL)), out_spec=_full((D_FF, D_MODEL)), name="dw_down")
    dc, dcw2, dcb2 = _ffn_act_bwd(dy2, w_down, u, cw2, cb2, tm=tx)
    du = _conv_bwd_input(dc, cw2, tm=tx)
    dxz1, dy, dscale2, dshift2, dln1_g, dln1_b, dgate1 = _ffn_up_bwd(
        du, wup4, dx1a, x1, scale2, x, y, mu1, r1, gate1, ln1_g, tm=tm)
    ns_up = wup4.shape[-1]
    dw_up4 = _mm_tn(x1, du, a_spec=pl.BlockSpec((tl, D_MODEL), lambda k, t: (t, 0)),
                    b_spec=pl.BlockSpec((None, tl, ns_up), lambda k, t: (k // 2, t, k % 2)), grid=(N_CHIPS, n_s),
                    out_shape=_sds((N_CHIPS, D_MODEL, ns_up)),
                    out_spec=pl.BlockSpec((None, D_MODEL, ns_up), lambda k, t: (k, 0, 0)),
                    mod=(shift2, scale2), name="dw_up")

    dgl, doat, dobt, dwba_p, dwbb_p = _merge_bwd(dy, oat, obt, pa, pb, proj, wba_p, wbb_p, w_out, tm=tm2)
    dwba, dwbb = _unpad_heads_w(dwba_p), _unpad_heads_w(dwbb_p)
    rowspec = pl.BlockSpec((tl, D_MODEL), lambda t: (t, 0))
    dw_out = _mm_tn(merged, dy, a_spec=rowspec, b_spec=rowspec, grid=(n_s,), out_shape=_sds((D_MODEL, D_MODEL)),
                    out_spec=_full((D_MODEL, D_MODEL)), name="dw_out")

    dqat, dkat, dvat, dkact, dvact, dsk = _attn_win_bwd(qat, doat, oat, lse_a, ka, ka.T, va, kac, kac.T, vac, sink, tq=tm)
    dka, dva, dkac, dvac = dkat.T, dvat.T, dkact.T, dvact.T
    dqbt, dkbt, dvbt, dkbct, dvbct = _attn_glob_bwd(qbt, dobt, obt, lse_b, mrun_b, pbt, kb.T, vb, kbc.T, vbc, tq=tm, tk=tm)
    dkb, dvb, dkbc, dvbc = dkbt.T, dvbt.T, dkbct.T, dvbct.T
    dsink = jnp.sum(dsk[:, :, 0, 0], axis=1)

    dproj, dqg, dkg = _prep_bwd(dqat, dka, dva, dqbt, dkb, dvb, proj, cos, sin, qg, kg, dgl, tm=tm, name="prep_bwd")
    grad_x, dscale1, dshift1 = _mm_nt4_mod_bwd(dproj, win4, dxz1, x, scale1, tm=tl, name="in_proj_bwd")
    ns_in = win4.shape[-1]
    win_spec = dict(b_spec=pl.BlockSpec((None, None, ns_in), lambda k, t: (0, 0, k)),
                    out_shape=_sds((N_CHIPS, D_MODEL, ns_in)),
                    out_spec=pl.BlockSpec((None, D_MODEL, ns_in), lambda k, t: (k, 0, 0)),
                    colsum_spec=pl.BlockSpec((8, ns_in), lambda k, t: (0, k)), colsum_shape=_sds((8, IN_COLS)))
    win_spec["b_spec"] = pl.BlockSpec((tl, ns_in), lambda k, t: (t, k))
    dw_in4, db_in = _mm_tn(x, dproj, a_spec=pl.BlockSpec((tl, D_MODEL), lambda k, t: (t, 0)), grid=(N_CHIPS, n_s),
                           mod=(shift1, scale1), name="dw_in", **win_spec)

    zq = jnp.zeros((N_HEADS, LANES, nc), F32)
    dproj_c, _, dkg_c = _prep_bwd(zq, dkac, dvac, zq, dkbc, dvbc, proj_c, cos_c, sin_c, qg, kg,
                                  jnp.zeros((nc, IN_COLS - OFF_GA), BF16), tm=nc, name="prep_bwd_ctx")
    _, dscale_c, dshift_c = _mm_nt4_mod_bwd(dproj_c, win4, jnp.zeros((nc, D_MODEL), F32), ctx, scale_c, tm=nc,
                                            name="in_proj_bwd_ctx")
    win_spec["b_spec"] = pl.BlockSpec((nc, ns_in), lambda k, t: (t, k))
    dw_in4, db_in_c = _mm_tn(ctx, dproj_c, a_spec=pl.BlockSpec((nc, D_MODEL), lambda k, t: (t, 0)), grid=(N_CHIPS, 1),
                             mod=(shift_c, scale_c), init=dw_in4, name="dw_in_ctx", **win_spec)

    dmod = jnp.concatenate([dshift1, dscale1, dgate1, dshift2, dscale2, dgate2], axis=1)
    dmodc = jnp.concatenate([dshift_c, dscale_c], axis=1)
    dmodc_pad = jnp.concatenate([dmodc, jnp.zeros((1, 4 * D_MODEL), F32)], axis=1)
    dmodc8 = _first_row(dmodc_pad).astype(BF16)
    z8 = jnp.zeros((8, D_MODEL), F32)
    dsilu_c, _, _ = _mm_nt4_mod_bwd(dmodc8, wmod4, z8, z8, zvec, tm=8, name="c_ctx_bwd")
    sg = _sigmoid(c_ctx)
    dc_ctx = dsilu_c[0:1] * sg * (1.0 + c_ctx * (1.0 - sg))

    dqn = jnp.sum(dqg.reshape(N_HEADS, HEAD_DIM), axis=0, keepdims=True)
    dkn = jnp.sum((dkg + dkg_c).reshape(N_KV, HEAD_DIM), axis=0, keepdims=True)
    grads = dict(
        w_in4=dw_in4, b_in=db_in[0:1] + db_in_c[0:1], sink=dsink, qn=dqn, kn=dkn, wba=dwba, wbb=dwbb, w_out=dw_out,
        ln1_g=dln1_g, ln1_b=dln1_b, w_up4=dw_up4, conv_w=dcw2.transpose(1, 0, 2).reshape(3, 2 * D_FF),
        conv_b=dcb2.reshape(1, 2 * D_FF), w_down=dw_down, ln2_g=dln2_g, ln2_b=dln2_b,
        c_ctx=dc_ctx, dmod=dmod, dmodc=dmodc)
    return loss, grad_x, grads


ANY = pl.BlockSpec(memory_space=pl.ANY)


def _mesh_pos():
    return lax.axis_index("x"), lax.axis_index("y"), lax.axis_index("c")


def _other_chips(x, y):
    return [(1 - x, y), (x, 1 - y), (1 - x, 1 - y)]


def _remote(src, dst, send, recv, dev):
    return pltpu.make_async_remote_copy(src_ref=src, dst_ref=dst, send_sem=send, recv_sem=recv, device_id=dev,
                                        device_id_type=MESH)


def _set_block(stack, block, k):
    return lax.dynamic_update_slice(stack, block[None], (k,) + (0,) * block.ndim)


def _gather_shards(arrs, small):
    na = len(arrs)
    halves = [a.shape[0] // 2 for a in arrs]

    def body(*refs):
        ins, small_ref = refs[:na], refs[na]
        outs, small_out = refs[na + 1:2 * na + 1], refs[2 * na + 1]
        send, recv = refs[2 * na + 2:]
        x, y, c = _mesh_pos()
        me = 2 * x + y
        chips = _other_chips(x, y)

        def half(a, cc):
            return pl.ds(cc * halves[a], halves[a])

        sends = []
        for j, chip in enumerate(chips):
            for a in range(na):
                sends.append(_remote(ins[a].at[half(a, c)], outs[a].at[me, half(a, c)], send.at[a, j], recv.at[a, j],
                                     (*chip, c)))
            sends.append(_remote(small_ref, small_out.at[me], send.at[na, j], recv.at[na, j], (*chip, c)))
        for cp in sends:
            cp.start()
        for j, chip in enumerate(chips):
            kj = 2 * chip[0] + chip[1]
            for a in range(na):
                landed = outs[a].at[kj, half(a, c)]
                _remote(landed, landed, send.at[a, j], recv.at[a, j], (*chip, c)).wait_recv()
                fwd = _remote(landed, landed, send.at[a, 3 + j], recv.at[a, 3 + j], (x, y, 1 - c))
                fwd.start()
                sends.append(fwd)
            _remote(small_ref, small_out.at[kj], send.at[na, j], recv.at[na, j], (*chip, c)).wait_recv()
        for j, chip in enumerate(chips):
            kj = 2 * chip[0] + chip[1]
            for a in range(na):
                other = outs[a].at[kj, half(a, 1 - c)]
                _remote(other, other, send.at[a, 3 + j], recv.at[a, 3 + j], (x, y, 1 - c)).wait_recv()
        for cp in sends:
            cp.wait_send()

    out_shape = [_sds((N_CHIPS,) + a.shape, a.dtype) for a in arrs] + [_sds((N_CHIPS,) + small.shape, small.dtype)]
    got = pl.pallas_call(
        body, name="gather_shards", in_specs=[ANY] * (na + 1), out_specs=[ANY] * (na + 1), out_shape=out_shape,
        scratch_shapes=[pltpu.SemaphoreType.DMA((na + 1, 6)), pltpu.SemaphoreType.DMA((na + 1, 6))],
    )(*arrs, small)
    xp, yp, _ = _mesh_pos()
    return [_set_block(g, a, 2 * xp + yp) for g, a in zip(got, list(arrs) + [small])]


def _allgather_rows(v):
    r, n = v.shape

    def body(v_ref, out_ref, send, recv, loc):
        x, y, c = _mesh_pos()
        me, sibling = (x, y, c), (x, y, 1 - c)
        chips = _other_chips(x, y)

        def rows(px, py, pc):
            return out_ref.at[4 * px + 2 * py + pc]

        def copy(k, block, to, src=None):
            return _remote(rows(*block) if src is None else src, rows(*block), send.at[k], recv.at[k], to)

        mine = pltpu.make_async_copy(v_ref, rows(*me), loc)
        mine.start()
        first = [copy(0, me, sibling, src=v_ref)] + [copy(1 + j, me, (*chip, c), src=v_ref) for j, chip in enumerate(chips)]
        for cp in first:
            cp.start()
        passed = [copy(4 + j, (*chip, c), sibling) for j, chip in enumerate(chips)]
        for j, chip in enumerate(chips):
            copy(1 + j, (*chip, c), me).wait_recv()
            passed[j].start()
        copy(0, sibling, me).wait_recv()
        for j, chip in enumerate(chips):
            copy(4 + j, (*chip, 1 - c), me).wait_recv()
        for cp in first + passed:
            cp.wait_send()
        mine.wait()

    return pl.pallas_call(
        body, name="allgather_rows", in_specs=[pl.BlockSpec(memory_space=pltpu.VMEM)],
        out_specs=pl.BlockSpec(memory_space=pltpu.VMEM), out_shape=_sds((N_DEV, r, n), v.dtype),
        scratch_shapes=[pltpu.SemaphoreType.DMA((7,)), pltpu.SemaphoreType.DMA((7,)), pltpu.SemaphoreType.DMA],
    )(v)


def _swap_other_half(g):
    nb, r, n = g.shape
    rh = r // 2

    def body(g_ref, out_ref, send, recv):
        x, y, c = _mesh_pos()
        cp = _remote(g_ref.at[:, pl.ds((1 - c) * rh, rh), :], out_ref, send, recv, (x, y, 1 - c))
        cp.start()
        cp.wait()

    return pl.pallas_call(
        body, name="swap_other_half", in_specs=[ANY], out_specs=ANY, out_shape=_sds((nb, rh, n), g.dtype),
        scratch_shapes=[pltpu.SemaphoreType.DMA, pltpu.SemaphoreType.DMA],
    )(g)


def _scatter_to_chips(p):
    def body(p_ref, out_ref, send, recv):
        x, y, c = _mesh_pos()
        me = 2 * x + y
        chips = _other_chips(x, y)
        sends = [_remote(p_ref.at[2 * chip[0] + chip[1]], out_ref.at[me], send.at[j], recv.at[j], (*chip, c))
                 for j, chip in enumerate(chips)]
        for cp in sends:
            cp.start()
        for j, chip in enumerate(chips):
            kj = 2 * chip[0] + chip[1]
            _remote(p_ref.at[kj], out_ref.at[kj], send.at[j], recv.at[j], (*chip, c)).wait_recv()
        for cp in sends:
            cp.wait_send()

    got = pl.pallas_call(
        body, name="scatter_to_chips", in_specs=[ANY], out_specs=ANY, out_shape=_sds(p.shape, p.dtype),
        scratch_shapes=[pltpu.SemaphoreType.DMA((3,)), pltpu.SemaphoreType.DMA((3,))],
    )(p)
    xp, yp, _ = _mesh_pos()
    me = 2 * xp + yp
    return _set_block(got, lax.dynamic_index_in_dim(p, me, axis=0, keepdims=False), me)


def _join_halves(f):
    def body(f_ref, out_ref, send, recv):
        x, y, c = _mesh_pos()
        cp = _remote(f_ref, out_ref, send, recv, (x, y, 1 - c))
        cp.start()
        cp.wait()

    other = pl.pallas_call(
        body, name="join_halves", in_specs=[ANY], out_specs=ANY, out_shape=_sds(f.shape, f.dtype),
        scratch_shapes=[pltpu.SemaphoreType.DMA, pltpu.SemaphoreType.DMA],
    )(f)
    first = lax.axis_index("c") == 0
    return jnp.concatenate([jnp.where(first, f, other), jnp.where(first, other, f)], axis=0)


def _row_tile(rows, cap=512):
    t = cap - cap % 8
    while rows % t:
        t -= 8
    return t


def _add_blocks(a, b, out_dtype):
    nb, r, n = a.shape
    tr = _row_tile(r)

    def body(a_ref, b_ref, o_ref):
        o_ref[...] = (a_ref[...] + b_ref[...]).astype(out_dtype)

    spec = pl.BlockSpec((None, tr, n), lambda k, i: (k, i, 0))
    return _call(body, name="add_blocks", grid=(nb, r // tr), in_specs=[spec, spec], out_specs=spec,
                 out_shape=_sds(a.shape, out_dtype), sem=("parallel", "parallel"))(a, b)


def _sum_leading(a, *, name):
    nk, r, n = a.shape
    tr = _row_tile(r)

    def body(a_ref, o_ref):
        acc = a_ref[0].astype(F32)
        for k in range(1, nk):
            acc = acc + a_ref[k].astype(F32)
        o_ref[...] = acc

    return _call(body, name=name, grid=(r // tr,), in_specs=[pl.BlockSpec((nk, tr, n), lambda i: (0, i, 0))],
                 out_specs=pl.BlockSpec((tr, n), lambda i: (i, 0)), out_shape=_sds((r, n)), sem=("parallel",))(a)


def _silu_outer(a, b):
    kdim, n = a.shape[1], b.shape[1]

    def body(a_ref, b_ref, o_ref):
        av = a_ref[...]
        av = av * _sigmoid(av)
        bv = b_ref[...]
        ah, bh = av.astype(BF16), bv.astype(BF16)
        al, bl = (av - ah.astype(F32)).astype(BF16), (bv - bh.astype(F32)).astype(BF16)
        o_ref[...] = _dot_tn(ah, bh) + (_dot_tn(ah, bl) + _dot_tn(al, bh))

    return _call(body, name="dw_mod", grid=(1,), in_specs=[_full(a.shape), _full(b.shape)], out_specs=_full((kdim, n)),
                 out_shape=_sds((kdim, n)))(a, b)


def _adamw(w, g, m, v):
    r, n = w.shape
    tr = _row_tile(r)

    def body(w_ref, g_ref, m_ref, v_ref, d_ref, nm_ref, nv_ref):
        gv = g_ref[...]
        nm = ADAM_B1 * m_ref[...] + (1.0 - ADAM_B1) * gv
        nv = ADAM_B2 * v_ref[...] + (1.0 - ADAM_B2) * (gv * gv)
        m_hat = nm / (1.0 - ADAM_B1 ** ADAM_STEP)
        v_hat = nv / (1.0 - ADAM_B2 ** ADAM_STEP)
        d_ref[...] = -ADAM_LR * (m_hat / (jnp.sqrt(v_hat) + ADAM_EPS) + ADAM_WD * w_ref[...])
        nm_ref[...] = nm
        nv_ref[...] = nv

    spec = pl.BlockSpec((tr, n), lambda i: (i, 0))
    return _call(body, name="adamw", grid=(r // tr,), in_specs=[spec] * 4, out_specs=[spec] * 3,
                 out_shape=[_sds((r, n))] * 3, sem=("parallel",))(w, g, m, v)


BIG = ("w_in", "w_branch_a", "w_branch_b", "w_out", "w_up", "w_down", "conv_w")
BIG_ROWS = 3584
MATRICES = ("w_mod", "w_in", "w_branch_a", "w_branch_b", "w_out", "w_up", "w_down")
SMALL = ("b_mod", "b_in", "conv_b", "ln1_g", "ln1_b", "ln2_g", "ln2_b", "c_ctx", "attn_sink", "q_norm_g", "k_norm_g", "conv_w")
SMALL_ROWS = 8 * len(SMALL)


def _rows(a, n_rows):
    flat = a.reshape(-1)
    return jnp.pad(flat, (0, n_rows * D_MODEL - flat.shape[0])).reshape(n_rows, D_MODEL)


def _group8(a):
    return _rep8(_rows(a, 1)) if a.size <= D_MODEL else _rows(a, 8)


def _ungroup8(p, shape):
    size = math.prod(shape)
    return (p[0, :size] if size <= D_MODEL else p.reshape(-1)[:size]).reshape(shape)


def _unpack_big(p, like):
    out, r = {}, 0
    for n in BIG:
        size = math.prod(like[n].shape)
        nr = size // D_MODEL if n != "conv_w" else 8
        out[n] = p[r:r + nr].reshape(-1)[:size].reshape(like[n].shape)
        r += nr
    return out


def _pack_small(t):
    return jnp.concatenate([_group8(t[n]) for n in SMALL], axis=0)


def _unpack_small(p, like):
    return {n: _ungroup8(p[8 * i:8 * i + 8], like[n].shape) for i, n in enumerate(SMALL)}


WEIGHTS = ("c_ctx", "w_mod", "b_mod", "w_in", "b_in", "attn_sink", "q_norm_g", "k_norm_g", "w_branch_a", "w_branch_b",
           "w_out", "ln1_g", "ln1_b", "w_up", "conv_w", "conv_b", "w_down", "ln2_g", "ln2_b")


def kernel(x, c, ctx, c_ctx, w_mod, b_mod, w_in, b_in, attn_sink, q_norm_g, k_norm_g, w_branch_a, w_branch_b, w_out, ln1_g, ln1_b, w_up, conv_w, conv_b, w_down, ln2_g, ln2_b, loss_target, m_c_ctx, m_w_mod, m_b_mod, m_w_in, m_b_in, m_attn_sink, m_q_norm_g, m_k_norm_g, m_w_branch_a, m_w_branch_b, m_w_out, m_ln1_g, m_ln1_b, m_w_up, m_conv_w, m_conv_b, m_w_down, m_ln2_g, m_ln2_b, v_c_ctx, v_w_mod, v_b_mod, v_w_in, v_b_in, v_attn_sink, v_q_norm_g, v_k_norm_g, v_w_branch_a, v_w_branch_b, v_w_out, v_ln1_g, v_ln1_b, v_w_up, v_conv_w, v_conv_b, v_w_down, v_ln2_g, v_ln2_b):
    w = dict(c_ctx=c_ctx, w_mod=w_mod, b_mod=b_mod, w_in=w_in, b_in=b_in, attn_sink=attn_sink, q_norm_g=q_norm_g,
             k_norm_g=k_norm_g, w_branch_a=w_branch_a, w_branch_b=w_branch_b, w_out=w_out, ln1_g=ln1_g, ln1_b=ln1_b,
             w_up=w_up, conv_w=conv_w, conv_b=conv_b, w_down=w_down, ln2_g=ln2_g, ln2_b=ln2_b)
    m = dict(c_ctx=m_c_ctx, w_mod=m_w_mod, b_mod=m_b_mod, w_in=m_w_in, b_in=m_b_in, attn_sink=m_attn_sink,
             q_norm_g=m_q_norm_g, k_norm_g=m_k_norm_g, w_branch_a=m_w_branch_a, w_branch_b=m_w_branch_b, w_out=m_w_out,
             ln1_g=m_ln1_g, ln1_b=m_ln1_b, w_up=m_w_up, conv_w=m_conv_w, conv_b=m_conv_b, w_down=m_w_down,
             ln2_g=m_ln2_g, ln2_b=m_ln2_b)
    v = dict(c_ctx=v_c_ctx, w_mod=v_w_mod, b_mod=v_b_mod, w_in=v_w_in, b_in=v_b_in, attn_sink=v_attn_sink,
             q_norm_g=v_q_norm_g, k_norm_g=v_k_norm_g, w_branch_a=v_w_branch_a, w_branch_b=v_w_branch_b, w_out=v_w_out,
             ln1_g=v_ln1_g, ln1_b=v_ln1_b, w_up=v_w_up, conv_w=v_conv_w, conv_b=v_conv_b, w_down=v_w_down,
             ln2_g=v_ln2_g, ln2_b=v_ln2_b)
    xp, yp, _ = _mesh_pos()
    me = 2 * xp + yp

    branches = jnp.concatenate([w_branch_a[0], w_branch_b[0]], axis=0)
    wide = jnp.concatenate([w_mod[0], w_in[0], w_up[0], branches], axis=1).astype(BF16)
    tall = jnp.concatenate([w_out[0], w_down[0]], axis=0).astype(BF16)
    wide4, tall4, cw4 = _gather_shards([wide, tall], conv_w[0])
    n_mod, n_in, n_up = w_mod.shape[-1], w_in.shape[-1], w_up.shape[-1]
    wmod4 = wide4[:, :, :n_mod]
    win4 = wide4[:, :, n_mod:n_mod + n_in]
    wup4 = wide4[:, :, n_mod + n_in:n_mod + n_in + n_up]
    br4 = wide4[:, :, n_mod + n_in + n_up:]
    n_br = w_branch_a.shape[1]
    wba = br4[:, :n_br].transpose(1, 0, 2).reshape(n_br, D_MODEL)
    wbb = br4[:, n_br:].transpose(1, 0, 2).reshape(n_br, D_MODEL)
    n_out = w_out.shape[1]
    w_out_full = tall4[:, :n_out].reshape(D_MODEL, D_MODEL)
    w_down_full = tall4[:, n_out:].reshape(D_FF, D_MODEL)
    cw_full = cw4.transpose(1, 0, 2).reshape(3, 2 * D_FF)

    loss, grad_x, g = _local_step(
        x[0], c, ctx[0], c_ctx[None], wmod4, b_mod, win4, b_in, attn_sink[0], q_norm_g, k_norm_g, wba, wbb, w_out_full,
        ln1_g, ln1_b, wup4, cw_full, conv_b, w_down_full, ln2_g, ln2_b, loss_target[0])
    loss = lax.psum(loss, ("x", "y", "c"))

    sent = dict(c=c, dmod=g["dmod"], dmodc=g["dmodc"], b_in=g["b_in"], conv_b=g["conv_b"], ln1_g=g["ln1_g"],
                ln1_b=g["ln1_b"], ln2_g=g["ln2_g"], ln2_b=g["ln2_b"], c_ctx=g["c_ctx"], attn_sink=g["sink"],
                q_norm_g=g["qn"], k_norm_g=g["kn"])
    every = _allgather_rows(jnp.concatenate([_group8(a) for a in sent.values()], axis=0))
    total = _sum_leading(every, name="sum_devices")
    slot = {n: slice(8 * i, 8 * i + 8) for i, n in enumerate(sent)}
    gs = {n: _ungroup8(total[slot[n]], sent[n].shape) for n in SMALL if n in sent}
    dmodc_sum = jnp.concatenate([_ungroup8(total[slot["dmodc"]], (1, 2 * D_MODEL)), jnp.zeros((1, 4 * D_MODEL), F32)],
                                axis=1)
    gs["b_mod"] = _ungroup8(total[slot["dmod"]], b_mod.shape) + dmodc_sum
    acts = jnp.concatenate([every[:, slot["c"].start], _rep8(c_ctx)], axis=0)
    dmods = jnp.concatenate([every[:, slot["dmod"]].reshape(N_DEV, -1)[:, :6 * D_MODEL], _first_row(dmodc_sum)], axis=0)
    g_w_mod = _silu_outer(acts, lax.dynamic_slice_in_dim(dmods, me * n_mod, n_mod, axis=1))

    cw_g4 = _to_blocks4(g["conv_w"])
    parts = [
        g["w_in4"].reshape(N_CHIPS, -1, D_MODEL), _to_blocks4(g["wba"]).reshape(N_CHIPS, -1, D_MODEL),
        _to_blocks4(g["wbb"]).reshape(N_CHIPS, -1, D_MODEL), g["w_out"].reshape(N_CHIPS, -1, D_MODEL),
        g["w_up4"].reshape(N_CHIPS, -1, D_MODEL), g["w_down"].reshape(N_CHIPS, -1, D_MODEL),
        jnp.pad(cw_g4.reshape(N_CHIPS, -1), ((0, 0), (0, 8 * D_MODEL - cw_g4.shape[1] * cw_g4.shape[2]))).reshape(
            N_CHIPS, 8, D_MODEL)]
    used = sum(p.shape[1] for p in parts)
    packed = jnp.concatenate(parts + [jnp.zeros((N_CHIPS, BIG_ROWS - used, D_MODEL), F32)], axis=1)
    rh = BIG_ROWS // 2
    cpos = lax.axis_index("c")
    my_half = lax.dynamic_slice_in_dim(packed, cpos * rh, rh, axis=1)
    chip_sum = _add_blocks(my_half, _swap_other_half(packed), BF16)
    half_sum = _sum_leading(_scatter_to_chips(chip_sum), name="sum_chips")
    g_big = _unpack_big(_join_halves(half_sum), w)

    grads = dict(gs, w_mod=g_w_mod, **g_big)
    grads = {n: grads[n].reshape(w[n].shape) for n in WEIGHTS}
    delta, new_m, new_v = {}, {}, {}
    for n in MATRICES:
        outs = _adamw(*[t[n][0] for t in (w, grads, m, v)])
        delta[n], new_m[n], new_v[n] = [o[None] for o in outs]
    outs = _adamw(*[_pack_small(t) for t in (w, grads, m, v)])
    for res, o in zip((delta, new_m, new_v), outs):
        res.update(_unpack_small(o, w))
    return (loss, grad_x[None], *[grads[n] for n in WEIGHTS], *[delta[n] for n in WEIGHTS],
            *[new_m[n] for n in WEIGHTS], *[new_v[n] for n in WEIGHTS])
```

```python
import functools
import math

import jax
import jax.numpy as jnp
from jax import lax
from jax.experimental import pallas as pl
from jax.experimental.pallas import tpu as pltpu

F32 = jnp.float32
BF16 = jnp.bfloat16

D_MODEL = 1024
HEAD_DIM = 64
N_HEADS = 8
N_KV = 2
WINDOW = 128
GRID_W = 64
ROPE_THETA = 10000.0
D_FF = 2816
LN_EPS = 1e-5
QK_EPS = 1e-6
ALPHA = 2.0 ** 0.25
Q_SCALE = HEAD_DIM ** -0.5
OFF_GA = 1536
IN_COLS = 3584
ADAM_LR, ADAM_B1, ADAM_B2, ADAM_EPS, ADAM_WD, ADAM_STEP = 0.001, 0.9, 0.999, 1e-8, 0.01, 10

LANES = 128
VMEM_BUDGET = 52 * 1024 * 1024
N_CHIPS = 4
N_DEV = 8
NEG = -1e30
MESH = pl.DeviceIdType.MESH


def _sigmoid(x):
    return 1.0 / (1.0 + jnp.exp(-x))


def _dot(a, b):
    return jnp.dot(a, b, preferred_element_type=F32)


def _dot_nt(a, b):
    return lax.dot_general(a, b, (((1,), (1,)), ((), ())), preferred_element_type=F32)


def _dot_tn(a, b):
    return lax.dot_general(a, b, (((0,), (0,)), ((), ())), preferred_element_type=F32)


def _call(body, *, name, grid, in_specs, out_specs, out_shape, scratch=(), sem=None, **kw):
    params = dict(vmem_limit_bytes=VMEM_BUDGET)
    if sem is not None:
        params["dimension_semantics"] = sem
    return pl.pallas_call(body, name=name, grid=grid, in_specs=in_specs, out_specs=out_specs,
                          out_shape=out_shape, scratch_shapes=list(scratch),
                          compiler_params=pltpu.CompilerParams(**params), **kw)


def _full(shape):
    n = len(shape)
    return pl.BlockSpec(shape, lambda *_: (0,) * n)


def _sds(shape, dtype=F32):
    return jax.ShapeDtypeStruct(shape, dtype)


def _mm_nn4(a, shift, scale, w4, bias, *, mode, split_out, out_dtype, tm, name):
    m, kdim = a.shape
    nb, _, ns = w4.shape

    def body(a_ref, sh_ref, sc_ref, w_ref, b_ref, o_ref):
        av = a_ref[...]
        if mode == "modulate":
            av = av * (1.0 + sc_ref[...]) + sh_ref[...]
        else:
            av = av * _sigmoid(av)
        o_ref[...] = (_dot(av.astype(BF16), w_ref[...]) + b_ref[...]).astype(out_dtype)

    if split_out:
        out_shape = _sds((2, m, 2 * ns), out_dtype)
        out_spec = pl.BlockSpec((None, tm, ns), lambda i, k: (k // 2, i, k % 2))
    else:
        out_shape = _sds((m, nb * ns), out_dtype)
        out_spec = pl.BlockSpec((tm, ns), lambda i, k: (i, k))
    return _call(
        body, name=name, grid=(m // tm, nb),
        in_specs=[pl.BlockSpec((tm, kdim), lambda i, k: (i, 0)),
                  pl.BlockSpec((1, kdim), lambda i, k: (0, 0)),
                  pl.BlockSpec((1, kdim), lambda i, k: (0, 0)),
                  pl.BlockSpec((None, kdim, ns), lambda i, k: (k, 0, 0)),
                  pl.BlockSpec((1, ns), lambda i, k: (0, k))],
        out_specs=out_spec, out_shape=out_shape, sem=("parallel", "arbitrary"),
    )(a, shift, scale, w4, bias)


def _mm_tn(a, b, *, a_spec, b_spec, grid, out_shape, out_spec, name, mod=None, init=None, colsum_spec=None,
           colsum_shape=None):
    red = len(grid) - 1
    has_mod, has_init, has_cs = mod is not None, init is not None, colsum_spec is not None

    def body(*refs):
        refs = list(refs)
        a_ref, b_ref = refs[0], refs[1]
        pos = 2
        if has_mod:
            sh_ref, sc_ref = refs[2], refs[3]
            pos = 4
        if has_init:
            init_ref = refs[pos]
            pos += 1
        o_ref = refs[pos]
        cs_ref = refs[pos + 1] if has_cs else None
        s = pl.program_id(red)

        @pl.when(s == 0)
        def _():
            o_ref[...] = init_ref[...] if has_init else jnp.zeros(o_ref.shape, F32)
            if has_cs:
                cs_ref[...] = jnp.zeros(cs_ref.shape, F32)

        av = a_ref[...]
        if has_mod:
            av = av * (1.0 + sc_ref[...]) + sh_ref[...]
        bv = b_ref[...]
        o_ref[...] += _dot_tn(av.astype(BF16), bv)
        if has_cs:
            cs_ref[...] += jnp.broadcast_to(jnp.sum(bv.astype(F32), axis=0, keepdims=True), cs_ref.shape)

    ins, in_specs = [a, b], [a_spec, b_spec]
    if has_mod:
        kdim = mod[0].shape[-1]
        ins += list(mod)
        in_specs += [_full((1, kdim)), _full((1, kdim))]
    if has_init:
        ins.append(init)
        in_specs.append(out_spec)
    out_specs, out_shapes = out_spec, out_shape
    if has_cs:
        out_specs, out_shapes = [out_spec, colsum_spec], [out_shape, colsum_shape]
    sem = ("parallel",) * red + ("arbitrary",)
    return _call(body, name=name, grid=grid, in_specs=in_specs, out_specs=out_specs, out_shape=out_shapes,
                 sem=sem)(*ins)


def _rope_tables(n_tok):
    pos = jnp.arange(n_tok, dtype=jnp.int32)
    rows = (pos // GRID_W).astype(F32)
    cols = (pos % GRID_W).astype(F32)
    n_freq = HEAD_DIM // 4
    inv_freq = ROPE_THETA ** (-jnp.arange(n_freq, dtype=F32) / n_freq)
    ang_r = rows[:, None] * inv_freq
    ang_c = cols[:, None] * inv_freq
    cos = jnp.concatenate([jnp.cos(ang_r)] * 2 + [jnp.cos(ang_c)] * 2, axis=-1)
    sin = jnp.concatenate([-jnp.sin(ang_r), jnp.sin(ang_r), -jnp.sin(ang_c), jnp.sin(ang_c)], axis=-1)
    return jnp.tile(cos, (1, 2)), jnp.tile(sin, (1, 2))


def _lane(shape):
    return lax.broadcasted_iota(jnp.int32, shape, 1)


def _rope_partner(t, lane):
    return jnp.where((lane % 32) < 16, pltpu.roll(t, LANES - 16, 1), pltpu.roll(t, 16, 1))


def _half_mean(s, lane):
    lo = jnp.sum(jnp.where(lane < HEAD_DIM, s, 0.0), axis=-1, keepdims=True)
    hi = jnp.sum(jnp.where(lane < HEAD_DIM, 0.0, s), axis=-1, keepdims=True)
    return jnp.where(lane < HEAD_DIM, lo, hi) * (1.0 / HEAD_DIM)


def _prep(proj, cos, sin, qg, kg, *, tm, name):
    m = proj.shape[0]

    def body(p_ref, cos_ref, sin_ref, qg_ref, kg_ref, qa_ref, ka_ref, va_ref, qb_ref, kb_ref, vb_ref):
        lane = _lane((tm, LANES))
        cosv, sinv = cos_ref[...], sin_ref[...]
        low = lane < HEAD_DIM

        def rope(t):
            return t * cosv + _rope_partner(t, lane) * sinv

        def rms(t, g):
            return t * lax.rsqrt(_half_mean(t * t, lane) + QK_EPS) * g

        def place(q_ref, j, chunk):
            sw = pltpu.roll(chunk, HEAD_DIM, 1)
            if j < 2:
                h0, h1 = jnp.where(low, chunk, 0.0), jnp.where(low, sw, 0.0)
            else:
                h0, h1 = jnp.where(low, 0.0, sw), jnp.where(low, 0.0, chunk)
            q_ref[2 * j] = h0.T.astype(BF16)
            q_ref[2 * j + 1] = h1.T.astype(BF16)

        for j in range(4):
            place(qa_ref, j, rope(p_ref[:, j * LANES:(j + 1) * LANES]) * Q_SCALE)
            place(qb_ref, j, rope(rms(p_ref[:, 768 + j * LANES:768 + (j + 1) * LANES], qg_ref[...])) * Q_SCALE)
        ka_ref[...] = rope(p_ref[:, 512:640]).astype(BF16)
        va_ref[...] = p_ref[:, 640:768].astype(BF16)
        kb_ref[...] = rope(rms(p_ref[:, 1280:1408], kg_ref[...])).astype(BF16)
        vb_ref[...] = p_ref[:, 1408:1536].astype(BF16)

    row = pl.BlockSpec((tm, LANES), lambda i: (i, 0))
    qspec = pl.BlockSpec((N_HEADS, LANES, tm), lambda i: (0, 0, i))
    return _call(
        body, name=name, grid=(m // tm,),
        in_specs=[pl.BlockSpec((tm, OFF_GA), lambda i: (i, 0)), row, row, _full((1, LANES)), _full((1, LANES))],
        out_specs=[qspec, row, row, qspec, row, row],
        out_shape=[_sds((N_HEADS, LANES, m), BF16), _sds((m, LANES), BF16), _sds((m, LANES), BF16),
                   _sds((N_HEADS, LANES, m), BF16), _sds((m, LANES), BF16), _sds((m, LANES), BF16)],
        sem=("parallel",),
    )(proj, cos, sin, qg, kg)


def _prep_bwd(dqa, dka, dva, dqb, dkb, dvb, proj, cos, sin, qg, kg, dgl, *, tm, name):
    m = proj.shape[0]

    def body(dqa_ref, dka_ref, dva_ref, dqb_ref, dkb_ref, dvb_ref, p_ref, cos_ref, sin_ref, qg_ref, kg_ref,
             dgl_ref, dp_ref, dqg_ref, dkg_ref):
        i = pl.program_id(0)
        lane = _lane((tm, LANES))
        cosv, sinv = cos_ref[...], sin_ref[...]
        low = lane < HEAD_DIM

        @pl.when(i == 0)
        def _():
            dqg_ref[...] = jnp.zeros(dqg_ref.shape, F32)
            dkg_ref[...] = jnp.zeros(dkg_ref.shape, F32)

        def unrope(d):
            return d * cosv - _rope_partner(d, lane) * sinv

        def unplace(dq_ref, j):
            d0, d1 = dq_ref[2 * j].T, dq_ref[2 * j + 1].T
            if j < 2:
                return jnp.where(low, d0, pltpu.roll(d1, HEAD_DIM, 1))
            return jnp.where(low, pltpu.roll(d0, HEAD_DIM, 1), d1)

        def unrms(dtn, t, g):
            r = lax.rsqrt(_half_mean(t * t, lane) + QK_EPS)
            u = dtn * g
            dt = r * u - t * (r * r * r) * _half_mean(u * t, lane)
            return dt, jnp.sum(dtn * t * r, axis=0, keepdims=True)

        for j in range(4):
            dp_ref[:, j * LANES:(j + 1) * LANES] = (unrope(unplace(dqa_ref, j)) * Q_SCALE).astype(BF16)
            c0 = 768 + j * LANES
            dt, dg = unrms(unrope(unplace(dqb_ref, j)) * Q_SCALE, p_ref[:, c0:c0 + LANES], qg_ref[...])
            dp_ref[:, c0:c0 + LANES] = dt.astype(BF16)
            dqg_ref[:, j * LANES:(j + 1) * LANES] += dg
        dp_ref[:, 512:640] = unrope(dka_ref[...]).astype(BF16)
        dp_ref[:, 640:768] = dva_ref[...].astype(BF16)
        dt, dg = unrms(unrope(dkb_ref[...]), p_ref[:, 1280:1408], kg_ref[...])
        dp_ref[:, 1280:1408] = dt.astype(BF16)
        dkg_ref[...] += dg
        dp_ref[:, 1408:1536] = dvb_ref[...].astype(BF16)
        dp_ref[:, OFF_GA:] = dgl_ref[...]

    row = pl.BlockSpec((tm, LANES), lambda i: (i, 0))
    qspec = pl.BlockSpec((N_HEADS, LANES, tm), lambda i: (0, 0, i))
    return _call(
        body, name=name, grid=(m // tm,),
        in_specs=[qspec, row, row, qspec, row, row, pl.BlockSpec((tm, OFF_GA), lambda i: (i, 0)), row, row,
                  _full((1, LANES)), _full((1, LANES)), pl.BlockSpec((tm, IN_COLS - OFF_GA), lambda i: (i, 0))],
        out_specs=[pl.BlockSpec((tm, IN_COLS), lambda i: (i, 0)), _full((1, 512)), _full((1, LANES))],
        out_shape=[_sds((m, IN_COLS), BF16), _sds((1, 512)), _sds((1, LANES))],
        sem=("arbitrary",),
    )(dqa, dka, dva, dqb, dkb, dvb, proj, cos, sin, qg, kg, dgl)


def _attn_glob_fwd(qt, k, vt, kc, vct, *, tq, tk):
    nh, _, s = qt.shape
    nc = kc.shape[0]
    n_chunks = s // tk
    half = LANES // 2

    def body(qt_ref, k_ref, vt_ref, kc_ref, vct_ref, ot_ref, lse_ref, mrun_ref, p_hbm,
             acc_sc, st_sc, stage_sc, stagec_sc, sems, semc):
        h, i = pl.program_id(0), pl.program_id(1)
        qtv = qt_ref[...]
        acc_sc[...] = jnp.zeros(acc_sc.shape, F32)

        def p_out(slot, c):
            return pltpu.make_async_copy(stage_sc.at[slot], p_hbm.at[h, i, pl.ds(pl.multiple_of(c * tk, tk), tk), :],
                                         sems.at[slot])

        def update(st, vtv, m_old):
            m_new = jnp.maximum(m_old, jnp.max(st, axis=0, keepdims=True))
            pb = jnp.exp(st - m_new).astype(BF16)
            acc_sc[...] = acc_sc[...] * jnp.exp(m_old - m_new) + _dot(vtv, pb)
            return m_new, pb

        m, pbc = update(_dot(kc_ref[...], qtv), vct_ref[...], jnp.full((1, tq), NEG, F32))
        mrun_ref[pl.ds(n_chunks, 1), :] = m
        stagec_sc[...] = pbc
        ctx_out = pltpu.make_async_copy(stagec_sc, p_hbm.at[h, i, pl.ds(s, nc), :], semc)
        ctx_out.start()

        def step(c, st, m_old):
            slot = c % 2
            off = pl.multiple_of(c * tk, tk)
            nxt = pl.multiple_of(jnp.minimum(c + 1, n_chunks - 1) * tk, tk)
            st_next = _dot(k_ref[pl.ds(nxt, tk), :], qtv)
            m_new, pb = update(st, vt_ref[:, pl.ds(off, tk)], m_old)
            mrun_ref[pl.ds(c, 1), :] = m_new
            stage_sc[slot] = pb
            p_out(slot, c).start()
            return st_next, m_new

        def loop(c, m_old):
            st_next, m_new = step(c, st_sc[...], m_old)
            p_out(1 - c % 2, c - 1).wait()
            st_sc[...] = st_next
            return m_new

        stage_sc[1] = jnp.zeros((tk, tq), BF16)
        pltpu.make_async_copy(stage_sc.at[1], p_hbm.at[h, i, pl.ds(s + nc, tk), :], sems.at[1]).start()
        st_sc[...] = _dot(k_ref[pl.ds(0, tk), :], qtv)
        m = lax.fori_loop(0, n_chunks, loop, m)
        p_out((n_chunks - 1) % 2, n_chunks - 1).wait()
        ctx_out.wait()
        acc = acc_sc[...]
        l = jnp.where(h < nh // N_KV, acc[half:half + 1], acc[0:1])
        ot_ref[...] = (acc / l).astype(BF16)
        lse_ref[...] = m + jnp.log(l)

    grp = nh // N_KV
    return _call(
        body, name="attn_glob_fwd", grid=(nh, s // tq),
        in_specs=[pl.BlockSpec((None, LANES, tq), lambda h, i: (h, 0, i)), _full((s, LANES)),
                  pl.BlockSpec((None, LANES, s), lambda h, i: (h // grp, 0, 0)), _full((nc, LANES)),
                  pl.BlockSpec((None, LANES, nc), lambda h, i: (h // grp, 0, 0))],
        out_specs=[pl.BlockSpec((None, LANES, tq), lambda h, i: (h, 0, i)),
                   pl.BlockSpec((None, 1, tq), lambda h, i: (h, 0, i)),
                   pl.BlockSpec((None, n_chunks + 1, tq), lambda h, i: (h, 0, i)), ANY],
        out_shape=[_sds((nh, LANES, s), BF16), _sds((nh, 1, s)), _sds((nh, n_chunks + 1, s)),
                   _sds((nh, s // tq, s + nc + tk, tq), BF16)],
        scratch=[pltpu.VMEM((LANES, tq), F32), pltpu.VMEM((tk, tq), F32), pltpu.VMEM((2, tk, tq), BF16),
                 pltpu.VMEM((nc, tq), BF16), pltpu.SemaphoreType.DMA((2,)), pltpu.SemaphoreType.DMA],
        sem=("parallel", "parallel"),
    )(qt, k, vt, kc, vct)


P_AHEAD = 3


def _attn_glob_bwd(qt, dot, ot, lse, mrun, p, kt, v, kct, vc, *, tq, tk):
    nh, _, s = qt.shape
    nc = vc.shape[0]
    n_q = s // tq
    n_chunks = s // tk
    n_run = mrun.shape[1] - 1
    per_run = n_chunks // n_run

    def body(qt_ref, dot_ref, ot_ref, lse_ref, mrun_ref, p_hbm, kt_ref, v_ref, kct_ref, vc_ref,
             dqt_ref, dkt_ref, dvt_ref, dkct_ref, dvct_ref, acc_sc, dp_sc, dkt_sc, dvt_sc, p_sc, pc_sc, sems, semc):
        h, i = pl.program_id(0), pl.program_id(1)

        @pl.when(jnp.logical_and(h == 0, i == 0))
        def _():
            dkct_ref[...] = jnp.zeros(dkct_ref.shape, F32)
            dvct_ref[...] = jnp.zeros(dvct_ref.shape, F32)
            dkt_sc[...] = jnp.zeros(dkt_sc.shape, F32)
            dvt_sc[...] = jnp.zeros(dvt_sc.shape, F32)


        def p_in(slot, c):
            return pltpu.make_async_copy(p_hbm.at[h, i, pl.ds(pl.multiple_of(c * tk, tk), tk), :], p_sc.at[slot],
                                         sems.at[slot])

        ctx_in = pltpu.make_async_copy(p_hbm.at[h, i, pl.ds(s, nc), :], pc_sc, semc)
        ctx_in.start()
        for c in range(P_AHEAD):
            p_in(c, min(c, n_chunks - 1)).start()
        qtv, dotv, lse = qt_ref[...], dot_ref[...], lse_ref[...]
        delta = jnp.sum(dotv.astype(F32) * ot_ref[...].astype(F32), axis=0, keepdims=True)

        def grads(pt_stored, m_row, dpt):
            pt = pt_stored.astype(F32) * jnp.exp(m_row - lse)
            return pt.astype(BF16), (pt * (dpt - delta)).astype(BF16)

        dp_sc[...] = _dot(v_ref[pl.ds(0, tk), :], dotv)
        ctx_in.wait()
        pb, dsb = grads(pc_sc[...], mrun_ref[pl.ds(n_run, 1), :], _dot(vc_ref[...], dotv))
        acc_sc[...] = _dot(kct_ref[...], dsb)
        dkct_ref[...] += _dot_nt(qtv, dsb)
        dvct_ref[...] += _dot_nt(dotv, pb)

        def loop(c, carry):
            slot = c % (P_AHEAD + 1)
            off = pl.multiple_of(c * tk, tk)
            nxt = pl.multiple_of(jnp.minimum(c + 1, n_chunks - 1) * tk, tk)
            p_in(slot, c).wait()
            p_in((c + P_AHEAD) % (P_AHEAD + 1), jnp.minimum(c + P_AHEAD, n_chunks - 1)).start()
            dpt = dp_sc[...]
            dp_next = _dot(v_ref[pl.ds(nxt, tk), :], dotv)
            pb, dsb = grads(p_sc[slot], mrun_ref[pl.ds(c // per_run, 1), :], dpt)
            acc_sc[...] += _dot(kt_ref[:, pl.ds(off, tk)], dsb)
            dkt_sc[:, pl.ds(off, tk)] += _dot_nt(qtv, dsb)
            dvt_sc[:, pl.ds(off, tk)] += _dot_nt(dotv, pb)
            dp_sc[...] = dp_next
            return carry

        lax.fori_loop(0, n_chunks, loop, 0)
        for c in range(n_chunks, n_chunks + P_AHEAD):
            p_in(c % (P_AHEAD + 1), n_chunks - 1).wait()
        dqt_ref[...] = acc_sc[...]

        @pl.when(jnp.logical_and(h == nh - 1, i == n_q - 1))
        def _():
            pltpu.sync_copy(dkt_sc, dkt_ref)
            pltpu.sync_copy(dvt_sc, dvt_ref)

    qs = pl.BlockSpec((None, LANES, tq), lambda h, i: (h, 0, i))
    rs = pl.BlockSpec((None, 1, tq), lambda h, i: (h, 0, i))
    return _call(
        body, name="attn_glob_bwd", grid=(nh, n_q),
        in_specs=[qs, qs, qs, rs, pl.BlockSpec((None, n_run + 1, tq), lambda h, i: (h, 0, i)), ANY,
                  _full((LANES, s)), _full((s, LANES)), _full((LANES, nc)), _full((nc, LANES))],
        out_specs=[qs, ANY, ANY, _full((LANES, nc)), _full((LANES, nc))],
        out_shape=[_sds((nh, LANES, s)), _sds((LANES, s)), _sds((LANES, s)), _sds((LANES, nc)), _sds((LANES, nc))],
        scratch=[pltpu.VMEM((LANES, tq), F32), pltpu.VMEM((tk, tq), F32), pltpu.VMEM((LANES, s), F32),
                 pltpu.VMEM((LANES, s), F32), pltpu.VMEM((P_AHEAD + 1, tk, tq), BF16), pltpu.VMEM((nc, tq), BF16),
                 pltpu.SemaphoreType.DMA((P_AHEAD + 1,)), pltpu.SemaphoreType.DMA],
        sem=("arbitrary", "arbitrary"),
    )(qt, dot, ot, lse, mrun, p, kt, v, kct, vc)


WIN_SPAN = 2 * WINDOW


def _band(rows0, cols0, shape):
    r = rows0 + lax.broadcasted_iota(jnp.int32, shape, 0)
    c = cols0 + lax.broadcasted_iota(jnp.int32, shape, 1)
    return jnp.abs(r - c) <= WINDOW


def _win_start(blk, t, s):
    return pl.multiple_of(jnp.clip(blk * t - WINDOW, 0, s - t - WIN_SPAN), WINDOW)


def _attn_win_fwd(qt, k, vt, kc, vct, sink, *, tq):
    nh, _, s = qt.shape
    nc = kc.shape[0]
    tw = tq + WIN_SPAN
    half = LANES // 2
    grp = nh // N_KV

    def body(sink_ref, qt_ref, k_ref, vt_ref, kc_ref, vct_ref, ot_ref, lse_ref):
        h, i = pl.program_id(0), pl.program_id(1)
        k0 = _win_start(i, tq, s)
        qtv = qt_ref[...]
        st = jnp.where(_band(k0, i * tq, (tw, tq)), _dot(k_ref[pl.ds(k0, tw), :], qtv), NEG)
        stc = _dot(kc_ref[...], qtv)
        snk = sink_ref[h]
        m = jnp.maximum(jnp.maximum(jnp.max(st, axis=0, keepdims=True), jnp.max(stc, axis=0, keepdims=True)), snk)
        acc = (_dot(vt_ref[:, pl.ds(k0, tw)], jnp.exp(st - m).astype(BF16))
               + _dot(vct_ref[...], jnp.exp(stc - m).astype(BF16)))
        l = jnp.where(h < grp, acc[half:half + 1], acc[0:1]) + jnp.exp(snk - m)
        ot_ref[...] = (acc / l).astype(BF16)
        lse_ref[...] = m + jnp.log(l)

    return _call(
        body, name="attn_win_fwd", grid=(nh, s // tq),
        in_specs=[pl.BlockSpec(memory_space=pltpu.SMEM),
                  pl.BlockSpec((None, LANES, tq), lambda h, i: (h, 0, i)), _full((s, LANES)),
                  pl.BlockSpec((None, LANES, s), lambda h, i: (h // grp, 0, 0)), _full((nc, LANES)),
                  pl.BlockSpec((None, LANES, nc), lambda h, i: (h // grp, 0, 0))],
        out_specs=[pl.BlockSpec((None, LANES, tq), lambda h, i: (h, 0, i)),
                   pl.BlockSpec((None, 1, tq), lambda h, i: (h, 0, i))],
        out_shape=[_sds((nh, LANES, s), BF16), _sds((nh, 1, s))],
        sem=("parallel", "parallel"),
    )(sink, qt, k, vt, kc, vct)


def _attn_win_bwd(qt, dot, ot, lse, k, kt, v, kc, kct, vc, sink, *, tq):
    nh, _, s = qt.shape
    nc = kc.shape[0]
    tw = tq + WIN_SPAN
    nq = s // tq

    def body(sink_ref, qt_ref, dot_ref, ot_ref, lse_ref, k_ref, kt_ref, v_ref, kc_ref, kct_ref, vc_ref,
             dqt_ref, dkt_ref, dvt_ref, dkct_ref, dvct_ref, dsk_ref, dkt_sc, dvt_sc):
        h, i = pl.program_id(0), pl.program_id(1)

        @pl.when(jnp.logical_and(h == 0, i == 0))
        def _():
            dkct_ref[...] = jnp.zeros(dkct_ref.shape, F32)
            dvct_ref[...] = jnp.zeros(dvct_ref.shape, F32)
            dkt_sc[...] = jnp.zeros(dkt_sc.shape, F32)
            dvt_sc[...] = jnp.zeros(dvt_sc.shape, F32)

        k0 = _win_start(i, tq, s)
        span = pl.ds(k0, tw)
        qtv, dotv, lse = qt_ref[...], dot_ref[...], lse_ref[...]
        delta = jnp.sum(dotv.astype(F32) * ot_ref[...].astype(F32), axis=0, keepdims=True)
        pt = jnp.where(_band(k0, i * tq, (tw, tq)), jnp.exp(_dot(k_ref[span, :], qtv) - lse), 0.0)
        dsb = (pt * (_dot(v_ref[span, :], dotv) - delta)).astype(BF16)
        pct = jnp.exp(_dot(kc_ref[...], qtv) - lse)
        dscb = (pct * (_dot(vc_ref[...], dotv) - delta)).astype(BF16)
        dqt_ref[...] = _dot(kt_ref[:, span], dsb) + _dot(kct_ref[...], dscb)
        dkt_sc[:, span] += _dot_nt(qtv, dsb)
        dvt_sc[:, span] += _dot_nt(dotv, pt.astype(BF16))
        dkct_ref[...] += _dot_nt(qtv, dscb)
        dvct_ref[...] += _dot_nt(dotv, pct.astype(BF16))
        dsk = -jnp.sum(jnp.exp(sink_ref[h] - lse) * delta)
        dsk_ref[...] = jnp.full(dsk_ref.shape, dsk, F32)

        @pl.when(jnp.logical_and(h == nh - 1, i == nq - 1))
        def _():
            pltpu.sync_copy(dkt_sc, dkt_ref)
            pltpu.sync_copy(dvt_sc, dvt_ref)

    qs = pl.BlockSpec((None, LANES, tq), lambda h, i: (h, 0, i))
    rs = pl.BlockSpec((None, 1, tq), lambda h, i: (h, 0, i))
    return _call(
        body, name="attn_win_bwd", grid=(nh, nq),
        in_specs=[pl.BlockSpec(memory_space=pltpu.SMEM), qs, qs, qs, rs, _full((s, LANES)), _full((LANES, s)),
                  _full((s, LANES)), _full((nc, LANES)), _full((LANES, nc)), _full((nc, LANES))],
        out_specs=[qs, ANY, ANY, _full((LANES, nc)), _full((LANES, nc)),
                   pl.BlockSpec((None, None, 8, LANES), lambda h, i: (h, i, 0, 0))],
        out_shape=[_sds((nh, LANES, s)), _sds((LANES, s)), _sds((LANES, s)), _sds((LANES, nc)), _sds((LANES, nc)),
                   _sds((nh, nq, 8, LANES))],
        scratch=[pltpu.VMEM((LANES, s), F32), pltpu.VMEM((LANES, s), F32)],
        sem=("arbitrary", "arbitrary"),
    )(sink, qt, dot, ot, lse, k, kt, v, kc, kct, vc)


def _ln_fwd(z, g, b):
    mu = jnp.mean(z, axis=-1, keepdims=True)
    zc = z - mu
    r = lax.rsqrt(jnp.mean(zc * zc, axis=-1, keepdims=True) + LN_EPS)
    return zc * r * g + b, mu, r


def _ln_bwd(dy, xhat, r, g):
    dxh = dy * g
    return r * (dxh - jnp.mean(dxh, axis=-1, keepdims=True) - xhat * jnp.mean(dxh * xhat, axis=-1, keepdims=True))


def _heads_matmul(ot_ref, w_ref):
    acc = _dot_tn(ot_ref[0], w_ref[0])
    for h in range(1, N_HEADS):
        acc += _dot_tn(ot_ref[h], w_ref[h])
    return acc


def _gate_specs(tm):
    return [pl.BlockSpec((tm, 512), functools.partial(lambda i, b: (i, b), b=OFF_GA // 512 + b)) for b in range(4)]


def _merge_fwd(oat, obt, proj, x, gate1, wba, wbb, w_out, ln_g, ln_b, *, tm):
    s = x.shape[0]

    def body(oa_ref, ob_ref, g0, g1, g2, g3, x_ref, gt_ref, wba_ref, wbb_ref, wo_ref, lg_ref, lb_ref,
             x1_ref, y_ref, mu_ref, r_ref, pa_ref, pb_ref, mg_ref):
        ga = _sigmoid(jnp.concatenate([g0[...], g1[...]], axis=1))
        gb = _sigmoid(jnp.concatenate([g2[...], g3[...]], axis=1))
        pa, pb = _heads_matmul(oa_ref, wba_ref), _heads_matmul(ob_ref, wbb_ref)
        merged = (ga * pa + gb * pb).astype(BF16)
        y = _dot(merged, wo_ref[...])
        x1, mu, r = _ln_fwd(ALPHA * x_ref[...] + gt_ref[...] * y, lg_ref[...], lb_ref[...])
        x1_ref[...] = x1
        y_ref[...] = y
        mu_ref[...] = mu
        r_ref[...] = r
        pa_ref[...] = pa.astype(BF16)
        pb_ref[...] = pb.astype(BF16)
        mg_ref[...] = merged

    hts = pl.BlockSpec((N_HEADS, LANES, tm), lambda i: (0, 0, i))
    row = pl.BlockSpec((tm, D_MODEL), lambda i: (i, 0))
    col = pl.BlockSpec((tm, 1), lambda i: (i, 0))
    vec = _full((1, D_MODEL))
    wh = _full((N_HEADS, LANES, D_MODEL))
    return _call(
        body, name="merge_fwd", grid=(s // tm,),
        in_specs=[hts, hts, *_gate_specs(tm), row, vec, wh, wh, _full((D_MODEL, D_MODEL)), vec, vec],
        out_specs=[row, row, col, col, row, row, row],
        out_shape=[_sds((s, D_MODEL)), _sds((s, D_MODEL)), _sds((s, 1)), _sds((s, 1)), _sds((s, D_MODEL), BF16),
                   _sds((s, D_MODEL), BF16), _sds((s, D_MODEL), BF16)],
        sem=("parallel",),
    )(oat, obt, proj, proj, proj, proj, x, gate1, wba, wbb, w_out, ln_g, ln_b)


def _merge_bwd(dy, oat, obt, pa, pb, proj, wba, wbb, w_out, *, tm):
    s = dy.shape[0]

    def body(dy_ref, oat_ref, obt_ref, pa_ref, pb_ref, g0, g1, g2, g3, wba_ref, wbb_ref, wo_ref,
             dgl_ref, doat_ref, dobt_ref, dwa_ref, dwb_ref):
        @pl.when(pl.program_id(0) == 0)
        def _():
            dwa_ref[...] = jnp.zeros(dwa_ref.shape, F32)
            dwb_ref[...] = jnp.zeros(dwb_ref.shape, F32)

        dm = _dot_nt(dy_ref[...], wo_ref[...])
        ga = _sigmoid(jnp.concatenate([g0[...], g1[...]], axis=1))
        gb = _sigmoid(jnp.concatenate([g2[...], g3[...]], axis=1))
        pa, pb = pa_ref[...].astype(F32), pb_ref[...].astype(F32)
        dgl_ref[:, :D_MODEL] = (dm * pa * ga * (1.0 - ga)).astype(BF16)
        dgl_ref[:, D_MODEL:] = (dm * pb * gb * (1.0 - gb)).astype(BF16)
        dpa, dpb = (dm * ga).astype(BF16), (dm * gb).astype(BF16)
        for h in range(N_HEADS):
            doat_ref[h] = _dot_nt(wba_ref[h], dpa).astype(BF16)
            dobt_ref[h] = _dot_nt(wbb_ref[h], dpb).astype(BF16)
            dwa_ref[h] += _dot(oat_ref[h], dpa)
            dwb_ref[h] += _dot(obt_ref[h], dpb)

    hts = pl.BlockSpec((N_HEADS, LANES, tm), lambda i: (0, 0, i))
    row = pl.BlockSpec((tm, D_MODEL), lambda i: (i, 0))
    wh = _full((N_HEADS, LANES, D_MODEL))
    return _call(
        body, name="merge_bwd", grid=(s // tm,),
        in_specs=[row, hts, hts, row, row, *_gate_specs(tm), wh, wh, _full((D_MODEL, D_MODEL))],
        out_specs=[pl.BlockSpec((tm, 2 * D_MODEL), lambda i: (i, 0)), hts, hts, wh, wh],
        out_shape=[_sds((s, 2 * D_MODEL), BF16), _sds((N_HEADS, LANES, s), BF16), _sds((N_HEADS, LANES, s), BF16),
                   _sds((N_HEADS, LANES, D_MODEL)), _sds((N_HEADS, LANES, D_MODEL))],
        sem=("arbitrary",),
    )(dy, oat, obt, pa, pb, proj, proj, proj, proj, wba, wbb, w_out)


FF_TC = 256


def _shift_rows(t, prev_row, next_row):
    n = t.shape[0]
    r = lax.broadcasted_iota(jnp.int32, t.shape, 0)
    up = jnp.where(r == 0, prev_row, pltpu.roll(t, 1, 0))
    dn = jnp.where(r == n - 1, next_row, pltpu.roll(t, n - 1, 0))
    return up, dn


HALO = 16


def _halo_specs(tm, s, tc):
    nb = s // HALO
    main = pl.BlockSpec((2, tm, tc), lambda j, i: (0, i, j))
    prev = pl.BlockSpec((2, HALO, tc), lambda j, i: (0, jnp.maximum(i * (tm // HALO) - 1, 0), j))
    nxt = pl.BlockSpec((2, HALO, tc), lambda j, i: (0, jnp.minimum((i + 1) * (tm // HALO), nb - 1), j))
    return main, prev, nxt


def _halo_rows(prev_ref, next_ref, half, i, n_i):
    prev_row = jnp.where(i == 0, 0.0, prev_ref[half, HALO - 1:HALO, :].astype(F32))
    next_row = jnp.where(i == n_i - 1, 0.0, next_ref[half, 0:1, :].astype(F32))
    return prev_row, next_row


def _conv(t, prev_row, next_row, w, b):
    up, dn = _shift_rows(t, prev_row, next_row)
    return w[0:1, :] * up + w[1:2, :] * t + w[2:3, :] * dn + b


def _ffn_act_fwd(u, cw, cb, *, tm):
    _, s, ff = u.shape
    n_i = s // tm

    def body(u_ref, up_ref, un_ref, cw_ref, cb_ref, a_ref):
        i = pl.program_id(1)
        gc = _conv(u_ref[0].astype(F32), *_halo_rows(up_ref, un_ref, 0, i, n_i), cw_ref[0], cb_ref[0])
        vc = _conv(u_ref[1].astype(F32), *_halo_rows(up_ref, un_ref, 1, i, n_i), cw_ref[1], cb_ref[1])
        a_ref[...] = (gc * _sigmoid(gc) * vc).astype(BF16)

    main, prev, nxt = _halo_specs(tm, s, FF_TC)
    return _call(
        body, name="ffn_act_fwd", grid=(ff // FF_TC, n_i),
        in_specs=[main, prev, nxt, pl.BlockSpec((2, 3, FF_TC), lambda j, i: (0, 0, j)),
                  pl.BlockSpec((2, 1, FF_TC), lambda j, i: (0, 0, j))],
        out_specs=pl.BlockSpec((tm, FF_TC), lambda j, i: (i, j)),
        out_shape=_sds((s, ff), BF16), sem=("parallel", "parallel"),
    )(u, u, u, cw, cb)


def _ffn_act_bwd(dy2, w_down, u, cw, cb, *, tm):
    _, s, ff = u.shape
    n_i = s // tm

    def body(dy_ref, wd_ref, u_ref, up_ref, un_ref, cw_ref, cb_ref, dc_ref, dcw_ref, dcb_ref):
        i = pl.program_id(1)

        @pl.when(i == 0)
        def _():
            dcw_ref[...] = jnp.zeros(dcw_ref.shape, F32)
            dcb_ref[...] = jnp.zeros(dcb_ref.shape, F32)

        da = _dot_nt(dy_ref[...], wd_ref[...])
        ug, uv = u_ref[0].astype(F32), u_ref[1].astype(F32)
        ugp, ugn = _shift_rows(ug, *_halo_rows(up_ref, un_ref, 0, i, n_i))
        uvp, uvn = _shift_rows(uv, *_halo_rows(up_ref, un_ref, 1, i, n_i))
        wg, wv = cw_ref[0], cw_ref[1]
        gc = wg[0:1, :] * ugp + wg[1:2, :] * ug + wg[2:3, :] * ugn + cb_ref[0]
        vc = wv[0:1, :] * uvp + wv[1:2, :] * uv + wv[2:3, :] * uvn + cb_ref[1]
        sg = _sigmoid(gc)
        dg = da * vc * sg * (1.0 + gc * (1.0 - sg))
        dv = da * gc * sg
        dc_ref[0] = dg.astype(BF16)
        dc_ref[1] = dv.astype(BF16)
        for half, (d, taps) in enumerate(((dg, (ugp, ug, ugn)), (dv, (uvp, uv, uvn)))):
            for tap in range(3):
                dcw_ref[half, tap:tap + 1, :] += jnp.sum(d * taps[tap], axis=0, keepdims=True)
            dcb_ref[half] += jnp.sum(d, axis=0, keepdims=True)

    main, prev, nxt = _halo_specs(tm, s, FF_TC)
    return _call(
        body, name="ffn_act_bwd", grid=(ff // FF_TC, n_i),
        in_specs=[pl.BlockSpec((tm, D_MODEL), lambda j, i: (i, 0)), pl.BlockSpec((FF_TC, D_MODEL), lambda j, i: (j, 0)),
                  main, prev, nxt, pl.BlockSpec((2, 3, FF_TC), lambda j, i: (0, 0, j)),
                  pl.BlockSpec((2, 1, FF_TC), lambda j, i: (0, 0, j))],
        out_specs=[main, pl.BlockSpec((2, 3, FF_TC), lambda j, i: (0, 0, j)),
                   pl.BlockSpec((2, 1, FF_TC), lambda j, i: (0, 0, j))],
        out_shape=[_sds((2, s, ff), BF16), _sds((2, 3, ff)), _sds((2, 1, ff))],
        sem=("parallel", "arbitrary"),
    )(dy2, w_down, u, u, u, cw, cb)


def _conv_bwd_input(dc, cw, *, tm):
    _, s, ff = dc.shape
    n_i = s // tm

    def body(d_ref, dp_ref, dn_ref, cw_ref, du_ref):
        i = pl.program_id(1)
        for half in range(2):
            d = d_ref[half].astype(F32)
            up, dn = _shift_rows(d, *_halo_rows(dp_ref, dn_ref, half, i, n_i))
            w = cw_ref[half]
            du_ref[half] = (w[0:1, :] * dn + w[1:2, :] * d + w[2:3, :] * up).astype(BF16)

    main, prev, nxt = _halo_specs(tm, s, FF_TC)
    return _call(
        body, name="conv_bwd_input", grid=(ff // FF_TC, n_i),
        in_specs=[main, prev, nxt, pl.BlockSpec((2, 3, FF_TC), lambda j, i: (0, 0, j))],
        out_specs=main, out_shape=_sds((2, s, ff), BF16), sem=("parallel", "parallel"),
    )(dc, dc, dc, cw)


def _ffn_down_loss(a, w_down, x1, target, gate2, ln_g, ln_b, *, tm):
    s, ff = a.shape
    n_i = s // tm

    def body(a_ref, wd_ref, x1_ref, tg_ref, gt_ref, lg_ref, lb_ref, ls_ref, dy_ref, dx_ref, dg_ref, db_ref, dgt_ref):
        @pl.when(pl.program_id(0) == 0)
        def _():
            dg_ref[...] = jnp.zeros(dg_ref.shape, F32)
            db_ref[...] = jnp.zeros(db_ref.shape, F32)
            dgt_ref[...] = jnp.zeros(dgt_ref.shape, F32)

        y2 = _dot(a_ref[...], wd_ref[...])
        z = ALPHA * x1_ref[...] + gt_ref[...] * y2
        mu = jnp.mean(z, axis=-1, keepdims=True)
        zc = z - mu
        r = lax.rsqrt(jnp.mean(zc * zc, axis=-1, keepdims=True) + LN_EPS)
        xhat = zc * r
        diff = xhat * lg_ref[...] + lb_ref[...] - tg_ref[...]
        ls_ref[...] = jnp.full(ls_ref.shape, 0.5 / D_MODEL * jnp.sum(diff * diff), F32)
        dx2 = diff * (1.0 / D_MODEL)
        dg_ref[...] += jnp.sum(dx2 * xhat, axis=0, keepdims=True)
        db_ref[...] += jnp.sum(dx2, axis=0, keepdims=True)
        dz = _ln_bwd(dx2, xhat, r, lg_ref[...])
        dgt_ref[...] += jnp.sum(dz * y2, axis=0, keepdims=True)
        dy_ref[...] = (gt_ref[...] * dz).astype(BF16)
        dx_ref[...] = ALPHA * dz

    row = pl.BlockSpec((tm, D_MODEL), lambda i: (i, 0))
    vec = _full((1, D_MODEL))
    return _call(
        body, name="ffn_down_loss", grid=(n_i,),
        in_specs=[pl.BlockSpec((tm, ff), lambda i: (i, 0)), _full((ff, D_MODEL)), row, row, vec, vec, vec],
        out_specs=[pl.BlockSpec((None, 8, LANES), lambda i: (i, 0, 0)), row, row, vec, vec, vec],
        out_shape=[_sds((n_i, 8, LANES)), _sds((s, D_MODEL), BF16), _sds((s, D_MODEL)),
                   _sds((1, D_MODEL)), _sds((1, D_MODEL)), _sds((1, D_MODEL))],
        sem=("arbitrary",),
    )(a, w_down, x1, target, gate2, ln_g, ln_b)


def _ffn_up_bwd(du, wup4, dx1a, x1, scale2, x, y, mu1, r1, gate1, ln_g, *, tm):
    s = x.shape[0]
    nb, _, ns = wup4.shape

    def body(du_ref, w_ref, dxa_ref, x1_ref, sc_ref, x_ref, y_ref, mu_ref, r_ref, gt_ref, lg_ref,
             dxo_ref, dy_ref, dsc_ref, dsh_ref, dg_ref, db_ref, dgt_ref, acc):
        i, k = pl.program_id(0), pl.program_id(1)

        @pl.when(jnp.logical_and(i == 0, k == 0))
        def _():
            for ref in (dsc_ref, dsh_ref, dg_ref, db_ref, dgt_ref):
                ref[...] = jnp.zeros(ref.shape, F32)

        @pl.when(k == 0)
        def _():
            acc[...] = jnp.zeros(acc.shape, F32)

        acc[...] += _dot_nt(du_ref[...], w_ref[...])

        @pl.when(k == nb - 1)
        def _():
            dh = acc[...]
            x1 = x1_ref[...]
            dsc_ref[...] += jnp.sum(dh * x1, axis=0, keepdims=True)
            dsh_ref[...] += jnp.sum(dh, axis=0, keepdims=True)
            dx1 = dxa_ref[...] + dh * (1.0 + sc_ref[...])
            yv = y_ref[...]
            xhat = (ALPHA * x_ref[...] + gt_ref[...] * yv - mu_ref[...]) * r_ref[...]
            dg_ref[...] += jnp.sum(dx1 * xhat, axis=0, keepdims=True)
            db_ref[...] += jnp.sum(dx1, axis=0, keepdims=True)
            dz = _ln_bwd(dx1, xhat, r_ref[...], lg_ref[...])
            dgt_ref[...] += jnp.sum(dz * yv, axis=0, keepdims=True)
            dy_ref[...] = (gt_ref[...] * dz).astype(BF16)
            dxo_ref[...] = ALPHA * dz

    row = pl.BlockSpec((tm, D_MODEL), lambda i, k: (i, 0))
    col = pl.BlockSpec((tm, 1), lambda i, k: (i, 0))
    vec = _full((1, D_MODEL))
    return _call(
        body, name="ffn_up_bwd", grid=(s // tm, nb),
        in_specs=[pl.BlockSpec((None, tm, ns), lambda i, k: (k // 2, i, k % 2)),
                  pl.BlockSpec((None, D_MODEL, ns), lambda i, k: (k, 0, 0)),
                  row, row, vec, row, row, col, col, vec, vec],
        out_specs=[row, row, vec, vec, vec, vec, vec],
        out_shape=[_sds((s, D_MODEL)), _sds((s, D_MODEL), BF16)] + [_sds((1, D_MODEL))] * 5,
        scratch=[pltpu.VMEM((tm, D_MODEL), F32)],
        sem=("arbitrary", "arbitrary"),
    )(du, wup4, dx1a, x1, scale2, x, y, mu1, r1, gate1, ln_g)


def _mm_nt4_mod_bwd(dp, w4, dxa, x, scale, *, tm, name):
    m = x.shape[0]
    nb, kdim, ns = w4.shape

    def body(dp_ref, w_ref, dxa_ref, x_ref, sc_ref, dx_ref, dsc_ref, dsh_ref, acc):
        i, k = pl.program_id(0), pl.program_id(1)

        @pl.when(jnp.logical_and(i == 0, k == 0))
        def _():
            dsc_ref[...] = jnp.zeros(dsc_ref.shape, F32)
            dsh_ref[...] = jnp.zeros(dsh_ref.shape, F32)

        @pl.when(k == 0)
        def _():
            acc[...] = jnp.zeros(acc.shape, F32)

        acc[...] += _dot_nt(dp_ref[...], w_ref[...])

        @pl.when(k == nb - 1)
        def _():
            dh = acc[...]
            dsc_ref[...] += jnp.sum(dh * x_ref[...], axis=0, keepdims=True)
            dsh_ref[...] += jnp.sum(dh, axis=0, keepdims=True)
            dx_ref[...] = dxa_ref[...] + dh * (1.0 + sc_ref[...])

    row = pl.BlockSpec((tm, kdim), lambda i, k: (i, 0))
    vec = _full((1, kdim))
    return _call(
        body, name=name, grid=(m // tm, nb),
        in_specs=[pl.BlockSpec((tm, ns), lambda i, k: (i, k)), pl.BlockSpec((None, kdim, ns), lambda i, k: (k, 0, 0)),
                  row, row, vec],
        out_specs=[row, vec, vec],
        out_shape=[_sds((m, kdim)), _sds((1, kdim)), _sds((1, kdim))],
        scratch=[pltpu.VMEM((tm, kdim), F32)],
        sem=("arbitrary", "arbitrary"),
    )(dp, w4, dxa, x, scale)


def _pad_heads_w(w):
    w8 = w.reshape(N_HEADS, HEAD_DIM, w.shape[-1])
    z = jnp.zeros_like(w8)
    first = (jnp.arange(N_HEADS) < N_HEADS // N_KV)[:, None, None]
    return jnp.where(first, jnp.concatenate([w8, z], axis=1), jnp.concatenate([z, w8], axis=1))


def _unpad_heads_w(g):
    first = (jnp.arange(N_HEADS) < N_HEADS // N_KV)[:, None, None]
    return jnp.where(first, g[:, :HEAD_DIM], g[:, HEAD_DIM:]).reshape(N_HEADS * HEAD_DIM, g.shape[-1])


def _ones_beside(vt):
    half = vt.shape[0] // 2
    ones = jnp.ones((half, vt.shape[1]), vt.dtype)
    return jnp.stack([jnp.concatenate([vt[:half], ones], axis=0), jnp.concatenate([ones, vt[half:]], axis=0)])


def _rep8(a):
    return jnp.broadcast_to(a.reshape(1, -1), (8, a.size))


def _first_row(a):
    r8 = _rep8(a)
    return jnp.where(lax.broadcasted_iota(jnp.int32, r8.shape, 0) == 0, r8, 0.0)


def _to_blocks4(w):
    k, n = w.shape
    return w.reshape(k, N_CHIPS, n // N_CHIPS).transpose(1, 0, 2)


def _local_step(x, c, ctx, c_ctx, wmod4, b_mod, win4, b_in, sink, qn, kn, wba, wbb, w_out, ln1_g, ln1_b,
                wup4, cw, cb, w_down, ln2_g, ln2_b, target):
    s, nc = x.shape[0], ctx.shape[0]
    tm = min(512, s)
    tm2 = min(256, s)
    tl = min(1024, s)
    tx = min(2048, s)
    zvec = jnp.zeros((1, D_MODEL), F32)

    cc = jnp.concatenate([_rep8(c), _rep8(c_ctx)], axis=0)
    mods = _mm_nn4(cc, zvec, zvec, wmod4, b_mod, mode="silu", split_out=False, out_dtype=F32, tm=16, name="mod_vectors")
    shift1, scale1, gate1, shift2, scale2, gate2 = [mods[0:1, i * D_MODEL:(i + 1) * D_MODEL] for i in range(6)]
    shift_c, scale_c = mods[8:9, :D_MODEL], mods[8:9, D_MODEL:2 * D_MODEL]

    cos, sin = _rope_tables(s)
    cos_c, sin_c = jnp.ones((nc, LANES), F32), jnp.zeros((nc, LANES), F32)
    qg, kg = jnp.tile(qn, (1, 2)), jnp.tile(kn, (1, 2))

    proj_c = _mm_nn4(ctx, shift_c, scale_c, win4, b_in, mode="modulate", split_out=False, out_dtype=F32, tm=nc,
                     name="in_proj_ctx")
    _, kac, vac, _, kbc, vbc = _prep(proj_c, cos_c, sin_c, qg, kg, tm=nc, name="prep_ctx")
    proj = _mm_nn4(x, shift1, scale1, win4, b_in, mode="modulate", split_out=False, out_dtype=F32, tm=tx, name="in_proj")
    qat, ka, va, qbt, kb, vb = _prep(proj, cos, sin, qg, kg, tm=tl, name="prep")
    oat, lse_a = _attn_win_fwd(qat, ka, _ones_beside(va.T), kac, _ones_beside(vac.T), sink, tq=tm)
    obt, lse_b, mrun_b, pbt = _attn_glob_fwd(qbt, kb, _ones_beside(vb.T), kbc, _ones_beside(vbc.T), tq=tm,
                                             tk=min(1024, s))
    wba_p, wbb_p = _pad_heads_w(wba), _pad_heads_w(wbb)
    x1, y, mu1, r1, pa, pb, merged = _merge_fwd(oat, obt, proj, x, gate1, wba_p, wbb_p, w_out, ln1_g, ln1_b, tm=tm)
    u = _mm_nn4(x1, shift2, scale2, wup4, jnp.zeros((1, 2 * D_FF), F32), mode="modulate", split_out=True,
                out_dtype=BF16, tm=tx, name="ffn_up")
    cw2 = cw.reshape(3, 2, D_FF).transpose(1, 0, 2)
    cb2 = cb.reshape(2, 1, D_FF)
    a = _ffn_act_fwd(u, cw2, cb2, tm=tx)
    ls, dy2, dx1a, dln2_g, dln2_b, dgate2 = _ffn_down_loss(a, w_down, x1, target, gate2, ln2_g, ln2_b, tm=tm)
    loss = jnp.sum(ls[:, 0, 0])

    n_s = s // tl
    dw_down = _mm_tn(a, dy2, a_spec=pl.BlockSpec((tl, D_FF), lambda t: (t, 0)),
                     b_spec=pl.BlockSpec((tl, D_MODEL), lambda t: (t, 0)), grid=(n_s,),
                     out_shape=_sds((D_FF, D_MODEL)), out_spec=_full((D_FF, D_MODEL)), name="dw_down")
    dc, dcw2, dcb2 = _ffn_act_bwd(dy2, w_down, u, cw2, cb2, tm=tx)
    du = _conv_bwd_input(dc, cw2, tm=tx)
    dxz1, dy, dscale2, dshift2, dln1_g, dln1_b, dgate1 = _ffn_up_bwd(
        du, wup4, dx1a, x1, scale2, x, y, mu1, r1, gate1, ln1_g, tm=tm)
    ns_up = wup4.shape[-1]
    dw_up4 = _mm_tn(x1, du, a_spec=pl.BlockSpec((tx, D_MODEL), lambda k, t: (t, 0)),
                    b_spec=pl.BlockSpec((None, tx, ns_up), lambda k, t: (k // 2, t, k % 2)), grid=(N_CHIPS, s // tx),
                    out_shape=_sds((N_CHIPS, D_MODEL, ns_up)),
                    out_spec=pl.BlockSpec((None, D_MODEL, ns_up), lambda k, t: (k, 0, 0)),
                    mod=(shift2, scale2), name="dw_up")

    dgl, doat, dobt, dwba_p, dwbb_p = _merge_bwd(dy, oat, obt, pa, pb, proj, wba_p, wbb_p, w_out, tm=tm2)
    dwba, dwbb = _unpad_heads_w(dwba_p), _unpad_heads_w(dwbb_p)
    rowspec = pl.BlockSpec((tl, D_MODEL), lambda t: (t, 0))
    dw_out = _mm_tn(merged, dy, a_spec=rowspec, b_spec=rowspec, grid=(n_s,), out_shape=_sds((D_MODEL, D_MODEL)),
                    out_spec=_full((D_MODEL, D_MODEL)), name="dw_out")

    dqat, dkat, dvat, dkact, dvact, dsk = _attn_win_bwd(qat, doat, oat, lse_a, ka, ka.T, va, kac, kac.T, vac, sink, tq=tm)
    dka, dva, dkac, dvac = dkat.T, dvat.T, dkact.T, dvact.T
    dqbt, dkbt, dvbt, dkbct, dvbct = _attn_glob_bwd(qbt, dobt, obt, lse_b, mrun_b, pbt, kb.T, vb, kbc.T, vbc, tq=tm, tk=tm)
    dkb, dvb, dkbc, dvbc = dkbt.T, dvbt.T, dkbct.T, dvbct.T
    dsink = jnp.sum(dsk[:, :, 0, 0], axis=1)

    dproj, dqg, dkg = _prep_bwd(dqat, dka, dva, dqbt, dkb, dvb, proj, cos, sin, qg, kg, dgl, tm=tm, name="prep_bwd")
    grad_x, dscale1, dshift1 = _mm_nt4_mod_bwd(dproj, win4, dxz1, x, scale1, tm=tl, name="in_proj_bwd")
    ns_in = win4.shape[-1]
    win_spec = dict(b_spec=pl.BlockSpec((None, None, ns_in), lambda k, t: (0, 0, k)),
                    out_shape=_sds((N_CHIPS, D_MODEL, ns_in)),
                    out_spec=pl.BlockSpec((None, D_MODEL, ns_in), lambda k, t: (k, 0, 0)),
                    colsum_spec=pl.BlockSpec((8, ns_in), lambda k, t: (0, k)), colsum_shape=_sds((8, IN_COLS)))
    win_spec["b_spec"] = pl.BlockSpec((tx, ns_in), lambda k, t: (t, k))
    dw_in4, db_in = _mm_tn(x, dproj, a_spec=pl.BlockSpec((tx, D_MODEL), lambda k, t: (t, 0)), grid=(N_CHIPS, s // tx),
                           mod=(shift1, scale1), name="dw_in", **win_spec)

    zq = jnp.zeros((N_HEADS, LANES, nc), F32)
    dproj_c, _, dkg_c = _prep_bwd(zq, dkac, dvac, zq, dkbc, dvbc, proj_c, cos_c, sin_c, qg, kg,
                                  jnp.zeros((nc, IN_COLS - OFF_GA), BF16), tm=nc, name="prep_bwd_ctx")
    _, dscale_c, dshift_c = _mm_nt4_mod_bwd(dproj_c, win4, jnp.zeros((nc, D_MODEL), F32), ctx, scale_c, tm=nc,
                                            name="in_proj_bwd_ctx")
    win_spec["b_spec"] = pl.BlockSpec((nc, ns_in), lambda k, t: (t, k))
    dw_in4, db_in_c = _mm_tn(ctx, dproj_c, a_spec=pl.BlockSpec((nc, D_MODEL), lambda k, t: (t, 0)), grid=(N_CHIPS, 1),
                             mod=(shift_c, scale_c), init=dw_in4, name="dw_in_ctx", **win_spec)

    dmod = jnp.concatenate([dshift1, dscale1, dgate1, dshift2, dscale2, dgate2], axis=1)
    dmodc = jnp.concatenate([dshift_c, dscale_c], axis=1)
    dmodc_pad = jnp.concatenate([dmodc, jnp.zeros((1, 4 * D_MODEL), F32)], axis=1)
    dmodc8 = _first_row(dmodc_pad).astype(BF16)
    z8 = jnp.zeros((8, D_MODEL), F32)
    dsilu_c, _, _ = _mm_nt4_mod_bwd(dmodc8, wmod4, z8, z8, zvec, tm=8, name="c_ctx_bwd")
    sg = _sigmoid(c_ctx)
    dc_ctx = dsilu_c[0:1] * sg * (1.0 + c_ctx * (1.0 - sg))

    dqn = jnp.sum(dqg.reshape(N_HEADS, HEAD_DIM), axis=0, keepdims=True)
    dkn = jnp.sum((dkg + dkg_c).reshape(N_KV, HEAD_DIM), axis=0, keepdims=True)
    grads = dict(
        w_in4=dw_in4, b_in=db_in[0:1] + db_in_c[0:1], sink=dsink, qn=dqn, kn=dkn, wba=dwba, wbb=dwbb, w_out=dw_out,
        ln1_g=dln1_g, ln1_b=dln1_b, w_up4=dw_up4, conv_w=dcw2.transpose(1, 0, 2).reshape(3, 2 * D_FF),
        conv_b=dcb2.reshape(1, 2 * D_FF), w_down=dw_down, ln2_g=dln2_g, ln2_b=dln2_b,
        c_ctx=dc_ctx, dmod=dmod, dmodc=dmodc)
    return loss, grad_x, grads


ANY = pl.BlockSpec(memory_space=pl.ANY)


def _mesh_pos():
    return lax.axis_index("x"), lax.axis_index("y"), lax.axis_index("c")


def _other_chips(x, y):
    return [(1 - x, y), (x, 1 - y), (1 - x, 1 - y)]


def _remote(src, dst, send, recv, dev):
    return pltpu.make_async_remote_copy(src_ref=src, dst_ref=dst, send_sem=send, recv_sem=recv, device_id=dev,
                                        device_id_type=MESH)


def _set_block(stack, block, k):
    return lax.dynamic_update_slice(stack, block[None], (k,) + (0,) * block.ndim)


def _gather_shards(arrs, small):
    na = len(arrs)
    halves = [a.shape[0] // 2 for a in arrs]

    def body(*refs):
        ins, small_ref = refs[:na], refs[na]
        outs, small_out = refs[na + 1:2 * na + 1], refs[2 * na + 1]
        send, recv = refs[2 * na + 2:]
        x, y, c = _mesh_pos()
        me = 2 * x + y
        chips = _other_chips(x, y)

        def half(a, cc):
            return pl.ds(cc * halves[a], halves[a])

        sends = []
        for j, chip in enumerate(chips):
            for a in range(na):
                sends.append(_remote(ins[a].at[half(a, c)], outs[a].at[me, half(a, c)], send.at[a, j], recv.at[a, j],
                                     (*chip, c)))
            sends.append(_remote(small_ref, small_out.at[me], send.at[na, j], recv.at[na, j], (*chip, c)))
        for cp in sends:
            cp.start()
        for j, chip in enumerate(chips):
            kj = 2 * chip[0] + chip[1]
            for a in range(na):
                landed = outs[a].at[kj, half(a, c)]
                _remote(landed, landed, send.at[a, j], recv.at[a, j], (*chip, c)).wait_recv()
                fwd = _remote(landed, landed, send.at[a, 3 + j], recv.at[a, 3 + j], (x, y, 1 - c))
                fwd.start()
                sends.append(fwd)
            _remote(small_ref, small_out.at[kj], send.at[na, j], recv.at[na, j], (*chip, c)).wait_recv()
        for j, chip in enumerate(chips):
            kj = 2 * chip[0] + chip[1]
            for a in range(na):
                other = outs[a].at[kj, half(a, 1 - c)]
                _remote(other, other, send.at[a, 3 + j], recv.at[a, 3 + j], (x, y, 1 - c)).wait_recv()
        for cp in sends:
            cp.wait_send()

    out_shape = [_sds((N_CHIPS,) + a.shape, a.dtype) for a in arrs] + [_sds((N_CHIPS,) + small.shape, small.dtype)]
    got = pl.pallas_call(
        body, name="gather_shards", in_specs=[ANY] * (na + 1), out_specs=[ANY] * (na + 1), out_shape=out_shape,
        scratch_shapes=[pltpu.SemaphoreType.DMA((na + 1, 6)), pltpu.SemaphoreType.DMA((na + 1, 6))],
    )(*arrs, small)
    xp, yp, _ = _mesh_pos()
    return [_set_block(g, a, 2 * xp + yp) for g, a in zip(got, list(arrs) + [small])]


def _allgather_rows(v):
    r, n = v.shape

    def body(v_ref, out_ref, send, recv, loc):
        x, y, c = _mesh_pos()
        me, sibling = (x, y, c), (x, y, 1 - c)
        chips = _other_chips(x, y)

        def rows(px, py, pc):
            return out_ref.at[4 * px + 2 * py + pc]

        def copy(k, block, to, src=None):
            return _remote(rows(*block) if src is None else src, rows(*block), send.at[k], recv.at[k], to)

        mine = pltpu.make_async_copy(v_ref, rows(*me), loc)
        mine.start()
        first = [copy(0, me, sibling, src=v_ref)] + [copy(1 + j, me, (*chip, c), src=v_ref) for j, chip in enumerate(chips)]
        for cp in first:
            cp.start()
        passed = [copy(4 + j, (*chip, c), sibling) for j, chip in enumerate(chips)]
        for j, chip in enumerate(chips):
            copy(1 + j, (*chip, c), me).wait_recv()
            passed[j].start()
        copy(0, sibling, me).wait_recv()
        for j, chip in enumerate(chips):
            copy(4 + j, (*chip, 1 - c), me).wait_recv()
        for cp in first + passed:
            cp.wait_send()
        mine.wait()

    return pl.pallas_call(
        body, name="allgather_rows", in_specs=[pl.BlockSpec(memory_space=pltpu.VMEM)],
        out_specs=pl.BlockSpec(memory_space=pltpu.VMEM), out_shape=_sds((N_DEV, r, n), v.dtype),
        scratch_shapes=[pltpu.SemaphoreType.DMA((7,)), pltpu.SemaphoreType.DMA((7,)), pltpu.SemaphoreType.DMA],
    )(v)


def _swap_other_half(g):
    nb, r, n = g.shape
    rh = r // 2

    def body(g_ref, out_ref, send, recv):
        x, y, c = _mesh_pos()
        cp = _remote(g_ref.at[:, pl.ds((1 - c) * rh, rh), :], out_ref, send, recv, (x, y, 1 - c))
        cp.start()
        cp.wait()

    return pl.pallas_call(
        body, name="swap_other_half", in_specs=[ANY], out_specs=ANY, out_shape=_sds((nb, rh, n), g.dtype),
        scratch_shapes=[pltpu.SemaphoreType.DMA, pltpu.SemaphoreType.DMA],
    )(g)


def _scatter_to_chips(p):
    def body(p_ref, out_ref, send, recv):
        x, y, c = _mesh_pos()
        me = 2 * x + y
        chips = _other_chips(x, y)
        sends = [_remote(p_ref.at[2 * chip[0] + chip[1]], out_ref.at[me], send.at[j], recv.at[j], (*chip, c))
                 for j, chip in enumerate(chips)]
        for cp in sends:
            cp.start()
        for j, chip in enumerate(chips):
            kj = 2 * chip[0] + chip[1]
            _remote(p_ref.at[kj], out_ref.at[kj], send.at[j], recv.at[j], (*chip, c)).wait_recv()
        for cp in sends:
            cp.wait_send()

    got = pl.pallas_call(
        body, name="scatter_to_chips", in_specs=[ANY], out_specs=ANY, out_shape=_sds(p.shape, p.dtype),
        scratch_shapes=[pltpu.SemaphoreType.DMA((3,)), pltpu.SemaphoreType.DMA((3,))],
    )(p)
    xp, yp, _ = _mesh_pos()
    me = 2 * xp + yp
    return _set_block(got, lax.dynamic_index_in_dim(p, me, axis=0, keepdims=False), me)


def _join_halves(f):
    def body(f_ref, out_ref, send, recv):
        x, y, c = _mesh_pos()
        cp = _remote(f_ref, out_ref, send, recv, (x, y, 1 - c))
        cp.start()
        cp.wait()

    other = pl.pallas_call(
        body, name="join_halves", in_specs=[ANY], out_specs=ANY, out_shape=_sds(f.shape, f.dtype),
        scratch_shapes=[pltpu.SemaphoreType.DMA, pltpu.SemaphoreType.DMA],
    )(f)
    first = lax.axis_index("c") == 0
    return jnp.concatenate([jnp.where(first, f, other), jnp.where(first, other, f)], axis=0)


def _row_tile(rows, cap=512):
    t = cap - cap % 8
    while rows % t:
        t -= 8
    return t


def _add_blocks(a, b, out_dtype):
    nb, r, n = a.shape
    tr = _row_tile(r)

    def body(a_ref, b_ref, o_ref):
        o_ref[...] = (a_ref[...] + b_ref[...]).astype(out_dtype)

    spec = pl.BlockSpec((None, tr, n), lambda k, i: (k, i, 0))
    return _call(body, name="add_blocks", grid=(nb, r // tr), in_specs=[spec, spec], out_specs=spec,
                 out_shape=_sds(a.shape, out_dtype), sem=("parallel", "parallel"))(a, b)


def _sum_leading(a, *, name):
    nk, r, n = a.shape
    tr = _row_tile(r)

    def body(a_ref, o_ref):
        acc = a_ref[0].astype(F32)
        for k in range(1, nk):
            acc = acc + a_ref[k].astype(F32)
        o_ref[...] = acc

    return _call(body, name=name, grid=(r // tr,), in_specs=[pl.BlockSpec((nk, tr, n), lambda i: (0, i, 0))],
                 out_specs=pl.BlockSpec((tr, n), lambda i: (i, 0)), out_shape=_sds((r, n)), sem=("parallel",))(a)


def _silu_outer(a, b):
    kdim, n = a.shape[1], b.shape[1]

    def body(a_ref, b_ref, o_ref):
        av = a_ref[...]
        av = av * _sigmoid(av)
        bv = b_ref[...]
        ah, bh = av.astype(BF16), bv.astype(BF16)
        al, bl = (av - ah.astype(F32)).astype(BF16), (bv - bh.astype(F32)).astype(BF16)
        o_ref[...] = _dot_tn(ah, bh) + (_dot_tn(ah, bl) + _dot_tn(al, bh))

    return _call(body, name="dw_mod", grid=(1,), in_specs=[_full(a.shape), _full(b.shape)], out_specs=_full((kdim, n)),
                 out_shape=_sds((kdim, n)))(a, b)


def _adamw(w, g, m, v):
    r, n = w.shape
    tr = _row_tile(r)

    def body(w_ref, g_ref, m_ref, v_ref, d_ref, nm_ref, nv_ref):
        gv = g_ref[...]
        nm = ADAM_B1 * m_ref[...] + (1.0 - ADAM_B1) * gv
        nv = ADAM_B2 * v_ref[...] + (1.0 - ADAM_B2) * (gv * gv)
        m_hat = nm / (1.0 - ADAM_B1 ** ADAM_STEP)
        v_hat = nv / (1.0 - ADAM_B2 ** ADAM_STEP)
        d_ref[...] = -ADAM_LR * (m_hat / (jnp.sqrt(v_hat) + ADAM_EPS) + ADAM_WD * w_ref[...])
        nm_ref[...] = nm
        nv_ref[...] = nv

    spec = pl.BlockSpec((tr, n), lambda i: (i, 0))
    return _call(body, name="adamw", grid=(r // tr,), in_specs=[spec] * 4, out_specs=[spec] * 3,
                 out_shape=[_sds((r, n))] * 3, sem=("parallel",))(w, g, m, v)


BIG = ("w_in", "w_branch_a", "w_branch_b", "w_out", "w_up", "w_down", "conv_w")
BIG_ROWS = 3584
MATRICES = ("w_mod", "w_in", "w_branch_a", "w_branch_b", "w_out", "w_up", "w_down")
SMALL = ("b_mod", "b_in", "conv_b", "ln1_g", "ln1_b", "ln2_g", "ln2_b", "c_ctx", "attn_sink", "q_norm_g", "k_norm_g", "conv_w")
SMALL_ROWS = 8 * len(SMALL)


def _rows(a, n_rows):
    flat = a.reshape(-1)
    return jnp.pad(flat, (0, n_rows * D_MODEL - flat.shape[0])).reshape(n_rows, D_MODEL)


def _group8(a):
    return _rep8(_rows(a, 1)) if a.size <= D_MODEL else _rows(a, 8)


def _ungroup8(p, shape):
    size = math.prod(shape)
    return (p[0, :size] if size <= D_MODEL else p.reshape(-1)[:size]).reshape(shape)


def _unpack_big(p, like):
    out, r = {}, 0
    for n in BIG:
        size = math.prod(like[n].shape)
        nr = size // D_MODEL if n != "conv_w" else 8
        out[n] = p[r:r + nr].reshape(-1)[:size].reshape(like[n].shape)
        r += nr
    return out


def _pack_small(t):
    return jnp.concatenate([_group8(t[n]) for n in SMALL], axis=0)


def _unpack_small(p, like):
    return {n: _ungroup8(p[8 * i:8 * i + 8], like[n].shape) for i, n in enumerate(SMALL)}


WEIGHTS = ("c_ctx", "w_mod", "b_mod", "w_in", "b_in", "attn_sink", "q_norm_g", "k_norm_g", "w_branch_a", "w_branch_b",
           "w_out", "ln1_g", "ln1_b", "w_up", "conv_w", "conv_b", "w_down", "ln2_g", "ln2_b")


def kernel(x, c, ctx, c_ctx, w_mod, b_mod, w_in, b_in, attn_sink, q_norm_g, k_norm_g, w_branch_a, w_branch_b, w_out, ln1_g, ln1_b, w_up, conv_w, conv_b, w_down, ln2_g, ln2_b, loss_target, m_c_ctx, m_w_mod, m_b_mod, m_w_in, m_b_in, m_attn_sink, m_q_norm_g, m_k_norm_g, m_w_branch_a, m_w_branch_b, m_w_out, m_ln1_g, m_ln1_b, m_w_up, m_conv_w, m_conv_b, m_w_down, m_ln2_g, m_ln2_b, v_c_ctx, v_w_mod, v_b_mod, v_w_in, v_b_in, v_attn_sink, v_q_norm_g, v_k_norm_g, v_w_branch_a, v_w_branch_b, v_w_out, v_ln1_g, v_ln1_b, v_w_up, v_conv_w, v_conv_b, v_w_down, v_ln2_g, v_ln2_b):
    w = dict(c_ctx=c_ctx, w_mod=w_mod, b_mod=b_mod, w_in=w_in, b_in=b_in, attn_sink=attn_sink, q_norm_g=q_norm_g,
             k_norm_g=k_norm_g, w_branch_a=w_branch_a, w_branch_b=w_branch_b, w_out=w_out, ln1_g=ln1_g, ln1_b=ln1_b,
             w_up=w_up, conv_w=conv_w, conv_b=conv_b, w_down=w_down, ln2_g=ln2_g, ln2_b=ln2_b)
    m = dict(c_ctx=m_c_ctx, w_mod=m_w_mod, b_mod=m_b_mod, w_in=m_w_in, b_in=m_b_in, attn_sink=m_attn_sink,
             q_norm_g=m_q_norm_g, k_norm_g=m_k_norm_g, w_branch_a=m_w_branch_a, w_branch_b=m_w_branch_b, w_out=m_w_out,
             ln1_g=m_ln1_g, ln1_b=m_ln1_b, w_up=m_w_up, conv_w=m_conv_w, conv_b=m_conv_b, w_down=m_w_down,
             ln2_g=m_ln2_g, ln2_b=m_ln2_b)
    v = dict(c_ctx=v_c_ctx, w_mod=v_w_mod, b_mod=v_b_mod, w_in=v_w_in, b_in=v_b_in, attn_sink=v_attn_sink,
             q_norm_g=v_q_norm_g, k_norm_g=v_k_norm_g, w_branch_a=v_w_branch_a, w_branch_b=v_w_branch_b, w_out=v_w_out,
             ln1_g=v_ln1_g, ln1_b=v_ln1_b, w_up=v_w_up, conv_w=v_conv_w, conv_b=v_conv_b, w_down=v_w_down,
             ln2_g=v_ln2_g, ln2_b=v_ln2_b)
    xp, yp, _ = _mesh_pos()
    me = 2 * xp + yp

    branches = jnp.concatenate([w_branch_a[0], w_branch_b[0]], axis=0)
    wide = jnp.concatenate([w_mod[0], w_in[0], w_up[0], branches], axis=1).astype(BF16)
    tall = jnp.concatenate([w_out[0], w_down[0]], axis=0).astype(BF16)
    wide4, tall4, cw4 = _gather_shards([wide, tall], conv_w[0])
    n_mod, n_in, n_up = w_mod.shape[-1], w_in.shape[-1], w_up.shape[-1]
    wmod4 = wide4[:, :, :n_mod]
    win4 = wide4[:, :, n_mod:n_mod + n_in]
    wup4 = wide4[:, :, n_mod + n_in:n_mod + n_in + n_up]
    br4 = wide4[:, :, n_mod + n_in + n_up:]
    n_br = w_branch_a.shape[1]
    wba = br4[:, :n_br].transpose(1, 0, 2).reshape(n_br, D_MODEL)
    wbb = br4[:, n_br:].transpose(1, 0, 2).reshape(n_br, D_MODEL)
    n_out = w_out.shape[1]
    w_out_full = tall4[:, :n_out].reshape(D_MODEL, D_MODEL)
    w_down_full = tall4[:, n_out:].reshape(D_FF, D_MODEL)
    cw_full = cw4.transpose(1, 0, 2).reshape(3, 2 * D_FF)

    loss, grad_x, g = _local_step(
        x[0], c, ctx[0], c_ctx[None], wmod4, b_mod, win4, b_in, attn_sink[0], q_norm_g, k_norm_g, wba, wbb, w_out_full,
        ln1_g, ln1_b, wup4, cw_full, conv_b, w_down_full, ln2_g, ln2_b, loss_target[0])
    loss = lax.psum(loss, ("x", "y", "c"))

    sent = dict(c=c, dmod=g["dmod"], dmodc=g["dmodc"], b_in=g["b_in"], conv_b=g["conv_b"], ln1_g=g["ln1_g"],
                ln1_b=g["ln1_b"], ln2_g=g["ln2_g"], ln2_b=g["ln2_b"], c_ctx=g["c_ctx"], attn_sink=g["sink"],
                q_norm_g=g["qn"], k_norm_g=g["kn"])
    every = _allgather_rows(jnp.concatenate([_group8(a) for a in sent.values()], axis=0))
    total = _sum_leading(every, name="sum_devices")
    slot = {n: slice(8 * i, 8 * i + 8) for i, n in enumerate(sent)}
    gs = {n: _ungroup8(total[slot[n]], sent[n].shape) for n in SMALL if n in sent}
    dmodc_sum = jnp.concatenate([_ungroup8(total[slot["dmodc"]], (1, 2 * D_MODEL)), jnp.zeros((1, 4 * D_MODEL), F32)],
                                axis=1)
    gs["b_mod"] = _ungroup8(total[slot["dmod"]], b_mod.shape) + dmodc_sum
    acts = jnp.concatenate([every[:, slot["c"].start], _rep8(c_ctx)], axis=0)
    dmods = jnp.concatenate([every[:, slot["dmod"]].reshape(N_DEV, -1)[:, :6 * D_MODEL], _first_row(dmodc_sum)], axis=0)
    g_w_mod = _silu_outer(acts, lax.dynamic_slice_in_dim(dmods, me * n_mod, n_mod, axis=1))

    cw_g4 = _to_blocks4(g["conv_w"])
    parts = [
        g["w_in4"].reshape(N_CHIPS, -1, D_MODEL), _to_blocks4(g["wba"]).reshape(N_CHIPS, -1, D_MODEL),
        _to_blocks4(g["wbb"]).reshape(N_CHIPS, -1, D_MODEL), g["w_out"].reshape(N_CHIPS, -1, D_MODEL),
        g["w_up4"].reshape(N_CHIPS, -1, D_MODEL), g["w_down"].reshape(N_CHIPS, -1, D_MODEL),
        jnp.pad(cw_g4.reshape(N_CHIPS, -1), ((0, 0), (0, 8 * D_MODEL - cw_g4.shape[1] * cw_g4.shape[2]))).reshape(
            N_CHIPS, 8, D_MODEL)]
    used = sum(p.shape[1] for p in parts)
    packed = jnp.concatenate(parts + [jnp.zeros((N_CHIPS, BIG_ROWS - used, D_MODEL), F32)], axis=1)
    rh = BIG_ROWS // 2
    cpos = lax.axis_index("c")
    my_half = lax.dynamic_slice_in_dim(packed, cpos * rh, rh, axis=1)
    chip_sum = _add_blocks(my_half, _swap_other_half(packed), BF16)
    half_sum = _sum_leading(_scatter_to_chips(chip_sum), name="sum_chips")
    g_big = _unpack_big(_join_halves(half_sum), w)

    grads = dict(gs, w_mod=g_w_mod, **g_big)
    grads = {n: grads[n].reshape(w[n].shape) for n in WEIGHTS}
    delta, new_m, new_v = {}, {}, {}
    for n in MATRICES:
        outs = _adamw(*[t[n][0] for t in (w, grads, m, v)])
        delta[n], new_m[n], new_v[n] = [o[None] for o in outs]
    outs = _adamw(*[_pack_small(t) for t in (w, grads, m, v)])
    for res, o in zip((delta, new_m, new_v), outs):
        res.update(_unpack_small(o, w))
    return (loss, grad_x[None], *[grads[n] for n in WEIGHTS], *[delta[n] for n in WEIGHTS],
            *[new_m[n] for n in WEIGHTS], *[new_v[n] for n in WEIGHTS])
```

```python
import functools
import math

import jax
import jax.numpy as jnp
from jax import lax
from jax.experimental import pallas as pl
from jax.experimental.pallas import tpu as pltpu

F32 = jnp.float32
BF16 = jnp.bfloat16

D_MODEL = 1024
HEAD_DIM = 64
N_HEADS = 8
N_KV = 2
WINDOW = 128
GRID_W = 64
ROPE_THETA = 10000.0
D_FF = 2816
LN_EPS = 1e-5
QK_EPS = 1e-6
ALPHA = 2.0 ** 0.25
Q_SCALE = HEAD_DIM ** -0.5
OFF_GA = 1536
IN_COLS = 3584
ADAM_LR, ADAM_B1, ADAM_B2, ADAM_EPS, ADAM_WD, ADAM_STEP = 0.001, 0.9, 0.999, 1e-8, 0.01, 10

LANES = 128
VMEM_BUDGET = 52 * 1024 * 1024
N_CHIPS = 4
N_DEV = 8
NEG = -1e30
MESH = pl.DeviceIdType.MESH


def _sigmoid(x):
    return 1.0 / (1.0 + jnp.exp(-x))


def _dot(a, b):
    return jnp.dot(a, b, preferred_element_type=F32)


def _dot_nt(a, b):
    return lax.dot_general(a, b, (((1,), (1,)), ((), ())), preferred_element_type=F32)


def _dot_tn(a, b):
    return lax.dot_general(a, b, (((0,), (0,)), ((), ())), preferred_element_type=F32)


def _call(body, *, name, grid, in_specs, out_specs, out_shape, scratch=(), sem=None, **kw):
    params = dict(vmem_limit_bytes=VMEM_BUDGET)
    if sem is not None:
        params["dimension_semantics"] = sem
    return pl.pallas_call(body, name=name, grid=grid, in_specs=in_specs, out_specs=out_specs,
                          out_shape=out_shape, scratch_shapes=list(scratch),
                          compiler_params=pltpu.CompilerParams(**params), **kw)


def _full(shape):
    n = len(shape)
    return pl.BlockSpec(shape, lambda *_: (0,) * n)


def _sds(shape, dtype=F32):
    return jax.ShapeDtypeStruct(shape, dtype)


def _mm_nn4(a, shift, scale, w4, bias, *, mode, split_out, out_dtype, tm, name):
    m, kdim = a.shape
    nb, _, ns = w4.shape

    def body(a_ref, sh_ref, sc_ref, w_ref, b_ref, o_ref):
        av = a_ref[...]
        if mode == "modulate":
            av = av * (1.0 + sc_ref[...]) + sh_ref[...]
        else:
            av = av * _sigmoid(av)
        o_ref[...] = (_dot(av.astype(BF16), w_ref[...]) + b_ref[...]).astype(out_dtype)

    if split_out:
        out_shape = _sds((2, m, 2 * ns), out_dtype)
        out_spec = pl.BlockSpec((None, tm, ns), lambda i, k: (k // 2, i, k % 2))
    else:
        out_shape = _sds((m, nb * ns), out_dtype)
        out_spec = pl.BlockSpec((tm, ns), lambda i, k: (i, k))
    return _call(
        body, name=name, grid=(m // tm, nb),
        in_specs=[pl.BlockSpec((tm, kdim), lambda i, k: (i, 0)),
                  pl.BlockSpec((1, kdim), lambda i, k: (0, 0)),
                  pl.BlockSpec((1, kdim), lambda i, k: (0, 0)),
                  pl.BlockSpec((None, kdim, ns), lambda i, k: (k, 0, 0)),
                  pl.BlockSpec((1, ns), lambda i, k: (0, k))],
        out_specs=out_spec, out_shape=out_shape, sem=("parallel", "arbitrary"),
    )(a, shift, scale, w4, bias)


def _mm_tn(a, b, *, a_spec, b_spec, grid, out_shape, out_spec, name, mod=None, init=None, colsum_spec=None,
           colsum_shape=None):
    red = len(grid) - 1
    has_mod, has_init, has_cs = mod is not None, init is not None, colsum_spec is not None

    def body(*refs):
        refs = list(refs)
        a_ref, b_ref = refs[0], refs[1]
        pos = 2
        if has_mod:
            sh_ref, sc_ref = refs[2], refs[3]
            pos = 4
        if has_init:
            init_ref = refs[pos]
            pos += 1
        o_ref = refs[pos]
        cs_ref = refs[pos + 1] if has_cs else None
        s = pl.program_id(red)

        @pl.when(s == 0)
        def _():
            o_ref[...] = init_ref[...] if has_init else jnp.zeros(o_ref.shape, F32)
            if has_cs:
                cs_ref[...] = jnp.zeros(cs_ref.shape, F32)

        av = a_ref[...]
        if has_mod:
            av = av * (1.0 + sc_ref[...]) + sh_ref[...]
        bv = b_ref[...]
        o_ref[...] += _dot_tn(av.astype(BF16), bv)
        if has_cs:
            cs_ref[...] += jnp.broadcast_to(jnp.sum(bv.astype(F32), axis=0, keepdims=True), cs_ref.shape)

    ins, in_specs = [a, b], [a_spec, b_spec]
    if has_mod:
        kdim = mod[0].shape[-1]
        ins += list(mod)
        in_specs += [_full((1, kdim)), _full((1, kdim))]
    if has_init:
        ins.append(init)
        in_specs.append(out_spec)
    out_specs, out_shapes = out_spec, out_shape
    if has_cs:
        out_specs, out_shapes = [out_spec, colsum_spec], [out_shape, colsum_shape]
    sem = ("parallel",) * red + ("arbitrary",)
    return _call(body, name=name, grid=grid, in_specs=in_specs, out_specs=out_specs, out_shape=out_shapes,
                 sem=sem)(*ins)


def _rope_tables(n_tok):
    pos = jnp.arange(n_tok, dtype=jnp.int32)
    rows = (pos // GRID_W).astype(F32)
    cols = (pos % GRID_W).astype(F32)
    n_freq = HEAD_DIM // 4
    inv_freq = ROPE_THETA ** (-jnp.arange(n_freq, dtype=F32) / n_freq)
    ang_r = rows[:, None] * inv_freq
    ang_c = cols[:, None] * inv_freq
    cos = jnp.concatenate([jnp.cos(ang_r)] * 2 + [jnp.cos(ang_c)] * 2, axis=-1)
    sin = jnp.concatenate([-jnp.sin(ang_r), jnp.sin(ang_r), -jnp.sin(ang_c), jnp.sin(ang_c)], axis=-1)
    return jnp.tile(cos, (1, 2)), jnp.tile(sin, (1, 2))


def _lane(shape):
    return lax.broadcasted_iota(jnp.int32, shape, 1)


def _rope_partner(t, lane):
    return jnp.where((lane % 32) < 16, pltpu.roll(t, LANES - 16, 1), pltpu.roll(t, 16, 1))


def _half_mean(s, lane):
    lo = jnp.sum(jnp.where(lane < HEAD_DIM, s, 0.0), axis=-1, keepdims=True)
    hi = jnp.sum(jnp.where(lane < HEAD_DIM, 0.0, s), axis=-1, keepdims=True)
    return jnp.where(lane < HEAD_DIM, lo, hi) * (1.0 / HEAD_DIM)


def _prep(proj, cos, sin, qg, kg, *, tm, name):
    m = proj.shape[0]

    def body(p_ref, cos_ref, sin_ref, qg_ref, kg_ref, qa_ref, ka_ref, va_ref, qb_ref, kb_ref, vb_ref):
        lane = _lane((tm, LANES))
        cosv, sinv = cos_ref[...], sin_ref[...]
        low = lane < HEAD_DIM

        def rope(t):
            return t * cosv + _rope_partner(t, lane) * sinv

        def rms(t, g):
            return t * lax.rsqrt(_half_mean(t * t, lane) + QK_EPS) * g

        def place(q_ref, j, chunk):
            sw = pltpu.roll(chunk, HEAD_DIM, 1)
            if j < 2:
                h0, h1 = jnp.where(low, chunk, 0.0), jnp.where(low, sw, 0.0)
            else:
                h0, h1 = jnp.where(low, 0.0, sw), jnp.where(low, 0.0, chunk)
            q_ref[2 * j] = h0.T.astype(BF16)
            q_ref[2 * j + 1] = h1.T.astype(BF16)

        for j in range(4):
            place(qa_ref, j, rope(p_ref[:, j * LANES:(j + 1) * LANES]) * Q_SCALE)
            place(qb_ref, j, rope(rms(p_ref[:, 768 + j * LANES:768 + (j + 1) * LANES], qg_ref[...])) * Q_SCALE)
        ka_ref[...] = rope(p_ref[:, 512:640]).astype(BF16)
        va_ref[...] = p_ref[:, 640:768].astype(BF16)
        kb_ref[...] = rope(rms(p_ref[:, 1280:1408], kg_ref[...])).astype(BF16)
        vb_ref[...] = p_ref[:, 1408:1536].astype(BF16)

    row = pl.BlockSpec((tm, LANES), lambda i: (i, 0))
    qspec = pl.BlockSpec((N_HEADS, LANES, tm), lambda i: (0, 0, i))
    return _call(
        body, name=name, grid=(m // tm,),
        in_specs=[pl.BlockSpec((tm, OFF_GA), lambda i: (i, 0)), row, row, _full((1, LANES)), _full((1, LANES))],
        out_specs=[qspec, row, row, qspec, row, row],
        out_shape=[_sds((N_HEADS, LANES, m), BF16), _sds((m, LANES), BF16), _sds((m, LANES), BF16),
                   _sds((N_HEADS, LANES, m), BF16), _sds((m, LANES), BF16), _sds((m, LANES), BF16)],
        sem=("parallel",),
    )(proj, cos, sin, qg, kg)


def _prep_bwd(dqa, dka, dva, dqb, dkb, dvb, proj, cos, sin, qg, kg, dgl, *, tm, name):
    m = proj.shape[0]

    def body(dqa_ref, dka_ref, dva_ref, dqb_ref, dkb_ref, dvb_ref, p_ref, cos_ref, sin_ref, qg_ref, kg_ref,
             dgl_ref, dp_ref, dqg_ref, dkg_ref):
        i = pl.program_id(0)
        lane = _lane((tm, LANES))
        cosv, sinv = cos_ref[...], sin_ref[...]
        low = lane < HEAD_DIM

        @pl.when(i == 0)
        def _():
            dqg_ref[...] = jnp.zeros(dqg_ref.shape, F32)
            dkg_ref[...] = jnp.zeros(dkg_ref.shape, F32)

        def unrope(d):
            return d * cosv - _rope_partner(d, lane) * sinv

        def unplace(dq_ref, j):
            d0, d1 = dq_ref[2 * j].T, dq_ref[2 * j + 1].T
            if j < 2:
                return jnp.where(low, d0, pltpu.roll(d1, HEAD_DIM, 1))
            return jnp.where(low, pltpu.roll(d0, HEAD_DIM, 1), d1)

        def unrms(dtn, t, g):
            r = lax.rsqrt(_half_mean(t * t, lane) + QK_EPS)
            u = dtn * g
            dt = r * u - t * (r * r * r) * _half_mean(u * t, lane)
            return dt, jnp.sum(dtn * t * r, axis=0, keepdims=True)

        for j in range(4):
            dp_ref[:, j * LANES:(j + 1) * LANES] = (unrope(unplace(dqa_ref, j)) * Q_SCALE).astype(BF16)
            c0 = 768 + j * LANES
            dt, dg = unrms(unrope(unplace(dqb_ref, j)) * Q_SCALE, p_ref[:, c0:c0 + LANES], qg_ref[...])
            dp_ref[:, c0:c0 + LANES] = dt.astype(BF16)
            dqg_ref[:, j * LANES:(j + 1) * LANES] += dg
        dp_ref[:, 512:640] = unrope(dka_ref[...]).astype(BF16)
        dp_ref[:, 640:768] = dva_ref[...].astype(BF16)
        dt, dg = unrms(unrope(dkb_ref[...]), p_ref[:, 1280:1408], kg_ref[...])
        dp_ref[:, 1280:1408] = dt.astype(BF16)
        dkg_ref[...] += dg
        dp_ref[:, 1408:1536] = dvb_ref[...].astype(BF16)
        dp_ref[:, OFF_GA:] = dgl_ref[...]

    row = pl.BlockSpec((tm, LANES), lambda i: (i, 0))
    qspec = pl.BlockSpec((N_HEADS, LANES, tm), lambda i: (0, 0, i))
    return _call(
        body, name=name, grid=(m // tm,),
        in_specs=[qspec, row, row, qspec, row, row, pl.BlockSpec((tm, OFF_GA), lambda i: (i, 0)), row, row,
                  _full((1, LANES)), _full((1, LANES)), pl.BlockSpec((tm, IN_COLS - OFF_GA), lambda i: (i, 0))],
        out_specs=[pl.BlockSpec((tm, IN_COLS), lambda i: (i, 0)), _full((1, 512)), _full((1, LANES))],
        out_shape=[_sds((m, IN_COLS), BF16), _sds((1, 512)), _sds((1, LANES))],
        sem=("arbitrary",),
    )(dqa, dka, dva, dqb, dkb, dvb, proj, cos, sin, qg, kg, dgl)


def _attn_glob_fwd(qt, k, vt, kc, vct, *, tq, tk):
    nh, _, s = qt.shape
    nc = kc.shape[0]
    n_chunks = s // tk
    half = LANES // 2

    def body(qt_ref, qn_ref, k_ref, vt_ref, kc_ref, vct_ref, ot_ref, lse_ref, mrun_ref, p_hbm,
             acc_sc, st_sc, stage_sc, stagec_sc, sems, semc):
        h, i = pl.program_id(0), pl.program_id(1)
        qtv = qt_ref[...]
        acc_sc[...] = jnp.zeros(acc_sc.shape, F32)

        def p_out(slot, c):
            return pltpu.make_async_copy(stage_sc.at[slot], p_hbm.at[h, i, pl.ds(pl.multiple_of(c * tk, tk), tk), :],
                                         sems.at[slot])

        def update(st, vtv, m_old):
            m_new = jnp.maximum(m_old, jnp.max(st, axis=0, keepdims=True))
            pb = jnp.exp(st - m_new).astype(BF16)
            acc_sc[...] = acc_sc[...] * jnp.exp(m_old - m_new) + _dot(vtv, pb)
            return m_new, pb

        m, pbc = update(_dot(kc_ref[...], qtv), vct_ref[...], jnp.full((1, tq), NEG, F32))
        mrun_ref[pl.ds(n_chunks, 1), :] = m
        stagec_sc[...] = pbc
        ctx_out = pltpu.make_async_copy(stagec_sc, p_hbm.at[h, i, pl.ds(s, nc), :], semc)
        ctx_out.start()

        def step(c, m_old, k_next, q_next):
            slot = c % 2
            off = pl.multiple_of(c * tk, tk)
            st = st_sc[...]
            st_next = _dot(k_next, q_next)
            m_new, pb = update(st, vt_ref[:, pl.ds(off, tk)], m_old)
            mrun_ref[pl.ds(c, 1), :] = m_new
            stage_sc[slot] = pb
            p_out(slot, c).start()
            p_out(1 - slot, c - 1).wait()
            st_sc[...] = st_next
            return m_new

        def loop(c, m_old):
            return step(c, m_old, k_ref[pl.ds(pl.multiple_of((c + 1) * tk, tk), tk), :], qtv)

        stage_sc[1] = jnp.zeros((tk, tq), BF16)
        pltpu.make_async_copy(stage_sc.at[1], p_hbm.at[h, i, pl.ds(s + nc, tk), :], sems.at[1]).start()

        @pl.when(jnp.logical_and(h == 0, i == 0))
        def _():
            st_sc[...] = _dot(k_ref[pl.ds(0, tk), :], qtv)

        m = lax.fori_loop(0, n_chunks - 1, loop, m)
        m = step(n_chunks - 1, m, k_ref[pl.ds(0, tk), :], qn_ref[...])
        p_out((n_chunks - 1) % 2, n_chunks - 1).wait()
        ctx_out.wait()
        acc = acc_sc[...]
        l = jnp.where(h < nh // N_KV, acc[half:half + 1], acc[0:1])
        ot_ref[...] = (acc / l).astype(BF16)
        lse_ref[...] = m + jnp.log(l)

    grp = nh // N_KV
    n_q = s // tq
    return _call(
        body, name="attn_glob_fwd", grid=(nh, n_q),
        in_specs=[pl.BlockSpec((None, LANES, tq), lambda h, i: (h, 0, i)),
                  pl.BlockSpec((None, LANES, tq), lambda h, i: (jnp.minimum(h + (i + 1) // n_q, nh - 1), 0, (i + 1) % n_q)),
                  _full((s, LANES)),
                  pl.BlockSpec((None, LANES, s), lambda h, i: (h // grp, 0, 0)), _full((nc, LANES)),
                  pl.BlockSpec((None, LANES, nc), lambda h, i: (h // grp, 0, 0))],
        out_specs=[pl.BlockSpec((None, LANES, tq), lambda h, i: (h, 0, i)),
                   pl.BlockSpec((None, 1, tq), lambda h, i: (h, 0, i)),
                   pl.BlockSpec((None, n_chunks + 1, tq), lambda h, i: (h, 0, i)), ANY],
        out_shape=[_sds((nh, LANES, s), BF16), _sds((nh, 1, s)), _sds((nh, n_chunks + 1, s)),
                   _sds((nh, s // tq, s + nc + tk, tq), BF16)],
        scratch=[pltpu.VMEM((LANES, tq), F32), pltpu.VMEM((tk, tq), F32), pltpu.VMEM((2, tk, tq), BF16),
                 pltpu.VMEM((nc, tq), BF16), pltpu.SemaphoreType.DMA((2,)), pltpu.SemaphoreType.DMA],
        sem=("arbitrary", "arbitrary"),
    )(qt, qt, k, vt, kc, vct)


P_AHEAD = 3


def _attn_glob_bwd(qt, dot, ot, lse, mrun, p, kt, v, kct, vc, *, tq, tk):
    nh, _, s = qt.shape
    nc = vc.shape[0]
    n_q = s // tq
    n_chunks = s // tk
    n_run = mrun.shape[1] - 1
    per_run = n_chunks // n_run

    def body(qt_ref, dot_ref, ot_ref, lse_ref, mrun_ref, p_hbm, kt_ref, v_ref, kct_ref, vc_ref,
             dqt_ref, dkt_ref, dvt_ref, dkct_ref, dvct_ref, acc_sc, dp_sc, dkt_sc, dvt_sc, p_sc, pc_sc, sems, semc):
        h, i = pl.program_id(0), pl.program_id(1)

        @pl.when(jnp.logical_and(h == 0, i == 0))
        def _():
            dkct_ref[...] = jnp.zeros(dkct_ref.shape, F32)
            dvct_ref[...] = jnp.zeros(dvct_ref.shape, F32)
            dkt_sc[...] = jnp.zeros(dkt_sc.shape, F32)
            dvt_sc[...] = jnp.zeros(dvt_sc.shape, F32)


        def p_in(slot, c):
            return pltpu.make_async_copy(p_hbm.at[h, i, pl.ds(pl.multiple_of(c * tk, tk), tk), :], p_sc.at[slot],
                                         sems.at[slot])

        ctx_in = pltpu.make_async_copy(p_hbm.at[h, i, pl.ds(s, nc), :], pc_sc, semc)
        ctx_in.start()
        for c in range(P_AHEAD):
            p_in(c, min(c, n_chunks - 1)).start()
        qtv, dotv, lse = qt_ref[...], dot_ref[...], lse_ref[...]
        delta = jnp.sum(dotv.astype(F32) * ot_ref[...].astype(F32), axis=0, keepdims=True)

        def grads(pt_stored, m_row, dpt):
            pt = pt_stored.astype(F32) * jnp.exp(m_row - lse)
            return pt.astype(BF16), (pt * (dpt - delta)).astype(BF16)

        dp_sc[...] = _dot(v_ref[pl.ds(0, tk), :], dotv)
        ctx_in.wait()
        pb, dsb = grads(pc_sc[...], mrun_ref[pl.ds(n_run, 1), :], _dot(vc_ref[...], dotv))
        acc_sc[...] = _dot(kct_ref[...], dsb)
        dkct_ref[...] += _dot_nt(qtv, dsb)
        dvct_ref[...] += _dot_nt(dotv, pb)

        def loop(c, carry):
            slot = c % (P_AHEAD + 1)
            off = pl.multiple_of(c * tk, tk)
            nxt = pl.multiple_of(jnp.minimum(c + 1, n_chunks - 1) * tk, tk)
            p_in(slot, c).wait()
            p_in((c + P_AHEAD) % (P_AHEAD + 1), jnp.minimum(c + P_AHEAD, n_chunks - 1)).start()
            dpt = dp_sc[...]
            dp_next = _dot(v_ref[pl.ds(nxt, tk), :], dotv)
            pb, dsb = grads(p_sc[slot], mrun_ref[pl.ds(c // per_run, 1), :], dpt)
            acc_sc[...] += _dot(kt_ref[:, pl.ds(off, tk)], dsb)
            dkt_sc[:, pl.ds(off, tk)] += _dot_nt(qtv, dsb)
            dvt_sc[:, pl.ds(off, tk)] += _dot_nt(dotv, pb)
            dp_sc[...] = dp_next
            return carry

        lax.fori_loop(0, n_chunks, loop, 0)
        for c in range(n_chunks, n_chunks + P_AHEAD):
            p_in(c % (P_AHEAD + 1), n_chunks - 1).wait()
        dqt_ref[...] = acc_sc[...]

        @pl.when(jnp.logical_and(h == nh - 1, i == n_q - 1))
        def _():
            pltpu.sync_copy(dkt_sc, dkt_ref)
            pltpu.sync_copy(dvt_sc, dvt_ref)

    qs = pl.BlockSpec((None, LANES, tq), lambda h, i: (h, 0, i))
    rs = pl.BlockSpec((None, 1, tq), lambda h, i: (h, 0, i))
    return _call(
        body, name="attn_glob_bwd", grid=(nh, n_q),
        in_specs=[qs, qs, qs, rs, pl.BlockSpec((None, n_run + 1, tq), lambda h, i: (h, 0, i)), ANY,
                  _full((LANES, s)), _full((s, LANES)), _full((LANES, nc)), _full((nc, LANES))],
        out_specs=[qs, ANY, ANY, _full((LANES, nc)), _full((LANES, nc))],
        out_shape=[_sds((nh, LANES, s)), _sds((LANES, s)), _sds((LANES, s)), _sds((LANES, nc)), _sds((LANES, nc))],
        scratch=[pltpu.VMEM((LANES, tq), F32), pltpu.VMEM((tk, tq), F32), pltpu.VMEM((LANES, s), F32),
                 pltpu.VMEM((LANES, s), F32), pltpu.VMEM((P_AHEAD + 1, tk, tq), BF16), pltpu.VMEM((nc, tq), BF16),
                 pltpu.SemaphoreType.DMA((P_AHEAD + 1,)), pltpu.SemaphoreType.DMA],
        sem=("arbitrary", "arbitrary"),
    )(qt, dot, ot, lse, mrun, p, kt, v, kct, vc)


WIN_SPAN = 2 * WINDOW


def _band(rows0, cols0, shape):
    r = rows0 + lax.broadcasted_iota(jnp.int32, shape, 0)
    c = cols0 + lax.broadcasted_iota(jnp.int32, shape, 1)
    return jnp.abs(r - c) <= WINDOW


def _win_start(blk, t, s):
    return pl.multiple_of(jnp.clip(blk * t - WINDOW, 0, s - t - WIN_SPAN), WINDOW)


def _attn_win_fwd(qt, k, vt, kc, vct, sink, *, tq):
    nh, _, s = qt.shape
    nc = kc.shape[0]
    tw = tq + WIN_SPAN
    half = LANES // 2
    grp = nh // N_KV

    def body(sink_ref, qt_ref, k_ref, vt_ref, kc_ref, vct_ref, ot_ref, lse_ref):
        h, i = pl.program_id(0), pl.program_id(1)
        k0 = _win_start(i, tq, s)
        qtv = qt_ref[...]
        st = jnp.where(_band(k0, i * tq, (tw, tq)), _dot(k_ref[pl.ds(k0, tw), :], qtv), NEG)
        stc = _dot(kc_ref[...], qtv)
        snk = sink_ref[h]
        m = jnp.maximum(jnp.maximum(jnp.max(st, axis=0, keepdims=True), jnp.max(stc, axis=0, keepdims=True)), snk)
        acc = (_dot(vt_ref[:, pl.ds(k0, tw)], jnp.exp(st - m).astype(BF16))
               + _dot(vct_ref[...], jnp.exp(stc - m).astype(BF16)))
        l = jnp.where(h < grp, acc[half:half + 1], acc[0:1]) + jnp.exp(snk - m)
        ot_ref[...] = (acc / l).astype(BF16)
        lse_ref[...] = m + jnp.log(l)

    return _call(
        body, name="attn_win_fwd", grid=(nh, s // tq),
        in_specs=[pl.BlockSpec(memory_space=pltpu.SMEM),
                  pl.BlockSpec((None, LANES, tq), lambda h, i: (h, 0, i)), _full((s, LANES)),
                  pl.BlockSpec((None, LANES, s), lambda h, i: (h // grp, 0, 0)), _full((nc, LANES)),
                  pl.BlockSpec((None, LANES, nc), lambda h, i: (h // grp, 0, 0))],
        out_specs=[pl.BlockSpec((None, LANES, tq), lambda h, i: (h, 0, i)),
                   pl.BlockSpec((None, 1, tq), lambda h, i: (h, 0, i))],
        out_shape=[_sds((nh, LANES, s), BF16), _sds((nh, 1, s))],
        sem=("parallel", "parallel"),
    )(sink, qt, k, vt, kc, vct)


def _attn_win_bwd(qt, dot, ot, lse, k, kt, v, kc, kct, vc, sink, *, tq):
    nh, _, s = qt.shape
    nc = kc.shape[0]
    tw = tq + WIN_SPAN
    nq = s // tq

    def body(sink_ref, qt_ref, dot_ref, ot_ref, lse_ref, k_ref, kt_ref, v_ref, kc_ref, kct_ref, vc_ref,
             dqt_ref, dkt_ref, dvt_ref, dkct_ref, dvct_ref, dsk_ref, dkt_sc, dvt_sc):
        h, i = pl.program_id(0), pl.program_id(1)

        @pl.when(jnp.logical_and(h == 0, i == 0))
        def _():
            dkct_ref[...] = jnp.zeros(dkct_ref.shape, F32)
            dvct_ref[...] = jnp.zeros(dvct_ref.shape, F32)
            dkt_sc[...] = jnp.zeros(dkt_sc.shape, F32)
            dvt_sc[...] = jnp.zeros(dvt_sc.shape, F32)

        k0 = _win_start(i, tq, s)
        span = pl.ds(k0, tw)
        qtv, dotv, lse = qt_ref[...], dot_ref[...], lse_ref[...]
        delta = jnp.sum(dotv.astype(F32) * ot_ref[...].astype(F32), axis=0, keepdims=True)
        pt = jnp.where(_band(k0, i * tq, (tw, tq)), jnp.exp(_dot(k_ref[span, :], qtv) - lse), 0.0)
        dsb = (pt * (_dot(v_ref[span, :], dotv) - delta)).astype(BF16)
        pct = jnp.exp(_dot(kc_ref[...], qtv) - lse)
        dscb = (pct * (_dot(vc_ref[...], dotv) - delta)).astype(BF16)
        dqt_ref[...] = _dot(kt_ref[:, span], dsb) + _dot(kct_ref[...], dscb)
        dkt_sc[:, span] += _dot_nt(qtv, dsb)
        dvt_sc[:, span] += _dot_nt(dotv, pt.astype(BF16))
        dkct_ref[...] += _dot_nt(qtv, dscb)
        dvct_ref[...] += _dot_nt(dotv, pct.astype(BF16))
        dsk = -jnp.sum(jnp.exp(sink_ref[h] - lse) * delta)
        dsk_ref[...] = jnp.full(dsk_ref.shape, dsk, F32)

        @pl.when(jnp.logical_and(h == nh - 1, i == nq - 1))
        def _():
            pltpu.sync_copy(dkt_sc, dkt_ref)
            pltpu.sync_copy(dvt_sc, dvt_ref)

    qs = pl.BlockSpec((None, LANES, tq), lambda h, i: (h, 0, i))
    rs = pl.BlockSpec((None, 1, tq), lambda h, i: (h, 0, i))
    return _call(
        body, name="attn_win_bwd", grid=(nh, nq),
        in_specs=[pl.BlockSpec(memory_space=pltpu.SMEM), qs, qs, qs, rs, _full((s, LANES)), _full((LANES, s)),
                  _full((s, LANES)), _full((nc, LANES)), _full((LANES, nc)), _full((nc, LANES))],
        out_specs=[qs, ANY, ANY, _full((LANES, nc)), _full((LANES, nc)),
                   pl.BlockSpec((None, None, 8, LANES), lambda h, i: (h, i, 0, 0))],
        out_shape=[_sds((nh, LANES, s)), _sds((LANES, s)), _sds((LANES, s)), _sds((LANES, nc)), _sds((LANES, nc)),
                   _sds((nh, nq, 8, LANES))],
        scratch=[pltpu.VMEM((LANES, s), F32), pltpu.VMEM((LANES, s), F32)],
        sem=("arbitrary", "arbitrary"),
    )(sink, qt, dot, ot, lse, k, kt, v, kc, kct, vc)


def _ln_fwd(z, g, b):
    mu = jnp.mean(z, axis=-1, keepdims=True)
    zc = z - mu
    r = lax.rsqrt(jnp.mean(zc * zc, axis=-1, keepdims=True) + LN_EPS)
    return zc * r * g + b, mu, r


def _ln_bwd(dy, xhat, r, g):
    dxh = dy * g
    return r * (dxh - jnp.mean(dxh, axis=-1, keepdims=True) - xhat * jnp.mean(dxh * xhat, axis=-1, keepdims=True))


def _heads_matmul(ot_ref, w_ref):
    acc = _dot_tn(ot_ref[0], w_ref[0])
    for h in range(1, N_HEADS):
        acc += _dot_tn(ot_ref[h], w_ref[h])
    return acc


def _gate_specs(tm):
    return [pl.BlockSpec((tm, 512), functools.partial(lambda i, b: (i, b), b=OFF_GA // 512 + b)) for b in range(4)]


def _merge_fwd(oat, obt, proj, x, gate1, wba, wbb, w_out, ln_g, ln_b, *, tm):
    s = x.shape[0]

    def body(oa_ref, ob_ref, g0, g1, g2, g3, x_ref, gt_ref, wba_ref, wbb_ref, wo_ref, lg_ref, lb_ref,
             x1_ref, y_ref, mu_ref, r_ref, pa_ref, pb_ref, mg_ref):
        ga = _sigmoid(jnp.concatenate([g0[...], g1[...]], axis=1))
        gb = _sigmoid(jnp.concatenate([g2[...], g3[...]], axis=1))
        pa, pb = _heads_matmul(oa_ref, wba_ref), _heads_matmul(ob_ref, wbb_ref)
        merged = (ga * pa + gb * pb).astype(BF16)
        y = _dot(merged, wo_ref[...])
        x1, mu, r = _ln_fwd(ALPHA * x_ref[...] + gt_ref[...] * y, lg_ref[...], lb_ref[...])
        x1_ref[...] = x1
        y_ref[...] = y
        mu_ref[...] = mu
        r_ref[...] = r
        pa_ref[...] = pa.astype(BF16)
        pb_ref[...] = pb.astype(BF16)
        mg_ref[...] = merged

    hts = pl.BlockSpec((N_HEADS, LANES, tm), lambda i: (0, 0, i))
    row = pl.BlockSpec((tm, D_MODEL), lambda i: (i, 0))
    col = pl.BlockSpec((tm, 1), lambda i: (i, 0))
    vec = _full((1, D_MODEL))
    wh = _full((N_HEADS, LANES, D_MODEL))
    return _call(
        body, name="merge_fwd", grid=(s // tm,),
        in_specs=[hts, hts, *_gate_specs(tm), row, vec, wh, wh, _full((D_MODEL, D_MODEL)), vec, vec],
        out_specs=[row, row, col, col, row, row, row],
        out_shape=[_sds((s, D_MODEL)), _sds((s, D_MODEL)), _sds((s, 1)), _sds((s, 1)), _sds((s, D_MODEL), BF16),
                   _sds((s, D_MODEL), BF16), _sds((s, D_MODEL), BF16)],
        sem=("parallel",),
    )(oat, obt, proj, proj, proj, proj, x, gate1, wba, wbb, w_out, ln_g, ln_b)


def _merge_bwd(dy, oat, obt, pa, pb, proj, wba, wbb, w_out, *, tm):
    s = dy.shape[0]

    def body(dy_ref, oat_ref, obt_ref, pa_ref, pb_ref, g0, g1, g2, g3, wba_ref, wbb_ref, wo_ref,
             dgl_ref, doat_ref, dobt_ref, dwa_ref, dwb_ref):
        @pl.when(pl.program_id(0) == 0)
        def _():
            dwa_ref[...] = jnp.zeros(dwa_ref.shape, F32)
            dwb_ref[...] = jnp.zeros(dwb_ref.shape, F32)

        dm = _dot_nt(dy_ref[...], wo_ref[...])
        ga = _sigmoid(jnp.concatenate([g0[...], g1[...]], axis=1))
        gb = _sigmoid(jnp.concatenate([g2[...], g3[...]], axis=1))
        pa, pb = pa_ref[...].astype(F32), pb_ref[...].astype(F32)
        dgl_ref[:, :D_MODEL] = (dm * pa * ga * (1.0 - ga)).astype(BF16)
        dgl_ref[:, D_MODEL:] = (dm * pb * gb * (1.0 - gb)).astype(BF16)
        dpa, dpb = (dm * ga).astype(BF16), (dm * gb).astype(BF16)
        for h in range(N_HEADS):
            doat_ref[h] = _dot_nt(wba_ref[h], dpa).astype(BF16)
            dobt_ref[h] = _dot_nt(wbb_ref[h], dpb).astype(BF16)
            dwa_ref[h] += _dot(oat_ref[h], dpa)
            dwb_ref[h] += _dot(obt_ref[h], dpb)

    hts = pl.BlockSpec((N_HEADS, LANES, tm), lambda i: (0, 0, i))
    row = pl.BlockSpec((tm, D_MODEL), lambda i: (i, 0))
    wh = _full((N_HEADS, LANES, D_MODEL))
    return _call(
        body, name="merge_bwd", grid=(s // tm,),
        in_specs=[row, hts, hts, row, row, *_gate_specs(tm), wh, wh, _full((D_MODEL, D_MODEL))],
        out_specs=[pl.BlockSpec((tm, 2 * D_MODEL), lambda i: (i, 0)), hts, hts, wh, wh],
        out_shape=[_sds((s, 2 * D_MODEL), BF16), _sds((N_HEADS, LANES, s), BF16), _sds((N_HEADS, LANES, s), BF16),
                   _sds((N_HEADS, LANES, D_MODEL)), _sds((N_HEADS, LANES, D_MODEL))],
        sem=("arbitrary",),
    )(dy, oat, obt, pa, pb, proj, proj, proj, proj, wba, wbb, w_out)


FF_TC = 256


def _shift_rows(t, prev_row, next_row):
    n = t.shape[0]
    r = lax.broadcasted_iota(jnp.int32, t.shape, 0)
    up = jnp.where(r == 0, prev_row, pltpu.roll(t, 1, 0))
    dn = jnp.where(r == n - 1, next_row, pltpu.roll(t, n - 1, 0))
    return up, dn


HALO = 16


def _halo_specs(tm, s, tc):
    nb = s // HALO
    main = pl.BlockSpec((2, tm, tc), lambda j, i: (0, i, j))
    prev = pl.BlockSpec((2, HALO, tc), lambda j, i: (0, jnp.maximum(i * (tm // HALO) - 1, 0), j))
    nxt = pl.BlockSpec((2, HALO, tc), lambda j, i: (0, jnp.minimum((i + 1) * (tm // HALO), nb - 1), j))
    return main, prev, nxt


def _halo_rows(prev_ref, next_ref, half, i, n_i):
    prev_row = jnp.where(i == 0, 0.0, prev_ref[half, HALO - 1:HALO, :].astype(F32))
    next_row = jnp.where(i == n_i - 1, 0.0, next_ref[half, 0:1, :].astype(F32))
    return prev_row, next_row


def _conv(t, prev_row, next_row, w, b):
    up, dn = _shift_rows(t, prev_row, next_row)
    return w[0:1, :] * up + w[1:2, :] * t + w[2:3, :] * dn + b


def _ffn_act_fwd(u, cw, cb, *, tm):
    _, s, ff = u.shape
    n_i = s // tm

    def body(u_ref, up_ref, un_ref, cw_ref, cb_ref, a_ref):
        i = pl.program_id(1)
        gc = _conv(u_ref[0].astype(F32), *_halo_rows(up_ref, un_ref, 0, i, n_i), cw_ref[0], cb_ref[0])
        vc = _conv(u_ref[1].astype(F32), *_halo_rows(up_ref, un_ref, 1, i, n_i), cw_ref[1], cb_ref[1])
        a_ref[...] = (gc * _sigmoid(gc) * vc).astype(BF16)

    main, prev, nxt = _halo_specs(tm, s, FF_TC)
    return _call(
        body, name="ffn_act_fwd", grid=(ff // FF_TC, n_i),
        in_specs=[main, prev, nxt, pl.BlockSpec((2, 3, FF_TC), lambda j, i: (0, 0, j)),
                  pl.BlockSpec((2, 1, FF_TC), lambda j, i: (0, 0, j))],
        out_specs=pl.BlockSpec((tm, FF_TC), lambda j, i: (i, j)),
        out_shape=_sds((s, ff), BF16), sem=("parallel", "parallel"),
    )(u, u, u, cw, cb)


def _ffn_act_bwd(dy2, w_down, u, cw, cb, *, tm):
    _, s, ff = u.shape
    n_i = s // tm

    def body(dy_ref, wd_ref, u_ref, up_ref, un_ref, cw_ref, cb_ref, dc_ref, dcw_ref, dcb_ref):
        i = pl.program_id(1)

        @pl.when(i == 0)
        def _():
            dcw_ref[...] = jnp.zeros(dcw_ref.shape, F32)
            dcb_ref[...] = jnp.zeros(dcb_ref.shape, F32)

        da = _dot_nt(dy_ref[...], wd_ref[...])
        ug, uv = u_ref[0].astype(F32), u_ref[1].astype(F32)
        ugp, ugn = _shift_rows(ug, *_halo_rows(up_ref, un_ref, 0, i, n_i))
        uvp, uvn = _shift_rows(uv, *_halo_rows(up_ref, un_ref, 1, i, n_i))
        wg, wv = cw_ref[0], cw_ref[1]
        gc = wg[0:1, :] * ugp + wg[1:2, :] * ug + wg[2:3, :] * ugn + cb_ref[0]
        vc = wv[0:1, :] * uvp + wv[1:2, :] * uv + wv[2:3, :] * uvn + cb_ref[1]
        sg = _sigmoid(gc)
        dg = da * vc * sg * (1.0 + gc * (1.0 - sg))
        dv = da * gc * sg
        dc_ref[0] = dg.astype(BF16)
        dc_ref[1] = dv.astype(BF16)
        for half, (d, taps) in enumerate(((dg, (ugp, ug, ugn)), (dv, (uvp, uv, uvn)))):
            for tap in range(3):
                dcw_ref[half, tap:tap + 1, :] += jnp.sum(d * taps[tap], axis=0, keepdims=True)
            dcb_ref[half] += jnp.sum(d, axis=0, keepdims=True)

    main, prev, nxt = _halo_specs(tm, s, FF_TC)
    return _call(
        body, name="ffn_act_bwd", grid=(ff // FF_TC, n_i),
        in_specs=[pl.BlockSpec((tm, D_MODEL), lambda j, i: (i, 0)), pl.BlockSpec((FF_TC, D_MODEL), lambda j, i: (j, 0)),
                  main, prev, nxt, pl.BlockSpec((2, 3, FF_TC), lambda j, i: (0, 0, j)),
                  pl.BlockSpec((2, 1, FF_TC), lambda j, i: (0, 0, j))],
        out_specs=[main, pl.BlockSpec((2, 3, FF_TC), lambda j, i: (0, 0, j)),
                   pl.BlockSpec((2, 1, FF_TC), lambda j, i: (0, 0, j))],
        out_shape=[_sds((2, s, ff), BF16), _sds((2, 3, ff)), _sds((2, 1, ff))],
        sem=("parallel", "arbitrary"),
    )(dy2, w_down, u, u, u, cw, cb)


def _conv_bwd_input(dc, cw, *, tm):
    _, s, ff = dc.shape
    n_i = s // tm

    def body(d_ref, dp_ref, dn_ref, cw_ref, du_ref):
        i = pl.program_id(1)
        for half in range(2):
            d = d_ref[half].astype(F32)
            up, dn = _shift_rows(d, *_halo_rows(dp_ref, dn_ref, half, i, n_i))
            w = cw_ref[half]
            du_ref[half] = (w[0:1, :] * dn + w[1:2, :] * d + w[2:3, :] * up).astype(BF16)

    main, prev, nxt = _halo_specs(tm, s, FF_TC)
    return _call(
        body, name="conv_bwd_input", grid=(ff // FF_TC, n_i),
        in_specs=[main, prev, nxt, pl.BlockSpec((2, 3, FF_TC), lambda j, i: (0, 0, j))],
        out_specs=main, out_shape=_sds((2, s, ff), BF16), sem=("parallel", "parallel"),
    )(dc, dc, dc, cw)


def _ffn_down_loss(a, w_down, x1, target, gate2, ln_g, ln_b, *, tm):
    s, ff = a.shape
    n_i = s // tm

    def body(a_ref, wd_ref, x1_ref, tg_ref, gt_ref, lg_ref, lb_ref, ls_ref, dy_ref, dx_ref, dg_ref, db_ref, dgt_ref):
        @pl.when(pl.program_id(0) == 0)
        def _():
            dg_ref[...] = jnp.zeros(dg_ref.shape, F32)
            db_ref[...] = jnp.zeros(db_ref.shape, F32)
            dgt_ref[...] = jnp.zeros(dgt_ref.shape, F32)

        y2 = _dot(a_ref[...], wd_ref[...])
        z = ALPHA * x1_ref[...] + gt_ref[...] * y2
        mu = jnp.mean(z, axis=-1, keepdims=True)
        zc = z - mu
        r = lax.rsqrt(jnp.mean(zc * zc, axis=-1, keepdims=True) + LN_EPS)
        xhat = zc * r
        diff = xhat * lg_ref[...] + lb_ref[...] - tg_ref[...]
        ls_ref[...] = jnp.full(ls_ref.shape, 0.5 / D_MODEL * jnp.sum(diff * diff), F32)
        dx2 = diff * (1.0 / D_MODEL)
        dg_ref[...] += jnp.sum(dx2 * xhat, axis=0, keepdims=True)
        db_ref[...] += jnp.sum(dx2, axis=0, keepdims=True)
        dz = _ln_bwd(dx2, xhat, r, lg_ref[...])
        dgt_ref[...] += jnp.sum(dz * y2, axis=0, keepdims=True)
        dy_ref[...] = (gt_ref[...] * dz).astype(BF16)
        dx_ref[...] = ALPHA * dz

    row = pl.BlockSpec((tm, D_MODEL), lambda i: (i, 0))
    vec = _full((1, D_MODEL))
    return _call(
        body, name="ffn_down_loss", grid=(n_i,),
        in_specs=[pl.BlockSpec((tm, ff), lambda i: (i, 0)), _full((ff, D_MODEL)), row, row, vec, vec, vec],
        out_specs=[pl.BlockSpec((None, 8, LANES), lambda i: (i, 0, 0)), row, row, vec, vec, vec],
        out_shape=[_sds((n_i, 8, LANES)), _sds((s, D_MODEL), BF16), _sds((s, D_MODEL)),
                   _sds((1, D_MODEL)), _sds((1, D_MODEL)), _sds((1, D_MODEL))],
        sem=("arbitrary",),
    )(a, w_down, x1, target, gate2, ln_g, ln_b)


def _ffn_up_bwd(du, wup4, dx1a, x1, scale2, x, y, mu1, r1, gate1, ln_g, *, tm):
    s = x.shape[0]
    nb, _, ns = wup4.shape

    def body(du_ref, w_ref, dxa_ref, x1_ref, sc_ref, x_ref, y_ref, mu_ref, r_ref, gt_ref, lg_ref,
             dxo_ref, dy_ref, dsc_ref, dsh_ref, dg_ref, db_ref, dgt_ref, acc):
        i, k = pl.program_id(0), pl.program_id(1)

        @pl.when(jnp.logical_and(i == 0, k == 0))
        def _():
            for ref in (dsc_ref, dsh_ref, dg_ref, db_ref, dgt_ref):
                ref[...] = jnp.zeros(ref.shape, F32)

        @pl.when(k == 0)
        def _():
            acc[...] = jnp.zeros(acc.shape, F32)

        acc[...] += _dot_nt(du_ref[...], w_ref[...])

        @pl.when(k == nb - 1)
        def _():
            dh = acc[...]
            x1 = x1_ref[...]
            dsc_ref[...] += jnp.sum(dh * x1, axis=0, keepdims=True)
            dsh_ref[...] += jnp.sum(dh, axis=0, keepdims=True)
            dx1 = dxa_ref[...] + dh * (1.0 + sc_ref[...])
            yv = y_ref[...]
            xhat = (ALPHA * x_ref[...] + gt_ref[...] * yv - mu_ref[...]) * r_ref[...]
            dg_ref[...] += jnp.sum(dx1 * xhat, axis=0, keepdims=True)
            db_ref[...] += jnp.sum(dx1, axis=0, keepdims=True)
            dz = _ln_bwd(dx1, xhat, r_ref[...], lg_ref[...])
            dgt_ref[...] += jnp.sum(dz * yv, axis=0, keepdims=True)
            dy_ref[...] = (gt_ref[...] * dz).astype(BF16)
            dxo_ref[...] = ALPHA * dz

    row = pl.BlockSpec((tm, D_MODEL), lambda i, k: (i, 0))
    col = pl.BlockSpec((tm, 1), lambda i, k: (i, 0))
    vec = _full((1, D_MODEL))
    return _call(
        body, name="ffn_up_bwd", grid=(s // tm, nb),
        in_specs=[pl.BlockSpec((None, tm, ns), lambda i, k: (k // 2, i, k % 2)),
                  pl.BlockSpec((None, D_MODEL, ns), lambda i, k: (k, 0, 0)),
                  row, row, vec, row, row, col, col, vec, vec],
        out_specs=[row, row, vec, vec, vec, vec, vec],
        out_shape=[_sds((s, D_MODEL)), _sds((s, D_MODEL), BF16)] + [_sds((1, D_MODEL))] * 5,
        scratch=[pltpu.VMEM((tm, D_MODEL), F32)],
        sem=("arbitrary", "arbitrary"),
    )(du, wup4, dx1a, x1, scale2, x, y, mu1, r1, gate1, ln_g)


def _mm_nt4_mod_bwd(dp, w4, dxa, x, scale, *, tm, name):
    m = x.shape[0]
    nb, kdim, ns = w4.shape

    def body(dp_ref, w_ref, dxa_ref, x_ref, sc_ref, dx_ref, dsc_ref, dsh_ref, acc):
        i, k = pl.program_id(0), pl.program_id(1)

        @pl.when(jnp.logical_and(i == 0, k == 0))
        def _():
            dsc_ref[...] = jnp.zeros(dsc_ref.shape, F32)
            dsh_ref[...] = jnp.zeros(dsh_ref.shape, F32)

        @pl.when(k == 0)
        def _():
            acc[...] = jnp.zeros(acc.shape, F32)

        acc[...] += _dot_nt(dp_ref[...], w_ref[...])

        @pl.when(k == nb - 1)
        def _():
            dh = acc[...]
            dsc_ref[...] += jnp.sum(dh * x_ref[...], axis=0, keepdims=True)
            dsh_ref[...] += jnp.sum(dh, axis=0, keepdims=True)
            dx_ref[...] = dxa_ref[...] + dh * (1.0 + sc_ref[...])

    row = pl.BlockSpec((tm, kdim), lambda i, k: (i, 0))
    vec = _full((1, kdim))
    return _call(
        body, name=name, grid=(m // tm, nb),
        in_specs=[pl.BlockSpec((tm, ns), lambda i, k: (i, k)), pl.BlockSpec((None, kdim, ns), lambda i, k: (k, 0, 0)),
                  row, row, vec],
        out_specs=[row, vec, vec],
        out_shape=[_sds((m, kdim)), _sds((1, kdim)), _sds((1, kdim))],
        scratch=[pltpu.VMEM((tm, kdim), F32)],
        sem=("arbitrary", "arbitrary"),
    )(dp, w4, dxa, x, scale)


def _pad_heads_w(w):
    w8 = w.reshape(N_HEADS, HEAD_DIM, w.shape[-1])
    z = jnp.zeros_like(w8)
    first = (jnp.arange(N_HEADS) < N_HEADS // N_KV)[:, None, None]
    return jnp.where(first, jnp.concatenate([w8, z], axis=1), jnp.concatenate([z, w8], axis=1))


def _unpad_heads_w(g):
    first = (jnp.arange(N_HEADS) < N_HEADS // N_KV)[:, None, None]
    return jnp.where(first, g[:, :HEAD_DIM], g[:, HEAD_DIM:]).reshape(N_HEADS * HEAD_DIM, g.shape[-1])


def _ones_beside(vt):
    half = vt.shape[0] // 2
    ones = jnp.ones((half, vt.shape[1]), vt.dtype)
    return jnp.stack([jnp.concatenate([vt[:half], ones], axis=0), jnp.concatenate([ones, vt[half:]], axis=0)])


def _rep8(a):
    return jnp.broadcast_to(a.reshape(1, -1), (8, a.size))


def _first_row(a):
    r8 = _rep8(a)
    return jnp.where(lax.broadcasted_iota(jnp.int32, r8.shape, 0) == 0, r8, 0.0)


def _to_blocks4(w):
    k, n = w.shape
    return w.reshape(k, N_CHIPS, n // N_CHIPS).transpose(1, 0, 2)


def _local_step(x, c, ctx, c_ctx, wmod4, b_mod, win4, b_in, sink, qn, kn, wba, wbb, w_out, ln1_g, ln1_b,
                wup4, cw, cb, w_down, ln2_g, ln2_b, target):
    s, nc = x.shape[0], ctx.shape[0]
    tm = min(512, s)
    tm2 = min(256, s)
    tl = min(1024, s)
    tx = min(2048, s)
    zvec = jnp.zeros((1, D_MODEL), F32)

    cc = jnp.concatenate([_rep8(c), _rep8(c_ctx)], axis=0)
    mods = _mm_nn4(cc, zvec, zvec, wmod4, b_mod, mode="silu", split_out=False, out_dtype=F32, tm=16, name="mod_vectors")
    shift1, scale1, gate1, shift2, scale2, gate2 = [mods[0:1, i * D_MODEL:(i + 1) * D_MODEL] for i in range(6)]
    shift_c, scale_c = mods[8:9, :D_MODEL], mods[8:9, D_MODEL:2 * D_MODEL]

    cos, sin = _rope_tables(s)
    cos_c, sin_c = jnp.ones((nc, LANES), F32), jnp.zeros((nc, LANES), F32)
    qg, kg = jnp.tile(qn, (1, 2)), jnp.tile(kn, (1, 2))

    proj_c = _mm_nn4(ctx, shift_c, scale_c, win4, b_in, mode="modulate", split_out=False, out_dtype=F32, tm=nc,
                     name="in_proj_ctx")
    _, kac, vac, _, kbc, vbc = _prep(proj_c, cos_c, sin_c, qg, kg, tm=nc, name="prep_ctx")
    proj = _mm_nn4(x, shift1, scale1, win4, b_in, mode="modulate", split_out=False, out_dtype=F32, tm=tx, name="in_proj")
    qat, ka, va, qbt, kb, vb = _prep(proj, cos, sin, qg, kg, tm=tl, name="prep")
    oat, lse_a = _attn_win_fwd(qat, ka, _ones_beside(va.T), kac, _ones_beside(vac.T), sink, tq=tm)
    obt, lse_b, mrun_b, pbt = _attn_glob_fwd(qbt, kb, _ones_beside(vb.T), kbc, _ones_beside(vbc.T), tq=tm,
                                             tk=min(1024, s))
    wba_p, wbb_p = _pad_heads_w(wba), _pad_heads_w(wbb)
    x1, y, mu1, r1, pa, pb, merged = _merge_fwd(oat, obt, proj, x, gate1, wba_p, wbb_p, w_out, ln1_g, ln1_b, tm=tm)
    u = _mm_nn4(x1, shift2, scale2, wup4, jnp.zeros((1, 2 * D_FF), F32), mode="modulate", split_out=True,
                out_dtype=BF16, tm=tx, name="ffn_up")
    cw2 = cw.reshape(3, 2, D_FF).transpose(1, 0, 2)
    cb2 = cb.reshape(2, 1, D_FF)
    a = _ffn_act_fwd(u, cw2, cb2, tm=tx)
    ls, dy2, dx1a, dln2_g, dln2_b, dgate2 = _ffn_down_loss(a, w_down, x1, target, gate2, ln2_g, ln2_b, tm=tm)
    loss = jnp.sum(ls[:, 0, 0])

    n_s = s // tl
    dw_down = _mm_tn(a, dy2, a_spec=pl.BlockSpec((tl, D_FF), lambda t: (t, 0)),
                     b_spec=pl.BlockSpec((tl, D_MODEL), lambda t: (t, 0)), grid=(n_s,),
                     out_shape=_sds((D_FF, D_MODEL)), out_spec=_full((D_FF, D_MODEL)), name="dw_down")
    dc, dcw2, dcb2 = _ffn_act_bwd(dy2, w_down, u, cw2, cb2, tm=tx)
    du = _conv_bwd_input(dc, cw2, tm=tx)
    dxz1, dy, dscale2, dshift2, dln1_g, dln1_b, dgate1 = _ffn_up_bwd(
        du, wup4, dx1a, x1, scale2, x, y, mu1, r1, gate1, ln1_g, tm=tm)
    ns_up = wup4.shape[-1]
    dw_up4 = _mm_tn(x1, du, a_spec=pl.BlockSpec((tx, D_MODEL), lambda k, t: (t, 0)),
                    b_spec=pl.BlockSpec((None, tx, ns_up), lambda k, t: (k // 2, t, k % 2)), grid=(N_CHIPS, s // tx),
                    out_shape=_sds((N_CHIPS, D_MODEL, ns_up)),
                    out_spec=pl.BlockSpec((None, D_MODEL, ns_up), lambda k, t: (k, 0, 0)),
                    mod=(shift2, scale2), name="dw_up")

    dgl, doat, dobt, dwba_p, dwbb_p = _merge_bwd(dy, oat, obt, pa, pb, proj, wba_p, wbb_p, w_out, tm=tm2)
    dwba, dwbb = _unpad_heads_w(dwba_p), _unpad_heads_w(dwbb_p)
    rowspec = pl.BlockSpec((tl, D_MODEL), lambda t: (t, 0))
    dw_out = _mm_tn(merged, dy, a_spec=rowspec, b_spec=rowspec, grid=(n_s,), out_shape=_sds((D_MODEL, D_MODEL)),
                    out_spec=_full((D_MODEL, D_MODEL)), name="dw_out")

    dqat, dkat, dvat, dkact, dvact, dsk = _attn_win_bwd(qat, doat, oat, lse_a, ka, ka.T, va, kac, kac.T, vac, sink, tq=tm)
    dka, dva, dkac, dvac = dkat.T, dvat.T, dkact.T, dvact.T
    dqbt, dkbt, dvbt, dkbct, dvbct = _attn_glob_bwd(qbt, dobt, obt, lse_b, mrun_b, pbt, kb.T, vb, kbc.T, vbc, tq=tm, tk=tm)
    dkb, dvb, dkbc, dvbc = dkbt.T, dvbt.T, dkbct.T, dvbct.T
    dsink = jnp.sum(dsk[:, :, 0, 0], axis=1)

    dproj, dqg, dkg = _prep_bwd(dqat, dka, dva, dqbt, dkb, dvb, proj, cos, sin, qg, kg, dgl, tm=tm, name="prep_bwd")
    grad_x, dscale1, dshift1 = _mm_nt4_mod_bwd(dproj, win4, dxz1, x, scale1, tm=tl, name="in_proj_bwd")
    ns_in = win4.shape[-1]
    win_spec = dict(b_spec=pl.BlockSpec((None, None, ns_in), lambda k, t: (0, 0, k)),
                    out_shape=_sds((N_CHIPS, D_MODEL, ns_in)),
                    out_spec=pl.BlockSpec((None, D_MODEL, ns_in), lambda k, t: (k, 0, 0)),
                    colsum_spec=pl.BlockSpec((8, ns_in), lambda k, t: (0, k)), colsum_shape=_sds((8, IN_COLS)))
    win_spec["b_spec"] = pl.BlockSpec((tx, ns_in), lambda k, t: (t, k))
    dw_in4, db_in = _mm_tn(x, dproj, a_spec=pl.BlockSpec((tx, D_MODEL), lambda k, t: (t, 0)), grid=(N_CHIPS, s // tx),
                           mod=(shift1, scale1), name="dw_in", **win_spec)

    zq = jnp.zeros((N_HEADS, LANES, nc), F32)
    dproj_c, _, dkg_c = _prep_bwd(zq, dkac, dvac, zq, dkbc, dvbc, proj_c, cos_c, sin_c, qg, kg,
                                  jnp.zeros((nc, IN_COLS - OFF_GA), BF16), tm=nc, name="prep_bwd_ctx")
    _, dscale_c, dshift_c = _mm_nt4_mod_bwd(dproj_c, win4, jnp.zeros((nc, D_MODEL), F32), ctx, scale_c, tm=nc,
                                            name="in_proj_bwd_ctx")
    win_spec["b_spec"] = pl.BlockSpec((nc, ns_in), lambda k, t: (t, k))
    dw_in4, db_in_c = _mm_tn(ctx, dproj_c, a_spec=pl.BlockSpec((nc, D_MODEL), lambda k, t: (t, 0)), grid=(N_CHIPS, 1),
                             mod=(shift_c, scale_c), init=dw_in4, name="dw_in_ctx", **win_spec)

    dmod = jnp.concatenate([dshift1, dscale1, dgate1, dshift2, dscale2, dgate2], axis=1)
    dmodc = jnp.concatenate([dshift_c, dscale_c], axis=1)
    dmodc_pad = jnp.concatenate([dmodc, jnp.zeros((1, 4 * D_MODEL), F32)], axis=1)
    dmodc8 = _first_row(dmodc_pad).astype(BF16)
    z8 = jnp.zeros((8, D_MODEL), F32)
    dsilu_c, _, _ = _mm_nt4_mod_bwd(dmodc8, wmod4, z8, z8, zvec, tm=8, name="c_ctx_bwd")
    sg = _sigmoid(c_ctx)
    dc_ctx = dsilu_c[0:1] * sg * (1.0 + c_ctx * (1.0 - sg))

    dqn = jnp.sum(dqg.reshape(N_HEADS, HEAD_DIM), axis=0, keepdims=True)
    dkn = jnp.sum((dkg + dkg_c).reshape(N_KV, HEAD_DIM), axis=0, keepdims=True)
    grads = dict(
        w_in4=dw_in4, b_in=db_in[0:1] + db_in_c[0:1], sink=dsink, qn=dqn, kn=dkn, wba=dwba, wbb=dwbb, w_out=dw_out,
        ln1_g=dln1_g, ln1_b=dln1_b, w_up4=dw_up4, conv_w=dcw2.transpose(1, 0, 2).reshape(3, 2 * D_FF),
        conv_b=dcb2.reshape(1, 2 * D_FF), w_down=dw_down, ln2_g=dln2_g, ln2_b=dln2_b,
        c_ctx=dc_ctx, dmod=dmod, dmodc=dmodc)
    return loss, grad_x, grads


ANY = pl.BlockSpec(memory_space=pl.ANY)


def _mesh_pos():
    return lax.axis_index("x"), lax.axis_index("y"), lax.axis_index("c")


def _other_chips(x, y):
    return [(1 - x, y), (x, 1 - y), (1 - x, 1 - y)]


def _remote(src, dst, send, recv, dev):
    return pltpu.make_async_remote_copy(src_ref=src, dst_ref=dst, send_sem=send, recv_sem=recv, device_id=dev,
                                        device_id_type=MESH)


def _set_block(stack, block, k):
    return lax.dynamic_update_slice(stack, block[None], (k,) + (0,) * block.ndim)


def _gather_shards(arrs, small):
    na = len(arrs)
    halves = [a.shape[0] // 2 for a in arrs]

    def body(*refs):
        ins, small_ref = refs[:na], refs[na]
        outs, small_out = refs[na + 1:2 * na + 1], refs[2 * na + 1]
        send, recv = refs[2 * na + 2:]
        x, y, c = _mesh_pos()
        me = 2 * x + y
        chips = _other_chips(x, y)

        def half(a, cc):
            return pl.ds(cc * halves[a], halves[a])

        sends = []
        for j, chip in enumerate(chips):
            for a in range(na):
                sends.append(_remote(ins[a].at[half(a, c)], outs[a].at[me, half(a, c)], send.at[a, j], recv.at[a, j],
                                     (*chip, c)))
            sends.append(_remote(small_ref, small_out.at[me], send.at[na, j], recv.at[na, j], (*chip, c)))
        for cp in sends:
            cp.start()
        for j, chip in enumerate(chips):
            kj = 2 * chip[0] + chip[1]
            for a in range(na):
                landed = outs[a].at[kj, half(a, c)]
                _remote(landed, landed, send.at[a, j], recv.at[a, j], (*chip, c)).wait_recv()
                fwd = _remote(landed, landed, send.at[a, 3 + j], recv.at[a, 3 + j], (x, y, 1 - c))
                fwd.start()
                sends.append(fwd)
            _remote(small_ref, small_out.at[kj], send.at[na, j], recv.at[na, j], (*chip, c)).wait_recv()
        for j, chip in enumerate(chips):
            kj = 2 * chip[0] + chip[1]
            for a in range(na):
                other = outs[a].at[kj, half(a, 1 - c)]
                _remote(other, other, send.at[a, 3 + j], recv.at[a, 3 + j], (x, y, 1 - c)).wait_recv()
        for cp in sends:
            cp.wait_send()

    out_shape = [_sds((N_CHIPS,) + a.shape, a.dtype) for a in arrs] + [_sds((N_CHIPS,) + small.shape, small.dtype)]
    got = pl.pallas_call(
        body, name="gather_shards", in_specs=[ANY] * (na + 1), out_specs=[ANY] * (na + 1), out_shape=out_shape,
        scratch_shapes=[pltpu.SemaphoreType.DMA((na + 1, 6)), pltpu.SemaphoreType.DMA((na + 1, 6))],
    )(*arrs, small)
    xp, yp, _ = _mesh_pos()
    return [_set_block(g, a, 2 * xp + yp) for g, a in zip(got, list(arrs) + [small])]


def _allgather_rows(v):
    r, n = v.shape

    def body(v_ref, out_ref, send, recv, loc):
        x, y, c = _mesh_pos()
        me, sibling = (x, y, c), (x, y, 1 - c)
        chips = _other_chips(x, y)

        def rows(px, py, pc):
            return out_ref.at[4 * px + 2 * py + pc]

        def copy(k, block, to, src=None):
            return _remote(rows(*block) if src is None else src, rows(*block), send.at[k], recv.at[k], to)

        mine = pltpu.make_async_copy(v_ref, rows(*me), loc)
        mine.start()
        first = [copy(0, me, sibling, src=v_ref)] + [copy(1 + j, me, (*chip, c), src=v_ref) for j, chip in enumerate(chips)]
        for cp in first:
            cp.start()
        passed = [copy(4 + j, (*chip, c), sibling) for j, chip in enumerate(chips)]
        for j, chip in enumerate(chips):
            copy(1 + j, (*chip, c), me).wait_recv()
            passed[j].start()
        copy(0, sibling, me).wait_recv()
        for j, chip in enumerate(chips):
            copy(4 + j, (*chip, 1 - c), me).wait_recv()
        for cp in first + passed:
            cp.wait_send()
        mine.wait()

    return pl.pallas_call(
        body, name="allgather_rows", in_specs=[pl.BlockSpec(memory_space=pltpu.VMEM)],
        out_specs=pl.BlockSpec(memory_space=pltpu.VMEM), out_shape=_sds((N_DEV, r, n), v.dtype),
        scratch_shapes=[pltpu.SemaphoreType.DMA((7,)), pltpu.SemaphoreType.DMA((7,)), pltpu.SemaphoreType.DMA],
    )(v)


def _swap_other_half(g):
    nb, r, n = g.shape
    rh = r // 2

    def body(g_ref, out_ref, send, recv):
        x, y, c = _mesh_pos()
        cp = _remote(g_ref.at[:, pl.ds((1 - c) * rh, rh), :], out_ref, send, recv, (x, y, 1 - c))
        cp.start()
        cp.wait()

    return pl.pallas_call(
        body, name="swap_other_half", in_specs=[ANY], out_specs=ANY, out_shape=_sds((nb, rh, n), g.dtype),
        scratch_shapes=[pltpu.SemaphoreType.DMA, pltpu.SemaphoreType.DMA],
    )(g)


def _scatter_to_chips(p):
    def body(p_ref, out_ref, send, recv):
        x, y, c = _mesh_pos()
        me = 2 * x + y
        chips = _other_chips(x, y)
        sends = [_remote(p_ref.at[2 * chip[0] + chip[1]], out_ref.at[me], send.at[j], recv.at[j], (*chip, c))
                 for j, chip in enumerate(chips)]
        for cp in sends:
            cp.start()
        for j, chip in enumerate(chips):
            kj = 2 * chip[0] + chip[1]
            _remote(p_ref.at[kj], out_ref.at[kj], send.at[j], recv.at[j], (*chip, c)).wait_recv()
        for cp in sends:
            cp.wait_send()

    got = pl.pallas_call(
        body, name="scatter_to_chips", in_specs=[ANY], out_specs=ANY, out_shape=_sds(p.shape, p.dtype),
        scratch_shapes=[pltpu.SemaphoreType.DMA((3,)), pltpu.SemaphoreType.DMA((3,))],
    )(p)
    xp, yp, _ = _mesh_pos()
    me = 2 * xp + yp
    return _set_block(got, lax.dynamic_index_in_dim(p, me, axis=0, keepdims=False), me)


def _join_halves(f):
    def body(f_ref, out_ref, send, recv):
        x, y, c = _mesh_pos()
        cp = _remote(f_ref, out_ref, send, recv, (x, y, 1 - c))
        cp.start()
        cp.wait()

    other = pl.pallas_call(
        body, name="join_halves", in_specs=[ANY], out_specs=ANY, out_shape=_sds(f.shape, f.dtype),
        scratch_shapes=[pltpu.SemaphoreType.DMA, pltpu.SemaphoreType.DMA],
    )(f)
    first = lax.axis_index("c") == 0
    return jnp.concatenate([jnp.where(first, f, other), jnp.where(first, other, f)], axis=0)


def _row_tile(rows, cap=512):
    t = cap - cap % 8
    while rows % t:
        t -= 8
    return t


def _add_blocks(a, b, out_dtype):
    nb, r, n = a.shape
    tr = _row_tile(r)

    def body(a_ref, b_ref, o_ref):
        o_ref[...] = (a_ref[...] + b_ref[...]).astype(out_dtype)

    spec = pl.BlockSpec((None, tr, n), lambda k, i: (k, i, 0))
    return _call(body, name="add_blocks", grid=(nb, r // tr), in_specs=[spec, spec], out_specs=spec,
                 out_shape=_sds(a.shape, out_dtype), sem=("parallel", "parallel"))(a, b)


def _sum_leading(a, *, name):
    nk, r, n = a.shape
    tr = _row_tile(r)

    def body(a_ref, o_ref):
        acc = a_ref[0].astype(F32)
        for k in range(1, nk):
            acc = acc + a_ref[k].astype(F32)
        o_ref[...] = acc

    return _call(body, name=name, grid=(r // tr,), in_specs=[pl.BlockSpec((nk, tr, n), lambda i: (0, i, 0))],
                 out_specs=pl.BlockSpec((tr, n), lambda i: (i, 0)), out_shape=_sds((r, n)), sem=("parallel",))(a)


def _silu_outer(a, b):
    kdim, n = a.shape[1], b.shape[1]

    def body(a_ref, b_ref, o_ref):
        av = a_ref[...]
        av = av * _sigmoid(av)
        bv = b_ref[...]
        ah, bh = av.astype(BF16), bv.astype(BF16)
        al, bl = (av - ah.astype(F32)).astype(BF16), (bv - bh.astype(F32)).astype(BF16)
        o_ref[...] = _dot_tn(ah, bh) + (_dot_tn(ah, bl) + _dot_tn(al, bh))

    return _call(body, name="dw_mod", grid=(1,), in_specs=[_full(a.shape), _full(b.shape)], out_specs=_full((kdim, n)),
                 out_shape=_sds((kdim, n)))(a, b)


def _adamw(w, g, m, v):
    r, n = w.shape
    tr = _row_tile(r)

    def body(w_ref, g_ref, m_ref, v_ref, d_ref, nm_ref, nv_ref):
        gv = g_ref[...]
        nm = ADAM_B1 * m_ref[...] + (1.0 - ADAM_B1) * gv
        nv = ADAM_B2 * v_ref[...] + (1.0 - ADAM_B2) * (gv * gv)
        m_hat = nm / (1.0 - ADAM_B1 ** ADAM_STEP)
        v_hat = nv / (1.0 - ADAM_B2 ** ADAM_STEP)
        d_ref[...] = -ADAM_LR * (m_hat / (jnp.sqrt(v_hat) + ADAM_EPS) + ADAM_WD * w_ref[...])
        nm_ref[...] = nm
        nv_ref[...] = nv

    spec = pl.BlockSpec((tr, n), lambda i: (i, 0))
    return _call(body, name="adamw", grid=(r // tr,), in_specs=[spec] * 4, out_specs=[spec] * 3,
                 out_shape=[_sds((r, n))] * 3, sem=("parallel",))(w, g, m, v)


BIG = ("w_in", "w_branch_a", "w_branch_b", "w_out", "w_up", "w_down", "conv_w")
BIG_ROWS = 3584
MATRICES = ("w_mod", "w_in", "w_branch_a", "w_branch_b", "w_out", "w_up", "w_down")
SMALL = ("b_mod", "b_in", "conv_b", "ln1_g", "ln1_b", "ln2_g", "ln2_b", "c_ctx", "attn_sink", "q_norm_g", "k_norm_g", "conv_w")
SMALL_ROWS = 8 * len(SMALL)


def _rows(a, n_rows):
    flat = a.reshape(-1)
    return jnp.pad(flat, (0, n_rows * D_MODEL - flat.shape[0])).reshape(n_rows, D_MODEL)


def _group8(a):
    return _rep8(_rows(a, 1)) if a.size <= D_MODEL else _rows(a, 8)


def _ungroup8(p, shape):
    size = math.prod(shape)
    return (p[0, :size] if size <= D_MODEL else p.reshape(-1)[:size]).reshape(shape)


def _unpack_big(p, like):
    out, r = {}, 0
    for n in BIG:
        size = math.prod(like[n].shape)
        nr = size // D_MODEL if n != "conv_w" else 8
        out[n] = p[r:r + nr].reshape(-1)[:size].reshape(like[n].shape)
        r += nr
    return out


def _pack_small(t):
    return jnp.concatenate([_group8(t[n]) for n in SMALL], axis=0)


def _unpack_small(p, like):
    return {n: _ungroup8(p[8 * i:8 * i + 8], like[n].shape) for i, n in enumerate(SMALL)}


WEIGHTS = ("c_ctx", "w_mod", "b_mod", "w_in", "b_in", "attn_sink", "q_norm_g", "k_norm_g", "w_branch_a", "w_branch_b",
           "w_out", "ln1_g", "ln1_b", "w_up", "conv_w", "conv_b", "w_down", "ln2_g", "ln2_b")


def kernel(x, c, ctx, c_ctx, w_mod, b_mod, w_in, b_in, attn_sink, q_norm_g, k_norm_g, w_branch_a, w_branch_b, w_out, ln1_g, ln1_b, w_up, conv_w, conv_b, w_down, ln2_g, ln2_b, loss_target, m_c_ctx, m_w_mod, m_b_mod, m_w_in, m_b_in, m_attn_sink, m_q_norm_g, m_k_norm_g, m_w_branch_a, m_w_branch_b, m_w_out, m_ln1_g, m_ln1_b, m_w_up, m_conv_w, m_conv_b, m_w_down, m_ln2_g, m_ln2_b, v_c_ctx, v_w_mod, v_b_mod, v_w_in, v_b_in, v_attn_sink, v_q_norm_g, v_k_norm_g, v_w_branch_a, v_w_branch_b, v_w_out, v_ln1_g, v_ln1_b, v_w_up, v_conv_w, v_conv_b, v_w_down, v_ln2_g, v_ln2_b):
    w = dict(c_ctx=c_ctx, w_mod=w_mod, b_mod=b_mod, w_in=w_in, b_in=b_in, attn_sink=attn_sink, q_norm_g=q_norm_g,
             k_norm_g=k_norm_g, w_branch_a=w_branch_a, w_branch_b=w_branch_b, w_out=w_out, ln1_g=ln1_g, ln1_b=ln1_b,
             w_up=w_up, conv_w=conv_w, conv_b=conv_b, w_down=w_down, ln2_g=ln2_g, ln2_b=ln2_b)
    m = dict(c_ctx=m_c_ctx, w_mod=m_w_mod, b_mod=m_b_mod, w_in=m_w_in, b_in=m_b_in, attn_sink=m_attn_sink,
             q_norm_g=m_q_norm_g, k_norm_g=m_k_norm_g, w_branch_a=m_w_branch_a, w_branch_b=m_w_branch_b, w_out=m_w_out,
             ln1_g=m_ln1_g, ln1_b=m_ln1_b, w_up=m_w_up, conv_w=m_conv_w, conv_b=m_conv_b, w_down=m_w_down,
             ln2_g=m_ln2_g, ln2_b=m_ln2_b)
    v = dict(c_ctx=v_c_ctx, w_mod=v_w_mod, b_mod=v_b_mod, w_in=v_w_in, b_in=v_b_in, attn_sink=v_attn_sink,
             q_norm_g=v_q_norm_g, k_norm_g=v_k_norm_g, w_branch_a=v_w_branch_a, w_branch_b=v_w_branch_b, w_out=v_w_out,
             ln1_g=v_ln1_g, ln1_b=v_ln1_b, w_up=v_w_up, conv_w=v_conv_w, conv_b=v_conv_b, w_down=v_w_down,
             ln2_g=v_ln2_g, ln2_b=v_ln2_b)
    xp, yp, _ = _mesh_pos()
    me = 2 * xp + yp

    branches = jnp.concatenate([w_branch_a[0], w_branch_b[0]], axis=0)
    wide = jnp.concatenate([w_mod[0], w_in[0], w_up[0], branches], axis=1).astype(BF16)
    tall = jnp.concatenate([w_out[0], w_down[0]], axis=0).astype(BF16)
    wide4, tall4, cw4 = _gather_shards([wide, tall], conv_w[0])
    n_mod, n_in, n_up = w_mod.shape[-1], w_in.shape[-1], w_up.shape[-1]
    wmod4 = wide4[:, :, :n_mod]
    win4 = wide4[:, :, n_mod:n_mod + n_in]
    wup4 = wide4[:, :, n_mod + n_in:n_mod + n_in + n_up]
    br4 = wide4[:, :, n_mod + n_in + n_up:]
    n_br = w_branch_a.shape[1]
    wba = br4[:, :n_br].transpose(1, 0, 2).reshape(n_br, D_MODEL)
    wbb = br4[:, n_br:].transpose(1, 0, 2).reshape(n_br, D_MODEL)
    n_out = w_out.shape[1]
    w_out_full = tall4[:, :n_out].reshape(D_MODEL, D_MODEL)
    w_down_full = tall4[:, n_out:].reshape(D_FF, D_MODEL)
    cw_full = cw4.transpose(1, 0, 2).reshape(3, 2 * D_FF)

    loss, grad_x, g = _local_step(
        x[0], c, ctx[0], c_ctx[None], wmod4, b_mod, win4, b_in, attn_sink[0], q_norm_g, k_norm_g, wba, wbb, w_out_full,
        ln1_g, ln1_b, wup4, cw_full, conv_b, w_down_full, ln2_g, ln2_b, loss_target[0])
    loss = lax.psum(loss, ("x", "y", "c"))

    sent = dict(c=c, dmod=g["dmod"], dmodc=g["dmodc"], b_in=g["b_in"], conv_b=g["conv_b"], ln1_g=g["ln1_g"],
                ln1_b=g["ln1_b"], ln2_g=g["ln2_g"], ln2_b=g["ln2_b"], c_ctx=g["c_ctx"], attn_sink=g["sink"],
                q_norm_g=g["qn"], k_norm_g=g["kn"])
    every = _allgather_rows(jnp.concatenate([_group8(a) for a in sent.values()], axis=0))
    total = _sum_leading(every, name="sum_devices")
    slot = {n: slice(8 * i, 8 * i + 8) for i, n in enumerate(sent)}
    gs = {n: _ungroup8(total[slot[n]], sent[n].shape) for n in SMALL if n in sent}
    dmodc_sum = jnp.concatenate([_ungroup8(total[slot["dmodc"]], (1, 2 * D_MODEL)), jnp.zeros((1, 4 * D_MODEL), F32)],
                                axis=1)
    gs["b_mod"] = _ungroup8(total[slot["dmod"]], b_mod.shape) + dmodc_sum
    acts = jnp.concatenate([every[:, slot["c"].start], _rep8(c_ctx)], axis=0)
    dmods = jnp.concatenate([every[:, slot["dmod"]].reshape(N_DEV, -1)[:, :6 * D_MODEL], _first_row(dmodc_sum)], axis=0)
    g_w_mod = _silu_outer(acts, lax.dynamic_slice_in_dim(dmods, me * n_mod, n_mod, axis=1))

    cw_g4 = _to_blocks4(g["conv_w"])
    parts = [
        g["w_in4"].reshape(N_CHIPS, -1, D_MODEL), _to_blocks4(g["wba"]).reshape(N_CHIPS, -1, D_MODEL),
        _to_blocks4(g["wbb"]).reshape(N_CHIPS, -1, D_MODEL), g["w_out"].reshape(N_CHIPS, -1, D_MODEL),
        g["w_up4"].reshape(N_CHIPS, -1, D_MODEL), g["w_down"].reshape(N_CHIPS, -1, D_MODEL),
        jnp.pad(cw_g4.reshape(N_CHIPS, -1), ((0, 0), (0, 8 * D_MODEL - cw_g4.shape[1] * cw_g4.shape[2]))).reshape(
            N_CHIPS, 8, D_MODEL)]
    used = sum(p.shape[1] for p in parts)
    packed = jnp.concatenate(parts + [jnp.zeros((N_CHIPS, BIG_ROWS - used, D_MODEL), F32)], axis=1)
    rh = BIG_ROWS // 2
    cpos = lax.axis_index("c")
    my_half = lax.dynamic_slice_in_dim(packed, cpos * rh, rh, axis=1)
    chip_sum = _add_blocks(my_half, _swap_other_half(packed), BF16)
    half_sum = _sum_leading(_scatter_to_chips(chip_sum), name="sum_chips")
    g_big = _unpack_big(_join_halves(half_sum), w)

    grads = dict(gs, w_mod=g_w_mod, **g_big)
    grads = {n: grads[n].reshape(w[n].shape) for n in WEIGHTS}
    delta, new_m, new_v = {}, {}, {}
    for n in MATRICES:
        outs = _adamw(*[t[n][0] for t in (w, grads, m, v)])
        delta[n], new_m[n], new_v[n] = [o[None] for o in outs]
    outs = _adamw(*[_pack_small(t) for t in (w, grads, m, v)])
    for res, o in zip((delta, new_m, new_v), outs):
        res.update(_unpack_small(o, w))
    return (loss, grad_x[None], *[grads[n] for n in WEIGHTS], *[delta[n] for n in WEIGHTS],
            *[new_m[n] for n in WEIGHTS], *[new_v[n] for n in WEIGHTS])
```

```python
import functools
import math

import jax
import jax.numpy as jnp
from jax import lax
from jax.experimental import pallas as pl
from jax.experimental.pallas import tpu as pltpu

F32 = jnp.float32
BF16 = jnp.bfloat16

D_MODEL = 1024
HEAD_DIM = 64
N_HEADS = 8
N_KV = 2
WINDOW = 128
GRID_W = 64
ROPE_THETA = 10000.0
D_FF = 2816
LN_EPS = 1e-5
QK_EPS = 1e-6
ALPHA = 2.0 ** 0.25
Q_SCALE = HEAD_DIM ** -0.5
OFF_GA = 1536
IN_COLS = 3584
ADAM_LR, ADAM_B1, ADAM_B2, ADAM_EPS, ADAM_WD, ADAM_STEP = 0.001, 0.9, 0.999, 1e-8, 0.01, 10

LANES = 128
VMEM_BUDGET = 52 * 1024 * 1024
N_CHIPS = 4
N_DEV = 8
NEG = -1e30
MESH = pl.DeviceIdType.MESH


def _sigmoid(x):
    return 1.0 / (1.0 + jnp.exp(-x))


def _dot(a, b):
    return jnp.dot(a, b, preferred_element_type=F32)


def _dot_nt(a, b):
    return lax.dot_general(a, b, (((1,), (1,)), ((), ())), preferred_element_type=F32)


def _dot_tn(a, b):
    return lax.dot_general(a, b, (((0,), (0,)), ((), ())), preferred_element_type=F32)


def _call(body, *, name, grid, in_specs, out_specs, out_shape, scratch=(), sem=None, **kw):
    params = dict(vmem_limit_bytes=VMEM_BUDGET)
    if sem is not None:
        params["dimension_semantics"] = sem
    return pl.pallas_call(body, name=name, grid=grid, in_specs=in_specs, out_specs=out_specs,
                          out_shape=out_shape, scratch_shapes=list(scratch),
                          compiler_params=pltpu.CompilerParams(**params), **kw)


def _full(shape):
    n = len(shape)
    return pl.BlockSpec(shape, lambda *_: (0,) * n)


def _sds(shape, dtype=F32):
    return jax.ShapeDtypeStruct(shape, dtype)


def _mm_nn4(a, shift, scale, w4, bias, *, mode, split_out, out_dtype, tm, name):
    m, kdim = a.shape
    nb, _, ns = w4.shape

    def body(a_ref, sh_ref, sc_ref, w_ref, b_ref, o_ref):
        av = a_ref[...]
        if mode == "modulate":
            av = av * (1.0 + sc_ref[...]) + sh_ref[...]
        else:
            av = av * _sigmoid(av)
        o_ref[...] = (_dot(av.astype(BF16), w_ref[...]) + b_ref[...]).astype(out_dtype)

    if split_out:
        out_shape = _sds((2, m, 2 * ns), out_dtype)
        out_spec = pl.BlockSpec((None, tm, ns), lambda i, k: (k // 2, i, k % 2))
    else:
        out_shape = _sds((m, nb * ns), out_dtype)
        out_spec = pl.BlockSpec((tm, ns), lambda i, k: (i, k))
    return _call(
        body, name=name, grid=(m // tm, nb),
        in_specs=[pl.BlockSpec((tm, kdim), lambda i, k: (i, 0)),
                  pl.BlockSpec((1, kdim), lambda i, k: (0, 0)),
                  pl.BlockSpec((1, kdim), lambda i, k: (0, 0)),
                  pl.BlockSpec((None, kdim, ns), lambda i, k: (k, 0, 0)),
                  pl.BlockSpec((1, ns), lambda i, k: (0, k))],
        out_specs=out_spec, out_shape=out_shape, sem=("parallel", "arbitrary"),
    )(a, shift, scale, w4, bias)


def _mm_tn(a, b, *, a_spec, b_spec, grid, out_shape, out_spec, name, mod=None, init=None, colsum_spec=None,
           colsum_shape=None):
    red = len(grid) - 1
    has_mod, has_init, has_cs = mod is not None, init is not None, colsum_spec is not None

    def body(*refs):
        refs = list(refs)
        a_ref, b_ref = refs[0], refs[1]
        pos = 2
        if has_mod:
            sh_ref, sc_ref = refs[2], refs[3]
            pos = 4
        if has_init:
            init_ref = refs[pos]
            pos += 1
        o_ref = refs[pos]
        cs_ref = refs[pos + 1] if has_cs else None
        s = pl.program_id(red)

        @pl.when(s == 0)
        def _():
            o_ref[...] = init_ref[...] if has_init else jnp.zeros(o_ref.shape, F32)
            if has_cs:
                cs_ref[...] = jnp.zeros(cs_ref.shape, F32)

        av = a_ref[...]
        if has_mod:
            av = av * (1.0 + sc_ref[...]) + sh_ref[...]
        bv = b_ref[...]
        o_ref[...] += _dot_tn(av.astype(BF16), bv)
        if has_cs:
            cs_ref[...] += jnp.broadcast_to(jnp.sum(bv.astype(F32), axis=0, keepdims=True), cs_ref.shape)

    ins, in_specs = [a, b], [a_spec, b_spec]
    if has_mod:
        kdim = mod[0].shape[-1]
        ins += list(mod)
        in_specs += [_full((1, kdim)), _full((1, kdim))]
    if has_init:
        ins.append(init)
        in_specs.append(out_spec)
    out_specs, out_shapes = out_spec, out_shape
    if has_cs:
        out_specs, out_shapes = [out_spec, colsum_spec], [out_shape, colsum_shape]
    sem = ("parallel",) * red + ("arbitrary",)
    return _call(body, name=name, grid=grid, in_specs=in_specs, out_specs=out_specs, out_shape=out_shapes,
                 sem=sem)(*ins)


def _rope_tables(n_tok):
    pos = jnp.arange(n_tok, dtype=jnp.int32)
    rows = (pos // GRID_W).astype(F32)
    cols = (pos % GRID_W).astype(F32)
    n_freq = HEAD_DIM // 4
    inv_freq = ROPE_THETA ** (-jnp.arange(n_freq, dtype=F32) / n_freq)
    ang_r = rows[:, None] * inv_freq
    ang_c = cols[:, None] * inv_freq
    cos = jnp.concatenate([jnp.cos(ang_r)] * 2 + [jnp.cos(ang_c)] * 2, axis=-1)
    sin = jnp.concatenate([-jnp.sin(ang_r), jnp.sin(ang_r), -jnp.sin(ang_c), jnp.sin(ang_c)], axis=-1)
    return jnp.tile(cos, (1, 2)), jnp.tile(sin, (1, 2))


def _lane(shape):
    return lax.broadcasted_iota(jnp.int32, shape, 1)


def _rope_partner(t, lane):
    return jnp.where((lane % 32) < 16, pltpu.roll(t, LANES - 16, 1), pltpu.roll(t, 16, 1))


def _half_mean(s, lane):
    lo = jnp.sum(jnp.where(lane < HEAD_DIM, s, 0.0), axis=-1, keepdims=True)
    hi = jnp.sum(jnp.where(lane < HEAD_DIM, 0.0, s), axis=-1, keepdims=True)
    return jnp.where(lane < HEAD_DIM, lo, hi) * (1.0 / HEAD_DIM)


def _prep(proj, cos, sin, qg, kg, *, tm, name):
    m = proj.shape[0]

    def body(p_ref, cos_ref, sin_ref, qg_ref, kg_ref, qa_ref, ka_ref, va_ref, qb_ref, kb_ref, vb_ref):
        lane = _lane((tm, LANES))
        cosv, sinv = cos_ref[...], sin_ref[...]
        low = lane < HEAD_DIM

        def rope(t):
            return t * cosv + _rope_partner(t, lane) * sinv

        def rms(t, g):
            return t * lax.rsqrt(_half_mean(t * t, lane) + QK_EPS) * g

        def place(q_ref, j, chunk):
            sw = pltpu.roll(chunk, HEAD_DIM, 1)
            if j < 2:
                h0, h1 = jnp.where(low, chunk, 0.0), jnp.where(low, sw, 0.0)
            else:
                h0, h1 = jnp.where(low, 0.0, sw), jnp.where(low, 0.0, chunk)
            q_ref[2 * j] = h0.T.astype(BF16)
            q_ref[2 * j + 1] = h1.T.astype(BF16)

        for j in range(4):
            place(qa_ref, j, rope(p_ref[:, j * LANES:(j + 1) * LANES]) * Q_SCALE)
            place(qb_ref, j, rope(rms(p_ref[:, 768 + j * LANES:768 + (j + 1) * LANES], qg_ref[...])) * Q_SCALE)
        ka_ref[...] = rope(p_ref[:, 512:640]).astype(BF16)
        va_ref[...] = p_ref[:, 640:768].astype(BF16)
        kb_ref[...] = rope(rms(p_ref[:, 1280:1408], kg_ref[...])).astype(BF16)
        vb_ref[...] = p_ref[:, 1408:1536].astype(BF16)

    row = pl.BlockSpec((tm, LANES), lambda i: (i, 0))
    qspec = pl.BlockSpec((N_HEADS, LANES, tm), lambda i: (0, 0, i))
    return _call(
        body, name=name, grid=(m // tm,),
        in_specs=[pl.BlockSpec((tm, OFF_GA), lambda i: (i, 0)), row, row, _full((1, LANES)), _full((1, LANES))],
        out_specs=[qspec, row, row, qspec, row, row],
        out_shape=[_sds((N_HEADS, LANES, m), BF16), _sds((m, LANES), BF16), _sds((m, LANES), BF16),
                   _sds((N_HEADS, LANES, m), BF16), _sds((m, LANES), BF16), _sds((m, LANES), BF16)],
        sem=("parallel",),
    )(proj, cos, sin, qg, kg)


def _prep_bwd(dqa, dka, dva, dqb, dkb, dvb, proj, cos, sin, qg, kg, dgl, *, tm, name):
    m = proj.shape[0]

    def body(dqa_ref, dka_ref, dva_ref, dqb_ref, dkb_ref, dvb_ref, p_ref, cos_ref, sin_ref, qg_ref, kg_ref,
             dgl_ref, dp_ref, dqg_ref, dkg_ref):
        i = pl.program_id(0)
        lane = _lane((tm, LANES))
        cosv, sinv = cos_ref[...], sin_ref[...]
        low = lane < HEAD_DIM

        @pl.when(i == 0)
        def _():
            dqg_ref[...] = jnp.zeros(dqg_ref.shape, F32)
            dkg_ref[...] = jnp.zeros(dkg_ref.shape, F32)

        def unrope(d):
            return d * cosv - _rope_partner(d, lane) * sinv

        def unplace(dq_ref, j):
            d0, d1 = dq_ref[2 * j].T, dq_ref[2 * j + 1].T
            if j < 2:
                return jnp.where(low, d0, pltpu.roll(d1, HEAD_DIM, 1))
            return jnp.where(low, pltpu.roll(d0, HEAD_DIM, 1), d1)

        def unrms(dtn, t, g):
            r = lax.rsqrt(_half_mean(t * t, lane) + QK_EPS)
            u = dtn * g
            dt = r * u - t * (r * r * r) * _half_mean(u * t, lane)
            return dt, jnp.sum(dtn * t * r, axis=0, keepdims=True)

        for j in range(4):
            dp_ref[:, j * LANES:(j + 1) * LANES] = (unrope(unplace(dqa_ref, j)) * Q_SCALE).astype(BF16)
            c0 = 768 + j * LANES
            dt, dg = unrms(unrope(unplace(dqb_ref, j)) * Q_SCALE, p_ref[:, c0:c0 + LANES], qg_ref[...])
            dp_ref[:, c0:c0 + LANES] = dt.astype(BF16)
            dqg_ref[:, j * LANES:(j + 1) * LANES] += dg
        dp_ref[:, 512:640] = unrope(dka_ref[...]).astype(BF16)
        dp_ref[:, 640:768] = dva_ref[...].astype(BF16)
        dt, dg = unrms(unrope(dkb_ref[...]), p_ref[:, 1280:1408], kg_ref[...])
        dp_ref[:, 1280:1408] = dt.astype(BF16)
        dkg_ref[...] += dg
        dp_ref[:, 1408:1536] = dvb_ref[...].astype(BF16)
        dp_ref[:, OFF_GA:] = dgl_ref[...]

    row = pl.BlockSpec((tm, LANES), lambda i: (i, 0))
    qspec = pl.BlockSpec((N_HEADS, LANES, tm), lambda i: (0, 0, i))
    return _call(
        body, name=name, grid=(m // tm,),
        in_specs=[qspec, row, row, qspec, row, row, pl.BlockSpec((tm, OFF_GA), lambda i: (i, 0)), row, row,
                  _full((1, LANES)), _full((1, LANES)), pl.BlockSpec((tm, IN_COLS - OFF_GA), lambda i: (i, 0))],
        out_specs=[pl.BlockSpec((tm, IN_COLS), lambda i: (i, 0)), _full((1, 512)), _full((1, LANES))],
        out_shape=[_sds((m, IN_COLS), BF16), _sds((1, 512)), _sds((1, LANES))],
        sem=("arbitrary",),
    )(dqa, dka, dva, dqb, dkb, dvb, proj, cos, sin, qg, kg, dgl)


def _attn_glob_fwd(qt, k, vt, kc, vct, *, tq, tk):
    nh, _, s = qt.shape
    nc = kc.shape[0]
    n_chunks = s // tk
    assert n_chunks % 2 == 0, "the staging slots alternate: the last chunk must use slot 1"
    half = LANES // 2

    def body(qt_ref, qn_ref, k_ref, vt_ref, kc_ref, vct_ref, ot_ref, lse_ref, mrun_ref, p_hbm,
             acc_sc, st_sc, stage_sc, stagec_sc, sems, semc):
        h, i = pl.program_id(0), pl.program_id(1)
        qtv = qt_ref[...]
        acc_sc[...] = jnp.zeros(acc_sc.shape, F32)

        def p_out(slot, c):
            return pltpu.make_async_copy(stage_sc.at[slot], p_hbm.at[h, i, pl.ds(pl.multiple_of(c * tk, tk), tk), :],
                                         sems.at[slot])

        def update(st, vtv, m_old):
            m_new = jnp.maximum(m_old, jnp.max(st, axis=0, keepdims=True))
            pb = jnp.exp(st - m_new).astype(BF16)
            acc_sc[...] = acc_sc[...] * jnp.exp(m_old - m_new) + _dot(vtv, pb)
            return m_new, pb

        m, pbc = update(_dot(kc_ref[...], qtv), vct_ref[...], jnp.full((1, tq), NEG, F32))
        mrun_ref[pl.ds(n_chunks, 1), :] = m
        stagec_sc[...] = pbc
        ctx_out = pltpu.make_async_copy(stagec_sc, p_hbm.at[h, i, pl.ds(s, nc), :], semc)
        ctx_out.start()

        def step(c, m_old, k_next, q_next):
            slot = c % 2
            off = pl.multiple_of(c * tk, tk)
            st = st_sc[...]
            st_next = _dot(k_next, q_next)
            m_new, pb = update(st, vt_ref[:, pl.ds(off, tk)], m_old)
            mrun_ref[pl.ds(c, 1), :] = m_new
            stage_sc[slot] = pb
            p_out(slot, c).start()
            p_out(1 - slot, c - 1).wait()
            st_sc[...] = st_next
            return m_new

        def loop(c, m_old):
            return step(c, m_old, k_ref[pl.ds(pl.multiple_of((c + 1) * tk, tk), tk), :], qtv)

        @pl.when(jnp.logical_and(h == 0, i == 0))
        def _():
            stage_sc[1] = jnp.zeros((tk, tq), BF16)
            pltpu.make_async_copy(stage_sc.at[1], p_hbm.at[h, i, pl.ds(s + nc, tk), :], sems.at[1]).start()
            st_sc[...] = _dot(k_ref[pl.ds(0, tk), :], qtv)

        m = lax.fori_loop(0, n_chunks - 1, loop, m)
        m = step(n_chunks - 1, m, k_ref[pl.ds(0, tk), :], qn_ref[...])

        @pl.when(jnp.logical_and(h == nh - 1, i == n_q - 1))
        def _():
            p_out(1, n_chunks - 1).wait()
        ctx_out.wait()
        acc = acc_sc[...]
        l = jnp.where(h < nh // N_KV, acc[half:half + 1], acc[0:1])
        ot_ref[...] = (acc / l).astype(BF16)
        lse_ref[...] = m + jnp.log(l)

    grp = nh // N_KV
    n_q = s // tq
    return _call(
        body, name="attn_glob_fwd", grid=(nh, n_q),
        in_specs=[pl.BlockSpec((None, LANES, tq), lambda h, i: (h, 0, i)),
                  pl.BlockSpec((None, LANES, tq), lambda h, i: (jnp.minimum(h + (i + 1) // n_q, nh - 1), 0, (i + 1) % n_q)),
                  _full((s, LANES)),
                  pl.BlockSpec((None, LANES, s), lambda h, i: (h // grp, 0, 0)), _full((nc, LANES)),
                  pl.BlockSpec((None, LANES, nc), lambda h, i: (h // grp, 0, 0))],
        out_specs=[pl.BlockSpec((None, LANES, tq), lambda h, i: (h, 0, i)),
                   pl.BlockSpec((None, 1, tq), lambda h, i: (h, 0, i)),
                   pl.BlockSpec((None, n_chunks + 1, tq), lambda h, i: (h, 0, i)), ANY],
        out_shape=[_sds((nh, LANES, s), BF16), _sds((nh, 1, s)), _sds((nh, n_chunks + 1, s)),
                   _sds((nh, s // tq, s + nc + tk, tq), BF16)],
        scratch=[pltpu.VMEM((LANES, tq), F32), pltpu.VMEM((tk, tq), F32), pltpu.VMEM((2, tk, tq), BF16),
                 pltpu.VMEM((nc, tq), BF16), pltpu.SemaphoreType.DMA((2,)), pltpu.SemaphoreType.DMA],
        sem=("arbitrary", "arbitrary"),
    )(qt, qt, k, vt, kc, vct)


P_AHEAD = 3


def _attn_glob_bwd(qt, dot, ot, lse, mrun, p, kt, v, kct, vc, *, tq, tk):
    nh, _, s = qt.shape
    nc = vc.shape[0]
    n_q = s // tq
    n_chunks = s // tk
    n_run = mrun.shape[1] - 1
    per_run = n_chunks // n_run

    def body(qt_ref, dot_ref, ot_ref, lse_ref, mrun_ref, p_hbm, kt_ref, v_ref, kct_ref, vc_ref,
             dqt_ref, dkt_ref, dvt_ref, dkct_ref, dvct_ref, acc_sc, dp_sc, dkt_sc, dvt_sc, p_sc, pc_sc, sems, semc):
        h, i = pl.program_id(0), pl.program_id(1)

        @pl.when(jnp.logical_and(h == 0, i == 0))
        def _():
            dkct_ref[...] = jnp.zeros(dkct_ref.shape, F32)
            dvct_ref[...] = jnp.zeros(dvct_ref.shape, F32)
            dkt_sc[...] = jnp.zeros(dkt_sc.shape, F32)
            dvt_sc[...] = jnp.zeros(dvt_sc.shape, F32)


        def p_in(slot, c):
            return pltpu.make_async_copy(p_hbm.at[h, i, pl.ds(pl.multiple_of(c * tk, tk), tk), :], p_sc.at[slot],
                                         sems.at[slot])

        ctx_in = pltpu.make_async_copy(p_hbm.at[h, i, pl.ds(s, nc), :], pc_sc, semc)
        ctx_in.start()
        for c in range(P_AHEAD):
            p_in(c, min(c, n_chunks - 1)).start()
        qtv, dotv, lse = qt_ref[...], dot_ref[...], lse_ref[...]
        delta = jnp.sum(dotv.astype(F32) * ot_ref[...].astype(F32), axis=0, keepdims=True)

        def grads(pt_stored, m_row, dpt):
            pt = pt_stored.astype(F32) * jnp.exp(m_row - lse)
            return pt.astype(BF16), (pt * (dpt - delta)).astype(BF16)

        dp_sc[...] = _dot(v_ref[pl.ds(0, tk), :], dotv)
        ctx_in.wait()
        pb, dsb = grads(pc_sc[...], mrun_ref[pl.ds(n_run, 1), :], _dot(vc_ref[...], dotv))
        acc_sc[...] = _dot(kct_ref[...], dsb)
        dkct_ref[...] += _dot_nt(qtv, dsb)
        dvct_ref[...] += _dot_nt(dotv, pb)

        def loop(c, carry):
            slot = c % (P_AHEAD + 1)
            off = pl.multiple_of(c * tk, tk)
            nxt = pl.multiple_of(jnp.minimum(c + 1, n_chunks - 1) * tk, tk)
            p_in(slot, c).wait()
            p_in((c + P_AHEAD) % (P_AHEAD + 1), jnp.minimum(c + P_AHEAD, n_chunks - 1)).start()
            dpt = dp_sc[...]
            dp_next = _dot(v_ref[pl.ds(nxt, tk), :], dotv)
            pb, dsb = grads(p_sc[slot], mrun_ref[pl.ds(c // per_run, 1), :], dpt)
            acc_sc[...] += _dot(kt_ref[:, pl.ds(off, tk)], dsb)
            dkt_sc[:, pl.ds(off, tk)] += _dot_nt(qtv, dsb)
            dvt_sc[:, pl.ds(off, tk)] += _dot_nt(dotv, pb)
            dp_sc[...] = dp_next
            return carry

        lax.fori_loop(0, n_chunks, loop, 0)
        for c in range(n_chunks, n_chunks + P_AHEAD):
            p_in(c % (P_AHEAD + 1), n_chunks - 1).wait()
        dqt_ref[...] = acc_sc[...]

        @pl.when(jnp.logical_and(h == nh - 1, i == n_q - 1))
        def _():
            pltpu.sync_copy(dkt_sc, dkt_ref)
            pltpu.sync_copy(dvt_sc, dvt_ref)

    qs = pl.BlockSpec((None, LANES, tq), lambda h, i: (h, 0, i))
    rs = pl.BlockSpec((None, 1, tq), lambda h, i: (h, 0, i))
    return _call(
        body, name="attn_glob_bwd", grid=(nh, n_q),
        in_specs=[qs, qs, qs, rs, pl.BlockSpec((None, n_run + 1, tq), lambda h, i: (h, 0, i)), ANY,
                  _full((LANES, s)), _full((s, LANES)), _full((LANES, nc)), _full((nc, LANES))],
        out_specs=[qs, ANY, ANY, _full((LANES, nc)), _full((LANES, nc))],
        out_shape=[_sds((nh, LANES, s)), _sds((LANES, s)), _sds((LANES, s)), _sds((LANES, nc)), _sds((LANES, nc))],
        scratch=[pltpu.VMEM((LANES, tq), F32), pltpu.VMEM((tk, tq), F32), pltpu.VMEM((LANES, s), F32),
                 pltpu.VMEM((LANES, s), F32), pltpu.VMEM((P_AHEAD + 1, tk, tq), BF16), pltpu.VMEM((nc, tq), BF16),
                 pltpu.SemaphoreType.DMA((P_AHEAD + 1,)), pltpu.SemaphoreType.DMA],
        sem=("arbitrary", "arbitrary"),
    )(qt, dot, ot, lse, mrun, p, kt, v, kct, vc)


WIN_SPAN = 2 * WINDOW


def _band(rows0, cols0, shape):
    r = rows0 + lax.broadcasted_iota(jnp.int32, shape, 0)
    c = cols0 + lax.broadcasted_iota(jnp.int32, shape, 1)
    return jnp.abs(r - c) <= WINDOW


def _win_start(blk, t, s):
    return pl.multiple_of(jnp.clip(blk * t - WINDOW, 0, s - t - WIN_SPAN), WINDOW)


def _attn_win_fwd(qt, k, vt, kc, vct, sink, *, tq):
    nh, _, s = qt.shape
    nc = kc.shape[0]
    tw = tq + WIN_SPAN
    half = LANES // 2
    grp = nh // N_KV

    def body(sink_ref, qt_ref, k_ref, vt_ref, kc_ref, vct_ref, ot_ref, lse_ref):
        h, i = pl.program_id(0), pl.program_id(1)
        k0 = _win_start(i, tq, s)
        qtv = qt_ref[...]
        st = jnp.where(_band(k0, i * tq, (tw, tq)), _dot(k_ref[pl.ds(k0, tw), :], qtv), NEG)
        stc = _dot(kc_ref[...], qtv)
        snk = sink_ref[h]
        m = jnp.maximum(jnp.maximum(jnp.max(st, axis=0, keepdims=True), jnp.max(stc, axis=0, keepdims=True)), snk)
        acc = (_dot(vt_ref[:, pl.ds(k0, tw)], jnp.exp(st - m).astype(BF16))
               + _dot(vct_ref[...], jnp.exp(stc - m).astype(BF16)))
        l = jnp.where(h < grp, acc[half:half + 1], acc[0:1]) + jnp.exp(snk - m)
        ot_ref[...] = (acc / l).astype(BF16)
        lse_ref[...] = m + jnp.log(l)

    return _call(
        body, name="attn_win_fwd", grid=(nh, s // tq),
        in_specs=[pl.BlockSpec(memory_space=pltpu.SMEM),
                  pl.BlockSpec((None, LANES, tq), lambda h, i: (h, 0, i)), _full((s, LANES)),
                  pl.BlockSpec((None, LANES, s), lambda h, i: (h // grp, 0, 0)), _full((nc, LANES)),
                  pl.BlockSpec((None, LANES, nc), lambda h, i: (h // grp, 0, 0))],
        out_specs=[pl.BlockSpec((None, LANES, tq), lambda h, i: (h, 0, i)),
                   pl.BlockSpec((None, 1, tq), lambda h, i: (h, 0, i))],
        out_shape=[_sds((nh, LANES, s), BF16), _sds((nh, 1, s))],
        sem=("parallel", "parallel"),
    )(sink, qt, k, vt, kc, vct)


def _attn_win_bwd(qt, dot, ot, lse, k, kt, v, kc, kct, vc, sink, *, tq):
    nh, _, s = qt.shape
    nc = kc.shape[0]
    tw = tq + WIN_SPAN
    nq = s // tq

    def body(sink_ref, qt_ref, dot_ref, ot_ref, lse_ref, k_ref, kt_ref, v_ref, kc_ref, kct_ref, vc_ref,
             dqt_ref, dkt_ref, dvt_ref, dkct_ref, dvct_ref, dsk_ref, dkt_sc, dvt_sc):
        h, i = pl.program_id(0), pl.program_id(1)

        @pl.when(jnp.logical_and(h == 0, i == 0))
        def _():
            dkct_ref[...] = jnp.zeros(dkct_ref.shape, F32)
            dvct_ref[...] = jnp.zeros(dvct_ref.shape, F32)
            dkt_sc[...] = jnp.zeros(dkt_sc.shape, F32)
            dvt_sc[...] = jnp.zeros(dvt_sc.shape, F32)

        k0 = _win_start(i, tq, s)
        span = pl.ds(k0, tw)
        qtv, dotv, lse = qt_ref[...], dot_ref[...], lse_ref[...]
        delta = jnp.sum(dotv.astype(F32) * ot_ref[...].astype(F32), axis=0, keepdims=True)
        pt = jnp.where(_band(k0, i * tq, (tw, tq)), jnp.exp(_dot(k_ref[span, :], qtv) - lse), 0.0)
        dsb = (pt * (_dot(v_ref[span, :], dotv) - delta)).astype(BF16)
        pct = jnp.exp(_dot(kc_ref[...], qtv) - lse)
        dscb = (pct * (_dot(vc_ref[...], dotv) - delta)).astype(BF16)
        dqt_ref[...] = _dot(kt_ref[:, span], dsb) + _dot(kct_ref[...], dscb)
        dkt_sc[:, span] += _dot_nt(qtv, dsb)
        dvt_sc[:, span] += _dot_nt(dotv, pt.astype(BF16))
        dkct_ref[...] += _dot_nt(qtv, dscb)
        dvct_ref[...] += _dot_nt(dotv, pct.astype(BF16))
        dsk = -jnp.sum(jnp.exp(sink_ref[h] - lse) * delta)
        dsk_ref[...] = jnp.full(dsk_ref.shape, dsk, F32)

        @pl.when(jnp.logical_and(h == nh - 1, i == nq - 1))
        def _():
            pltpu.sync_copy(dkt_sc, dkt_ref)
            pltpu.sync_copy(dvt_sc, dvt_ref)

    qs = pl.BlockSpec((None, LANES, tq), lambda h, i: (h, 0, i))
    rs = pl.BlockSpec((None, 1, tq), lambda h, i: (h, 0, i))
    return _call(
        body, name="attn_win_bwd", grid=(nh, nq),
        in_specs=[pl.BlockSpec(memory_space=pltpu.SMEM), qs, qs, qs, rs, _full((s, LANES)), _full((LANES, s)),
                  _full((s, LANES)), _full((nc, LANES)), _full((LANES, nc)), _full((nc, LANES))],
        out_specs=[qs, ANY, ANY, _full((LANES, nc)), _full((LANES, nc)),
                   pl.BlockSpec((None, None, 8, LANES), lambda h, i: (h, i, 0, 0))],
        out_shape=[_sds((nh, LANES, s)), _sds((LANES, s)), _sds((LANES, s)), _sds((LANES, nc)), _sds((LANES, nc)),
                   _sds((nh, nq, 8, LANES))],
        scratch=[pltpu.VMEM((LANES, s), F32), pltpu.VMEM((LANES, s), F32)],
        sem=("arbitrary", "arbitrary"),
    )(sink, qt, dot, ot, lse, k, kt, v, kc, kct, vc)


def _ln_fwd(z, g, b):
    mu = jnp.mean(z, axis=-1, keepdims=True)
    zc = z - mu
    r = lax.rsqrt(jnp.mean(zc * zc, axis=-1, keepdims=True) + LN_EPS)
    return zc * r * g + b, mu, r


def _ln_bwd(dy, xhat, r, g):
    dxh = dy * g
    return r * (dxh - jnp.mean(dxh, axis=-1, keepdims=True) - xhat * jnp.mean(dxh * xhat, axis=-1, keepdims=True))


def _heads_matmul(ot_ref, w_ref):
    acc = _dot_tn(ot_ref[0], w_ref[0])
    for h in range(1, N_HEADS):
        acc += _dot_tn(ot_ref[h], w_ref[h])
    return acc


def _gate_specs(tm):
    return [pl.BlockSpec((tm, 512), functools.partial(lambda i, b: (i, b), b=OFF_GA // 512 + b)) for b in range(4)]


def _merge_fwd(oat, obt, proj, x, gate1, wba, wbb, w_out, ln_g, ln_b, *, tm):
    s = x.shape[0]

    def body(oa_ref, ob_ref, g0, g1, g2, g3, x_ref, gt_ref, wba_ref, wbb_ref, wo_ref, lg_ref, lb_ref,
             x1_ref, y_ref, mu_ref, r_ref, pa_ref, pb_ref, mg_ref):
        ga = _sigmoid(jnp.concatenate([g0[...], g1[...]], axis=1))
        gb = _sigmoid(jnp.concatenate([g2[...], g3[...]], axis=1))
        pa, pb = _heads_matmul(oa_ref, wba_ref), _heads_matmul(ob_ref, wbb_ref)
        merged = (ga * pa + gb * pb).astype(BF16)
        y = _dot(merged, wo_ref[...])
        x1, mu, r = _ln_fwd(ALPHA * x_ref[...] + gt_ref[...] * y, lg_ref[...], lb_ref[...])
        x1_ref[...] = x1
        y_ref[...] = y
        mu_ref[...] = mu
        r_ref[...] = r
        pa_ref[...] = pa.astype(BF16)
        pb_ref[...] = pb.astype(BF16)
        mg_ref[...] = merged

    hts = pl.BlockSpec((N_HEADS, LANES, tm), lambda i: (0, 0, i))
    row = pl.BlockSpec((tm, D_MODEL), lambda i: (i, 0))
    col = pl.BlockSpec((tm, 1), lambda i: (i, 0))
    vec = _full((1, D_MODEL))
    wh = _full((N_HEADS, LANES, D_MODEL))
    return _call(
        body, name="merge_fwd", grid=(s // tm,),
        in_specs=[hts, hts, *_gate_specs(tm), row, vec, wh, wh, _full((D_MODEL, D_MODEL)), vec, vec],
        out_specs=[row, row, col, col, row, row, row],
        out_shape=[_sds((s, D_MODEL)), _sds((s, D_MODEL)), _sds((s, 1)), _sds((s, 1)), _sds((s, D_MODEL), BF16),
                   _sds((s, D_MODEL), BF16), _sds((s, D_MODEL), BF16)],
        sem=("parallel",),
    )(oat, obt, proj, proj, proj, proj, x, gate1, wba, wbb, w_out, ln_g, ln_b)


def _merge_bwd(dy, oat, obt, pa, pb, proj, wba, wbb, w_out, *, tm):
    s = dy.shape[0]

    def body(dy_ref, oat_ref, obt_ref, pa_ref, pb_ref, g0, g1, g2, g3, wba_ref, wbb_ref, wo_ref,
             dgl_ref, doat_ref, dobt_ref, dwa_ref, dwb_ref):
        @pl.when(pl.program_id(0) == 0)
        def _():
            dwa_ref[...] = jnp.zeros(dwa_ref.shape, F32)
            dwb_ref[...] = jnp.zeros(dwb_ref.shape, F32)

        dm = _dot_nt(dy_ref[...], wo_ref[...])
        ga = _sigmoid(jnp.concatenate([g0[...], g1[...]], axis=1))
        gb = _sigmoid(jnp.concatenate([g2[...], g3[...]], axis=1))
        pa, pb = pa_ref[...].astype(F32), pb_ref[...].astype(F32)
        dgl_ref[:, :D_MODEL] = (dm * pa * ga * (1.0 - ga)).astype(BF16)
        dgl_ref[:, D_MODEL:] = (dm * pb * gb * (1.0 - gb)).astype(BF16)
        dpa, dpb = (dm * ga).astype(BF16), (dm * gb).astype(BF16)
        for h in range(N_HEADS):
            doat_ref[h] = _dot_nt(wba_ref[h], dpa).astype(BF16)
            dobt_ref[h] = _dot_nt(wbb_ref[h], dpb).astype(BF16)
            dwa_ref[h] += _dot(oat_ref[h], dpa)
            dwb_ref[h] += _dot(obt_ref[h], dpb)

    hts = pl.BlockSpec((N_HEADS, LANES, tm), lambda i: (0, 0, i))
    row = pl.BlockSpec((tm, D_MODEL), lambda i: (i, 0))
    wh = _full((N_HEADS, LANES, D_MODEL))
    return _call(
        body, name="merge_bwd", grid=(s // tm,),
        in_specs=[row, hts, hts, row, row, *_gate_specs(tm), wh, wh, _full((D_MODEL, D_MODEL))],
        out_specs=[pl.BlockSpec((tm, 2 * D_MODEL), lambda i: (i, 0)), hts, hts, wh, wh],
        out_shape=[_sds((s, 2 * D_MODEL), BF16), _sds((N_HEADS, LANES, s), BF16), _sds((N_HEADS, LANES, s), BF16),
                   _sds((N_HEADS, LANES, D_MODEL)), _sds((N_HEADS, LANES, D_MODEL))],
        sem=("arbitrary",),
    )(dy, oat, obt, pa, pb, proj, proj, proj, proj, wba, wbb, w_out)


FF_TC = 256


def _shift_rows(t, prev_row, next_row):
    n = t.shape[0]
    r = lax.broadcasted_iota(jnp.int32, t.shape, 0)
    up = jnp.where(r == 0, prev_row, pltpu.roll(t, 1, 0))
    dn = jnp.where(r == n - 1, next_row, pltpu.roll(t, n - 1, 0))
    return up, dn


HALO = 16


def _halo_specs(tm, s, tc):
    nb = s // HALO
    main = pl.BlockSpec((2, tm, tc), lambda j, i: (0, i, j))
    prev = pl.BlockSpec((2, HALO, tc), lambda j, i: (0, jnp.maximum(i * (tm // HALO) - 1, 0), j))
    nxt = pl.BlockSpec((2, HALO, tc), lambda j, i: (0, jnp.minimum((i + 1) * (tm // HALO), nb - 1), j))
    return main, prev, nxt


def _halo_rows(prev_ref, next_ref, half, i, n_i):
    prev_row = jnp.where(i == 0, 0.0, prev_ref[half, HALO - 1:HALO, :].astype(F32))
    next_row = jnp.where(i == n_i - 1, 0.0, next_ref[half, 0:1, :].astype(F32))
    return prev_row, next_row


def _conv(t, prev_row, next_row, w, b):
    up, dn = _shift_rows(t, prev_row, next_row)
    return w[0:1, :] * up + w[1:2, :] * t + w[2:3, :] * dn + b


def _ffn_act_fwd(u, cw, cb, *, tm):
    _, s, ff = u.shape
    n_i = s // tm

    def body(u_ref, up_ref, un_ref, cw_ref, cb_ref, a_ref):
        i = pl.program_id(1)
        gc = _conv(u_ref[0].astype(F32), *_halo_rows(up_ref, un_ref, 0, i, n_i), cw_ref[0], cb_ref[0])
        vc = _conv(u_ref[1].astype(F32), *_halo_rows(up_ref, un_ref, 1, i, n_i), cw_ref[1], cb_ref[1])
        a_ref[...] = (gc * _sigmoid(gc) * vc).astype(BF16)

    main, prev, nxt = _halo_specs(tm, s, FF_TC)
    return _call(
        body, name="ffn_act_fwd", grid=(ff // FF_TC, n_i),
        in_specs=[main, prev, nxt, pl.BlockSpec((2, 3, FF_TC), lambda j, i: (0, 0, j)),
                  pl.BlockSpec((2, 1, FF_TC), lambda j, i: (0, 0, j))],
        out_specs=pl.BlockSpec((tm, FF_TC), lambda j, i: (i, j)),
        out_shape=_sds((s, ff), BF16), sem=("parallel", "parallel"),
    )(u, u, u, cw, cb)


def _ffn_act_bwd(dy2, w_down, u, cw, cb, *, tm):
    _, s, ff = u.shape
    n_i = s // tm

    def body(dy_ref, wd_ref, u_ref, up_ref, un_ref, cw_ref, cb_ref, dc_ref, dcw_ref, dcb_ref):
        i = pl.program_id(1)

        @pl.when(i == 0)
        def _():
            dcw_ref[...] = jnp.zeros(dcw_ref.shape, F32)
            dcb_ref[...] = jnp.zeros(dcb_ref.shape, F32)

        da = _dot_nt(dy_ref[...], wd_ref[...])
        ug, uv = u_ref[0].astype(F32), u_ref[1].astype(F32)
        ugp, ugn = _shift_rows(ug, *_halo_rows(up_ref, un_ref, 0, i, n_i))
        uvp, uvn = _shift_rows(uv, *_halo_rows(up_ref, un_ref, 1, i, n_i))
        wg, wv = cw_ref[0], cw_ref[1]
        gc = wg[0:1, :] * ugp + wg[1:2, :] * ug + wg[2:3, :] * ugn + cb_ref[0]
        vc = wv[0:1, :] * uvp + wv[1:2, :] * uv + wv[2:3, :] * uvn + cb_ref[1]
        sg = _sigmoid(gc)
        dg = da * vc * sg * (1.0 + gc * (1.0 - sg))
        dv = da * gc * sg
        dc_ref[0] = dg.astype(BF16)
        dc_ref[1] = dv.astype(BF16)
        for half, (d, taps) in enumerate(((dg, (ugp, ug, ugn)), (dv, (uvp, uv, uvn)))):
            for tap in range(3):
                dcw_ref[half, tap:tap + 1, :] += jnp.sum(d * taps[tap], axis=0, keepdims=True)
            dcb_ref[half] += jnp.sum(d, axis=0, keepdims=True)

    main, prev, nxt = _halo_specs(tm, s, FF_TC)
    return _call(
        body, name="ffn_act_bwd", grid=(ff // FF_TC, n_i),
        in_specs=[pl.BlockSpec((tm, D_MODEL), lambda j, i: (i, 0)), pl.BlockSpec((FF_TC, D_MODEL), lambda j, i: (j, 0)),
                  main, prev, nxt, pl.BlockSpec((2, 3, FF_TC), lambda j, i: (0, 0, j)),
                  pl.BlockSpec((2, 1, FF_TC), lambda j, i: (0, 0, j))],
        out_specs=[main, pl.BlockSpec((2, 3, FF_TC), lambda j, i: (0, 0, j)),
                   pl.BlockSpec((2, 1, FF_TC), lambda j, i: (0, 0, j))],
        out_shape=[_sds((2, s, ff), BF16), _sds((2, 3, ff)), _sds((2, 1, ff))],
        sem=("parallel", "arbitrary"),
    )(dy2, w_down, u, u, u, cw, cb)


def _conv_bwd_input(dc, cw, *, tm):
    _, s, ff = dc.shape
    n_i = s // tm

    def body(d_ref, dp_ref, dn_ref, cw_ref, du_ref):
        i = pl.program_id(1)
        for half in range(2):
            d = d_ref[half].astype(F32)
            up, dn = _shift_rows(d, *_halo_rows(dp_ref, dn_ref, half, i, n_i))
            w = cw_ref[half]
            du_ref[half] = (w[0:1, :] * dn + w[1:2, :] * d + w[2:3, :] * up).astype(BF16)

    main, prev, nxt = _halo_specs(tm, s, FF_TC)
    return _call(
        body, name="conv_bwd_input", grid=(ff // FF_TC, n_i),
        in_specs=[main, prev, nxt, pl.BlockSpec((2, 3, FF_TC), lambda j, i: (0, 0, j))],
        out_specs=main, out_shape=_sds((2, s, ff), BF16), sem=("parallel", "parallel"),
    )(dc, dc, dc, cw)


def _ffn_down_loss(a, w_down, x1, target, gate2, ln_g, ln_b, *, tm):
    s, ff = a.shape
    n_i = s // tm

    def body(a_ref, wd_ref, x1_ref, tg_ref, gt_ref, lg_ref, lb_ref, ls_ref, dy_ref, dx_ref, dg_ref, db_ref, dgt_ref):
        @pl.when(pl.program_id(0) == 0)
        def _():
            dg_ref[...] = jnp.zeros(dg_ref.shape, F32)
            db_ref[...] = jnp.zeros(db_ref.shape, F32)
            dgt_ref[...] = jnp.zeros(dgt_ref.shape, F32)

        y2 = _dot(a_ref[...], wd_ref[...])
        z = ALPHA * x1_ref[...] + gt_ref[...] * y2
        mu = jnp.mean(z, axis=-1, keepdims=True)
        zc = z - mu
        r = lax.rsqrt(jnp.mean(zc * zc, axis=-1, keepdims=True) + LN_EPS)
        xhat = zc * r
        diff = xhat * lg_ref[...] + lb_ref[...] - tg_ref[...]
        ls_ref[...] = jnp.full(ls_ref.shape, 0.5 / D_MODEL * jnp.sum(diff * diff), F32)
        dx2 = diff * (1.0 / D_MODEL)
        dg_ref[...] += jnp.sum(dx2 * xhat, axis=0, keepdims=True)
        db_ref[...] += jnp.sum(dx2, axis=0, keepdims=True)
        dz = _ln_bwd(dx2, xhat, r, lg_ref[...])
        dgt_ref[...] += jnp.sum(dz * y2, axis=0, keepdims=True)
        dy_ref[...] = (gt_ref[...] * dz).astype(BF16)
        dx_ref[...] = ALPHA * dz

    row = pl.BlockSpec((tm, D_MODEL), lambda i: (i, 0))
    vec = _full((1, D_MODEL))
    return _call(
        body, name="ffn_down_loss", grid=(n_i,),
        in_specs=[pl.BlockSpec((tm, ff), lambda i: (i, 0)), _full((ff, D_MODEL)), row, row, vec, vec, vec],
        out_specs=[pl.BlockSpec((None, 8, LANES), lambda i: (i, 0, 0)), row, row, vec, vec, vec],
        out_shape=[_sds((n_i, 8, LANES)), _sds((s, D_MODEL), BF16), _sds((s, D_MODEL)),
                   _sds((1, D_MODEL)), _sds((1, D_MODEL)), _sds((1, D_MODEL))],
        sem=("arbitrary",),
    )(a, w_down, x1, target, gate2, ln_g, ln_b)


def _ffn_up_bwd(du, wup4, dx1a, x1, scale2, x, y, mu1, r1, gate1, ln_g, *, tm):
    s = x.shape[0]
    nb, _, ns = wup4.shape

    def body(du_ref, w_ref, dxa_ref, x1_ref, sc_ref, x_ref, y_ref, mu_ref, r_ref, gt_ref, lg_ref,
             dxo_ref, dy_ref, dsc_ref, dsh_ref, dg_ref, db_ref, dgt_ref, acc):
        i, k = pl.program_id(0), pl.program_id(1)

        @pl.when(jnp.logical_and(i == 0, k == 0))
        def _():
            for ref in (dsc_ref, dsh_ref, dg_ref, db_ref, dgt_ref):
                ref[...] = jnp.zeros(ref.shape, F32)

        @pl.when(k == 0)
        def _():
            acc[...] = jnp.zeros(acc.shape, F32)

        acc[...] += _dot_nt(du_ref[...], w_ref[...])

        @pl.when(k == nb - 1)
        def _():
            dh = acc[...]
            x1 = x1_ref[...]
            dsc_ref[...] += jnp.sum(dh * x1, axis=0, keepdims=True)
            dsh_ref[...] += jnp.sum(dh, axis=0, keepdims=True)
            dx1 = dxa_ref[...] + dh * (1.0 + sc_ref[...])
            yv = y_ref[...]
            xhat = (ALPHA * x_ref[...] + gt_ref[...] * yv - mu_ref[...]) * r_ref[...]
            dg_ref[...] += jnp.sum(dx1 * xhat, axis=0, keepdims=True)
            db_ref[...] += jnp.sum(dx1, axis=0, keepdims=True)
            dz = _ln_bwd(dx1, xhat, r_ref[...], lg_ref[...])
            dgt_ref[...] += jnp.sum(dz * yv, axis=0, keepdims=True)
            dy_ref[...] = (gt_ref[...] * dz).astype(BF16)
            dxo_ref[...] = ALPHA * dz

    row = pl.BlockSpec((tm, D_MODEL), lambda i, k: (i, 0))
    col = pl.BlockSpec((tm, 1), lambda i, k: (i, 0))
    vec = _full((1, D_MODEL))
    return _call(
        body, name="ffn_up_bwd", grid=(s // tm, nb),
        in_specs=[pl.BlockSpec((None, tm, ns), lambda i, k: (k // 2, i, k % 2)),
                  pl.BlockSpec((None, D_MODEL, ns), lambda i, k: (k, 0, 0)),
                  row, row, vec, row, row, col, col, vec, vec],
        out_specs=[row, row, vec, vec, vec, vec, vec],
        out_shape=[_sds((s, D_MODEL)), _sds((s, D_MODEL), BF16)] + [_sds((1, D_MODEL))] * 5,
        scratch=[pltpu.VMEM((tm, D_MODEL), F32)],
        sem=("arbitrary", "arbitrary"),
    )(du, wup4, dx1a, x1, scale2, x, y, mu1, r1, gate1, ln_g)


def _mm_nt4_mod_bwd(dp, w4, dxa, x, scale, *, tm, name):
    m = x.shape[0]
    nb, kdim, ns = w4.shape

    def body(dp_ref, w_ref, dxa_ref, x_ref, sc_ref, dx_ref, dsc_ref, dsh_ref, acc):
        i, k = pl.program_id(0), pl.program_id(1)

        @pl.when(jnp.logical_and(i == 0, k == 0))
        def _():
            dsc_ref[...] = jnp.zeros(dsc_ref.shape, F32)
            dsh_ref[...] = jnp.zeros(dsh_ref.shape, F32)

        @pl.when(k == 0)
        def _():
            acc[...] = jnp.zeros(acc.shape, F32)

        acc[...] += _dot_nt(dp_ref[...], w_ref[...])

        @pl.when(k == nb - 1)
        def _():
            dh = acc[...]
            dsc_ref[...] += jnp.sum(dh * x_ref[...], axis=0, keepdims=True)
            dsh_ref[...] += jnp.sum(dh, axis=0, keepdims=True)
            dx_ref[...] = dxa_ref[...] + dh * (1.0 + sc_ref[...])

    row = pl.BlockSpec((tm, kdim), lambda i, k: (i, 0))
    vec = _full((1, kdim))
    return _call(
        body, name=name, grid=(m // tm, nb),
        in_specs=[pl.BlockSpec((tm, ns), lambda i, k: (i, k)), pl.BlockSpec((None, kdim, ns), lambda i, k: (k, 0, 0)),
                  row, row, vec],
        out_specs=[row, vec, vec],
        out_shape=[_sds((m, kdim)), _sds((1, kdim)), _sds((1, kdim))],
        scratch=[pltpu.VMEM((tm, kdim), F32)],
        sem=("arbitrary", "arbitrary"),
    )(dp, w4, dxa, x, scale)


def _pad_heads_w(w):
    w8 = w.reshape(N_HEADS, HEAD_DIM, w.shape[-1])
    z = jnp.zeros_like(w8)
    first = (jnp.arange(N_HEADS) < N_HEADS // N_KV)[:, None, None]
    return jnp.where(first, jnp.concatenate([w8, z], axis=1), jnp.concatenate([z, w8], axis=1))


def _unpad_heads_w(g):
    first = (jnp.arange(N_HEADS) < N_HEADS // N_KV)[:, None, None]
    return jnp.where(first, g[:, :HEAD_DIM], g[:, HEAD_DIM:]).reshape(N_HEADS * HEAD_DIM, g.shape[-1])


def _ones_beside(vt):
    half = vt.shape[0] // 2
    ones = jnp.ones((half, vt.shape[1]), vt.dtype)
    return jnp.stack([jnp.concatenate([vt[:half], ones], axis=0), jnp.concatenate([ones, vt[half:]], axis=0)])


def _rep8(a):
    return jnp.broadcast_to(a.reshape(1, -1), (8, a.size))


def _first_row(a):
    r8 = _rep8(a)
    return jnp.where(lax.broadcasted_iota(jnp.int32, r8.shape, 0) == 0, r8, 0.0)


def _to_blocks4(w):
    k, n = w.shape
    return w.reshape(k, N_CHIPS, n // N_CHIPS).transpose(1, 0, 2)


def _local_step(x, c, ctx, c_ctx, wmod4, b_mod, win4, b_in, sink, qn, kn, wba, wbb, w_out, ln1_g, ln1_b,
                wup4, cw, cb, w_down, ln2_g, ln2_b, target):
    s, nc = x.shape[0], ctx.shape[0]
    tm = min(512, s)
    tm2 = min(256, s)
    tl = min(1024, s)
    tx = min(2048, s)
    zvec = jnp.zeros((1, D_MODEL), F32)

    cc = jnp.concatenate([_rep8(c), _rep8(c_ctx)], axis=0)
    mods = _mm_nn4(cc, zvec, zvec, wmod4, b_mod, mode="silu", split_out=False, out_dtype=F32, tm=16, name="mod_vectors")
    shift1, scale1, gate1, shift2, scale2, gate2 = [mods[0:1, i * D_MODEL:(i + 1) * D_MODEL] for i in range(6)]
    shift_c, scale_c = mods[8:9, :D_MODEL], mods[8:9, D_MODEL:2 * D_MODEL]

    cos, sin = _rope_tables(s)
    cos_c, sin_c = jnp.ones((nc, LANES), F32), jnp.zeros((nc, LANES), F32)
    qg, kg = jnp.tile(qn, (1, 2)), jnp.tile(kn, (1, 2))

    proj_c = _mm_nn4(ctx, shift_c, scale_c, win4, b_in, mode="modulate", split_out=False, out_dtype=F32, tm=nc,
                     name="in_proj_ctx")
    _, kac, vac, _, kbc, vbc = _prep(proj_c, cos_c, sin_c, qg, kg, tm=nc, name="prep_ctx")
    proj = _mm_nn4(x, shift1, scale1, win4, b_in, mode="modulate", split_out=False, out_dtype=F32, tm=tx, name="in_proj")
    qat, ka, va, qbt, kb, vb = _prep(proj, cos, sin, qg, kg, tm=tl, name="prep")
    oat, lse_a = _attn_win_fwd(qat, ka, _ones_beside(va.T), kac, _ones_beside(vac.T), sink, tq=tm)
    obt, lse_b, mrun_b, pbt = _attn_glob_fwd(qbt, kb, _ones_beside(vb.T), kbc, _ones_beside(vbc.T), tq=tm,
                                             tk=min(1024, s))
    wba_p, wbb_p = _pad_heads_w(wba), _pad_heads_w(wbb)
    x1, y, mu1, r1, pa, pb, merged = _merge_fwd(oat, obt, proj, x, gate1, wba_p, wbb_p, w_out, ln1_g, ln1_b, tm=tm)
    u = _mm_nn4(x1, shift2, scale2, wup4, jnp.zeros((1, 2 * D_FF), F32), mode="modulate", split_out=True,
                out_dtype=BF16, tm=tx, name="ffn_up")
    cw2 = cw.reshape(3, 2, D_FF).transpose(1, 0, 2)
    cb2 = cb.reshape(2, 1, D_FF)
    a = _ffn_act_fwd(u, cw2, cb2, tm=tx)
    ls, dy2, dx1a, dln2_g, dln2_b, dgate2 = _ffn_down_loss(a, w_down, x1, target, gate2, ln2_g, ln2_b, tm=tm)
    loss = jnp.sum(ls[:, 0, 0])

    n_s = s // tl
    dw_down = _mm_tn(a, dy2, a_spec=pl.BlockSpec((tl, D_FF), lambda t: (t, 0)),
                     b_spec=pl.BlockSpec((tl, D_MODEL), lambda t: (t, 0)), grid=(n_s,),
                     out_shape=_sds((D_FF, D_MODEL)), out_spec=_full((D_FF, D_MODEL)), name="dw_down")
    dc, dcw2, dcb2 = _ffn_act_bwd(dy2, w_down, u, cw2, cb2, tm=tx)
    du = _conv_bwd_input(dc, cw2, tm=tx)
    dxz1, dy, dscale2, dshift2, dln1_g, dln1_b, dgate1 = _ffn_up_bwd(
        du, wup4, dx1a, x1, scale2, x, y, mu1, r1, gate1, ln1_g, tm=tm)
    ns_up = wup4.shape[-1]
    dw_up4 = _mm_tn(x1, du, a_spec=pl.BlockSpec((tx, D_MODEL), lambda k, t: (t, 0)),
                    b_spec=pl.BlockSpec((None, tx, ns_up), lambda k, t: (k // 2, t, k % 2)), grid=(N_CHIPS, s // tx),
                    out_shape=_sds((N_CHIPS, D_MODEL, ns_up)),
                    out_spec=pl.BlockSpec((None, D_MODEL, ns_up), lambda k, t: (k, 0, 0)),
                    mod=(shift2, scale2), name="dw_up")

    dgl, doat, dobt, dwba_p, dwbb_p = _merge_bwd(dy, oat, obt, pa, pb, proj, wba_p, wbb_p, w_out, tm=tm2)
    dwba, dwbb = _unpad_heads_w(dwba_p), _unpad_heads_w(dwbb_p)
    rowspec = pl.BlockSpec((tl, D_MODEL), lambda t: (t, 0))
    dw_out = _mm_tn(merged, dy, a_spec=rowspec, b_spec=rowspec, grid=(n_s,), out_shape=_sds((D_MODEL, D_MODEL)),
                    out_spec=_full((D_MODEL, D_MODEL)), name="dw_out")

    dqat, dkat, dvat, dkact, dvact, dsk = _attn_win_bwd(qat, doat, oat, lse_a, ka, ka.T, va, kac, kac.T, vac, sink, tq=tm)
    dka, dva, dkac, dvac = dkat.T, dvat.T, dkact.T, dvact.T
    dqbt, dkbt, dvbt, dkbct, dvbct = _attn_glob_bwd(qbt, dobt, obt, lse_b, mrun_b, pbt, kb.T, vb, kbc.T, vbc, tq=tm, tk=tm)
    dkb, dvb, dkbc, dvbc = dkbt.T, dvbt.T, dkbct.T, dvbct.T
    dsink = jnp.sum(dsk[:, :, 0, 0], axis=1)

    dproj, dqg, dkg = _prep_bwd(dqat, dka, dva, dqbt, dkb, dvb, proj, cos, sin, qg, kg, dgl, tm=tm, name="prep_bwd")
    grad_x, dscale1, dshift1 = _mm_nt4_mod_bwd(dproj, win4, dxz1, x, scale1, tm=tl, name="in_proj_bwd")
    ns_in = win4.shape[-1]
    win_spec = dict(b_spec=pl.BlockSpec((None, None, ns_in), lambda k, t: (0, 0, k)),
                    out_shape=_sds((N_CHIPS, D_MODEL, ns_in)),
                    out_spec=pl.BlockSpec((None, D_MODEL, ns_in), lambda k, t: (k, 0, 0)),
                    colsum_spec=pl.BlockSpec((8, ns_in), lambda k, t: (0, k)), colsum_shape=_sds((8, IN_COLS)))
    win_spec["b_spec"] = pl.BlockSpec((tx, ns_in), lambda k, t: (t, k))
    dw_in4, db_in = _mm_tn(x, dproj, a_spec=pl.BlockSpec((tx, D_MODEL), lambda k, t: (t, 0)), grid=(N_CHIPS, s // tx),
                           mod=(shift1, scale1), name="dw_in", **win_spec)

    zq = jnp.zeros((N_HEADS, LANES, nc), F32)
    dproj_c, _, dkg_c = _prep_bwd(zq, dkac, dvac, zq, dkbc, dvbc, proj_c, cos_c, sin_c, qg, kg,
                                  jnp.zeros((nc, IN_COLS - OFF_GA), BF16), tm=nc, name="prep_bwd_ctx")
    _, dscale_c, dshift_c = _mm_nt4_mod_bwd(dproj_c, win4, jnp.zeros((nc, D_MODEL), F32), ctx, scale_c, tm=nc,
                                            name="in_proj_bwd_ctx")
    win_spec["b_spec"] = pl.BlockSpec((nc, ns_in), lambda k, t: (t, k))
    dw_in4, db_in_c = _mm_tn(ctx, dproj_c, a_spec=pl.BlockSpec((nc, D_MODEL), lambda k, t: (t, 0)), grid=(N_CHIPS, 1),
                             mod=(shift_c, scale_c), init=dw_in4, name="dw_in_ctx", **win_spec)

    dmod = jnp.concatenate([dshift1, dscale1, dgate1, dshift2, dscale2, dgate2], axis=1)
    dmodc = jnp.concatenate([dshift_c, dscale_c], axis=1)
    dmodc_pad = jnp.concatenate([dmodc, jnp.zeros((1, 4 * D_MODEL), F32)], axis=1)
    dmodc8 = _first_row(dmodc_pad).astype(BF16)
    z8 = jnp.zeros((8, D_MODEL), F32)
    dsilu_c, _, _ = _mm_nt4_mod_bwd(dmodc8, wmod4, z8, z8, zvec, tm=8, name="c_ctx_bwd")
    sg = _sigmoid(c_ctx)
    dc_ctx = dsilu_c[0:1] * sg * (1.0 + c_ctx * (1.0 - sg))

    dqn = jnp.sum(dqg.reshape(N_HEADS, HEAD_DIM), axis=0, keepdims=True)
    dkn = jnp.sum((dkg + dkg_c).reshape(N_KV, HEAD_DIM), axis=0, keepdims=True)
    grads = dict(
        w_in4=dw_in4, b_in=db_in[0:1] + db_in_c[0:1], sink=dsink, qn=dqn, kn=dkn, wba=dwba, wbb=dwbb, w_out=dw_out,
        ln1_g=dln1_g, ln1_b=dln1_b, w_up4=dw_up4, conv_w=dcw2.transpose(1, 0, 2).reshape(3, 2 * D_FF),
        conv_b=dcb2.reshape(1, 2 * D_FF), w_down=dw_down, ln2_g=dln2_g, ln2_b=dln2_b,
        c_ctx=dc_ctx, dmod=dmod, dmodc=dmodc)
    return loss, grad_x, grads


ANY = pl.BlockSpec(memory_space=pl.ANY)


def _mesh_pos():
    return lax.axis_index("x"), lax.axis_index("y"), lax.axis_index("c")


def _other_chips(x, y):
    return [(1 - x, y), (x, 1 - y), (1 - x, 1 - y)]


def _remote(src, dst, send, recv, dev):
    return pltpu.make_async_remote_copy(src_ref=src, dst_ref=dst, send_sem=send, recv_sem=recv, device_id=dev,
                                        device_id_type=MESH)


def _set_block(stack, block, k):
    return lax.dynamic_update_slice(stack, block[None], (k,) + (0,) * block.ndim)


def _gather_shards(arrs, small):
    na = len(arrs)
    halves = [a.shape[0] // 2 for a in arrs]

    def body(*refs):
        ins, small_ref = refs[:na], refs[na]
        outs, small_out = refs[na + 1:2 * na + 1], refs[2 * na + 1]
        send, recv = refs[2 * na + 2:]
        x, y, c = _mesh_pos()
        me = 2 * x + y
        chips = _other_chips(x, y)

        def half(a, cc):
            return pl.ds(cc * halves[a], halves[a])

        sends = []
        for j, chip in enumerate(chips):
            for a in range(na):
                sends.append(_remote(ins[a].at[half(a, c)], outs[a].at[me, half(a, c)], send.at[a, j], recv.at[a, j],
                                     (*chip, c)))
            sends.append(_remote(small_ref, small_out.at[me], send.at[na, j], recv.at[na, j], (*chip, c)))
        for cp in sends:
            cp.start()
        for j, chip in enumerate(chips):
            kj = 2 * chip[0] + chip[1]
            for a in range(na):
                landed = outs[a].at[kj, half(a, c)]
                _remote(landed, landed, send.at[a, j], recv.at[a, j], (*chip, c)).wait_recv()
                fwd = _remote(landed, landed, send.at[a, 3 + j], recv.at[a, 3 + j], (x, y, 1 - c))
                fwd.start()
                sends.append(fwd)
            _remote(small_ref, small_out.at[kj], send.at[na, j], recv.at[na, j], (*chip, c)).wait_recv()
        for j, chip in enumerate(chips):
            kj = 2 * chip[0] + chip[1]
            for a in range(na):
                other = outs[a].at[kj, half(a, 1 - c)]
                _remote(other, other, send.at[a, 3 + j], recv.at[a, 3 + j], (x, y, 1 - c)).wait_recv()
        for cp in sends:
            cp.wait_send()

    out_shape = [_sds((N_CHIPS,) + a.shape, a.dtype) for a in arrs] + [_sds((N_CHIPS,) + small.shape, small.dtype)]
    got = pl.pallas_call(
        body, name="gather_shards", in_specs=[ANY] * (na + 1), out_specs=[ANY] * (na + 1), out_shape=out_shape,
        scratch_shapes=[pltpu.SemaphoreType.DMA((na + 1, 6)), pltpu.SemaphoreType.DMA((na + 1, 6))],
    )(*arrs, small)
    xp, yp, _ = _mesh_pos()
    return [_set_block(g, a, 2 * xp + yp) for g, a in zip(got, list(arrs) + [small])]


def _allgather_rows(v):
    r, n = v.shape

    def body(v_ref, out_ref, send, recv, loc):
        x, y, c = _mesh_pos()
        me, sibling = (x, y, c), (x, y, 1 - c)
        chips = _other_chips(x, y)

        def rows(px, py, pc):
            return out_ref.at[4 * px + 2 * py + pc]

        def copy(k, block, to, src=None):
            return _remote(rows(*block) if src is None else src, rows(*block), send.at[k], recv.at[k], to)

        mine = pltpu.make_async_copy(v_ref, rows(*me), loc)
        mine.start()
        first = [copy(0, me, sibling, src=v_ref)] + [copy(1 + j, me, (*chip, c), src=v_ref) for j, chip in enumerate(chips)]
        for cp in first:
            cp.start()
        passed = [copy(4 + j, (*chip, c), sibling) for j, chip in enumerate(chips)]
        for j, chip in enumerate(chips):
            copy(1 + j, (*chip, c), me).wait_recv()
            passed[j].start()
        copy(0, sibling, me).wait_recv()
        for j, chip in enumerate(chips):
            copy(4 + j, (*chip, 1 - c), me).wait_recv()
        for cp in first + passed:
            cp.wait_send()
        mine.wait()

    return pl.pallas_call(
        body, name="allgather_rows", in_specs=[pl.BlockSpec(memory_space=pltpu.VMEM)],
        out_specs=pl.BlockSpec(memory_space=pltpu.VMEM), out_shape=_sds((N_DEV, r, n), v.dtype),
        scratch_shapes=[pltpu.SemaphoreType.DMA((7,)), pltpu.SemaphoreType.DMA((7,)), pltpu.SemaphoreType.DMA],
    )(v)


def _swap_other_half(g):
    nb, r, n = g.shape
    rh = r // 2

    def body(g_ref, out_ref, send, recv):
        x, y, c = _mesh_pos()
        cp = _remote(g_ref.at[:, pl.ds((1 - c) * rh, rh), :], out_ref, send, recv, (x, y, 1 - c))
        cp.start()
        cp.wait()

    return pl.pallas_call(
        body, name="swap_other_half", in_specs=[ANY], out_specs=ANY, out_shape=_sds((nb, rh, n), g.dtype),
        scratch_shapes=[pltpu.SemaphoreType.DMA, pltpu.SemaphoreType.DMA],
    )(g)


def _scatter_to_chips(p):
    def body(p_ref, out_ref, send, recv):
        x, y, c = _mesh_pos()
        me = 2 * x + y
        chips = _other_chips(x, y)
        sends = [_remote(p_ref.at[2 * chip[0] + chip[1]], out_ref.at[me], send.at[j], recv.at[j], (*chip, c))
                 for j, chip in enumerate(chips)]
        for cp in sends:
            cp.start()
        for j, chip in enumerate(chips):
            kj = 2 * chip[0] + chip[1]
            _remote(p_ref.at[kj], out_ref.at[kj], send.at[j], recv.at[j], (*chip, c)).wait_recv()
        for cp in sends:
            cp.wait_send()

    got = pl.pallas_call(
        body, name="scatter_to_chips", in_specs=[ANY], out_specs=ANY, out_shape=_sds(p.shape, p.dtype),
        scratch_shapes=[pltpu.SemaphoreType.DMA((3,)), pltpu.SemaphoreType.DMA((3,))],
    )(p)
    xp, yp, _ = _mesh_pos()
    me = 2 * xp + yp
    return _set_block(got, lax.dynamic_index_in_dim(p, me, axis=0, keepdims=False), me)


def _join_halves(f):
    def body(f_ref, out_ref, send, recv):
        x, y, c = _mesh_pos()
        cp = _remote(f_ref, out_ref, send, recv, (x, y, 1 - c))
        cp.start()
        cp.wait()

    other = pl.pallas_call(
        body, name="join_halves", in_specs=[ANY], out_specs=ANY, out_shape=_sds(f.shape, f.dtype),
        scratch_shapes=[pltpu.SemaphoreType.DMA, pltpu.SemaphoreType.DMA],
    )(f)
    first = lax.axis_index("c") == 0
    return jnp.concatenate([jnp.where(first, f, other), jnp.where(first, other, f)], axis=0)


def _row_tile(rows, cap=512):
    t = cap - cap % 8
    while rows % t:
        t -= 8
    return t


def _add_blocks(a, b, out_dtype):
    nb, r, n = a.shape
    tr = _row_tile(r)

    def body(a_ref, b_ref, o_ref):
        o_ref[...] = (a_ref[...] + b_ref[...]).astype(out_dtype)

    spec = pl.BlockSpec((None, tr, n), lambda k, i: (k, i, 0))
    return _call(body, name="add_blocks", grid=(nb, r // tr), in_specs=[spec, spec], out_specs=spec,
                 out_shape=_sds(a.shape, out_dtype), sem=("parallel", "parallel"))(a, b)


def _sum_leading(a, *, name):
    nk, r, n = a.shape
    tr = _row_tile(r)

    def body(a_ref, o_ref):
        acc = a_ref[0].astype(F32)
        for k in range(1, nk):
            acc = acc + a_ref[k].astype(F32)
        o_ref[...] = acc

    return _call(body, name=name, grid=(r // tr,), in_specs=[pl.BlockSpec((nk, tr, n), lambda i: (0, i, 0))],
                 out_specs=pl.BlockSpec((tr, n), lambda i: (i, 0)), out_shape=_sds((r, n)), sem=("parallel",))(a)


def _silu_outer(a, b):
    kdim, n = a.shape[1], b.shape[1]

    def body(a_ref, b_ref, o_ref):
        av = a_ref[...]
        av = av * _sigmoid(av)
        bv = b_ref[...]
        ah, bh = av.astype(BF16), bv.astype(BF16)
        al, bl = (av - ah.astype(F32)).astype(BF16), (bv - bh.astype(F32)).astype(BF16)
        o_ref[...] = _dot_tn(ah, bh) + (_dot_tn(ah, bl) + _dot_tn(al, bh))

    return _call(body, name="dw_mod", grid=(1,), in_specs=[_full(a.shape), _full(b.shape)], out_specs=_full((kdim, n)),
                 out_shape=_sds((kdim, n)))(a, b)


def _adamw(w, g, m, v):
    r, n = w.shape
    tr = _row_tile(r)

    def body(w_ref, g_ref, m_ref, v_ref, d_ref, nm_ref, nv_ref):
        gv = g_ref[...]
        nm = ADAM_B1 * m_ref[...] + (1.0 - ADAM_B1) * gv
        nv = ADAM_B2 * v_ref[...] + (1.0 - ADAM_B2) * (gv * gv)
        m_hat = nm / (1.0 - ADAM_B1 ** ADAM_STEP)
        v_hat = nv / (1.0 - ADAM_B2 ** ADAM_STEP)
        d_ref[...] = -ADAM_LR * (m_hat / (jnp.sqrt(v_hat) + ADAM_EPS) + ADAM_WD * w_ref[...])
        nm_ref[...] = nm
        nv_ref[...] = nv

    spec = pl.BlockSpec((tr, n), lambda i: (i, 0))
    return _call(body, name="adamw", grid=(r // tr,), in_specs=[spec] * 4, out_specs=[spec] * 3,
                 out_shape=[_sds((r, n))] * 3, sem=("parallel",))(w, g, m, v)


BIG = ("w_in", "w_branch_a", "w_branch_b", "w_out", "w_up", "w_down", "conv_w")
BIG_ROWS = 3584
MATRICES = ("w_mod", "w_in", "w_branch_a", "w_branch_b", "w_out", "w_up", "w_down")
SMALL = ("b_mod", "b_in", "conv_b", "ln1_g", "ln1_b", "ln2_g", "ln2_b", "c_ctx", "attn_sink", "q_norm_g", "k_norm_g", "conv_w")
SMALL_ROWS = 8 * len(SMALL)


def _rows(a, n_rows):
    flat = a.reshape(-1)
    return jnp.pad(flat, (0, n_rows * D_MODEL - flat.shape[0])).reshape(n_rows, D_MODEL)


def _group8(a):
    return _rep8(_rows(a, 1)) if a.size <= D_MODEL else _rows(a, 8)


def _ungroup8(p, shape):
    size = math.prod(shape)
    return (p[0, :size] if size <= D_MODEL else p.reshape(-1)[:size]).reshape(shape)


def _unpack_big(p, like):
    out, r = {}, 0
    for n in BIG:
        size = math.prod(like[n].shape)
        nr = size // D_MODEL if n != "conv_w" else 8
        out[n] = p[r:r + nr].reshape(-1)[:size].reshape(like[n].shape)
        r += nr
    return out


def _pack_small(t):
    return jnp.concatenate([_group8(t[n]) for n in SMALL], axis=0)


def _unpack_small(p, like):
    return {n: _ungroup8(p[8 * i:8 * i + 8], like[n].shape) for i, n in enumerate(SMALL)}


WEIGHTS = ("c_ctx", "w_mod", "b_mod", "w_in", "b_in", "attn_sink", "q_norm_g", "k_norm_g", "w_branch_a", "w_branch_b",
           "w_out", "ln1_g", "ln1_b", "w_up", "conv_w", "conv_b", "w_down", "ln2_g", "ln2_b")


def kernel(x, c, ctx, c_ctx, w_mod, b_mod, w_in, b_in, attn_sink, q_norm_g, k_norm_g, w_branch_a, w_branch_b, w_out, ln1_g, ln1_b, w_up, conv_w, conv_b, w_down, ln2_g, ln2_b, loss_target, m_c_ctx, m_w_mod, m_b_mod, m_w_in, m_b_in, m_attn_sink, m_q_norm_g, m_k_norm_g, m_w_branch_a, m_w_branch_b, m_w_out, m_ln1_g, m_ln1_b, m_w_up, m_conv_w, m_conv_b, m_w_down, m_ln2_g, m_ln2_b, v_c_ctx, v_w_mod, v_b_mod, v_w_in, v_b_in, v_attn_sink, v_q_norm_g, v_k_norm_g, v_w_branch_a, v_w_branch_b, v_w_out, v_ln1_g, v_ln1_b, v_w_up, v_conv_w, v_conv_b, v_w_down, v_ln2_g, v_ln2_b):
    w = dict(c_ctx=c_ctx, w_mod=w_mod, b_mod=b_mod, w_in=w_in, b_in=b_in, attn_sink=attn_sink, q_norm_g=q_norm_g,
             k_norm_g=k_norm_g, w_branch_a=w_branch_a, w_branch_b=w_branch_b, w_out=w_out, ln1_g=ln1_g, ln1_b=ln1_b,
             w_up=w_up, conv_w=conv_w, conv_b=conv_b, w_down=w_down, ln2_g=ln2_g, ln2_b=ln2_b)
    m = dict(c_ctx=m_c_ctx, w_mod=m_w_mod, b_mod=m_b_mod, w_in=m_w_in, b_in=m_b_in, attn_sink=m_attn_sink,
             q_norm_g=m_q_norm_g, k_norm_g=m_k_norm_g, w_branch_a=m_w_branch_a, w_branch_b=m_w_branch_b, w_out=m_w_out,
             ln1_g=m_ln1_g, ln1_b=m_ln1_b, w_up=m_w_up, conv_w=m_conv_w, conv_b=m_conv_b, w_down=m_w_down,
             ln2_g=m_ln2_g, ln2_b=m_ln2_b)
    v = dict(c_ctx=v_c_ctx, w_mod=v_w_mod, b_mod=v_b_mod, w_in=v_w_in, b_in=v_b_in, attn_sink=v_attn_sink,
             q_norm_g=v_q_norm_g, k_norm_g=v_k_norm_g, w_branch_a=v_w_branch_a, w_branch_b=v_w_branch_b, w_out=v_w_out,
             ln1_g=v_ln1_g, ln1_b=v_ln1_b, w_up=v_w_up, conv_w=v_conv_w, conv_b=v_conv_b, w_down=v_w_down,
             ln2_g=v_ln2_g, ln2_b=v_ln2_b)
    xp, yp, _ = _mesh_pos()
    me = 2 * xp + yp

    branches = jnp.concatenate([w_branch_a[0], w_branch_b[0]], axis=0)
    wide = jnp.concatenate([w_mod[0], w_in[0], w_up[0], branches], axis=1).astype(BF16)
    tall = jnp.concatenate([w_out[0], w_down[0]], axis=0).astype(BF16)
    wide4, tall4, cw4 = _gather_shards([wide, tall], conv_w[0])
    n_mod, n_in, n_up = w_mod.shape[-1], w_in.shape[-1], w_up.shape[-1]
    wmod4 = wide4[:, :, :n_mod]
    win4 = wide4[:, :, n_mod:n_mod + n_in]
    wup4 = wide4[:, :, n_mod + n_in:n_mod + n_in + n_up]
    br4 = wide4[:, :, n_mod + n_in + n_up:]
    n_br = w_branch_a.shape[1]
    wba = br4[:, :n_br].transpose(1, 0, 2).reshape(n_br, D_MODEL)
    wbb = br4[:, n_br:].transpose(1, 0, 2).reshape(n_br, D_MODEL)
    n_out = w_out.shape[1]
    w_out_full = tall4[:, :n_out].reshape(D_MODEL, D_MODEL)
    w_down_full = tall4[:, n_out:].reshape(D_FF, D_MODEL)
    cw_full = cw4.transpose(1, 0, 2).reshape(3, 2 * D_FF)

    loss, grad_x, g = _local_step(
        x[0], c, ctx[0], c_ctx[None], wmod4, b_mod, win4, b_in, attn_sink[0], q_norm_g, k_norm_g, wba, wbb, w_out_full,
        ln1_g, ln1_b, wup4, cw_full, conv_b, w_down_full, ln2_g, ln2_b, loss_target[0])
    loss = lax.psum(loss, ("x", "y", "c"))

    sent = dict(c=c, dmod=g["dmod"], dmodc=g["dmodc"], b_in=g["b_in"], conv_b=g["conv_b"], ln1_g=g["ln1_g"],
                ln1_b=g["ln1_b"], ln2_g=g["ln2_g"], ln2_b=g["ln2_b"], c_ctx=g["c_ctx"], attn_sink=g["sink"],
                q_norm_g=g["qn"], k_norm_g=g["kn"])
    every = _allgather_rows(jnp.concatenate([_group8(a) for a in sent.values()], axis=0))
    total = _sum_leading(every, name="sum_devices")
    slot = {n: slice(8 * i, 8 * i + 8) for i, n in enumerate(sent)}
    gs = {n: _ungroup8(total[slot[n]], sent[n].shape) for n in SMALL if n in sent}
    dmodc_sum = jnp.concatenate([_ungroup8(total[slot["dmodc"]], (1, 2 * D_MODEL)), jnp.zeros((1, 4 * D_MODEL), F32)],
                                axis=1)
    gs["b_mod"] = _ungroup8(total[slot["dmod"]], b_mod.shape) + dmodc_sum
    acts = jnp.concatenate([every[:, slot["c"].start], _rep8(c_ctx)], axis=0)
    dmods = jnp.concatenate([every[:, slot["dmod"]].reshape(N_DEV, -1)[:, :6 * D_MODEL], _first_row(dmodc_sum)], axis=0)
    g_w_mod = _silu_outer(acts, lax.dynamic_slice_in_dim(dmods, me * n_mod, n_mod, axis=1))

    cw_g4 = _to_blocks4(g["conv_w"])
    parts = [
        g["w_in4"].reshape(N_CHIPS, -1, D_MODEL), _to_blocks4(g["wba"]).reshape(N_CHIPS, -1, D_MODEL),
        _to_blocks4(g["wbb"]).reshape(N_CHIPS, -1, D_MODEL), g["w_out"].reshape(N_CHIPS, -1, D_MODEL),
        g["w_up4"].reshape(N_CHIPS, -1, D_MODEL), g["w_down"].reshape(N_CHIPS, -1, D_MODEL),
        jnp.pad(cw_g4.reshape(N_CHIPS, -1), ((0, 0), (0, 8 * D_MODEL - cw_g4.shape[1] * cw_g4.shape[2]))).reshape(
            N_CHIPS, 8, D_MODEL)]
    used = sum(p.shape[1] for p in parts)
    packed = jnp.concatenate(parts + [jnp.zeros((N_CHIPS, BIG_ROWS - used, D_MODEL), F32)], axis=1)
    rh = BIG_ROWS // 2
    cpos = lax.axis_index("c")
    my_half = lax.dynamic_slice_in_dim(packed, cpos * rh, rh, axis=1)
    chip_sum = _add_blocks(my_half, _swap_other_half(packed), BF16)
    half_sum = _sum_leading(_scatter_to_chips(chip_sum), name="sum_chips")
    g_big = _unpack_big(_join_halves(half_sum), w)

    grads = dict(gs, w_mod=g_w_mod, **g_big)
    grads = {n: grads[n].reshape(w[n].shape) for n in WEIGHTS}
    delta, new_m, new_v = {}, {}, {}
    for n in MATRICES:
        outs = _adamw(*[t[n][0] for t in (w, grads, m, v)])
        delta[n], new_m[n], new_v[n] = [o[None] for o in outs]
    outs = _adamw(*[_pack_small(t) for t in (w, grads, m, v)])
    for res, o in zip((delta, new_m, new_v), outs):
        res.update(_unpack_small(o, w))
    return (loss, grad_x[None], *[grads[n] for n in WEIGHTS], *[delta[n] for n in WEIGHTS],
            *[new_m[n] for n in WEIGHTS], *[new_v[n] for n in WEIGHTS])
```

```python
import functools
import math

import jax
import jax.numpy as jnp
from jax import lax
from jax.experimental import pallas as pl
from jax.experimental.pallas import tpu as pltpu

F32 = jnp.float32
BF16 = jnp.bfloat16

D_MODEL = 1024
HEAD_DIM = 64
N_HEADS = 8
N_KV = 2
WINDOW = 128
GRID_W = 64
ROPE_THETA = 10000.0
D_FF = 2816
LN_EPS = 1e-5
QK_EPS = 1e-6
ALPHA = 2.0 ** 0.25
Q_SCALE = HEAD_DIM ** -0.5
OFF_GA = 1536
IN_COLS = 3584
ADAM_LR, ADAM_B1, ADAM_B2, ADAM_EPS, ADAM_WD, ADAM_STEP = 0.001, 0.9, 0.999, 1e-8, 0.01, 10

LANES = 128
VMEM_BUDGET = 52 * 1024 * 1024
N_CHIPS = 4
N_DEV = 8
NEG = -1e30
MESH = pl.DeviceIdType.MESH


def _sigmoid(x):
    return 1.0 / (1.0 + jnp.exp(-x))


def _dot(a, b):
    return jnp.dot(a, b, preferred_element_type=F32)


def _dot_nt(a, b):
    return lax.dot_general(a, b, (((1,), (1,)), ((), ())), preferred_element_type=F32)


def _dot_tn(a, b):
    return lax.dot_general(a, b, (((0,), (0,)), ((), ())), preferred_element_type=F32)


def _call(body, *, name, grid, in_specs, out_specs, out_shape, scratch=(), sem=None, **kw):
    params = dict(vmem_limit_bytes=VMEM_BUDGET)
    if sem is not None:
        params["dimension_semantics"] = sem
    return pl.pallas_call(body, name=name, grid=grid, in_specs=in_specs, out_specs=out_specs,
                          out_shape=out_shape, scratch_shapes=list(scratch),
                          compiler_params=pltpu.CompilerParams(**params), **kw)


def _full(shape):
    n = len(shape)
    return pl.BlockSpec(shape, lambda *_: (0,) * n)


def _sds(shape, dtype=F32):
    return jax.ShapeDtypeStruct(shape, dtype)


def _mm_nn4(a, shift, scale, w4, bias, *, mode, split_out, out_dtype, tm, name):
    m, kdim = a.shape
    nb, _, ns = w4.shape

    def body(a_ref, sh_ref, sc_ref, w_ref, b_ref, o_ref):
        av = a_ref[...]
        if mode == "modulate":
            av = av * (1.0 + sc_ref[...]) + sh_ref[...]
        else:
            av = av * _sigmoid(av)
        o_ref[...] = (_dot(av.astype(BF16), w_ref[...]) + b_ref[...]).astype(out_dtype)

    if split_out:
        out_shape = _sds((2, m, 2 * ns), out_dtype)
        out_spec = pl.BlockSpec((None, tm, ns), lambda i, k: (k // 2, i, k % 2))
    else:
        out_shape = _sds((m, nb * ns), out_dtype)
        out_spec = pl.BlockSpec((tm, ns), lambda i, k: (i, k))
    return _call(
        body, name=name, grid=(m // tm, nb),
        in_specs=[pl.BlockSpec((tm, kdim), lambda i, k: (i, 0)),
                  pl.BlockSpec((1, kdim), lambda i, k: (0, 0)),
                  pl.BlockSpec((1, kdim), lambda i, k: (0, 0)),
                  pl.BlockSpec((None, kdim, ns), lambda i, k: (k, 0, 0)),
                  pl.BlockSpec((1, ns), lambda i, k: (0, k))],
        out_specs=out_spec, out_shape=out_shape, sem=("parallel", "arbitrary"),
    )(a, shift, scale, w4, bias)


def _mm_tn(a, b, *, a_spec, b_spec, grid, out_shape, out_spec, name, mod=None, init=None, colsum_spec=None,
           colsum_shape=None):
    red = len(grid) - 1
    has_mod, has_init, has_cs = mod is not None, init is not None, colsum_spec is not None

    def body(*refs):
        refs = list(refs)
        a_ref, b_ref = refs[0], refs[1]
        pos = 2
        if has_mod:
            sh_ref, sc_ref = refs[2], refs[3]
            pos = 4
        if has_init:
            init_ref = refs[pos]
            pos += 1
        o_ref = refs[pos]
        cs_ref = refs[pos + 1] if has_cs else None
        s = pl.program_id(red)

        @pl.when(s == 0)
        def _():
            o_ref[...] = init_ref[...] if has_init else jnp.zeros(o_ref.shape, F32)
            if has_cs:
                cs_ref[...] = jnp.zeros(cs_ref.shape, F32)

        av = a_ref[...]
        if has_mod:
            av = av * (1.0 + sc_ref[...]) + sh_ref[...]
        bv = b_ref[...]
        o_ref[...] += _dot_tn(av.astype(BF16), bv)
        if has_cs:
            cs_ref[...] += jnp.broadcast_to(jnp.sum(bv.astype(F32), axis=0, keepdims=True), cs_ref.shape)

    ins, in_specs = [a, b], [a_spec, b_spec]
    if has_mod:
        kdim = mod[0].shape[-1]
        ins += list(mod)
        in_specs += [_full((1, kdim)), _full((1, kdim))]
    if has_init:
        ins.append(init)
        in_specs.append(out_spec)
    out_specs, out_shapes = out_spec, out_shape
    if has_cs:
        out_specs, out_shapes = [out_spec, colsum_spec], [out_shape, colsum_shape]
    sem = ("parallel",) * red + ("arbitrary",)
    return _call(body, name=name, grid=grid, in_specs=in_specs, out_specs=out_specs, out_shape=out_shapes,
                 sem=sem)(*ins)


def _rope_tables(n_tok):
    pos = jnp.arange(n_tok, dtype=jnp.int32)
    rows = (pos // GRID_W).astype(F32)
    cols = (pos % GRID_W).astype(F32)
    n_freq = HEAD_DIM // 4
    inv_freq = ROPE_THETA ** (-jnp.arange(n_freq, dtype=F32) / n_freq)
    ang_r = rows[:, None] * inv_freq
    ang_c = cols[:, None] * inv_freq
    cos = jnp.concatenate([jnp.cos(ang_r)] * 2 + [jnp.cos(ang_c)] * 2, axis=-1)
    sin = jnp.concatenate([-jnp.sin(ang_r), jnp.sin(ang_r), -jnp.sin(ang_c), jnp.sin(ang_c)], axis=-1)
    return jnp.tile(cos, (1, 2)), jnp.tile(sin, (1, 2))


def _lane(shape):
    return lax.broadcasted_iota(jnp.int32, shape, 1)


def _rope_partner(t, lane):
    return jnp.where((lane % 32) < 16, pltpu.roll(t, LANES - 16, 1), pltpu.roll(t, 16, 1))


def _half_mean(s, lane):
    lo = jnp.sum(jnp.where(lane < HEAD_DIM, s, 0.0), axis=-1, keepdims=True)
    hi = jnp.sum(jnp.where(lane < HEAD_DIM, 0.0, s), axis=-1, keepdims=True)
    return jnp.where(lane < HEAD_DIM, lo, hi) * (1.0 / HEAD_DIM)


def _prep(proj, cos, sin, qg, kg, *, tm, name):
    m = proj.shape[0]

    def body(p_ref, cos_ref, sin_ref, qg_ref, kg_ref, qa_ref, ka_ref, va_ref, qb_ref, kb_ref, vb_ref):
        lane = _lane((tm, LANES))
        cosv, sinv = cos_ref[...], sin_ref[...]
        low = lane < HEAD_DIM

        def rope(t):
            return t * cosv + _rope_partner(t, lane) * sinv

        def rms(t, g):
            return t * lax.rsqrt(_half_mean(t * t, lane) + QK_EPS) * g

        def place(q_ref, j, chunk):
            sw = pltpu.roll(chunk, HEAD_DIM, 1)
            if j < 2:
                h0, h1 = jnp.where(low, chunk, 0.0), jnp.where(low, sw, 0.0)
            else:
                h0, h1 = jnp.where(low, 0.0, sw), jnp.where(low, 0.0, chunk)
            q_ref[2 * j] = h0.T.astype(BF16)
            q_ref[2 * j + 1] = h1.T.astype(BF16)

        for j in range(4):
            place(qa_ref, j, rope(p_ref[:, j * LANES:(j + 1) * LANES]) * Q_SCALE)
            place(qb_ref, j, rope(rms(p_ref[:, 768 + j * LANES:768 + (j + 1) * LANES], qg_ref[...])) * Q_SCALE)
        ka_ref[...] = rope(p_ref[:, 512:640]).astype(BF16)
        va_ref[...] = p_ref[:, 640:768].astype(BF16)
        kb_ref[...] = rope(rms(p_ref[:, 1280:1408], kg_ref[...])).astype(BF16)
        vb_ref[...] = p_ref[:, 1408:1536].astype(BF16)

    row = pl.BlockSpec((tm, LANES), lambda i: (i, 0))
    qspec = pl.BlockSpec((N_HEADS, LANES, tm), lambda i: (0, 0, i))
    return _call(
        body, name=name, grid=(m // tm,),
        in_specs=[pl.BlockSpec((tm, OFF_GA), lambda i: (i, 0)), row, row, _full((1, LANES)), _full((1, LANES))],
        out_specs=[qspec, row, row, qspec, row, row],
        out_shape=[_sds((N_HEADS, LANES, m), BF16), _sds((m, LANES), BF16), _sds((m, LANES), BF16),
                   _sds((N_HEADS, LANES, m), BF16), _sds((m, LANES), BF16), _sds((m, LANES), BF16)],
        sem=("parallel",),
    )(proj, cos, sin, qg, kg)


def _prep_bwd(dqa, dka, dva, dqb, dkb, dvb, proj, cos, sin, qg, kg, dgl, *, tm, name):
    m = proj.shape[0]

    def body(dqa_ref, dka_ref, dva_ref, dqb_ref, dkb_ref, dvb_ref, p_ref, cos_ref, sin_ref, qg_ref, kg_ref,
             dgl_ref, dp_ref, dqg_ref, dkg_ref):
        i = pl.program_id(0)
        lane = _lane((tm, LANES))
        cosv, sinv = cos_ref[...], sin_ref[...]
        low = lane < HEAD_DIM

        @pl.when(i == 0)
        def _():
            dqg_ref[...] = jnp.zeros(dqg_ref.shape, F32)
            dkg_ref[...] = jnp.zeros(dkg_ref.shape, F32)

        def unrope(d):
            return d * cosv - _rope_partner(d, lane) * sinv

        def unplace(dq_ref, j):
            d0, d1 = dq_ref[2 * j].T, dq_ref[2 * j + 1].T
            if j < 2:
                return jnp.where(low, d0, pltpu.roll(d1, HEAD_DIM, 1))
            return jnp.where(low, pltpu.roll(d0, HEAD_DIM, 1), d1)

        def unrms(dtn, t, g):
            r = lax.rsqrt(_half_mean(t * t, lane) + QK_EPS)
            u = dtn * g
            dt = r * u - t * (r * r * r) * _half_mean(u * t, lane)
            return dt, jnp.sum(dtn * t * r, axis=0, keepdims=True)

        for j in range(4):
            dp_ref[:, j * LANES:(j + 1) * LANES] = (unrope(unplace(dqa_ref, j)) * Q_SCALE).astype(BF16)
            c0 = 768 + j * LANES
            dt, dg = unrms(unrope(unplace(dqb_ref, j)) * Q_SCALE, p_ref[:, c0:c0 + LANES], qg_ref[...])
            dp_ref[:, c0:c0 + LANES] = dt.astype(BF16)
            dqg_ref[:, j * LANES:(j + 1) * LANES] += dg
        dp_ref[:, 512:640] = unrope(dka_ref[...]).astype(BF16)
        dp_ref[:, 640:768] = dva_ref[...].astype(BF16)
        dt, dg = unrms(unrope(dkb_ref[...]), p_ref[:, 1280:1408], kg_ref[...])
        dp_ref[:, 1280:1408] = dt.astype(BF16)
        dkg_ref[...] += dg
        dp_ref[:, 1408:1536] = dvb_ref[...].astype(BF16)
        dp_ref[:, OFF_GA:] = dgl_ref[...]

    row = pl.BlockSpec((tm, LANES), lambda i: (i, 0))
    qspec = pl.BlockSpec((N_HEADS, LANES, tm), lambda i: (0, 0, i))
    return _call(
        body, name=name, grid=(m // tm,),
        in_specs=[qspec, row, row, qspec, row, row, pl.BlockSpec((tm, OFF_GA), lambda i: (i, 0)), row, row,
                  _full((1, LANES)), _full((1, LANES)), pl.BlockSpec((tm, IN_COLS - OFF_GA), lambda i: (i, 0))],
        out_specs=[pl.BlockSpec((tm, IN_COLS), lambda i: (i, 0)), _full((1, 512)), _full((1, LANES))],
        out_shape=[_sds((m, IN_COLS), BF16), _sds((1, 512)), _sds((1, LANES))],
        sem=("arbitrary",),
    )(dqa, dka, dva, dqb, dkb, dvb, proj, cos, sin, qg, kg, dgl)


def _attn_glob_fwd(qt, k, vt, kc, vct, *, tq, tk):
    nh, _, s = qt.shape
    nc = kc.shape[0]
    n_chunks = s // tk
    assert n_chunks % 2 == 0, "the staging slots alternate: the last chunk must use slot 1"
    half = LANES // 2

    def body(qt_ref, qn_ref, k_ref, vt_ref, kc_ref, vct_ref, ot_ref, lse_ref, mrun_ref, p_hbm,
             acc_sc, st_sc, stage_sc, stagec_sc, sems, semc):
        h, i = pl.program_id(0), pl.program_id(1)
        qtv = qt_ref[...]
        acc_sc[...] = jnp.zeros(acc_sc.shape, F32)

        def p_out(slot, c):
            return pltpu.make_async_copy(stage_sc.at[slot], p_hbm.at[h, i, pl.ds(pl.multiple_of(c * tk, tk), tk), :],
                                         sems.at[slot])

        def update(st, vtv, m_old):
            m_new = jnp.maximum(m_old, jnp.max(st, axis=0, keepdims=True))
            pb = jnp.exp(st - m_new).astype(BF16)
            acc_sc[...] = acc_sc[...] * jnp.exp(m_old - m_new) + _dot(vtv, pb)
            return m_new, pb

        m, pbc = update(_dot(kc_ref[...], qtv), vct_ref[...], jnp.full((1, tq), NEG, F32))
        mrun_ref[pl.ds(n_chunks, 1), :] = m
        stagec_sc[...] = pbc
        ctx_out = pltpu.make_async_copy(stagec_sc, p_hbm.at[h, i, pl.ds(s, nc), :], semc)
        ctx_out.start()

        def step(c, m_old, k_next, q_next):
            slot = c % 2
            off = pl.multiple_of(c * tk, tk)
            st = st_sc[...]
            st_next = _dot(k_next, q_next)
            m_new, pb = update(st, vt_ref[:, pl.ds(off, tk)], m_old)
            mrun_ref[pl.ds(c, 1), :] = m_new
            stage_sc[slot] = pb
            p_out(slot, c).start()
            p_out(1 - slot, c - 1).wait()
            st_sc[...] = st_next
            return m_new

        def loop(c, m_old):
            return step(c, m_old, k_ref[pl.ds(pl.multiple_of((c + 1) * tk, tk), tk), :], qtv)

        @pl.when(jnp.logical_and(h == 0, i == 0))
        def _():
            stage_sc[1] = jnp.zeros((tk, tq), BF16)
            pltpu.make_async_copy(stage_sc.at[1], p_hbm.at[h, i, pl.ds(s + nc, tk), :], sems.at[1]).start()
            st_sc[...] = _dot(k_ref[pl.ds(0, tk), :], qtv)

        m = lax.fori_loop(0, n_chunks - 1, loop, m)
        m = step(n_chunks - 1, m, k_ref[pl.ds(0, tk), :], qn_ref[...])

        @pl.when(jnp.logical_and(h == nh - 1, i == n_q - 1))
        def _():
            p_out(1, n_chunks - 1).wait()
        ctx_out.wait()
        acc = acc_sc[...]
        l = jnp.where(h < nh // N_KV, acc[half:half + 1], acc[0:1])
        ot_ref[...] = (acc / l).astype(BF16)
        lse_ref[...] = m + jnp.log(l)

    grp = nh // N_KV
    n_q = s // tq
    return _call(
        body, name="attn_glob_fwd", grid=(nh, n_q),
        in_specs=[pl.BlockSpec((None, LANES, tq), lambda h, i: (h, 0, i)),
                  pl.BlockSpec((None, LANES, tq), lambda h, i: (jnp.minimum(h + (i + 1) // n_q, nh - 1), 0, (i + 1) % n_q)),
                  _full((s, LANES)),
                  pl.BlockSpec((None, LANES, s), lambda h, i: (h // grp, 0, 0)), _full((nc, LANES)),
                  pl.BlockSpec((None, LANES, nc), lambda h, i: (h // grp, 0, 0))],
        out_specs=[pl.BlockSpec((None, LANES, tq), lambda h, i: (h, 0, i)),
                   pl.BlockSpec((None, 1, tq), lambda h, i: (h, 0, i)),
                   pl.BlockSpec((None, n_chunks + 1, tq), lambda h, i: (h, 0, i)), ANY],
        out_shape=[_sds((nh, LANES, s), BF16), _sds((nh, 1, s)), _sds((nh, n_chunks + 1, s)),
                   _sds((nh, s // tq, s + nc + tk, tq), BF16)],
        scratch=[pltpu.VMEM((LANES, tq), F32), pltpu.VMEM((tk, tq), F32), pltpu.VMEM((2, tk, tq), BF16),
                 pltpu.VMEM((nc, tq), BF16), pltpu.SemaphoreType.DMA((2,)), pltpu.SemaphoreType.DMA],
        sem=("arbitrary", "arbitrary"),
    )(qt, qt, k, vt, kc, vct)


P_AHEAD = 3


def _attn_glob_bwd(qt, dot, ot, lse, mrun, p, kt, v, kct, vc, *, tq, tk):
    nh, _, s = qt.shape
    nc = vc.shape[0]
    n_q = s // tq
    n_chunks = s // tk
    n_run = mrun.shape[1] - 1
    per_run = n_chunks // n_run
    assert n_chunks % (P_AHEAD + 1) == 0, "the p^T slots must line up from one grid step to the next"

    def body(qt_ref, dot_ref, ot_ref, lse_ref, mrun_ref, p_hbm, kt_ref, v_ref, kct_ref, vc_ref,
             dqt_ref, dkt_ref, dvt_ref, dkct_ref, dvct_ref, acc_sc, dp_sc, dkt_sc, dvt_sc, p_sc, pc_sc, sems, semc):
        h, i = pl.program_id(0), pl.program_id(1)

        @pl.when(jnp.logical_and(h == 0, i == 0))
        def _():
            dkct_ref[...] = jnp.zeros(dkct_ref.shape, F32)
            dvct_ref[...] = jnp.zeros(dvct_ref.shape, F32)
            dkt_sc[...] = jnp.zeros(dkt_sc.shape, F32)
            dvt_sc[...] = jnp.zeros(dvt_sc.shape, F32)


        def p_in(slot, c, hh=h, ii=i):
            return pltpu.make_async_copy(p_hbm.at[hh, ii, pl.ds(pl.multiple_of(c * tk, tk), tk), :], p_sc.at[slot],
                                         sems.at[slot])

        h_next, i_next = jnp.minimum(h + (i + 1) // n_q, nh - 1), (i + 1) % n_q
        ctx_in = pltpu.make_async_copy(p_hbm.at[h, i, pl.ds(s, nc), :], pc_sc, semc)
        ctx_in.start()

        @pl.when(jnp.logical_and(h == 0, i == 0))
        def _():
            for c in range(P_AHEAD):
                p_in(c, c).start()
        qtv, dotv, lse = qt_ref[...], dot_ref[...], lse_ref[...]
        delta = jnp.sum(dotv.astype(F32) * ot_ref[...].astype(F32), axis=0, keepdims=True)

        def grads(pt_stored, m_row, dpt):
            pt = pt_stored.astype(F32) * jnp.exp(m_row - lse)
            return pt.astype(BF16), (pt * (dpt - delta)).astype(BF16)

        dp_sc[...] = _dot(v_ref[pl.ds(0, tk), :], dotv)
        ctx_in.wait()
        pb, dsb = grads(pc_sc[...], mrun_ref[pl.ds(n_run, 1), :], _dot(vc_ref[...], dotv))
        acc_sc[...] = _dot(kct_ref[...], dsb)
        dkct_ref[...] += _dot_nt(qtv, dsb)
        dvct_ref[...] += _dot_nt(dotv, pb)

        def loop(c, carry):
            slot = c % (P_AHEAD + 1)
            off = pl.multiple_of(c * tk, tk)
            nxt = pl.multiple_of(jnp.minimum(c + 1, n_chunks - 1) * tk, tk)
            p_in(slot, c).wait()
            ahead = c + P_AHEAD
            wrap = ahead >= n_chunks
            p_in(ahead % (P_AHEAD + 1), jnp.where(wrap, ahead - n_chunks, ahead), jnp.where(wrap, h_next, h),
                 jnp.where(wrap, i_next, i)).start()
            dpt = dp_sc[...]
            dp_next = _dot(v_ref[pl.ds(nxt, tk), :], dotv)
            pb, dsb = grads(p_sc[slot], mrun_ref[pl.ds(c // per_run, 1), :], dpt)
            acc_sc[...] += _dot(kt_ref[:, pl.ds(off, tk)], dsb)
            dkt_sc[:, pl.ds(off, tk)] += _dot_nt(qtv, dsb)
            dvt_sc[:, pl.ds(off, tk)] += _dot_nt(dotv, pb)
            dp_sc[...] = dp_next
            return carry

        lax.fori_loop(0, n_chunks, loop, 0)
        dqt_ref[...] = acc_sc[...]

        @pl.when(jnp.logical_and(h == nh - 1, i == n_q - 1))
        def _():
            for c in range(P_AHEAD):
                p_in(c, c).wait()
            pltpu.sync_copy(dkt_sc, dkt_ref)
            pltpu.sync_copy(dvt_sc, dvt_ref)

    qs = pl.BlockSpec((None, LANES, tq), lambda h, i: (h, 0, i))
    rs = pl.BlockSpec((None, 1, tq), lambda h, i: (h, 0, i))
    return _call(
        body, name="attn_glob_bwd", grid=(nh, n_q),
        in_specs=[qs, qs, qs, rs, pl.BlockSpec((None, n_run + 1, tq), lambda h, i: (h, 0, i)), ANY,
                  _full((LANES, s)), _full((s, LANES)), _full((LANES, nc)), _full((nc, LANES))],
        out_specs=[qs, ANY, ANY, _full((LANES, nc)), _full((LANES, nc))],
        out_shape=[_sds((nh, LANES, s)), _sds((LANES, s)), _sds((LANES, s)), _sds((LANES, nc)), _sds((LANES, nc))],
        scratch=[pltpu.VMEM((LANES, tq), F32), pltpu.VMEM((tk, tq), F32), pltpu.VMEM((LANES, s), F32),
                 pltpu.VMEM((LANES, s), F32), pltpu.VMEM((P_AHEAD + 1, tk, tq), BF16), pltpu.VMEM((nc, tq), BF16),
                 pltpu.SemaphoreType.DMA((P_AHEAD + 1,)), pltpu.SemaphoreType.DMA],
        sem=("arbitrary", "arbitrary"),
    )(qt, dot, ot, lse, mrun, p, kt, v, kct, vc)


WIN_SPAN = 2 * WINDOW


def _band(rows0, cols0, shape):
    r = rows0 + lax.broadcasted_iota(jnp.int32, shape, 0)
    c = cols0 + lax.broadcasted_iota(jnp.int32, shape, 1)
    return jnp.abs(r - c) <= WINDOW


def _win_start(blk, t, s):
    return pl.multiple_of(jnp.clip(blk * t - WINDOW, 0, s - t - WIN_SPAN), WINDOW)


def _attn_win_fwd(qt, k, vt, kc, vct, sink, *, tq):
    nh, _, s = qt.shape
    nc = kc.shape[0]
    tw = tq + WIN_SPAN
    half = LANES // 2
    grp = nh // N_KV

    def body(sink_ref, qt_ref, k_ref, vt_ref, kc_ref, vct_ref, ot_ref, lse_ref):
        h, i = pl.program_id(0), pl.program_id(1)
        k0 = _win_start(i, tq, s)
        qtv = qt_ref[...]
        st = jnp.where(_band(k0, i * tq, (tw, tq)), _dot(k_ref[pl.ds(k0, tw), :], qtv), NEG)
        stc = _dot(kc_ref[...], qtv)
        snk = sink_ref[h]
        m = jnp.maximum(jnp.maximum(jnp.max(st, axis=0, keepdims=True), jnp.max(stc, axis=0, keepdims=True)), snk)
        acc = (_dot(vt_ref[:, pl.ds(k0, tw)], jnp.exp(st - m).astype(BF16))
               + _dot(vct_ref[...], jnp.exp(stc - m).astype(BF16)))
        l = jnp.where(h < grp, acc[half:half + 1], acc[0:1]) + jnp.exp(snk - m)
        ot_ref[...] = (acc / l).astype(BF16)
        lse_ref[...] = m + jnp.log(l)

    return _call(
        body, name="attn_win_fwd", grid=(nh, s // tq),
        in_specs=[pl.BlockSpec(memory_space=pltpu.SMEM),
                  pl.BlockSpec((None, LANES, tq), lambda h, i: (h, 0, i)), _full((s, LANES)),
                  pl.BlockSpec((None, LANES, s), lambda h, i: (h // grp, 0, 0)), _full((nc, LANES)),
                  pl.BlockSpec((None, LANES, nc), lambda h, i: (h // grp, 0, 0))],
        out_specs=[pl.BlockSpec((None, LANES, tq), lambda h, i: (h, 0, i)),
                   pl.BlockSpec((None, 1, tq), lambda h, i: (h, 0, i))],
        out_shape=[_sds((nh, LANES, s), BF16), _sds((nh, 1, s))],
        sem=("parallel", "parallel"),
    )(sink, qt, k, vt, kc, vct)


def _attn_win_bwd(qt, dot, ot, lse, k, kt, v, kc, kct, vc, sink, *, tq):
    nh, _, s = qt.shape
    nc = kc.shape[0]
    tw = tq + WIN_SPAN
    nq = s // tq

    def body(sink_ref, qt_ref, dot_ref, ot_ref, lse_ref, k_ref, kt_ref, v_ref, kc_ref, kct_ref, vc_ref,
             dqt_ref, dkt_ref, dvt_ref, dkct_ref, dvct_ref, dsk_ref, dkt_sc, dvt_sc):
        h, i = pl.program_id(0), pl.program_id(1)

        @pl.when(jnp.logical_and(h == 0, i == 0))
        def _():
            dkct_ref[...] = jnp.zeros(dkct_ref.shape, F32)
            dvct_ref[...] = jnp.zeros(dvct_ref.shape, F32)
            dkt_sc[...] = jnp.zeros(dkt_sc.shape, F32)
            dvt_sc[...] = jnp.zeros(dvt_sc.shape, F32)

        k0 = _win_start(i, tq, s)
        span = pl.ds(k0, tw)
        qtv, dotv, lse = qt_ref[...], dot_ref[...], lse_ref[...]
        delta = jnp.sum(dotv.astype(F32) * ot_ref[...].astype(F32), axis=0, keepdims=True)
        pt = jnp.where(_band(k0, i * tq, (tw, tq)), jnp.exp(_dot(k_ref[span, :], qtv) - lse), 0.0)
        dsb = (pt * (_dot(v_ref[span, :], dotv) - delta)).astype(BF16)
        pct = jnp.exp(_dot(kc_ref[...], qtv) - lse)
        dscb = (pct * (_dot(vc_ref[...], dotv) - delta)).astype(BF16)
        dqt_ref[...] = _dot(kt_ref[:, span], dsb) + _dot(kct_ref[...], dscb)
        dkt_sc[:, span] += _dot_nt(qtv, dsb)
        dvt_sc[:, span] += _dot_nt(dotv, pt.astype(BF16))
        dkct_ref[...] += _dot_nt(qtv, dscb)
        dvct_ref[...] += _dot_nt(dotv, pct.astype(BF16))
        dsk = -jnp.sum(jnp.exp(sink_ref[h] - lse) * delta)
        dsk_ref[...] = jnp.full(dsk_ref.shape, dsk, F32)

        @pl.when(jnp.logical_and(h == nh - 1, i == nq - 1))
        def _():
            pltpu.sync_copy(dkt_sc, dkt_ref)
            pltpu.sync_copy(dvt_sc, dvt_ref)

    qs = pl.BlockSpec((None, LANES, tq), lambda h, i: (h, 0, i))
    rs = pl.BlockSpec((None, 1, tq), lambda h, i: (h, 0, i))
    return _call(
        body, name="attn_win_bwd", grid=(nh, nq),
        in_specs=[pl.BlockSpec(memory_space=pltpu.SMEM), qs, qs, qs, rs, _full((s, LANES)), _full((LANES, s)),
                  _full((s, LANES)), _full((nc, LANES)), _full((LANES, nc)), _full((nc, LANES))],
        out_specs=[qs, ANY, ANY, _full((LANES, nc)), _full((LANES, nc)),
                   pl.BlockSpec((None, None, 8, LANES), lambda h, i: (h, i, 0, 0))],
        out_shape=[_sds((nh, LANES, s)), _sds((LANES, s)), _sds((LANES, s)), _sds((LANES, nc)), _sds((LANES, nc)),
                   _sds((nh, nq, 8, LANES))],
        scratch=[pltpu.VMEM((LANES, s), F32), pltpu.VMEM((LANES, s), F32)],
        sem=("arbitrary", "arbitrary"),
    )(sink, qt, dot, ot, lse, k, kt, v, kc, kct, vc)


def _ln_fwd(z, g, b):
    mu = jnp.mean(z, axis=-1, keepdims=True)
    zc = z - mu
    r = lax.rsqrt(jnp.mean(zc * zc, axis=-1, keepdims=True) + LN_EPS)
    return zc * r * g + b, mu, r


def _ln_bwd(dy, xhat, r, g):
    dxh = dy * g
    return r * (dxh - jnp.mean(dxh, axis=-1, keepdims=True) - xhat * jnp.mean(dxh * xhat, axis=-1, keepdims=True))


def _heads_matmul(ot_ref, w_ref):
    acc = _dot_tn(ot_ref[0], w_ref[0])
    for h in range(1, N_HEADS):
        acc += _dot_tn(ot_ref[h], w_ref[h])
    return acc


def _gate_specs(tm):
    return [pl.BlockSpec((tm, 512), functools.partial(lambda i, b: (i, b), b=OFF_GA // 512 + b)) for b in range(4)]


def _merge_fwd(oat, obt, proj, x, gate1, wba, wbb, w_out, ln_g, ln_b, *, tm):
    s = x.shape[0]

    def body(oa_ref, ob_ref, g0, g1, g2, g3, x_ref, gt_ref, wba_ref, wbb_ref, wo_ref, lg_ref, lb_ref,
             x1_ref, y_ref, mu_ref, r_ref, pa_ref, pb_ref, mg_ref):
        ga = _sigmoid(jnp.concatenate([g0[...], g1[...]], axis=1))
        gb = _sigmoid(jnp.concatenate([g2[...], g3[...]], axis=1))
        pa, pb = _heads_matmul(oa_ref, wba_ref), _heads_matmul(ob_ref, wbb_ref)
        merged = (ga * pa + gb * pb).astype(BF16)
        y = _dot(merged, wo_ref[...])
        x1, mu, r = _ln_fwd(ALPHA * x_ref[...] + gt_ref[...] * y, lg_ref[...], lb_ref[...])
        x1_ref[...] = x1
        y_ref[...] = y
        mu_ref[...] = mu
        r_ref[...] = r
        pa_ref[...] = pa.astype(BF16)
        pb_ref[...] = pb.astype(BF16)
        mg_ref[...] = merged

    hts = pl.BlockSpec((N_HEADS, LANES, tm), lambda i: (0, 0, i))
    row = pl.BlockSpec((tm, D_MODEL), lambda i: (i, 0))
    col = pl.BlockSpec((tm, 1), lambda i: (i, 0))
    vec = _full((1, D_MODEL))
    wh = _full((N_HEADS, LANES, D_MODEL))
    return _call(
        body, name="merge_fwd", grid=(s // tm,),
        in_specs=[hts, hts, *_gate_specs(tm), row, vec, wh, wh, _full((D_MODEL, D_MODEL)), vec, vec],
        out_specs=[row, row, col, col, row, row, row],
        out_shape=[_sds((s, D_MODEL)), _sds((s, D_MODEL)), _sds((s, 1)), _sds((s, 1)), _sds((s, D_MODEL), BF16),
                   _sds((s, D_MODEL), BF16), _sds((s, D_MODEL), BF16)],
        sem=("parallel",),
    )(oat, obt, proj, proj, proj, proj, x, gate1, wba, wbb, w_out, ln_g, ln_b)


def _merge_bwd(dy, oat, obt, pa, pb, proj, wba, wbb, w_out, *, tm):
    s = dy.shape[0]

    def body(dy_ref, oat_ref, obt_ref, pa_ref, pb_ref, g0, g1, g2, g3, wba_ref, wbb_ref, wo_ref,
             dgl_ref, doat_ref, dobt_ref, dwa_ref, dwb_ref):
        @pl.when(pl.program_id(0) == 0)
        def _():
            dwa_ref[...] = jnp.zeros(dwa_ref.shape, F32)
            dwb_ref[...] = jnp.zeros(dwb_ref.shape, F32)

        dm = _dot_nt(dy_ref[...], wo_ref[...])
        ga = _sigmoid(jnp.concatenate([g0[...], g1[...]], axis=1))
        gb = _sigmoid(jnp.concatenate([g2[...], g3[...]], axis=1))
        pa, pb = pa_ref[...].astype(F32), pb_ref[...].astype(F32)
        dgl_ref[:, :D_MODEL] = (dm * pa * ga * (1.0 - ga)).astype(BF16)
        dgl_ref[:, D_MODEL:] = (dm * pb * gb * (1.0 - gb)).astype(BF16)
        dpa, dpb = (dm * ga).astype(BF16), (dm * gb).astype(BF16)
        for h in range(N_HEADS):
            doat_ref[h] = _dot_nt(wba_ref[h], dpa).astype(BF16)
            dobt_ref[h] = _dot_nt(wbb_ref[h], dpb).astype(BF16)
            dwa_ref[h] += _dot(oat_ref[h], dpa)
            dwb_ref[h] += _dot(obt_ref[h], dpb)

    hts = pl.BlockSpec((N_HEADS, LANES, tm), lambda i: (0, 0, i))
    row = pl.BlockSpec((tm, D_MODEL), lambda i: (i, 0))
    wh = _full((N_HEADS, LANES, D_MODEL))
    return _call(
        body, name="merge_bwd", grid=(s // tm,),
        in_specs=[row, hts, hts, row, row, *_gate_specs(tm), wh, wh, _full((D_MODEL, D_MODEL))],
        out_specs=[pl.BlockSpec((tm, 2 * D_MODEL), lambda i: (i, 0)), hts, hts, wh, wh],
        out_shape=[_sds((s, 2 * D_MODEL), BF16), _sds((N_HEADS, LANES, s), BF16), _sds((N_HEADS, LANES, s), BF16),
                   _sds((N_HEADS, LANES, D_MODEL)), _sds((N_HEADS, LANES, D_MODEL))],
        sem=("arbitrary",),
    )(dy, oat, obt, pa, pb, proj, proj, proj, proj, wba, wbb, w_out)


FF_TC = 256


def _shift_rows(t, prev_row, next_row):
    n = t.shape[0]
    r = lax.broadcasted_iota(jnp.int32, t.shape, 0)
    up = jnp.where(r == 0, prev_row, pltpu.roll(t, 1, 0))
    dn = jnp.where(r == n - 1, next_row, pltpu.roll(t, n - 1, 0))
    return up, dn


HALO = 16


def _halo_specs(tm, s, tc):
    nb = s // HALO
    main = pl.BlockSpec((2, tm, tc), lambda j, i: (0, i, j))
    prev = pl.BlockSpec((2, HALO, tc), lambda j, i: (0, jnp.maximum(i * (tm // HALO) - 1, 0), j))
    nxt = pl.BlockSpec((2, HALO, tc), lambda j, i: (0, jnp.minimum((i + 1) * (tm // HALO), nb - 1), j))
    return main, prev, nxt


def _halo_rows(prev_ref, next_ref, half, i, n_i):
    prev_row = jnp.where(i == 0, 0.0, prev_ref[half, HALO - 1:HALO, :].astype(F32))
    next_row = jnp.where(i == n_i - 1, 0.0, next_ref[half, 0:1, :].astype(F32))
    return prev_row, next_row


def _conv(t, prev_row, next_row, w, b):
    up, dn = _shift_rows(t, prev_row, next_row)
    return w[0:1, :] * up + w[1:2, :] * t + w[2:3, :] * dn + b


def _ffn_act_fwd(u, cw, cb, *, tm):
    _, s, ff = u.shape
    n_i = s // tm

    def body(u_ref, up_ref, un_ref, cw_ref, cb_ref, a_ref):
        i = pl.program_id(1)
        gc = _conv(u_ref[0].astype(F32), *_halo_rows(up_ref, un_ref, 0, i, n_i), cw_ref[0], cb_ref[0])
        vc = _conv(u_ref[1].astype(F32), *_halo_rows(up_ref, un_ref, 1, i, n_i), cw_ref[1], cb_ref[1])
        a_ref[...] = (gc * _sigmoid(gc) * vc).astype(BF16)

    main, prev, nxt = _halo_specs(tm, s, FF_TC)
    return _call(
        body, name="ffn_act_fwd", grid=(ff // FF_TC, n_i),
        in_specs=[main, prev, nxt, pl.BlockSpec((2, 3, FF_TC), lambda j, i: (0, 0, j)),
                  pl.BlockSpec((2, 1, FF_TC), lambda j, i: (0, 0, j))],
        out_specs=pl.BlockSpec((tm, FF_TC), lambda j, i: (i, j)),
        out_shape=_sds((s, ff), BF16), sem=("parallel", "parallel"),
    )(u, u, u, cw, cb)


def _ffn_act_bwd(dy2, w_down, u, cw, cb, *, tm):
    _, s, ff = u.shape
    n_i = s // tm

    def body(dy_ref, wd_ref, u_ref, up_ref, un_ref, cw_ref, cb_ref, dc_ref, dcw_ref, dcb_ref):
        i = pl.program_id(1)

        @pl.when(i == 0)
        def _():
            dcw_ref[...] = jnp.zeros(dcw_ref.shape, F32)
            dcb_ref[...] = jnp.zeros(dcb_ref.shape, F32)

        da = _dot_nt(dy_ref[...], wd_ref[...])
        ug, uv = u_ref[0].astype(F32), u_ref[1].astype(F32)
        ugp, ugn = _shift_rows(ug, *_halo_rows(up_ref, un_ref, 0, i, n_i))
        uvp, uvn = _shift_rows(uv, *_halo_rows(up_ref, un_ref, 1, i, n_i))
        wg, wv = cw_ref[0], cw_ref[1]
        gc = wg[0:1, :] * ugp + wg[1:2, :] * ug + wg[2:3, :] * ugn + cb_ref[0]
        vc = wv[0:1, :] * uvp + wv[1:2, :] * uv + wv[2:3, :] * uvn + cb_ref[1]
        sg = _sigmoid(gc)
        dg = da * vc * sg * (1.0 + gc * (1.0 - sg))
        dv = da * gc * sg
        dc_ref[0] = dg.astype(BF16)
        dc_ref[1] = dv.astype(BF16)
        for half, (d, taps) in enumerate(((dg, (ugp, ug, ugn)), (dv, (uvp, uv, uvn)))):
            for tap in range(3):
                dcw_ref[half, tap:tap + 1, :] += jnp.sum(d * taps[tap], axis=0, keepdims=True)
            dcb_ref[half] += jnp.sum(d, axis=0, keepdims=True)

    main, prev, nxt = _halo_specs(tm, s, FF_TC)
    return _call(
        body, name="ffn_act_bwd", grid=(ff // FF_TC, n_i),
        in_specs=[pl.BlockSpec((tm, D_MODEL), lambda j, i: (i, 0)), pl.BlockSpec((FF_TC, D_MODEL), lambda j, i: (j, 0)),
                  main, prev, nxt, pl.BlockSpec((2, 3, FF_TC), lambda j, i: (0, 0, j)),
                  pl.BlockSpec((2, 1, FF_TC), lambda j, i: (0, 0, j))],
        out_specs=[main, pl.BlockSpec((2, 3, FF_TC), lambda j, i: (0, 0, j)),
                   pl.BlockSpec((2, 1, FF_TC), lambda j, i: (0, 0, j))],
        out_shape=[_sds((2, s, ff), BF16), _sds((2, 3, ff)), _sds((2, 1, ff))],
        sem=("parallel", "arbitrary"),
    )(dy2, w_down, u, u, u, cw, cb)


def _conv_bwd_input(dc, cw, *, tm):
    _, s, ff = dc.shape
    n_i = s // tm

    def body(d_ref, dp_ref, dn_ref, cw_ref, du_ref):
        i = pl.program_id(1)
        for half in range(2):
            d = d_ref[half].astype(F32)
            up, dn = _shift_rows(d, *_halo_rows(dp_ref, dn_ref, half, i, n_i))
            w = cw_ref[half]
            du_ref[half] = (w[0:1, :] * dn + w[1:2, :] * d + w[2:3, :] * up).astype(BF16)

    main, prev, nxt = _halo_specs(tm, s, FF_TC)
    return _call(
        body, name="conv_bwd_input", grid=(ff // FF_TC, n_i),
        in_specs=[main, prev, nxt, pl.BlockSpec((2, 3, FF_TC), lambda j, i: (0, 0, j))],
        out_specs=main, out_shape=_sds((2, s, ff), BF16), sem=("parallel", "parallel"),
    )(dc, dc, dc, cw)


def _ffn_down_loss(a, w_down, x1, target, gate2, ln_g, ln_b, *, tm):
    s, ff = a.shape
    n_i = s // tm

    def body(a_ref, wd_ref, x1_ref, tg_ref, gt_ref, lg_ref, lb_ref, ls_ref, dy_ref, dx_ref, dg_ref, db_ref, dgt_ref):
        @pl.when(pl.program_id(0) == 0)
        def _():
            dg_ref[...] = jnp.zeros(dg_ref.shape, F32)
            db_ref[...] = jnp.zeros(db_ref.shape, F32)
            dgt_ref[...] = jnp.zeros(dgt_ref.shape, F32)

        y2 = _dot(a_ref[...], wd_ref[...])
        z = ALPHA * x1_ref[...] + gt_ref[...] * y2
        mu = jnp.mean(z, axis=-1, keepdims=True)
        zc = z - mu
        r = lax.rsqrt(jnp.mean(zc * zc, axis=-1, keepdims=True) + LN_EPS)
        xhat = zc * r
        diff = xhat * lg_ref[...] + lb_ref[...] - tg_ref[...]
        ls_ref[...] = jnp.full(ls_ref.shape, 0.5 / D_MODEL * jnp.sum(diff * diff), F32)
        dx2 = diff * (1.0 / D_MODEL)
        dg_ref[...] += jnp.sum(dx2 * xhat, axis=0, keepdims=True)
        db_ref[...] += jnp.sum(dx2, axis=0, keepdims=True)
        dz = _ln_bwd(dx2, xhat, r, lg_ref[...])
        dgt_ref[...] += jnp.sum(dz * y2, axis=0, keepdims=True)
        dy_ref[...] = (gt_ref[...] * dz).astype(BF16)
        dx_ref[...] = ALPHA * dz

    row = pl.BlockSpec((tm, D_MODEL), lambda i: (i, 0))
    vec = _full((1, D_MODEL))
    return _call(
        body, name="ffn_down_loss", grid=(n_i,),
        in_specs=[pl.BlockSpec((tm, ff), lambda i: (i, 0)), _full((ff, D_MODEL)), row, row, vec, vec, vec],
        out_specs=[pl.BlockSpec((None, 8, LANES), lambda i: (i, 0, 0)), row, row, vec, vec, vec],
        out_shape=[_sds((n_i, 8, LANES)), _sds((s, D_MODEL), BF16), _sds((s, D_MODEL)),
                   _sds((1, D_MODEL)), _sds((1, D_MODEL)), _sds((1, D_MODEL))],
        sem=("arbitrary",),
    )(a, w_down, x1, target, gate2, ln_g, ln_b)


def _ffn_up_bwd(du, wup4, dx1a, x1, scale2, x, y, mu1, r1, gate1, ln_g, *, tm):
    s = x.shape[0]
    nb, _, ns = wup4.shape

    def body(du_ref, w_ref, dxa_ref, x1_ref, sc_ref, x_ref, y_ref, mu_ref, r_ref, gt_ref, lg_ref,
             dxo_ref, dy_ref, dsc_ref, dsh_ref, dg_ref, db_ref, dgt_ref, acc):
        i, k = pl.program_id(0), pl.program_id(1)

        @pl.when(jnp.logical_and(i == 0, k == 0))
        def _():
            for ref in (dsc_ref, dsh_ref, dg_ref, db_ref, dgt_ref):
                ref[...] = jnp.zeros(ref.shape, F32)

        @pl.when(k == 0)
        def _():
            acc[...] = jnp.zeros(acc.shape, F32)

        acc[...] += _dot_nt(du_ref[...], w_ref[...])

        @pl.when(k == nb - 1)
        def _():
            dh = acc[...]
            x1 = x1_ref[...]
            dsc_ref[...] += jnp.sum(dh * x1, axis=0, keepdims=True)
            dsh_ref[...] += jnp.sum(dh, axis=0, keepdims=True)
            dx1 = dxa_ref[...] + dh * (1.0 + sc_ref[...])
            yv = y_ref[...]
            xhat = (ALPHA * x_ref[...] + gt_ref[...] * yv - mu_ref[...]) * r_ref[...]
            dg_ref[...] += jnp.sum(dx1 * xhat, axis=0, keepdims=True)
            db_ref[...] += jnp.sum(dx1, axis=0, keepdims=True)
            dz = _ln_bwd(dx1, xhat, r_ref[...], lg_ref[...])
            dgt_ref[...] += jnp.sum(dz * yv, axis=0, keepdims=True)
            dy_ref[...] = (gt_ref[...] * dz).astype(BF16)
            dxo_ref[...] = ALPHA * dz

    row = pl.BlockSpec((tm, D_MODEL), lambda i, k: (i, 0))
    col = pl.BlockSpec((tm, 1), lambda i, k: (i, 0))
    vec = _full((1, D_MODEL))
    return _call(
        body, name="ffn_up_bwd", grid=(s // tm, nb),
        in_specs=[pl.BlockSpec((None, tm, ns), lambda i, k: (k // 2, i, k % 2)),
                  pl.BlockSpec((None, D_MODEL, ns), lambda i, k: (k, 0, 0)),
                  row, row, vec, row, row, col, col, vec, vec],
        out_specs=[row, row, vec, vec, vec, vec, vec],
        out_shape=[_sds((s, D_MODEL)), _sds((s, D_MODEL), BF16)] + [_sds((1, D_MODEL))] * 5,
        scratch=[pltpu.VMEM((tm, D_MODEL), F32)],
        sem=("arbitrary", "arbitrary"),
    )(du, wup4, dx1a, x1, scale2, x, y, mu1, r1, gate1, ln_g)


def _mm_nt4_mod_bwd(dp, w4, dxa, x, scale, *, tm, name):
    m = x.shape[0]
    nb, kdim, ns = w4.shape

    def body(dp_ref, w_ref, dxa_ref, x_ref, sc_ref, dx_ref, dsc_ref, dsh_ref, acc):
        i, k = pl.program_id(0), pl.program_id(1)

        @pl.when(jnp.logical_and(i == 0, k == 0))
        def _():
            dsc_ref[...] = jnp.zeros(dsc_ref.shape, F32)
            dsh_ref[...] = jnp.zeros(dsh_ref.shape, F32)

        @pl.when(k == 0)
        def _():
            acc[...] = jnp.zeros(acc.shape, F32)

        acc[...] += _dot_nt(dp_ref[...], w_ref[...])

        @pl.when(k == nb - 1)
        def _():
            dh = acc[...]
            dsc_ref[...] += jnp.sum(dh * x_ref[...], axis=0, keepdims=True)
            dsh_ref[...] += jnp.sum(dh, axis=0, keepdims=True)
            dx_ref[...] = dxa_ref[...] + dh * (1.0 + sc_ref[...])

    row = pl.BlockSpec((tm, kdim), lambda i, k: (i, 0))
    vec = _full((1, kdim))
    return _call(
        body, name=name, grid=(m // tm, nb),
        in_specs=[pl.BlockSpec((tm, ns), lambda i, k: (i, k)), pl.BlockSpec((None, kdim, ns), lambda i, k: (k, 0, 0)),
                  row, row, vec],
        out_specs=[row, vec, vec],
        out_shape=[_sds((m, kdim)), _sds((1, kdim)), _sds((1, kdim))],
        scratch=[pltpu.VMEM((tm, kdim), F32)],
        sem=("arbitrary", "arbitrary"),
    )(dp, w4, dxa, x, scale)


def _pad_heads_w(w):
    w8 = w.reshape(N_HEADS, HEAD_DIM, w.shape[-1])
    z = jnp.zeros_like(w8)
    first = (jnp.arange(N_HEADS) < N_HEADS // N_KV)[:, None, None]
    return jnp.where(first, jnp.concatenate([w8, z], axis=1), jnp.concatenate([z, w8], axis=1))


def _unpad_heads_w(g):
    first = (jnp.arange(N_HEADS) < N_HEADS // N_KV)[:, None, None]
    return jnp.where(first, g[:, :HEAD_DIM], g[:, HEAD_DIM:]).reshape(N_HEADS * HEAD_DIM, g.shape[-1])


def _ones_beside(vt):
    half = vt.shape[0] // 2
    ones = jnp.ones((half, vt.shape[1]), vt.dtype)
    return jnp.stack([jnp.concatenate([vt[:half], ones], axis=0), jnp.concatenate([ones, vt[half:]], axis=0)])


def _rep8(a):
    return jnp.broadcast_to(a.reshape(1, -1), (8, a.size))


def _first_row(a):
    r8 = _rep8(a)
    return jnp.where(lax.broadcasted_iota(jnp.int32, r8.shape, 0) == 0, r8, 0.0)


def _to_blocks4(w):
    k, n = w.shape
    return w.reshape(k, N_CHIPS, n // N_CHIPS).transpose(1, 0, 2)


def _local_step(x, c, ctx, c_ctx, wmod4, b_mod, win4, b_in, sink, qn, kn, wba, wbb, w_out, ln1_g, ln1_b,
                wup4, cw, cb, w_down, ln2_g, ln2_b, target):
    s, nc = x.shape[0], ctx.shape[0]
    tm = min(512, s)
    tm2 = min(256, s)
    tl = min(1024, s)
    tx = min(2048, s)
    zvec = jnp.zeros((1, D_MODEL), F32)

    cc = jnp.concatenate([_rep8(c), _rep8(c_ctx)], axis=0)
    mods = _mm_nn4(cc, zvec, zvec, wmod4, b_mod, mode="silu", split_out=False, out_dtype=F32, tm=16, name="mod_vectors")
    shift1, scale1, gate1, shift2, scale2, gate2 = [mods[0:1, i * D_MODEL:(i + 1) * D_MODEL] for i in range(6)]
    shift_c, scale_c = mods[8:9, :D_MODEL], mods[8:9, D_MODEL:2 * D_MODEL]

    cos, sin = _rope_tables(s)
    cos_c, sin_c = jnp.ones((nc, LANES), F32), jnp.zeros((nc, LANES), F32)
    qg, kg = jnp.tile(qn, (1, 2)), jnp.tile(kn, (1, 2))

    proj_c = _mm_nn4(ctx, shift_c, scale_c, win4, b_in, mode="modulate", split_out=False, out_dtype=F32, tm=nc,
                     name="in_proj_ctx")
    _, kac, vac, _, kbc, vbc = _prep(proj_c, cos_c, sin_c, qg, kg, tm=nc, name="prep_ctx")
    proj = _mm_nn4(x, shift1, scale1, win4, b_in, mode="modulate", split_out=False, out_dtype=F32, tm=tx, name="in_proj")
    qat, ka, va, qbt, kb, vb = _prep(proj, cos, sin, qg, kg, tm=tl, name="prep")
    oat, lse_a = _attn_win_fwd(qat, ka, _ones_beside(va.T), kac, _ones_beside(vac.T), sink, tq=tm)
    obt, lse_b, mrun_b, pbt = _attn_glob_fwd(qbt, kb, _ones_beside(vb.T), kbc, _ones_beside(vbc.T), tq=tm,
                                             tk=min(1024, s))
    wba_p, wbb_p = _pad_heads_w(wba), _pad_heads_w(wbb)
    x1, y, mu1, r1, pa, pb, merged = _merge_fwd(oat, obt, proj, x, gate1, wba_p, wbb_p, w_out, ln1_g, ln1_b, tm=tm)
    u = _mm_nn4(x1, shift2, scale2, wup4, jnp.zeros((1, 2 * D_FF), F32), mode="modulate", split_out=True,
                out_dtype=BF16, tm=tx, name="ffn_up")
    cw2 = cw.reshape(3, 2, D_FF).transpose(1, 0, 2)
    cb2 = cb.reshape(2, 1, D_FF)
    a = _ffn_act_fwd(u, cw2, cb2, tm=tx)
    ls, dy2, dx1a, dln2_g, dln2_b, dgate2 = _ffn_down_loss(a, w_down, x1, target, gate2, ln2_g, ln2_b, tm=tm)
    loss = jnp.sum(ls[:, 0, 0])

    n_s = s // tl
    dw_down = _mm_tn(a, dy2, a_spec=pl.BlockSpec((tl, D_FF), lambda t: (t, 0)),
                     b_spec=pl.BlockSpec((tl, D_MODEL), lambda t: (t, 0)), grid=(n_s,),
                     out_shape=_sds((D_FF, D_MODEL)), out_spec=_full((D_FF, D_MODEL)), name="dw_down")
    dc, dcw2, dcb2 = _ffn_act_bwd(dy2, w_down, u, cw2, cb2, tm=tx)
    du = _conv_bwd_input(dc, cw2, tm=tx)
    dxz1, dy, dscale2, dshift2, dln1_g, dln1_b, dgate1 = _ffn_up_bwd(
        du, wup4, dx1a, x1, scale2, x, y, mu1, r1, gate1, ln1_g, tm=tm)
    ns_up = wup4.shape[-1]
    dw_up4 = _mm_tn(x1, du, a_spec=pl.BlockSpec((tx, D_MODEL), lambda k, t: (t, 0)),
                    b_spec=pl.BlockSpec((None, tx, ns_up), lambda k, t: (k // 2, t, k % 2)), grid=(N_CHIPS, s // tx),
                    out_shape=_sds((N_CHIPS, D_MODEL, ns_up)),
                    out_spec=pl.BlockSpec((None, D_MODEL, ns_up), lambda k, t: (k, 0, 0)),
                    mod=(shift2, scale2), name="dw_up")

    dgl, doat, dobt, dwba_p, dwbb_p = _merge_bwd(dy, oat, obt, pa, pb, proj, wba_p, wbb_p, w_out, tm=tm2)
    dwba, dwbb = _unpad_heads_w(dwba_p), _unpad_heads_w(dwbb_p)
    rowspec = pl.BlockSpec((tl, D_MODEL), lambda t: (t, 0))
    dw_out = _mm_tn(merged, dy, a_spec=rowspec, b_spec=rowspec, grid=(n_s,), out_shape=_sds((D_MODEL, D_MODEL)),
                    out_spec=_full((D_MODEL, D_MODEL)), name="dw_out")

    dqat, dkat, dvat, dkact, dvact, dsk = _attn_win_bwd(qat, doat, oat, lse_a, ka, ka.T, va, kac, kac.T, vac, sink, tq=tm)
    dka, dva, dkac, dvac = dkat.T, dvat.T, dkact.T, dvact.T
    dqbt, dkbt, dvbt, dkbct, dvbct = _attn_glob_bwd(qbt, dobt, obt, lse_b, mrun_b, pbt, kb.T, vb, kbc.T, vbc, tq=tm, tk=tm)
    dkb, dvb, dkbc, dvbc = dkbt.T, dvbt.T, dkbct.T, dvbct.T
    dsink = jnp.sum(dsk[:, :, 0, 0], axis=1)

    dproj, dqg, dkg = _prep_bwd(dqat, dka, dva, dqbt, dkb, dvb, proj, cos, sin, qg, kg, dgl, tm=tm, name="prep_bwd")
    grad_x, dscale1, dshift1 = _mm_nt4_mod_bwd(dproj, win4, dxz1, x, scale1, tm=tl, name="in_proj_bwd")
    ns_in = win4.shape[-1]
    win_spec = dict(b_spec=pl.BlockSpec((None, None, ns_in), lambda k, t: (0, 0, k)),
                    out_shape=_sds((N_CHIPS, D_MODEL, ns_in)),
                    out_spec=pl.BlockSpec((None, D_MODEL, ns_in), lambda k, t: (k, 0, 0)),
                    colsum_spec=pl.BlockSpec((8, ns_in), lambda k, t: (0, k)), colsum_shape=_sds((8, IN_COLS)))
    win_spec["b_spec"] = pl.BlockSpec((tx, ns_in), lambda k, t: (t, k))
    dw_in4, db_in = _mm_tn(x, dproj, a_spec=pl.BlockSpec((tx, D_MODEL), lambda k, t: (t, 0)), grid=(N_CHIPS, s // tx),
                           mod=(shift1, scale1), name="dw_in", **win_spec)

    zq = jnp.zeros((N_HEADS, LANES, nc), F32)
    dproj_c, _, dkg_c = _prep_bwd(zq, dkac, dvac, zq, dkbc, dvbc, proj_c, cos_c, sin_c, qg, kg,
                                  jnp.zeros((nc, IN_COLS - OFF_GA), BF16), tm=nc, name="prep_bwd_ctx")
    _, dscale_c, dshift_c = _mm_nt4_mod_bwd(dproj_c, win4, jnp.zeros((nc, D_MODEL), F32), ctx, scale_c, tm=nc,
                                            name="in_proj_bwd_ctx")
    win_spec["b_spec"] = pl.BlockSpec((nc, ns_in), lambda k, t: (t, k))
    dw_in4, db_in_c = _mm_tn(ctx, dproj_c, a_spec=pl.BlockSpec((nc, D_MODEL), lambda k, t: (t, 0)), grid=(N_CHIPS, 1),
                             mod=(shift_c, scale_c), init=dw_in4, name="dw_in_ctx", **win_spec)

    dmod = jnp.concatenate([dshift1, dscale1, dgate1, dshift2, dscale2, dgate2], axis=1)
    dmodc = jnp.concatenate([dshift_c, dscale_c], axis=1)
    dmodc_pad = jnp.concatenate([dmodc, jnp.zeros((1, 4 * D_MODEL), F32)], axis=1)
    dmodc8 = _first_row(dmodc_pad).astype(BF16)
    z8 = jnp.zeros((8, D_MODEL), F32)
    dsilu_c, _, _ = _mm_nt4_mod_bwd(dmodc8, wmod4, z8, z8, zvec, tm=8, name="c_ctx_bwd")
    sg = _sigmoid(c_ctx)
    dc_ctx = dsilu_c[0:1] * sg * (1.0 + c_ctx * (1.0 - sg))

    dqn = jnp.sum(dqg.reshape(N_HEADS, HEAD_DIM), axis=0, keepdims=True)
    dkn = jnp.sum((dkg + dkg_c).reshape(N_KV, HEAD_DIM), axis=0, keepdims=True)
    grads = dict(
        w_in4=dw_in4, b_in=db_in[0:1] + db_in_c[0:1], sink=dsink, qn=dqn, kn=dkn, wba=dwba, wbb=dwbb, w_out=dw_out,
        ln1_g=dln1_g, ln1_b=dln1_b, w_up4=dw_up4, conv_w=dcw2.transpose(1, 0, 2).reshape(3, 2 * D_FF),
        conv_b=dcb2.reshape(1, 2 * D_FF), w_down=dw_down, ln2_g=dln2_g, ln2_b=dln2_b,
        c_ctx=dc_ctx, dmod=dmod, dmodc=dmodc)
    return loss, grad_x, grads


ANY = pl.BlockSpec(memory_space=pl.ANY)


def _mesh_pos():
    return lax.axis_index("x"), lax.axis_index("y"), lax.axis_index("c")


def _other_chips(x, y):
    return [(1 - x, y), (x, 1 - y), (1 - x, 1 - y)]


def _remote(src, dst, send, recv, dev):
    return pltpu.make_async_remote_copy(src_ref=src, dst_ref=dst, send_sem=send, recv_sem=recv, device_id=dev,
                                        device_id_type=MESH)


def _set_block(stack, block, k):
    return lax.dynamic_update_slice(stack, block[None], (k,) + (0,) * block.ndim)


def _gather_shards(arrs, small):
    na = len(arrs)
    halves = [a.shape[0] // 2 for a in arrs]

    def body(*refs):
        ins, small_ref = refs[:na], refs[na]
        outs, small_out = refs[na + 1:2 * na + 1], refs[2 * na + 1]
        send, recv = refs[2 * na + 2:]
        x, y, c = _mesh_pos()
        me = 2 * x + y
        chips = _other_chips(x, y)

        def half(a, cc):
            return pl.ds(cc * halves[a], halves[a])

        sends = []
        for j, chip in enumerate(chips):
            for a in range(na):
                sends.append(_remote(ins[a].at[half(a, c)], outs[a].at[me, half(a, c)], send.at[a, j], recv.at[a, j],
                                     (*chip, c)))
            sends.append(_remote(small_ref, small_out.at[me], send.at[na, j], recv.at[na, j], (*chip, c)))
        for cp in sends:
            cp.start()
        for j, chip in enumerate(chips):
            kj = 2 * chip[0] + chip[1]
            for a in range(na):
                landed = outs[a].at[kj, half(a, c)]
                _remote(landed, landed, send.at[a, j], recv.at[a, j], (*chip, c)).wait_recv()
                fwd = _remote(landed, landed, send.at[a, 3 + j], recv.at[a, 3 + j], (x, y, 1 - c))
                fwd.start()
                sends.append(fwd)
            _remote(small_ref, small_out.at[kj], send.at[na, j], recv.at[na, j], (*chip, c)).wait_recv()
        for j, chip in enumerate(chips):
            kj = 2 * chip[0] + chip[1]
            for a in range(na):
                other = outs[a].at[kj, half(a, 1 - c)]
                _remote(other, other, send.at[a, 3 + j], recv.at[a, 3 + j], (x, y, 1 - c)).wait_recv()
        for cp in sends:
            cp.wait_send()

    out_shape = [_sds((N_CHIPS,) + a.shape, a.dtype) for a in arrs] + [_sds((N_CHIPS,) + small.shape, small.dtype)]
    got = pl.pallas_call(
        body, name="gather_shards", in_specs=[ANY] * (na + 1), out_specs=[ANY] * (na + 1), out_shape=out_shape,
        scratch_shapes=[pltpu.SemaphoreType.DMA((na + 1, 6)), pltpu.SemaphoreType.DMA((na + 1, 6))],
    )(*arrs, small)
    xp, yp, _ = _mesh_pos()
    return [_set_block(g, a, 2 * xp + yp) for g, a in zip(got, list(arrs) + [small])]


def _allgather_rows(v):
    r, n = v.shape

    def body(v_ref, out_ref, send, recv, loc):
        x, y, c = _mesh_pos()
        me, sibling = (x, y, c), (x, y, 1 - c)
        chips = _other_chips(x, y)

        def rows(px, py, pc):
            return out_ref.at[4 * px + 2 * py + pc]

        def copy(k, block, to, src=None):
            return _remote(rows(*block) if src is None else src, rows(*block), send.at[k], recv.at[k], to)

        mine = pltpu.make_async_copy(v_ref, rows(*me), loc)
        mine.start()
        first = [copy(0, me, sibling, src=v_ref)] + [copy(1 + j, me, (*chip, c), src=v_ref) for j, chip in enumerate(chips)]
        for cp in first:
            cp.start()
        passed = [copy(4 + j, (*chip, c), sibling) for j, chip in enumerate(chips)]
        for j, chip in enumerate(chips):
            copy(1 + j, (*chip, c), me).wait_recv()
            passed[j].start()
        copy(0, sibling, me).wait_recv()
        for j, chip in enumerate(chips):
            copy(4 + j, (*chip, 1 - c), me).wait_recv()
        for cp in first + passed:
            cp.wait_send()
        mine.wait()

    return pl.pallas_call(
        body, name="allgather_rows", in_specs=[pl.BlockSpec(memory_space=pltpu.VMEM)],
        out_specs=pl.BlockSpec(memory_space=pltpu.VMEM), out_shape=_sds((N_DEV, r, n), v.dtype),
        scratch_shapes=[pltpu.SemaphoreType.DMA((7,)), pltpu.SemaphoreType.DMA((7,)), pltpu.SemaphoreType.DMA],
    )(v)


def _swap_other_half(g):
    nb, r, n = g.shape
    rh = r // 2

    def body(g_ref, out_ref, send, recv):
        x, y, c = _mesh_pos()
        cp = _remote(g_ref.at[:, pl.ds((1 - c) * rh, rh), :], out_ref, send, recv, (x, y, 1 - c))
        cp.start()
        cp.wait()

    return pl.pallas_call(
        body, name="swap_other_half", in_specs=[ANY], out_specs=ANY, out_shape=_sds((nb, rh, n), g.dtype),
        scratch_shapes=[pltpu.SemaphoreType.DMA, pltpu.SemaphoreType.DMA],
    )(g)


def _scatter_to_chips(p):
    def body(p_ref, out_ref, send, recv):
        x, y, c = _mesh_pos()
        me = 2 * x + y
        chips = _other_chips(x, y)
        sends = [_remote(p_ref.at[2 * chip[0] + chip[1]], out_ref.at[me], send.at[j], recv.at[j], (*chip, c))
                 for j, chip in enumerate(chips)]
        for cp in sends:
            cp.start()
        for j, chip in enumerate(chips):
            kj = 2 * chip[0] + chip[1]
            _remote(p_ref.at[kj], out_ref.at[kj], send.at[j], recv.at[j], (*chip, c)).wait_recv()
        for cp in sends:
            cp.wait_send()

    got = pl.pallas_call(
        body, name="scatter_to_chips", in_specs=[ANY], out_specs=ANY, out_shape=_sds(p.shape, p.dtype),
        scratch_shapes=[pltpu.SemaphoreType.DMA((3,)), pltpu.SemaphoreType.DMA((3,))],
    )(p)
    xp, yp, _ = _mesh_pos()
    me = 2 * xp + yp
    return _set_block(got, lax.dynamic_index_in_dim(p, me, axis=0, keepdims=False), me)


def _join_halves(f):
    def body(f_ref, out_ref, send, recv):
        x, y, c = _mesh_pos()
        cp = _remote(f_ref, out_ref, send, recv, (x, y, 1 - c))
        cp.start()
        cp.wait()

    other = pl.pallas_call(
        body, name="join_halves", in_specs=[ANY], out_specs=ANY, out_shape=_sds(f.shape, f.dtype),
        scratch_shapes=[pltpu.SemaphoreType.DMA, pltpu.SemaphoreType.DMA],
    )(f)
    first = lax.axis_index("c") == 0
    return jnp.concatenate([jnp.where(first, f, other), jnp.where(first, other, f)], axis=0)


def _row_tile(rows, cap=512):
    t = cap - cap % 8
    while rows % t:
        t -= 8
    return t


def _add_blocks(a, b, out_dtype):
    nb, r, n = a.shape
    tr = _row_tile(r)

    def body(a_ref, b_ref, o_ref):
        o_ref[...] = (a_ref[...] + b_ref[...]).astype(out_dtype)

    spec = pl.BlockSpec((None, tr, n), lambda k, i: (k, i, 0))
    return _call(body, name="add_blocks", grid=(nb, r // tr), in_specs=[spec, spec], out_specs=spec,
                 out_shape=_sds(a.shape, out_dtype), sem=("parallel", "parallel"))(a, b)


def _sum_leading(a, *, name):
    nk, r, n = a.shape
    tr = _row_tile(r)

    def body(a_ref, o_ref):
        acc = a_ref[0].astype(F32)
        for k in range(1, nk):
            acc = acc + a_ref[k].astype(F32)
        o_ref[...] = acc

    return _call(body, name=name, grid=(r // tr,), in_specs=[pl.BlockSpec((nk, tr, n), lambda i: (0, i, 0))],
                 out_specs=pl.BlockSpec((tr, n), lambda i: (i, 0)), out_shape=_sds((r, n)), sem=("parallel",))(a)


def _silu_outer(a, b):
    kdim, n = a.shape[1], b.shape[1]

    def body(a_ref, b_ref, o_ref):
        av = a_ref[...]
        av = av * _sigmoid(av)
        bv = b_ref[...]
        ah, bh = av.astype(BF16), bv.astype(BF16)
        al, bl = (av - ah.astype(F32)).astype(BF16), (bv - bh.astype(F32)).astype(BF16)
        o_ref[...] = _dot_tn(ah, bh) + (_dot_tn(ah, bl) + _dot_tn(al, bh))

    return _call(body, name="dw_mod", grid=(1,), in_specs=[_full(a.shape), _full(b.shape)], out_specs=_full((kdim, n)),
                 out_shape=_sds((kdim, n)))(a, b)


def _adamw(w, g, m, v):
    r, n = w.shape
    tr = _row_tile(r)

    def body(w_ref, g_ref, m_ref, v_ref, d_ref, nm_ref, nv_ref):
        gv = g_ref[...]
        nm = ADAM_B1 * m_ref[...] + (1.0 - ADAM_B1) * gv
        nv = ADAM_B2 * v_ref[...] + (1.0 - ADAM_B2) * (gv * gv)
        m_hat = nm / (1.0 - ADAM_B1 ** ADAM_STEP)
        v_hat = nv / (1.0 - ADAM_B2 ** ADAM_STEP)
        d_ref[...] = -ADAM_LR * (m_hat / (jnp.sqrt(v_hat) + ADAM_EPS) + ADAM_WD * w_ref[...])
        nm_ref[...] = nm
        nv_ref[...] = nv

    spec = pl.BlockSpec((tr, n), lambda i: (i, 0))
    return _call(body, name="adamw", grid=(r // tr,), in_specs=[spec] * 4, out_specs=[spec] * 3,
                 out_shape=[_sds((r, n))] * 3, sem=("parallel",))(w, g, m, v)


BIG = ("w_in", "w_branch_a", "w_branch_b", "w_out", "w_up", "w_down", "conv_w")
BIG_ROWS = 3584
MATRICES = ("w_mod", "w_in", "w_branch_a", "w_branch_b", "w_out", "w_up", "w_down")
SMALL = ("b_mod", "b_in", "conv_b", "ln1_g", "ln1_b", "ln2_g", "ln2_b", "c_ctx", "attn_sink", "q_norm_g", "k_norm_g", "conv_w")
SMALL_ROWS = 8 * len(SMALL)


def _rows(a, n_rows):
    flat = a.reshape(-1)
    return jnp.pad(flat, (0, n_rows * D_MODEL - flat.shape[0])).reshape(n_rows, D_MODEL)


def _group8(a):
    return _rep8(_rows(a, 1)) if a.size <= D_MODEL else _rows(a, 8)


def _ungroup8(p, shape):
    size = math.prod(shape)
    return (p[0, :size] if size <= D_MODEL else p.reshape(-1)[:size]).reshape(shape)


def _unpack_big(p, like):
    out, r = {}, 0
    for n in BIG:
        size = math.prod(like[n].shape)
        nr = size // D_MODEL if n != "conv_w" else 8
        out[n] = p[r:r + nr].reshape(-1)[:size].reshape(like[n].shape)
        r += nr
    return out


def _pack_small(t):
    return jnp.concatenate([_group8(t[n]) for n in SMALL], axis=0)


def _unpack_small(p, like):
    return {n: _ungroup8(p[8 * i:8 * i + 8], like[n].shape) for i, n in enumerate(SMALL)}


WEIGHTS = ("c_ctx", "w_mod", "b_mod", "w_in", "b_in", "attn_sink", "q_norm_g", "k_norm_g", "w_branch_a", "w_branch_b",
           "w_out", "ln1_g", "ln1_b", "w_up", "conv_w", "conv_b", "w_down", "ln2_g", "ln2_b")


def kernel(x, c, ctx, c_ctx, w_mod, b_mod, w_in, b_in, attn_sink, q_norm_g, k_norm_g, w_branch_a, w_branch_b, w_out, ln1_g, ln1_b, w_up, conv_w, conv_b, w_down, ln2_g, ln2_b, loss_target, m_c_ctx, m_w_mod, m_b_mod, m_w_in, m_b_in, m_attn_sink, m_q_norm_g, m_k_norm_g, m_w_branch_a, m_w_branch_b, m_w_out, m_ln1_g, m_ln1_b, m_w_up, m_conv_w, m_conv_b, m_w_down, m_ln2_g, m_ln2_b, v_c_ctx, v_w_mod, v_b_mod, v_w_in, v_b_in, v_attn_sink, v_q_norm_g, v_k_norm_g, v_w_branch_a, v_w_branch_b, v_w_out, v_ln1_g, v_ln1_b, v_w_up, v_conv_w, v_conv_b, v_w_down, v_ln2_g, v_ln2_b):
    w = dict(c_ctx=c_ctx, w_mod=w_mod, b_mod=b_mod, w_in=w_in, b_in=b_in, attn_sink=attn_sink, q_norm_g=q_norm_g,
             k_norm_g=k_norm_g, w_branch_a=w_branch_a, w_branch_b=w_branch_b, w_out=w_out, ln1_g=ln1_g, ln1_b=ln1_b,
             w_up=w_up, conv_w=conv_w, conv_b=conv_b, w_down=w_down, ln2_g=ln2_g, ln2_b=ln2_b)
    m = dict(c_ctx=m_c_ctx, w_mod=m_w_mod, b_mod=m_b_mod, w_in=m_w_in, b_in=m_b_in, attn_sink=m_attn_sink,
             q_norm_g=m_q_norm_g, k_norm_g=m_k_norm_g, w_branch_a=m_w_branch_a, w_branch_b=m_w_branch_b, w_out=m_w_out,
             ln1_g=m_ln1_g, ln1_b=m_ln1_b, w_up=m_w_up, conv_w=m_conv_w, conv_b=m_conv_b, w_down=m_w_down,
             ln2_g=m_ln2_g, ln2_b=m_ln2_b)
    v = dict(c_ctx=v_c_ctx, w_mod=v_w_mod, b_mod=v_b_mod, w_in=v_w_in, b_in=v_b_in, attn_sink=v_attn_sink,
             q_norm_g=v_q_norm_g, k_norm_g=v_k_norm_g, w_branch_a=v_w_branch_a, w_branch_b=v_w_branch_b, w_out=v_w_out,
             ln1_g=v_ln1_g, ln1_b=v_ln1_b, w_up=v_w_up, conv_w=v_conv_w, conv_b=v_conv_b, w_down=v_w_down,
             ln2_g=v_ln2_g, ln2_b=v_ln2_b)
    xp, yp, _ = _mesh_pos()
    me = 2 * xp + yp

    branches = jnp.concatenate([w_branch_a[0], w_branch_b[0]], axis=0)
    wide = jnp.concatenate([w_mod[0], w_in[0], w_up[0], branches], axis=1).astype(BF16)
    tall = jnp.concatenate([w_out[0], w_down[0]], axis=0).astype(BF16)
    wide4, tall4, cw4 = _gather_shards([wide, tall], conv_w[0])
    n_mod, n_in, n_up = w_mod.shape[-1], w_in.shape[-1], w_up.shape[-1]
    wmod4 = wide4[:, :, :n_mod]
    win4 = wide4[:, :, n_mod:n_mod + n_in]
    wup4 = wide4[:, :, n_mod + n_in:n_mod + n_in + n_up]
    br4 = wide4[:, :, n_mod + n_in + n_up:]
    n_br = w_branch_a.shape[1]
    wba = br4[:, :n_br].transpose(1, 0, 2).reshape(n_br, D_MODEL)
    wbb = br4[:, n_br:].transpose(1, 0, 2).reshape(n_br, D_MODEL)
    n_out = w_out.shape[1]
    w_out_full = tall4[:, :n_out].reshape(D_MODEL, D_MODEL)
    w_down_full = tall4[:, n_out:].reshape(D_FF, D_MODEL)
    cw_full = cw4.transpose(1, 0, 2).reshape(3, 2 * D_FF)

    loss, grad_x, g = _local_step(
        x[0], c, ctx[0], c_ctx[None], wmod4, b_mod, win4, b_in, attn_sink[0], q_norm_g, k_norm_g, wba, wbb, w_out_full,
        ln1_g, ln1_b, wup4, cw_full, conv_b, w_down_full, ln2_g, ln2_b, loss_target[0])
    loss = lax.psum(loss, ("x", "y", "c"))

    sent = dict(c=c, dmod=g["dmod"], dmodc=g["dmodc"], b_in=g["b_in"], conv_b=g["conv_b"], ln1_g=g["ln1_g"],
                ln1_b=g["ln1_b"], ln2_g=g["ln2_g"], ln2_b=g["ln2_b"], c_ctx=g["c_ctx"], attn_sink=g["sink"],
                q_norm_g=g["qn"], k_norm_g=g["kn"])
    every = _allgather_rows(jnp.concatenate([_group8(a) for a in sent.values()], axis=0))
    total = _sum_leading(every, name="sum_devices")
    slot = {n: slice(8 * i, 8 * i + 8) for i, n in enumerate(sent)}
    gs = {n: _ungroup8(total[slot[n]], sent[n].shape) for n in SMALL if n in sent}
    dmodc_sum = jnp.concatenate([_ungroup8(total[slot["dmodc"]], (1, 2 * D_MODEL)), jnp.zeros((1, 4 * D_MODEL), F32)],
                                axis=1)
    gs["b_mod"] = _ungroup8(total[slot["dmod"]], b_mod.shape) + dmodc_sum
    acts = jnp.concatenate([every[:, slot["c"].start], _rep8(c_ctx)], axis=0)
    dmods = jnp.concatenate([every[:, slot["dmod"]].reshape(N_DEV, -1)[:, :6 * D_MODEL], _first_row(dmodc_sum)], axis=0)
    g_w_mod = _silu_outer(acts, lax.dynamic_slice_in_dim(dmods, me * n_mod, n_mod, axis=1))

    cw_g4 = _to_blocks4(g["conv_w"])
    parts = [
        g["w_in4"].reshape(N_CHIPS, -1, D_MODEL), _to_blocks4(g["wba"]).reshape(N_CHIPS, -1, D_MODEL),
        _to_blocks4(g["wbb"]).reshape(N_CHIPS, -1, D_MODEL), g["w_out"].reshape(N_CHIPS, -1, D_MODEL),
        g["w_up4"].reshape(N_CHIPS, -1, D_MODEL), g["w_down"].reshape(N_CHIPS, -1, D_MODEL),
        jnp.pad(cw_g4.reshape(N_CHIPS, -1), ((0, 0), (0, 8 * D_MODEL - cw_g4.shape[1] * cw_g4.shape[2]))).reshape(
            N_CHIPS, 8, D_MODEL)]
    used = sum(p.shape[1] for p in parts)
    packed = jnp.concatenate(parts + [jnp.zeros((N_CHIPS, BIG_ROWS - used, D_MODEL), F32)], axis=1)
    rh = BIG_ROWS // 2
    cpos = lax.axis_index("c")
    my_half = lax.dynamic_slice_in_dim(packed, cpos * rh, rh, axis=1)
    chip_sum = _add_blocks(my_half, _swap_other_half(packed), BF16)
    half_sum = _sum_leading(_scatter_to_chips(chip_sum), name="sum_chips")
    g_big = _unpack_big(_join_halves(half_sum), w)

    grads = dict(gs, w_mod=g_w_mod, **g_big)
    grads = {n: grads[n].reshape(w[n].shape) for n in WEIGHTS}
    delta, new_m, new_v = {}, {}, {}
    for n in MATRICES:
        outs = _adamw(*[t[n][0] for t in (w, grads, m, v)])
        delta[n], new_m[n], new_v[n] = [o[None] for o in outs]
    outs = _adamw(*[_pack_small(t) for t in (w, grads, m, v)])
    for res, o in zip((delta, new_m, new_v), outs):
        res.update(_unpack_small(o, w))
    return (loss, grad_x[None], *[grads[n] for n in WEIGHTS], *[delta[n] for n in WEIGHTS],
            *[new_m[n] for n in WEIGHTS], *[new_v[n] for n in WEIGHTS])
```
